```python
import math
import numpy as np
import jax
import jax.numpy as jnp
from jax import lax

D_MODEL = 1024
BATCH = 4
SEQ = 4096
DEPTH = 4

HEAD_DIM = 64
ROPE_THETA = 10000.0
NORM_EPS = 1e-6
D_FF = 4 * D_MODEL
Q_BLOCK = 128

A_HEADS = D_MODEL // (2 * HEAD_DIM)
IDX_HEADS = 4
DSA_TOPK = 256

B_HEADS = D_MODEL // (2 * HEAD_DIM)
B_KV_GROUPS = 2
CMP_BLOCK = 32
CMP_STRIDE = 16
CMP_HIDDEN = 256
SLC_BLOCK = 64
SLC_TOPN = 16
WINDOW = 512

C_HEADS = D_MODEL // (2 * HEAD_DIM)

EVEN_SPLITS = (A_HEADS * HEAD_DIM, HEAD_DIM, HEAD_DIM, IDX_HEADS * HEAD_DIM, HEAD_DIM, IDX_HEADS, B_HEADS * HEAD_DIM, 6 * B_KV_GROUPS * HEAD_DIM, 3 * B_HEADS)
EVEN_IN = sum(EVEN_SPLITS)
EVEN_MIX = (A_HEADS + B_HEADS) * HEAD_DIM
ODD_MIX = C_HEADS * 2 * HEAD_DIM
ODD_IN = 3 * ODD_MIX

kernel_name = 'hybrid_dsa_nsa_diffattn_trunk'


def rms_norm(x, g):
    xf = x.astype(jnp.float32)
    y = xf * lax.rsqrt(jnp.mean(xf * xf, axis=-1, keepdims=True) + NORM_EPS)
    return (y * g.astype(jnp.float32)).astype(x.dtype)


def rope_tables(T):
    inv = 1.0 / (ROPE_THETA ** (jnp.arange(0, HEAD_DIM, 2, dtype=jnp.float32) / HEAD_DIM))
    ang = jnp.arange(T, dtype=jnp.float32)[:, None] * inv[None, :]
    return jnp.cos(ang), jnp.sin(ang)


def apply_rope(x, cos, sin):
    x1, x2 = jnp.split(x, 2, axis=-1)
    shape = (1, cos.shape[0]) + (1,) * (x.ndim - 3) + (cos.shape[1],)
    c = cos.reshape(shape).astype(x.dtype)
    s = sin.reshape(shape).astype(x.dtype)
    return jnp.concatenate([x1 * c - x2 * s, x2 * c + x1 * s], axis=-1)


def masked_softmax(s, mask):
    s = jnp.where(mask, s.astype(jnp.float32), -jnp.inf)
    m = jnp.max(s, axis=-1, keepdims=True)
    m = jnp.where(jnp.isfinite(m), m, 0.0)
    e = jnp.exp(s - m)
    return e / jnp.maximum(jnp.sum(e, axis=-1, keepdims=True), 1e-30)


def split_cols(a, sizes):
    offs = np.cumsum(sizes)[:-1].tolist()
    return jnp.split(a, offs, axis=-1)


def sweep_query_blocks(fn, T):
    nb = T // Q_BLOCK
    out = lax.map(fn, jnp.arange(nb))
    out = jnp.moveaxis(out, 0, 1)
    return out.reshape((out.shape[0], T) + out.shape[3:])


def dsa_attention(q, k, v, q_idx, k_idx, w_idx):
    T = q.shape[1]
    topk = min(DSA_TOPK, T // 4)
    scale = HEAD_DIM ** -0.5
    key_pos = jnp.arange(T)

    def block(i):
        t0 = i * Q_BLOCK
        tq = t0 + jnp.arange(Q_BLOCK)
        qb = lax.dynamic_slice_in_dim(q, t0, Q_BLOCK, 1)
        qib = lax.dynamic_slice_in_dim(q_idx, t0, Q_BLOCK, 1)
        wb = lax.dynamic_slice_in_dim(w_idx, t0, Q_BLOCK, 1)
        rel = jax.nn.relu(jnp.einsum('bqhd,bsd->bqhs', qib, k_idx).astype(jnp.float32))
        score = jnp.einsum('bqh,bqhs->bqs', wb.astype(jnp.float32), rel)
        score = jnp.where(key_pos[None, :] <= tq[:, None], score, -jnp.inf)
        _, idx = lax.top_k(score, topk)
        valid = idx <= tq[None, :, None]
        kg = jax.vmap(lambda kk, ii: kk[ii])(k, idx)
        vg = jax.vmap(lambda vv, ii: vv[ii])(v, idx)
        s = jnp.einsum('bqhd,bqkd->bhqk', qb, kg) * scale
        p = masked_softmax(s, valid[:, None])
        return jnp.einsum('bhqk,bqkd->bqhd', p.astype(vg.dtype), vg)

    return sweep_query_blocks(block, T)


def compress_blocks(x, pe, w1, w2):
    bsz, T, G, D = x.shape
    n_ch = T // CMP_STRIDE
    r = CMP_BLOCK // CMP_STRIDE
    n_c = n_ch - r + 1
    ch = x.reshape(bsz, n_ch, CMP_STRIDE, G, D)
    blk = jnp.concatenate([ch[:, j:j + n_c] for j in range(r)], axis=2)
    blk = blk + pe[None, None, :, None, :].astype(x.dtype)
    h = jax.nn.gelu(jnp.einsum('bnlgd,ldf->bngf', blk, w1.reshape(CMP_BLOCK, D, CMP_HIDDEN)))
    return jnp.einsum('bngf,fd->bngd', h, w2)


def slc_overlap(n_c, n_s):
    c0 = np.arange(n_c) * CMP_STRIDE
    c1 = c0 + CMP_BLOCK
    s0 = np.arange(n_s) * SLC_BLOCK
    s1 = s0 + SLC_BLOCK
    return ((c0[:, None] < s1[None, :]) & (c1[:, None] > s0[None, :])).astype(np.float32)


def nsa_attention(q, k_c, v_c, k_s, v_s, k_w, v_w, gate_logits, cmp_pe, cmp_w1, cmp_w2):
    bsz, T = q.shape[0], q.shape[1]
    G, J, D = B_KV_GROUPS, B_HEADS // B_KV_GROUPS, HEAD_DIM
    scale = D ** -0.5
    qg = q.reshape(bsz, T, G, J, D)
    g = jax.nn.sigmoid(gate_logits.astype(jnp.float32)).astype(q.dtype).reshape(bsz, T, G, J, 3)
    t_all = jnp.arange(T)

    kc = compress_blocks(k_c, cmp_pe[0], cmp_w1[0], cmp_w2[0])
    vc = compress_blocks(v_c, cmp_pe[1], cmp_w1[1], cmp_w2[1])
    n_c = kc.shape[1]
    cmp_end = jnp.arange(n_c) * CMP_STRIDE + CMP_BLOCK - 1
    s_c = jnp.einsum('btgjd,bngd->bgjtn', qg, kc) * scale
    p_c = masked_softmax(s_c, cmp_end[None, :] <= t_all[:, None])
    o_c = jnp.einsum('bgjtn,bngd->btgjd', p_c.astype(vc.dtype), vc)

    n_s = T // SLC_BLOCK
    n_sel = min(SLC_TOPN, n_s)
    imp = jnp.einsum('bgjtn,nm->bgtm', p_c, jnp.asarray(slc_overlap(n_c, n_s)))
    ks_blk = k_s.reshape(bsz, n_s, SLC_BLOCK, G, D).transpose(0, 3, 1, 2, 4)
    vs_blk = v_s.reshape(bsz, n_s, SLC_BLOCK, G, D).transpose(0, 3, 1, 2, 4)
    kw_pad = jnp.pad(k_w, ((0, 0), (WINDOW, 0), (0, 0), (0, 0)))
    vw_pad = jnp.pad(v_w, ((0, 0), (WINDOW, 0), (0, 0), (0, 0)))
    b_ix = jnp.arange(bsz)[:, None, None, None]
    g_ix = jnp.arange(G)[None, :, None, None]
    blk_ids = jnp.arange(n_s)

    def block(i):
        t0 = i * Q_BLOCK
        tq = t0 + jnp.arange(Q_BLOCK)
        qb = lax.dynamic_slice_in_dim(qg, t0, Q_BLOCK, 1)
        gq = lax.dynamic_slice_in_dim(g, t0, Q_BLOCK, 1)
        sc = lax.dynamic_slice_in_dim(imp, t0, Q_BLOCK, 2)
        cur = tq // SLC_BLOCK
        forced = (blk_ids[None, :] == 0) | (blk_ids[None, :] == cur[:, None]) | (blk_ids[None, :] == cur[:, None] - 1)
        admissible = blk_ids[None, :] * SLC_BLOCK <= tq[:, None]
        sc = jnp.where(forced, jnp.inf, sc)
        sc = jnp.where(admissible, sc, -jnp.inf)
        _, idx = lax.top_k(sc, n_sel)
        kg = ks_blk[b_ix, g_ix, idx]
        vg = vs_blk[b_ix, g_ix, idx]
        pos = idx[..., None] * SLC_BLOCK + jnp.arange(SLC_BLOCK)
        valid = (pos <= tq[None, None, :, None, None]).reshape(bsz, G, Q_BLOCK, n_sel * SLC_BLOCK)[:, :, None]
        s_s = jnp.einsum('bqgjd,bgqnsd->bgjqns', qb, kg).reshape(bsz, G, J, Q_BLOCK, n_sel * SLC_BLOCK) * scale
        p_s = masked_softmax(s_s, valid)
        o_s = jnp.einsum('bgjqk,bgqkd->bqgjd', p_s.astype(vg.dtype), vg.reshape(bsz, G, Q_BLOCK, n_sel * SLC_BLOCK, D))
        kw = lax.dynamic_slice_in_dim(kw_pad, t0, WINDOW + Q_BLOCK, 1)
        vw = lax.dynamic_slice_in_dim(vw_pad, t0, WINDOW + Q_BLOCK, 1)
        pos_w = t0 - WINDOW + jnp.arange(WINDOW + Q_BLOCK)
        valid_w = (pos_w[None, :] >= 0) & (pos_w[None, :] <= tq[:, None]) & (pos_w[None, :] > tq[:, None] - WINDOW)
        s_w = jnp.einsum('bqgjd,bsgd->bgjqs', qb, kw) * scale
        p_w = masked_softmax(s_w, valid_w)
        o_w = jnp.einsum('bgjqs,bsgd->bqgjd', p_w.astype(vw.dtype), vw)
        return gq[..., 1:2] * o_s + gq[..., 2:3] * o_w

    o_sw = sweep_query_blocks(block, T)
    o = g[..., 0:1] * o_c + o_sw
    return o.reshape(bsz, T, B_HEADS, D)


def even_mixer(h, w_in, cmp_pe, cmp_w1, cmp_w2, w_out, cos, sin):
    bsz, T, _ = h.shape
    proj = jnp.einsum('btd,de->bte', h, w_in)
    qa, ka, va, qi, ki, wi, qb, kvb, gb = split_cols(proj, EVEN_SPLITS)
    qa = apply_rope(qa.reshape(bsz, T, A_HEADS, HEAD_DIM), cos, sin)
    ka = apply_rope(ka, cos, sin)
    qi = apply_rope(qi.reshape(bsz, T, IDX_HEADS, HEAD_DIM), cos, sin)
    ki = apply_rope(ki, cos, sin)
    o_a = dsa_attention(qa, ka, va, qi, ki, wi)
    qb = apply_rope(qb.reshape(bsz, T, B_HEADS, HEAD_DIM), cos, sin)
    kvb = kvb.reshape(bsz, T, 6, B_KV_GROUPS, HEAD_DIM)
    k_c = apply_rope(kvb[:, :, 0], cos, sin)
    v_c = kvb[:, :, 1]
    k_s = apply_rope(kvb[:, :, 2], cos, sin)
    v_s = kvb[:, :, 3]
    k_w = apply_rope(kvb[:, :, 4], cos, sin)
    v_w = kvb[:, :, 5]
    o_b = nsa_attention(qb, k_c, v_c, k_s, v_s, k_w, v_w, gb, cmp_pe, cmp_w1, cmp_w2)
    o = jnp.concatenate([o_a.reshape(bsz, T, -1), o_b.reshape(bsz, T, -1)], axis=-1)
    return jnp.einsum('bte,ed->btd', o, w_out)


def diff_mixer(h, w_in, lam, subln_g, w_out, cos, sin, lambda_init):
    bsz, T, _ = h.shape
    proj = jnp.einsum('btd,de->bte', h, w_in)
    q, k, v = jnp.split(proj, 3, axis=-1)
    q = apply_rope(q.reshape(bsz, T, C_HEADS * 2, HEAD_DIM), cos, sin).reshape(bsz, T, C_HEADS, 2, HEAD_DIM)
    k = apply_rope(k.reshape(bsz, T, C_HEADS * 2, HEAD_DIM), cos, sin).reshape(bsz, T, C_HEADS, 2, HEAD_DIM)
    v = v.reshape(bsz, T, C_HEADS, 2 * HEAD_DIM)
    lf = lam.astype(jnp.float32)
    lam_val = jnp.exp(jnp.sum(lf[0] * lf[1])) - jnp.exp(jnp.sum(lf[2] * lf[3])) + lambda_init
    scale = HEAD_DIM ** -0.5
    key_pos = jnp.arange(T)

    def block(i):
        t0 = i * Q_BLOCK
        tq = t0 + jnp.arange(Q_BLOCK)
        qb = lax.dynamic_slice_in_dim(q, t0, Q_BLOCK, 1)
        s = jnp.einsum('bqhcd,bshcd->bhcqs', qb, k) * scale
        p = masked_softmax(s, key_pos[None, :] <= tq[:, None])
        a = p[:, :, 0] - lam_val * p[:, :, 1]
        return jnp.einsum('bhqs,bshe->bqhe', a.astype(v.dtype), v)

    o = sweep_query_blocks(block, T)
    o = rms_norm(o, subln_g) * (1.0 - lambda_init)
    return jnp.einsum('bte,ed->btd', o.reshape(bsz, T, ODD_MIX), w_out)


def sq_relu_mlp(h, w_up, w_down):
    u = jnp.einsum('btd,df->btf', h, w_up)
    return jnp.einsum('btf,fd->btd', jnp.square(jax.nn.relu(u)), w_down)


def setup_inputs(seed: int = 0) -> dict:
    key = jax.random.key(seed)
    ks = jax.random.split(key, 16)
    ne = (DEPTH + 1) // 2
    no = DEPTH // 2

    def nrm(k, shape, fan_in):
        return jax.random.normal(k, shape, jnp.float32) * (fan_in ** -0.5)

    def gain(k, shape):
        return 1.0 + 0.05 * jax.random.normal(k, shape, jnp.float32)

    return {
        'x': jax.random.normal(ks[0], (BATCH, SEQ, D_MODEL), jnp.float32),
        'mix_norm_g': gain(ks[1], (DEPTH, D_MODEL)),
        'mlp_norm_g': gain(ks[2], (DEPTH, D_MODEL)),
        'even_w_in': nrm(ks[3], (ne, D_MODEL, EVEN_IN), D_MODEL),
        'even_cmp_pe': 0.02 * jax.random.normal(ks[4], (ne, 2, CMP_BLOCK, HEAD_DIM), jnp.float32),
        'even_cmp_w1': nrm(ks[5], (ne, 2, CMP_BLOCK * HEAD_DIM, CMP_HIDDEN), CMP_BLOCK * HEAD_DIM),
        'even_cmp_w2': nrm(ks[6], (ne, 2, CMP_HIDDEN, HEAD_DIM), CMP_HIDDEN),
        'even_w_out': nrm(ks[7], (ne, EVEN_MIX, D_MODEL), EVEN_MIX),
        'odd_w_in': nrm(ks[8], (no, D_MODEL, ODD_IN), D_MODEL),
        'odd_lambda': 0.1 * jax.random.normal(ks[9], (no, 4, HEAD_DIM), jnp.float32),
        'odd_subln_g': gain(ks[10], (no, 2 * HEAD_DIM)),
        'odd_w_out': nrm(ks[11], (no, ODD_MIX, D_MODEL), ODD_MIX),
        'mlp_w_up': nrm(ks[12], (DEPTH, D_MODEL, D_FF), D_MODEL),
        'mlp_w_down': nrm(ks[13], (DEPTH, D_FF, D_MODEL), D_FF),
        'final_norm_g': gain(ks[14], (D_MODEL,)),
    }


def reference(x, mix_norm_g, mlp_norm_g, even_w_in, even_cmp_pe, even_cmp_w1, even_cmp_w2, even_w_out, odd_w_in, odd_lambda, odd_subln_g, odd_w_out, mlp_w_up, mlp_w_down, final_norm_g):
    T = x.shape[1]
    cos, sin = rope_tables(T)
    h = x
    for layer in range(DEPTH):
        u = rms_norm(h, mix_norm_g[layer])
        if layer % 2 == 0:
            e = layer // 2
            h = h + even_mixer(u, even_w_in[e], even_cmp_pe[e], even_cmp_w1[e], even_cmp_w2[e], even_w_out[e], cos, sin)
        else:
            o = layer // 2
            lambda_init = 0.8 - 0.6 * math.exp(-0.3 * layer)
            h = h + diff_mixer(u, odd_w_in[o], odd_lambda[o], odd_subln_g[o], odd_w_out[o], cos, sin, lambda_init)
        h = h + sq_relu_mlp(rms_norm(h, mlp_norm_g[layer]), mlp_w_up[layer], mlp_w_down[layer])
    return rms_norm(h, final_norm_g)
```

```python
import functools
import math

import numpy as np
import jax
import jax.numpy as jnp
from jax import lax
from jax.experimental import pallas as pl
from jax.experimental.pallas import tpu as pltpu

HEAD_DIM = 64
HALF = HEAD_DIM // 2
LANES = 128
ROPE_THETA = 10000.0
NORM_EPS = 1e-6
SCALE = HEAD_DIM ** -0.5

A_HEADS = 8
IDX_HEADS = 4
DSA_TOPK = 256
B_HEADS = 8
B_KV_GROUPS = 2
B_PER_GROUP = B_HEADS // B_KV_GROUPS
CMP_BLOCK = 32
CMP_STRIDE = 16
CMP_HIDDEN = 256
SLC_BLOCK = 64
SLC_TOPN = 16
WINDOW = 512
C_HEADS = 8

NEG_INF = float("-inf")
M_FLOOR = -1e30
INT_MIN = -(2 ** 31)

VMEM_LIMIT = 56 * 1024 * 1024

BF16 = jnp.bfloat16
F32 = jnp.float32

_NT = (((1,), (1,)), ((), ()))


def _dot_nt(a, b):
    return lax.dot_general(a, b, _NT, preferred_element_type=F32)


def _dot(a, b):
    return jnp.dot(a, b, preferred_element_type=F32)


def _params(sem):
    return pltpu.CompilerParams(dimension_semantics=sem, vmem_limit_bytes=VMEM_LIMIT)


def _pair_cols(base_a, base_b):
    lane = np.arange(LANES)
    half = lane // 64
    which = (lane % 64) // HALF
    i = lane % HALF
    base = np.where(which == 0, base_a, base_b)
    return base + half * HALF + i


def _dup_cols(base):
    lane = np.arange(LANES)
    return base + lane % HEAD_DIM


_PAIR_D = _pair_cols(0, 0)
_PAIR_WHICH = (np.arange(LANES) % 64) // HALF


def _gather_cols(w, cols):
    cols = np.asarray(cols)
    safe = np.where(cols >= 0, cols, 0)
    g = jnp.take(w, jnp.asarray(safe, dtype=jnp.int32), axis=1)
    return jnp.where(jnp.asarray(cols >= 0)[None, :], g, 0.0)


def _rope_slabs(T):
    inv = 1.0 / (ROPE_THETA ** (jnp.arange(0, HEAD_DIM, 2, dtype=F32) / HEAD_DIM))
    ang = jnp.arange(T, dtype=F32)[:, None] * inv[None, :]
    cos, sin = jnp.cos(ang), jnp.sin(ang)
    cos_slab = jnp.tile(cos, (1, 4))
    sin_slab = jnp.concatenate([-sin, -sin, sin, sin], axis=1)
    return cos_slab, sin_slab


def _proj_kernel(x_ref, g_ref, w_ref, cos_ref, sin_ref, ob_ref, *rest, plan, chunk):
    of_ref = rest[0] if rest else None
    x = x_ref[0]
    ms = jnp.mean(x * x, axis=-1, keepdims=True)
    xn = ((x * lax.rsqrt(ms + NORM_EPS)) * g_ref[...]).astype(BF16)
    cos = cos_ref[...]
    sin = sin_ref[...]
    n = len(plan)
    for c0 in range(0, n, chunk):
        c1 = min(c0 + chunk, n)
        r = _dot(xn, w_ref[:, c0 * LANES:c1 * LANES])
        for s in range(c0, c1):
            roped, dest, idx = plan[s]
            y = r[:, (s - c0) * LANES:(s - c0 + 1) * LANES]
            if roped:
                y = y * cos + pltpu.roll(y, 64, 1) * sin
            if dest == "b":
                ob_ref[0, idx] = y.astype(BF16)
            else:
                of_ref[0, idx] = y


def _project(x, g, w, cos_slab, sin_slab, plan, tm=512, chunk=4):
    B, T, D = x.shape
    nb = sum(1 for p in plan if p[1] == "b")
    nf = len(plan) - nb
    tm = min(tm, T)
    kern = functools.partial(_proj_kernel, plan=tuple(plan), chunk=chunk)
    out_specs = [pl.BlockSpec((1, nb, tm, LANES), lambda b, i: (b, 0, i, 0))]
    out_shape = [jax.ShapeDtypeStruct((B, nb, T, LANES), BF16)]
    if nf:
        out_specs.append(pl.BlockSpec((1, nf, tm, LANES), lambda b, i: (b, 0, i, 0)))
        out_shape.append(jax.ShapeDtypeStruct((B, nf, T, LANES), F32))
    return pl.pallas_call(
        kern,
        grid=(B, T // tm),
        in_specs=[
            pl.BlockSpec((1, tm, D), lambda b, i: (b, i, 0)),
            pl.BlockSpec((1, D), lambda b, i: (0, 0)),
            pl.BlockSpec((D, len(plan) * LANES), lambda b, i: (0, 0)),
            pl.BlockSpec((tm, LANES), lambda b, i: (i, 0)),
            pl.BlockSpec((tm, LANES), lambda b, i: (i, 0)),
        ],
        out_specs=out_specs,
        out_shape=out_shape,
        compiler_params=_params(("parallel", "parallel")),
        name="norm_proj_rope",
    )(x, g.reshape(1, D), w, cos_slab, sin_slab)


def _lane_iota(shape):
    return lax.broadcasted_iota(jnp.int32, shape, len(shape) - 1)


def _head_select(slab, which, scale=None):
    lane = _lane_iota(slab.shape)
    keep = ((lane % 64) // HALF) == which
    if scale is not None:
        slab = slab * scale
    return jnp.where(keep, slab, jnp.zeros_like(slab))


def _online_update(s, v, m_ref, l_ref, acc_ref):
    m_prev = m_ref[...]
    m_new = jnp.maximum(m_prev, jnp.max(s, axis=-1, keepdims=True))
    alpha = jnp.exp(m_prev - m_new)
    p = jnp.exp(s - m_new)
    l_ref[...] = alpha * l_ref[...] + jnp.sum(p, axis=-1, keepdims=True)
    acc_ref[...] = alpha * acc_ref[...] + _dot(p.astype(BF16), v)
    m_ref[...] = m_new


def _softmax_once(s):
    m = jnp.max(s, axis=-1, keepdims=True)
    m = jnp.where(m > NEG_INF, m, 0.0)
    e = jnp.exp(s - m)
    return e / jnp.maximum(jnp.sum(e, axis=-1, keepdims=True), 1e-30)


def _merge_pair(even, odd):
    lane = _lane_iota(even.shape)
    return jnp.where(lane < 64, even, odd)


def _diff_attn_kernel(lam_ref, q_ref, k_ref, v_ref, g_ref, o_ref, m_sc, l_sc, acc_sc,
                      *, tq, lambda_init):
    qi = pl.program_id(2)
    q = q_ref[0, 0]
    qs = [_head_select(q, c, SCALE) for c in range(2)]
    for c in range(2):
        m_sc[c] = jnp.full((tq, 1), M_FLOOR, F32)
        l_sc[c] = jnp.zeros((tq, 1), F32)
        acc_sc[c] = jnp.zeros((tq, LANES), F32)

    def step(j, masked):
        start = pl.multiple_of(j * tq, tq)
        k = k_ref[0, 0, pl.ds(start, tq), :]
        v = v_ref[0, 0, pl.ds(start, tq), :]
        for c in range(2):
            s = _dot_nt(qs[c], k)
            if masked:
                row = lax.broadcasted_iota(jnp.int32, s.shape, 0)
                col = lax.broadcasted_iota(jnp.int32, s.shape, 1)
                s = jnp.where(row >= col, s, NEG_INF)
            _online_update(s, v, m_sc.at[c], l_sc.at[c], acc_sc.at[c])

    def body(j, carry):
        step(j, False)
        return carry

    lax.fori_loop(0, qi, body, 0)
    step(qi, True)

    lam = lam_ref[...]
    s01 = jnp.sum(lam[0:1] * lam[1:2], axis=-1, keepdims=True)
    s23 = jnp.sum(lam[2:3] * lam[3:4], axis=-1, keepdims=True)
    lam_val = jnp.exp(s01) - jnp.exp(s23) + lambda_init
    o0 = acc_sc[0] / jnp.maximum(l_sc[0], 1e-30)
    o1 = acc_sc[1] / jnp.maximum(l_sc[1], 1e-30)
    o = o0 - lam_val * o1
    y = o * lax.rsqrt(jnp.mean(o * o, axis=-1, keepdims=True) + NORM_EPS)
    y = (y * g_ref[...]) * (1.0 - lambda_init)
    o_ref[0] = y.astype(o_ref.dtype)


def _diff_attention(slabs, lam, subln_g, lambda_init, tq=256):
    B, S, T, _ = slabs.shape
    H = C_HEADS
    tq = min(tq, T)
    kern = functools.partial(_diff_attn_kernel, tq=tq, lambda_init=lambda_init)
    return pl.pallas_call(
        kern,
        grid=(B, H, T // tq),
        in_specs=[
            pl.BlockSpec((4, HEAD_DIM), lambda b, h, i: (0, 0)),
            pl.BlockSpec((1, 1, tq, LANES), lambda b, h, i: (b, h, i, 0)),
            pl.BlockSpec((1, 1, T, LANES), lambda b, h, i: (b, H + h, 0, 0)),
            pl.BlockSpec((1, 1, T, LANES), lambda b, h, i: (b, 2 * H + h, 0, 0)),
            pl.BlockSpec((1, LANES), lambda b, h, i: (0, 0)),
        ],
        out_specs=pl.BlockSpec((1, tq, LANES), lambda b, h, i: (b, i, h)),
        out_shape=jax.ShapeDtypeStruct((B, T, H * LANES), BF16),
        scratch_shapes=[
            pltpu.VMEM((2, tq, 1), F32),
            pltpu.VMEM((2, tq, 1), F32),
            pltpu.VMEM((2, tq, LANES), F32),
        ],
        compiler_params=_params(("parallel", "parallel", "arbitrary")),
        name="diff_attention",
    )(lam, slabs, slabs, slabs, subln_g.reshape(1, LANES))


def _dsa_select_kernel(qi_ref, ki_ref, misc_ref, tri_ref, bias_ref, key_sc,
                       *, tq, T, tk, topk):
    t0 = pl.program_id(1) * tq
    kk = ki_ref[0, 0]
    w = misc_ref[0, 0]
    score = jnp.zeros((tq, T), F32)
    for h in range(IDX_HEADS):
        r = _dot_nt(_head_select(qi_ref[0, h // 2], h % 2), kk)
        score = score + w[:, h:h + 1] * jnp.maximum(r, 0.0)

    row_t = t0 + lax.broadcasted_iota(jnp.int32, (tq, T), 0)
    col = lax.broadcasted_iota(jnp.int32, (tq, T), 1)
    bits = pltpu.bitcast(score, jnp.int32)
    key = bits ^ ((bits >> 31) & jnp.int32(0x7FFFFFFF))
    key = jnp.where(score == 0.0, 0, key)
    key = jnp.where(col <= row_t, key, INT_MIN)
    key_sc[...] = key

    kf = float(topk)

    def count_ge(c):
        return jnp.sum(jnp.where(key_sc[...] >= c, 1.0, 0.0), axis=1, keepdims=True)

    ans = jnp.where(count_ge(jnp.zeros((tq, 1), jnp.int32)) >= kf, 0, INT_MIN)

    def body(i, ans):
        cand = ans | (jnp.int32(1) << (30 - i))
        return jnp.where(count_ge(cand) >= kf, cand, ans)

    ans = lax.fori_loop(0, 31, body, ans)

    cnt_gt = jnp.sum(jnp.where(key_sc[...] > ans, 1.0, 0.0), axis=1, keepdims=True)
    need = kf - cnt_gt
    carry = jnp.zeros((tq, 1), F32)
    cw = tri_ref.shape[0]
    for c in range(T // cw):
        kc = key_sc[:, c * cw:(c + 1) * cw]
        eq = jnp.where(kc == ans, 1.0, 0.0)
        pre = _dot(eq.astype(BF16), tri_ref[...]) + carry
        take = jnp.where(kc > ans, 1.0, jnp.where(pre < need, eq, 0.0))
        rt = t0 + lax.broadcasted_iota(jnp.int32, (tq, cw), 0)
        cc = c * cw + lax.broadcasted_iota(jnp.int32, (tq, cw), 1)
        sel = jnp.where(cc <= rt, take, 0.0)
        bias = jnp.where(sel > 0.5, 0.0, NEG_INF).astype(BF16)
        j, off = divmod(c * cw, tk)
        bias_ref[0, j, :, off:off + cw] = bias
        carry = carry + jnp.sum(eq, axis=1, keepdims=True)


def _dsa_select(slabs_b, slabs_f, qi_slab0, ki_slab, misc_slab, tq=128, tk=512):
    B, S, T, _ = slabs_b.shape
    tk = min(tk, T)
    cw = 256
    topk = min(DSA_TOPK, T // 4)
    tri = jnp.asarray(np.triu(np.ones((cw, cw), np.float32), 1), BF16)
    kern = functools.partial(_dsa_select_kernel, tq=tq, T=T, tk=tk, topk=topk)
    return pl.pallas_call(
        kern,
        grid=(B, T // tq),
        in_specs=[
            pl.BlockSpec((1, 2, tq, LANES), lambda b, i: (b, qi_slab0 // 2, i, 0)),
            pl.BlockSpec((1, 1, T, LANES), lambda b, i: (b, ki_slab, 0, 0)),
            pl.BlockSpec((1, 1, tq, LANES), lambda b, i: (b, misc_slab, i, 0)),
            pl.BlockSpec((cw, cw), lambda b, i: (0, 0)),
        ],
        out_specs=pl.BlockSpec((1, T // tk, tq, tk), lambda b, i: (b, 0, i, 0)),
        out_shape=jax.ShapeDtypeStruct((B, T // tk, T, tk), BF16),
        scratch_shapes=[pltpu.VMEM((tq, T), jnp.int32)],
        compiler_params=_params(("parallel", "parallel")),
        name="dsa_select",
    )(slabs_b, slabs_b, slabs_f, tri)


def _dsa_attn_kernel(q_ref, k_ref, v_ref, bias_ref, o_ref, qst, m_sc, l_sc, acc_sc,
                     *, tq, tk):
    qi = pl.program_id(1)
    H = A_HEADS
    for h in range(H):
        qst[h * tq:(h + 1) * tq, :] = _head_select(q_ref[0, h // 2], h % 2, SCALE)
    m_sc[...] = jnp.full(m_sc.shape, M_FLOOR, F32)
    l_sc[...] = jnp.zeros(l_sc.shape, F32)
    acc_sc[...] = jnp.zeros(acc_sc.shape, F32)
    nkv = (qi * tq) // tk + 1

    def body(j, carry):
        start = pl.multiple_of(j * tk, tk)
        k = k_ref[0, 0, pl.ds(start, tk), :]
        v = v_ref[0, 0, pl.ds(start, tk), :]
        s = _dot_nt(qst[...], k)
        b = bias_ref[0, j].astype(F32)
        s = (s.reshape(H, tq, tk) + b[None]).reshape(H * tq, tk)
        _online_update(s, v, m_sc, l_sc, acc_sc)
        return carry

    lax.fori_loop(0, nkv, body, 0)
    o = acc_sc[...] / jnp.maximum(l_sc[...], 1e-30)
    for p in range(H // 2):
        even = o[(2 * p) * tq:(2 * p + 1) * tq]
        odd = o[(2 * p + 1) * tq:(2 * p + 2) * tq]
        o_ref[0, :, p * LANES:(p + 1) * LANES] = _merge_pair(even, odd).astype(o_ref.dtype)


def _dsa_attention(slabs_b, bias, q_slab0, k_slab, v_slab, tq=128):
    B, S, T, _ = slabs_b.shape
    nk, tk = bias.shape[1], bias.shape[3]
    H = A_HEADS
    kern = functools.partial(_dsa_attn_kernel, tq=tq, tk=tk)
    return pl.pallas_call(
        kern,
        grid=(B, T // tq),
        in_specs=[
            pl.BlockSpec((1, H // 2, tq, LANES), lambda b, i: (b, q_slab0 // (H // 2), i, 0)),
            pl.BlockSpec((1, 1, T, LANES), lambda b, i: (b, k_slab, 0, 0)),
            pl.BlockSpec((1, 1, T, LANES), lambda b, i: (b, v_slab, 0, 0)),
            pl.BlockSpec((1, nk, tq, tk), lambda b, i: (b, 0, i, 0)),
        ],
        out_specs=pl.BlockSpec((1, tq, H * HEAD_DIM), lambda b, i: (b, i, 0)),
        out_shape=jax.ShapeDtypeStruct((B, T, H * HEAD_DIM), BF16),
        scratch_shapes=[
            pltpu.VMEM((H * tq, LANES), BF16),
            pltpu.VMEM((H * tq, 1), F32),
            pltpu.VMEM((H * tq, 1), F32),
            pltpu.VMEM((H * tq, LANES), F32),
        ],
        compiler_params=_params(("parallel", "arbitrary")),
        name="dsa_attention",
    )(slabs_b, slabs_b, slabs_b, bias)


def _compress_kernel(x_ref, pe_ref, w1_ref, w2_ref, o_ref, *, n_ch):
    x = x_ref[0, 0]
    xt = (x + pe_ref[0, 0]).astype(BF16)
    xb = (x + pe_ref[0, 1]).astype(BF16)
    for g in range(B_KV_GROUPS):
        a = _dot(xt, w1_ref[0, g, 0])
        b = _dot(xb, w1_ref[0, g, 1])
        h = jax.nn.gelu(a + pltpu.roll(b, n_ch - 1, 0))
        o_ref[0, 0, g] = _dot(h.astype(BF16), w2_ref[0]).astype(o_ref.dtype)


def _compress(flat, pe, w1, w2):
    _, B, n_ch, W = flat.shape
    G = B_KV_GROUPS
    kern = functools.partial(_compress_kernel, n_ch=n_ch)
    return pl.pallas_call(
        kern,
        grid=(2, B),
        in_specs=[
            pl.BlockSpec((1, 1, n_ch, W), lambda s, b: (s, b, 0, 0)),
            pl.BlockSpec((1, 2, 1, W), lambda s, b: (s, 0, 0, 0)),
            pl.BlockSpec((1, G, 2, W, CMP_HIDDEN), lambda s, b: (s, 0, 0, 0, 0)),
            pl.BlockSpec((1, CMP_HIDDEN, LANES), lambda s, b: (s, 0, 0)),
        ],
        out_specs=pl.BlockSpec((1, 1, G, n_ch, LANES), lambda s, b: (s, b, 0, 0, 0)),
        out_shape=jax.ShapeDtypeStruct((2, B, G, n_ch, LANES), BF16),
        compiler_params=_params(("parallel", "parallel")),
        name="nsa_compress",
    )(flat, pe, w1, w2)


def _nsa_kernel(q_ref, kc_ref, vc_ref, ks_ref, vs_ref, kw_ref, vw_ref, misc_ref, ov_ref, ex_ref,
                o_ref, qst, m_sc, l_sc, acc_sc, *, tq, tk, T, n_s, n_sel, gate_lane0):
    g = pl.program_id(1)
    qi = pl.program_id(2)
    t0 = qi * tq
    J = B_PER_GROUP
    R = J * tq
    for j in range(J):
        qst[j * tq:(j + 1) * tq, :] = _head_select(q_ref[0, j // 2], j % 2, SCALE)
    q = qst[...]

    def row_time(shape):
        r = lax.broadcasted_iota(jnp.int32, shape, 0)
        return t0 + (r % tq)

    kc = kc_ref[0, 0, 0]
    vc = vc_ref[0, 0, 0]
    n_ch = kc.shape[0]
    s_c = _dot_nt(q, kc)
    cmp_end = lax.broadcasted_iota(jnp.int32, (R, n_ch), 1) * CMP_STRIDE + (CMP_BLOCK - 1)
    s_c = jnp.where(cmp_end <= row_time((R, n_ch)), s_c, NEG_INF)
    p_c = _softmax_once(s_c)
    o_c = _dot(p_c.astype(BF16), vc)

    psum = p_c[0:tq]
    for j in range(1, J):
        psum = psum + p_c[j * tq:(j + 1) * tq]
    p_hi = psum.astype(BF16)
    p_lo = (psum - p_hi.astype(F32)).astype(BF16)
    imp = _dot(p_hi, ov_ref[...]) + _dot(p_lo, ov_ref[...])
    blk = _lane_iota((tq, LANES))
    t_row = t0 + lax.broadcasted_iota(jnp.int32, (tq, LANES), 0)
    cur = t_row // SLC_BLOCK
    forced = (blk == 0) | (blk == cur) | (blk == cur - 1)
    sc = jnp.where(forced, jnp.inf, imp)
    sc = jnp.where(blk * SLC_BLOCK <= t_row, sc, NEG_INF)
    rank = jnp.zeros((tq, LANES), F32)
    for m in range(n_s):
        cm = sc[:, m:m + 1]
        beats = jnp.where(cm > sc, 1.0, jnp.where(cm == sc, jnp.where(blk > m, 1.0, 0.0), 0.0))
        rank = rank + beats
    sel = jnp.where(rank < float(n_sel), 1.0, 0.0).astype(BF16)

    m_sc[...] = jnp.full(m_sc.shape, M_FLOOR, F32)
    l_sc[...] = jnp.zeros(l_sc.shape, F32)
    acc_sc[...] = jnp.zeros(acc_sc.shape, F32)
    nkv = t0 // tk + 1

    def body(j, carry):
        start = pl.multiple_of(j * tk, tk)
        k = ks_ref[0, 0, pl.ds(start, tk), :]
        v = vs_ref[0, 0, pl.ds(start, tk), :]
        s = _dot_nt(q, k)
        hit = _dot(sel, ex_ref[j])
        pos = start + lax.broadcasted_iota(jnp.int32, (tq, tk), 1)
        tr = t0 + lax.broadcasted_iota(jnp.int32, (tq, tk), 0)
        bias = jnp.where(hit > 0.5, jnp.where(pos <= tr, 0.0, NEG_INF), NEG_INF)
        s = (s.reshape(J, tq, tk) + bias[None]).reshape(R, tk)
        _online_update(s, v, m_sc, l_sc, acc_sc)
        return carry

    lax.fori_loop(0, nkv, body, 0)
    o_s = acc_sc[...] / jnp.maximum(l_sc[...], 1e-30)

    wlen = min(WINDOW + tq, T)
    wstart = pl.multiple_of(jnp.maximum(t0 - WINDOW, 0), tq)
    kw = kw_ref[0, 0, pl.ds(wstart, wlen), :]
    vw = vw_ref[0, 0, pl.ds(wstart, wlen), :]
    s_w = _dot_nt(q, kw)
    pos = wstart + lax.broadcasted_iota(jnp.int32, (R, wlen), 1)
    tr = row_time((R, wlen))
    s_w = jnp.where(pos <= tr, jnp.where(pos > tr - WINDOW, s_w, NEG_INF), NEG_INF)
    o_w = _dot(_softmax_once(s_w).astype(BF16), vw)

    gates = jax.nn.sigmoid(misc_ref[0, 0])

    def gate_col(c):
        cols = []
        for j in range(J):
            l0 = gate_lane0 + j * 3 + c
            l1 = gate_lane0 + (J + j) * 3 + c
            cols.append(jnp.where(g == 0, gates[:, l0:l0 + 1], gates[:, l1:l1 + 1]))
        return jnp.concatenate(cols, axis=0)

    o = gate_col(0) * o_c + (gate_col(1) * o_s + gate_col(2) * o_w)
    for p in range(J // 2):
        even = o[(2 * p) * tq:(2 * p + 1) * tq]
        odd = o[(2 * p + 1) * tq:(2 * p + 2) * tq]
        o_ref[0, :, p * LANES:(p + 1) * LANES] = _merge_pair(even, odd).astype(o_ref.dtype)


def _nsa_attention(slabs_b, slabs_f, cmp_kv, lay, tq=128, tk=512):
    B, S, T, _ = slabs_b.shape
    G, J = B_KV_GROUPS, B_PER_GROUP
    tk = min(tk, T)
    n_ch = T // CMP_STRIDE
    n_c = n_ch - CMP_BLOCK // CMP_STRIDE + 1
    n_s = T // SLC_BLOCK
    n_sel = min(SLC_TOPN, n_s)
    assert n_s <= LANES
    c0 = np.arange(n_ch) * CMP_STRIDE
    s0 = np.arange(LANES) * SLC_BLOCK
    ov = ((c0[:, None] < s0[None, :] + SLC_BLOCK) & (c0[:, None] + CMP_BLOCK > s0[None, :]))
    ov = ov & (np.arange(n_ch)[:, None] < n_c) & (np.arange(LANES)[None, :] < n_s)
    ov = jnp.asarray(ov.astype(np.float32), BF16)
    pos = np.arange(T).reshape(T // tk, 1, tk)
    ex = (pos // SLC_BLOCK == np.arange(LANES).reshape(1, LANES, 1))
    ex = jnp.asarray(ex.astype(np.float32), BF16)
    kern = functools.partial(_nsa_kernel, tq=tq, tk=tk, T=T, n_s=n_s, n_sel=n_sel,
                             gate_lane0=lay["gate_lane0"])
    R = J * tq
    slab = lambda off: pl.BlockSpec((1, 1, T, LANES), lambda b, g, i: (b, off + g, 0, 0))
    return pl.pallas_call(
        kern,
        grid=(B, G, T // tq),
        in_specs=[
            pl.BlockSpec((1, 2, tq, LANES), lambda b, g, i: (b, lay["qB"] // 2 + g, i, 0)),
            pl.BlockSpec((1, 1, 1, n_ch, LANES), lambda b, g, i: (0, b, g, 0, 0)),
            pl.BlockSpec((1, 1, 1, n_ch, LANES), lambda b, g, i: (1, b, g, 0, 0)),
            slab(lay["ks"]), slab(lay["vs"]), slab(lay["kw"]), slab(lay["vw"]),
            pl.BlockSpec((1, 1, tq, LANES), lambda b, g, i: (b, lay["misc"], i, 0)),
            pl.BlockSpec((n_ch, LANES), lambda b, g, i: (0, 0)),
            pl.BlockSpec((T // tk, LANES, tk), lambda b, g, i: (0, 0, 0)),
        ],
        out_specs=pl.BlockSpec((1, tq, J * HEAD_DIM), lambda b, g, i: (b, i, g)),
        out_shape=jax.ShapeDtypeStruct((B, T, B_HEADS * HEAD_DIM), BF16),
        scratch_shapes=[
            pltpu.VMEM((R, LANES), BF16),
            pltpu.VMEM((R, 1), F32),
            pltpu.VMEM((R, 1), F32),
            pltpu.VMEM((R, LANES), F32),
        ],
        compiler_params=_params(("parallel", "parallel", "arbitrary")),
        name="nsa_attention",
    )(slabs_b, cmp_kv, cmp_kv, slabs_b, slabs_b, slabs_b, slabs_b, slabs_f, ov, ex)


def _out_proj_kernel(*refs, n_in):
    h_ref = refs[0]
    o_refs = refs[1:1 + n_in]
    w_refs = refs[1 + n_in:1 + 2 * n_in]
    out_ref = refs[1 + 2 * n_in]
    acc = _dot(o_refs[0][...], w_refs[0][...])
    for i in range(1, n_in):
        acc = acc + _dot(o_refs[i][...], w_refs[i][...])
    out_ref[...] = h_ref[...] + acc


def _out_proj(h2, outs, ws, tm=512):
    N, D = h2.shape
    tm = min(tm, N)
    n_in = len(outs)
    kern = functools.partial(_out_proj_kernel, n_in=n_in)
    in_specs = [pl.BlockSpec((tm, D), lambda i: (i, 0))]
    in_specs += [pl.BlockSpec((tm, o.shape[1]), lambda i: (i, 0)) for o in outs]
    in_specs += [pl.BlockSpec(w.shape, lambda i: (0, 0)) for w in ws]
    return pl.pallas_call(
        kern,
        grid=(N // tm,),
        in_specs=in_specs,
        out_specs=pl.BlockSpec((tm, D), lambda i: (i, 0)),
        out_shape=jax.ShapeDtypeStruct((N, D), F32),
        compiler_params=_params(("parallel",)),
        name="out_proj_residual",
    )(h2, *outs, *ws)


def _mlp_kernel(h_ref, g_ref, wu_ref, wd_ref, o_ref, xn_sc):
    f = pl.program_id(1)

    @pl.when(f == 0)
    def _():
        x = h_ref[...]
        ms = jnp.mean(x * x, axis=-1, keepdims=True)
        xn_sc[...] = ((x * lax.rsqrt(ms + NORM_EPS)) * g_ref[...]).astype(BF16)
        o_ref[...] = x

    u = _dot(xn_sc[...], wu_ref[...])
    a = jnp.square(jnp.maximum(u, 0.0)).astype(BF16)
    o_ref[...] += _dot(a, wd_ref[...])


def _mlp(h2, g, w_up, w_down, tm=512, tf=1024):
    N, D = h2.shape
    F = w_up.shape[1]
    tm = min(tm, N)
    return pl.pallas_call(
        _mlp_kernel,
        grid=(N // tm, F // tf),
        in_specs=[
            pl.BlockSpec((tm, D), lambda i, f: (i, 0)),
            pl.BlockSpec((1, D), lambda i, f: (0, 0)),
            pl.BlockSpec((D, tf), lambda i, f: (0, f)),
            pl.BlockSpec((tf, D), lambda i, f: (f, 0)),
        ],
        out_specs=pl.BlockSpec((tm, D), lambda i, f: (i, 0)),
        out_shape=jax.ShapeDtypeStruct((N, D), F32),
        scratch_shapes=[pltpu.VMEM((tm, D), BF16)],
        compiler_params=_params(("parallel", "arbitrary")),
        name="sq_relu_mlp",
    )(h2, g.reshape(1, D), w_up, w_down)


def _rmsnorm_kernel(x_ref, g_ref, o_ref):
    x = x_ref[...]
    ms = jnp.mean(x * x, axis=-1, keepdims=True)
    o_ref[...] = (x * lax.rsqrt(ms + NORM_EPS)) * g_ref[...]


def _rmsnorm(h2, g, tm=512):
    N, D = h2.shape
    tm = min(tm, N)
    return pl.pallas_call(
        _rmsnorm_kernel,
        grid=(N // tm,),
        in_specs=[pl.BlockSpec((tm, D), lambda i: (i, 0)), pl.BlockSpec((1, D), lambda i: (0, 0))],
        out_specs=pl.BlockSpec((tm, D), lambda i: (i, 0)),
        out_shape=jax.ShapeDtypeStruct((N, D), F32),
        compiler_params=_params(("parallel",)),
        name="final_rmsnorm",
    )(h2, g.reshape(1, D))


def _even_layout():
    offs = {}
    o = 0
    for name, size in (("qa", 512), ("ka", 64), ("va", 64), ("qi", 256), ("ki", 64), ("wi", 4),
                       ("qb", 512), ("kvb", 768), ("gb", 24)):
        offs[name] = o
        o += size
    kvb = lambda which, g: offs["kvb"] + (which * B_KV_GROUPS + g) * HEAD_DIM
    cols, plan, lay = [], [], {}
    nb = nf = 0

    def add(c, roped, dest):
        nonlocal nb, nf
        cols.append(c)
        if dest == "b":
            plan.append((roped, "b", nb))
            nb += 1
            return nb - 1
        plan.append((roped, "f", nf))
        nf += 1
        return nf - 1

    lay["qA"] = nb
    for p in range(4):
        add(_pair_cols(offs["qa"] + 2 * p * 64, offs["qa"] + (2 * p + 1) * 64), True, "b")
    lay["qB"] = nb
    for p in range(4):
        add(_pair_cols(offs["qb"] + 2 * p * 64, offs["qb"] + (2 * p + 1) * 64), True, "b")
    lay["qi"] = nb
    for p in range(2):
        add(_pair_cols(offs["qi"] + 2 * p * 64, offs["qi"] + (2 * p + 1) * 64), True, "b")
    lay["kA"] = add(_pair_cols(offs["ka"], offs["ka"]), True, "b")
    lay["ki"] = add(_pair_cols(offs["ki"], offs["ki"]), True, "b")
    lay["ks"] = nb
    for g in range(2):
        add(_pair_cols(kvb(2, g), kvb(2, g)), True, "b")
    lay["kw"] = nb
    for g in range(2):
        add(_pair_cols(kvb(4, g), kvb(4, g)), True, "b")
    lay["kc"] = add(_pair_cols(kvb(0, 0), kvb(0, 1)), True, "f")
    lay["vc"] = add(_pair_cols(kvb(1, 0), kvb(1, 1)), False, "f")
    misc = np.full(LANES, -1)
    misc[0:IDX_HEADS] = offs["wi"] + np.arange(IDX_HEADS)
    lay["gate_lane0"] = 8
    misc[8:8 + 24] = offs["gb"] + np.arange(24)
    lay["misc"] = add(misc, False, "f")
    lay["vA"] = add(_dup_cols(offs["va"]), False, "b")
    lay["vs"] = nb
    for g in range(2):
        add(_dup_cols(kvb(3, g)), False, "b")
    lay["vw"] = nb
    for g in range(2):
        add(_dup_cols(kvb(5, g)), False, "b")
    return np.concatenate(cols), plan, lay


def _odd_layout():
    cols, plan = [], []
    for h in range(C_HEADS):
        cols.append(_pair_cols(h * 128, h * 128 + 64))
        plan.append((True, "b", h))
    for h in range(C_HEADS):
        cols.append(_pair_cols(1024 + h * 128, 1024 + h * 128 + 64))
        plan.append((True, "b", C_HEADS + h))
    for h in range(C_HEADS):
        cols.append(2048 + h * 128 + np.arange(LANES))
        plan.append((False, "b", 2 * C_HEADS + h))
    return np.concatenate(cols), plan


def _compress_weights(pe, w1, w2):
    d = _PAIR_D
    which = _PAIR_WHICH
    pe_l = pe[:, :, d]
    pe_l = pe_l.reshape(2, 2, 1, CMP_STRIDE * LANES)
    w1r = w1.reshape(2, CMP_BLOCK, HEAD_DIM, CMP_HIDDEN)[:, :, d, :]
    per_g = []
    for g in range(B_KV_GROUPS):
        keep = jnp.asarray(which == g)[None, None, :, None]
        per_g.append(jnp.where(keep, w1r, 0.0))
    w1g = jnp.stack(per_g, axis=1)
    w1g = w1g.reshape(2, B_KV_GROUPS, 2, CMP_STRIDE * LANES, CMP_HIDDEN).astype(BF16)
    w2k = w2[0][:, d]
    w2v = w2[1][:, np.arange(LANES) % HEAD_DIM]
    w2l = jnp.stack([w2k, w2v], axis=0).astype(BF16)
    return pe_l, w1g, w2l


def _even_mixer(h, norm_g, w_in, cmp_pe, cmp_w1, cmp_w2, w_out, cos_slab, sin_slab):
    B, T, D = h.shape
    cols, plan, lay = _even_layout()
    w = _gather_cols(w_in, cols).astype(BF16)
    slabs_b, slabs_f = _project(h, norm_g, w, cos_slab, sin_slab, plan)

    bias = _dsa_select(slabs_b, slabs_f, lay["qi"], lay["ki"], lay["misc"])
    o_a = _dsa_attention(slabs_b, bias, lay["qA"], lay["kA"], lay["vA"])

    n_ch = T // CMP_STRIDE
    flat = slabs_f[:, lay["kc"]:lay["vc"] + 1].reshape(B, 2, n_ch, CMP_STRIDE * LANES)
    flat = jnp.swapaxes(flat, 0, 1)
    pe_l, w1g, w2l = _compress_weights(cmp_pe, cmp_w1, cmp_w2)
    cmp_kv = _compress(flat, pe_l, w1g, w2l)
    o_b = _nsa_attention(slabs_b, slabs_f, cmp_kv, lay)

    na = A_HEADS * HEAD_DIM
    wo = w_out.astype(BF16)
    h2 = _out_proj(h.reshape(B * T, D), [o_a.reshape(B * T, -1), o_b.reshape(B * T, -1)],
                   [wo[:na], wo[na:]])
    return h2.reshape(B, T, D)


def _odd_mixer(h, norm_g, w_in, lam, subln_g, w_out, cos_slab, sin_slab, lambda_init):
    B, T, D = h.shape
    cols, plan = _odd_layout()
    w = _gather_cols(w_in, cols).astype(BF16)
    (slabs_b,) = _project(h, norm_g, w, cos_slab, sin_slab, plan)
    o = _diff_attention(slabs_b, lam, subln_g, lambda_init)
    h2 = _out_proj(h.reshape(B * T, D), [o.reshape(B * T, -1)], [w_out.astype(BF16)])
    return h2.reshape(B, T, D)


def kernel(x, mix_norm_g, mlp_norm_g, even_w_in, even_cmp_pe, even_cmp_w1, even_cmp_w2, even_w_out, odd_w_in, odd_lambda, odd_subln_g, odd_w_out, mlp_w_up, mlp_w_down, final_norm_g):
    B, T, D = x.shape
    depth = mix_norm_g.shape[0]
    cos_slab, sin_slab = _rope_slabs(T)
    h = x
    for layer in range(depth):
        if layer % 2 == 0:
            e = layer // 2
            h = _even_mixer(h, mix_norm_g[layer], even_w_in[e], even_cmp_pe[e], even_cmp_w1[e],
                            even_cmp_w2[e], even_w_out[e], cos_slab, sin_slab)
        else:
            o = layer // 2
            lambda_init = 0.8 - 0.6 * math.exp(-0.3 * layer)
            h = _odd_mixer(h, mix_norm_g[layer], odd_w_in[o], odd_lambda[o], odd_subln_g[o],
                           odd_w_out[o], cos_slab, sin_slab, lambda_init)
        h2 = _mlp(h.reshape(B * T, D), mlp_norm_g[layer], mlp_w_up[layer].astype(BF16),
                  mlp_w_down[layer].astype(BF16))
        h = h2.reshape(B, T, D)
    return _rmsnorm(h.reshape(B * T, D), final_norm_g).reshape(B, T, D)
```

```python
import functools
import math

import numpy as np
import jax
import jax.numpy as jnp
from jax import lax
from jax.experimental import pallas as pl
from jax.experimental.pallas import tpu as pltpu

HEAD_DIM = 64
HALF = HEAD_DIM // 2
LANES = 128
ROPE_THETA = 10000.0
NORM_EPS = 1e-6
SCALE = HEAD_DIM ** -0.5

A_HEADS = 8
IDX_HEADS = 4
DSA_TOPK = 256
B_HEADS = 8
B_KV_GROUPS = 2
B_PER_GROUP = B_HEADS // B_KV_GROUPS
CMP_BLOCK = 32
CMP_STRIDE = 16
CMP_HIDDEN = 256
SLC_BLOCK = 64
SLC_TOPN = 16
WINDOW = 512
C_HEADS = 8

KV_TILE = 512
Q_TILE = 128

NEG_INF = float("-inf")
M_FLOOR = -1e30
INT_MIN = -(2 ** 31)

VMEM_LIMIT = 56 * 1024 * 1024

BF16 = jnp.bfloat16
F32 = jnp.float32


def _dot(a, b):
    return jnp.dot(a, b, preferred_element_type=F32)


def _dot_tn(a, b):
    return lax.dot_general(a, b, (((0,), (0,)), ((), ())), preferred_element_type=F32)


def _params(sem):
    return pltpu.CompilerParams(dimension_semantics=sem, vmem_limit_bytes=VMEM_LIMIT)


def _iota(shape, axis):
    return lax.broadcasted_iota(jnp.int32, shape, axis)


def _pair_cols(base_a, base_b):
    lane = np.arange(LANES)
    half = lane // 64
    which = (lane % 64) // HALF
    i = lane % HALF
    base = np.where(which == 0, base_a, base_b)
    return base + half * HALF + i


def _dup_cols(base):
    return base + np.arange(LANES) % HEAD_DIM


_PAIR_D = _pair_cols(0, 0)
_PAIR_WHICH = (np.arange(LANES) % 64) // HALF


def _gather_cols(w, cols):
    cols = np.asarray(cols)
    safe = np.where(cols >= 0, cols, 0)
    g = jnp.take(w, jnp.asarray(safe, dtype=jnp.int32), axis=1)
    return jnp.where(jnp.asarray(cols >= 0)[None, :], g, 0.0)


def _rope_slabs(T):
    inv = 1.0 / (ROPE_THETA ** (jnp.arange(0, HEAD_DIM, 2, dtype=F32) / HEAD_DIM))
    ang = jnp.arange(T, dtype=F32)[:, None] * inv[None, :]
    cos, sin = jnp.cos(ang), jnp.sin(ang)
    cos_slab = jnp.tile(cos, (1, 4))
    sin_slab = jnp.concatenate([-sin, -sin, sin, sin], axis=1)
    return cos_slab, sin_slab


_KINDS = ("qT", "k", "vT", "f", "fT")


def _proj_kernel(x_ref, g_ref, w_ref, cos_ref, sin_ref, *out_refs, plan, kinds, chunk):
    outs = dict(zip(kinds, out_refs))
    x = x_ref[0]
    ms = jnp.mean(x * x, axis=-1, keepdims=True)
    xn = ((x * lax.rsqrt(ms + NORM_EPS)) * g_ref[...]).astype(BF16)
    cos = cos_ref[...]
    sin = sin_ref[...]
    n = len(plan)
    for c0 in range(0, n, chunk):
        c1 = min(c0 + chunk, n)
        r = _dot(xn, w_ref[:, c0 * LANES:c1 * LANES])
        for s in range(c0, c1):
            roped, kind, idx = plan[s]
            y = r[:, (s - c0) * LANES:(s - c0 + 1) * LANES]
            if roped:
                y = y * cos + pltpu.roll(y, 64, 1) * sin
            if kind == "qT":
                outs[kind][0, idx] = y.T.astype(BF16)
            elif kind == "k":
                outs[kind][0, idx] = y.astype(BF16)
            elif kind == "vT":
                outs[kind][0, idx, 0] = y.T.astype(BF16)
            elif kind == "f":
                outs[kind][0, idx] = y
            else:
                outs[kind][0, idx] = y.T


def _project(x, g, w, cos_slab, sin_slab, plan, chunk=4):
    B, T, D = x.shape
    tm = KV_TILE
    assert T % tm == 0
    count = {k: sum(1 for p in plan if p[1] == k) for k in _KINDS}
    kinds = tuple(k for k in _KINDS if count[k])
    out_specs, out_shape = [], []
    for k in kinds:
        n = count[k]
        if k in ("qT", "fT"):
            out_specs.append(pl.BlockSpec((1, n, LANES, tm), lambda b, i: (b, 0, 0, i)))
            out_shape.append(jax.ShapeDtypeStruct((B, n, LANES, T), BF16 if k == "qT" else F32))
        elif k in ("k", "f"):
            out_specs.append(pl.BlockSpec((1, n, tm, LANES), lambda b, i: (b, 0, i, 0)))
            out_shape.append(jax.ShapeDtypeStruct((B, n, T, LANES), BF16 if k == "k" else F32))
        else:
            out_specs.append(pl.BlockSpec((1, n, 1, LANES, tm), lambda b, i: (b, 0, i, 0, 0)))
            out_shape.append(jax.ShapeDtypeStruct((B, n, T // tm, LANES, tm), BF16))
    kern = functools.partial(_proj_kernel, plan=tuple(plan), kinds=kinds, chunk=chunk)
    outs = pl.pallas_call(
        kern,
        grid=(B, T // tm),
        in_specs=[
            pl.BlockSpec((1, tm, D), lambda b, i: (b, i, 0)),
            pl.BlockSpec((1, D), lambda b, i: (0, 0)),
            pl.BlockSpec((D, len(plan) * LANES), lambda b, i: (0, 0)),
            pl.BlockSpec((tm, LANES), lambda b, i: (i, 0)),
            pl.BlockSpec((tm, LANES), lambda b, i: (i, 0)),
        ],
        out_specs=out_specs,
        out_shape=out_shape,
        compiler_params=_params(("parallel", "parallel")),
        name="norm_proj_rope",
    )(x, g.reshape(1, D), w, cos_slab, sin_slab)
    return dict(zip(kinds, outs))


def _head_rows(slab_t, which, scale=None):
    row = _iota(slab_t.shape, 0)
    keep = ((row % 64) // HALF) == which
    if scale is not None:
        slab_t = slab_t * scale
    return jnp.where(keep, slab_t, jnp.zeros_like(slab_t))


def _online_update(s, v_t, m_ref, l_ref, acc_ref):
    m_prev = m_ref[...]
    m_new = jnp.maximum(m_prev, jnp.max(s, axis=0, keepdims=True))
    alpha = jnp.exp(m_prev - m_new)
    p = jnp.exp(s - m_new)
    l_ref[...] = alpha * l_ref[...] + jnp.sum(p, axis=0, keepdims=True)
    acc_ref[...] = alpha * acc_ref[...] + _dot(v_t, p.astype(BF16))
    m_ref[...] = m_new


def _softmax_keys(s):
    m = jnp.max(s, axis=0, keepdims=True)
    m = jnp.where(m > NEG_INF, m, 0.0)
    e = jnp.exp(s - m)
    return e / jnp.maximum(jnp.sum(e, axis=0, keepdims=True), 1e-30)


def _merge_pair_rows(even, odd):
    return jnp.where(_iota(even.shape, 0) < 64, even, odd)


def _init_flash(m_ref, l_ref, acc_ref):
    m_ref[...] = jnp.full(m_ref.shape, M_FLOOR, F32)
    l_ref[...] = jnp.zeros(l_ref.shape, F32)
    acc_ref[...] = jnp.zeros(acc_ref.shape, F32)


def _diff_attn_kernel(lam_ref, q_ref, k_ref, v_ref, g_ref, o_ref, m_sc, l_sc, acc_sc,
                      *, tq, lambda_init):
    qi = pl.program_id(2)
    q_t = q_ref[0, 0]
    qs = [_head_rows(q_t, c, SCALE) for c in range(2)]
    for c in range(2):
        _init_flash(m_sc.at[c], l_sc.at[c], acc_sc.at[c])

    def step(j, masked):
        start = pl.multiple_of(j * tq, tq)
        k = k_ref[0, 0, pl.ds(start, tq), :]
        v_t = v_ref[0, 0, j]
        for c in range(2):
            s = _dot(k, qs[c])
            if masked:
                s = jnp.where(_iota(s.shape, 0) <= _iota(s.shape, 1), s, NEG_INF)
            _online_update(s, v_t, m_sc.at[c], l_sc.at[c], acc_sc.at[c])

    def body(j, carry):
        step(j, False)
        return carry

    lax.fori_loop(0, qi, body, 0)
    step(qi, True)

    lam = lam_ref[...]
    s01 = jnp.sum(lam[0:1] * lam[1:2], axis=-1, keepdims=True)
    s23 = jnp.sum(lam[2:3] * lam[3:4], axis=-1, keepdims=True)
    lam_val = jnp.exp(s01) - jnp.exp(s23) + lambda_init
    o0 = acc_sc[0] / jnp.maximum(l_sc[0], 1e-30)
    o1 = acc_sc[1] / jnp.maximum(l_sc[1], 1e-30)
    o = o0 - lam_val * o1
    y = o * lax.rsqrt(jnp.mean(o * o, axis=0, keepdims=True) + NORM_EPS)
    y = (y * g_ref[...]) * (1.0 - lambda_init)
    o_ref[0] = y.T.astype(o_ref.dtype)


def _diff_attention(sl, lam, subln_g, lambda_init):
    q_t, k, v_t = sl["qT"], sl["k"], sl["vT"]
    B, H, T, _ = k.shape
    tq = KV_TILE
    nk = T // tq
    kern = functools.partial(_diff_attn_kernel, tq=tq, lambda_init=lambda_init)
    return pl.pallas_call(
        kern,
        grid=(B, H, T // tq),
        in_specs=[
            pl.BlockSpec((4, HEAD_DIM), lambda b, h, i: (0, 0)),
            pl.BlockSpec((1, 1, LANES, tq), lambda b, h, i: (b, h, 0, i)),
            pl.BlockSpec((1, 1, T, LANES), lambda b, h, i: (b, h, 0, 0)),
            pl.BlockSpec((1, 1, nk, LANES, tq), lambda b, h, i: (b, h, 0, 0, 0)),
            pl.BlockSpec((LANES, 1), lambda b, h, i: (0, 0)),
        ],
        out_specs=pl.BlockSpec((1, tq, LANES), lambda b, h, i: (b, i, h)),
        out_shape=jax.ShapeDtypeStruct((B, T, H * LANES), BF16),
        scratch_shapes=[
            pltpu.VMEM((2, 1, tq), F32),
            pltpu.VMEM((2, 1, tq), F32),
            pltpu.VMEM((2, LANES, tq), F32),
        ],
        compiler_params=_params(("parallel", "parallel", "arbitrary")),
        name="diff_attention",
    )(lam, q_t, k, v_t, subln_g.reshape(LANES, 1))


def _dsa_select_kernel(qi_ref, ki_ref, misc_ref, tril_ref, bias_ref, key_sc, *, tq, ck, nk, topk):
    t0 = pl.program_id(1) * tq
    nvalid = (t0 + tq - 1) // ck + 1
    w = misc_ref[0, 0]
    qh = [_head_rows(qi_ref[0, h // 2], h % 2) for h in range(IDX_HEADS)]
    t_q = t0 + _iota((ck, tq), 1)

    def causal(c):
        return (c * ck + _iota((ck, tq), 0)) <= t_q

    def fill(c, carry):
        kk = ki_ref[0, 0, pl.ds(pl.multiple_of(c * ck, ck), ck), :]
        score = jnp.zeros((ck, tq), F32)
        for h in range(IDX_HEADS):
            score = score + w[h:h + 1, :] * jnp.maximum(_dot(kk, qh[h]), 0.0)
        bits = pltpu.bitcast(score, jnp.int32)
        key = bits ^ ((bits >> 31) & jnp.int32(0x7FFFFFFF))
        key = jnp.where(score == 0.0, 0, key)
        key_sc[c] = jnp.where(causal(c), key, INT_MIN)
        return carry

    lax.fori_loop(0, nvalid, fill, 0)

    def count(pred):
        def chunk(c, acc):
            ind = pred(key_sc[c], c)
            return acc + jnp.sum(ind.reshape(ck // 8, 8, tq), axis=0)
        acc = lax.fori_loop(0, nvalid, chunk, jnp.zeros((8, tq), F32))
        return jnp.sum(acc, axis=0, keepdims=True)

    kf = float(topk)
    zero = jnp.zeros((1, tq), jnp.int32)
    ans = jnp.where(count(lambda kc, c: jnp.where(kc >= zero, 1.0, 0.0)) >= kf, 0, INT_MIN)

    def bit_step(i, ans):
        cand = ans | (jnp.int32(1) << (30 - i))
        cnt = count(lambda kc, c: jnp.where(kc >= cand, 1.0, 0.0))
        return jnp.where(cnt >= kf, cand, ans)

    ans = lax.fori_loop(0, 31, bit_step, ans)

    cnt_gt = count(lambda kc, c: jnp.where(kc > ans, 1.0, 0.0))
    cnt_eq = count(lambda kc, c: jnp.where(kc == ans, jnp.where(causal(c), 1.0, 0.0), 0.0))
    need = kf - cnt_gt
    has_tie = jnp.max(cnt_eq - need) > 0.0

    @pl.when(jnp.logical_not(has_tie))
    def _():
        def emit(c, carry):
            sel = jnp.where(key_sc[c] >= ans, jnp.where(causal(c), 0.0, NEG_INF), NEG_INF)
            bias_ref[0, 0, c] = sel.astype(BF16)
            return carry
        lax.fori_loop(0, nvalid, emit, 0)

    @pl.when(has_tie)
    def _():
        def emit(c, carry):
            kc = key_sc[c]
            eq = jnp.where(kc == ans, jnp.where(causal(c), 1.0, 0.0), 0.0)
            pre = _dot(tril_ref[...], eq.astype(BF16)) + carry
            take = jnp.where(kc > ans, 1.0, jnp.where(pre < need, eq, 0.0))
            sel = jnp.where(take > 0.5, jnp.where(causal(c), 0.0, NEG_INF), NEG_INF)
            bias_ref[0, 0, c] = sel.astype(BF16)
            return carry + jnp.sum(eq, axis=0, keepdims=True)
        lax.fori_loop(0, nvalid, emit, jnp.zeros((1, tq), F32))

    def blank(c, carry):
        bias_ref[0, 0, c] = jnp.full((ck, tq), NEG_INF, BF16)
        return carry

    lax.fori_loop(nvalid, nk, blank, 0)


def _dsa_select(sl, lay):
    q_t, k, misc_t = sl["qT"], sl["k"], sl["fT"]
    B, _, T, _ = k.shape
    tq, ck = Q_TILE, KV_TILE
    nk = T // ck
    topk = min(DSA_TOPK, T // 4)
    tril = jnp.asarray(np.tril(np.ones((ck, ck), np.float32), -1), BF16)
    kern = functools.partial(_dsa_select_kernel, tq=tq, ck=ck, nk=nk, topk=topk)
    return pl.pallas_call(
        kern,
        grid=(B, T // tq),
        in_specs=[
            pl.BlockSpec((1, 2, LANES, tq), lambda b, i: (b, lay["qi"] // 2, 0, i)),
            pl.BlockSpec((1, 1, T, LANES), lambda b, i: (b, lay["ki"], 0, 0)),
            pl.BlockSpec((1, 1, LANES, tq), lambda b, i: (b, lay["misc"], 0, i)),
            pl.BlockSpec((ck, ck), lambda b, i: (0, 0)),
        ],
        out_specs=pl.BlockSpec((1, 1, nk, ck, tq), lambda b, i: (b, i, 0, 0, 0)),
        out_shape=jax.ShapeDtypeStruct((B, T // tq, nk, ck, tq), BF16),
        scratch_shapes=[pltpu.VMEM((nk, ck, tq), jnp.int32)],
        compiler_params=_params(("parallel", "parallel")),
        name="dsa_select",
    )(q_t, k, misc_t, tril)


def _store_head_pairs(o_ref, o, n_heads, tq):
    for p in range(n_heads // 2):
        even = o[:, (2 * p) * tq:(2 * p + 1) * tq]
        odd = o[:, (2 * p + 1) * tq:(2 * p + 2) * tq]
        o_ref[0, :, p * LANES:(p + 1) * LANES] = _merge_pair_rows(even, odd).T.astype(o_ref.dtype)


def _dsa_attn_kernel(q_ref, k_ref, v_ref, bias_ref, o_ref, qst, m_sc, l_sc, acc_sc, *, tq, tk):
    qi = pl.program_id(1)
    H = A_HEADS
    for h in range(H):
        qst[:, h * tq:(h + 1) * tq] = _head_rows(q_ref[0, h // 2], h % 2, SCALE)
    _init_flash(m_sc, l_sc, acc_sc)
    nkv = (qi * tq) // tk + 1

    def body(j, carry):
        k = k_ref[0, 0, pl.ds(pl.multiple_of(j * tk, tk), tk), :]
        b = bias_ref[0, 0, j].astype(F32)
        s = _dot(k, qst[...]) + jnp.concatenate([b] * H, axis=1)
        _online_update(s, v_ref[0, 0, j], m_sc, l_sc, acc_sc)
        return carry

    lax.fori_loop(0, nkv, body, 0)
    o = acc_sc[...] / jnp.maximum(l_sc[...], 1e-30)
    _store_head_pairs(o_ref, o, H, tq)


def _dsa_attention(sl, bias, lay):
    q_t, k, v_t = sl["qT"], sl["k"], sl["vT"]
    B, _, T, _ = k.shape
    tq, tk = Q_TILE, KV_TILE
    nk = T // tk
    H = A_HEADS
    N = H * tq
    kern = functools.partial(_dsa_attn_kernel, tq=tq, tk=tk)
    return pl.pallas_call(
        kern,
        grid=(B, T // tq),
        in_specs=[
            pl.BlockSpec((1, H // 2, LANES, tq), lambda b, i: (b, lay["qA"] // (H // 2), 0, i)),
            pl.BlockSpec((1, 1, T, LANES), lambda b, i: (b, lay["kA"], 0, 0)),
            pl.BlockSpec((1, 1, nk, LANES, tk), lambda b, i: (b, lay["vA"], 0, 0, 0)),
            pl.BlockSpec((1, 1, nk, tk, tq), lambda b, i: (b, i, 0, 0, 0)),
        ],
        out_specs=pl.BlockSpec((1, tq, H * HEAD_DIM), lambda b, i: (b, i, 0)),
        out_shape=jax.ShapeDtypeStruct((B, T, H * HEAD_DIM), BF16),
        scratch_shapes=[
            pltpu.VMEM((LANES, N), BF16),
            pltpu.VMEM((1, N), F32),
            pltpu.VMEM((1, N), F32),
            pltpu.VMEM((LANES, N), F32),
        ],
        compiler_params=_params(("parallel", "arbitrary")),
        name="dsa_attention",
    )(q_t, k, v_t, bias)


def _compress_kernel(x_ref, pe_ref, w1_ref, w2_ref, o_ref, ot_ref, *, n_ch):
    x = x_ref[0, 0]
    xt = (x + pe_ref[0, 0]).astype(BF16)
    xb = (x + pe_ref[0, 1]).astype(BF16)
    for g in range(B_KV_GROUPS):
        a = _dot(xt, w1_ref[0, g, 0])
        b = _dot(xb, w1_ref[0, g, 1])
        h = jax.nn.gelu(a + pltpu.roll(b, n_ch - 1, 0))
        r = _dot(h.astype(BF16), w2_ref[0])
        o_ref[0, 0, g] = r.astype(BF16)
        ot_ref[0, 0, g] = r.T.astype(BF16)


def _compress(flat, pe, w1, w2):
    _, B, n_ch, W = flat.shape
    G = B_KV_GROUPS
    kern = functools.partial(_compress_kernel, n_ch=n_ch)
    return pl.pallas_call(
        kern,
        grid=(2, B),
        in_specs=[
            pl.BlockSpec((1, 1, n_ch, W), lambda s, b: (s, b, 0, 0)),
            pl.BlockSpec((1, 2, 1, W), lambda s, b: (s, 0, 0, 0)),
            pl.BlockSpec((1, G, 2, W, CMP_HIDDEN), lambda s, b: (s, 0, 0, 0, 0)),
            pl.BlockSpec((1, CMP_HIDDEN, LANES), lambda s, b: (s, 0, 0)),
        ],
        out_specs=[
            pl.BlockSpec((1, 1, G, n_ch, LANES), lambda s, b: (s, b, 0, 0, 0)),
            pl.BlockSpec((1, 1, G, LANES, n_ch), lambda s, b: (s, b, 0, 0, 0)),
        ],
        out_shape=[
            jax.ShapeDtypeStruct((2, B, G, n_ch, LANES), BF16),
            jax.ShapeDtypeStruct((2, B, G, LANES, n_ch), BF16),
        ],
        compiler_params=_params(("parallel", "parallel")),
        name="nsa_compress",
    )(flat, pe, w1, w2)


def _nsa_kernel(q_ref, kc_ref, vc_ref, ks_ref, vs_ref, kw_ref, vw_ref, misc_ref, ov_ref, ex_ref,
                o_ref, qst, m_sc, l_sc, acc_sc, *, tq, tk, T, n_s, n_sel, gate_row0):
    g = pl.program_id(1)
    qi = pl.program_id(2)
    t0 = qi * tq
    J = B_PER_GROUP
    N = J * tq
    for j in range(J):
        qst[:, j * tq:(j + 1) * tq] = _head_rows(q_ref[0, j // 2], j % 2, SCALE)
    q = qst[...]

    def q_time(shape):
        return t0 + (_iota(shape, 1) % tq)

    kc = kc_ref[0, 0, 0]
    n_ch = kc.shape[0]
    s_c = _dot(kc, q)
    cmp_end = _iota((n_ch, N), 0) * CMP_STRIDE + (CMP_BLOCK - 1)
    s_c = jnp.where(cmp_end <= q_time((n_ch, N)), s_c, NEG_INF)
    p_c = _softmax_keys(s_c)
    o_c = _dot(vc_ref[0, 0, 0], p_c.astype(BF16))

    psum = p_c[:, 0:tq]
    for j in range(1, J):
        psum = psum + p_c[:, j * tq:(j + 1) * tq]
    p_hi = psum.astype(BF16)
    p_lo = (psum - p_hi.astype(F32)).astype(BF16)
    imp = _dot(ov_ref[...], p_hi) + _dot(ov_ref[...], p_lo)
    rows = -(-n_s // 8) * 8
    imp = imp[0:rows]
    blk = _iota((rows, tq), 0)
    t_q = t0 + _iota((rows, tq), 1)
    cur = t_q // SLC_BLOCK
    forced = (blk == 0) | (blk == cur) | (blk == cur - 1)
    sc = jnp.where(forced, jnp.inf, imp)
    sc = jnp.where(blk * SLC_BLOCK <= t_q, sc, NEG_INF)
    rank = jnp.zeros((rows, tq), F32)
    for m in range(n_s):
        cm = sc[m:m + 1, :]
        beats = jnp.where(cm > sc, 1.0, jnp.where(cm == sc, jnp.where(blk > m, 1.0, 0.0), 0.0))
        rank = rank + beats
    sel = jnp.where(rank < float(n_sel), 1.0, 0.0)
    if rows < LANES:
        sel = jnp.concatenate([sel, jnp.zeros((LANES - rows, tq), F32)], axis=0)
    sel = sel.astype(BF16)

    _init_flash(m_sc, l_sc, acc_sc)
    nkv = t0 // tk + 1

    def body(j, carry):
        start = pl.multiple_of(j * tk, tk)
        k = ks_ref[0, 0, pl.ds(start, tk), :]
        hit = _dot(ex_ref[j], sel)
        pos = start + _iota((tk, tq), 0)
        tt = t0 + _iota((tk, tq), 1)
        bias = jnp.where(hit > 0.5, jnp.where(pos <= tt, 0.0, NEG_INF), NEG_INF)
        s = _dot(k, q) + jnp.concatenate([bias] * J, axis=1)
        _online_update(s, vs_ref[0, 0, j], m_sc, l_sc, acc_sc)
        return carry

    lax.fori_loop(0, nkv, body, 0)
    o_s = acc_sc[...] / jnp.maximum(l_sc[...], 1e-30)

    wlen = min(WINDOW + tq, T)
    wstart = pl.multiple_of(jnp.maximum(t0 - WINDOW, 0), tq)
    kw = kw_ref[0, 0, pl.ds(wstart, wlen), :]
    vw = vw_ref[0, 0, pl.ds(wstart, wlen), :]
    s_w = _dot(kw, q)
    pos = wstart + _iota((wlen, N), 0)
    tt = q_time((wlen, N))
    s_w = jnp.where(pos <= tt, jnp.where(pos > tt - WINDOW, s_w, NEG_INF), NEG_INF)
    o_w = _dot_tn(vw, _softmax_keys(s_w).astype(BF16))

    gates = jax.nn.sigmoid(misc_ref[0, 0])

    def gate_row(c):
        parts = []
        for j in range(J):
            r0 = gate_row0 + j * 3 + c
            r1 = gate_row0 + (J + j) * 3 + c
            parts.append(jnp.where(g == 0, gates[r0:r0 + 1, :], gates[r1:r1 + 1, :]))
        return jnp.concatenate(parts, axis=1)

    o = gate_row(0) * o_c + (gate_row(1) * o_s + gate_row(2) * o_w)
    _store_head_pairs(o_ref, o, J, tq)


def _nsa_attention(sl, cmp_k, cmp_vt, lay):
    q_t, k, v_t, misc_t = sl["qT"], sl["k"], sl["vT"], sl["fT"]
    B, _, T, _ = k.shape
    G, J = B_KV_GROUPS, B_PER_GROUP
    tq, tk = Q_TILE, KV_TILE
    nk = T // tk
    n_ch = T // CMP_STRIDE
    n_c = n_ch - CMP_BLOCK // CMP_STRIDE + 1
    n_s = T // SLC_BLOCK
    n_sel = min(SLC_TOPN, n_s)
    assert n_s <= LANES
    c0 = np.arange(n_ch) * CMP_STRIDE
    s0 = np.arange(LANES) * SLC_BLOCK
    ov = ((c0[None, :] < s0[:, None] + SLC_BLOCK) & (c0[None, :] + CMP_BLOCK > s0[:, None]))
    ov = ov & (np.arange(n_ch)[None, :] < n_c) & (np.arange(LANES)[:, None] < n_s)
    ov = jnp.asarray(ov.astype(np.float32), BF16)
    pos = np.arange(T).reshape(nk, tk, 1)
    ex = (pos // SLC_BLOCK == np.arange(LANES).reshape(1, 1, LANES))
    ex = jnp.asarray(ex.astype(np.float32), BF16)
    kern = functools.partial(_nsa_kernel, tq=tq, tk=tk, T=T, n_s=n_s, n_sel=n_sel,
                             gate_row0=lay["gate_row0"])
    N = J * tq
    kslab = lambda off: pl.BlockSpec((1, 1, T, LANES), lambda b, g, i: (b, off + g, 0, 0))
    return pl.pallas_call(
        kern,
        grid=(B, G, T // tq),
        in_specs=[
            pl.BlockSpec((1, 2, LANES, tq), lambda b, g, i: (b, lay["qB"] // 2 + g, 0, i)),
            pl.BlockSpec((1, 1, 1, n_ch, LANES), lambda b, g, i: (0, b, g, 0, 0)),
            pl.BlockSpec((1, 1, 1, LANES, n_ch), lambda b, g, i: (1, b, g, 0, 0)),
            kslab(lay["ks"]),
            pl.BlockSpec((1, 1, nk, LANES, tk), lambda b, g, i: (b, lay["vs"] + g, 0, 0, 0)),
            kslab(lay["kw"]),
            kslab(lay["vw"]),
            pl.BlockSpec((1, 1, LANES, tq), lambda b, g, i: (b, lay["misc"], 0, i)),
            pl.BlockSpec((LANES, n_ch), lambda b, g, i: (0, 0)),
            pl.BlockSpec((nk, tk, LANES), lambda b, g, i: (0, 0, 0)),
        ],
        out_specs=pl.BlockSpec((1, tq, J * HEAD_DIM), lambda b, g, i: (b, i, g)),
        out_shape=jax.ShapeDtypeStruct((B, T, B_HEADS * HEAD_DIM), BF16),
        scratch_shapes=[
            pltpu.VMEM((LANES, N), BF16),
            pltpu.VMEM((1, N), F32),
            pltpu.VMEM((1, N), F32),
            pltpu.VMEM((LANES, N), F32),
        ],
        compiler_params=_params(("parallel", "parallel", "arbitrary")),
        name="nsa_attention",
    )(q_t, cmp_k, cmp_vt, k, v_t, k, k, misc_t, ov, ex)


def _out_proj_kernel(*refs, n_in):
    h_ref = refs[0]
    o_refs = refs[1:1 + n_in]
    w_refs = refs[1 + n_in:1 + 2 * n_in]
    out_ref = refs[1 + 2 * n_in]
    acc = _dot(o_refs[0][...], w_refs[0][...])
    for i in range(1, n_in):
        acc = acc + _dot(o_refs[i][...], w_refs[i][...])
    out_ref[...] = h_ref[...] + acc


def _out_proj(h2, outs, ws, tm=512):
    N, D = h2.shape
    tm = min(tm, N)
    n_in = len(outs)
    kern = functools.partial(_out_proj_kernel, n_in=n_in)
    in_specs = [pl.BlockSpec((tm, D), lambda i: (i, 0))]
    in_specs += [pl.BlockSpec((tm, o.shape[1]), lambda i: (i, 0)) for o in outs]
    in_specs += [pl.BlockSpec(w.shape, lambda i: (0, 0)) for w in ws]
    return pl.pallas_call(
        kern,
        grid=(N // tm,),
        in_specs=in_specs,
        out_specs=pl.BlockSpec((tm, D), lambda i: (i, 0)),
        out_shape=jax.ShapeDtypeStruct((N, D), F32),
        compiler_params=_params(("parallel",)),
        name="out_proj_residual",
    )(h2, *outs, *ws)


def _mlp_kernel(h_ref, g_ref, wu_ref, wd_ref, o_ref, xn_sc):
    f = pl.program_id(1)

    @pl.when(f == 0)
    def _():
        x = h_ref[...]
        ms = jnp.mean(x * x, axis=-1, keepdims=True)
        xn_sc[...] = ((x * lax.rsqrt(ms + NORM_EPS)) * g_ref[...]).astype(BF16)
        o_ref[...] = x

    u = _dot(xn_sc[...], wu_ref[...])
    a = jnp.square(jnp.maximum(u, 0.0)).astype(BF16)
    o_ref[...] += _dot(a, wd_ref[...])


def _mlp(h2, g, w_up, w_down, tm=512, tf=1024):
    N, D = h2.shape
    F = w_up.shape[1]
    tm = min(tm, N)
    return pl.pallas_call(
        _mlp_kernel,
        grid=(N // tm, F // tf),
        in_specs=[
            pl.BlockSpec((tm, D), lambda i, f: (i, 0)),
            pl.BlockSpec((1, D), lambda i, f: (0, 0)),
            pl.BlockSpec((D, tf), lambda i, f: (0, f)),
            pl.BlockSpec((tf, D), lambda i, f: (f, 0)),
        ],
        out_specs=pl.BlockSpec((tm, D), lambda i, f: (i, 0)),
        out_shape=jax.ShapeDtypeStruct((N, D), F32),
        scratch_shapes=[pltpu.VMEM((tm, D), BF16)],
        compiler_params=_params(("parallel", "arbitrary")),
        name="sq_relu_mlp",
    )(h2, g.reshape(1, D), w_up, w_down)


def _rmsnorm_kernel(x_ref, g_ref, o_ref):
    x = x_ref[...]
    ms = jnp.mean(x * x, axis=-1, keepdims=True)
    o_ref[...] = (x * lax.rsqrt(ms + NORM_EPS)) * g_ref[...]


def _rmsnorm(h2, g, tm=512):
    N, D = h2.shape
    tm = min(tm, N)
    return pl.pallas_call(
        _rmsnorm_kernel,
        grid=(N // tm,),
        in_specs=[pl.BlockSpec((tm, D), lambda i: (i, 0)), pl.BlockSpec((1, D), lambda i: (0, 0))],
        out_specs=pl.BlockSpec((tm, D), lambda i: (i, 0)),
        out_shape=jax.ShapeDtypeStruct((N, D), F32),
        compiler_params=_params(("parallel",)),
        name="final_rmsnorm",
    )(h2, g.reshape(1, D))


def _even_layout():
    offs = {}
    o = 0
    for name, size in (("qa", 512), ("ka", 64), ("va", 64), ("qi", 256), ("ki", 64), ("wi", 4),
                       ("qb", 512), ("kvb", 768), ("gb", 24)):
        offs[name] = o
        o += size
    kvb = lambda which, g: offs["kvb"] + (which * B_KV_GROUPS + g) * HEAD_DIM
    cols, plan, lay = [], [], {}
    n = {k: 0 for k in _KINDS}

    def add(c, roped, kind):
        cols.append(c)
        plan.append((roped, kind, n[kind]))
        n[kind] += 1
        return n[kind] - 1

    lay["qA"] = n["qT"]
    for p in range(4):
        add(_pair_cols(offs["qa"] + 2 * p * 64, offs["qa"] + (2 * p + 1) * 64), True, "qT")
    lay["qB"] = n["qT"]
    for p in range(4):
        add(_pair_cols(offs["qb"] + 2 * p * 64, offs["qb"] + (2 * p + 1) * 64), True, "qT")
    lay["qi"] = n["qT"]
    for p in range(2):
        add(_pair_cols(offs["qi"] + 2 * p * 64, offs["qi"] + (2 * p + 1) * 64), True, "qT")
    lay["kA"] = add(_pair_cols(offs["ka"], offs["ka"]), True, "k")
    lay["ki"] = add(_pair_cols(offs["ki"], offs["ki"]), True, "k")
    lay["ks"] = n["k"]
    for g in range(2):
        add(_pair_cols(kvb(2, g), kvb(2, g)), True, "k")
    lay["kw"] = n["k"]
    for g in range(2):
        add(_pair_cols(kvb(4, g), kvb(4, g)), True, "k")
    lay["kc"] = add(_pair_cols(kvb(0, 0), kvb(0, 1)), True, "f")
    lay["vc"] = add(_pair_cols(kvb(1, 0), kvb(1, 1)), False, "f")
    misc = np.full(LANES, -1)
    misc[0:IDX_HEADS] = offs["wi"] + np.arange(IDX_HEADS)
    lay["gate_row0"] = 8
    misc[8:8 + 24] = offs["gb"] + np.arange(24)
    lay["misc"] = add(misc, False, "fT")
    lay["vw"] = n["k"]
    for g in range(2):
        add(_dup_cols(kvb(5, g)), False, "k")
    lay["vA"] = add(_dup_cols(offs["va"]), False, "vT")
    lay["vs"] = n["vT"]
    for g in range(2):
        add(_dup_cols(kvb(3, g)), False, "vT")
    return np.concatenate(cols), plan, lay


def _odd_layout():
    cols, plan = [], []
    for h in range(C_HEADS):
        cols.append(_pair_cols(h * 128, h * 128 + 64))
        plan.append((True, "qT", h))
    for h in range(C_HEADS):
        cols.append(_pair_cols(1024 + h * 128, 1024 + h * 128 + 64))
        plan.append((True, "k", h))
    for h in range(C_HEADS):
        cols.append(2048 + h * 128 + np.arange(LANES))
        plan.append((False, "vT", h))
    return np.concatenate(cols), plan


def _compress_weights(pe, w1, w2):
    d = _PAIR_D
    which = _PAIR_WHICH
    pe_l = pe[:, :, d]
    pe_l = pe_l.reshape(2, 2, 1, CMP_STRIDE * LANES)
    w1r = w1.reshape(2, CMP_BLOCK, HEAD_DIM, CMP_HIDDEN)[:, :, d, :]
    per_g = []
    for g in range(B_KV_GROUPS):
        keep = jnp.asarray(which == g)[None, None, :, None]
        per_g.append(jnp.where(keep, w1r, 0.0))
    w1g = jnp.stack(per_g, axis=1)
    w1g = w1g.reshape(2, B_KV_GROUPS, 2, CMP_STRIDE * LANES, CMP_HIDDEN).astype(BF16)
    w2k = w2[0][:, d]
    w2v = w2[1][:, np.arange(LANES) % HEAD_DIM]
    w2l = jnp.stack([w2k, w2v], axis=0).astype(BF16)
    return pe_l, w1g, w2l


def _even_mixer(h, norm_g, w_in, cmp_pe, cmp_w1, cmp_w2, w_out, cos_slab, sin_slab):
    B, T, D = h.shape
    cols, plan, lay = _even_layout()
    w = _gather_cols(w_in, cols).astype(BF16)
    sl = _project(h, norm_g, w, cos_slab, sin_slab, plan)

    bias = _dsa_select(sl, lay)
    o_a = _dsa_attention(sl, bias, lay)

    n_ch = T // CMP_STRIDE
    flat = sl["f"][:, lay["kc"]:lay["vc"] + 1].reshape(B, 2, n_ch, CMP_STRIDE * LANES)
    flat = jnp.swapaxes(flat, 0, 1)
    pe_l, w1g, w2l = _compress_weights(cmp_pe, cmp_w1, cmp_w2)
    cmp_k, cmp_vt = _compress(flat, pe_l, w1g, w2l)
    o_b = _nsa_attention(sl, cmp_k, cmp_vt, lay)

    na = A_HEADS * HEAD_DIM
    wo = w_out.astype(BF16)
    h2 = _out_proj(h.reshape(B * T, D), [o_a.reshape(B * T, -1), o_b.reshape(B * T, -1)],
                   [wo[:na], wo[na:]])
    return h2.reshape(B, T, D)


def _odd_mixer(h, norm_g, w_in, lam, subln_g, w_out, cos_slab, sin_slab, lambda_init):
    B, T, D = h.shape
    cols, plan = _odd_layout()
    w = _gather_cols(w_in, cols).astype(BF16)
    sl = _project(h, norm_g, w, cos_slab, sin_slab, plan)
    o = _diff_attention(sl, lam, subln_g, lambda_init)
    h2 = _out_proj(h.reshape(B * T, D), [o.reshape(B * T, -1)], [w_out.astype(BF16)])
    return h2.reshape(B, T, D)


def kernel(x, mix_norm_g, mlp_norm_g, even_w_in, even_cmp_pe, even_cmp_w1, even_cmp_w2, even_w_out, odd_w_in, odd_lambda, odd_subln_g, odd_w_out, mlp_w_up, mlp_w_down, final_norm_g):
    B, T, D = x.shape
    depth = mix_norm_g.shape[0]
    cos_slab, sin_slab = _rope_slabs(T)
    h = x
    for layer in range(depth):
        if layer % 2 == 0:
            e = layer // 2
            h = _even_mixer(h, mix_norm_g[layer], even_w_in[e], even_cmp_pe[e], even_cmp_w1[e],
                            even_cmp_w2[e], even_w_out[e], cos_slab, sin_slab)
        else:
            o = layer // 2
            lambda_init = 0.8 - 0.6 * math.exp(-0.3 * layer)
            h = _odd_mixer(h, mix_norm_g[layer], odd_w_in[o], odd_lambda[o], odd_subln_g[o],
                           odd_w_out[o], cos_slab, sin_slab, lambda_init)
        h2 = _mlp(h.reshape(B * T, D), mlp_norm_g[layer], mlp_w_up[layer].astype(BF16),
                  mlp_w_down[layer].astype(BF16))
        h = h2.reshape(B, T, D)
    return _rmsnorm(h.reshape(B * T, D), final_norm_g).reshape(B, T, D)
```

```python
import functools
import math

import numpy as np
import jax
import jax.numpy as jnp
from jax import lax
from jax.experimental import pallas as pl
from jax.experimental.pallas import tpu as pltpu

HEAD_DIM = 64
HALF = HEAD_DIM // 2
LANES = 128
ROPE_THETA = 10000.0
NORM_EPS = 1e-6
SCALE = HEAD_DIM ** -0.5

A_HEADS = 8
IDX_HEADS = 4
DSA_TOPK = 256
B_HEADS = 8
B_KV_GROUPS = 2
B_PER_GROUP = B_HEADS // B_KV_GROUPS
CMP_BLOCK = 32
CMP_STRIDE = 16
CMP_HIDDEN = 256
SLC_BLOCK = 64
SLC_TOPN = 16
WINDOW = 512
C_HEADS = 8

KV_TILE = 512
Q_TILE = 128

NEG_INF = float("-inf")
M_FLOOR = -1e30
INT_MIN = -(2 ** 31)

VMEM_LIMIT = 56 * 1024 * 1024

BF16 = jnp.bfloat16
F32 = jnp.float32


def _dot(a, b):
    return jnp.dot(a, b, preferred_element_type=F32)


def _dot_tn(a, b):
    return lax.dot_general(a, b, (((0,), (0,)), ((), ())), preferred_element_type=F32)


def _params(sem):
    return pltpu.CompilerParams(dimension_semantics=sem, vmem_limit_bytes=VMEM_LIMIT)


def _iota(shape, axis):
    return lax.broadcasted_iota(jnp.int32, shape, axis)


def _pair_cols(base_a, base_b):
    lane = np.arange(LANES)
    half = lane // 64
    which = (lane % 64) // HALF
    i = lane % HALF
    base = np.where(which == 0, base_a, base_b)
    return base + half * HALF + i


def _dup_cols(base):
    return base + np.arange(LANES) % HEAD_DIM


_PAIR_D = _pair_cols(0, 0)
_PAIR_WHICH = (np.arange(LANES) % 64) // HALF


def _gather_cols(w, cols):
    cols = np.asarray(cols)
    safe = np.where(cols >= 0, cols, 0)
    g = jnp.take(w, jnp.asarray(safe, dtype=jnp.int32), axis=1)
    return jnp.where(jnp.asarray(cols >= 0)[None, :], g, 0.0)


def _rope_slabs(T):
    inv = 1.0 / (ROPE_THETA ** (jnp.arange(0, HEAD_DIM, 2, dtype=F32) / HEAD_DIM))
    ang = jnp.arange(T, dtype=F32)[:, None] * inv[None, :]
    cos, sin = jnp.cos(ang), jnp.sin(ang)
    cos_slab = jnp.tile(cos, (1, 4))
    sin_slab = jnp.concatenate([-sin, -sin, sin, sin], axis=1)
    return cos_slab, sin_slab


_KINDS = ("qT", "k", "vT", "f", "fT")


def _proj_kernel(x_ref, g_ref, w_ref, cos_ref, sin_ref, *out_refs, plan, kinds, chunk):
    outs = dict(zip(kinds, out_refs))
    x = x_ref[0]
    ms = jnp.mean(x * x, axis=-1, keepdims=True)
    xn = ((x * lax.rsqrt(ms + NORM_EPS)) * g_ref[...]).astype(BF16)
    cos = cos_ref[...]
    sin = sin_ref[...]
    n = len(plan)
    for c0 in range(0, n, chunk):
        c1 = min(c0 + chunk, n)
        r = _dot(xn, w_ref[:, c0 * LANES:c1 * LANES])
        for s in range(c0, c1):
            roped, kind, idx = plan[s]
            y = r[:, (s - c0) * LANES:(s - c0 + 1) * LANES]
            if roped:
                y = y * cos + pltpu.roll(y, 64, 1) * sin
            if kind == "qT":
                outs[kind][0, idx] = y.T.astype(BF16)
            elif kind == "k":
                outs[kind][0, idx] = y.astype(BF16)
            elif kind == "vT":
                outs[kind][0, idx, 0] = y.T.astype(BF16)
            elif kind == "f":
                outs[kind][0, idx] = y
            else:
                outs[kind][0, idx] = y.T


def _project(x, g, w, cos_slab, sin_slab, plan, chunk=4):
    B, T, D = x.shape
    tm = KV_TILE
    assert T % tm == 0
    count = {k: sum(1 for p in plan if p[1] == k) for k in _KINDS}
    kinds = tuple(k for k in _KINDS if count[k])
    out_specs, out_shape = [], []
    for k in kinds:
        n = count[k]
        if k in ("qT", "fT"):
            out_specs.append(pl.BlockSpec((1, n, LANES, tm), lambda b, i: (b, 0, 0, i)))
            out_shape.append(jax.ShapeDtypeStruct((B, n, LANES, T), BF16 if k == "qT" else F32))
        elif k in ("k", "f"):
            out_specs.append(pl.BlockSpec((1, n, tm, LANES), lambda b, i: (b, 0, i, 0)))
            out_shape.append(jax.ShapeDtypeStruct((B, n, T, LANES), BF16 if k == "k" else F32))
        else:
            out_specs.append(pl.BlockSpec((1, n, 1, LANES, tm), lambda b, i: (b, 0, i, 0, 0)))
            out_shape.append(jax.ShapeDtypeStruct((B, n, T // tm, LANES, tm), BF16))
    kern = functools.partial(_proj_kernel, plan=tuple(plan), kinds=kinds, chunk=chunk)
    outs = pl.pallas_call(
        kern,
        grid=(B, T // tm),
        in_specs=[
            pl.BlockSpec((1, tm, D), lambda b, i: (b, i, 0)),
            pl.BlockSpec((1, D), lambda b, i: (0, 0)),
            pl.BlockSpec((D, len(plan) * LANES), lambda b, i: (0, 0)),
            pl.BlockSpec((tm, LANES), lambda b, i: (i, 0)),
            pl.BlockSpec((tm, LANES), lambda b, i: (i, 0)),
        ],
        out_specs=out_specs,
        out_shape=out_shape,
        compiler_params=_params(("parallel", "parallel")),
        name="norm_proj_rope",
    )(x, g.reshape(1, D), w, cos_slab, sin_slab)
    return dict(zip(kinds, outs))


def _head_rows(slab_t, which, scale=None):
    row = _iota(slab_t.shape, 0)
    keep = ((row % 64) // HALF) == which
    if scale is not None:
        slab_t = slab_t * scale
    return jnp.where(keep, slab_t, jnp.zeros_like(slab_t))


def _tree(op, xs):
    while len(xs) > 1:
        xs = [op(xs[i], xs[i + 1]) if i + 1 < len(xs) else xs[i] for i in range(0, len(xs), 2)]
    return xs[0]


def _fold_rows(op, x, ways=4):
    rows, n = x.shape
    per = rows // ways
    parts = [op(x[i * per:(i + 1) * per].reshape(per // 8, 8, n), axis=0) for i in range(ways)]
    return _tree(jnp.maximum if op is jnp.max else jnp.add, parts)


ROW_BLOCK = 64


def _online_update(s_sc, p_sc, v_t, m_ref, l_ref, acc_ref):
    tk, n = s_sc.shape
    m_prev = m_ref[...]
    m_tile = jnp.max(_fold_rows(jnp.max, s_sc[...]), axis=0, keepdims=True)
    m_new = jnp.maximum(m_prev, m_tile)
    alpha = jnp.exp(m_prev - m_new)
    sums = []
    for r in range(tk // ROW_BLOCK):
        rows = slice(r * ROW_BLOCK, (r + 1) * ROW_BLOCK)
        p = jnp.exp(s_sc[rows, :] - m_new)
        sums.append(jnp.sum(p.reshape(ROW_BLOCK // 8, 8, n), axis=0))
        p_sc[rows, :] = p.astype(BF16)
    l_tile = jnp.sum(_tree(jnp.add, sums), axis=0, keepdims=True)
    l_ref[...] = alpha * l_ref[...] + l_tile
    acc_ref[...] = alpha * acc_ref[...] + _dot(v_t, p_sc[...])
    m_ref[...] = m_new


def _softmax_keys(s):
    m = jnp.max(s, axis=0, keepdims=True)
    m = jnp.where(m > NEG_INF, m, 0.0)
    e = jnp.exp(s - m)
    return e / jnp.maximum(jnp.sum(e, axis=0, keepdims=True), 1e-30)


def _merge_pair_rows(even, odd):
    return jnp.where(_iota(even.shape, 0) < 64, even, odd)


def _init_flash(m_ref, l_ref, acc_ref):
    m_ref[...] = jnp.full(m_ref.shape, M_FLOOR, F32)
    l_ref[...] = jnp.zeros(l_ref.shape, F32)
    acc_ref[...] = jnp.zeros(acc_ref.shape, F32)


def _diff_attn_kernel(lam_ref, q_ref, k_ref, v_ref, g_ref, o_ref, m_sc, l_sc, acc_sc, s_sc, p_sc,
                      *, tq, lambda_init):
    qi = pl.program_id(2)
    q_t = q_ref[0, 0]
    qs = [_head_rows(q_t, c, SCALE) for c in range(2)]
    for c in range(2):
        _init_flash(m_sc.at[c], l_sc.at[c], acc_sc.at[c])

    def step(j, masked):
        start = pl.multiple_of(j * tq, tq)
        k = k_ref[0, 0, pl.ds(start, tq), :]
        v_t = v_ref[0, 0, j]
        for c in range(2):
            s = _dot(k, qs[c])
            if masked:
                s = jnp.where(_iota(s.shape, 0) <= _iota(s.shape, 1), s, NEG_INF)
            s_sc[c] = s
            _online_update(s_sc.at[c], p_sc.at[c], v_t, m_sc.at[c], l_sc.at[c], acc_sc.at[c])

    def body(j, carry):
        step(j, False)
        return carry

    lax.fori_loop(0, qi, body, 0)
    step(qi, True)

    lam = lam_ref[...]
    s01 = jnp.sum(lam[0:1] * lam[1:2], axis=-1, keepdims=True)
    s23 = jnp.sum(lam[2:3] * lam[3:4], axis=-1, keepdims=True)
    lam_val = jnp.exp(s01) - jnp.exp(s23) + lambda_init
    o0 = acc_sc[0] / jnp.maximum(l_sc[0], 1e-30)
    o1 = acc_sc[1] / jnp.maximum(l_sc[1], 1e-30)
    o = o0 - lam_val * o1
    y = o * lax.rsqrt(jnp.mean(o * o, axis=0, keepdims=True) + NORM_EPS)
    y = (y * g_ref[...]) * (1.0 - lambda_init)
    o_ref[0] = y.T.astype(o_ref.dtype)


def _diff_attention(sl, lam, subln_g, lambda_init):
    q_t, k, v_t = sl["qT"], sl["k"], sl["vT"]
    B, H, T, _ = k.shape
    tq = KV_TILE
    nk = T // tq
    kern = functools.partial(_diff_attn_kernel, tq=tq, lambda_init=lambda_init)
    return pl.pallas_call(
        kern,
        grid=(B, H, T // tq),
        in_specs=[
            pl.BlockSpec((4, HEAD_DIM), lambda b, h, i: (0, 0)),
            pl.BlockSpec((1, 1, LANES, tq), lambda b, h, i: (b, h, 0, i)),
            pl.BlockSpec((1, 1, T, LANES), lambda b, h, i: (b, h, 0, 0)),
            pl.BlockSpec((1, 1, nk, LANES, tq), lambda b, h, i: (b, h, 0, 0, 0)),
            pl.BlockSpec((LANES, 1), lambda b, h, i: (0, 0)),
        ],
        out_specs=pl.BlockSpec((1, tq, LANES), lambda b, h, i: (b, i, h)),
        out_shape=jax.ShapeDtypeStruct((B, T, H * LANES), BF16),
        scratch_shapes=[
            pltpu.VMEM((2, 1, tq), F32),
            pltpu.VMEM((2, 1, tq), F32),
            pltpu.VMEM((2, LANES, tq), F32),
            pltpu.VMEM((2, tq, tq), F32),
            pltpu.VMEM((2, tq, tq), BF16),
        ],
        compiler_params=_params(("parallel", "parallel", "arbitrary")),
        name="diff_attention",
    )(lam, q_t, k, v_t, subln_g.reshape(LANES, 1))


def _dsa_select_kernel(qi_ref, ki_ref, misc_ref, tril_ref, bias_ref, key_sc, *, tq, ck, nk, topk):
    t0 = pl.program_id(1) * tq
    nvalid = (t0 + tq - 1) // ck + 1
    w = misc_ref[0, 0]
    qh = jnp.concatenate([_head_rows(qi_ref[0, h // 2], h % 2) for h in range(IDX_HEADS)], axis=1)
    t_q = t0 + _iota((ck, tq), 1)

    def causal(c):
        return (c * ck + _iota((ck, tq), 0)) <= t_q

    def fill(c, carry):
        kk = ki_ref[0, 0, pl.ds(pl.multiple_of(c * ck, ck), ck), :]
        r = _dot(kk, qh)
        score = jnp.zeros((ck, tq), F32)
        for h in range(IDX_HEADS):
            score = score + w[h:h + 1, :] * jnp.maximum(r[:, h * tq:(h + 1) * tq], 0.0)
        bits = pltpu.bitcast(score, jnp.int32)
        key = bits ^ ((bits >> 31) & jnp.int32(0x7FFFFFFF))
        key = jnp.where(score == 0.0, 0, key)
        key_sc[c] = jnp.where(causal(c), key, INT_MIN)
        return carry

    lax.fori_loop(0, nvalid, fill, 0)

    def count(pred):
        def chunk(c, acc):
            ind = pred(key_sc[c], c)
            return acc + jnp.sum(ind.reshape(8, ck // 8, tq), axis=0)
        acc = lax.fori_loop(0, nvalid, chunk, jnp.zeros((ck // 8, tq), F32))
        return jnp.sum(acc, axis=0, keepdims=True)

    kf = float(topk)
    zero = jnp.zeros((1, tq), jnp.int32)
    ans = jnp.where(count(lambda kc, c: jnp.where(kc >= zero, 1.0, 0.0)) >= kf, 0, INT_MIN)

    def bit_step(i, ans):
        cand = ans | (jnp.int32(1) << (30 - i))
        cnt = count(lambda kc, c: jnp.where(kc >= cand, 1.0, 0.0))
        return jnp.where(cnt >= kf, cand, ans)

    ans = lax.fori_loop(0, 31, bit_step, ans)

    cnt_gt = count(lambda kc, c: jnp.where(kc > ans, 1.0, 0.0))
    cnt_eq = count(lambda kc, c: jnp.where(kc == ans, jnp.where(causal(c), 1.0, 0.0), 0.0))
    need = kf - cnt_gt
    has_tie = jnp.max(cnt_eq - need) > 0.0

    @pl.when(jnp.logical_not(has_tie))
    def _():
        def emit(c, carry):
            sel = jnp.where(key_sc[c] >= ans, jnp.where(causal(c), 0.0, NEG_INF), NEG_INF)
            bias_ref[0, 0, c] = sel.astype(BF16)
            return carry
        lax.fori_loop(0, nvalid, emit, 0)

    @pl.when(has_tie)
    def _():
        def emit(c, carry):
            kc = key_sc[c]
            eq = jnp.where(kc == ans, jnp.where(causal(c), 1.0, 0.0), 0.0)
            pre = _dot(tril_ref[...], eq.astype(BF16)) + carry
            take = jnp.where(kc > ans, 1.0, jnp.where(pre < need, eq, 0.0))
            sel = jnp.where(take > 0.5, jnp.where(causal(c), 0.0, NEG_INF), NEG_INF)
            bias_ref[0, 0, c] = sel.astype(BF16)
            return carry + jnp.sum(eq, axis=0, keepdims=True)
        lax.fori_loop(0, nvalid, emit, jnp.zeros((1, tq), F32))

    def blank(c, carry):
        bias_ref[0, 0, c] = jnp.full((ck, tq), NEG_INF, BF16)
        return carry

    lax.fori_loop(nvalid, nk, blank, 0)


def _dsa_select(sl, lay):
    q_t, k, misc_t = sl["qT"], sl["k"], sl["fT"]
    B, _, T, _ = k.shape
    tq, ck = Q_TILE, KV_TILE
    nk = T // ck
    topk = min(DSA_TOPK, T // 4)
    tril = jnp.asarray(np.tril(np.ones((ck, ck), np.float32), -1), BF16)
    kern = functools.partial(_dsa_select_kernel, tq=tq, ck=ck, nk=nk, topk=topk)
    return pl.pallas_call(
        kern,
        grid=(B, T // tq),
        in_specs=[
            pl.BlockSpec((1, 2, LANES, tq), lambda b, i: (b, lay["qi"] // 2, 0, i)),
            pl.BlockSpec((1, 1, T, LANES), lambda b, i: (b, lay["ki"], 0, 0)),
            pl.BlockSpec((1, 1, LANES, tq), lambda b, i: (b, lay["misc"], 0, i)),
            pl.BlockSpec((ck, ck), lambda b, i: (0, 0)),
        ],
        out_specs=pl.BlockSpec((1, 1, nk, ck, tq), lambda b, i: (b, i, 0, 0, 0)),
        out_shape=jax.ShapeDtypeStruct((B, T // tq, nk, ck, tq), BF16),
        scratch_shapes=[pltpu.VMEM((nk, ck, tq), jnp.int32)],
        compiler_params=_params(("parallel", "parallel")),
        name="dsa_select",
    )(q_t, k, misc_t, tril)


def _store_head_pairs(o_ref, o, n_heads, tq):
    for p in range(n_heads // 2):
        even = o[:, (2 * p) * tq:(2 * p + 1) * tq]
        odd = o[:, (2 * p + 1) * tq:(2 * p + 2) * tq]
        o_ref[0, :, p * LANES:(p + 1) * LANES] = _merge_pair_rows(even, odd).T.astype(o_ref.dtype)


def _dsa_attn_kernel(q_ref, k_ref, v_ref, bias_ref, o_ref, qst, m_sc, l_sc, acc_sc, s_sc, p_sc,
                     *, tq, tk):
    qi = pl.program_id(1)
    H = A_HEADS
    for h in range(H):
        qst[:, h * tq:(h + 1) * tq] = _head_rows(q_ref[0, h // 2], h % 2, SCALE)
    _init_flash(m_sc, l_sc, acc_sc)
    nkv = (qi * tq) // tk + 1

    def body(j, carry):
        k = k_ref[0, 0, pl.ds(pl.multiple_of(j * tk, tk), tk), :]
        b = bias_ref[0, 0, j].astype(F32)
        s_sc[...] = _dot(k, qst[...]) + jnp.concatenate([b] * H, axis=1)
        _online_update(s_sc, p_sc, v_ref[0, 0, j], m_sc, l_sc, acc_sc)
        return carry

    lax.fori_loop(0, nkv, body, 0)
    o = acc_sc[...] / jnp.maximum(l_sc[...], 1e-30)
    _store_head_pairs(o_ref, o, H, tq)


def _dsa_attention(sl, bias, lay):
    q_t, k, v_t = sl["qT"], sl["k"], sl["vT"]
    B, _, T, _ = k.shape
    tq, tk = Q_TILE, KV_TILE
    nk = T // tk
    H = A_HEADS
    N = H * tq
    kern = functools.partial(_dsa_attn_kernel, tq=tq, tk=tk)
    return pl.pallas_call(
        kern,
        grid=(B, T // tq),
        in_specs=[
            pl.BlockSpec((1, H // 2, LANES, tq), lambda b, i: (b, lay["qA"] // (H // 2), 0, i)),
            pl.BlockSpec((1, 1, T, LANES), lambda b, i: (b, lay["kA"], 0, 0)),
            pl.BlockSpec((1, 1, nk, LANES, tk), lambda b, i: (b, lay["vA"], 0, 0, 0)),
            pl.BlockSpec((1, 1, nk, tk, tq), lambda b, i: (b, i, 0, 0, 0)),
        ],
        out_specs=pl.BlockSpec((1, tq, H * HEAD_DIM), lambda b, i: (b, i, 0)),
        out_shape=jax.ShapeDtypeStruct((B, T, H * HEAD_DIM), BF16),
        scratch_shapes=[
            pltpu.VMEM((LANES, N), BF16),
            pltpu.VMEM((1, N), F32),
            pltpu.VMEM((1, N), F32),
            pltpu.VMEM((LANES, N), F32),
            pltpu.VMEM((tk, N), F32),
            pltpu.VMEM((tk, N), BF16),
        ],
        compiler_params=_params(("parallel", "arbitrary")),
        name="dsa_attention",
    )(q_t, k, v_t, bias)


def _compress_kernel(x_ref, pe_ref, w1_ref, w2_ref, o_ref, ot_ref, *, n_ch):
    x = x_ref[0, 0]
    xt = (x + pe_ref[0, 0]).astype(BF16)
    xb = (x + pe_ref[0, 1]).astype(BF16)
    for g in range(B_KV_GROUPS):
        a = _dot(xt, w1_ref[0, g, 0])
        b = _dot(xb, w1_ref[0, g, 1])
        h = jax.nn.gelu(a + pltpu.roll(b, n_ch - 1, 0))
        r = _dot(h.astype(BF16), w2_ref[0])
        o_ref[0, 0, g] = r.astype(BF16)
        ot_ref[0, 0, g] = r.T.astype(BF16)


def _compress(flat, pe, w1, w2):
    _, B, n_ch, W = flat.shape
    G = B_KV_GROUPS
    kern = functools.partial(_compress_kernel, n_ch=n_ch)
    return pl.pallas_call(
        kern,
        grid=(2, B),
        in_specs=[
            pl.BlockSpec((1, 1, n_ch, W), lambda s, b: (s, b, 0, 0)),
            pl.BlockSpec((1, 2, 1, W), lambda s, b: (s, 0, 0, 0)),
            pl.BlockSpec((1, G, 2, W, CMP_HIDDEN), lambda s, b: (s, 0, 0, 0, 0)),
            pl.BlockSpec((1, CMP_HIDDEN, LANES), lambda s, b: (s, 0, 0)),
        ],
        out_specs=[
            pl.BlockSpec((1, 1, G, n_ch, LANES), lambda s, b: (s, b, 0, 0, 0)),
            pl.BlockSpec((1, 1, G, LANES, n_ch), lambda s, b: (s, b, 0, 0, 0)),
        ],
        out_shape=[
            jax.ShapeDtypeStruct((2, B, G, n_ch, LANES), BF16),
            jax.ShapeDtypeStruct((2, B, G, LANES, n_ch), BF16),
        ],
        compiler_params=_params(("parallel", "parallel")),
        name="nsa_compress",
    )(flat, pe, w1, w2)


def _nsa_kernel(q_ref, kc_ref, vc_ref, ks_ref, vs_ref, kw_ref, vw_ref, misc_ref, ov_ref, ex_ref,
                o_ref, qst, m_sc, l_sc, acc_sc, s_sc, p_sc, *, tq, tk, T, n_s, n_sel, gate_row0):
    g = pl.program_id(1)
    qi = pl.program_id(2)
    t0 = qi * tq
    J = B_PER_GROUP
    N = J * tq
    for j in range(J):
        qst[:, j * tq:(j + 1) * tq] = _head_rows(q_ref[0, j // 2], j % 2, SCALE)
    q = qst[...]

    def q_time(shape):
        return t0 + (_iota(shape, 1) % tq)

    kc = kc_ref[0, 0, 0]
    n_ch = kc.shape[0]
    s_c = _dot(kc, q)
    cmp_end = _iota((n_ch, N), 0) * CMP_STRIDE + (CMP_BLOCK - 1)
    s_c = jnp.where(cmp_end <= q_time((n_ch, N)), s_c, NEG_INF)
    p_c = _softmax_keys(s_c)
    o_c = _dot(vc_ref[0, 0, 0], p_c.astype(BF16))

    psum = p_c[:, 0:tq]
    for j in range(1, J):
        psum = psum + p_c[:, j * tq:(j + 1) * tq]
    p_hi = psum.astype(BF16)
    p_lo = (psum - p_hi.astype(F32)).astype(BF16)
    imp = _dot(ov_ref[...], p_hi) + _dot(ov_ref[...], p_lo)
    rows = -(-n_s // 8) * 8
    imp = imp[0:rows]
    blk = _iota((rows, tq), 0)
    t_q = t0 + _iota((rows, tq), 1)
    cur = t_q // SLC_BLOCK
    forced = (blk == 0) | (blk == cur) | (blk == cur - 1)
    sc = jnp.where(forced, jnp.inf, imp)
    sc = jnp.where(blk * SLC_BLOCK <= t_q, sc, NEG_INF)
    rank = jnp.zeros((rows, tq), F32)
    for m in range(n_s):
        cm = sc[m:m + 1, :]
        beats = jnp.where(cm > sc, 1.0, jnp.where(cm == sc, jnp.where(blk > m, 1.0, 0.0), 0.0))
        rank = rank + beats
    sel = jnp.where(rank < float(n_sel), 1.0, 0.0)
    if rows < LANES:
        sel = jnp.concatenate([sel, jnp.zeros((LANES - rows, tq), F32)], axis=0)
    sel = sel.astype(BF16)

    _init_flash(m_sc, l_sc, acc_sc)
    nkv = t0 // tk + 1

    def body(j, carry):
        start = pl.multiple_of(j * tk, tk)
        k = ks_ref[0, 0, pl.ds(start, tk), :]
        hit = _dot(ex_ref[j], sel)
        pos = start + _iota((tk, tq), 0)
        tt = t0 + _iota((tk, tq), 1)
        bias = jnp.where(hit > 0.5, jnp.where(pos <= tt, 0.0, NEG_INF), NEG_INF)
        s_sc[...] = _dot(k, q) + jnp.concatenate([bias] * J, axis=1)
        _online_update(s_sc, p_sc, vs_ref[0, 0, j], m_sc, l_sc, acc_sc)
        return carry

    lax.fori_loop(0, nkv, body, 0)
    o_s = acc_sc[...] / jnp.maximum(l_sc[...], 1e-30)

    wlen = min(WINDOW + tq, T)
    wstart = pl.multiple_of(jnp.maximum(t0 - WINDOW, 0), tq)
    kw = kw_ref[0, 0, pl.ds(wstart, wlen), :]
    vw = vw_ref[0, 0, pl.ds(wstart, wlen), :]
    s_w = _dot(kw, q)
    pos = wstart + _iota((wlen, N), 0)
    tt = q_time((wlen, N))
    s_w = jnp.where(pos <= tt, jnp.where(pos > tt - WINDOW, s_w, NEG_INF), NEG_INF)
    o_w = _dot_tn(vw, _softmax_keys(s_w).astype(BF16))

    gates = jax.nn.sigmoid(misc_ref[0, 0])

    def gate_row(c):
        parts = []
        for j in range(J):
            r0 = gate_row0 + j * 3 + c
            r1 = gate_row0 + (J + j) * 3 + c
            parts.append(jnp.where(g == 0, gates[r0:r0 + 1, :], gates[r1:r1 + 1, :]))
        return jnp.concatenate(parts, axis=1)

    o = gate_row(0) * o_c + (gate_row(1) * o_s + gate_row(2) * o_w)
    _store_head_pairs(o_ref, o, J, tq)


def _nsa_attention(sl, cmp_k, cmp_vt, lay):
    q_t, k, v_t, misc_t = sl["qT"], sl["k"], sl["vT"], sl["fT"]
    B, _, T, _ = k.shape
    G, J = B_KV_GROUPS, B_PER_GROUP
    tq, tk = Q_TILE, KV_TILE
    nk = T // tk
    n_ch = T // CMP_STRIDE
    n_c = n_ch - CMP_BLOCK // CMP_STRIDE + 1
    n_s = T // SLC_BLOCK
    n_sel = min(SLC_TOPN, n_s)
    assert n_s <= LANES
    c0 = np.arange(n_ch) * CMP_STRIDE
    s0 = np.arange(LANES) * SLC_BLOCK
    ov = ((c0[None, :] < s0[:, None] + SLC_BLOCK) & (c0[None, :] + CMP_BLOCK > s0[:, None]))
    ov = ov & (np.arange(n_ch)[None, :] < n_c) & (np.arange(LANES)[:, None] < n_s)
    ov = jnp.asarray(ov.astype(np.float32), BF16)
    pos = np.arange(T).reshape(nk, tk, 1)
    ex = (pos // SLC_BLOCK == np.arange(LANES).reshape(1, 1, LANES))
    ex = jnp.asarray(ex.astype(np.float32), BF16)
    kern = functools.partial(_nsa_kernel, tq=tq, tk=tk, T=T, n_s=n_s, n_sel=n_sel,
                             gate_row0=lay["gate_row0"])
    N = J * tq
    kslab = lambda off: pl.BlockSpec((1, 1, T, LANES), lambda b, g, i: (b, off + g, 0, 0))
    return pl.pallas_call(
        kern,
        grid=(B, G, T // tq),
        in_specs=[
            pl.BlockSpec((1, 2, LANES, tq), lambda b, g, i: (b, lay["qB"] // 2 + g, 0, i)),
            pl.BlockSpec((1, 1, 1, n_ch, LANES), lambda b, g, i: (0, b, g, 0, 0)),
            pl.BlockSpec((1, 1, 1, LANES, n_ch), lambda b, g, i: (1, b, g, 0, 0)),
            kslab(lay["ks"]),
            pl.BlockSpec((1, 1, nk, LANES, tk), lambda b, g, i: (b, lay["vs"] + g, 0, 0, 0)),
            kslab(lay["kw"]),
            kslab(lay["vw"]),
            pl.BlockSpec((1, 1, LANES, tq), lambda b, g, i: (b, lay["misc"], 0, i)),
            pl.BlockSpec((LANES, n_ch), lambda b, g, i: (0, 0)),
            pl.BlockSpec((nk, tk, LANES), lambda b, g, i: (0, 0, 0)),
        ],
        out_specs=pl.BlockSpec((1, tq, J * HEAD_DIM), lambda b, g, i: (b, i, g)),
        out_shape=jax.ShapeDtypeStruct((B, T, B_HEADS * HEAD_DIM), BF16),
        scratch_shapes=[
            pltpu.VMEM((LANES, N), BF16),
            pltpu.VMEM((1, N), F32),
            pltpu.VMEM((1, N), F32),
            pltpu.VMEM((LANES, N), F32),
            pltpu.VMEM((tk, N), F32),
            pltpu.VMEM((tk, N), BF16),
        ],
        compiler_params=_params(("parallel", "parallel", "arbitrary")),
        name="nsa_attention",
    )(q_t, cmp_k, cmp_vt, k, v_t, k, k, misc_t, ov, ex)


def _out_proj_kernel(*refs, n_in):
    h_ref = refs[0]
    o_refs = refs[1:1 + n_in]
    w_refs = refs[1 + n_in:1 + 2 * n_in]
    out_ref = refs[1 + 2 * n_in]
    acc = _dot(o_refs[0][...], w_refs[0][...])
    for i in range(1, n_in):
        acc = acc + _dot(o_refs[i][...], w_refs[i][...])
    out_ref[...] = h_ref[...] + acc


def _out_proj(h2, outs, ws, tm=512):
    N, D = h2.shape
    tm = min(tm, N)
    n_in = len(outs)
    kern = functools.partial(_out_proj_kernel, n_in=n_in)
    in_specs = [pl.BlockSpec((tm, D), lambda i: (i, 0))]
    in_specs += [pl.BlockSpec((tm, o.shape[1]), lambda i: (i, 0)) for o in outs]
    in_specs += [pl.BlockSpec(w.shape, lambda i: (0, 0)) for w in ws]
    return pl.pallas_call(
        kern,
        grid=(N // tm,),
        in_specs=in_specs,
        out_specs=pl.BlockSpec((tm, D), lambda i: (i, 0)),
        out_shape=jax.ShapeDtypeStruct((N, D), F32),
        compiler_params=_params(("parallel",)),
        name="out_proj_residual",
    )(h2, *outs, *ws)


def _mlp_kernel(h_ref, g_ref, wu_ref, wd_ref, o_ref, xn_sc):
    f = pl.program_id(1)

    @pl.when(f == 0)
    def _():
        x = h_ref[...]
        ms = jnp.mean(x * x, axis=-1, keepdims=True)
        xn_sc[...] = ((x * lax.rsqrt(ms + NORM_EPS)) * g_ref[...]).astype(BF16)
        o_ref[...] = x

    u = _dot(xn_sc[...], wu_ref[...])
    a = jnp.square(jnp.maximum(u, 0.0)).astype(BF16)
    o_ref[...] += _dot(a, wd_ref[...])


def _mlp(h2, g, w_up, w_down, tm=512, tf=1024):
    N, D = h2.shape
    F = w_up.shape[1]
    tm = min(tm, N)
    return pl.pallas_call(
        _mlp_kernel,
        grid=(N // tm, F // tf),
        in_specs=[
            pl.BlockSpec((tm, D), lambda i, f: (i, 0)),
            pl.BlockSpec((1, D), lambda i, f: (0, 0)),
            pl.BlockSpec((D, tf), lambda i, f: (0, f)),
            pl.BlockSpec((tf, D), lambda i, f: (f, 0)),
        ],
        out_specs=pl.BlockSpec((tm, D), lambda i, f: (i, 0)),
        out_shape=jax.ShapeDtypeStruct((N, D), F32),
        scratch_shapes=[pltpu.VMEM((tm, D), BF16)],
        compiler_params=_params(("parallel", "arbitrary")),
        name="sq_relu_mlp",
    )(h2, g.reshape(1, D), w_up, w_down)


def _rmsnorm_kernel(x_ref, g_ref, o_ref):
    x = x_ref[...]
    ms = jnp.mean(x * x, axis=-1, keepdims=True)
    o_ref[...] = (x * lax.rsqrt(ms + NORM_EPS)) * g_ref[...]


def _rmsnorm(h2, g, tm=512):
    N, D = h2.shape
    tm = min(tm, N)
    return pl.pallas_call(
        _rmsnorm_kernel,
        grid=(N // tm,),
        in_specs=[pl.BlockSpec((tm, D), lambda i: (i, 0)), pl.BlockSpec((1, D), lambda i: (0, 0))],
        out_specs=pl.BlockSpec((tm, D), lambda i: (i, 0)),
        out_shape=jax.ShapeDtypeStruct((N, D), F32),
        compiler_params=_params(("parallel",)),
        name="final_rmsnorm",
    )(h2, g.reshape(1, D))


def _even_layout():
    offs = {}
    o = 0
    for name, size in (("qa", 512), ("ka", 64), ("va", 64), ("qi", 256), ("ki", 64), ("wi", 4),
                       ("qb", 512), ("kvb", 768), ("gb", 24)):
        offs[name] = o
        o += size
    kvb = lambda which, g: offs["kvb"] + (which * B_KV_GROUPS + g) * HEAD_DIM
    cols, plan, lay = [], [], {}
    n = {k: 0 for k in _KINDS}

    def add(c, roped, kind):
        cols.append(c)
        plan.append((roped, kind, n[kind]))
        n[kind] += 1
        return n[kind] - 1

    lay["qA"] = n["qT"]
    for p in range(4):
        add(_pair_cols(offs["qa"] + 2 * p * 64, offs["qa"] + (2 * p + 1) * 64), True, "qT")
    lay["qB"] = n["qT"]
    for p in range(4):
        add(_pair_cols(offs["qb"] + 2 * p * 64, offs["qb"] + (2 * p + 1) * 64), True, "qT")
    lay["qi"] = n["qT"]
    for p in range(2):
        add(_pair_cols(offs["qi"] + 2 * p * 64, offs["qi"] + (2 * p + 1) * 64), True, "qT")
    lay["kA"] = add(_pair_cols(offs["ka"], offs["ka"]), True, "k")
    lay["ki"] = add(_pair_cols(offs["ki"], offs["ki"]), True, "k")
    lay["ks"] = n["k"]
    for g in range(2):
        add(_pair_cols(kvb(2, g), kvb(2, g)), True, "k")
    lay["kw"] = n["k"]
    for g in range(2):
        add(_pair_cols(kvb(4, g), kvb(4, g)), True, "k")
    lay["kc"] = add(_pair_cols(kvb(0, 0), kvb(0, 1)), True, "f")
    lay["vc"] = add(_pair_cols(kvb(1, 0), kvb(1, 1)), False, "f")
    misc = np.full(LANES, -1)
    misc[0:IDX_HEADS] = offs["wi"] + np.arange(IDX_HEADS)
    lay["gate_row0"] = 8
    misc[8:8 + 24] = offs["gb"] + np.arange(24)
    lay["misc"] = add(misc, False, "fT")
    lay["vw"] = n["k"]
    for g in range(2):
        add(_dup_cols(kvb(5, g)), False, "k")
    lay["vA"] = add(_dup_cols(offs["va"]), False, "vT")
    lay["vs"] = n["vT"]
    for g in range(2):
        add(_dup_cols(kvb(3, g)), False, "vT")
    return np.concatenate(cols), plan, lay


def _odd_layout():
    cols, plan = [], []
    for h in range(C_HEADS):
        cols.append(_pair_cols(h * 128, h * 128 + 64))
        plan.append((True, "qT", h))
    for h in range(C_HEADS):
        cols.append(_pair_cols(1024 + h * 128, 1024 + h * 128 + 64))
        plan.append((True, "k", h))
    for h in range(C_HEADS):
        cols.append(2048 + h * 128 + np.arange(LANES))
        plan.append((False, "vT", h))
    return np.concatenate(cols), plan


def _compress_weights(pe, w1, w2):
    d = _PAIR_D
    which = _PAIR_WHICH
    pe_l = pe[:, :, d]
    pe_l = pe_l.reshape(2, 2, 1, CMP_STRIDE * LANES)
    w1r = w1.reshape(2, CMP_BLOCK, HEAD_DIM, CMP_HIDDEN)[:, :, d, :]
    per_g = []
    for g in range(B_KV_GROUPS):
        keep = jnp.asarray(which == g)[None, None, :, None]
        per_g.append(jnp.where(keep, w1r, 0.0))
    w1g = jnp.stack(per_g, axis=1)
    w1g = w1g.reshape(2, B_KV_GROUPS, 2, CMP_STRIDE * LANES, CMP_HIDDEN).astype(BF16)
    w2k = w2[0][:, d]
    w2v = w2[1][:, np.arange(LANES) % HEAD_DIM]
    w2l = jnp.stack([w2k, w2v], axis=0).astype(BF16)
    return pe_l, w1g, w2l


def _even_mixer(h, norm_g, w_in, cmp_pe, cmp_w1, cmp_w2, w_out, cos_slab, sin_slab):
    B, T, D = h.shape
    cols, plan, lay = _even_layout()
    w = _gather_cols(w_in, cols).astype(BF16)
    sl = _project(h, norm_g, w, cos_slab, sin_slab, plan)

    bias = _dsa_select(sl, lay)
    o_a = _dsa_attention(sl, bias, lay)

    n_ch = T // CMP_STRIDE
    flat = sl["f"][:, lay["kc"]:lay["vc"] + 1].reshape(B, 2, n_ch, CMP_STRIDE * LANES)
    flat = jnp.swapaxes(flat, 0, 1)
    pe_l, w1g, w2l = _compress_weights(cmp_pe, cmp_w1, cmp_w2)
    cmp_k, cmp_vt = _compress(flat, pe_l, w1g, w2l)
    o_b = _nsa_attention(sl, cmp_k, cmp_vt, lay)

    na = A_HEADS * HEAD_DIM
    wo = w_out.astype(BF16)
    h2 = _out_proj(h.reshape(B * T, D), [o_a.reshape(B * T, -1), o_b.reshape(B * T, -1)],
                   [wo[:na], wo[na:]])
    return h2.reshape(B, T, D)


def _odd_mixer(h, norm_g, w_in, lam, subln_g, w_out, cos_slab, sin_slab, lambda_init):
    B, T, D = h.shape
    cols, plan = _odd_layout()
    w = _gather_cols(w_in, cols).astype(BF16)
    sl = _project(h, norm_g, w, cos_slab, sin_slab, plan)
    o = _diff_attention(sl, lam, subln_g, lambda_init)
    h2 = _out_proj(h.reshape(B * T, D), [o.reshape(B * T, -1)], [w_out.astype(BF16)])
    return h2.reshape(B, T, D)


def kernel(x, mix_norm_g, mlp_norm_g, even_w_in, even_cmp_pe, even_cmp_w1, even_cmp_w2, even_w_out, odd_w_in, odd_lambda, odd_subln_g, odd_w_out, mlp_w_up, mlp_w_down, final_norm_g):
    B, T, D = x.shape
    depth = mix_norm_g.shape[0]
    cos_slab, sin_slab = _rope_slabs(T)
    h = x
    for layer in range(depth):
        if layer % 2 == 0:
            e = layer // 2
            h = _even_mixer(h, mix_norm_g[layer], even_w_in[e], even_cmp_pe[e], even_cmp_w1[e],
                            even_cmp_w2[e], even_w_out[e], cos_slab, sin_slab)
        else:
            o = layer // 2
            lambda_init = 0.8 - 0.6 * math.exp(-0.3 * layer)
            h = _odd_mixer(h, mix_norm_g[layer], odd_w_in[o], odd_lambda[o], odd_subln_g[o],
                           odd_w_out[o], cos_slab, sin_slab, lambda_init)
        h2 = _mlp(h.reshape(B * T, D), mlp_norm_g[layer], mlp_w_up[layer].astype(BF16),
                  mlp_w_down[layer].astype(BF16))
        h = h2.reshape(B, T, D)
    return _rmsnorm(h.reshape(B * T, D), final_norm_g).reshape(B, T, D)
```

```python
import functools
import math

import numpy as np
import jax
import jax.numpy as jnp
from jax import lax
from jax.experimental import pallas as pl
from jax.experimental.pallas import tpu as pltpu

HEAD_DIM = 64
HALF = HEAD_DIM // 2
LANES = 128
ROPE_THETA = 10000.0
NORM_EPS = 1e-6
SCALE = HEAD_DIM ** -0.5

A_HEADS = 8
IDX_HEADS = 4
DSA_TOPK = 256
B_HEADS = 8
B_KV_GROUPS = 2
B_PER_GROUP = B_HEADS // B_KV_GROUPS
CMP_BLOCK = 32
CMP_STRIDE = 16
CMP_HIDDEN = 256
SLC_BLOCK = 64
SLC_TOPN = 16
WINDOW = 512
C_HEADS = 8

KV_TILE = 512
Q_TILE = 128

NEG_INF = float("-inf")
M_FLOOR = -1e30
INT_MIN = -(2 ** 31)

VMEM_LIMIT = 56 * 1024 * 1024

BF16 = jnp.bfloat16
F32 = jnp.float32


def _dot(a, b):
    return jnp.dot(a, b, preferred_element_type=F32)


def _dot_tn(a, b):
    return lax.dot_general(a, b, (((0,), (0,)), ((), ())), preferred_element_type=F32)


def _params(sem):
    return pltpu.CompilerParams(dimension_semantics=sem, vmem_limit_bytes=VMEM_LIMIT)


def _iota(shape, axis):
    return lax.broadcasted_iota(jnp.int32, shape, axis)


def _pair_cols(base_a, base_b):
    lane = np.arange(LANES)
    half = lane // 64
    which = (lane % 64) // HALF
    i = lane % HALF
    base = np.where(which == 0, base_a, base_b)
    return base + half * HALF + i


def _dup_cols(base):
    return base + np.arange(LANES) % HEAD_DIM


_PAIR_D = _pair_cols(0, 0)
_PAIR_WHICH = (np.arange(LANES) % 64) // HALF


def _gather_cols(w, cols):
    cols = np.asarray(cols)
    safe = np.where(cols >= 0, cols, 0)
    g = jnp.take(w, jnp.asarray(safe, dtype=jnp.int32), axis=1)
    return jnp.where(jnp.asarray(cols >= 0)[None, :], g, 0.0)


def _rope_slabs(T):
    inv = 1.0 / (ROPE_THETA ** (jnp.arange(0, HEAD_DIM, 2, dtype=F32) / HEAD_DIM))
    ang = jnp.arange(T, dtype=F32)[:, None] * inv[None, :]
    cos, sin = jnp.cos(ang), jnp.sin(ang)
    cos_slab = jnp.tile(cos, (1, 4))
    sin_slab = jnp.concatenate([-sin, -sin, sin, sin], axis=1)
    return cos_slab, sin_slab


_KINDS = ("qT", "k", "vT", "f", "fT")


def _proj_kernel(x_ref, g_ref, w_ref, cos_ref, sin_ref, *out_refs, plan, kinds, chunk):
    outs = dict(zip(kinds, out_refs))
    x = x_ref[0]
    ms = jnp.mean(x * x, axis=-1, keepdims=True)
    xn = ((x * lax.rsqrt(ms + NORM_EPS)) * g_ref[...]).astype(BF16)
    cos = cos_ref[...]
    sin = sin_ref[...]
    n = len(plan)
    for c0 in range(0, n, chunk):
        c1 = min(c0 + chunk, n)
        r = _dot(xn, w_ref[:, c0 * LANES:c1 * LANES])
        for s in range(c0, c1):
            roped, kind, idx = plan[s]
            y = r[:, (s - c0) * LANES:(s - c0 + 1) * LANES]
            if roped:
                y = y * cos + pltpu.roll(y, 64, 1) * sin
            if kind == "qT":
                outs[kind][0, idx] = y.T.astype(BF16)
            elif kind == "k":
                outs[kind][0, idx] = y.astype(BF16)
            elif kind == "vT":
                outs[kind][0, idx, 0] = y.T.astype(BF16)
            elif kind == "f":
                outs[kind][0, idx] = y
            else:
                outs[kind][0, idx] = y.T


def _project(x, g, w, cos_slab, sin_slab, plan, chunk=4):
    B, T, D = x.shape
    tm = KV_TILE
    assert T % tm == 0
    count = {k: sum(1 for p in plan if p[1] == k) for k in _KINDS}
    kinds = tuple(k for k in _KINDS if count[k])
    out_specs, out_shape = [], []
    for k in kinds:
        n = count[k]
        if k in ("qT", "fT"):
            out_specs.append(pl.BlockSpec((1, n, LANES, tm), lambda b, i: (b, 0, 0, i)))
            out_shape.append(jax.ShapeDtypeStruct((B, n, LANES, T), BF16 if k == "qT" else F32))
        elif k in ("k", "f"):
            out_specs.append(pl.BlockSpec((1, n, tm, LANES), lambda b, i: (b, 0, i, 0)))
            out_shape.append(jax.ShapeDtypeStruct((B, n, T, LANES), BF16 if k == "k" else F32))
        else:
            out_specs.append(pl.BlockSpec((1, n, 1, LANES, tm), lambda b, i: (b, 0, i, 0, 0)))
            out_shape.append(jax.ShapeDtypeStruct((B, n, T // tm, LANES, tm), BF16))
    kern = functools.partial(_proj_kernel, plan=tuple(plan), kinds=kinds, chunk=chunk)
    outs = pl.pallas_call(
        kern,
        grid=(B, T // tm),
        in_specs=[
            pl.BlockSpec((1, tm, D), lambda b, i: (b, i, 0)),
            pl.BlockSpec((1, D), lambda b, i: (0, 0)),
            pl.BlockSpec((D, len(plan) * LANES), lambda b, i: (0, 0)),
            pl.BlockSpec((tm, LANES), lambda b, i: (i, 0)),
            pl.BlockSpec((tm, LANES), lambda b, i: (i, 0)),
        ],
        out_specs=out_specs,
        out_shape=out_shape,
        compiler_params=_params(("parallel", "parallel")),
        name="norm_proj_rope",
    )(x, g.reshape(1, D), w, cos_slab, sin_slab)
    return dict(zip(kinds, outs))


def _head_rows(slab_t, which, scale=None):
    row = _iota(slab_t.shape, 0)
    keep = ((row % 64) // HALF) == which
    if scale is not None:
        slab_t = slab_t * scale
    return jnp.where(keep, slab_t, jnp.zeros_like(slab_t))


def _tree(op, xs):
    while len(xs) > 1:
        xs = [op(xs[i], xs[i + 1]) if i + 1 < len(xs) else xs[i] for i in range(0, len(xs), 2)]
    return xs[0]


def _fold_rows(op, x, ways=4):
    rows, n = x.shape
    per = rows // ways
    parts = [op(x[i * per:(i + 1) * per].reshape(per // 8, 8, n), axis=0) for i in range(ways)]
    return _tree(jnp.maximum if op is jnp.max else jnp.add, parts)


ROW_BLOCK = 64


def _online_update(s_sc, p_sc, v_t, m_ref, l_ref, acc_ref):
    tk, n = s_sc.shape
    m_prev = m_ref[...]
    m_tile = jnp.max(_fold_rows(jnp.max, s_sc[...]), axis=0, keepdims=True)
    m_new = jnp.maximum(m_prev, m_tile)
    alpha = jnp.exp(m_prev - m_new)
    sums = []
    for r in range(tk // ROW_BLOCK):
        rows = slice(r * ROW_BLOCK, (r + 1) * ROW_BLOCK)
        p = jnp.exp(s_sc[rows, :] - m_new)
        sums.append(jnp.sum(p.reshape(ROW_BLOCK // 8, 8, n), axis=0))
        p_sc[rows, :] = p.astype(BF16)
    l_tile = jnp.sum(_tree(jnp.add, sums), axis=0, keepdims=True)
    l_ref[...] = alpha * l_ref[...] + l_tile
    acc_ref[...] = alpha * acc_ref[...] + _dot(v_t, p_sc[...])
    m_ref[...] = m_new


def _flash_pipeline(n, put_scores, update, buf_a, buf_b):
    put_scores(buf_a, 0)

    def pair(ii, carry):
        j0 = 2 * ii
        put_scores(buf_b, jnp.minimum(j0 + 1, n - 1))
        update(buf_a, j0)

        @pl.when(j0 + 1 < n)
        def _():
            put_scores(buf_a, jnp.minimum(j0 + 2, n - 1))
            update(buf_b, j0 + 1)

        return carry

    lax.fori_loop(0, (n + 1) // 2, pair, 0)


def _softmax_keys(s):
    m = jnp.max(s, axis=0, keepdims=True)
    m = jnp.where(m > NEG_INF, m, 0.0)
    e = jnp.exp(s - m)
    return e / jnp.maximum(jnp.sum(e, axis=0, keepdims=True), 1e-30)


def _merge_pair_rows(even, odd):
    return jnp.where(_iota(even.shape, 0) < 64, even, odd)


def _init_flash(m_ref, l_ref, acc_ref):
    m_ref[...] = jnp.full(m_ref.shape, M_FLOOR, F32)
    l_ref[...] = jnp.zeros(l_ref.shape, F32)
    acc_ref[...] = jnp.zeros(acc_ref.shape, F32)


def _diff_attn_kernel(lam_ref, q_ref, k_ref, v_ref, g_ref, o_ref, m_sc, l_sc, acc_sc, sa_sc, sb_sc,
                      p_sc, *, tq, lambda_init):
    qi = pl.program_id(2)
    q_t = q_ref[0, 0]
    qs = [_head_rows(q_t, c, SCALE) for c in range(2)]
    for c in range(2):
        _init_flash(m_sc.at[c], l_sc.at[c], acc_sc.at[c])

    def put_scores(buf, j, masked=False):
        k = k_ref[0, 0, pl.ds(pl.multiple_of(j * tq, tq), tq), :]
        for c in range(2):
            s = _dot(k, qs[c])
            if masked:
                s = jnp.where(_iota(s.shape, 0) <= _iota(s.shape, 1), s, NEG_INF)
            buf[c] = s

    def update(buf, j):
        v_t = v_ref[0, 0, j]
        for c in range(2):
            _online_update(buf.at[c], p_sc.at[c], v_t, m_sc.at[c], l_sc.at[c], acc_sc.at[c])

    _flash_pipeline(qi, put_scores, update, sa_sc, sb_sc)
    put_scores(sa_sc, qi, masked=True)
    update(sa_sc, qi)

    lam = lam_ref[...]
    s01 = jnp.sum(lam[0:1] * lam[1:2], axis=-1, keepdims=True)
    s23 = jnp.sum(lam[2:3] * lam[3:4], axis=-1, keepdims=True)
    lam_val = jnp.exp(s01) - jnp.exp(s23) + lambda_init
    o0 = acc_sc[0] / jnp.maximum(l_sc[0], 1e-30)
    o1 = acc_sc[1] / jnp.maximum(l_sc[1], 1e-30)
    o = o0 - lam_val * o1
    y = o * lax.rsqrt(jnp.mean(o * o, axis=0, keepdims=True) + NORM_EPS)
    y = (y * g_ref[...]) * (1.0 - lambda_init)
    o_ref[0] = y.T.astype(o_ref.dtype)


def _diff_attention(sl, lam, subln_g, lambda_init):
    q_t, k, v_t = sl["qT"], sl["k"], sl["vT"]
    B, H, T, _ = k.shape
    tq = KV_TILE
    nk = T // tq
    kern = functools.partial(_diff_attn_kernel, tq=tq, lambda_init=lambda_init)
    return pl.pallas_call(
        kern,
        grid=(B, H, T // tq),
        in_specs=[
            pl.BlockSpec((4, HEAD_DIM), lambda b, h, i: (0, 0)),
            pl.BlockSpec((1, 1, LANES, tq), lambda b, h, i: (b, h, 0, i)),
            pl.BlockSpec((1, 1, T, LANES), lambda b, h, i: (b, h, 0, 0)),
            pl.BlockSpec((1, 1, nk, LANES, tq), lambda b, h, i: (b, h, 0, 0, 0)),
            pl.BlockSpec((LANES, 1), lambda b, h, i: (0, 0)),
        ],
        out_specs=pl.BlockSpec((1, tq, LANES), lambda b, h, i: (b, i, h)),
        out_shape=jax.ShapeDtypeStruct((B, T, H * LANES), BF16),
        scratch_shapes=[
            pltpu.VMEM((2, 1, tq), F32),
            pltpu.VMEM((2, 1, tq), F32),
            pltpu.VMEM((2, LANES, tq), F32),
            pltpu.VMEM((2, tq, tq), F32),
            pltpu.VMEM((2, tq, tq), F32),
            pltpu.VMEM((2, tq, tq), BF16),
        ],
        compiler_params=_params(("parallel", "parallel", "arbitrary")),
        name="diff_attention",
    )(lam, q_t, k, v_t, subln_g.reshape(LANES, 1))


def _dsa_select_kernel(qi_ref, ki_ref, misc_ref, tril_ref, bias_ref, key_sc, *, tq, ck, nk, topk):
    t0 = pl.program_id(1) * tq
    nvalid = (t0 + tq - 1) // ck + 1
    w = misc_ref[0, 0]
    qh = jnp.concatenate([_head_rows(qi_ref[0, h // 2], h % 2) for h in range(IDX_HEADS)], axis=1)
    t_q = t0 + _iota((ck, tq), 1)

    def causal(c):
        return (c * ck + _iota((ck, tq), 0)) <= t_q

    def fill(c, carry):
        kk = ki_ref[0, 0, pl.ds(pl.multiple_of(c * ck, ck), ck), :]
        r = _dot(kk, qh)
        score = jnp.zeros((ck, tq), F32)
        for h in range(IDX_HEADS):
            score = score + w[h:h + 1, :] * jnp.maximum(r[:, h * tq:(h + 1) * tq], 0.0)
        bits = pltpu.bitcast(score, jnp.int32)
        key = bits ^ ((bits >> 31) & jnp.int32(0x7FFFFFFF))
        key = jnp.where(score == 0.0, 0, key)
        key_sc[c] = jnp.where(causal(c), key, INT_MIN)
        return carry

    lax.fori_loop(0, nvalid, fill, 0)

    def count(pred):
        def chunk(c, acc):
            ind = pred(key_sc[c], c)
            return acc + jnp.sum(ind.reshape(8, ck // 8, tq), axis=0)
        acc = lax.fori_loop(0, nvalid, chunk, jnp.zeros((ck // 8, tq), F32))
        return jnp.sum(acc, axis=0, keepdims=True)

    kf = float(topk)
    zero = jnp.zeros((1, tq), jnp.int32)
    ans = jnp.where(count(lambda kc, c: jnp.where(kc >= zero, 1.0, 0.0)) >= kf, 0, INT_MIN)

    def bit_step(i, ans):
        cand = ans | (jnp.int32(1) << (30 - i))
        cnt = count(lambda kc, c: jnp.where(kc >= cand, 1.0, 0.0))
        return jnp.where(cnt >= kf, cand, ans)

    ans = lax.fori_loop(0, 31, bit_step, ans)

    cnt_gt = count(lambda kc, c: jnp.where(kc > ans, 1.0, 0.0))
    cnt_eq = count(lambda kc, c: jnp.where(kc == ans, jnp.where(causal(c), 1.0, 0.0), 0.0))
    need = kf - cnt_gt
    has_tie = jnp.max(cnt_eq - need) > 0.0

    @pl.when(jnp.logical_not(has_tie))
    def _():
        def emit(c, carry):
            sel = jnp.where(key_sc[c] >= ans, jnp.where(causal(c), 0.0, NEG_INF), NEG_INF)
            bias_ref[0, 0, c] = sel.astype(BF16)
            return carry
        lax.fori_loop(0, nvalid, emit, 0)

    @pl.when(has_tie)
    def _():
        def emit(c, carry):
            kc = key_sc[c]
            eq = jnp.where(kc == ans, jnp.where(causal(c), 1.0, 0.0), 0.0)
            pre = _dot(tril_ref[...], eq.astype(BF16)) + carry
            take = jnp.where(kc > ans, 1.0, jnp.where(pre < need, eq, 0.0))
            sel = jnp.where(take > 0.5, jnp.where(causal(c), 0.0, NEG_INF), NEG_INF)
            bias_ref[0, 0, c] = sel.astype(BF16)
            return carry + jnp.sum(eq, axis=0, keepdims=True)
        lax.fori_loop(0, nvalid, emit, jnp.zeros((1, tq), F32))

    def blank(c, carry):
        bias_ref[0, 0, c] = jnp.full((ck, tq), NEG_INF, BF16)
        return carry

    lax.fori_loop(nvalid, nk, blank, 0)


def _dsa_select(sl, lay):
    q_t, k, misc_t = sl["qT"], sl["k"], sl["fT"]
    B, _, T, _ = k.shape
    tq, ck = Q_TILE, KV_TILE
    nk = T // ck
    topk = min(DSA_TOPK, T // 4)
    tril = jnp.asarray(np.tril(np.ones((ck, ck), np.float32), -1), BF16)
    kern = functools.partial(_dsa_select_kernel, tq=tq, ck=ck, nk=nk, topk=topk)
    return pl.pallas_call(
        kern,
        grid=(B, T // tq),
        in_specs=[
            pl.BlockSpec((1, 2, LANES, tq), lambda b, i: (b, lay["qi"] // 2, 0, i)),
            pl.BlockSpec((1, 1, T, LANES), lambda b, i: (b, lay["ki"], 0, 0)),
            pl.BlockSpec((1, 1, LANES, tq), lambda b, i: (b, lay["misc"], 0, i)),
            pl.BlockSpec((ck, ck), lambda b, i: (0, 0)),
        ],
        out_specs=pl.BlockSpec((1, 1, nk, ck, tq), lambda b, i: (b, i, 0, 0, 0)),
        out_shape=jax.ShapeDtypeStruct((B, T // tq, nk, ck, tq), BF16),
        scratch_shapes=[pltpu.VMEM((nk, ck, tq), jnp.int32)],
        compiler_params=_params(("parallel", "parallel")),
        name="dsa_select",
    )(q_t, k, misc_t, tril)


def _store_head_pairs(o_ref, o, n_heads, tq):
    for p in range(n_heads // 2):
        even = o[:, (2 * p) * tq:(2 * p + 1) * tq]
        odd = o[:, (2 * p + 1) * tq:(2 * p + 2) * tq]
        o_ref[0, :, p * LANES:(p + 1) * LANES] = _merge_pair_rows(even, odd).T.astype(o_ref.dtype)


def _dsa_attn_kernel(q_ref, k_ref, v_ref, bias_ref, o_ref, qst, m_sc, l_sc, acc_sc, sa_sc, sb_sc,
                     p_sc, *, tq, tk):
    qi = pl.program_id(1)
    H = A_HEADS
    for h in range(H):
        qst[:, h * tq:(h + 1) * tq] = _head_rows(q_ref[0, h // 2], h % 2, SCALE)
    _init_flash(m_sc, l_sc, acc_sc)
    nkv = (qi * tq) // tk + 1

    def put_scores(buf, j):
        k = k_ref[0, 0, pl.ds(pl.multiple_of(j * tk, tk), tk), :]
        b = bias_ref[0, 0, j].astype(F32)
        buf[...] = _dot(k, qst[...]) + jnp.concatenate([b] * H, axis=1)

    def update(buf, j):
        _online_update(buf, p_sc, v_ref[0, 0, j], m_sc, l_sc, acc_sc)

    _flash_pipeline(nkv, put_scores, update, sa_sc, sb_sc)
    o = acc_sc[...] / jnp.maximum(l_sc[...], 1e-30)
    _store_head_pairs(o_ref, o, H, tq)


def _dsa_attention(sl, bias, lay):
    q_t, k, v_t = sl["qT"], sl["k"], sl["vT"]
    B, _, T, _ = k.shape
    tq, tk = Q_TILE, KV_TILE
    nk = T // tk
    H = A_HEADS
    N = H * tq
    kern = functools.partial(_dsa_attn_kernel, tq=tq, tk=tk)
    return pl.pallas_call(
        kern,
        grid=(B, T // tq),
        in_specs=[
            pl.BlockSpec((1, H // 2, LANES, tq), lambda b, i: (b, lay["qA"] // (H // 2), 0, i)),
            pl.BlockSpec((1, 1, T, LANES), lambda b, i: (b, lay["kA"], 0, 0)),
            pl.BlockSpec((1, 1, nk, LANES, tk), lambda b, i: (b, lay["vA"], 0, 0, 0)),
            pl.BlockSpec((1, 1, nk, tk, tq), lambda b, i: (b, i, 0, 0, 0)),
        ],
        out_specs=pl.BlockSpec((1, tq, H * HEAD_DIM), lambda b, i: (b, i, 0)),
        out_shape=jax.ShapeDtypeStruct((B, T, H * HEAD_DIM), BF16),
        scratch_shapes=[
            pltpu.VMEM((LANES, N), BF16),
            pltpu.VMEM((1, N), F32),
            pltpu.VMEM((1, N), F32),
            pltpu.VMEM((LANES, N), F32),
            pltpu.VMEM((tk, N), F32),
            pltpu.VMEM((tk, N), F32),
            pltpu.VMEM((tk, N), BF16),
        ],
        compiler_params=_params(("parallel", "arbitrary")),
        name="dsa_attention",
    )(q_t, k, v_t, bias)


def _compress_kernel(x_ref, pe_ref, w1_ref, w2_ref, o_ref, ot_ref, *, n_ch):
    x = x_ref[0, 0]
    xt = (x + pe_ref[0, 0]).astype(BF16)
    xb = (x + pe_ref[0, 1]).astype(BF16)
    for g in range(B_KV_GROUPS):
        a = _dot(xt, w1_ref[0, g, 0])
        b = _dot(xb, w1_ref[0, g, 1])
        h = jax.nn.gelu(a + pltpu.roll(b, n_ch - 1, 0))
        r = _dot(h.astype(BF16), w2_ref[0])
        o_ref[0, 0, g] = r.astype(BF16)
        ot_ref[0, 0, g] = r.T.astype(BF16)


def _compress(flat, pe, w1, w2):
    _, B, n_ch, W = flat.shape
    G = B_KV_GROUPS
    kern = functools.partial(_compress_kernel, n_ch=n_ch)
    return pl.pallas_call(
        kern,
        grid=(2, B),
        in_specs=[
            pl.BlockSpec((1, 1, n_ch, W), lambda s, b: (s, b, 0, 0)),
            pl.BlockSpec((1, 2, 1, W), lambda s, b: (s, 0, 0, 0)),
            pl.BlockSpec((1, G, 2, W, CMP_HIDDEN), lambda s, b: (s, 0, 0, 0, 0)),
            pl.BlockSpec((1, CMP_HIDDEN, LANES), lambda s, b: (s, 0, 0)),
        ],
        out_specs=[
            pl.BlockSpec((1, 1, G, n_ch, LANES), lambda s, b: (s, b, 0, 0, 0)),
            pl.BlockSpec((1, 1, G, LANES, n_ch), lambda s, b: (s, b, 0, 0, 0)),
        ],
        out_shape=[
            jax.ShapeDtypeStruct((2, B, G, n_ch, LANES), BF16),
            jax.ShapeDtypeStruct((2, B, G, LANES, n_ch), BF16),
        ],
        compiler_params=_params(("parallel", "parallel")),
        name="nsa_compress",
    )(flat, pe, w1, w2)


def _nsa_kernel(q_ref, kc_ref, vc_ref, ks_ref, vs_ref, kw_ref, vw_ref, misc_ref, ov_ref, ex_ref,
                o_ref, qst, m_sc, l_sc, acc_sc, sa_sc, sb_sc, p_sc, *, tq, tk, T, n_s, n_sel,
                gate_row0):
    g = pl.program_id(1)
    qi = pl.program_id(2)
    t0 = qi * tq
    J = B_PER_GROUP
    N = J * tq
    for j in range(J):
        qst[:, j * tq:(j + 1) * tq] = _head_rows(q_ref[0, j // 2], j % 2, SCALE)
    q = qst[...]

    def q_time(shape):
        return t0 + (_iota(shape, 1) % tq)

    kc = kc_ref[0, 0, 0]
    n_ch = kc.shape[0]
    s_c = _dot(kc, q)
    cmp_end = _iota((n_ch, N), 0) * CMP_STRIDE + (CMP_BLOCK - 1)
    s_c = jnp.where(cmp_end <= q_time((n_ch, N)), s_c, NEG_INF)
    p_c = _softmax_keys(s_c)
    o_c = _dot(vc_ref[0, 0, 0], p_c.astype(BF16))

    psum = p_c[:, 0:tq]
    for j in range(1, J):
        psum = psum + p_c[:, j * tq:(j + 1) * tq]
    p_hi = psum.astype(BF16)
    p_lo = (psum - p_hi.astype(F32)).astype(BF16)
    imp = _dot(ov_ref[...], p_hi) + _dot(ov_ref[...], p_lo)
    rows = -(-n_s // 8) * 8
    imp = imp[0:rows]
    blk = _iota((rows, tq), 0)
    t_q = t0 + _iota((rows, tq), 1)
    cur = t_q // SLC_BLOCK
    forced = (blk == 0) | (blk == cur) | (blk == cur - 1)
    sc = jnp.where(forced, jnp.inf, imp)
    sc = jnp.where(blk * SLC_BLOCK <= t_q, sc, NEG_INF)
    rank = jnp.zeros((rows, tq), F32)
    for m in range(n_s):
        cm = sc[m:m + 1, :]
        beats = jnp.where(cm > sc, 1.0, jnp.where(cm == sc, jnp.where(blk > m, 1.0, 0.0), 0.0))
        rank = rank + beats
    sel = jnp.where(rank < float(n_sel), 1.0, 0.0)
    if rows < LANES:
        sel = jnp.concatenate([sel, jnp.zeros((LANES - rows, tq), F32)], axis=0)
    sel = sel.astype(BF16)

    _init_flash(m_sc, l_sc, acc_sc)
    nkv = t0 // tk + 1

    def put_scores(buf, j):
        start = pl.multiple_of(j * tk, tk)
        k = ks_ref[0, 0, pl.ds(start, tk), :]
        hit = _dot(ex_ref[j], sel)
        pos = start + _iota((tk, tq), 0)
        tt = t0 + _iota((tk, tq), 1)
        bias = jnp.where(hit > 0.5, jnp.where(pos <= tt, 0.0, NEG_INF), NEG_INF)
        buf[...] = _dot(k, qst[...]) + jnp.concatenate([bias] * J, axis=1)

    def update(buf, j):
        _online_update(buf, p_sc, vs_ref[0, 0, j], m_sc, l_sc, acc_sc)

    _flash_pipeline(nkv, put_scores, update, sa_sc, sb_sc)
    o_s = acc_sc[...] / jnp.maximum(l_sc[...], 1e-30)

    wlen = min(WINDOW + tq, T)
    wstart = pl.multiple_of(jnp.maximum(t0 - WINDOW, 0), tq)
    kw = kw_ref[0, 0, pl.ds(wstart, wlen), :]
    vw = vw_ref[0, 0, pl.ds(wstart, wlen), :]
    s_w = _dot(kw, q)
    pos = wstart + _iota((wlen, N), 0)
    tt = q_time((wlen, N))
    s_w = jnp.where(pos <= tt, jnp.where(pos > tt - WINDOW, s_w, NEG_INF), NEG_INF)
    o_w = _dot_tn(vw, _softmax_keys(s_w).astype(BF16))

    gates = jax.nn.sigmoid(misc_ref[0, 0])

    def gate_row(c):
        parts = []
        for j in range(J):
            r0 = gate_row0 + j * 3 + c
            r1 = gate_row0 + (J + j) * 3 + c
            parts.append(jnp.where(g == 0, gates[r0:r0 + 1, :], gates[r1:r1 + 1, :]))
        return jnp.concatenate(parts, axis=1)

    o = gate_row(0) * o_c + (gate_row(1) * o_s + gate_row(2) * o_w)
    _store_head_pairs(o_ref, o, J, tq)


def _nsa_attention(sl, cmp_k, cmp_vt, lay):
    q_t, k, v_t, misc_t = sl["qT"], sl["k"], sl["vT"], sl["fT"]
    B, _, T, _ = k.shape
    G, J = B_KV_GROUPS, B_PER_GROUP
    tq, tk = Q_TILE, KV_TILE
    nk = T // tk
    n_ch = T // CMP_STRIDE
    n_c = n_ch - CMP_BLOCK // CMP_STRIDE + 1
    n_s = T // SLC_BLOCK
    n_sel = min(SLC_TOPN, n_s)
    assert n_s <= LANES
    c0 = np.arange(n_ch) * CMP_STRIDE
    s0 = np.arange(LANES) * SLC_BLOCK
    ov = ((c0[None, :] < s0[:, None] + SLC_BLOCK) & (c0[None, :] + CMP_BLOCK > s0[:, None]))
    ov = ov & (np.arange(n_ch)[None, :] < n_c) & (np.arange(LANES)[:, None] < n_s)
    ov = jnp.asarray(ov.astype(np.float32), BF16)
    pos = np.arange(T).reshape(nk, tk, 1)
    ex = (pos // SLC_BLOCK == np.arange(LANES).reshape(1, 1, LANES))
    ex = jnp.asarray(ex.astype(np.float32), BF16)
    kern = functools.partial(_nsa_kernel, tq=tq, tk=tk, T=T, n_s=n_s, n_sel=n_sel,
                             gate_row0=lay["gate_row0"])
    N = J * tq
    kslab = lambda off: pl.BlockSpec((1, 1, T, LANES), lambda b, g, i: (b, off + g, 0, 0))
    return pl.pallas_call(
        kern,
        grid=(B, G, T // tq),
        in_specs=[
            pl.BlockSpec((1, 2, LANES, tq), lambda b, g, i: (b, lay["qB"] // 2 + g, 0, i)),
            pl.BlockSpec((1, 1, 1, n_ch, LANES), lambda b, g, i: (0, b, g, 0, 0)),
            pl.BlockSpec((1, 1, 1, LANES, n_ch), lambda b, g, i: (1, b, g, 0, 0)),
            kslab(lay["ks"]),
            pl.BlockSpec((1, 1, nk, LANES, tk), lambda b, g, i: (b, lay["vs"] + g, 0, 0, 0)),
            kslab(lay["kw"]),
            kslab(lay["vw"]),
            pl.BlockSpec((1, 1, LANES, tq), lambda b, g, i: (b, lay["misc"], 0, i)),
            pl.BlockSpec((LANES, n_ch), lambda b, g, i: (0, 0)),
            pl.BlockSpec((nk, tk, LANES), lambda b, g, i: (0, 0, 0)),
        ],
        out_specs=pl.BlockSpec((1, tq, J * HEAD_DIM), lambda b, g, i: (b, i, g)),
        out_shape=jax.ShapeDtypeStruct((B, T, B_HEADS * HEAD_DIM), BF16),
        scratch_shapes=[
            pltpu.VMEM((LANES, N), BF16),
            pltpu.VMEM((1, N), F32),
            pltpu.VMEM((1, N), F32),
            pltpu.VMEM((LANES, N), F32),
            pltpu.VMEM((tk, N), F32),
            pltpu.VMEM((tk, N), F32),
            pltpu.VMEM((tk, N), BF16),
        ],
        compiler_params=_params(("parallel", "parallel", "arbitrary")),
        name="nsa_attention",
    )(q_t, cmp_k, cmp_vt, k, v_t, k, k, misc_t, ov, ex)


def _out_proj_kernel(*refs, n_in):
    h_ref = refs[0]
    o_refs = refs[1:1 + n_in]
    w_refs = refs[1 + n_in:1 + 2 * n_in]
    out_ref = refs[1 + 2 * n_in]
    acc = _dot(o_refs[0][...], w_refs[0][...])
    for i in range(1, n_in):
        acc = acc + _dot(o_refs[i][...], w_refs[i][...])
    out_ref[...] = h_ref[...] + acc


def _out_proj(h2, outs, ws, tm=512):
    N, D = h2.shape
    tm = min(tm, N)
    n_in = len(outs)
    kern = functools.partial(_out_proj_kernel, n_in=n_in)
    in_specs = [pl.BlockSpec((tm, D), lambda i: (i, 0))]
    in_specs += [pl.BlockSpec((tm, o.shape[1]), lambda i: (i, 0)) for o in outs]
    in_specs += [pl.BlockSpec(w.shape, lambda i: (0, 0)) for w in ws]
    return pl.pallas_call(
        kern,
        grid=(N // tm,),
        in_specs=in_specs,
        out_specs=pl.BlockSpec((tm, D), lambda i: (i, 0)),
        out_shape=jax.ShapeDtypeStruct((N, D), F32),
        compiler_params=_params(("parallel",)),
        name="out_proj_residual",
    )(h2, *outs, *ws)


def _mlp_kernel(h_ref, g_ref, wu_ref, wd_ref, o_ref, xn_sc):
    f = pl.program_id(1)

    @pl.when(f == 0)
    def _():
        x = h_ref[...]
        ms = jnp.mean(x * x, axis=-1, keepdims=True)
        xn_sc[...] = ((x * lax.rsqrt(ms + NORM_EPS)) * g_ref[...]).astype(BF16)
        o_ref[...] = x

    u = _dot(xn_sc[...], wu_ref[...])
    a = jnp.square(jnp.maximum(u, 0.0)).astype(BF16)
    o_ref[...] += _dot(a, wd_ref[...])


def _mlp(h2, g, w_up, w_down, tm=512, tf=1024):
    N, D = h2.shape
    F = w_up.shape[1]
    tm = min(tm, N)
    return pl.pallas_call(
        _mlp_kernel,
        grid=(N // tm, F // tf),
        in_specs=[
            pl.BlockSpec((tm, D), lambda i, f: (i, 0)),
            pl.BlockSpec((1, D), lambda i, f: (0, 0)),
            pl.BlockSpec((D, tf), lambda i, f: (0, f)),
            pl.BlockSpec((tf, D), lambda i, f: (f, 0)),
        ],
        out_specs=pl.BlockSpec((tm, D), lambda i, f: (i, 0)),
        out_shape=jax.ShapeDtypeStruct((N, D), F32),
        scratch_shapes=[pltpu.VMEM((tm, D), BF16)],
        compiler_params=_params(("parallel", "arbitrary")),
        name="sq_relu_mlp",
    )(h2, g.reshape(1, D), w_up, w_down)


def _rmsnorm_kernel(x_ref, g_ref, o_ref):
    x = x_ref[...]
    ms = jnp.mean(x * x, axis=-1, keepdims=True)
    o_ref[...] = (x * lax.rsqrt(ms + NORM_EPS)) * g_ref[...]


def _rmsnorm(h2, g, tm=512):
    N, D = h2.shape
    tm = min(tm, N)
    return pl.pallas_call(
        _rmsnorm_kernel,
        grid=(N // tm,),
        in_specs=[pl.BlockSpec((tm, D), lambda i: (i, 0)), pl.BlockSpec((1, D), lambda i: (0, 0))],
        out_specs=pl.BlockSpec((tm, D), lambda i: (i, 0)),
        out_shape=jax.ShapeDtypeStruct((N, D), F32),
        compiler_params=_params(("parallel",)),
        name="final_rmsnorm",
    )(h2, g.reshape(1, D))


def _even_layout():
    offs = {}
    o = 0
    for name, size in (("qa", 512), ("ka", 64), ("va", 64), ("qi", 256), ("ki", 64), ("wi", 4),
                       ("qb", 512), ("kvb", 768), ("gb", 24)):
        offs[name] = o
        o += size
    kvb = lambda which, g: offs["kvb"] + (which * B_KV_GROUPS + g) * HEAD_DIM
    cols, plan, lay = [], [], {}
    n = {k: 0 for k in _KINDS}

    def add(c, roped, kind):
        cols.append(c)
        plan.append((roped, kind, n[kind]))
        n[kind] += 1
        return n[kind] - 1

    lay["qA"] = n["qT"]
    for p in range(4):
        add(_pair_cols(offs["qa"] + 2 * p * 64, offs["qa"] + (2 * p + 1) * 64), True, "qT")
    lay["qB"] = n["qT"]
    for p in range(4):
        add(_pair_cols(offs["qb"] + 2 * p * 64, offs["qb"] + (2 * p + 1) * 64), True, "qT")
    lay["qi"] = n["qT"]
    for p in range(2):
        add(_pair_cols(offs["qi"] + 2 * p * 64, offs["qi"] + (2 * p + 1) * 64), True, "qT")
    lay["kA"] = add(_pair_cols(offs["ka"], offs["ka"]), True, "k")
    lay["ki"] = add(_pair_cols(offs["ki"], offs["ki"]), True, "k")
    lay["ks"] = n["k"]
    for g in range(2):
        add(_pair_cols(kvb(2, g), kvb(2, g)), True, "k")
    lay["kw"] = n["k"]
    for g in range(2):
        add(_pair_cols(kvb(4, g), kvb(4, g)), True, "k")
    lay["kc"] = add(_pair_cols(kvb(0, 0), kvb(0, 1)), True, "f")
    lay["vc"] = add(_pair_cols(kvb(1, 0), kvb(1, 1)), False, "f")
    misc = np.full(LANES, -1)
    misc[0:IDX_HEADS] = offs["wi"] + np.arange(IDX_HEADS)
    lay["gate_row0"] = 8
    misc[8:8 + 24] = offs["gb"] + np.arange(24)
    lay["misc"] = add(misc, False, "fT")
    lay["vw"] = n["k"]
    for g in range(2):
        add(_dup_cols(kvb(5, g)), False, "k")
    lay["vA"] = add(_dup_cols(offs["va"]), False, "vT")
    lay["vs"] = n["vT"]
    for g in range(2):
        add(_dup_cols(kvb(3, g)), False, "vT")
    return np.concatenate(cols), plan, lay


def _odd_layout():
    cols, plan = [], []
    for h in range(C_HEADS):
        cols.append(_pair_cols(h * 128, h * 128 + 64))
        plan.append((True, "qT", h))
    for h in range(C_HEADS):
        cols.append(_pair_cols(1024 + h * 128, 1024 + h * 128 + 64))
        plan.append((True, "k", h))
    for h in range(C_HEADS):
        cols.append(2048 + h * 128 + np.arange(LANES))
        plan.append((False, "vT", h))
    return np.concatenate(cols), plan


def _compress_weights(pe, w1, w2):
    d = _PAIR_D
    which = _PAIR_WHICH
    pe_l = pe[:, :, d]
    pe_l = pe_l.reshape(2, 2, 1, CMP_STRIDE * LANES)
    w1r = w1.reshape(2, CMP_BLOCK, HEAD_DIM, CMP_HIDDEN)[:, :, d, :]
    per_g = []
    for g in range(B_KV_GROUPS):
        keep = jnp.asarray(which == g)[None, None, :, None]
        per_g.append(jnp.where(keep, w1r, 0.0))
    w1g = jnp.stack(per_g, axis=1)
    w1g = w1g.reshape(2, B_KV_GROUPS, 2, CMP_STRIDE * LANES, CMP_HIDDEN).astype(BF16)
    w2k = w2[0][:, d]
    w2v = w2[1][:, np.arange(LANES) % HEAD_DIM]
    w2l = jnp.stack([w2k, w2v], axis=0).astype(BF16)
    return pe_l, w1g, w2l


def _even_mixer(h, norm_g, w_in, cmp_pe, cmp_w1, cmp_w2, w_out, cos_slab, sin_slab):
    B, T, D = h.shape
    cols, plan, lay = _even_layout()
    w = _gather_cols(w_in, cols).astype(BF16)
    sl = _project(h, norm_g, w, cos_slab, sin_slab, plan)

    bias = _dsa_select(sl, lay)
    o_a = _dsa_attention(sl, bias, lay)

    n_ch = T // CMP_STRIDE
    flat = sl["f"][:, lay["kc"]:lay["vc"] + 1].reshape(B, 2, n_ch, CMP_STRIDE * LANES)
    flat = jnp.swapaxes(flat, 0, 1)
    pe_l, w1g, w2l = _compress_weights(cmp_pe, cmp_w1, cmp_w2)
    cmp_k, cmp_vt = _compress(flat, pe_l, w1g, w2l)
    o_b = _nsa_attention(sl, cmp_k, cmp_vt, lay)

    na = A_HEADS * HEAD_DIM
    wo = w_out.astype(BF16)
    h2 = _out_proj(h.reshape(B * T, D), [o_a.reshape(B * T, -1), o_b.reshape(B * T, -1)],
                   [wo[:na], wo[na:]])
    return h2.reshape(B, T, D)


def _odd_mixer(h, norm_g, w_in, lam, subln_g, w_out, cos_slab, sin_slab, lambda_init):
    B, T, D = h.shape
    cols, plan = _odd_layout()
    w = _gather_cols(w_in, cols).astype(BF16)
    sl = _project(h, norm_g, w, cos_slab, sin_slab, plan)
    o = _diff_attention(sl, lam, subln_g, lambda_init)
    h2 = _out_proj(h.reshape(B * T, D), [o.reshape(B * T, -1)], [w_out.astype(BF16)])
    return h2.reshape(B, T, D)


def kernel(x, mix_norm_g, mlp_norm_g, even_w_in, even_cmp_pe, even_cmp_w1, even_cmp_w2, even_w_out, odd_w_in, odd_lambda, odd_subln_g, odd_w_out, mlp_w_up, mlp_w_down, final_norm_g):
    B, T, D = x.shape
    depth = mix_norm_g.shape[0]
    cos_slab, sin_slab = _rope_slabs(T)
    h = x
    for layer in range(depth):
        if layer % 2 == 0:
            e = layer // 2
            h = _even_mixer(h, mix_norm_g[layer], even_w_in[e], even_cmp_pe[e], even_cmp_w1[e],
                            even_cmp_w2[e], even_w_out[e], cos_slab, sin_slab)
        else:
            o = layer // 2
            lambda_init = 0.8 - 0.6 * math.exp(-0.3 * layer)
            h = _odd_mixer(h, mix_norm_g[layer], odd_w_in[o], odd_lambda[o], odd_subln_g[o],
                           odd_w_out[o], cos_slab, sin_slab, lambda_init)
        h2 = _mlp(h.reshape(B * T, D), mlp_norm_g[layer], mlp_w_up[layer].astype(BF16),
                  mlp_w_down[layer].astype(BF16))
        h = h2.reshape(B, T, D)
    return _rmsnorm(h.reshape(B * T, D), final_norm_g).reshape(B, T, D)
```

```python
import functools
import math

import numpy as np
import jax
import jax.numpy as jnp
from jax import lax
from jax.experimental import pallas as pl
from jax.experimental.pallas import tpu as pltpu

HEAD_DIM = 64
HALF = HEAD_DIM // 2
LANES = 128
ROPE_THETA = 10000.0
NORM_EPS = 1e-6
SCALE = HEAD_DIM ** -0.5

A_HEADS = 8
IDX_HEADS = 4
DSA_TOPK = 256
B_HEADS = 8
B_KV_GROUPS = 2
B_PER_GROUP = B_HEADS // B_KV_GROUPS
CMP_BLOCK = 32
CMP_STRIDE = 16
CMP_HIDDEN = 256
SLC_BLOCK = 64
SLC_TOPN = 16
WINDOW = 512
C_HEADS = 8

KV_TILE = 512
Q_TILE = 128

LOG2E = math.log2(math.e)
Q_SCALE = SCALE * LOG2E
ONES_ROWS = 16
BIG = 2.0 ** 100

NEG_INF = float("-inf")
M_FLOOR = -1e30
INT_MIN = -(2 ** 31)

VMEM_LIMIT = 56 * 1024 * 1024

BF16 = jnp.bfloat16
F32 = jnp.float32


def _dot(a, b):
    return jnp.dot(a, b, preferred_element_type=F32)


def _dot_tn(a, b):
    return lax.dot_general(a, b, (((0,), (0,)), ((), ())), preferred_element_type=F32)


def _params(sem):
    return pltpu.CompilerParams(dimension_semantics=sem, vmem_limit_bytes=VMEM_LIMIT)


def _iota(shape, axis):
    return lax.broadcasted_iota(jnp.int32, shape, axis)


def _pair_cols(base_a, base_b):
    lane = np.arange(LANES)
    half = lane // 64
    which = (lane % 64) // HALF
    i = lane % HALF
    base = np.where(which == 0, base_a, base_b)
    return base + half * HALF + i


def _dup_cols(base):
    return base + np.arange(LANES) % HEAD_DIM


def _head_cols(base):
    lane = np.arange(LANES)
    return np.where(lane < HEAD_DIM, base + lane, -1)


_PAIR_D = _pair_cols(0, 0)
_PAIR_WHICH = (np.arange(LANES) % 64) // HALF


def _gather_cols(w, cols):
    cols = np.asarray(cols)
    safe = np.where(cols >= 0, cols, 0)
    g = jnp.take(w, jnp.asarray(safe, dtype=jnp.int32), axis=1)
    return jnp.where(jnp.asarray(cols >= 0)[None, :], g, 0.0)


def _rope_slabs(T):
    inv = 1.0 / (ROPE_THETA ** (jnp.arange(0, HEAD_DIM, 2, dtype=F32) / HEAD_DIM))
    ang = jnp.arange(T, dtype=F32)[:, None] * inv[None, :]
    cos, sin = jnp.cos(ang), jnp.sin(ang)
    cos_slab = jnp.tile(cos, (1, 4))
    sin_slab = jnp.concatenate([-sin, -sin, sin, sin], axis=1)
    return cos_slab, sin_slab


_KINDS = ("qT", "k", "vT", "f", "fT")


def _proj_kernel(x_ref, g_ref, w_ref, cos_ref, sin_ref, *out_refs, plan, kinds, chunk):
    outs = dict(zip(kinds, out_refs))
    x = x_ref[0]
    ms = jnp.mean(x * x, axis=-1, keepdims=True)
    xn = ((x * lax.rsqrt(ms + NORM_EPS)) * g_ref[...]).astype(BF16)
    cos = cos_ref[...]
    sin = sin_ref[...]
    n = len(plan)
    for c0 in range(0, n, chunk):
        c1 = min(c0 + chunk, n)
        r = _dot(xn, w_ref[:, c0 * LANES:c1 * LANES])
        for s in range(c0, c1):
            roped, kind, idx, opt = plan[s]
            y = r[:, (s - c0) * LANES:(s - c0 + 1) * LANES]
            if roped:
                y = y * cos + pltpu.roll(y, 64, 1) * sin
            if kind == "qT":
                if opt is not None:
                    y = y * opt
                outs[kind][0, idx] = y.T.astype(BF16)
            elif kind == "k":
                outs[kind][0, idx] = y.astype(BF16)
            elif kind == "vT":
                if opt == "ones_row_64":
                    y = jnp.where(_iota(y.shape, 1) == HEAD_DIM, 1.0, y)
                    outs[kind][0, idx, 0] = y.T.astype(BF16)
                else:
                    outs[kind][0, idx, 0, 0:LANES, :] = y.T.astype(BF16)
                    outs[kind][0, idx, 0, LANES:, :] = jnp.ones((ONES_ROWS, y.shape[0]), BF16)
            elif kind == "f":
                outs[kind][0, idx] = y
            else:
                outs[kind][0, idx] = y.T


def _project(x, g, w, cos_slab, sin_slab, plan, chunk=4):
    B, T, D = x.shape
    tm = KV_TILE
    assert T % tm == 0
    count = {k: sum(1 for p in plan if p[1] == k) for k in _KINDS}
    kinds = tuple(k for k in _KINDS if count[k])
    out_specs, out_shape = [], []
    for k in kinds:
        n = count[k]
        if k in ("qT", "fT"):
            out_specs.append(pl.BlockSpec((1, n, LANES, tm), lambda b, i: (b, 0, 0, i)))
            out_shape.append(jax.ShapeDtypeStruct((B, n, LANES, T), BF16 if k == "qT" else F32))
        elif k in ("k", "f"):
            out_specs.append(pl.BlockSpec((1, n, tm, LANES), lambda b, i: (b, 0, i, 0)))
            out_shape.append(jax.ShapeDtypeStruct((B, n, T, LANES), BF16 if k == "k" else F32))
        else:
            wide = any(p[1] == "vT" and p[3] != "ones_row_64" for p in plan)
            rows = LANES + ONES_ROWS if wide else LANES
            out_specs.append(pl.BlockSpec((1, n, 1, rows, tm), lambda b, i: (b, 0, i, 0, 0)))
            out_shape.append(jax.ShapeDtypeStruct((B, n, T // tm, rows, tm), BF16))
    kern = functools.partial(_proj_kernel, plan=tuple(plan), kinds=kinds, chunk=chunk)
    outs = pl.pallas_call(
        kern,
        grid=(B, T // tm),
        in_specs=[
            pl.BlockSpec((1, tm, D), lambda b, i: (b, i, 0)),
            pl.BlockSpec((1, D), lambda b, i: (0, 0)),
            pl.BlockSpec((D, len(plan) * LANES), lambda b, i: (0, 0)),
            pl.BlockSpec((tm, LANES), lambda b, i: (i, 0)),
            pl.BlockSpec((tm, LANES), lambda b, i: (i, 0)),
        ],
        out_specs=out_specs,
        out_shape=out_shape,
        compiler_params=_params(("parallel", "parallel")),
        name="norm_proj_rope",
    )(x, g.reshape(1, D), w, cos_slab, sin_slab)
    return dict(zip(kinds, outs))


def _head_rows(slab_t, which):
    row = _iota(slab_t.shape, 0)
    keep = ((row % 64) // HALF) == which
    return jnp.where(keep, slab_t, jnp.zeros_like(slab_t))


def _tree(op, xs):
    while len(xs) > 1:
        xs = [op(xs[i], xs[i + 1]) if i + 1 < len(xs) else xs[i] for i in range(0, len(xs), 2)]
    return xs[0]


def _fold_rows_max(x, ways=4):
    rows, n = x.shape
    per = rows // ways
    parts = [jnp.max(x[i * per:(i + 1) * per].reshape(per // 8, 8, n), axis=0) for i in range(ways)]
    return _tree(jnp.maximum, parts)


ROW_BLOCK = 64


def _online_update(s_sc, p_sc, v_aug, m_ref, acc_ref):
    tk, n = s_sc.shape
    m_prev = m_ref[...]
    m_tile = jnp.max(_fold_rows_max(s_sc[...]), axis=0, keepdims=True)
    m_new = jnp.maximum(m_prev, m_tile)
    alpha = jnp.exp2(m_prev - m_new)
    for r in range(tk // ROW_BLOCK):
        rows = slice(r * ROW_BLOCK, (r + 1) * ROW_BLOCK)
        p_sc[rows, :] = jnp.exp2(s_sc[rows, :] - m_new).astype(BF16)
    acc_ref[...] = alpha * acc_ref[...] + _dot(v_aug, p_sc[...])
    m_ref[...] = m_new


def _flash_pipeline(n, put_scores, update, buf_a, buf_b, put_last=None):
    if put_last is None:
        put_scores(buf_a, 0)
    else:
        @pl.when(n > 0)
        def _():
            put_scores(buf_a, 0)

    def pair(p, carry):
        j = 2 * p
        put_scores(buf_b, j + 1)
        update(buf_a, j)

        @pl.when(j + 2 < n)
        def _():
            put_scores(buf_a, j + 2)
            update(buf_b, j + 1)

        return carry

    lax.fori_loop(0, n // 2, pair, 0)
    in_b = jnp.logical_and(n > 0, n % 2 == 0)
    in_a = n % 2 == 1

    def finish(cur, other):
        if put_last is not None:
            put_last(other, n)
        update(cur, n - 1)
        if put_last is not None:
            update(other, n)

    pl.when(in_b)(lambda: finish(buf_b, buf_a))
    pl.when(in_a)(lambda: finish(buf_a, buf_b))
    if put_last is not None:
        @pl.when(n == 0)
        def _():
            put_last(buf_a, 0)
            update(buf_a, 0)


def _softmax_keys(s):
    m = jnp.max(s, axis=0, keepdims=True)
    m = jnp.where(m > NEG_INF, m, 0.0)
    e = jnp.exp2(s - m)
    return e / jnp.maximum(jnp.sum(e, axis=0, keepdims=True), 1e-30)


def _init_flash(m_ref, acc_ref):
    m_ref[...] = jnp.full(m_ref.shape, M_FLOOR, F32)
    acc_ref[...] = jnp.zeros(acc_ref.shape, F32)


def _diff_attn_kernel(lam_ref, q_ref, k_ref, v_ref, g_ref, o_ref, m_sc, acc_sc, sa_sc, sb_sc,
                      p_sc, *, tq, lambda_init):
    qi = pl.program_id(2)
    q_t = q_ref[0, 0]
    qs = [_head_rows(q_t, c) for c in range(2)]
    for c in range(2):
        _init_flash(m_sc.at[c], acc_sc.at[c])

    def put_scores(buf, j, masked=False):
        k = k_ref[0, 0, pl.ds(pl.multiple_of(j * tq, tq), tq), :]
        for c in range(2):
            s = _dot(k, qs[c])
            if masked:
                s = jnp.where(_iota(s.shape, 0) <= _iota(s.shape, 1), s, NEG_INF)
            buf[c] = s

    def put_diagonal(buf, j):
        put_scores(buf, j, masked=True)

    def update(buf, j):
        v_aug = v_ref[0, 0, j]
        for c in range(2):
            _online_update(buf.at[c], p_sc.at[c], v_aug, m_sc.at[c], acc_sc.at[c])

    _flash_pipeline(qi, put_scores, update, sa_sc, sb_sc, put_last=put_diagonal)

    lam = lam_ref[...]
    s01 = jnp.sum(lam[0:1] * lam[1:2], axis=-1, keepdims=True)
    s23 = jnp.sum(lam[2:3] * lam[3:4], axis=-1, keepdims=True)
    lam_val = jnp.exp(s01) - jnp.exp(s23) + lambda_init
    o0 = acc_sc[0, 0:LANES] / jnp.maximum(acc_sc[0, LANES:LANES + 1], 1e-30)
    o1 = acc_sc[1, 0:LANES] / jnp.maximum(acc_sc[1, LANES:LANES + 1], 1e-30)
    o = o0 - lam_val * o1
    y = o * lax.rsqrt(jnp.mean(o * o, axis=0, keepdims=True) + NORM_EPS)
    y = (y * g_ref[...]) * (1.0 - lambda_init)
    o_ref[0] = y.T.astype(o_ref.dtype)


def _diff_attention(sl, lam, subln_g, lambda_init):
    q_t, k, v_t = sl["qT"], sl["k"], sl["vT"]
    B, H, T, _ = k.shape
    tq = KV_TILE
    nk = T // tq
    kern = functools.partial(_diff_attn_kernel, tq=tq, lambda_init=lambda_init)
    return pl.pallas_call(
        kern,
        grid=(B, H, T // tq),
        in_specs=[
            pl.BlockSpec((4, HEAD_DIM), lambda b, h, i: (0, 0)),
            pl.BlockSpec((1, 1, LANES, tq), lambda b, h, i: (b, h, 0, i)),
            pl.BlockSpec((1, 1, T, LANES), lambda b, h, i: (b, h, 0, 0)),
            pl.BlockSpec((1, 1, nk, LANES + ONES_ROWS, tq), lambda b, h, i: (b, h, 0, 0, 0)),
            pl.BlockSpec((LANES, 1), lambda b, h, i: (0, 0)),
        ],
        out_specs=pl.BlockSpec((1, tq, LANES), lambda b, h, i: (b, i, h)),
        out_shape=jax.ShapeDtypeStruct((B, T, H * LANES), BF16),
        scratch_shapes=[
            pltpu.VMEM((2, 1, tq), F32),
            pltpu.VMEM((2, LANES + ONES_ROWS, tq), F32),
            pltpu.VMEM((2, tq, tq), F32),
            pltpu.VMEM((2, tq, tq), F32),
            pltpu.VMEM((2, tq, tq), BF16),
        ],
        compiler_params=_params(("parallel", "parallel", "arbitrary")),
        name="diff_attention",
    )(lam, q_t, k, v_t, subln_g.reshape(LANES, 1))


def _dsa_select_kernel(qi_ref, ki_ref, misc_ref, tril_ref, bias_ref, key_sc, *, tq, ck, nk, topk):
    t0 = pl.program_id(1) * tq
    nvalid = (t0 + tq - 1) // ck + 1
    w = misc_ref[0, 0]
    qh = jnp.concatenate([_head_rows(qi_ref[0, h // 2], h % 2) for h in range(IDX_HEADS)], axis=1)
    t_q = t0 + _iota((ck, tq), 1)

    def causal(c):
        return (c * ck + _iota((ck, tq), 0)) <= t_q

    def fill(c, carry):
        kk = ki_ref[0, 0, pl.ds(pl.multiple_of(c * ck, ck), ck), :]
        r = _dot(kk, qh)
        score = jnp.zeros((ck, tq), F32)
        for h in range(IDX_HEADS):
            score = score + w[h:h + 1, :] * jnp.maximum(r[:, h * tq:(h + 1) * tq], 0.0)
        bits = pltpu.bitcast(score, jnp.int32)
        key = bits ^ ((bits >> 31) & jnp.int32(0x7FFFFFFF))
        key = jnp.where(score == 0.0, 0, key)
        key_sc[c] = jnp.where(causal(c), key, INT_MIN)
        return carry

    lax.fori_loop(0, nvalid, fill, 0)

    def count(pred):
        def chunk(c, acc):
            ind = pred(key_sc[c], c)
            return acc + jnp.sum(ind.reshape(8, ck // 8, tq), axis=0)
        acc = lax.fori_loop(0, nvalid, chunk, jnp.zeros((ck // 8, tq), F32))
        return jnp.sum(acc, axis=0, keepdims=True)

    kf = float(topk)
    zero = jnp.zeros((1, tq), jnp.int32)
    ans = jnp.where(count(lambda kc, c: jnp.where(kc >= zero, 1.0, 0.0)) >= kf, 0, INT_MIN)

    def bit_step(i, ans):
        cand = ans | (jnp.int32(1) << (30 - i))
        cnt = count(lambda kc, c: jnp.where(kc >= cand, 1.0, 0.0))
        return jnp.where(cnt >= kf, cand, ans)

    ans = lax.fori_loop(0, 31, bit_step, ans)

    cnt_gt = count(lambda kc, c: jnp.where(kc > ans, 1.0, 0.0))
    cnt_eq = count(lambda kc, c: jnp.where(kc == ans, jnp.where(causal(c), 1.0, 0.0), 0.0))
    need = kf - cnt_gt
    has_tie = jnp.max(cnt_eq - need) > 0.0

    @pl.when(jnp.logical_not(has_tie))
    def _():
        def emit(c, carry):
            sel = jnp.where(key_sc[c] >= ans, jnp.where(causal(c), 0.0, NEG_INF), NEG_INF)
            bias_ref[0, 0, c] = sel.astype(BF16)
            return carry
        lax.fori_loop(0, nvalid, emit, 0)

    @pl.when(has_tie)
    def _():
        def emit(c, carry):
            kc = key_sc[c]
            eq = jnp.where(kc == ans, jnp.where(causal(c), 1.0, 0.0), 0.0)
            pre = _dot(tril_ref[...], eq.astype(BF16)) + carry
            take = jnp.where(kc > ans, 1.0, jnp.where(pre < need, eq, 0.0))
            sel = jnp.where(take > 0.5, jnp.where(causal(c), 0.0, NEG_INF), NEG_INF)
            bias_ref[0, 0, c] = sel.astype(BF16)
            return carry + jnp.sum(eq, axis=0, keepdims=True)
        lax.fori_loop(0, nvalid, emit, jnp.zeros((1, tq), F32))

    def blank(c, carry):
        bias_ref[0, 0, c] = jnp.full((ck, tq), NEG_INF, BF16)
        return carry

    lax.fori_loop(nvalid, nk, blank, 0)


def _dsa_select(sl, lay):
    q_t, k, misc_t = sl["qT"], sl["k"], sl["fT"]
    B, _, T, _ = k.shape
    tq, ck = Q_TILE, KV_TILE
    nk = T // ck
    topk = min(DSA_TOPK, T // 4)
    tril = jnp.asarray(np.tril(np.ones((ck, ck), np.float32), -1), BF16)
    kern = functools.partial(_dsa_select_kernel, tq=tq, ck=ck, nk=nk, topk=topk)
    return pl.pallas_call(
        kern,
        grid=(B, T // tq),
        in_specs=[
            pl.BlockSpec((1, 2, LANES, tq), lambda b, i: (b, lay["qi"] // 2, 0, i)),
            pl.BlockSpec((1, 1, T, LANES), lambda b, i: (b, lay["ki"], 0, 0)),
            pl.BlockSpec((1, 1, LANES, tq), lambda b, i: (b, lay["misc"], 0, i)),
            pl.BlockSpec((ck, ck), lambda b, i: (0, 0)),
        ],
        out_specs=pl.BlockSpec((1, 1, nk, ck, tq), lambda b, i: (b, i, 0, 0, 0)),
        out_shape=jax.ShapeDtypeStruct((B, T // tq, nk, ck, tq), BF16),
        scratch_shapes=[pltpu.VMEM((nk, ck, tq), jnp.int32)],
        compiler_params=_params(("parallel", "parallel")),
        name="dsa_select",
    )(q_t, k, misc_t, tril)


def _store_head_pairs(o_ref, o, n_heads, tq):
    for p in range(n_heads // 2):
        even = o[0:HEAD_DIM, (2 * p) * tq:(2 * p + 1) * tq]
        odd = o[0:HEAD_DIM, (2 * p + 1) * tq:(2 * p + 2) * tq]
        pair = jnp.concatenate([even, odd], axis=0)
        o_ref[0, :, p * LANES:(p + 1) * LANES] = pair.T.astype(o_ref.dtype)


def _normalized(acc):
    return acc[0:HEAD_DIM] / jnp.maximum(acc[HEAD_DIM:HEAD_DIM + 1], 1e-30)


def _dsa_attn_kernel(q_ref, k_ref, v_ref, bias_ref, o_ref, qst, m_sc, acc_sc, sa_sc, sb_sc,
                     p_sc, *, tq, tk):
    qi = pl.program_id(1)
    H = A_HEADS
    for h in range(H):
        qst[:, h * tq:(h + 1) * tq] = _head_rows(q_ref[0, h // 2], h % 2)
    _init_flash(m_sc, acc_sc)
    nkv = (qi * tq) // tk + 1

    def put_scores(buf, j):
        k = k_ref[0, 0, pl.ds(pl.multiple_of(j * tk, tk), tk), :]
        b = bias_ref[0, 0, j].astype(F32)
        buf[...] = _dot(k, qst[...]) + jnp.concatenate([b] * H, axis=1)

    def update(buf, j):
        _online_update(buf, p_sc, v_ref[0, 0, j], m_sc, acc_sc)

    _flash_pipeline(nkv, put_scores, update, sa_sc, sb_sc)
    _store_head_pairs(o_ref, _normalized(acc_sc[...]), H, tq)


def _dsa_attention(sl, bias, lay):
    q_t, k, v_t = sl["qT"], sl["k"], sl["vT"]
    B, _, T, _ = k.shape
    tq, tk = Q_TILE, KV_TILE
    nk = T // tk
    H = A_HEADS
    N = H * tq
    kern = functools.partial(_dsa_attn_kernel, tq=tq, tk=tk)
    return pl.pallas_call(
        kern,
        grid=(B, T // tq),
        in_specs=[
            pl.BlockSpec((1, H // 2, LANES, tq), lambda b, i: (b, lay["qA"] // (H // 2), 0, i)),
            pl.BlockSpec((1, 1, T, LANES), lambda b, i: (b, lay["kA"], 0, 0)),
            pl.BlockSpec((1, 1, nk, LANES, tk), lambda b, i: (b, lay["vA"], 0, 0, 0)),
            pl.BlockSpec((1, 1, nk, tk, tq), lambda b, i: (b, i, 0, 0, 0)),
        ],
        out_specs=pl.BlockSpec((1, tq, H * HEAD_DIM), lambda b, i: (b, i, 0)),
        out_shape=jax.ShapeDtypeStruct((B, T, H * HEAD_DIM), BF16),
        scratch_shapes=[
            pltpu.VMEM((LANES, N), BF16),
            pltpu.VMEM((1, N), F32),
            pltpu.VMEM((LANES, N), F32),
            pltpu.VMEM((tk, N), F32),
            pltpu.VMEM((tk, N), F32),
            pltpu.VMEM((tk, N), BF16),
        ],
        compiler_params=_params(("parallel", "arbitrary")),
        name="dsa_attention",
    )(q_t, k, v_t, bias)


def _compress_kernel(x_ref, pe_ref, w1_ref, w2_ref, o_ref, ot_ref, *, n_ch):
    x = x_ref[0, 0]
    xt = (x + pe_ref[0, 0]).astype(BF16)
    xb = (x + pe_ref[0, 1]).astype(BF16)
    for g in range(B_KV_GROUPS):
        a = _dot(xt, w1_ref[0, g, 0])
        b = _dot(xb, w1_ref[0, g, 1])
        h = jax.nn.gelu(a + pltpu.roll(b, n_ch - 1, 0))
        r = _dot(h.astype(BF16), w2_ref[0])
        o_ref[0, 0, g] = r.astype(BF16)
        ot_ref[0, 0, g] = r.T.astype(BF16)


def _compress(flat, pe, w1, w2):
    _, B, n_ch, W = flat.shape
    G = B_KV_GROUPS
    kern = functools.partial(_compress_kernel, n_ch=n_ch)
    return pl.pallas_call(
        kern,
        grid=(2, B),
        in_specs=[
            pl.BlockSpec((1, 1, n_ch, W), lambda s, b: (s, b, 0, 0)),
            pl.BlockSpec((1, 2, 1, W), lambda s, b: (s, 0, 0, 0)),
            pl.BlockSpec((1, G, 2, W, CMP_HIDDEN), lambda s, b: (s, 0, 0, 0, 0)),
            pl.BlockSpec((1, CMP_HIDDEN, LANES), lambda s, b: (s, 0, 0)),
        ],
        out_specs=[
            pl.BlockSpec((1, 1, G, n_ch, LANES), lambda s, b: (s, b, 0, 0, 0)),
            pl.BlockSpec((1, 1, G, LANES, n_ch), lambda s, b: (s, b, 0, 0, 0)),
        ],
        out_shape=[
            jax.ShapeDtypeStruct((2, B, G, n_ch, LANES), BF16),
            jax.ShapeDtypeStruct((2, B, G, LANES, n_ch), BF16),
        ],
        compiler_params=_params(("parallel", "parallel")),
        name="nsa_compress",
    )(flat, pe, w1, w2)


def _nsa_kernel(q_ref, kc_ref, vc_ref, ks_ref, vs_ref, kw_ref, vw_ref, misc_ref, ov_ref, ex_ref,
                cz_ref, o_ref, qst, m_sc, acc_sc, sa_sc, sb_sc, p_sc, *, tq, tk, T, n_s, n_sel,
                gate_row0):
    g = pl.program_id(1)
    qi = pl.program_id(2)
    t0 = qi * tq
    J = B_PER_GROUP
    N = J * tq
    for j in range(J):
        qst[0:LANES, j * tq:(j + 1) * tq] = _head_rows(q_ref[0, j // 2], j % 2)
    q = qst[0:LANES, :]

    def q_time(shape):
        return t0 + (_iota(shape, 1) % tq)

    kc = kc_ref[0, 0, 0]
    n_ch = kc.shape[0]
    s_c = _dot(kc, q)
    cmp_end = _iota((n_ch, N), 0) * CMP_STRIDE + (CMP_BLOCK - 1)
    s_c = jnp.where(cmp_end <= q_time((n_ch, N)), s_c, NEG_INF)
    p_c = _softmax_keys(s_c)
    o_c = _dot(vc_ref[0, 0, 0], p_c.astype(BF16))

    psum = p_c[:, 0:tq]
    for j in range(1, J):
        psum = psum + p_c[:, j * tq:(j + 1) * tq]
    p_hi = psum.astype(BF16)
    p_lo = (psum - p_hi.astype(F32)).astype(BF16)
    imp = _dot(ov_ref[...], p_hi) + _dot(ov_ref[...], p_lo)
    rows = -(-n_s // 8) * 8
    imp = imp[0:rows]
    blk = _iota((rows, tq), 0)
    t_q = t0 + _iota((rows, tq), 1)
    cur = t_q // SLC_BLOCK
    forced = (blk == 0) | (blk == cur) | (blk == cur - 1)
    sc = jnp.where(forced, jnp.inf, imp)
    sc = jnp.where(blk * SLC_BLOCK <= t_q, sc, NEG_INF)
    rank = jnp.zeros((rows, tq), F32)
    for m in range(n_s):
        cm = sc[m:m + 1, :]
        beats = jnp.where(cm > sc, 1.0, jnp.where(cm == sc, jnp.where(blk > m, 1.0, 0.0), 0.0))
        rank = rank + beats
    drop = jnp.where(rank < float(n_sel), 0.0, 1.0)
    if rows < LANES:
        drop = jnp.concatenate([drop, jnp.ones((LANES - rows, tq), F32)], axis=0)
    drop = drop.astype(BF16)
    qst[LANES:2 * LANES, :] = jnp.concatenate([drop] * J, axis=1)

    _init_flash(m_sc, acc_sc)
    jd = t0 // tk
    off = (t0 - jd * tk) // tq
    n_off = tk // tq

    def put_scores(buf, j):
        k = ks_ref[0, 0, pl.ds(pl.multiple_of(j * tk, tk), tk), :]
        lhs = jnp.concatenate([k, ex_ref[j]], axis=1)
        cz = cz_ref[jnp.where(j == jd, off, n_off)].astype(F32)
        buf[...] = _dot(lhs, qst[...]) + jnp.concatenate([cz] * J, axis=1)

    def update(buf, j):
        _online_update(buf, p_sc, vs_ref[0, 0, j], m_sc, acc_sc)

    _flash_pipeline(jd + 1, put_scores, update, sa_sc, sb_sc)
    o_s = _normalized(acc_sc[...])

    wlen = min(WINDOW + tq, T)
    wstart = pl.multiple_of(jnp.maximum(t0 - WINDOW, 0), tq)
    kw = kw_ref[0, 0, pl.ds(wstart, wlen), :]
    vw = vw_ref[0, 0, pl.ds(wstart, wlen), :]
    s_w = _dot(kw, q)
    pos = wstart + _iota((wlen, N), 0)
    tt = q_time((wlen, N))
    s_w = jnp.where(pos <= tt, jnp.where(pos > tt - WINDOW, s_w, NEG_INF), NEG_INF)
    o_w = _dot_tn(vw, _softmax_keys(s_w).astype(BF16))

    gates = jax.nn.sigmoid(misc_ref[0, 0])

    def gate_row(c):
        parts = []
        for j in range(J):
            r0 = gate_row0 + j * 3 + c
            r1 = gate_row0 + (J + j) * 3 + c
            parts.append(jnp.where(g == 0, gates[r0:r0 + 1, :], gates[r1:r1 + 1, :]))
        return jnp.concatenate(parts, axis=1)

    o = gate_row(0) * o_c[0:HEAD_DIM] + (gate_row(1) * o_s + gate_row(2) * o_w[0:HEAD_DIM])
    _store_head_pairs(o_ref, o, J, tq)


def _nsa_attention(sl, cmp_k, cmp_vt, lay):
    q_t, k, v_t, misc_t = sl["qT"], sl["k"], sl["vT"], sl["fT"]
    B, _, T, _ = k.shape
    G, J = B_KV_GROUPS, B_PER_GROUP
    tq, tk = Q_TILE, KV_TILE
    nk = T // tk
    n_ch = T // CMP_STRIDE
    n_c = n_ch - CMP_BLOCK // CMP_STRIDE + 1
    n_s = T // SLC_BLOCK
    n_sel = min(SLC_TOPN, n_s)
    assert n_s <= LANES
    c0 = np.arange(n_ch) * CMP_STRIDE
    s0 = np.arange(LANES) * SLC_BLOCK
    ov = ((c0[None, :] < s0[:, None] + SLC_BLOCK) & (c0[None, :] + CMP_BLOCK > s0[:, None]))
    ov = ov & (np.arange(n_ch)[None, :] < n_c) & (np.arange(LANES)[:, None] < n_s)
    ov = jnp.asarray(ov.astype(np.float32), BF16)
    pos = np.arange(T).reshape(nk, tk, 1)
    ex = (pos // SLC_BLOCK == np.arange(LANES).reshape(1, 1, LANES))
    ex = jnp.asarray(ex.astype(np.float32) * -BIG, BF16)
    n_off = tk // tq
    kp = np.arange(tk).reshape(1, tk, 1)
    tl = np.arange(tq).reshape(1, 1, tq) + np.arange(n_off + 1).reshape(n_off + 1, 1, 1) * tq
    cz = np.where((kp <= tl) | (np.arange(n_off + 1).reshape(-1, 1, 1) == n_off), 0.0, -BIG)
    cz = jnp.asarray(cz.astype(np.float32), BF16)
    kern = functools.partial(_nsa_kernel, tq=tq, tk=tk, T=T, n_s=n_s, n_sel=n_sel,
                             gate_row0=lay["gate_row0"])
    N = J * tq
    kslab = lambda off: pl.BlockSpec((1, 1, T, LANES), lambda b, g, i: (b, off + g, 0, 0))
    return pl.pallas_call(
        kern,
        grid=(B, G, T // tq),
        in_specs=[
            pl.BlockSpec((1, 2, LANES, tq), lambda b, g, i: (b, lay["qB"] // 2 + g, 0, i)),
            pl.BlockSpec((1, 1, 1, n_ch, LANES), lambda b, g, i: (0, b, g, 0, 0)),
            pl.BlockSpec((1, 1, 1, LANES, n_ch), lambda b, g, i: (1, b, g, 0, 0)),
            kslab(lay["ks"]),
            pl.BlockSpec((1, 1, nk, LANES, tk), lambda b, g, i: (b, lay["vs"] + g, 0, 0, 0)),
            kslab(lay["kw"]),
            kslab(lay["vw"]),
            pl.BlockSpec((1, 1, LANES, tq), lambda b, g, i: (b, lay["misc"], 0, i)),
            pl.BlockSpec((LANES, n_ch), lambda b, g, i: (0, 0)),
            pl.BlockSpec((nk, tk, LANES), lambda b, g, i: (0, 0, 0)),
            pl.BlockSpec((n_off + 1, tk, tq), lambda b, g, i: (0, 0, 0)),
        ],
        out_specs=pl.BlockSpec((1, tq, J * HEAD_DIM), lambda b, g, i: (b, i, g)),
        out_shape=jax.ShapeDtypeStruct((B, T, B_HEADS * HEAD_DIM), BF16),
        scratch_shapes=[
            pltpu.VMEM((2 * LANES, N), BF16),
            pltpu.VMEM((1, N), F32),
            pltpu.VMEM((LANES, N), F32),
            pltpu.VMEM((tk, N), F32),
            pltpu.VMEM((tk, N), F32),
            pltpu.VMEM((tk, N), BF16),
        ],
        compiler_params=_params(("parallel", "parallel", "arbitrary")),
        name="nsa_attention",
    )(q_t, cmp_k, cmp_vt, k, v_t, k, k, misc_t, ov, ex, cz)


def _out_proj_kernel(*refs, n_in):
    h_ref = refs[0]
    o_refs = refs[1:1 + n_in]
    w_refs = refs[1 + n_in:1 + 2 * n_in]
    out_ref = refs[1 + 2 * n_in]
    acc = _dot(o_refs[0][...], w_refs[0][...])
    for i in range(1, n_in):
        acc = acc + _dot(o_refs[i][...], w_refs[i][...])
    out_ref[...] = h_ref[...] + acc


def _out_proj(h2, outs, ws, tm=512):
    N, D = h2.shape
    tm = min(tm, N)
    n_in = len(outs)
    kern = functools.partial(_out_proj_kernel, n_in=n_in)
    in_specs = [pl.BlockSpec((tm, D), lambda i: (i, 0))]
    in_specs += [pl.BlockSpec((tm, o.shape[1]), lambda i: (i, 0)) for o in outs]
    in_specs += [pl.BlockSpec(w.shape, lambda i: (0, 0)) for w in ws]
    return pl.pallas_call(
        kern,
        grid=(N // tm,),
        in_specs=in_specs,
        out_specs=pl.BlockSpec((tm, D), lambda i: (i, 0)),
        out_shape=jax.ShapeDtypeStruct((N, D), F32),
        compiler_params=_params(("parallel",)),
        name="out_proj_residual",
    )(h2, *outs, *ws)


def _mlp_kernel(h_ref, g_ref, wu_ref, wd_ref, o_ref, xn_sc):
    f = pl.program_id(1)

    @pl.when(f == 0)
    def _():
        x = h_ref[...]
        ms = jnp.mean(x * x, axis=-1, keepdims=True)
        xn_sc[...] = ((x * lax.rsqrt(ms + NORM_EPS)) * g_ref[...]).astype(BF16)
        o_ref[...] = x

    u = _dot(xn_sc[...], wu_ref[...])
    a = jnp.square(jnp.maximum(u, 0.0)).astype(BF16)
    o_ref[...] += _dot(a, wd_ref[...])


def _mlp(h2, g, w_up, w_down, tm=512, tf=1024):
    N, D = h2.shape
    F = w_up.shape[1]
    tm = min(tm, N)
    return pl.pallas_call(
        _mlp_kernel,
        grid=(N // tm, F // tf),
        in_specs=[
            pl.BlockSpec((tm, D), lambda i, f: (i, 0)),
            pl.BlockSpec((1, D), lambda i, f: (0, 0)),
            pl.BlockSpec((D, tf), lambda i, f: (0, f)),
            pl.BlockSpec((tf, D), lambda i, f: (f, 0)),
        ],
        out_specs=pl.BlockSpec((tm, D), lambda i, f: (i, 0)),
        out_shape=jax.ShapeDtypeStruct((N, D), F32),
        scratch_shapes=[pltpu.VMEM((tm, D), BF16)],
        compiler_params=_params(("parallel", "arbitrary")),
        name="sq_relu_mlp",
    )(h2, g.reshape(1, D), w_up, w_down)


def _rmsnorm_kernel(x_ref, g_ref, o_ref):
    x = x_ref[...]
    ms = jnp.mean(x * x, axis=-1, keepdims=True)
    o_ref[...] = (x * lax.rsqrt(ms + NORM_EPS)) * g_ref[...]


def _rmsnorm(h2, g, tm=512):
    N, D = h2.shape
    tm = min(tm, N)
    return pl.pallas_call(
        _rmsnorm_kernel,
        grid=(N // tm,),
        in_specs=[pl.BlockSpec((tm, D), lambda i: (i, 0)), pl.BlockSpec((1, D), lambda i: (0, 0))],
        out_specs=pl.BlockSpec((tm, D), lambda i: (i, 0)),
        out_shape=jax.ShapeDtypeStruct((N, D), F32),
        compiler_params=_params(("parallel",)),
        name="final_rmsnorm",
    )(h2, g.reshape(1, D))


def _even_layout():
    offs = {}
    o = 0
    for name, size in (("qa", 512), ("ka", 64), ("va", 64), ("qi", 256), ("ki", 64), ("wi", 4),
                       ("qb", 512), ("kvb", 768), ("gb", 24)):
        offs[name] = o
        o += size
    kvb = lambda which, g: offs["kvb"] + (which * B_KV_GROUPS + g) * HEAD_DIM
    cols, plan, lay = [], [], {}
    n = {k: 0 for k in _KINDS}

    def add(c, roped, kind, opt=None):
        cols.append(c)
        plan.append((roped, kind, n[kind], opt))
        n[kind] += 1
        return n[kind] - 1

    lay["qA"] = n["qT"]
    for p in range(4):
        add(_pair_cols(offs["qa"] + 2 * p * 64, offs["qa"] + (2 * p + 1) * 64), True, "qT", Q_SCALE)
    lay["qB"] = n["qT"]
    for p in range(4):
        add(_pair_cols(offs["qb"] + 2 * p * 64, offs["qb"] + (2 * p + 1) * 64), True, "qT", Q_SCALE)
    lay["qi"] = n["qT"]
    for p in range(2):
        add(_pair_cols(offs["qi"] + 2 * p * 64, offs["qi"] + (2 * p + 1) * 64), True, "qT")
    lay["kA"] = add(_pair_cols(offs["ka"], offs["ka"]), True, "k")
    lay["ki"] = add(_pair_cols(offs["ki"], offs["ki"]), True, "k")
    lay["ks"] = n["k"]
    for g in range(2):
        add(_pair_cols(kvb(2, g), kvb(2, g)), True, "k")
    lay["kw"] = n["k"]
    for g in range(2):
        add(_pair_cols(kvb(4, g), kvb(4, g)), True, "k")
    lay["kc"] = add(_pair_cols(kvb(0, 0), kvb(0, 1)), True, "f")
    lay["vc"] = add(_pair_cols(kvb(1, 0), kvb(1, 1)), False, "f")
    misc = np.full(LANES, -1)
    misc[0:IDX_HEADS] = offs["wi"] + np.arange(IDX_HEADS)
    lay["gate_row0"] = 8
    misc[8:8 + 24] = offs["gb"] + np.arange(24)
    lay["misc"] = add(misc, False, "fT")
    lay["vw"] = n["k"]
    for g in range(2):
        add(_dup_cols(kvb(5, g)), False, "k")
    lay["vA"] = add(_head_cols(offs["va"]), False, "vT", "ones_row_64")
    lay["vs"] = n["vT"]
    for g in range(2):
        add(_head_cols(kvb(3, g)), False, "vT", "ones_row_64")
    return np.concatenate(cols), plan, lay


def _odd_layout():
    cols, plan = [], []
    for h in range(C_HEADS):
        cols.append(_pair_cols(h * 128, h * 128 + 64))
        plan.append((True, "qT", h, Q_SCALE))
    for h in range(C_HEADS):
        cols.append(_pair_cols(1024 + h * 128, 1024 + h * 128 + 64))
        plan.append((True, "k", h, None))
    for h in range(C_HEADS):
        cols.append(2048 + h * 128 + np.arange(LANES))
        plan.append((False, "vT", h, "ones_rows_below"))
    return np.concatenate(cols), plan


def _compress_weights(pe, w1, w2):
    d = _PAIR_D
    which = _PAIR_WHICH
    pe_l = pe[:, :, d]
    pe_l = pe_l.reshape(2, 2, 1, CMP_STRIDE * LANES)
    w1r = w1.reshape(2, CMP_BLOCK, HEAD_DIM, CMP_HIDDEN)[:, :, d, :]
    per_g = []
    for g in range(B_KV_GROUPS):
        keep = jnp.asarray(which == g)[None, None, :, None]
        per_g.append(jnp.where(keep, w1r, 0.0))
    w1g = jnp.stack(per_g, axis=1)
    w1g = w1g.reshape(2, B_KV_GROUPS, 2, CMP_STRIDE * LANES, CMP_HIDDEN).astype(BF16)
    w2k = w2[0][:, d]
    w2v = w2[1][:, np.arange(LANES) % HEAD_DIM]
    w2l = jnp.stack([w2k, w2v], axis=0).astype(BF16)
    return pe_l, w1g, w2l


def _even_mixer(h, norm_g, w_in, cmp_pe, cmp_w1, cmp_w2, w_out, cos_slab, sin_slab):
    B, T, D = h.shape
    cols, plan, lay = _even_layout()
    w = _gather_cols(w_in, cols).astype(BF16)
    sl = _project(h, norm_g, w, cos_slab, sin_slab, plan)

    bias = _dsa_select(sl, lay)
    o_a = _dsa_attention(sl, bias, lay)

    n_ch = T // CMP_STRIDE
    flat = sl["f"][:, lay["kc"]:lay["vc"] + 1].reshape(B, 2, n_ch, CMP_STRIDE * LANES)
    flat = jnp.swapaxes(flat, 0, 1)
    pe_l, w1g, w2l = _compress_weights(cmp_pe, cmp_w1, cmp_w2)
    cmp_k, cmp_vt = _compress(flat, pe_l, w1g, w2l)
    o_b = _nsa_attention(sl, cmp_k, cmp_vt, lay)

    na = A_HEADS * HEAD_DIM
    wo = w_out.astype(BF16)
    h2 = _out_proj(h.reshape(B * T, D), [o_a.reshape(B * T, -1), o_b.reshape(B * T, -1)],
                   [wo[:na], wo[na:]])
    return h2.reshape(B, T, D)


def _odd_mixer(h, norm_g, w_in, lam, subln_g, w_out, cos_slab, sin_slab, lambda_init):
    B, T, D = h.shape
    cols, plan = _odd_layout()
    w = _gather_cols(w_in, cols).astype(BF16)
    sl = _project(h, norm_g, w, cos_slab, sin_slab, plan)
    o = _diff_attention(sl, lam, subln_g, lambda_init)
    h2 = _out_proj(h.reshape(B * T, D), [o.reshape(B * T, -1)], [w_out.astype(BF16)])
    return h2.reshape(B, T, D)


def kernel(x, mix_norm_g, mlp_norm_g, even_w_in, even_cmp_pe, even_cmp_w1, even_cmp_w2, even_w_out, odd_w_in, odd_lambda, odd_subln_g, odd_w_out, mlp_w_up, mlp_w_down, final_norm_g):
    B, T, D = x.shape
    depth = mix_norm_g.shape[0]
    cos_slab, sin_slab = _rope_slabs(T)
    h = x
    for layer in range(depth):
        if layer % 2 == 0:
            e = layer // 2
            h = _even_mixer(h, mix_norm_g[layer], even_w_in[e], even_cmp_pe[e], even_cmp_w1[e],
                            even_cmp_w2[e], even_w_out[e], cos_slab, sin_slab)
        else:
            o = layer // 2
            lambda_init = 0.8 - 0.6 * math.exp(-0.3 * layer)
            h = _odd_mixer(h, mix_norm_g[layer], odd_w_in[o], odd_lambda[o], odd_subln_g[o],
                           odd_w_out[o], cos_slab, sin_slab, lambda_init)
        h2 = _mlp(h.reshape(B * T, D), mlp_norm_g[layer], mlp_w_up[layer].astype(BF16),
                  mlp_w_down[layer].astype(BF16))
        h = h2.reshape(B, T, D)
    return _rmsnorm(h.reshape(B * T, D), final_norm_g).reshape(B, T, D)
```

```python
import functools
import math

import numpy as np
import jax
import jax.numpy as jnp
from jax import lax
from jax.experimental import pallas as pl
from jax.experimental.pallas import tpu as pltpu

HEAD_DIM = 64
HALF = HEAD_DIM // 2
LANES = 128
ROPE_THETA = 10000.0
NORM_EPS = 1e-6
SCALE = HEAD_DIM ** -0.5

A_HEADS = 8
IDX_HEADS = 4
DSA_TOPK = 256
B_HEADS = 8
B_KV_GROUPS = 2
B_PER_GROUP = B_HEADS // B_KV_GROUPS
CMP_BLOCK = 32
CMP_STRIDE = 16
CMP_HIDDEN = 256
SLC_BLOCK = 64
SLC_TOPN = 16
WINDOW = 512
C_HEADS = 8

KV_TILE = 512
Q_TILE = 128

LOG2E = math.log2(math.e)
Q_SCALE = SCALE * LOG2E
ONES_ROWS = 16
BIG = 2.0 ** 100

NEG_INF = float("-inf")
M_FLOOR = -1e30
INT_MIN = -(2 ** 31)

VMEM_LIMIT = 56 * 1024 * 1024

BF16 = jnp.bfloat16
F32 = jnp.float32


def _dot(a, b):
    return jnp.dot(a, b, preferred_element_type=F32)


def _dot_tn(a, b):
    return lax.dot_general(a, b, (((0,), (0,)), ((), ())), preferred_element_type=F32)


def _params(sem):
    return pltpu.CompilerParams(dimension_semantics=sem, vmem_limit_bytes=VMEM_LIMIT)


def _iota(shape, axis):
    return lax.broadcasted_iota(jnp.int32, shape, axis)


def _pair_cols(base_a, base_b):
    lane = np.arange(LANES)
    half = lane // 64
    which = (lane % 64) // HALF
    i = lane % HALF
    base = np.where(which == 0, base_a, base_b)
    return base + half * HALF + i


def _dup_cols(base):
    return base + np.arange(LANES) % HEAD_DIM


def _head_cols(base):
    lane = np.arange(LANES)
    return np.where(lane < HEAD_DIM, base + lane, -1)


_PAIR_D = _pair_cols(0, 0)
_PAIR_WHICH = (np.arange(LANES) % 64) // HALF


def _gather_cols(w, cols):
    cols = np.asarray(cols)
    safe = np.where(cols >= 0, cols, 0)
    g = jnp.take(w, jnp.asarray(safe, dtype=jnp.int32), axis=1)
    return jnp.where(jnp.asarray(cols >= 0)[None, :], g, 0.0)


def _rope_slabs(T):
    inv = 1.0 / (ROPE_THETA ** (jnp.arange(0, HEAD_DIM, 2, dtype=F32) / HEAD_DIM))
    ang = jnp.arange(T, dtype=F32)[:, None] * inv[None, :]
    cos, sin = jnp.cos(ang), jnp.sin(ang)
    cos_slab = jnp.tile(cos, (1, 4))
    sin_slab = jnp.concatenate([-sin, -sin, sin, sin], axis=1)
    return cos_slab, sin_slab


_KINDS = ("qT", "k", "vT", "vTw", "f", "fT")


def _proj_kernel(x_ref, g_ref, w_ref, cos_ref, sin_ref, *out_refs, plan, kinds, chunk):
    outs = dict(zip(kinds, out_refs))
    x = x_ref[0]
    ms = jnp.mean(x * x, axis=-1, keepdims=True)
    xn = ((x * lax.rsqrt(ms + NORM_EPS)) * g_ref[...]).astype(BF16)
    cos = cos_ref[...]
    sin = sin_ref[...]
    n = len(plan)
    for c0 in range(0, n, chunk):
        c1 = min(c0 + chunk, n)
        r = _dot(xn, w_ref[:, c0 * LANES:c1 * LANES])
        for s in range(c0, c1):
            roped, kind, idx, opt = plan[s]
            y = r[:, (s - c0) * LANES:(s - c0 + 1) * LANES]
            if roped:
                y = y * cos + pltpu.roll(y, 64, 1) * sin
            if kind == "qT":
                if opt is not None:
                    y = y * opt
                outs[kind][0, idx] = y.T.astype(BF16)
            elif kind == "k":
                outs[kind][0, idx] = y.astype(BF16)
            elif kind == "vT":
                if opt == "ones_row_64":
                    y = jnp.where(_iota(y.shape, 1) == HEAD_DIM, 1.0, y)
                    outs[kind][0, idx, 0] = y.T.astype(BF16)
                else:
                    outs[kind][0, idx, 0, 0:LANES, :] = y.T.astype(BF16)
                    outs[kind][0, idx, 0, LANES:, :] = jnp.ones((ONES_ROWS, y.shape[0]), BF16)
            elif kind == "vTw":
                y_t = jnp.where(_iota(y.shape, 1) == HEAD_DIM, 1.0, y).T.astype(BF16)
                for sub in range(y.shape[0] // Q_TILE):
                    outs[kind][0, idx, sub] = y_t[:, sub * Q_TILE:(sub + 1) * Q_TILE]
            elif kind == "f":
                outs[kind][0, idx] = y
            else:
                outs[kind][0, idx] = y.T


def _project(x, g, w, cos_slab, sin_slab, plan, chunk=4):
    B, T, D = x.shape
    tm = KV_TILE
    assert T % tm == 0
    count = {k: sum(1 for p in plan if p[1] == k) for k in _KINDS}
    kinds = tuple(k for k in _KINDS if count[k])
    out_specs, out_shape = [], []
    for k in kinds:
        n = count[k]
        if k in ("qT", "fT"):
            out_specs.append(pl.BlockSpec((1, n, LANES, tm), lambda b, i: (b, 0, 0, i)))
            out_shape.append(jax.ShapeDtypeStruct((B, n, LANES, T), BF16 if k == "qT" else F32))
        elif k in ("k", "f"):
            out_specs.append(pl.BlockSpec((1, n, tm, LANES), lambda b, i: (b, 0, i, 0)))
            out_shape.append(jax.ShapeDtypeStruct((B, n, T, LANES), BF16 if k == "k" else F32))
        elif k == "vTw":
            sub = tm // Q_TILE
            out_specs.append(pl.BlockSpec((1, n, sub, LANES, Q_TILE), lambda b, i: (b, 0, i, 0, 0)))
            out_shape.append(jax.ShapeDtypeStruct((B, n, T // Q_TILE, LANES, Q_TILE), BF16))
        else:
            wide = any(p[1] == "vT" and p[3] != "ones_row_64" for p in plan)
            rows = LANES + ONES_ROWS if wide else LANES
            out_specs.append(pl.BlockSpec((1, n, 1, rows, tm), lambda b, i: (b, 0, i, 0, 0)))
            out_shape.append(jax.ShapeDtypeStruct((B, n, T // tm, rows, tm), BF16))
    kern = functools.partial(_proj_kernel, plan=tuple(plan), kinds=kinds, chunk=chunk)
    outs = pl.pallas_call(
        kern,
        grid=(B, T // tm),
        in_specs=[
            pl.BlockSpec((1, tm, D), lambda b, i: (b, i, 0)),
            pl.BlockSpec((1, D), lambda b, i: (0, 0)),
            pl.BlockSpec((D, len(plan) * LANES), lambda b, i: (0, 0)),
            pl.BlockSpec((tm, LANES), lambda b, i: (i, 0)),
            pl.BlockSpec((tm, LANES), lambda b, i: (i, 0)),
        ],
        out_specs=out_specs,
        out_shape=out_shape,
        compiler_params=_params(("parallel", "parallel")),
        name="norm_proj_rope",
    )(x, g.reshape(1, D), w, cos_slab, sin_slab)
    return dict(zip(kinds, outs))


def _head_rows(slab_t, which):
    row = _iota(slab_t.shape, 0)
    keep = ((row % 64) // HALF) == which
    return jnp.where(keep, slab_t, jnp.zeros_like(slab_t))


def _tree(op, xs):
    while len(xs) > 1:
        xs = [op(xs[i], xs[i + 1]) if i + 1 < len(xs) else xs[i] for i in range(0, len(xs), 2)]
    return xs[0]


def _fold_rows_max(x, ways=4):
    rows, n = x.shape
    per = rows // ways
    parts = [jnp.max(x[i * per:(i + 1) * per].reshape(per // 8, 8, n), axis=0) for i in range(ways)]
    return _tree(jnp.maximum, parts)


ROW_BLOCK = 64


def _online_update(s_sc, p_sc, v_aug, m_ref, acc_ref):
    tk, n = s_sc.shape
    m_prev = m_ref[...]
    m_tile = jnp.max(_fold_rows_max(s_sc[...]), axis=0, keepdims=True)
    m_new = jnp.maximum(m_prev, m_tile)
    alpha = jnp.exp2(m_prev - m_new)
    for r in range(tk // ROW_BLOCK):
        rows = slice(r * ROW_BLOCK, (r + 1) * ROW_BLOCK)
        p_sc[rows, :] = jnp.exp2(s_sc[rows, :] - m_new).astype(BF16)
    acc_ref[...] = alpha * acc_ref[...] + _dot(v_aug, p_sc[...])
    m_ref[...] = m_new


def _flash_pipeline(n, put_scores, update, buf_a, buf_b, put_last=None):
    if put_last is None:
        put_scores(buf_a, 0)
    else:
        @pl.when(n > 0)
        def _():
            put_scores(buf_a, 0)

    def pair(p, carry):
        j = 2 * p
        put_scores(buf_b, j + 1)
        update(buf_a, j)

        @pl.when(j + 2 < n)
        def _():
            put_scores(buf_a, j + 2)
            update(buf_b, j + 1)

        return carry

    lax.fori_loop(0, n // 2, pair, 0)
    in_b = jnp.logical_and(n > 0, n % 2 == 0)
    in_a = n % 2 == 1

    def finish(cur, other):
        if put_last is not None:
            put_last(other, n)
        update(cur, n - 1)
        if put_last is not None:
            update(other, n)

    pl.when(in_b)(lambda: finish(buf_b, buf_a))
    pl.when(in_a)(lambda: finish(buf_a, buf_b))
    if put_last is not None:
        @pl.when(n == 0)
        def _():
            put_last(buf_a, 0)
            update(buf_a, 0)


def _softmax_keys(s):
    m = jnp.max(s, axis=0, keepdims=True)
    m = jnp.where(m > NEG_INF, m, 0.0)
    e = jnp.exp2(s - m)
    return e / jnp.maximum(jnp.sum(e, axis=0, keepdims=True), 1e-30)


def _init_flash(m_ref, acc_ref):
    m_ref[...] = jnp.full(m_ref.shape, M_FLOOR, F32)
    acc_ref[...] = jnp.zeros(acc_ref.shape, F32)


def _diff_attn_kernel(lam_ref, q_ref, k_ref, v_ref, g_ref, o_ref, m_sc, acc_sc, sa_sc, sb_sc,
                      p_sc, *, tq, lambda_init):
    qi = pl.program_id(2)
    q_t = q_ref[0, 0]
    qs = [_head_rows(q_t, c) for c in range(2)]
    for c in range(2):
        _init_flash(m_sc.at[c], acc_sc.at[c])

    def put_scores(buf, j, masked=False):
        k = k_ref[0, 0, pl.ds(pl.multiple_of(j * tq, tq), tq), :]
        for c in range(2):
            s = _dot(k, qs[c])
            if masked:
                s = jnp.where(_iota(s.shape, 0) <= _iota(s.shape, 1), s, NEG_INF)
            buf[c] = s

    def put_diagonal(buf, j):
        put_scores(buf, j, masked=True)

    def update(buf, j):
        v_aug = v_ref[0, 0, j]
        for c in range(2):
            _online_update(buf.at[c], p_sc.at[c], v_aug, m_sc.at[c], acc_sc.at[c])

    _flash_pipeline(qi, put_scores, update, sa_sc, sb_sc, put_last=put_diagonal)

    lam = lam_ref[...]
    s01 = jnp.sum(lam[0:1] * lam[1:2], axis=-1, keepdims=True)
    s23 = jnp.sum(lam[2:3] * lam[3:4], axis=-1, keepdims=True)
    lam_val = jnp.exp(s01) - jnp.exp(s23) + lambda_init
    o0 = acc_sc[0, 0:LANES] / jnp.maximum(acc_sc[0, LANES:LANES + 1], 1e-30)
    o1 = acc_sc[1, 0:LANES] / jnp.maximum(acc_sc[1, LANES:LANES + 1], 1e-30)
    o = o0 - lam_val * o1
    y = o * lax.rsqrt(jnp.mean(o * o, axis=0, keepdims=True) + NORM_EPS)
    y = (y * g_ref[...]) * (1.0 - lambda_init)
    o_ref[0] = y.T.astype(o_ref.dtype)


def _diff_attention(sl, lam, subln_g, lambda_init):
    q_t, k, v_t = sl["qT"], sl["k"], sl["vT"]
    B, H, T, _ = k.shape
    tq = KV_TILE
    nk = T // tq
    kern = functools.partial(_diff_attn_kernel, tq=tq, lambda_init=lambda_init)
    return pl.pallas_call(
        kern,
        grid=(B, H, T // tq),
        in_specs=[
            pl.BlockSpec((4, HEAD_DIM), lambda b, h, i: (0, 0)),
            pl.BlockSpec((1, 1, LANES, tq), lambda b, h, i: (b, h, 0, i)),
            pl.BlockSpec((1, 1, T, LANES), lambda b, h, i: (b, h, 0, 0)),
            pl.BlockSpec((1, 1, nk, LANES + ONES_ROWS, tq), lambda b, h, i: (b, h, 0, 0, 0)),
            pl.BlockSpec((LANES, 1), lambda b, h, i: (0, 0)),
        ],
        out_specs=pl.BlockSpec((1, tq, LANES), lambda b, h, i: (b, i, h)),
        out_shape=jax.ShapeDtypeStruct((B, T, H * LANES), BF16),
        scratch_shapes=[
            pltpu.VMEM((2, 1, tq), F32),
            pltpu.VMEM((2, LANES + ONES_ROWS, tq), F32),
            pltpu.VMEM((2, tq, tq), F32),
            pltpu.VMEM((2, tq, tq), F32),
            pltpu.VMEM((2, tq, tq), BF16),
        ],
        compiler_params=_params(("parallel", "parallel", "arbitrary")),
        name="diff_attention",
    )(lam, q_t, k, v_t, subln_g.reshape(LANES, 1))


def _dsa_select_kernel(qi_ref, ki_ref, misc_ref, tril_ref, bias_ref, key_sc, *, tq, ck, nk, topk):
    t0 = pl.program_id(1) * tq
    nvalid = (t0 + tq - 1) // ck + 1
    w = misc_ref[0, 0]
    qh = jnp.concatenate([_head_rows(qi_ref[0, h // 2], h % 2) for h in range(IDX_HEADS)], axis=1)
    t_q = t0 + _iota((ck, tq), 1)

    def causal(c):
        return (c * ck + _iota((ck, tq), 0)) <= t_q

    def fill(c, carry):
        kk = ki_ref[0, 0, pl.ds(pl.multiple_of(c * ck, ck), ck), :]
        r = _dot(kk, qh)
        score = jnp.zeros((ck, tq), F32)
        for h in range(IDX_HEADS):
            score = score + w[h:h + 1, :] * jnp.maximum(r[:, h * tq:(h + 1) * tq], 0.0)
        bits = pltpu.bitcast(score, jnp.int32)
        key = bits ^ ((bits >> 31) & jnp.int32(0x7FFFFFFF))
        key = jnp.where(score == 0.0, 0, key)
        key_sc[c] = jnp.where(causal(c), key, INT_MIN)
        return carry

    lax.fori_loop(0, nvalid, fill, 0)

    def count(pred):
        def chunk(c, acc):
            ind = pred(key_sc[c], c)
            return acc + jnp.sum(ind.reshape(8, ck // 8, tq), axis=0)
        acc = lax.fori_loop(0, nvalid, chunk, jnp.zeros((ck // 8, tq), F32))
        return jnp.sum(acc, axis=0, keepdims=True)

    kf = float(topk)
    zero = jnp.zeros((1, tq), jnp.int32)
    ans = jnp.where(count(lambda kc, c: jnp.where(kc >= zero, 1.0, 0.0)) >= kf, 0, INT_MIN)

    def bit_step(i, ans):
        cand = ans | (jnp.int32(1) << (30 - i))
        cnt = count(lambda kc, c: jnp.where(kc >= cand, 1.0, 0.0))
        return jnp.where(cnt >= kf, cand, ans)

    ans = lax.fori_loop(0, 31, bit_step, ans)

    cnt_gt = count(lambda kc, c: jnp.where(kc > ans, 1.0, 0.0))
    cnt_eq = count(lambda kc, c: jnp.where(kc == ans, jnp.where(causal(c), 1.0, 0.0), 0.0))
    need = kf - cnt_gt
    has_tie = jnp.max(cnt_eq - need) > 0.0

    @pl.when(jnp.logical_not(has_tie))
    def _():
        def emit(c, carry):
            sel = jnp.where(key_sc[c] >= ans, jnp.where(causal(c), 0.0, NEG_INF), NEG_INF)
            bias_ref[0, 0, c] = sel.astype(BF16)
            return carry
        lax.fori_loop(0, nvalid, emit, 0)

    @pl.when(has_tie)
    def _():
        def emit(c, carry):
            kc = key_sc[c]
            eq = jnp.where(kc == ans, jnp.where(causal(c), 1.0, 0.0), 0.0)
            pre = _dot(tril_ref[...], eq.astype(BF16)) + carry
            take = jnp.where(kc > ans, 1.0, jnp.where(pre < need, eq, 0.0))
            sel = jnp.where(take > 0.5, jnp.where(causal(c), 0.0, NEG_INF), NEG_INF)
            bias_ref[0, 0, c] = sel.astype(BF16)
            return carry + jnp.sum(eq, axis=0, keepdims=True)
        lax.fori_loop(0, nvalid, emit, jnp.zeros((1, tq), F32))

    def blank(c, carry):
        bias_ref[0, 0, c] = jnp.full((ck, tq), NEG_INF, BF16)
        return carry

    lax.fori_loop(nvalid, nk, blank, 0)


def _dsa_select(sl, lay):
    q_t, k, misc_t = sl["qT"], sl["k"], sl["fT"]
    B, _, T, _ = k.shape
    tq, ck = Q_TILE, KV_TILE
    nk = T // ck
    topk = min(DSA_TOPK, T // 4)
    tril = jnp.asarray(np.tril(np.ones((ck, ck), np.float32), -1), BF16)
    kern = functools.partial(_dsa_select_kernel, tq=tq, ck=ck, nk=nk, topk=topk)
    return pl.pallas_call(
        kern,
        grid=(B, T // tq),
        in_specs=[
            pl.BlockSpec((1, 2, LANES, tq), lambda b, i: (b, lay["qi"] // 2, 0, i)),
            pl.BlockSpec((1, 1, T, LANES), lambda b, i: (b, lay["ki"], 0, 0)),
            pl.BlockSpec((1, 1, LANES, tq), lambda b, i: (b, lay["misc"], 0, i)),
            pl.BlockSpec((ck, ck), lambda b, i: (0, 0)),
        ],
        out_specs=pl.BlockSpec((1, 1, nk, ck, tq), lambda b, i: (b, i, 0, 0, 0)),
        out_shape=jax.ShapeDtypeStruct((B, T // tq, nk, ck, tq), BF16),
        scratch_shapes=[pltpu.VMEM((nk, ck, tq), jnp.int32)],
        compiler_params=_params(("parallel", "parallel")),
        name="dsa_select",
    )(q_t, k, misc_t, tril)


def _store_head_pairs(o_ref, o, n_heads, tq):
    for p in range(n_heads // 2):
        even = o[0:HEAD_DIM, (2 * p) * tq:(2 * p + 1) * tq]
        odd = o[0:HEAD_DIM, (2 * p + 1) * tq:(2 * p + 2) * tq]
        pair = jnp.concatenate([even, odd], axis=0)
        o_ref[0, :, p * LANES:(p + 1) * LANES] = pair.T.astype(o_ref.dtype)


def _normalized(acc):
    return acc[0:HEAD_DIM] / jnp.maximum(acc[HEAD_DIM:HEAD_DIM + 1], 1e-30)


def _dsa_attn_kernel(q_ref, k_ref, v_ref, bias_ref, o_ref, qst, m_sc, acc_sc, sa_sc, sb_sc,
                     p_sc, *, tq, tk):
    qi = pl.program_id(1)
    H = A_HEADS
    for h in range(H):
        qst[:, h * tq:(h + 1) * tq] = _head_rows(q_ref[0, h // 2], h % 2)
    _init_flash(m_sc, acc_sc)
    nkv = (qi * tq) // tk + 1

    def put_scores(buf, j):
        k = k_ref[0, 0, pl.ds(pl.multiple_of(j * tk, tk), tk), :]
        b = bias_ref[0, 0, j].astype(F32)
        buf[...] = _dot(k, qst[...]) + jnp.concatenate([b] * H, axis=1)

    def update(buf, j):
        _online_update(buf, p_sc, v_ref[0, 0, j], m_sc, acc_sc)

    _flash_pipeline(nkv, put_scores, update, sa_sc, sb_sc)
    _store_head_pairs(o_ref, _normalized(acc_sc[...]), H, tq)


def _dsa_attention(sl, bias, lay):
    q_t, k, v_t = sl["qT"], sl["k"], sl["vT"]
    B, _, T, _ = k.shape
    tq, tk = Q_TILE, KV_TILE
    nk = T // tk
    H = A_HEADS
    N = H * tq
    kern = functools.partial(_dsa_attn_kernel, tq=tq, tk=tk)
    return pl.pallas_call(
        kern,
        grid=(B, T // tq),
        in_specs=[
            pl.BlockSpec((1, H // 2, LANES, tq), lambda b, i: (b, lay["qA"] // (H // 2), 0, i)),
            pl.BlockSpec((1, 1, T, LANES), lambda b, i: (b, lay["kA"], 0, 0)),
            pl.BlockSpec((1, 1, nk, LANES, tk), lambda b, i: (b, lay["vA"], 0, 0, 0)),
            pl.BlockSpec((1, 1, nk, tk, tq), lambda b, i: (b, i, 0, 0, 0)),
        ],
        out_specs=pl.BlockSpec((1, tq, H * HEAD_DIM), lambda b, i: (b, i, 0)),
        out_shape=jax.ShapeDtypeStruct((B, T, H * HEAD_DIM), BF16),
        scratch_shapes=[
            pltpu.VMEM((LANES, N), BF16),
            pltpu.VMEM((1, N), F32),
            pltpu.VMEM((LANES, N), F32),
            pltpu.VMEM((tk, N), F32),
            pltpu.VMEM((tk, N), F32),
            pltpu.VMEM((tk, N), BF16),
        ],
        compiler_params=_params(("parallel", "arbitrary")),
        name="dsa_attention",
    )(q_t, k, v_t, bias)


def _compress_kernel(x_ref, pe_ref, w1_ref, w2_ref, o_ref, ot_ref, *, n_ch):
    x = x_ref[0, 0]
    xt = (x + pe_ref[0, 0]).astype(BF16)
    xb = (x + pe_ref[0, 1]).astype(BF16)
    for g in range(B_KV_GROUPS):
        a = _dot(xt, w1_ref[0, g, 0])
        b = _dot(xb, w1_ref[0, g, 1])
        h = jax.nn.gelu(a + pltpu.roll(b, n_ch - 1, 0))
        r = _dot(h.astype(BF16), w2_ref[0])
        o_ref[0, 0, g] = r.astype(BF16)
        ot_ref[0, 0, g] = r.T.astype(BF16)


def _compress(flat, pe, w1, w2):
    _, B, n_ch, W = flat.shape
    G = B_KV_GROUPS
    kern = functools.partial(_compress_kernel, n_ch=n_ch)
    return pl.pallas_call(
        kern,
        grid=(2, B),
        in_specs=[
            pl.BlockSpec((1, 1, n_ch, W), lambda s, b: (s, b, 0, 0)),
            pl.BlockSpec((1, 2, 1, W), lambda s, b: (s, 0, 0, 0)),
            pl.BlockSpec((1, G, 2, W, CMP_HIDDEN), lambda s, b: (s, 0, 0, 0, 0)),
            pl.BlockSpec((1, CMP_HIDDEN, LANES), lambda s, b: (s, 0, 0)),
        ],
        out_specs=[
            pl.BlockSpec((1, 1, G, n_ch, LANES), lambda s, b: (s, b, 0, 0, 0)),
            pl.BlockSpec((1, 1, G, LANES, n_ch), lambda s, b: (s, b, 0, 0, 0)),
        ],
        out_shape=[
            jax.ShapeDtypeStruct((2, B, G, n_ch, LANES), BF16),
            jax.ShapeDtypeStruct((2, B, G, LANES, n_ch), BF16),
        ],
        compiler_params=_params(("parallel", "parallel")),
        name="nsa_compress",
    )(flat, pe, w1, w2)


def _nsa_kernel(q_ref, kc_ref, vc_ref, ks_ref, vs_ref, kw_ref, vw_ref, misc_ref, ov_ref, ex_ref,
                cz_ref, wz_ref, o_ref, qst, m_sc, acc_sc, sa_sc, sb_sc, p_sc, mw_sc, accw_sc,
                sw_sc, pw_sc, *, tq, tk, T, n_s, n_sel, gate_row0):
    g = pl.program_id(1)
    qi = pl.program_id(2)
    t0 = qi * tq
    J = B_PER_GROUP
    N = J * tq
    for j in range(J):
        qst[0:LANES, j * tq:(j + 1) * tq] = _head_rows(q_ref[0, j // 2], j % 2)
    q = qst[0:LANES, :]

    def q_time(shape):
        return t0 + (_iota(shape, 1) % tq)

    kc = kc_ref[0, 0, 0]
    n_ch = kc.shape[0]
    s_c = _dot(kc, q)
    cmp_end = _iota((n_ch, N), 0) * CMP_STRIDE + (CMP_BLOCK - 1)
    s_c = jnp.where(cmp_end <= q_time((n_ch, N)), s_c, NEG_INF)
    p_c = _softmax_keys(s_c)
    o_c = _dot(vc_ref[0, 0, 0], p_c.astype(BF16))

    psum = p_c[:, 0:tq]
    for j in range(1, J):
        psum = psum + p_c[:, j * tq:(j + 1) * tq]
    p_hi = psum.astype(BF16)
    p_lo = (psum - p_hi.astype(F32)).astype(BF16)
    imp = _dot(ov_ref[...], p_hi) + _dot(ov_ref[...], p_lo)
    rows = -(-n_s // 8) * 8
    imp = imp[0:rows]
    blk = _iota((rows, tq), 0)
    t_q = t0 + _iota((rows, tq), 1)
    cur = t_q // SLC_BLOCK
    forced = (blk == 0) | (blk == cur) | (blk == cur - 1)
    sc = jnp.where(forced, jnp.inf, imp)
    sc = jnp.where(blk * SLC_BLOCK <= t_q, sc, NEG_INF)
    rank = jnp.zeros((rows, tq), F32)
    for m in range(n_s):
        cm = sc[m:m + 1, :]
        beats = jnp.where(cm > sc, 1.0, jnp.where(cm == sc, jnp.where(blk > m, 1.0, 0.0), 0.0))
        rank = rank + beats
    drop = jnp.where(rank < float(n_sel), 0.0, 1.0)
    if rows < LANES:
        drop = jnp.concatenate([drop, jnp.ones((LANES - rows, tq), F32)], axis=0)
    drop = drop.astype(BF16)
    qst[LANES:2 * LANES, :] = jnp.concatenate([drop] * J, axis=1)

    _init_flash(m_sc, acc_sc)
    jd = t0 // tk
    off = (t0 - jd * tk) // tq
    n_off = tk // tq

    def put_scores(buf, j):
        k = ks_ref[0, 0, pl.ds(pl.multiple_of(j * tk, tk), tk), :]
        lhs = jnp.concatenate([k, ex_ref[j]], axis=1)
        cz = cz_ref[jnp.where(j == jd, off, n_off)].astype(F32)
        buf[...] = _dot(lhs, qst[...]) + jnp.concatenate([cz] * J, axis=1)

    def update(buf, j):
        _online_update(buf, p_sc, vs_ref[0, 0, j], m_sc, acc_sc)

    _flash_pipeline(jd + 1, put_scores, update, sa_sc, sb_sc)
    o_s = _normalized(acc_sc[...])

    wlen = min(WINDOW + tq, T)
    wstart = pl.multiple_of(jnp.maximum(t0 - WINDOW, 0), tq)
    kw = kw_ref[0, 0, pl.ds(wstart, wlen), :]
    n_wt = wlen // tq
    wt = jnp.maximum(qi - (n_wt - 1), 0)
    vw = jnp.concatenate([vw_ref[0, 0, wt + i] for i in range(n_wt)], axis=1)
    wz = wz_ref[jnp.minimum(qi, n_wt - 1)].astype(F32)
    sw_sc[...] = _dot(kw, q) + jnp.concatenate([wz] * J, axis=1)
    _init_flash(mw_sc, accw_sc)
    _online_update(sw_sc, pw_sc, vw, mw_sc, accw_sc)
    o_w = _normalized(accw_sc[...])

    gates = jax.nn.sigmoid(misc_ref[0, 0])

    def gate_row(c):
        parts = []
        for j in range(J):
            r0 = gate_row0 + j * 3 + c
            r1 = gate_row0 + (J + j) * 3 + c
            parts.append(jnp.where(g == 0, gates[r0:r0 + 1, :], gates[r1:r1 + 1, :]))
        return jnp.concatenate(parts, axis=1)

    o = gate_row(0) * o_c[0:HEAD_DIM] + (gate_row(1) * o_s + gate_row(2) * o_w)
    _store_head_pairs(o_ref, o, J, tq)


def _nsa_attention(sl, cmp_k, cmp_vt, lay):
    q_t, k, v_t, misc_t = sl["qT"], sl["k"], sl["vT"], sl["fT"]
    B, _, T, _ = k.shape
    G, J = B_KV_GROUPS, B_PER_GROUP
    tq, tk = Q_TILE, KV_TILE
    nk = T // tk
    n_ch = T // CMP_STRIDE
    n_c = n_ch - CMP_BLOCK // CMP_STRIDE + 1
    n_s = T // SLC_BLOCK
    n_sel = min(SLC_TOPN, n_s)
    assert n_s <= LANES
    c0 = np.arange(n_ch) * CMP_STRIDE
    s0 = np.arange(LANES) * SLC_BLOCK
    ov = ((c0[None, :] < s0[:, None] + SLC_BLOCK) & (c0[None, :] + CMP_BLOCK > s0[:, None]))
    ov = ov & (np.arange(n_ch)[None, :] < n_c) & (np.arange(LANES)[:, None] < n_s)
    ov = jnp.asarray(ov.astype(np.float32), BF16)
    pos = np.arange(T).reshape(nk, tk, 1)
    ex = (pos // SLC_BLOCK == np.arange(LANES).reshape(1, 1, LANES))
    ex = jnp.asarray(ex.astype(np.float32) * -BIG, BF16)
    n_off = tk // tq
    kp = np.arange(tk).reshape(1, tk, 1)
    tl = np.arange(tq).reshape(1, 1, tq) + np.arange(n_off + 1).reshape(n_off + 1, 1, 1) * tq
    cz = np.where((kp <= tl) | (np.arange(n_off + 1).reshape(-1, 1, 1) == n_off), 0.0, -BIG)
    cz = jnp.asarray(cz.astype(np.float32), BF16)
    wlen = min(WINDOW + tq, T)
    n_wt = wlen // tq
    assert wlen % tq == 0 and T >= wlen
    kp = np.arange(wlen).reshape(1, wlen, 1)
    tl = np.arange(tq).reshape(1, 1, tq)
    early = kp <= tl + np.arange(n_wt).reshape(n_wt, 1, 1) * tq
    late = (kp > tl) & (kp <= tl + WINDOW)
    band = np.where(np.arange(n_wt).reshape(n_wt, 1, 1) == n_wt - 1, late, early)
    wz = jnp.asarray(np.where(band, 0.0, -BIG).astype(np.float32), BF16)
    kern = functools.partial(_nsa_kernel, tq=tq, tk=tk, T=T, n_s=n_s, n_sel=n_sel,
                             gate_row0=lay["gate_row0"])
    N = J * tq
    kslab = lambda off: pl.BlockSpec((1, 1, T, LANES), lambda b, g, i: (b, off + g, 0, 0))
    return pl.pallas_call(
        kern,
        grid=(B, G, T // tq),
        in_specs=[
            pl.BlockSpec((1, 2, LANES, tq), lambda b, g, i: (b, lay["qB"] // 2 + g, 0, i)),
            pl.BlockSpec((1, 1, 1, n_ch, LANES), lambda b, g, i: (0, b, g, 0, 0)),
            pl.BlockSpec((1, 1, 1, LANES, n_ch), lambda b, g, i: (1, b, g, 0, 0)),
            kslab(lay["ks"]),
            pl.BlockSpec((1, 1, nk, LANES, tk), lambda b, g, i: (b, lay["vs"] + g, 0, 0, 0)),
            kslab(lay["kw"]),
            pl.BlockSpec((1, 1, T // tq, LANES, tq), lambda b, g, i: (b, lay["vw"] + g, 0, 0, 0)),
            pl.BlockSpec((1, 1, LANES, tq), lambda b, g, i: (b, lay["misc"], 0, i)),
            pl.BlockSpec((LANES, n_ch), lambda b, g, i: (0, 0)),
            pl.BlockSpec((nk, tk, LANES), lambda b, g, i: (0, 0, 0)),
            pl.BlockSpec((n_off + 1, tk, tq), lambda b, g, i: (0, 0, 0)),
            pl.BlockSpec((n_wt, wlen, tq), lambda b, g, i: (0, 0, 0)),
        ],
        out_specs=pl.BlockSpec((1, tq, J * HEAD_DIM), lambda b, g, i: (b, i, g)),
        out_shape=jax.ShapeDtypeStruct((B, T, B_HEADS * HEAD_DIM), BF16),
        scratch_shapes=[
            pltpu.VMEM((2 * LANES, N), BF16),
            pltpu.VMEM((1, N), F32),
            pltpu.VMEM((LANES, N), F32),
            pltpu.VMEM((tk, N), F32),
            pltpu.VMEM((tk, N), F32),
            pltpu.VMEM((tk, N), BF16),
            pltpu.VMEM((1, N), F32),
            pltpu.VMEM((LANES, N), F32),
            pltpu.VMEM((wlen, N), F32),
            pltpu.VMEM((wlen, N), BF16),
        ],
        compiler_params=_params(("parallel", "parallel", "arbitrary")),
        name="nsa_attention",
    )(q_t, cmp_k, cmp_vt, k, v_t, k, sl["vTw"], misc_t, ov, ex, cz, wz)


def _out_proj_kernel(*refs, n_in):
    h_ref = refs[0]
    o_refs = refs[1:1 + n_in]
    w_refs = refs[1 + n_in:1 + 2 * n_in]
    out_ref = refs[1 + 2 * n_in]
    acc = _dot(o_refs[0][...], w_refs[0][...])
    for i in range(1, n_in):
        acc = acc + _dot(o_refs[i][...], w_refs[i][...])
    out_ref[...] = h_ref[...] + acc


def _out_proj(h2, outs, ws, tm=512):
    N, D = h2.shape
    tm = min(tm, N)
    n_in = len(outs)
    kern = functools.partial(_out_proj_kernel, n_in=n_in)
    in_specs = [pl.BlockSpec((tm, D), lambda i: (i, 0))]
    in_specs += [pl.BlockSpec((tm, o.shape[1]), lambda i: (i, 0)) for o in outs]
    in_specs += [pl.BlockSpec(w.shape, lambda i: (0, 0)) for w in ws]
    return pl.pallas_call(
        kern,
        grid=(N // tm,),
        in_specs=in_specs,
        out_specs=pl.BlockSpec((tm, D), lambda i: (i, 0)),
        out_shape=jax.ShapeDtypeStruct((N, D), F32),
        compiler_params=_params(("parallel",)),
        name="out_proj_residual",
    )(h2, *outs, *ws)


def _rms(x, g):
    ms = jnp.mean(x * x, axis=-1, keepdims=True)
    return (x * lax.rsqrt(ms + NORM_EPS)) * g


def _mlp_kernel(h_ref, g_ref, wu_ref, wd_ref, *rest, final_norm):
    if final_norm:
        fg_ref, o_ref, xn_sc = rest
    else:
        o_ref, xn_sc = rest
    f = pl.program_id(1)

    @pl.when(f == 0)
    def _():
        x = h_ref[...]
        xn_sc[...] = _rms(x, g_ref[...]).astype(BF16)
        o_ref[...] = x

    u = _dot(xn_sc[...], wu_ref[...])
    a = jnp.square(jnp.maximum(u, 0.0)).astype(BF16)
    o_ref[...] += _dot(a, wd_ref[...])

    if final_norm:
        @pl.when(f == pl.num_programs(1) - 1)
        def _():
            o_ref[...] = _rms(o_ref[...], fg_ref[...])


def _mlp(h2, g, w_up, w_down, final_g=None, tm=1024, tf=1024):
    N, D = h2.shape
    F = w_up.shape[1]
    tm = min(tm, N)
    in_specs = [
        pl.BlockSpec((tm, D), lambda i, f: (i, 0)),
        pl.BlockSpec((1, D), lambda i, f: (0, 0)),
        pl.BlockSpec((D, tf), lambda i, f: (0, f)),
        pl.BlockSpec((tf, D), lambda i, f: (f, 0)),
    ]
    args = [h2, g.reshape(1, D), w_up, w_down]
    if final_g is not None:
        in_specs.append(pl.BlockSpec((1, D), lambda i, f: (0, 0)))
        args.append(final_g.reshape(1, D))
    return pl.pallas_call(
        functools.partial(_mlp_kernel, final_norm=final_g is not None),
        grid=(N // tm, F // tf),
        in_specs=in_specs,
        out_specs=pl.BlockSpec((tm, D), lambda i, f: (i, 0)),
        out_shape=jax.ShapeDtypeStruct((N, D), F32),
        scratch_shapes=[pltpu.VMEM((tm, D), BF16)],
        compiler_params=_params(("parallel", "arbitrary")),
        name="sq_relu_mlp",
    )(*args)


def _even_layout():
    offs = {}
    o = 0
    for name, size in (("qa", 512), ("ka", 64), ("va", 64), ("qi", 256), ("ki", 64), ("wi", 4),
                       ("qb", 512), ("kvb", 768), ("gb", 24)):
        offs[name] = o
        o += size
    kvb = lambda which, g: offs["kvb"] + (which * B_KV_GROUPS + g) * HEAD_DIM
    cols, plan, lay = [], [], {}
    n = {k: 0 for k in _KINDS}

    def add(c, roped, kind, opt=None):
        cols.append(c)
        plan.append((roped, kind, n[kind], opt))
        n[kind] += 1
        return n[kind] - 1

    lay["qA"] = n["qT"]
    for p in range(4):
        add(_pair_cols(offs["qa"] + 2 * p * 64, offs["qa"] + (2 * p + 1) * 64), True, "qT", Q_SCALE)
    lay["qB"] = n["qT"]
    for p in range(4):
        add(_pair_cols(offs["qb"] + 2 * p * 64, offs["qb"] + (2 * p + 1) * 64), True, "qT", Q_SCALE)
    lay["qi"] = n["qT"]
    for p in range(2):
        add(_pair_cols(offs["qi"] + 2 * p * 64, offs["qi"] + (2 * p + 1) * 64), True, "qT")
    lay["kA"] = add(_pair_cols(offs["ka"], offs["ka"]), True, "k")
    lay["ki"] = add(_pair_cols(offs["ki"], offs["ki"]), True, "k")
    lay["ks"] = n["k"]
    for g in range(2):
        add(_pair_cols(kvb(2, g), kvb(2, g)), True, "k")
    lay["kw"] = n["k"]
    for g in range(2):
        add(_pair_cols(kvb(4, g), kvb(4, g)), True, "k")
    lay["kc"] = add(_pair_cols(kvb(0, 0), kvb(0, 1)), True, "f")
    lay["vc"] = add(_pair_cols(kvb(1, 0), kvb(1, 1)), False, "f")
    misc = np.full(LANES, -1)
    misc[0:IDX_HEADS] = offs["wi"] + np.arange(IDX_HEADS)
    lay["gate_row0"] = 8
    misc[8:8 + 24] = offs["gb"] + np.arange(24)
    lay["misc"] = add(misc, False, "fT")
    lay["vw"] = n["vTw"]
    for g in range(2):
        add(_head_cols(kvb(5, g)), False, "vTw")
    lay["vA"] = add(_head_cols(offs["va"]), False, "vT", "ones_row_64")
    lay["vs"] = n["vT"]
    for g in range(2):
        add(_head_cols(kvb(3, g)), False, "vT", "ones_row_64")
    return np.concatenate(cols), plan, lay


def _odd_layout():
    cols, plan = [], []
    for h in range(C_HEADS):
        cols.append(_pair_cols(h * 128, h * 128 + 64))
        plan.append((True, "qT", h, Q_SCALE))
    for h in range(C_HEADS):
        cols.append(_pair_cols(1024 + h * 128, 1024 + h * 128 + 64))
        plan.append((True, "k", h, None))
    for h in range(C_HEADS):
        cols.append(2048 + h * 128 + np.arange(LANES))
        plan.append((False, "vT", h, "ones_rows_below"))
    return np.concatenate(cols), plan


def _compress_weights(pe, w1, w2):
    d = _PAIR_D
    which = _PAIR_WHICH
    pe_l = pe[:, :, d]
    pe_l = pe_l.reshape(2, 2, 1, CMP_STRIDE * LANES)
    w1r = w1.reshape(2, CMP_BLOCK, HEAD_DIM, CMP_HIDDEN)[:, :, d, :]
    per_g = []
    for g in range(B_KV_GROUPS):
        keep = jnp.asarray(which == g)[None, None, :, None]
        per_g.append(jnp.where(keep, w1r, 0.0))
    w1g = jnp.stack(per_g, axis=1)
    w1g = w1g.reshape(2, B_KV_GROUPS, 2, CMP_STRIDE * LANES, CMP_HIDDEN).astype(BF16)
    w2k = w2[0][:, d]
    w2v = w2[1][:, np.arange(LANES) % HEAD_DIM]
    w2l = jnp.stack([w2k, w2v], axis=0).astype(BF16)
    return pe_l, w1g, w2l


def _even_mixer(h, norm_g, w_in, cmp_pe, cmp_w1, cmp_w2, w_out, cos_slab, sin_slab):
    B, T, D = h.shape
    cols, plan, lay = _even_layout()
    w = _gather_cols(w_in, cols).astype(BF16)
    sl = _project(h, norm_g, w, cos_slab, sin_slab, plan)

    bias = _dsa_select(sl, lay)
    o_a = _dsa_attention(sl, bias, lay)

    n_ch = T // CMP_STRIDE
    flat = sl["f"][:, lay["kc"]:lay["vc"] + 1].reshape(B, 2, n_ch, CMP_STRIDE * LANES)
    flat = jnp.swapaxes(flat, 0, 1)
    pe_l, w1g, w2l = _compress_weights(cmp_pe, cmp_w1, cmp_w2)
    cmp_k, cmp_vt = _compress(flat, pe_l, w1g, w2l)
    o_b = _nsa_attention(sl, cmp_k, cmp_vt, lay)

    na = A_HEADS * HEAD_DIM
    wo = w_out.astype(BF16)
    h2 = _out_proj(h.reshape(B * T, D), [o_a.reshape(B * T, -1), o_b.reshape(B * T, -1)],
                   [wo[:na], wo[na:]])
    return h2.reshape(B, T, D)


def _odd_mixer(h, norm_g, w_in, lam, subln_g, w_out, cos_slab, sin_slab, lambda_init):
    B, T, D = h.shape
    cols, plan = _odd_layout()
    w = _gather_cols(w_in, cols).astype(BF16)
    sl = _project(h, norm_g, w, cos_slab, sin_slab, plan)
    o = _diff_attention(sl, lam, subln_g, lambda_init)
    h2 = _out_proj(h.reshape(B * T, D), [o.reshape(B * T, -1)], [w_out.astype(BF16)])
    return h2.reshape(B, T, D)


def kernel(x, mix_norm_g, mlp_norm_g, even_w_in, even_cmp_pe, even_cmp_w1, even_cmp_w2, even_w_out, odd_w_in, odd_lambda, odd_subln_g, odd_w_out, mlp_w_up, mlp_w_down, final_norm_g):
    B, T, D = x.shape
    depth = mix_norm_g.shape[0]
    assert depth >= 1
    cos_slab, sin_slab = _rope_slabs(T)
    h = x
    for layer in range(depth):
        if layer % 2 == 0:
            e = layer // 2
            h = _even_mixer(h, mix_norm_g[layer], even_w_in[e], even_cmp_pe[e], even_cmp_w1[e],
                            even_cmp_w2[e], even_w_out[e], cos_slab, sin_slab)
        else:
            o = layer // 2
            lambda_init = 0.8 - 0.6 * math.exp(-0.3 * layer)
            h = _odd_mixer(h, mix_norm_g[layer], odd_w_in[o], odd_lambda[o], odd_subln_g[o],
                           odd_w_out[o], cos_slab, sin_slab, lambda_init)
        h2 = _mlp(h.reshape(B * T, D), mlp_norm_g[layer], mlp_w_up[layer].astype(BF16),
                  mlp_w_down[layer].astype(BF16),
                  final_g=final_norm_g if layer == depth - 1 else None)
        h = h2.reshape(B, T, D)
    return h
```

```python
import functools
import math

import numpy as np
import jax
import jax.numpy as jnp
from jax import lax
from jax.experimental import pallas as pl
from jax.experimental.pallas import tpu as pltpu

HEAD_DIM = 64
HALF = HEAD_DIM // 2
LANES = 128
ROPE_THETA = 10000.0
NORM_EPS = 1e-6
SCALE = HEAD_DIM ** -0.5

A_HEADS = 8
IDX_HEADS = 4
DSA_TOPK = 256
B_HEADS = 8
B_KV_GROUPS = 2
B_PER_GROUP = B_HEADS // B_KV_GROUPS
CMP_BLOCK = 32
CMP_STRIDE = 16
CMP_HIDDEN = 256
SLC_BLOCK = 64
SLC_TOPN = 16
WINDOW = 512
C_HEADS = 8

KV_TILE = 512
Q_TILE = 128
DSA_Q_TILE = 256

LOG2E = math.log2(math.e)
Q_SCALE = SCALE * LOG2E
ONES_ROWS = 16
BIG = 2.0 ** 100

NEG_INF = float("-inf")
M_FLOOR = -1e30
INT_MIN = -(2 ** 31)

VMEM_LIMIT = 56 * 1024 * 1024

BF16 = jnp.bfloat16
F32 = jnp.float32


def _dot(a, b):
    return jnp.dot(a, b, preferred_element_type=F32)


def _dot_tn(a, b):
    return lax.dot_general(a, b, (((0,), (0,)), ((), ())), preferred_element_type=F32)


def _params(sem):
    return pltpu.CompilerParams(dimension_semantics=sem, vmem_limit_bytes=VMEM_LIMIT)


def _iota(shape, axis):
    return lax.broadcasted_iota(jnp.int32, shape, axis)


def _pair_cols(base_a, base_b):
    lane = np.arange(LANES)
    half = lane // 64
    which = (lane % 64) // HALF
    i = lane % HALF
    base = np.where(which == 0, base_a, base_b)
    return base + half * HALF + i


def _dup_cols(base):
    return base + np.arange(LANES) % HEAD_DIM


def _head_cols(base):
    lane = np.arange(LANES)
    return np.where(lane < HEAD_DIM, base + lane, -1)


_PAIR_D = _pair_cols(0, 0)
_PAIR_WHICH = (np.arange(LANES) % 64) // HALF


def _gather_cols(w, cols):
    cols = np.asarray(cols)
    safe = np.where(cols >= 0, cols, 0)
    g = jnp.take(w, jnp.asarray(safe, dtype=jnp.int32), axis=1)
    return jnp.where(jnp.asarray(cols >= 0)[None, :], g, 0.0)


def _rope_slabs(T):
    inv = 1.0 / (ROPE_THETA ** (jnp.arange(0, HEAD_DIM, 2, dtype=F32) / HEAD_DIM))
    ang = jnp.arange(T, dtype=F32)[:, None] * inv[None, :]
    cos, sin = jnp.cos(ang), jnp.sin(ang)
    cos_slab = jnp.tile(cos, (1, 4))
    sin_slab = jnp.concatenate([-sin, -sin, sin, sin], axis=1)
    return cos_slab, sin_slab


_KINDS = ("qT", "k", "vT", "vTw", "f", "fT")


def _proj_kernel(x_ref, g_ref, w_ref, cos_ref, sin_ref, *out_refs, plan, kinds, chunk):
    outs = dict(zip(kinds, out_refs))
    x = x_ref[0]
    ms = jnp.mean(x * x, axis=-1, keepdims=True)
    xn = ((x * lax.rsqrt(ms + NORM_EPS)) * g_ref[...]).astype(BF16)
    cos = cos_ref[...]
    sin = sin_ref[...]
    n = len(plan)
    for c0 in range(0, n, chunk):
        c1 = min(c0 + chunk, n)
        r = _dot(xn, w_ref[:, c0 * LANES:c1 * LANES])
        for s in range(c0, c1):
            roped, kind, idx, opt = plan[s]
            y = r[:, (s - c0) * LANES:(s - c0 + 1) * LANES]
            if roped:
                y = y * cos + pltpu.roll(y, 64, 1) * sin
            if kind == "qT":
                if opt is not None:
                    y = y * opt
                outs[kind][0, idx] = y.T.astype(BF16)
            elif kind == "k":
                outs[kind][0, idx] = y.astype(BF16)
            elif kind == "vT":
                if opt == "ones_row_64":
                    y = jnp.where(_iota(y.shape, 1) == HEAD_DIM, 1.0, y)
                    outs[kind][0, idx, 0] = y.T.astype(BF16)
                else:
                    outs[kind][0, idx, 0, 0:LANES, :] = y.T.astype(BF16)
                    outs[kind][0, idx, 0, LANES:, :] = jnp.ones((ONES_ROWS, y.shape[0]), BF16)
            elif kind == "vTw":
                y_t = jnp.where(_iota(y.shape, 1) == HEAD_DIM, 1.0, y).T.astype(BF16)
                for sub in range(y.shape[0] // Q_TILE):
                    outs[kind][0, idx, sub] = y_t[:, sub * Q_TILE:(sub + 1) * Q_TILE]
            elif kind == "f":
                outs[kind][0, idx] = y
            else:
                outs[kind][0, idx] = y.T


def _project(x, g, w, cos_slab, sin_slab, plan, chunk=4):
    B, T, D = x.shape
    tm = KV_TILE
    assert T % tm == 0
    count = {k: sum(1 for p in plan if p[1] == k) for k in _KINDS}
    kinds = tuple(k for k in _KINDS if count[k])
    out_specs, out_shape = [], []
    for k in kinds:
        n = count[k]
        if k in ("qT", "fT"):
            out_specs.append(pl.BlockSpec((1, n, LANES, tm), lambda b, i: (b, 0, 0, i)))
            out_shape.append(jax.ShapeDtypeStruct((B, n, LANES, T), BF16 if k == "qT" else F32))
        elif k in ("k", "f"):
            out_specs.append(pl.BlockSpec((1, n, tm, LANES), lambda b, i: (b, 0, i, 0)))
            out_shape.append(jax.ShapeDtypeStruct((B, n, T, LANES), BF16 if k == "k" else F32))
        elif k == "vTw":
            sub = tm // Q_TILE
            out_specs.append(pl.BlockSpec((1, n, sub, LANES, Q_TILE), lambda b, i: (b, 0, i, 0, 0)))
            out_shape.append(jax.ShapeDtypeStruct((B, n, T // Q_TILE, LANES, Q_TILE), BF16))
        else:
            wide = any(p[1] == "vT" and p[3] != "ones_row_64" for p in plan)
            rows = LANES + ONES_ROWS if wide else LANES
            out_specs.append(pl.BlockSpec((1, n, 1, rows, tm), lambda b, i: (b, 0, i, 0, 0)))
            out_shape.append(jax.ShapeDtypeStruct((B, n, T // tm, rows, tm), BF16))
    kern = functools.partial(_proj_kernel, plan=tuple(plan), kinds=kinds, chunk=chunk)
    outs = pl.pallas_call(
        kern,
        grid=(B, T // tm),
        in_specs=[
            pl.BlockSpec((1, tm, D), lambda b, i: (b, i, 0)),
            pl.BlockSpec((1, D), lambda b, i: (0, 0)),
            pl.BlockSpec((D, len(plan) * LANES), lambda b, i: (0, 0)),
            pl.BlockSpec((tm, LANES), lambda b, i: (i, 0)),
            pl.BlockSpec((tm, LANES), lambda b, i: (i, 0)),
        ],
        out_specs=out_specs,
        out_shape=out_shape,
        compiler_params=_params(("parallel", "parallel")),
        name="norm_proj_rope",
    )(x, g.reshape(1, D), w, cos_slab, sin_slab)
    return dict(zip(kinds, outs))


def _head_rows(slab_t, which):
    row = _iota(slab_t.shape, 0)
    keep = ((row % 64) // HALF) == which
    return jnp.where(keep, slab_t, jnp.zeros_like(slab_t))


def _tree(op, xs):
    while len(xs) > 1:
        xs = [op(xs[i], xs[i + 1]) if i + 1 < len(xs) else xs[i] for i in range(0, len(xs), 2)]
    return xs[0]


def _fold_rows_max(x, ways=4):
    rows, n = x.shape
    per = rows // ways
    parts = [jnp.max(x[i * per:(i + 1) * per].reshape(per // 8, 8, n), axis=0) for i in range(ways)]
    return _tree(jnp.maximum, parts)


ROW_BLOCK = 64


def _online_update(s_sc, p_sc, v_aug, m_ref, acc_ref):
    tk, n = s_sc.shape
    m_prev = m_ref[...]
    m_tile = jnp.max(_fold_rows_max(s_sc[...]), axis=0, keepdims=True)
    m_new = jnp.maximum(m_prev, m_tile)
    alpha = jnp.exp2(m_prev - m_new)
    for r in range(tk // ROW_BLOCK):
        rows = slice(r * ROW_BLOCK, (r + 1) * ROW_BLOCK)
        p_sc[rows, :] = jnp.exp2(s_sc[rows, :] - m_new).astype(BF16)
    acc_ref[...] = alpha * acc_ref[...] + _dot(v_aug, p_sc[...])
    m_ref[...] = m_new


def _flash_pipeline(n, put_scores, update, buf_a, buf_b, put_last=None):
    if put_last is None:
        put_scores(buf_a, 0)
    else:
        @pl.when(n > 0)
        def _():
            put_scores(buf_a, 0)

    def pair(p, carry):
        j = 2 * p
        put_scores(buf_b, j + 1)
        update(buf_a, j)

        @pl.when(j + 2 < n)
        def _():
            put_scores(buf_a, j + 2)
            update(buf_b, j + 1)

        return carry

    lax.fori_loop(0, n // 2, pair, 0)
    in_b = jnp.logical_and(n > 0, n % 2 == 0)
    in_a = n % 2 == 1

    def finish(cur, other):
        if put_last is not None:
            put_last(other, n)
        update(cur, n - 1)
        if put_last is not None:
            update(other, n)

    pl.when(in_b)(lambda: finish(buf_b, buf_a))
    pl.when(in_a)(lambda: finish(buf_a, buf_b))
    if put_last is not None:
        @pl.when(n == 0)
        def _():
            put_last(buf_a, 0)
            update(buf_a, 0)


def _softmax_keys(s):
    m = jnp.max(s, axis=0, keepdims=True)
    m = jnp.where(m > NEG_INF, m, 0.0)
    e = jnp.exp2(s - m)
    return e / jnp.maximum(jnp.sum(e, axis=0, keepdims=True), 1e-30)


def _init_flash(m_ref, acc_ref):
    m_ref[...] = jnp.full(m_ref.shape, M_FLOOR, F32)
    acc_ref[...] = jnp.zeros(acc_ref.shape, F32)


def _diff_attn_kernel(lam_ref, q_ref, k_ref, v_ref, g_ref, o_ref, m_sc, acc_sc, sa_sc, sb_sc,
                      p_sc, *, tq, lambda_init):
    qi = pl.program_id(2)
    q_t = q_ref[0, 0]
    qs = [_head_rows(q_t, c) for c in range(2)]
    for c in range(2):
        _init_flash(m_sc.at[c], acc_sc.at[c])

    def put_scores(buf, j, masked=False):
        k = k_ref[0, 0, pl.ds(pl.multiple_of(j * tq, tq), tq), :]
        for c in range(2):
            s = _dot(k, qs[c])
            if masked:
                s = jnp.where(_iota(s.shape, 0) <= _iota(s.shape, 1), s, NEG_INF)
            buf[c] = s

    def put_diagonal(buf, j):
        put_scores(buf, j, masked=True)

    def update(buf, j):
        v_aug = v_ref[0, 0, j]
        for c in range(2):
            _online_update(buf.at[c], p_sc.at[c], v_aug, m_sc.at[c], acc_sc.at[c])

    _flash_pipeline(qi, put_scores, update, sa_sc, sb_sc, put_last=put_diagonal)

    lam = lam_ref[...]
    s01 = jnp.sum(lam[0:1] * lam[1:2], axis=-1, keepdims=True)
    s23 = jnp.sum(lam[2:3] * lam[3:4], axis=-1, keepdims=True)
    lam_val = jnp.exp(s01) - jnp.exp(s23) + lambda_init
    o0 = acc_sc[0, 0:LANES] / jnp.maximum(acc_sc[0, LANES:LANES + 1], 1e-30)
    o1 = acc_sc[1, 0:LANES] / jnp.maximum(acc_sc[1, LANES:LANES + 1], 1e-30)
    o = o0 - lam_val * o1
    y = o * lax.rsqrt(jnp.mean(o * o, axis=0, keepdims=True) + NORM_EPS)
    y = (y * g_ref[...]) * (1.0 - lambda_init)
    o_ref[0] = y.T.astype(o_ref.dtype)


def _diff_attention(sl, lam, subln_g, lambda_init):
    q_t, k, v_t = sl["qT"], sl["k"], sl["vT"]
    B, H, T, _ = k.shape
    tq = KV_TILE
    nk = T // tq
    kern = functools.partial(_diff_attn_kernel, tq=tq, lambda_init=lambda_init)
    return pl.pallas_call(
        kern,
        grid=(B, H, T // tq),
        in_specs=[
            pl.BlockSpec((4, HEAD_DIM), lambda b, h, i: (0, 0)),
            pl.BlockSpec((1, 1, LANES, tq), lambda b, h, i: (b, h, 0, i)),
            pl.BlockSpec((1, 1, T, LANES), lambda b, h, i: (b, h, 0, 0)),
            pl.BlockSpec((1, 1, nk, LANES + ONES_ROWS, tq), lambda b, h, i: (b, h, 0, 0, 0)),
            pl.BlockSpec((LANES, 1), lambda b, h, i: (0, 0)),
        ],
        out_specs=pl.BlockSpec((1, tq, LANES), lambda b, h, i: (b, i, h)),
        out_shape=jax.ShapeDtypeStruct((B, T, H * LANES), BF16),
        scratch_shapes=[
            pltpu.VMEM((2, 1, tq), F32),
            pltpu.VMEM((2, LANES + ONES_ROWS, tq), F32),
            pltpu.VMEM((2, tq, tq), F32),
            pltpu.VMEM((2, tq, tq), F32),
            pltpu.VMEM((2, tq, tq), BF16),
        ],
        compiler_params=_params(("parallel", "parallel", "arbitrary")),
        name="diff_attention",
    )(lam, q_t, k, v_t, subln_g.reshape(LANES, 1))


def _dsa_select_kernel(qi_ref, ki_ref, misc_ref, tril_ref, bias_ref, key_sc, *, tq, ck, nk, topk):
    t0 = pl.program_id(1) * tq
    nvalid = (t0 + tq - 1) // ck + 1
    w = misc_ref[0, 0]
    qh = jnp.concatenate([_head_rows(qi_ref[0, h // 2], h % 2) for h in range(IDX_HEADS)], axis=1)
    t_q = t0 + _iota((ck, tq), 1)

    def causal(c):
        return (c * ck + _iota((ck, tq), 0)) <= t_q

    def fill(c, carry):
        kk = ki_ref[0, 0, pl.ds(pl.multiple_of(c * ck, ck), ck), :]
        r = _dot(kk, qh)
        score = jnp.zeros((ck, tq), F32)
        for h in range(IDX_HEADS):
            score = score + w[h:h + 1, :] * jnp.maximum(r[:, h * tq:(h + 1) * tq], 0.0)
        bits = pltpu.bitcast(score, jnp.int32)
        key = bits ^ ((bits >> 31) & jnp.int32(0x7FFFFFFF))
        key = jnp.where(score == 0.0, 0, key)
        key_sc[c] = jnp.where(causal(c), key, INT_MIN)
        return carry

    lax.fori_loop(0, nvalid, fill, 0)

    def count(pred):
        def chunk(c, acc):
            ind = pred(key_sc[c], c)
            return acc + jnp.sum(ind.reshape(8, ck // 8, tq), axis=0)
        acc = lax.fori_loop(0, nvalid, chunk, jnp.zeros((ck // 8, tq), F32))
        return jnp.sum(acc, axis=0, keepdims=True)

    kf = float(topk)
    zero = jnp.zeros((1, tq), jnp.int32)
    ans = jnp.where(count(lambda kc, c: jnp.where(kc >= zero, 1.0, 0.0)) >= kf, 0, INT_MIN)

    def bit_step(i, ans):
        cand = ans | (jnp.int32(1) << (30 - i))
        cnt = count(lambda kc, c: jnp.where(kc >= cand, 1.0, 0.0))
        return jnp.where(cnt >= kf, cand, ans)

    ans = lax.fori_loop(0, 31, bit_step, ans)

    cnt_gt = count(lambda kc, c: jnp.where(kc > ans, 1.0, 0.0))
    cnt_eq = count(lambda kc, c: jnp.where(kc == ans, jnp.where(causal(c), 1.0, 0.0), 0.0))
    need = kf - cnt_gt
    has_tie = jnp.max(cnt_eq - need) > 0.0

    @pl.when(jnp.logical_not(has_tie))
    def _():
        def emit(c, carry):
            sel = jnp.where(key_sc[c] >= ans, jnp.where(causal(c), 0.0, NEG_INF), NEG_INF)
            bias_ref[0, 0, c] = sel.astype(BF16)
            return carry
        lax.fori_loop(0, nvalid, emit, 0)

    @pl.when(has_tie)
    def _():
        def emit(c, carry):
            kc = key_sc[c]
            eq = jnp.where(kc == ans, jnp.where(causal(c), 1.0, 0.0), 0.0)
            pre = _dot(tril_ref[...], eq.astype(BF16)) + carry
            take = jnp.where(kc > ans, 1.0, jnp.where(pre < need, eq, 0.0))
            sel = jnp.where(take > 0.5, jnp.where(causal(c), 0.0, NEG_INF), NEG_INF)
            bias_ref[0, 0, c] = sel.astype(BF16)
            return carry + jnp.sum(eq, axis=0, keepdims=True)
        lax.fori_loop(0, nvalid, emit, jnp.zeros((1, tq), F32))

    def blank(c, carry):
        bias_ref[0, 0, c] = jnp.full((ck, tq), NEG_INF, BF16)
        return carry

    lax.fori_loop(nvalid, nk, blank, 0)


def _dsa_select(sl, lay):
    q_t, k, misc_t = sl["qT"], sl["k"], sl["fT"]
    B, _, T, _ = k.shape
    tq, ck = Q_TILE, KV_TILE
    nk = T // ck
    topk = min(DSA_TOPK, T // 4)
    tril = jnp.asarray(np.tril(np.ones((ck, ck), np.float32), -1), BF16)
    kern = functools.partial(_dsa_select_kernel, tq=tq, ck=ck, nk=nk, topk=topk)
    return pl.pallas_call(
        kern,
        grid=(B, T // tq),
        in_specs=[
            pl.BlockSpec((1, 2, LANES, tq), lambda b, i: (b, lay["qi"] // 2, 0, i)),
            pl.BlockSpec((1, 1, T, LANES), lambda b, i: (b, lay["ki"], 0, 0)),
            pl.BlockSpec((1, 1, LANES, tq), lambda b, i: (b, lay["misc"], 0, i)),
            pl.BlockSpec((ck, ck), lambda b, i: (0, 0)),
        ],
        out_specs=pl.BlockSpec((1, 1, nk, ck, tq), lambda b, i: (b, i, 0, 0, 0)),
        out_shape=jax.ShapeDtypeStruct((B, T // tq, nk, ck, tq), BF16),
        scratch_shapes=[pltpu.VMEM((nk, ck, tq), jnp.int32)],
        compiler_params=_params(("parallel", "parallel")),
        name="dsa_select",
    )(q_t, k, misc_t, tril)


def _store_head_pairs(o_ref, o, n_heads, tq):
    for p in range(n_heads // 2):
        even = o[0:HEAD_DIM, (2 * p) * tq:(2 * p + 1) * tq]
        odd = o[0:HEAD_DIM, (2 * p + 1) * tq:(2 * p + 2) * tq]
        pair = jnp.concatenate([even, odd], axis=0)
        o_ref[0, :, p * LANES:(p + 1) * LANES] = pair.T.astype(o_ref.dtype)


def _normalized(acc):
    return acc[0:HEAD_DIM] / jnp.maximum(acc[HEAD_DIM:HEAD_DIM + 1], 1e-30)


def _dsa_attn_kernel(q_ref, k_ref, v_ref, bias_ref, o_ref, qst, m_sc, acc_sc, sa_sc, sb_sc,
                     p_sc, *, tq, tk):
    qi = pl.program_id(1)
    H = A_HEADS
    for h in range(H):
        qst[:, h * tq:(h + 1) * tq] = _head_rows(q_ref[0, h // 2], h % 2)
    _init_flash(m_sc, acc_sc)
    nkv = (qi * tq) // tk + 1

    def put_scores(buf, j):
        k = k_ref[0, 0, pl.ds(pl.multiple_of(j * tk, tk), tk), :]
        parts = [bias_ref[0, s, j].astype(F32) for s in range(bias_ref.shape[1])]
        b = parts[0] if len(parts) == 1 else jnp.concatenate(parts, axis=1)
        buf[...] = _dot(k, qst[...]) + jnp.concatenate([b] * H, axis=1)

    def update(buf, j):
        _online_update(buf, p_sc, v_ref[0, 0, j], m_sc, acc_sc)

    _flash_pipeline(nkv, put_scores, update, sa_sc, sb_sc)
    _store_head_pairs(o_ref, _normalized(acc_sc[...]), H, tq)


def _dsa_attention(sl, bias, lay):
    q_t, k, v_t = sl["qT"], sl["k"], sl["vT"]
    B, _, T, _ = k.shape
    tq, tk = DSA_Q_TILE, KV_TILE
    nk = T // tk
    H = A_HEADS
    N = H * tq
    sel_tq = bias.shape[4]
    nsub = tq // sel_tq
    kern = functools.partial(_dsa_attn_kernel, tq=tq, tk=tk)
    return pl.pallas_call(
        kern,
        grid=(B, T // tq),
        in_specs=[
            pl.BlockSpec((1, H // 2, LANES, tq), lambda b, i: (b, lay["qA"] // (H // 2), 0, i)),
            pl.BlockSpec((1, 1, T, LANES), lambda b, i: (b, lay["kA"], 0, 0)),
            pl.BlockSpec((1, 1, nk, LANES, tk), lambda b, i: (b, lay["vA"], 0, 0, 0)),
            pl.BlockSpec((1, nsub, nk, tk, sel_tq), lambda b, i: (b, i, 0, 0, 0)),
        ],
        out_specs=pl.BlockSpec((1, tq, H * HEAD_DIM), lambda b, i: (b, i, 0)),
        out_shape=jax.ShapeDtypeStruct((B, T, H * HEAD_DIM), BF16),
        scratch_shapes=[
            pltpu.VMEM((LANES, N), BF16),
            pltpu.VMEM((1, N), F32),
            pltpu.VMEM((LANES, N), F32),
            pltpu.VMEM((tk, N), F32),
            pltpu.VMEM((tk, N), F32),
            pltpu.VMEM((tk, N), BF16),
        ],
        compiler_params=_params(("parallel", "arbitrary")),
        name="dsa_attention",
    )(q_t, k, v_t, bias)


def _compress_kernel(x_ref, pe_ref, w1_ref, w2_ref, o_ref, ot_ref, *, n_ch):
    x = x_ref[0, 0]
    xt = (x + pe_ref[0, 0]).astype(BF16)
    xb = (x + pe_ref[0, 1]).astype(BF16)
    for g in range(B_KV_GROUPS):
        a = _dot(xt, w1_ref[0, g, 0])
        b = _dot(xb, w1_ref[0, g, 1])
        h = jax.nn.gelu(a + pltpu.roll(b, n_ch - 1, 0))
        r = _dot(h.astype(BF16), w2_ref[0])
        o_ref[0, 0, g] = r.astype(BF16)
        ot_ref[0, 0, g] = r.T.astype(BF16)


def _compress(flat, pe, w1, w2):
    _, B, n_ch, W = flat.shape
    G = B_KV_GROUPS
    kern = functools.partial(_compress_kernel, n_ch=n_ch)
    return pl.pallas_call(
        kern,
        grid=(2, B),
        in_specs=[
            pl.BlockSpec((1, 1, n_ch, W), lambda s, b: (s, b, 0, 0)),
            pl.BlockSpec((1, 2, 1, W), lambda s, b: (s, 0, 0, 0)),
            pl.BlockSpec((1, G, 2, W, CMP_HIDDEN), lambda s, b: (s, 0, 0, 0, 0)),
            pl.BlockSpec((1, CMP_HIDDEN, LANES), lambda s, b: (s, 0, 0)),
        ],
        out_specs=[
            pl.BlockSpec((1, 1, G, n_ch, LANES), lambda s, b: (s, b, 0, 0, 0)),
            pl.BlockSpec((1, 1, G, LANES, n_ch), lambda s, b: (s, b, 0, 0, 0)),
        ],
        out_shape=[
            jax.ShapeDtypeStruct((2, B, G, n_ch, LANES), BF16),
            jax.ShapeDtypeStruct((2, B, G, LANES, n_ch), BF16),
        ],
        compiler_params=_params(("parallel", "parallel")),
        name="nsa_compress",
    )(flat, pe, w1, w2)


def _nsa_kernel(q_ref, kc_ref, vc_ref, ks_ref, vs_ref, kw_ref, vw_ref, misc_ref, ov_ref, ex_ref,
                cz_ref, wz_ref, o_ref, qst, m_sc, acc_sc, sa_sc, sb_sc, p_sc, mw_sc, accw_sc,
                sw_sc, pw_sc, *, tq, tk, T, n_s, n_sel, gate_row0):
    g = pl.program_id(1)
    qi = pl.program_id(2)
    t0 = qi * tq
    J = B_PER_GROUP
    N = J * tq
    for j in range(J):
        qst[0:LANES, j * tq:(j + 1) * tq] = _head_rows(q_ref[0, j // 2], j % 2)
    q = qst[0:LANES, :]

    def q_time(shape):
        return t0 + (_iota(shape, 1) % tq)

    kc = kc_ref[0, 0, 0]
    n_ch = kc.shape[0]
    s_c = _dot(kc, q)
    cmp_end = _iota((n_ch, N), 0) * CMP_STRIDE + (CMP_BLOCK - 1)
    s_c = jnp.where(cmp_end <= q_time((n_ch, N)), s_c, NEG_INF)
    p_c = _softmax_keys(s_c)
    o_c = _dot(vc_ref[0, 0, 0], p_c.astype(BF16))

    psum = p_c[:, 0:tq]
    for j in range(1, J):
        psum = psum + p_c[:, j * tq:(j + 1) * tq]
    p_hi = psum.astype(BF16)
    p_lo = (psum - p_hi.astype(F32)).astype(BF16)
    imp = _dot(ov_ref[...], p_hi) + _dot(ov_ref[...], p_lo)
    rows = -(-n_s // 8) * 8
    imp = imp[0:rows]
    blk = _iota((rows, tq), 0)
    t_q = t0 + _iota((rows, tq), 1)
    cur = t_q // SLC_BLOCK
    forced = (blk == 0) | (blk == cur) | (blk == cur - 1)
    sc = jnp.where(forced, jnp.inf, imp)
    sc = jnp.where(blk * SLC_BLOCK <= t_q, sc, NEG_INF)
    rank = jnp.zeros((rows, tq), F32)
    for m in range(n_s):
        cm = sc[m:m + 1, :]
        beats = jnp.where(cm > sc, 1.0, jnp.where(cm == sc, jnp.where(blk > m, 1.0, 0.0), 0.0))
        rank = rank + beats
    drop = jnp.where(rank < float(n_sel), 0.0, 1.0)
    if rows < LANES:
        drop = jnp.concatenate([drop, jnp.ones((LANES - rows, tq), F32)], axis=0)
    drop = drop.astype(BF16)
    qst[LANES:2 * LANES, :] = jnp.concatenate([drop] * J, axis=1)

    _init_flash(m_sc, acc_sc)
    jd = t0 // tk
    off = (t0 - jd * tk) // tq
    n_off = tk // tq

    def put_scores(buf, j):
        k = ks_ref[0, 0, pl.ds(pl.multiple_of(j * tk, tk), tk), :]
        lhs = jnp.concatenate([k, ex_ref[j]], axis=1)
        cz = cz_ref[jnp.where(j == jd, off, n_off)].astype(F32)
        buf[...] = _dot(lhs, qst[...]) + jnp.concatenate([cz] * J, axis=1)

    def update(buf, j):
        _online_update(buf, p_sc, vs_ref[0, 0, j], m_sc, acc_sc)

    _flash_pipeline(jd + 1, put_scores, update, sa_sc, sb_sc)
    o_s = _normalized(acc_sc[...])

    wlen = min(WINDOW + tq, T)
    wstart = pl.multiple_of(jnp.maximum(t0 - WINDOW, 0), tq)
    kw = kw_ref[0, 0, pl.ds(wstart, wlen), :]
    n_wt = wlen // tq
    wt = jnp.maximum(qi - (n_wt - 1), 0)
    vw = jnp.concatenate([vw_ref[0, 0, wt + i] for i in range(n_wt)], axis=1)
    wz = wz_ref[jnp.minimum(qi, n_wt - 1)].astype(F32)
    sw_sc[...] = _dot(kw, q) + jnp.concatenate([wz] * J, axis=1)
    _init_flash(mw_sc, accw_sc)
    _online_update(sw_sc, pw_sc, vw, mw_sc, accw_sc)
    o_w = _normalized(accw_sc[...])

    gates = jax.nn.sigmoid(misc_ref[0, 0])

    def gate_row(c):
        parts = []
        for j in range(J):
            r0 = gate_row0 + j * 3 + c
            r1 = gate_row0 + (J + j) * 3 + c
            parts.append(jnp.where(g == 0, gates[r0:r0 + 1, :], gates[r1:r1 + 1, :]))
        return jnp.concatenate(parts, axis=1)

    o = gate_row(0) * o_c[0:HEAD_DIM] + (gate_row(1) * o_s + gate_row(2) * o_w)
    _store_head_pairs(o_ref, o, J, tq)


def _nsa_attention(sl, cmp_k, cmp_vt, lay):
    q_t, k, v_t, misc_t = sl["qT"], sl["k"], sl["vT"], sl["fT"]
    B, _, T, _ = k.shape
    G, J = B_KV_GROUPS, B_PER_GROUP
    tq, tk = Q_TILE, KV_TILE
    nk = T // tk
    n_ch = T // CMP_STRIDE
    n_c = n_ch - CMP_BLOCK // CMP_STRIDE + 1
    n_s = T // SLC_BLOCK
    n_sel = min(SLC_TOPN, n_s)
    assert n_s <= LANES
    c0 = np.arange(n_ch) * CMP_STRIDE
    s0 = np.arange(LANES) * SLC_BLOCK
    ov = ((c0[None, :] < s0[:, None] + SLC_BLOCK) & (c0[None, :] + CMP_BLOCK > s0[:, None]))
    ov = ov & (np.arange(n_ch)[None, :] < n_c) & (np.arange(LANES)[:, None] < n_s)
    ov = jnp.asarray(ov.astype(np.float32), BF16)
    pos = np.arange(T).reshape(nk, tk, 1)
    ex = (pos // SLC_BLOCK == np.arange(LANES).reshape(1, 1, LANES))
    ex = jnp.asarray(ex.astype(np.float32) * -BIG, BF16)
    n_off = tk // tq
    kp = np.arange(tk).reshape(1, tk, 1)
    tl = np.arange(tq).reshape(1, 1, tq) + np.arange(n_off + 1).reshape(n_off + 1, 1, 1) * tq
    cz = np.where((kp <= tl) | (np.arange(n_off + 1).reshape(-1, 1, 1) == n_off), 0.0, -BIG)
    cz = jnp.asarray(cz.astype(np.float32), BF16)
    wlen = min(WINDOW + tq, T)
    n_wt = wlen // tq
    assert wlen % tq == 0 and T >= wlen
    kp = np.arange(wlen).reshape(1, wlen, 1)
    tl = np.arange(tq).reshape(1, 1, tq)
    early = kp <= tl + np.arange(n_wt).reshape(n_wt, 1, 1) * tq
    late = (kp > tl) & (kp <= tl + WINDOW)
    band = np.where(np.arange(n_wt).reshape(n_wt, 1, 1) == n_wt - 1, late, early)
    wz = jnp.asarray(np.where(band, 0.0, -BIG).astype(np.float32), BF16)
    kern = functools.partial(_nsa_kernel, tq=tq, tk=tk, T=T, n_s=n_s, n_sel=n_sel,
                             gate_row0=lay["gate_row0"])
    N = J * tq
    kslab = lambda off: pl.BlockSpec((1, 1, T, LANES), lambda b, g, i: (b, off + g, 0, 0))
    return pl.pallas_call(
        kern,
        grid=(B, G, T // tq),
        in_specs=[
            pl.BlockSpec((1, 2, LANES, tq), lambda b, g, i: (b, lay["qB"] // 2 + g, 0, i)),
            pl.BlockSpec((1, 1, 1, n_ch, LANES), lambda b, g, i: (0, b, g, 0, 0)),
            pl.BlockSpec((1, 1, 1, LANES, n_ch), lambda b, g, i: (1, b, g, 0, 0)),
            kslab(lay["ks"]),
            pl.BlockSpec((1, 1, nk, LANES, tk), lambda b, g, i: (b, lay["vs"] + g, 0, 0, 0)),
            kslab(lay["kw"]),
            pl.BlockSpec((1, 1, T // tq, LANES, tq), lambda b, g, i: (b, lay["vw"] + g, 0, 0, 0)),
            pl.BlockSpec((1, 1, LANES, tq), lambda b, g, i: (b, lay["misc"], 0, i)),
            pl.BlockSpec((LANES, n_ch), lambda b, g, i: (0, 0)),
            pl.BlockSpec((nk, tk, LANES), lambda b, g, i: (0, 0, 0)),
            pl.BlockSpec((n_off + 1, tk, tq), lambda b, g, i: (0, 0, 0)),
            pl.BlockSpec((n_wt, wlen, tq), lambda b, g, i: (0, 0, 0)),
        ],
        out_specs=pl.BlockSpec((1, tq, J * HEAD_DIM), lambda b, g, i: (b, i, g)),
        out_shape=jax.ShapeDtypeStruct((B, T, B_HEADS * HEAD_DIM), BF16),
        scratch_shapes=[
            pltpu.VMEM((2 * LANES, N), BF16),
            pltpu.VMEM((1, N), F32),
            pltpu.VMEM((LANES, N), F32),
            pltpu.VMEM((tk, N), F32),
            pltpu.VMEM((tk, N), F32),
            pltpu.VMEM((tk, N), BF16),
            pltpu.VMEM((1, N), F32),
            pltpu.VMEM((LANES, N), F32),
            pltpu.VMEM((wlen, N), F32),
            pltpu.VMEM((wlen, N), BF16),
        ],
        compiler_params=_params(("parallel", "parallel", "arbitrary")),
        name="nsa_attention",
    )(q_t, cmp_k, cmp_vt, k, v_t, k, sl["vTw"], misc_t, ov, ex, cz, wz)


def _rms(x, g):
    ms = jnp.mean(x * x, axis=-1, keepdims=True)
    return (x * lax.rsqrt(ms + NORM_EPS)) * g


def _layer_tail_kernel(*refs, n_in, final_norm):
    h_ref = refs[0]
    o_refs = refs[1:1 + n_in]
    w_refs = refs[1 + n_in:1 + 2 * n_in]
    g_ref, wu_ref, wd_ref = refs[1 + 2 * n_in:4 + 2 * n_in]
    rest = refs[4 + 2 * n_in:]
    if final_norm:
        fg_ref, out_ref, xn_sc = rest
    else:
        out_ref, xn_sc = rest
    f = pl.program_id(1)

    @pl.when(f == 0)
    def _():
        mix = _dot(o_refs[0][...], w_refs[0][...])
        for i in range(1, n_in):
            mix = mix + _dot(o_refs[i][...], w_refs[i][...])
        x = h_ref[...] + mix
        xn_sc[...] = _rms(x, g_ref[...]).astype(BF16)
        out_ref[...] = x

    u = _dot(xn_sc[...], wu_ref[...])
    a = jnp.square(jnp.maximum(u, 0.0)).astype(BF16)
    out_ref[...] += _dot(a, wd_ref[...])

    if final_norm:
        @pl.when(f == pl.num_programs(1) - 1)
        def _():
            out_ref[...] = _rms(out_ref[...], fg_ref[...])


def _layer_tail(h2, outs, ws, g, w_up, w_down, final_g=None, tm=1024, tf=1024):
    N, D = h2.shape
    F = w_up.shape[1]
    tm = min(tm, N)
    n_in = len(outs)
    in_specs = [pl.BlockSpec((tm, D), lambda i, f: (i, 0))]
    in_specs += [pl.BlockSpec((tm, o.shape[1]), lambda i, f: (i, 0)) for o in outs]
    in_specs += [pl.BlockSpec(w.shape, lambda i, f: (0, 0)) for w in ws]
    in_specs += [
        pl.BlockSpec((1, D), lambda i, f: (0, 0)),
        pl.BlockSpec((D, tf), lambda i, f: (0, f)),
        pl.BlockSpec((tf, D), lambda i, f: (f, 0)),
    ]
    args = [h2, *outs, *ws, g.reshape(1, D), w_up, w_down]
    if final_g is not None:
        in_specs.append(pl.BlockSpec((1, D), lambda i, f: (0, 0)))
        args.append(final_g.reshape(1, D))
    kern = functools.partial(_layer_tail_kernel, n_in=n_in, final_norm=final_g is not None)
    return pl.pallas_call(
        kern,
        grid=(N // tm, F // tf),
        in_specs=in_specs,
        out_specs=pl.BlockSpec((tm, D), lambda i, f: (i, 0)),
        out_shape=jax.ShapeDtypeStruct((N, D), F32),
        scratch_shapes=[pltpu.VMEM((tm, D), BF16)],
        compiler_params=_params(("parallel", "arbitrary")),
        name="out_proj_mlp",
    )(*args)


def _even_layout():
    offs = {}
    o = 0
    for name, size in (("qa", 512), ("ka", 64), ("va", 64), ("qi", 256), ("ki", 64), ("wi", 4),
                       ("qb", 512), ("kvb", 768), ("gb", 24)):
        offs[name] = o
        o += size
    kvb = lambda which, g: offs["kvb"] + (which * B_KV_GROUPS + g) * HEAD_DIM
    cols, plan, lay = [], [], {}
    n = {k: 0 for k in _KINDS}

    def add(c, roped, kind, opt=None):
        cols.append(c)
        plan.append((roped, kind, n[kind], opt))
        n[kind] += 1
        return n[kind] - 1

    lay["qA"] = n["qT"]
    for p in range(4):
        add(_pair_cols(offs["qa"] + 2 * p * 64, offs["qa"] + (2 * p + 1) * 64), True, "qT", Q_SCALE)
    lay["qB"] = n["qT"]
    for p in range(4):
        add(_pair_cols(offs["qb"] + 2 * p * 64, offs["qb"] + (2 * p + 1) * 64), True, "qT", Q_SCALE)
    lay["qi"] = n["qT"]
    for p in range(2):
        add(_pair_cols(offs["qi"] + 2 * p * 64, offs["qi"] + (2 * p + 1) * 64), True, "qT")
    lay["kA"] = add(_pair_cols(offs["ka"], offs["ka"]), True, "k")
    lay["ki"] = add(_pair_cols(offs["ki"], offs["ki"]), True, "k")
    lay["ks"] = n["k"]
    for g in range(2):
        add(_pair_cols(kvb(2, g), kvb(2, g)), True, "k")
    lay["kw"] = n["k"]
    for g in range(2):
        add(_pair_cols(kvb(4, g), kvb(4, g)), True, "k")
    lay["kc"] = add(_pair_cols(kvb(0, 0), kvb(0, 1)), True, "f")
    lay["vc"] = add(_pair_cols(kvb(1, 0), kvb(1, 1)), False, "f")
    misc = np.full(LANES, -1)
    misc[0:IDX_HEADS] = offs["wi"] + np.arange(IDX_HEADS)
    lay["gate_row0"] = 8
    misc[8:8 + 24] = offs["gb"] + np.arange(24)
    lay["misc"] = add(misc, False, "fT")
    lay["vw"] = n["vTw"]
    for g in range(2):
        add(_head_cols(kvb(5, g)), False, "vTw")
    lay["vA"] = add(_head_cols(offs["va"]), False, "vT", "ones_row_64")
    lay["vs"] = n["vT"]
    for g in range(2):
        add(_head_cols(kvb(3, g)), False, "vT", "ones_row_64")
    return np.concatenate(cols), plan, lay


def _odd_layout():
    cols, plan = [], []
    for h in range(C_HEADS):
        cols.append(_pair_cols(h * 128, h * 128 + 64))
        plan.append((True, "qT", h, Q_SCALE))
    for h in range(C_HEADS):
        cols.append(_pair_cols(1024 + h * 128, 1024 + h * 128 + 64))
        plan.append((True, "k", h, None))
    for h in range(C_HEADS):
        cols.append(2048 + h * 128 + np.arange(LANES))
        plan.append((False, "vT", h, "ones_rows_below"))
    return np.concatenate(cols), plan


def _compress_weights(pe, w1, w2):
    d = _PAIR_D
    which = _PAIR_WHICH
    pe_l = pe[:, :, d]
    pe_l = pe_l.reshape(2, 2, 1, CMP_STRIDE * LANES)
    w1r = w1.reshape(2, CMP_BLOCK, HEAD_DIM, CMP_HIDDEN)[:, :, d, :]
    per_g = []
    for g in range(B_KV_GROUPS):
        keep = jnp.asarray(which == g)[None, None, :, None]
        per_g.append(jnp.where(keep, w1r, 0.0))
    w1g = jnp.stack(per_g, axis=1)
    w1g = w1g.reshape(2, B_KV_GROUPS, 2, CMP_STRIDE * LANES, CMP_HIDDEN).astype(BF16)
    w2k = w2[0][:, d]
    w2v = w2[1][:, np.arange(LANES) % HEAD_DIM]
    w2l = jnp.stack([w2k, w2v], axis=0).astype(BF16)
    return pe_l, w1g, w2l


def _even_mixer(h, norm_g, w_in, cmp_pe, cmp_w1, cmp_w2, w_out, cos_slab, sin_slab):
    B, T, D = h.shape
    cols, plan, lay = _even_layout()
    w = _gather_cols(w_in, cols).astype(BF16)
    sl = _project(h, norm_g, w, cos_slab, sin_slab, plan)

    bias = _dsa_select(sl, lay)
    o_a = _dsa_attention(sl, bias, lay)

    n_ch = T // CMP_STRIDE
    flat = sl["f"][:, lay["kc"]:lay["vc"] + 1].reshape(B, 2, n_ch, CMP_STRIDE * LANES)
    flat = jnp.swapaxes(flat, 0, 1)
    pe_l, w1g, w2l = _compress_weights(cmp_pe, cmp_w1, cmp_w2)
    cmp_k, cmp_vt = _compress(flat, pe_l, w1g, w2l)
    o_b = _nsa_attention(sl, cmp_k, cmp_vt, lay)

    na = A_HEADS * HEAD_DIM
    wo = w_out.astype(BF16)
    return [o_a.reshape(B * T, -1), o_b.reshape(B * T, -1)], [wo[:na], wo[na:]]


def _odd_mixer(h, norm_g, w_in, lam, subln_g, w_out, cos_slab, sin_slab, lambda_init):
    B, T, D = h.shape
    cols, plan = _odd_layout()
    w = _gather_cols(w_in, cols).astype(BF16)
    sl = _project(h, norm_g, w, cos_slab, sin_slab, plan)
    o = _diff_attention(sl, lam, subln_g, lambda_init)
    return [o.reshape(B * T, -1)], [w_out.astype(BF16)]


def kernel(x, mix_norm_g, mlp_norm_g, even_w_in, even_cmp_pe, even_cmp_w1, even_cmp_w2, even_w_out, odd_w_in, odd_lambda, odd_subln_g, odd_w_out, mlp_w_up, mlp_w_down, final_norm_g):
    B, T, D = x.shape
    depth = mix_norm_g.shape[0]
    assert depth >= 1
    cos_slab, sin_slab = _rope_slabs(T)
    h = x
    for layer in range(depth):
        if layer % 2 == 0:
            e = layer // 2
            outs, ws = _even_mixer(h, mix_norm_g[layer], even_w_in[e], even_cmp_pe[e],
                                   even_cmp_w1[e], even_cmp_w2[e], even_w_out[e], cos_slab, sin_slab)
        else:
            o = layer // 2
            lambda_init = 0.8 - 0.6 * math.exp(-0.3 * layer)
            outs, ws = _odd_mixer(h, mix_norm_g[layer], odd_w_in[o], odd_lambda[o], odd_subln_g[o],
                                  odd_w_out[o], cos_slab, sin_slab, lambda_init)
        h2 = _layer_tail(h.reshape(B * T, D), outs, ws, mlp_norm_g[layer],
                         mlp_w_up[layer].astype(BF16), mlp_w_down[layer].astype(BF16),
                         final_g=final_norm_g if layer == depth - 1 else None)
        h = h2.reshape(B, T, D)
    return h
```

```python
import functools
import math

import numpy as np
import jax
import jax.numpy as jnp
from jax import lax
from jax.experimental import pallas as pl
from jax.experimental.pallas import tpu as pltpu

HEAD_DIM = 64
HALF = HEAD_DIM // 2
LANES = 128
ROPE_THETA = 10000.0
NORM_EPS = 1e-6
SCALE = HEAD_DIM ** -0.5

A_HEADS = 8
IDX_HEADS = 4
DSA_TOPK = 256
B_HEADS = 8
B_KV_GROUPS = 2
B_PER_GROUP = B_HEADS // B_KV_GROUPS
CMP_BLOCK = 32
CMP_STRIDE = 16
CMP_HIDDEN = 256
SLC_BLOCK = 64
SLC_TOPN = 16
WINDOW = 512
C_HEADS = 8

KV_TILE = 512
SEL_Q_TILE = 128
DSA_Q_TILE = 256
NSA_Q_TILE = 256
VW_TILE = 128

LOG2E = math.log2(math.e)
Q_SCALE = SCALE * LOG2E
ONES_ROWS = 16
BIG = 2.0 ** 100

NEG_INF = float("-inf")
M_FLOOR = -1e30
INT_MIN = -(2 ** 31)

VMEM_LIMIT = 56 * 1024 * 1024

BF16 = jnp.bfloat16
F32 = jnp.float32


def _dot(a, b):
    return jnp.dot(a, b, preferred_element_type=F32)


def _dot_tn(a, b):
    return lax.dot_general(a, b, (((0,), (0,)), ((), ())), preferred_element_type=F32)


def _params(sem):
    return pltpu.CompilerParams(dimension_semantics=sem, vmem_limit_bytes=VMEM_LIMIT)


def _iota(shape, axis):
    return lax.broadcasted_iota(jnp.int32, shape, axis)


def _pair_cols(base_a, base_b):
    lane = np.arange(LANES)
    half = lane // 64
    which = (lane % 64) // HALF
    i = lane % HALF
    base = np.where(which == 0, base_a, base_b)
    return base + half * HALF + i


def _dup_cols(base):
    return base + np.arange(LANES) % HEAD_DIM


def _head_cols(base):
    lane = np.arange(LANES)
    return np.where(lane < HEAD_DIM, base + lane, -1)


_PAIR_D = _pair_cols(0, 0)
_PAIR_WHICH = (np.arange(LANES) % 64) // HALF


def _gather_cols(w, cols):
    cols = np.asarray(cols)
    safe = np.where(cols >= 0, cols, 0)
    g = jnp.take(w, jnp.asarray(safe, dtype=jnp.int32), axis=1)
    return jnp.where(jnp.asarray(cols >= 0)[None, :], g, 0.0)


def _rope_slabs(T):
    inv = 1.0 / (ROPE_THETA ** (jnp.arange(0, HEAD_DIM, 2, dtype=F32) / HEAD_DIM))
    ang = jnp.arange(T, dtype=F32)[:, None] * inv[None, :]
    cos, sin = jnp.cos(ang), jnp.sin(ang)
    cos_slab = jnp.tile(cos, (1, 4))
    sin_slab = jnp.concatenate([-sin, -sin, sin, sin], axis=1)
    return cos_slab, sin_slab


_KINDS = ("qT", "k", "vT", "vTw", "f", "fT")


def _proj_kernel(x_ref, g_ref, w_ref, cos_ref, sin_ref, *out_refs, plan, kinds, chunk):
    outs = dict(zip(kinds, out_refs))
    x = x_ref[0]
    ms = jnp.mean(x * x, axis=-1, keepdims=True)
    xn = ((x * lax.rsqrt(ms + NORM_EPS)) * g_ref[...]).astype(BF16)
    cos = cos_ref[...]
    sin = sin_ref[...]
    n = len(plan)
    for c0 in range(0, n, chunk):
        c1 = min(c0 + chunk, n)
        r = _dot(xn, w_ref[:, c0 * LANES:c1 * LANES])
        for s in range(c0, c1):
            roped, kind, idx, opt = plan[s]
            y = r[:, (s - c0) * LANES:(s - c0 + 1) * LANES]
            if roped:
                y = y * cos + pltpu.roll(y, 64, 1) * sin
            if kind == "qT":
                if opt is not None:
                    y = y * opt
                outs[kind][0, idx] = y.T.astype(BF16)
            elif kind == "k":
                outs[kind][0, idx] = y.astype(BF16)
            elif kind == "vT":
                if opt == "ones_row_64":
                    y = jnp.where(_iota(y.shape, 1) == HEAD_DIM, 1.0, y)
                    outs[kind][0, idx, 0] = y.T.astype(BF16)
                else:
                    outs[kind][0, idx, 0, 0:LANES, :] = y.T.astype(BF16)
                    outs[kind][0, idx, 0, LANES:, :] = jnp.ones((ONES_ROWS, y.shape[0]), BF16)
            elif kind == "vTw":
                y_t = jnp.where(_iota(y.shape, 1) == HEAD_DIM, 1.0, y).T.astype(BF16)
                for sub in range(y.shape[0] // VW_TILE):
                    outs[kind][0, idx, sub] = y_t[:, sub * VW_TILE:(sub + 1) * VW_TILE]
            elif kind == "f":
                outs[kind][0, idx] = y
            else:
                outs[kind][0, idx] = y.T


def _project(x, g, w, cos_slab, sin_slab, plan, chunk=4):
    B, T, D = x.shape
    tm = KV_TILE
    assert T % tm == 0
    count = {k: sum(1 for p in plan if p[1] == k) for k in _KINDS}
    kinds = tuple(k for k in _KINDS if count[k])
    out_specs, out_shape = [], []
    for k in kinds:
        n = count[k]
        if k in ("qT", "fT"):
            out_specs.append(pl.BlockSpec((1, n, LANES, tm), lambda b, i: (b, 0, 0, i)))
            out_shape.append(jax.ShapeDtypeStruct((B, n, LANES, T), BF16 if k == "qT" else F32))
        elif k in ("k", "f"):
            out_specs.append(pl.BlockSpec((1, n, tm, LANES), lambda b, i: (b, 0, i, 0)))
            out_shape.append(jax.ShapeDtypeStruct((B, n, T, LANES), BF16 if k == "k" else F32))
        elif k == "vTw":
            sub = tm // VW_TILE
            out_specs.append(pl.BlockSpec((1, n, sub, LANES, VW_TILE), lambda b, i: (b, 0, i, 0, 0)))
            out_shape.append(jax.ShapeDtypeStruct((B, n, T // VW_TILE, LANES, VW_TILE), BF16))
        else:
            wide = any(p[1] == "vT" and p[3] != "ones_row_64" for p in plan)
            rows = LANES + ONES_ROWS if wide else LANES
            out_specs.append(pl.BlockSpec((1, n, 1, rows, tm), lambda b, i: (b, 0, i, 0, 0)))
            out_shape.append(jax.ShapeDtypeStruct((B, n, T // tm, rows, tm), BF16))
    kern = functools.partial(_proj_kernel, plan=tuple(plan), kinds=kinds, chunk=chunk)
    outs = pl.pallas_call(
        kern,
        grid=(B, T // tm),
        in_specs=[
            pl.BlockSpec((1, tm, D), lambda b, i: (b, i, 0)),
            pl.BlockSpec((1, D), lambda b, i: (0, 0)),
            pl.BlockSpec((D, len(plan) * LANES), lambda b, i: (0, 0)),
            pl.BlockSpec((tm, LANES), lambda b, i: (i, 0)),
            pl.BlockSpec((tm, LANES), lambda b, i: (i, 0)),
        ],
        out_specs=out_specs,
        out_shape=out_shape,
        compiler_params=_params(("parallel", "parallel")),
        name="norm_proj_rope",
    )(x, g.reshape(1, D), w, cos_slab, sin_slab)
    return dict(zip(kinds, outs))


def _head_rows(slab_t, which):
    row = _iota(slab_t.shape, 0)
    keep = ((row % 64) // HALF) == which
    return jnp.where(keep, slab_t, jnp.zeros_like(slab_t))


def _tree(op, xs):
    while len(xs) > 1:
        xs = [op(xs[i], xs[i + 1]) if i + 1 < len(xs) else xs[i] for i in range(0, len(xs), 2)]
    return xs[0]


def _fold_rows_max(x, ways=4):
    rows, n = x.shape
    per = rows // ways
    parts = [jnp.max(x[i * per:(i + 1) * per].reshape(per // 8, 8, n), axis=0) for i in range(ways)]
    return _tree(jnp.maximum, parts)


ROW_BLOCK = 64


def _online_update(s_sc, p_sc, v_aug, m_ref, acc_ref):
    tk, n = s_sc.shape
    m_prev = m_ref[...]
    m_tile = jnp.max(_fold_rows_max(s_sc[...]), axis=0, keepdims=True)
    m_new = jnp.maximum(m_prev, m_tile)
    alpha = jnp.exp2(m_prev - m_new)
    for r in range(tk // ROW_BLOCK):
        rows = slice(r * ROW_BLOCK, (r + 1) * ROW_BLOCK)
        p_sc[rows, :] = jnp.exp2(s_sc[rows, :] - m_new).astype(BF16)
    acc_ref[...] = alpha * acc_ref[...] + _dot(v_aug, p_sc[...])
    m_ref[...] = m_new


def _flash_pipeline(n, put_scores, update, buf_a, buf_b, put_last=None):
    if put_last is None:
        put_scores(buf_a, 0)
    else:
        @pl.when(n > 0)
        def _():
            put_scores(buf_a, 0)

    def pair(p, carry):
        j = 2 * p
        put_scores(buf_b, j + 1)
        update(buf_a, j)

        @pl.when(j + 2 < n)
        def _():
            put_scores(buf_a, j + 2)
            update(buf_b, j + 1)

        return carry

    lax.fori_loop(0, n // 2, pair, 0)
    in_b = jnp.logical_and(n > 0, n % 2 == 0)
    in_a = n % 2 == 1

    def finish(cur, other):
        if put_last is not None:
            put_last(other, n)
        update(cur, n - 1)
        if put_last is not None:
            update(other, n)

    pl.when(in_b)(lambda: finish(buf_b, buf_a))
    pl.when(in_a)(lambda: finish(buf_a, buf_b))
    if put_last is not None:
        @pl.when(n == 0)
        def _():
            put_last(buf_a, 0)
            update(buf_a, 0)


def _softmax_keys(s):
    m = jnp.max(s, axis=0, keepdims=True)
    m = jnp.where(m > NEG_INF, m, 0.0)
    e = jnp.exp2(s - m)
    return e / jnp.maximum(jnp.sum(e, axis=0, keepdims=True), 1e-30)


def _init_flash(m_ref, acc_ref):
    m_ref[...] = jnp.full(m_ref.shape, M_FLOOR, F32)
    acc_ref[...] = jnp.zeros(acc_ref.shape, F32)


def _diff_attn_kernel(lam_ref, q_ref, k_ref, v_ref, g_ref, o_ref, m_sc, acc_sc, sa_sc, sb_sc,
                      p_sc, *, tq, lambda_init):
    qi = pl.program_id(2)
    q_t = q_ref[0, 0]
    qs = [_head_rows(q_t, c) for c in range(2)]
    for c in range(2):
        _init_flash(m_sc.at[c], acc_sc.at[c])

    def put_scores(buf, j, masked=False):
        k = k_ref[0, 0, pl.ds(pl.multiple_of(j * tq, tq), tq), :]
        for c in range(2):
            s = _dot(k, qs[c])
            if masked:
                s = jnp.where(_iota(s.shape, 0) <= _iota(s.shape, 1), s, NEG_INF)
            buf[c] = s

    def put_diagonal(buf, j):
        put_scores(buf, j, masked=True)

    def update(buf, j):
        v_aug = v_ref[0, 0, j]
        for c in range(2):
            _online_update(buf.at[c], p_sc.at[c], v_aug, m_sc.at[c], acc_sc.at[c])

    _flash_pipeline(qi, put_scores, update, sa_sc, sb_sc, put_last=put_diagonal)

    lam = lam_ref[...]
    s01 = jnp.sum(lam[0:1] * lam[1:2], axis=-1, keepdims=True)
    s23 = jnp.sum(lam[2:3] * lam[3:4], axis=-1, keepdims=True)
    lam_val = jnp.exp(s01) - jnp.exp(s23) + lambda_init
    o0 = acc_sc[0, 0:LANES] / jnp.maximum(acc_sc[0, LANES:LANES + 1], 1e-30)
    o1 = acc_sc[1, 0:LANES] / jnp.maximum(acc_sc[1, LANES:LANES + 1], 1e-30)
    o = o0 - lam_val * o1
    y = o * lax.rsqrt(jnp.mean(o * o, axis=0, keepdims=True) + NORM_EPS)
    y = (y * g_ref[...]) * (1.0 - lambda_init)
    o_ref[0] = y.T.astype(o_ref.dtype)


def _diff_attention(sl, lam, subln_g, lambda_init):
    q_t, k, v_t = sl["qT"], sl["k"], sl["vT"]
    B, H, T, _ = k.shape
    tq = KV_TILE
    nk = T // tq
    kern = functools.partial(_diff_attn_kernel, tq=tq, lambda_init=lambda_init)
    return pl.pallas_call(
        kern,
        grid=(B, H, T // tq),
        in_specs=[
            pl.BlockSpec((4, HEAD_DIM), lambda b, h, i: (0, 0)),
            pl.BlockSpec((1, 1, LANES, tq), lambda b, h, i: (b, h, 0, i)),
            pl.BlockSpec((1, 1, T, LANES), lambda b, h, i: (b, h, 0, 0)),
            pl.BlockSpec((1, 1, nk, LANES + ONES_ROWS, tq), lambda b, h, i: (b, h, 0, 0, 0)),
            pl.BlockSpec((LANES, 1), lambda b, h, i: (0, 0)),
        ],
        out_specs=pl.BlockSpec((1, tq, LANES), lambda b, h, i: (b, i, h)),
        out_shape=jax.ShapeDtypeStruct((B, T, H * LANES), BF16),
        scratch_shapes=[
            pltpu.VMEM((2, 1, tq), F32),
            pltpu.VMEM((2, LANES + ONES_ROWS, tq), F32),
            pltpu.VMEM((2, tq, tq), F32),
            pltpu.VMEM((2, tq, tq), F32),
            pltpu.VMEM((2, tq, tq), BF16),
        ],
        compiler_params=_params(("parallel", "parallel", "arbitrary")),
        name="diff_attention",
    )(lam, q_t, k, v_t, subln_g.reshape(LANES, 1))


def _dsa_select_kernel(qi_ref, ki_ref, misc_ref, tril_ref, bias_ref, key_sc, *, tq, ck, nk, topk):
    t0 = pl.program_id(1) * tq
    nvalid = (t0 + tq - 1) // ck + 1
    w = misc_ref[0, 0]
    qh = jnp.concatenate([_head_rows(qi_ref[0, h // 2], h % 2) for h in range(IDX_HEADS)], axis=1)
    t_q = t0 + _iota((ck, tq), 1)

    def causal(c):
        return (c * ck + _iota((ck, tq), 0)) <= t_q

    def fill(c, carry):
        kk = ki_ref[0, 0, pl.ds(pl.multiple_of(c * ck, ck), ck), :]
        r = _dot(kk, qh)
        score = jnp.zeros((ck, tq), F32)
        for h in range(IDX_HEADS):
            score = score + w[h:h + 1, :] * jnp.maximum(r[:, h * tq:(h + 1) * tq], 0.0)
        bits = pltpu.bitcast(score, jnp.int32)
        key = bits ^ ((bits >> 31) & jnp.int32(0x7FFFFFFF))
        key = jnp.where(score == 0.0, 0, key)
        key_sc[c] = jnp.where(causal(c), key, INT_MIN)
        return carry

    lax.fori_loop(0, nvalid, fill, 0)

    def count(pred):
        def chunk(c, acc):
            ind = pred(key_sc[c], c)
            return acc + jnp.sum(ind.reshape(8, ck // 8, tq), axis=0)
        acc = lax.fori_loop(0, nvalid, chunk, jnp.zeros((ck // 8, tq), F32))
        return jnp.sum(acc, axis=0, keepdims=True)

    kf = float(topk)
    zero = jnp.zeros((1, tq), jnp.int32)
    ans = jnp.where(count(lambda kc, c: jnp.where(kc >= zero, 1.0, 0.0)) >= kf, 0, INT_MIN)

    def bit_step(i, ans):
        cand = ans | (jnp.int32(1) << (30 - i))
        cnt = count(lambda kc, c: jnp.where(kc >= cand, 1.0, 0.0))
        return jnp.where(cnt >= kf, cand, ans)

    ans = lax.fori_loop(0, 31, bit_step, ans)

    cnt_gt = count(lambda kc, c: jnp.where(kc > ans, 1.0, 0.0))
    cnt_eq = count(lambda kc, c: jnp.where(kc == ans, jnp.where(causal(c), 1.0, 0.0), 0.0))
    need = kf - cnt_gt
    has_tie = jnp.max(cnt_eq - need) > 0.0

    @pl.when(jnp.logical_not(has_tie))
    def _():
        def emit(c, carry):
            sel = jnp.where(key_sc[c] >= ans, jnp.where(causal(c), 0.0, NEG_INF), NEG_INF)
            bias_ref[0, 0, c] = sel.astype(BF16)
            return carry
        lax.fori_loop(0, nvalid, emit, 0)

    @pl.when(has_tie)
    def _():
        def emit(c, carry):
            kc = key_sc[c]
            eq = jnp.where(kc == ans, jnp.where(causal(c), 1.0, 0.0), 0.0)
            pre = _dot(tril_ref[...], eq.astype(BF16)) + carry
            take = jnp.where(kc > ans, 1.0, jnp.where(pre < need, eq, 0.0))
            sel = jnp.where(take > 0.5, jnp.where(causal(c), 0.0, NEG_INF), NEG_INF)
            bias_ref[0, 0, c] = sel.astype(BF16)
            return carry + jnp.sum(eq, axis=0, keepdims=True)
        lax.fori_loop(0, nvalid, emit, jnp.zeros((1, tq), F32))

    def blank(c, carry):
        bias_ref[0, 0, c] = jnp.full((ck, tq), NEG_INF, BF16)
        return carry

    lax.fori_loop(nvalid, nk, blank, 0)


def _dsa_select(sl, lay):
    q_t, k, misc_t = sl["qT"], sl["k"], sl["fT"]
    B, _, T, _ = k.shape
    tq, ck = SEL_Q_TILE, KV_TILE
    nk = T // ck
    topk = min(DSA_TOPK, T // 4)
    tril = jnp.asarray(np.tril(np.ones((ck, ck), np.float32), -1), BF16)
    kern = functools.partial(_dsa_select_kernel, tq=tq, ck=ck, nk=nk, topk=topk)
    return pl.pallas_call(
        kern,
        grid=(B, T // tq),
        in_specs=[
            pl.BlockSpec((1, 2, LANES, tq), lambda b, i: (b, lay["qi"] // 2, 0, i)),
            pl.BlockSpec((1, 1, T, LANES), lambda b, i: (b, lay["ki"], 0, 0)),
            pl.BlockSpec((1, 1, LANES, tq), lambda b, i: (b, lay["misc"], 0, i)),
            pl.BlockSpec((ck, ck), lambda b, i: (0, 0)),
        ],
        out_specs=pl.BlockSpec((1, 1, nk, ck, tq), lambda b, i: (b, i, 0, 0, 0)),
        out_shape=jax.ShapeDtypeStruct((B, T // tq, nk, ck, tq), BF16),
        scratch_shapes=[pltpu.VMEM((nk, ck, tq), jnp.int32)],
        compiler_params=_params(("parallel", "parallel")),
        name="dsa_select",
    )(q_t, k, misc_t, tril)


def _store_head_pairs(o_ref, o, n_heads, tq):
    for p in range(n_heads // 2):
        even = o[0:HEAD_DIM, (2 * p) * tq:(2 * p + 1) * tq]
        odd = o[0:HEAD_DIM, (2 * p + 1) * tq:(2 * p + 2) * tq]
        pair = jnp.concatenate([even, odd], axis=0)
        o_ref[0, :, p * LANES:(p + 1) * LANES] = pair.T.astype(o_ref.dtype)


def _normalized(acc):
    return acc[0:HEAD_DIM] / jnp.maximum(acc[HEAD_DIM:HEAD_DIM + 1], 1e-30)


def _dsa_attn_kernel(q_ref, k_ref, v_ref, bias_ref, o_ref, qst, m_sc, acc_sc, sa_sc, sb_sc,
                     p_sc, *, tq, tk):
    qi = pl.program_id(1)
    H = A_HEADS
    for h in range(H):
        qst[:, h * tq:(h + 1) * tq] = _head_rows(q_ref[0, h // 2], h % 2)
    _init_flash(m_sc, acc_sc)
    nkv = (qi * tq) // tk + 1

    def put_scores(buf, j):
        k = k_ref[0, 0, pl.ds(pl.multiple_of(j * tk, tk), tk), :]
        parts = [bias_ref[0, s, j].astype(F32) for s in range(bias_ref.shape[1])]
        b = parts[0] if len(parts) == 1 else jnp.concatenate(parts, axis=1)
        buf[...] = _dot(k, qst[...]) + jnp.concatenate([b] * H, axis=1)

    def update(buf, j):
        _online_update(buf, p_sc, v_ref[0, 0, j], m_sc, acc_sc)

    _flash_pipeline(nkv, put_scores, update, sa_sc, sb_sc)
    _store_head_pairs(o_ref, _normalized(acc_sc[...]), H, tq)


def _dsa_attention(sl, bias, lay):
    q_t, k, v_t = sl["qT"], sl["k"], sl["vT"]
    B, _, T, _ = k.shape
    tq, tk = DSA_Q_TILE, KV_TILE
    nk = T // tk
    H = A_HEADS
    N = H * tq
    sel_tq = bias.shape[4]
    nsub = tq // sel_tq
    kern = functools.partial(_dsa_attn_kernel, tq=tq, tk=tk)
    return pl.pallas_call(
        kern,
        grid=(B, T // tq),
        in_specs=[
            pl.BlockSpec((1, H // 2, LANES, tq), lambda b, i: (b, lay["qA"] // (H // 2), 0, i)),
            pl.BlockSpec((1, 1, T, LANES), lambda b, i: (b, lay["kA"], 0, 0)),
            pl.BlockSpec((1, 1, nk, LANES, tk), lambda b, i: (b, lay["vA"], 0, 0, 0)),
            pl.BlockSpec((1, nsub, nk, tk, sel_tq), lambda b, i: (b, i, 0, 0, 0)),
        ],
        out_specs=pl.BlockSpec((1, tq, H * HEAD_DIM), lambda b, i: (b, i, 0)),
        out_shape=jax.ShapeDtypeStruct((B, T, H * HEAD_DIM), BF16),
        scratch_shapes=[
            pltpu.VMEM((LANES, N), BF16),
            pltpu.VMEM((1, N), F32),
            pltpu.VMEM((LANES, N), F32),
            pltpu.VMEM((tk, N), F32),
            pltpu.VMEM((tk, N), F32),
            pltpu.VMEM((tk, N), BF16),
        ],
        compiler_params=_params(("parallel", "arbitrary")),
        name="dsa_attention",
    )(q_t, k, v_t, bias)


def _compress_kernel(x_ref, pe_ref, w1_ref, w2_ref, o_ref, ot_ref, *, n_ch):
    x = x_ref[0, 0]
    xt = (x + pe_ref[0, 0]).astype(BF16)
    xb = (x + pe_ref[0, 1]).astype(BF16)
    for g in range(B_KV_GROUPS):
        a = _dot(xt, w1_ref[0, g, 0])
        b = _dot(xb, w1_ref[0, g, 1])
        h = jax.nn.gelu(a + pltpu.roll(b, n_ch - 1, 0))
        r = _dot(h.astype(BF16), w2_ref[0])
        o_ref[0, 0, g] = r.astype(BF16)
        ot_ref[0, 0, g] = r.T.astype(BF16)


def _compress(flat, pe, w1, w2):
    _, B, n_ch, W = flat.shape
    G = B_KV_GROUPS
    kern = functools.partial(_compress_kernel, n_ch=n_ch)
    return pl.pallas_call(
        kern,
        grid=(2, B),
        in_specs=[
            pl.BlockSpec((1, 1, n_ch, W), lambda s, b: (s, b, 0, 0)),
            pl.BlockSpec((1, 2, 1, W), lambda s, b: (s, 0, 0, 0)),
            pl.BlockSpec((1, G, 2, W, CMP_HIDDEN), lambda s, b: (s, 0, 0, 0, 0)),
            pl.BlockSpec((1, CMP_HIDDEN, LANES), lambda s, b: (s, 0, 0)),
        ],
        out_specs=[
            pl.BlockSpec((1, 1, G, n_ch, LANES), lambda s, b: (s, b, 0, 0, 0)),
            pl.BlockSpec((1, 1, G, LANES, n_ch), lambda s, b: (s, b, 0, 0, 0)),
        ],
        out_shape=[
            jax.ShapeDtypeStruct((2, B, G, n_ch, LANES), BF16),
            jax.ShapeDtypeStruct((2, B, G, LANES, n_ch), BF16),
        ],
        compiler_params=_params(("parallel", "parallel")),
        name="nsa_compress",
    )(flat, pe, w1, w2)


def _nsa_kernel(q_ref, kc_ref, vc_ref, ks_ref, vs_ref, kw_ref, vw_ref, misc_ref, ov_ref, ex_ref,
                cz_ref, wz_ref, o_ref, qst, m_sc, acc_sc, sa_sc, sb_sc, p_sc, mw_sc, accw_sc,
                sw_sc, pw_sc, *, tq, tk, T, n_s, n_sel, gate_row0):
    g = pl.program_id(1)
    qi = pl.program_id(2)
    t0 = qi * tq
    J = B_PER_GROUP
    N = J * tq
    for j in range(J):
        qst[0:LANES, j * tq:(j + 1) * tq] = _head_rows(q_ref[0, j // 2], j % 2)
    q = qst[0:LANES, :]

    def q_time(shape):
        return t0 + (_iota(shape, 1) % tq)

    wlen = min(WINDOW + tq, T)
    wstart = pl.multiple_of(jnp.maximum(t0 - WINDOW, 0), tq)
    kw = kw_ref[0, 0, pl.ds(wstart, wlen), :]
    n_wt = wlen // tq
    wt = wstart // VW_TILE
    vw = jnp.concatenate([vw_ref[0, 0, wt + i] for i in range(wlen // VW_TILE)], axis=1)
    wz = wz_ref[jnp.minimum(qi, n_wt - 1)].astype(F32)
    sw_sc[...] = _dot(kw, q) + jnp.concatenate([wz] * J, axis=1)
    _init_flash(mw_sc, accw_sc)
    _online_update(sw_sc, pw_sc, vw, mw_sc, accw_sc)

    kc = kc_ref[0, 0, 0]
    n_ch = kc.shape[0]
    s_c = _dot(kc, q)
    cmp_end = _iota((n_ch, N), 0) * CMP_STRIDE + (CMP_BLOCK - 1)
    s_c = jnp.where(cmp_end <= q_time((n_ch, N)), s_c, NEG_INF)
    p_c = _softmax_keys(s_c)
    o_c = _dot(vc_ref[0, 0, 0], p_c.astype(BF16))

    psum = p_c[:, 0:tq]
    for j in range(1, J):
        psum = psum + p_c[:, j * tq:(j + 1) * tq]
    p_hi = psum.astype(BF16)
    p_lo = (psum - p_hi.astype(F32)).astype(BF16)
    imp = _dot(ov_ref[...], p_hi) + _dot(ov_ref[...], p_lo)
    rows = -(-n_s // 8) * 8
    imp = imp[0:rows]
    blk = _iota((rows, tq), 0)
    t_q = t0 + _iota((rows, tq), 1)
    cur = t_q // SLC_BLOCK
    forced = (blk == 0) | (blk == cur) | (blk == cur - 1)
    sc = jnp.where(forced, jnp.inf, imp)
    sc = jnp.where(blk * SLC_BLOCK <= t_q, sc, NEG_INF)
    groups = [sc[8 * r:8 * r + 8] for r in range(rows // 8)]
    ranks = [jnp.zeros((8, tq), F32) for _ in groups]
    blk8 = _iota((8, tq), 0)
    for m in range(n_s):
        cm = sc[m:m + 1, :]
        for r, grp in enumerate(groups):
            gt = jnp.where(cm > grp, 1.0, 0.0)
            ge = jnp.where(cm >= grp, 1.0, 0.0)
            if 8 * r + 7 <= m:
                first = gt
            elif 8 * r > m:
                first = ge
            else:
                first = jnp.where(blk8 + 8 * r > m, ge, gt)
            ranks[r] = ranks[r] + first
    rank = jnp.concatenate(ranks, axis=0)
    drop = jnp.where(rank < float(n_sel), 0.0, 1.0)
    if rows < LANES:
        drop = jnp.concatenate([drop, jnp.ones((LANES - rows, tq), F32)], axis=0)
    drop = drop.astype(BF16)
    qst[LANES:2 * LANES, :] = jnp.concatenate([drop] * J, axis=1)

    _init_flash(m_sc, acc_sc)
    jd = t0 // tk
    off = (t0 - jd * tk) // tq
    n_off = tk // tq

    def put_scores(buf, j):
        k = ks_ref[0, 0, pl.ds(pl.multiple_of(j * tk, tk), tk), :]
        lhs = jnp.concatenate([k, ex_ref[j]], axis=1)
        cz = cz_ref[jnp.where(j == jd, off, n_off)].astype(F32)
        buf[...] = _dot(lhs, qst[...]) + jnp.concatenate([cz] * J, axis=1)

    def update(buf, j):
        _online_update(buf, p_sc, vs_ref[0, 0, j], m_sc, acc_sc)

    _flash_pipeline(jd + 1, put_scores, update, sa_sc, sb_sc)
    o_s = _normalized(acc_sc[...])
    o_w = _normalized(accw_sc[...])

    gates = jax.nn.sigmoid(misc_ref[0, 0])

    def gate_row(c):
        parts = []
        for j in range(J):
            r0 = gate_row0 + j * 3 + c
            r1 = gate_row0 + (J + j) * 3 + c
            parts.append(jnp.where(g == 0, gates[r0:r0 + 1, :], gates[r1:r1 + 1, :]))
        return jnp.concatenate(parts, axis=1)

    o = gate_row(0) * o_c[0:HEAD_DIM] + (gate_row(1) * o_s + gate_row(2) * o_w)
    _store_head_pairs(o_ref, o, J, tq)


def _nsa_attention(sl, cmp_k, cmp_vt, lay):
    q_t, k, v_t, misc_t = sl["qT"], sl["k"], sl["vT"], sl["fT"]
    B, _, T, _ = k.shape
    G, J = B_KV_GROUPS, B_PER_GROUP
    tq, tk = NSA_Q_TILE, KV_TILE
    nk = T // tk
    n_ch = T // CMP_STRIDE
    n_c = n_ch - CMP_BLOCK // CMP_STRIDE + 1
    n_s = T // SLC_BLOCK
    n_sel = min(SLC_TOPN, n_s)
    assert n_s <= LANES
    c0 = np.arange(n_ch) * CMP_STRIDE
    s0 = np.arange(LANES) * SLC_BLOCK
    ov = ((c0[None, :] < s0[:, None] + SLC_BLOCK) & (c0[None, :] + CMP_BLOCK > s0[:, None]))
    ov = ov & (np.arange(n_ch)[None, :] < n_c) & (np.arange(LANES)[:, None] < n_s)
    ov = jnp.asarray(ov.astype(np.float32), BF16)
    pos = np.arange(T).reshape(nk, tk, 1)
    ex = (pos // SLC_BLOCK == np.arange(LANES).reshape(1, 1, LANES))
    ex = jnp.asarray(ex.astype(np.float32) * -BIG, BF16)
    n_off = tk // tq
    kp = np.arange(tk).reshape(1, tk, 1)
    tl = np.arange(tq).reshape(1, 1, tq) + np.arange(n_off + 1).reshape(n_off + 1, 1, 1) * tq
    cz = np.where((kp <= tl) | (np.arange(n_off + 1).reshape(-1, 1, 1) == n_off), 0.0, -BIG)
    cz = jnp.asarray(cz.astype(np.float32), BF16)
    wlen = min(WINDOW + tq, T)
    n_wt = wlen // tq
    assert wlen % tq == 0 and T >= wlen
    kp = np.arange(wlen).reshape(1, wlen, 1)
    tl = np.arange(tq).reshape(1, 1, tq)
    early = kp <= tl + np.arange(n_wt).reshape(n_wt, 1, 1) * tq
    late = (kp > tl) & (kp <= tl + WINDOW)
    band = np.where(np.arange(n_wt).reshape(n_wt, 1, 1) == n_wt - 1, late, early)
    wz = jnp.asarray(np.where(band, 0.0, -BIG).astype(np.float32), BF16)
    kern = functools.partial(_nsa_kernel, tq=tq, tk=tk, T=T, n_s=n_s, n_sel=n_sel,
                             gate_row0=lay["gate_row0"])
    N = J * tq
    kslab = lambda off: pl.BlockSpec((1, 1, T, LANES), lambda b, g, i: (b, off + g, 0, 0))
    return pl.pallas_call(
        kern,
        grid=(B, G, T // tq),
        in_specs=[
            pl.BlockSpec((1, 2, LANES, tq), lambda b, g, i: (b, lay["qB"] // 2 + g, 0, i)),
            pl.BlockSpec((1, 1, 1, n_ch, LANES), lambda b, g, i: (0, b, g, 0, 0)),
            pl.BlockSpec((1, 1, 1, LANES, n_ch), lambda b, g, i: (1, b, g, 0, 0)),
            kslab(lay["ks"]),
            pl.BlockSpec((1, 1, nk, LANES, tk), lambda b, g, i: (b, lay["vs"] + g, 0, 0, 0)),
            kslab(lay["kw"]),
            pl.BlockSpec((1, 1, T // VW_TILE, LANES, VW_TILE),
                         lambda b, g, i: (b, lay["vw"] + g, 0, 0, 0)),
            pl.BlockSpec((1, 1, LANES, tq), lambda b, g, i: (b, lay["misc"], 0, i)),
            pl.BlockSpec((LANES, n_ch), lambda b, g, i: (0, 0)),
            pl.BlockSpec((nk, tk, LANES), lambda b, g, i: (0, 0, 0)),
            pl.BlockSpec((n_off + 1, tk, tq), lambda b, g, i: (0, 0, 0)),
            pl.BlockSpec((n_wt, wlen, tq), lambda b, g, i: (0, 0, 0)),
        ],
        out_specs=pl.BlockSpec((1, tq, J * HEAD_DIM), lambda b, g, i: (b, i, g)),
        out_shape=jax.ShapeDtypeStruct((B, T, B_HEADS * HEAD_DIM), BF16),
        scratch_shapes=[
            pltpu.VMEM((2 * LANES, N), BF16),
            pltpu.VMEM((1, N), F32),
            pltpu.VMEM((LANES, N), F32),
            pltpu.VMEM((tk, N), F32),
            pltpu.VMEM((tk, N), F32),
            pltpu.VMEM((tk, N), BF16),
            pltpu.VMEM((1, N), F32),
            pltpu.VMEM((LANES, N), F32),
            pltpu.VMEM((wlen, N), F32),
            pltpu.VMEM((wlen, N), BF16),
        ],
        compiler_params=_params(("parallel", "parallel", "arbitrary")),
        name="nsa_attention",
    )(q_t, cmp_k, cmp_vt, k, v_t, k, sl["vTw"], misc_t, ov, ex, cz, wz)


def _rms(x, g):
    ms = jnp.mean(x * x, axis=-1, keepdims=True)
    return (x * lax.rsqrt(ms + NORM_EPS)) * g


def _layer_tail_kernel(*refs, n_in, final_norm):
    h_ref = refs[0]
    o_refs = refs[1:1 + n_in]
    w_refs = refs[1 + n_in:1 + 2 * n_in]
    g_ref, wu_ref, wd_ref = refs[1 + 2 * n_in:4 + 2 * n_in]
    rest = refs[4 + 2 * n_in:]
    if final_norm:
        fg_ref, out_ref, xn_sc = rest
    else:
        out_ref, xn_sc = rest
    f = pl.program_id(1)

    @pl.when(f == 0)
    def _():
        mix = _dot(o_refs[0][...], w_refs[0][...])
        for i in range(1, n_in):
            mix = mix + _dot(o_refs[i][...], w_refs[i][...])
        x = h_ref[...] + mix
        xn_sc[...] = _rms(x, g_ref[...]).astype(BF16)
        out_ref[...] = x

    u = _dot(xn_sc[...], wu_ref[...])
    a = jnp.square(jnp.maximum(u, 0.0)).astype(BF16)
    out_ref[...] += _dot(a, wd_ref[...])

    if final_norm:
        @pl.when(f == pl.num_programs(1) - 1)
        def _():
            out_ref[...] = _rms(out_ref[...], fg_ref[...])


def _layer_tail(h2, outs, ws, g, w_up, w_down, final_g=None, tm=1024, tf=1024):
    N, D = h2.shape
    F = w_up.shape[1]
    tm = min(tm, N)
    n_in = len(outs)
    in_specs = [pl.BlockSpec((tm, D), lambda i, f: (i, 0))]
    in_specs += [pl.BlockSpec((tm, o.shape[1]), lambda i, f: (i, 0)) for o in outs]
    in_specs += [pl.BlockSpec(w.shape, lambda i, f: (0, 0)) for w in ws]
    in_specs += [
        pl.BlockSpec((1, D), lambda i, f: (0, 0)),
        pl.BlockSpec((D, tf), lambda i, f: (0, f)),
        pl.BlockSpec((tf, D), lambda i, f: (f, 0)),
    ]
    args = [h2, *outs, *ws, g.reshape(1, D), w_up, w_down]
    if final_g is not None:
        in_specs.append(pl.BlockSpec((1, D), lambda i, f: (0, 0)))
        args.append(final_g.reshape(1, D))
    kern = functools.partial(_layer_tail_kernel, n_in=n_in, final_norm=final_g is not None)
    return pl.pallas_call(
        kern,
        grid=(N // tm, F // tf),
        in_specs=in_specs,
        out_specs=pl.BlockSpec((tm, D), lambda i, f: (i, 0)),
        out_shape=jax.ShapeDtypeStruct((N, D), F32),
        scratch_shapes=[pltpu.VMEM((tm, D), BF16)],
        compiler_params=_params(("parallel", "arbitrary")),
        name="out_proj_mlp",
    )(*args)


def _even_layout():
    offs = {}
    o = 0
    for name, size in (("qa", 512), ("ka", 64), ("va", 64), ("qi", 256), ("ki", 64), ("wi", 4),
                       ("qb", 512), ("kvb", 768), ("gb", 24)):
        offs[name] = o
        o += size
    kvb = lambda which, g: offs["kvb"] + (which * B_KV_GROUPS + g) * HEAD_DIM
    cols, plan, lay = [], [], {}
    n = {k: 0 for k in _KINDS}

    def add(c, roped, kind, opt=None):
        cols.append(c)
        plan.append((roped, kind, n[kind], opt))
        n[kind] += 1
        return n[kind] - 1

    lay["qA"] = n["qT"]
    for p in range(4):
        add(_pair_cols(offs["qa"] + 2 * p * 64, offs["qa"] + (2 * p + 1) * 64), True, "qT", Q_SCALE)
    lay["qB"] = n["qT"]
    for p in range(4):
        add(_pair_cols(offs["qb"] + 2 * p * 64, offs["qb"] + (2 * p + 1) * 64), True, "qT", Q_SCALE)
    lay["qi"] = n["qT"]
    for p in range(2):
        add(_pair_cols(offs["qi"] + 2 * p * 64, offs["qi"] + (2 * p + 1) * 64), True, "qT")
    lay["kA"] = add(_pair_cols(offs["ka"], offs["ka"]), True, "k")
    lay["ki"] = add(_pair_cols(offs["ki"], offs["ki"]), True, "k")
    lay["ks"] = n["k"]
    for g in range(2):
        add(_pair_cols(kvb(2, g), kvb(2, g)), True, "k")
    lay["kw"] = n["k"]
    for g in range(2):
        add(_pair_cols(kvb(4, g), kvb(4, g)), True, "k")
    lay["kc"] = add(_pair_cols(kvb(0, 0), kvb(0, 1)), True, "f")
    lay["vc"] = add(_pair_cols(kvb(1, 0), kvb(1, 1)), False, "f")
    misc = np.full(LANES, -1)
    misc[0:IDX_HEADS] = offs["wi"] + np.arange(IDX_HEADS)
    lay["gate_row0"] = 8
    misc[8:8 + 24] = offs["gb"] + np.arange(24)
    lay["misc"] = add(misc, False, "fT")
    lay["vw"] = n["vTw"]
    for g in range(2):
        add(_head_cols(kvb(5, g)), False, "vTw")
    lay["vA"] = add(_head_cols(offs["va"]), False, "vT", "ones_row_64")
    lay["vs"] = n["vT"]
    for g in range(2):
        add(_head_cols(kvb(3, g)), False, "vT", "ones_row_64")
    return np.concatenate(cols), plan, lay


def _odd_layout():
    cols, plan = [], []
    for h in range(C_HEADS):
        cols.append(_pair_cols(h * 128, h * 128 + 64))
        plan.append((True, "qT", h, Q_SCALE))
    for h in range(C_HEADS):
        cols.append(_pair_cols(1024 + h * 128, 1024 + h * 128 + 64))
        plan.append((True, "k", h, None))
    for h in range(C_HEADS):
        cols.append(2048 + h * 128 + np.arange(LANES))
        plan.append((False, "vT", h, "ones_rows_below"))
    return np.concatenate(cols), plan


def _compress_weights(pe, w1, w2):
    d = _PAIR_D
    which = _PAIR_WHICH
    pe_l = pe[:, :, d]
    pe_l = pe_l.reshape(2, 2, 1, CMP_STRIDE * LANES)
    w1r = w1.reshape(2, CMP_BLOCK, HEAD_DIM, CMP_HIDDEN)[:, :, d, :]
    per_g = []
    for g in range(B_KV_GROUPS):
        keep = jnp.asarray(which == g)[None, None, :, None]
        per_g.append(jnp.where(keep, w1r, 0.0))
    w1g = jnp.stack(per_g, axis=1)
    w1g = w1g.reshape(2, B_KV_GROUPS, 2, CMP_STRIDE * LANES, CMP_HIDDEN).astype(BF16)
    w2k = w2[0][:, d]
    w2v = w2[1][:, np.arange(LANES) % HEAD_DIM]
    w2l = jnp.stack([w2k, w2v], axis=0).astype(BF16)
    return pe_l, w1g, w2l


def _even_mixer(h, norm_g, w_in, cmp_pe, cmp_w1, cmp_w2, w_out, cos_slab, sin_slab):
    B, T, D = h.shape
    cols, plan, lay = _even_layout()
    w = _gather_cols(w_in, cols).astype(BF16)
    sl = _project(h, norm_g, w, cos_slab, sin_slab, plan)

    bias = _dsa_select(sl, lay)
    o_a = _dsa_attention(sl, bias, lay)

    n_ch = T // CMP_STRIDE
    flat = sl["f"][:, lay["kc"]:lay["vc"] + 1].reshape(B, 2, n_ch, CMP_STRIDE * LANES)
    flat = jnp.swapaxes(flat, 0, 1)
    pe_l, w1g, w2l = _compress_weights(cmp_pe, cmp_w1, cmp_w2)
    cmp_k, cmp_vt = _compress(flat, pe_l, w1g, w2l)
    o_b = _nsa_attention(sl, cmp_k, cmp_vt, lay)

    na = A_HEADS * HEAD_DIM
    wo = w_out.astype(BF16)
    return [o_a.reshape(B * T, -1), o_b.reshape(B * T, -1)], [wo[:na], wo[na:]]


def _odd_mixer(h, norm_g, w_in, lam, subln_g, w_out, cos_slab, sin_slab, lambda_init):
    B, T, D = h.shape
    cols, plan = _odd_layout()
    w = _gather_cols(w_in, cols).astype(BF16)
    sl = _project(h, norm_g, w, cos_slab, sin_slab, plan)
    o = _diff_attention(sl, lam, subln_g, lambda_init)
    return [o.reshape(B * T, -1)], [w_out.astype(BF16)]


def kernel(x, mix_norm_g, mlp_norm_g, even_w_in, even_cmp_pe, even_cmp_w1, even_cmp_w2, even_w_out, odd_w_in, odd_lambda, odd_subln_g, odd_w_out, mlp_w_up, mlp_w_down, final_norm_g):
    B, T, D = x.shape
    depth = mix_norm_g.shape[0]
    assert depth >= 1
    cos_slab, sin_slab = _rope_slabs(T)
    h = x
    for layer in range(depth):
        if layer % 2 == 0:
            e = layer // 2
            outs, ws = _even_mixer(h, mix_norm_g[layer], even_w_in[e], even_cmp_pe[e],
                                   even_cmp_w1[e], even_cmp_w2[e], even_w_out[e], cos_slab, sin_slab)
        else:
            o = layer // 2
            lambda_init = 0.8 - 0.6 * math.exp(-0.3 * layer)
            outs, ws = _odd_mixer(h, mix_norm_g[layer], odd_w_in[o], odd_lambda[o], odd_subln_g[o],
                                  odd_w_out[o], cos_slab, sin_slab, lambda_init)
        h2 = _layer_tail(h.reshape(B * T, D), outs, ws, mlp_norm_g[layer],
                         mlp_w_up[layer].astype(BF16), mlp_w_down[layer].astype(BF16),
                         final_g=final_norm_g if layer == depth - 1 else None)
        h = h2.reshape(B, T, D)
    return h
```

```python
import functools
import math

import numpy as np
import jax
import jax.numpy as jnp
from jax import lax
from jax.experimental import pallas as pl
from jax.experimental.pallas import tpu as pltpu

HEAD_DIM = 64
HALF = HEAD_DIM // 2
LANES = 128
ROPE_THETA = 10000.0
NORM_EPS = 1e-6
SCALE = HEAD_DIM ** -0.5

A_HEADS = 8
IDX_HEADS = 4
DSA_TOPK = 256
B_HEADS = 8
B_KV_GROUPS = 2
B_PER_GROUP = B_HEADS // B_KV_GROUPS
CMP_BLOCK = 32
CMP_STRIDE = 16
CMP_HIDDEN = 256
SLC_BLOCK = 64
SLC_TOPN = 16
WINDOW = 512
C_HEADS = 8

KV_TILE = 512
SEL_Q_TILE = 128
DSA_Q_TILE = 256
NSA_Q_TILE = 256
VW_TILE = 128

LOG2E = math.log2(math.e)
Q_SCALE = SCALE * LOG2E
ONES_ROWS = 16
BIG = 2.0 ** 100

NEG_INF = float("-inf")
M_FLOOR = -1e30
INT_MIN = -(2 ** 31)

VMEM_LIMIT = 56 * 1024 * 1024

BF16 = jnp.bfloat16
F32 = jnp.float32


def _dot(a, b):
    return jnp.dot(a, b, preferred_element_type=F32)


def _dot_tn(a, b):
    return lax.dot_general(a, b, (((0,), (0,)), ((), ())), preferred_element_type=F32)


def _params(sem):
    return pltpu.CompilerParams(dimension_semantics=sem, vmem_limit_bytes=VMEM_LIMIT)


def _iota(shape, axis):
    return lax.broadcasted_iota(jnp.int32, shape, axis)


def _pair_cols(base_a, base_b):
    lane = np.arange(LANES)
    half = lane // 64
    which = (lane % 64) // HALF
    i = lane % HALF
    base = np.where(which == 0, base_a, base_b)
    return base + half * HALF + i


def _dup_cols(base):
    return base + np.arange(LANES) % HEAD_DIM


def _head_cols(base):
    lane = np.arange(LANES)
    return np.where(lane < HEAD_DIM, base + lane, -1)


_PAIR_D = _pair_cols(0, 0)
_PAIR_WHICH = (np.arange(LANES) % 64) // HALF


def _gather_cols(w, cols):
    cols = np.asarray(cols)
    safe = np.where(cols >= 0, cols, 0)
    g = jnp.take(w, jnp.asarray(safe, dtype=jnp.int32), axis=1)
    return jnp.where(jnp.asarray(cols >= 0)[None, :], g, 0.0)


def _rope_slabs(T):
    inv = 1.0 / (ROPE_THETA ** (jnp.arange(0, HEAD_DIM, 2, dtype=F32) / HEAD_DIM))
    ang = jnp.arange(T, dtype=F32)[:, None] * inv[None, :]
    cos, sin = jnp.cos(ang), jnp.sin(ang)
    cos_slab = jnp.tile(cos, (1, 4))
    sin_slab = jnp.concatenate([-sin, -sin, sin, sin], axis=1)
    return cos_slab, sin_slab


_KINDS = ("qT", "k", "vT", "vTw", "f", "fT")


def _proj_kernel(x_ref, g_ref, w_ref, cos_ref, sin_ref, *out_refs, plan, kinds, chunk):
    outs = dict(zip(kinds, out_refs))
    x = x_ref[0]
    ms = jnp.mean(x * x, axis=-1, keepdims=True)
    xn = ((x * lax.rsqrt(ms + NORM_EPS)) * g_ref[...]).astype(BF16)
    cos = cos_ref[...]
    sin = sin_ref[...]
    n = len(plan)
    for c0 in range(0, n, chunk):
        c1 = min(c0 + chunk, n)
        r = _dot(xn, w_ref[:, c0 * LANES:c1 * LANES])
        for s in range(c0, c1):
            roped, kind, idx, opt = plan[s]
            y = r[:, (s - c0) * LANES:(s - c0 + 1) * LANES]
            if roped:
                y = y * cos + pltpu.roll(y, 64, 1) * sin
            if kind == "qT":
                if opt is not None:
                    y = y * opt
                outs[kind][0, idx] = y.T.astype(BF16)
            elif kind == "k":
                outs[kind][0, idx] = y.astype(BF16)
            elif kind == "vT":
                if opt == "ones_row_64":
                    y = jnp.where(_iota(y.shape, 1) == HEAD_DIM, 1.0, y)
                    outs[kind][0, idx, 0] = y.T.astype(BF16)
                else:
                    outs[kind][0, idx, 0, 0:LANES, :] = y.T.astype(BF16)
                    outs[kind][0, idx, 0, LANES:, :] = jnp.ones((ONES_ROWS, y.shape[0]), BF16)
            elif kind == "vTw":
                y_t = jnp.where(_iota(y.shape, 1) == HEAD_DIM, 1.0, y).T.astype(BF16)
                for sub in range(y.shape[0] // VW_TILE):
                    outs[kind][0, idx, sub] = y_t[:, sub * VW_TILE:(sub + 1) * VW_TILE]
            elif kind == "f":
                outs[kind][0, idx] = y
            else:
                outs[kind][0, idx] = y.T


def _project(x, g, w, cos_slab, sin_slab, plan, chunk=4):
    B, T, D = x.shape
    tm = KV_TILE
    assert T % tm == 0
    count = {k: sum(1 for p in plan if p[1] == k) for k in _KINDS}
    kinds = tuple(k for k in _KINDS if count[k])
    out_specs, out_shape = [], []
    for k in kinds:
        n = count[k]
        if k in ("qT", "fT"):
            out_specs.append(pl.BlockSpec((1, n, LANES, tm), lambda b, i: (b, 0, 0, i)))
            out_shape.append(jax.ShapeDtypeStruct((B, n, LANES, T), BF16 if k == "qT" else F32))
        elif k in ("k", "f"):
            out_specs.append(pl.BlockSpec((1, n, tm, LANES), lambda b, i: (b, 0, i, 0)))
            out_shape.append(jax.ShapeDtypeStruct((B, n, T, LANES), BF16 if k == "k" else F32))
        elif k == "vTw":
            sub = tm // VW_TILE
            out_specs.append(pl.BlockSpec((1, n, sub, LANES, VW_TILE), lambda b, i: (b, 0, i, 0, 0)))
            out_shape.append(jax.ShapeDtypeStruct((B, n, T // VW_TILE, LANES, VW_TILE), BF16))
        else:
            wide = any(p[1] == "vT" and p[3] != "ones_row_64" for p in plan)
            rows = LANES + ONES_ROWS if wide else LANES
            out_specs.append(pl.BlockSpec((1, n, 1, rows, tm), lambda b, i: (b, 0, i, 0, 0)))
            out_shape.append(jax.ShapeDtypeStruct((B, n, T // tm, rows, tm), BF16))
    kern = functools.partial(_proj_kernel, plan=tuple(plan), kinds=kinds, chunk=chunk)
    outs = pl.pallas_call(
        kern,
        grid=(B, T // tm),
        in_specs=[
            pl.BlockSpec((1, tm, D), lambda b, i: (b, i, 0)),
            pl.BlockSpec((1, D), lambda b, i: (0, 0)),
            pl.BlockSpec((D, len(plan) * LANES), lambda b, i: (0, 0)),
            pl.BlockSpec((tm, LANES), lambda b, i: (i, 0)),
            pl.BlockSpec((tm, LANES), lambda b, i: (i, 0)),
        ],
        out_specs=out_specs,
        out_shape=out_shape,
        compiler_params=_params(("parallel", "parallel")),
        name="norm_proj_rope",
    )(x, g.reshape(1, D), w, cos_slab, sin_slab)
    return dict(zip(kinds, outs))


def _head_rows(slab_t, which):
    row = _iota(slab_t.shape, 0)
    keep = ((row % 64) // HALF) == which
    return jnp.where(keep, slab_t, jnp.zeros_like(slab_t))


def _tree(op, xs):
    while len(xs) > 1:
        xs = [op(xs[i], xs[i + 1]) if i + 1 < len(xs) else xs[i] for i in range(0, len(xs), 2)]
    return xs[0]


def _fold_rows_max(x, ways=4):
    rows, n = x.shape
    per = rows // ways
    parts = [jnp.max(x[i * per:(i + 1) * per].reshape(per // 8, 8, n), axis=0) for i in range(ways)]
    return _tree(jnp.maximum, parts)


ROW_BLOCK = 64


def _online_update(s_sc, p_sc, v_aug, m_ref, acc_ref):
    tk, n = s_sc.shape
    m_prev = m_ref[...]
    m_tile = jnp.max(_fold_rows_max(s_sc[...]), axis=0, keepdims=True)
    m_new = jnp.maximum(m_prev, m_tile)
    alpha = jnp.exp2(m_prev - m_new)
    for r in range(tk // ROW_BLOCK):
        rows = slice(r * ROW_BLOCK, (r + 1) * ROW_BLOCK)
        p_sc[rows, :] = jnp.exp2(s_sc[rows, :] - m_new).astype(BF16)
    acc_ref[...] = alpha * acc_ref[...] + _dot(v_aug, p_sc[...])
    m_ref[...] = m_new


def _flash_pipeline(n, put_scores, update, buf_a, buf_b, put_last=None):
    if put_last is None:
        put_scores(buf_a, 0)
    else:
        @pl.when(n > 0)
        def _():
            put_scores(buf_a, 0)

    def pair(p, carry):
        j = 2 * p
        put_scores(buf_b, j + 1)
        update(buf_a, j)

        @pl.when(j + 2 < n)
        def _():
            put_scores(buf_a, j + 2)
            update(buf_b, j + 1)

        return carry

    lax.fori_loop(0, n // 2, pair, 0)
    in_b = jnp.logical_and(n > 0, n % 2 == 0)
    in_a = n % 2 == 1

    def finish(cur, other):
        if put_last is not None:
            put_last(other, n)
        update(cur, n - 1)
        if put_last is not None:
            update(other, n)

    pl.when(in_b)(lambda: finish(buf_b, buf_a))
    pl.when(in_a)(lambda: finish(buf_a, buf_b))
    if put_last is not None:
        @pl.when(n == 0)
        def _():
            put_last(buf_a, 0)
            update(buf_a, 0)


def _softmax_keys(s):
    m = jnp.max(s, axis=0, keepdims=True)
    m = jnp.where(m > NEG_INF, m, 0.0)
    e = jnp.exp2(s - m)
    return e / jnp.maximum(jnp.sum(e, axis=0, keepdims=True), 1e-30)


def _init_flash(m_ref, acc_ref):
    m_ref[...] = jnp.full(m_ref.shape, M_FLOOR, F32)
    acc_ref[...] = jnp.zeros(acc_ref.shape, F32)


def _diff_attn_kernel(lam_ref, q_ref, k_ref, v_ref, g_ref, o_ref, m_sc, acc_sc, sa_sc, sb_sc,
                      p_sc, *, tq, lambda_init):
    qi = pl.program_id(2)
    q_t = q_ref[0, 0]
    qs = [_head_rows(q_t, c) for c in range(2)]
    for c in range(2):
        _init_flash(m_sc.at[c], acc_sc.at[c])

    def put_scores(buf, j, masked=False):
        k = k_ref[0, 0, pl.ds(pl.multiple_of(j * tq, tq), tq), :]
        for c in range(2):
            s = _dot(k, qs[c])
            if masked:
                s = jnp.where(_iota(s.shape, 0) <= _iota(s.shape, 1), s, NEG_INF)
            buf[c] = s

    def put_diagonal(buf, j):
        put_scores(buf, j, masked=True)

    def update(buf, j):
        v_aug = v_ref[0, 0, j]
        for c in range(2):
            _online_update(buf.at[c], p_sc.at[c], v_aug, m_sc.at[c], acc_sc.at[c])

    _flash_pipeline(qi, put_scores, update, sa_sc, sb_sc, put_last=put_diagonal)

    lam = lam_ref[...]
    s01 = jnp.sum(lam[0:1] * lam[1:2], axis=-1, keepdims=True)
    s23 = jnp.sum(lam[2:3] * lam[3:4], axis=-1, keepdims=True)
    lam_val = jnp.exp(s01) - jnp.exp(s23) + lambda_init
    o0 = acc_sc[0, 0:LANES] / jnp.maximum(acc_sc[0, LANES:LANES + 1], 1e-30)
    o1 = acc_sc[1, 0:LANES] / jnp.maximum(acc_sc[1, LANES:LANES + 1], 1e-30)
    o = o0 - lam_val * o1
    y = o * lax.rsqrt(jnp.mean(o * o, axis=0, keepdims=True) + NORM_EPS)
    y = (y * g_ref[...]) * (1.0 - lambda_init)
    o_ref[0] = y.T.astype(o_ref.dtype)


def _diff_attention(sl, lam, subln_g, lambda_init):
    q_t, k, v_t = sl["qT"], sl["k"], sl["vT"]
    B, H, T, _ = k.shape
    tq = KV_TILE
    nk = T // tq
    kern = functools.partial(_diff_attn_kernel, tq=tq, lambda_init=lambda_init)
    return pl.pallas_call(
        kern,
        grid=(B, H, T // tq),
        in_specs=[
            pl.BlockSpec((4, HEAD_DIM), lambda b, h, i: (0, 0)),
            pl.BlockSpec((1, 1, LANES, tq), lambda b, h, i: (b, h, 0, i)),
            pl.BlockSpec((1, 1, T, LANES), lambda b, h, i: (b, h, 0, 0)),
            pl.BlockSpec((1, 1, nk, LANES + ONES_ROWS, tq), lambda b, h, i: (b, h, 0, 0, 0)),
            pl.BlockSpec((LANES, 1), lambda b, h, i: (0, 0)),
        ],
        out_specs=pl.BlockSpec((1, tq, LANES), lambda b, h, i: (b, i, h)),
        out_shape=jax.ShapeDtypeStruct((B, T, H * LANES), BF16),
        scratch_shapes=[
            pltpu.VMEM((2, 1, tq), F32),
            pltpu.VMEM((2, LANES + ONES_ROWS, tq), F32),
            pltpu.VMEM((2, tq, tq), F32),
            pltpu.VMEM((2, tq, tq), F32),
            pltpu.VMEM((2, tq, tq), BF16),
        ],
        compiler_params=_params(("parallel", "parallel", "arbitrary")),
        name="diff_attention",
    )(lam, q_t, k, v_t, subln_g.reshape(LANES, 1))


def _dsa_select_kernel(qi_ref, ki_ref, misc_ref, tril_ref, bias_ref, key_sc, keyh_sc,
                       *, tq, ck, nk, topk):
    t0 = pl.program_id(1) * tq
    nvalid = (t0 + tq - 1) // ck + 1
    w = misc_ref[0, 0]
    qh = jnp.concatenate([_head_rows(qi_ref[0, h // 2], h % 2) for h in range(IDX_HEADS)], axis=1)
    t_q = t0 + _iota((ck, tq), 1)

    def causal(c):
        return (c * ck + _iota((ck, tq), 0)) <= t_q

    def fill(c, carry):
        kk = ki_ref[0, 0, pl.ds(pl.multiple_of(c * ck, ck), ck), :]
        r = _dot(kk, qh)
        score = jnp.zeros((ck, tq), F32)
        for h in range(IDX_HEADS):
            score = score + w[h:h + 1, :] * jnp.maximum(r[:, h * tq:(h + 1) * tq], 0.0)
        bits = pltpu.bitcast(score, jnp.int32)
        key = bits ^ ((bits >> 31) & jnp.int32(0x7FFFFFFF))
        key = jnp.where(score == 0.0, 0, key)
        key = jnp.where(causal(c), key, INT_MIN)
        key_sc[c] = key
        keyh_sc[c] = (key >> 16).astype(jnp.int16)
        return carry

    lax.fori_loop(0, nvalid, fill, 0)

    def count(pred):
        def chunk(c, acc):
            ind = pred(key_sc[c], c)
            return acc + jnp.sum(ind.reshape(8, ck // 8, tq), axis=0)
        acc = lax.fori_loop(0, nvalid, chunk, jnp.zeros((ck // 8, tq), F32))
        return jnp.sum(acc, axis=0, keepdims=True)

    def count_high(cand_h):
        one = jnp.ones((ck, tq), jnp.int16)
        nil = jnp.zeros((ck, tq), jnp.int16)

        def chunk(c, acc):
            ind = jnp.where(keyh_sc[c] >= cand_h, one, nil)
            per = ck // 8
            return acc + _tree(jnp.add, [ind[i * per:(i + 1) * per] for i in range(8)])
        acc = lax.fori_loop(0, nvalid, chunk, jnp.zeros((ck // 8, tq), jnp.int16))
        return jnp.sum(acc.astype(F32), axis=0, keepdims=True)

    kf = float(topk)

    def high_half(x):
        return (x >> 16).astype(jnp.int16)

    zero = jnp.zeros((1, tq), jnp.int32)
    ans = jnp.where(count_high(high_half(zero)) >= kf, 0, INT_MIN)

    def high_step(i, ans):
        cand = ans | (jnp.int32(1) << (30 - i))
        return jnp.where(count_high(high_half(cand)) >= kf, cand, ans)

    ans = lax.fori_loop(0, 15, high_step, ans)

    def low_half(x):
        return ((x & 0xFFFF) - 32768).astype(jnp.int16)

    top_h = high_half(ans)

    def proxy(c, carry):
        kh = keyh_sc[c]
        mid = jnp.where(kh == top_h, low_half(key_sc[c]), jnp.full((ck, tq), -32768, jnp.int16))
        keyh_sc[c] = jnp.where(kh > top_h, jnp.full((ck, tq), 32767, jnp.int16), mid)
        return carry

    lax.fori_loop(0, nvalid, proxy, 0)

    def low_step(i, ans):
        cand = ans | (jnp.int32(1) << (15 - i))
        return jnp.where(count_high(low_half(cand)) >= kf, cand, ans)

    ans = lax.fori_loop(0, 16, low_step, ans)

    cnt_gt = count(lambda kc, c: jnp.where(kc > ans, 1.0, 0.0))
    cnt_eq = count(lambda kc, c: jnp.where(kc == ans, jnp.where(causal(c), 1.0, 0.0), 0.0))
    need = kf - cnt_gt
    has_tie = jnp.max(cnt_eq - need) > 0.0

    @pl.when(jnp.logical_not(has_tie))
    def _():
        def emit(c, carry):
            sel = jnp.where(key_sc[c] >= ans, jnp.where(causal(c), 0.0, NEG_INF), NEG_INF)
            bias_ref[0, 0, c] = sel.astype(BF16)
            return carry
        lax.fori_loop(0, nvalid, emit, 0)

    @pl.when(has_tie)
    def _():
        def emit(c, carry):
            kc = key_sc[c]
            eq = jnp.where(kc == ans, jnp.where(causal(c), 1.0, 0.0), 0.0)
            pre = _dot(tril_ref[...], eq.astype(BF16)) + carry
            take = jnp.where(kc > ans, 1.0, jnp.where(pre < need, eq, 0.0))
            sel = jnp.where(take > 0.5, jnp.where(causal(c), 0.0, NEG_INF), NEG_INF)
            bias_ref[0, 0, c] = sel.astype(BF16)
            return carry + jnp.sum(eq, axis=0, keepdims=True)
        lax.fori_loop(0, nvalid, emit, jnp.zeros((1, tq), F32))

    def blank(c, carry):
        bias_ref[0, 0, c] = jnp.full((ck, tq), NEG_INF, BF16)
        return carry

    lax.fori_loop(nvalid, nk, blank, 0)


def _dsa_select(sl, lay):
    q_t, k, misc_t = sl["qT"], sl["k"], sl["fT"]
    B, _, T, _ = k.shape
    tq, ck = SEL_Q_TILE, KV_TILE
    nk = T // ck
    topk = min(DSA_TOPK, T // 4)
    tril = jnp.asarray(np.tril(np.ones((ck, ck), np.float32), -1), BF16)
    kern = functools.partial(_dsa_select_kernel, tq=tq, ck=ck, nk=nk, topk=topk)
    return pl.pallas_call(
        kern,
        grid=(B, T // tq),
        in_specs=[
            pl.BlockSpec((1, 2, LANES, tq), lambda b, i: (b, lay["qi"] // 2, 0, i)),
            pl.BlockSpec((1, 1, T, LANES), lambda b, i: (b, lay["ki"], 0, 0)),
            pl.BlockSpec((1, 1, LANES, tq), lambda b, i: (b, lay["misc"], 0, i)),
            pl.BlockSpec((ck, ck), lambda b, i: (0, 0)),
        ],
        out_specs=pl.BlockSpec((1, 1, nk, ck, tq), lambda b, i: (b, i, 0, 0, 0)),
        out_shape=jax.ShapeDtypeStruct((B, T // tq, nk, ck, tq), BF16),
        scratch_shapes=[pltpu.VMEM((nk, ck, tq), jnp.int32), pltpu.VMEM((nk, ck, tq), jnp.int16)],
        compiler_params=_params(("parallel", "parallel")),
        name="dsa_select",
    )(q_t, k, misc_t, tril)


def _store_head_pairs(o_ref, o, n_heads, tq):
    for p in range(n_heads // 2):
        even = o[0:HEAD_DIM, (2 * p) * tq:(2 * p + 1) * tq]
        odd = o[0:HEAD_DIM, (2 * p + 1) * tq:(2 * p + 2) * tq]
        pair = jnp.concatenate([even, odd], axis=0)
        o_ref[0, :, p * LANES:(p + 1) * LANES] = pair.T.astype(o_ref.dtype)


def _normalized(acc):
    return acc[0:HEAD_DIM] / jnp.maximum(acc[HEAD_DIM:HEAD_DIM + 1], 1e-30)


def _dsa_attn_kernel(q_ref, k_ref, v_ref, bias_ref, o_ref, qst, m_sc, acc_sc, sa_sc, sb_sc,
                     p_sc, *, tq, tk):
    qi = pl.program_id(1)
    H = A_HEADS
    for h in range(H):
        qst[:, h * tq:(h + 1) * tq] = _head_rows(q_ref[0, h // 2], h % 2)
    _init_flash(m_sc, acc_sc)
    nkv = (qi * tq) // tk + 1

    def put_scores(buf, j):
        k = k_ref[0, 0, pl.ds(pl.multiple_of(j * tk, tk), tk), :]
        parts = [bias_ref[0, s, j].astype(F32) for s in range(bias_ref.shape[1])]
        b = parts[0] if len(parts) == 1 else jnp.concatenate(parts, axis=1)
        buf[...] = _dot(k, qst[...]) + jnp.concatenate([b] * H, axis=1)

    def update(buf, j):
        _online_update(buf, p_sc, v_ref[0, 0, j], m_sc, acc_sc)

    _flash_pipeline(nkv, put_scores, update, sa_sc, sb_sc)
    _store_head_pairs(o_ref, _normalized(acc_sc[...]), H, tq)


def _dsa_attention(sl, bias, lay):
    q_t, k, v_t = sl["qT"], sl["k"], sl["vT"]
    B, _, T, _ = k.shape
    tq, tk = DSA_Q_TILE, KV_TILE
    nk = T // tk
    H = A_HEADS
    N = H * tq
    sel_tq = bias.shape[4]
    nsub = tq // sel_tq
    kern = functools.partial(_dsa_attn_kernel, tq=tq, tk=tk)
    return pl.pallas_call(
        kern,
        grid=(B, T // tq),
        in_specs=[
            pl.BlockSpec((1, H // 2, LANES, tq), lambda b, i: (b, lay["qA"] // (H // 2), 0, i)),
            pl.BlockSpec((1, 1, T, LANES), lambda b, i: (b, lay["kA"], 0, 0)),
            pl.BlockSpec((1, 1, nk, LANES, tk), lambda b, i: (b, lay["vA"], 0, 0, 0)),
            pl.BlockSpec((1, nsub, nk, tk, sel_tq), lambda b, i: (b, i, 0, 0, 0)),
        ],
        out_specs=pl.BlockSpec((1, tq, H * HEAD_DIM), lambda b, i: (b, i, 0)),
        out_shape=jax.ShapeDtypeStruct((B, T, H * HEAD_DIM), BF16),
        scratch_shapes=[
            pltpu.VMEM((LANES, N), BF16),
            pltpu.VMEM((1, N), F32),
            pltpu.VMEM((LANES, N), F32),
            pltpu.VMEM((tk, N), F32),
            pltpu.VMEM((tk, N), F32),
            pltpu.VMEM((tk, N), BF16),
        ],
        compiler_params=_params(("parallel", "arbitrary")),
        name="dsa_attention",
    )(q_t, k, v_t, bias)


def _compress_kernel(x_ref, pe_ref, w1_ref, w2_ref, o_ref, ot_ref, *, n_ch):
    x = x_ref[0, 0]
    xt = (x + pe_ref[0, 0]).astype(BF16)
    xb = (x + pe_ref[0, 1]).astype(BF16)
    for g in range(B_KV_GROUPS):
        a = _dot(xt, w1_ref[0, g, 0])
        b = _dot(xb, w1_ref[0, g, 1])
        h = jax.nn.gelu(a + pltpu.roll(b, n_ch - 1, 0))
        r = _dot(h.astype(BF16), w2_ref[0])
        o_ref[0, 0, g] = r.astype(BF16)
        ot_ref[0, 0, g] = r.T.astype(BF16)


def _compress(flat, pe, w1, w2):
    _, B, n_ch, W = flat.shape
    G = B_KV_GROUPS
    kern = functools.partial(_compress_kernel, n_ch=n_ch)
    return pl.pallas_call(
        kern,
        grid=(2, B),
        in_specs=[
            pl.BlockSpec((1, 1, n_ch, W), lambda s, b: (s, b, 0, 0)),
            pl.BlockSpec((1, 2, 1, W), lambda s, b: (s, 0, 0, 0)),
            pl.BlockSpec((1, G, 2, W, CMP_HIDDEN), lambda s, b: (s, 0, 0, 0, 0)),
            pl.BlockSpec((1, CMP_HIDDEN, LANES), lambda s, b: (s, 0, 0)),
        ],
        out_specs=[
            pl.BlockSpec((1, 1, G, n_ch, LANES), lambda s, b: (s, b, 0, 0, 0)),
            pl.BlockSpec((1, 1, G, LANES, n_ch), lambda s, b: (s, b, 0, 0, 0)),
        ],
        out_shape=[
            jax.ShapeDtypeStruct((2, B, G, n_ch, LANES), BF16),
            jax.ShapeDtypeStruct((2, B, G, LANES, n_ch), BF16),
        ],
        compiler_params=_params(("parallel", "parallel")),
        name="nsa_compress",
    )(flat, pe, w1, w2)


def _nsa_kernel(q_ref, kc_ref, vc_ref, ks_ref, vs_ref, kw_ref, vw_ref, misc_ref, ov_ref, ex_ref,
                cz_ref, wz_ref, o_ref, qst, m_sc, acc_sc, sa_sc, sb_sc, p_sc, mw_sc, accw_sc,
                sw_sc, pw_sc, *, tq, tk, T, n_s, n_sel, gate_row0):
    g = pl.program_id(1)
    qi = pl.program_id(2)
    t0 = qi * tq
    J = B_PER_GROUP
    N = J * tq
    for j in range(J):
        qst[0:LANES, j * tq:(j + 1) * tq] = _head_rows(q_ref[0, j // 2], j % 2)
    q = qst[0:LANES, :]

    def q_time(shape):
        return t0 + (_iota(shape, 1) % tq)

    wlen = min(WINDOW + tq, T)
    wstart = pl.multiple_of(jnp.maximum(t0 - WINDOW, 0), tq)
    kw = kw_ref[0, 0, pl.ds(wstart, wlen), :]
    n_wt = wlen // tq
    wt = wstart // VW_TILE
    vw = jnp.concatenate([vw_ref[0, 0, wt + i] for i in range(wlen // VW_TILE)], axis=1)
    wz = wz_ref[jnp.minimum(qi, n_wt - 1)].astype(F32)
    sw_sc[...] = _dot(kw, q) + jnp.concatenate([wz] * J, axis=1)
    _init_flash(mw_sc, accw_sc)
    _online_update(sw_sc, pw_sc, vw, mw_sc, accw_sc)

    kc = kc_ref[0, 0, 0]
    n_ch = kc.shape[0]
    s_c = _dot(kc, q)
    cmp_end = _iota((n_ch, N), 0) * CMP_STRIDE + (CMP_BLOCK - 1)
    s_c = jnp.where(cmp_end <= q_time((n_ch, N)), s_c, NEG_INF)
    p_c = _softmax_keys(s_c)
    o_c = _dot(vc_ref[0, 0, 0], p_c.astype(BF16))

    psum = p_c[:, 0:tq]
    for j in range(1, J):
        psum = psum + p_c[:, j * tq:(j + 1) * tq]
    p_hi = psum.astype(BF16)
    p_lo = (psum - p_hi.astype(F32)).astype(BF16)
    imp = _dot(ov_ref[...], p_hi) + _dot(ov_ref[...], p_lo)
    rows = -(-n_s // 8) * 8
    imp = imp[0:rows]
    blk = _iota((rows, tq), 0)
    t_q = t0 + _iota((rows, tq), 1)
    cur = t_q // SLC_BLOCK
    forced = (blk == 0) | (blk == cur) | (blk == cur - 1)
    sc = jnp.where(forced, jnp.inf, imp)
    sc = jnp.where(blk * SLC_BLOCK <= t_q, sc, NEG_INF)
    groups = [sc[8 * r:8 * r + 8] for r in range(rows // 8)]
    ranks = [jnp.zeros((8, tq), F32) for _ in groups]
    blk8 = _iota((8, tq), 0)
    for m in range(n_s):
        cm = sc[m:m + 1, :]
        for r, grp in enumerate(groups):
            gt = jnp.where(cm > grp, 1.0, 0.0)
            ge = jnp.where(cm >= grp, 1.0, 0.0)
            if 8 * r + 7 <= m:
                first = gt
            elif 8 * r > m:
                first = ge
            else:
                first = jnp.where(blk8 + 8 * r > m, ge, gt)
            ranks[r] = ranks[r] + first
    rank = jnp.concatenate(ranks, axis=0)
    drop = jnp.where(rank < float(n_sel), 0.0, 1.0)
    if rows < LANES:
        drop = jnp.concatenate([drop, jnp.ones((LANES - rows, tq), F32)], axis=0)
    drop = drop.astype(BF16)
    qst[LANES:2 * LANES, :] = jnp.concatenate([drop] * J, axis=1)

    _init_flash(m_sc, acc_sc)
    jd = t0 // tk
    off = (t0 - jd * tk) // tq
    n_off = tk // tq

    def put_scores(buf, j):
        k = ks_ref[0, 0, pl.ds(pl.multiple_of(j * tk, tk), tk), :]
        lhs = jnp.concatenate([k, ex_ref[j]], axis=1)
        cz = cz_ref[jnp.where(j == jd, off, n_off)].astype(F32)
        buf[...] = _dot(lhs, qst[...]) + jnp.concatenate([cz] * J, axis=1)

    def update(buf, j):
        _online_update(buf, p_sc, vs_ref[0, 0, j], m_sc, acc_sc)

    _flash_pipeline(jd + 1, put_scores, update, sa_sc, sb_sc)
    o_s = _normalized(acc_sc[...])
    o_w = _normalized(accw_sc[...])

    gates = jax.nn.sigmoid(misc_ref[0, 0])

    def gate_row(c):
        parts = []
        for j in range(J):
            r0 = gate_row0 + j * 3 + c
            r1 = gate_row0 + (J + j) * 3 + c
            parts.append(jnp.where(g == 0, gates[r0:r0 + 1, :], gates[r1:r1 + 1, :]))
        return jnp.concatenate(parts, axis=1)

    o = gate_row(0) * o_c[0:HEAD_DIM] + (gate_row(1) * o_s + gate_row(2) * o_w)
    _store_head_pairs(o_ref, o, J, tq)


def _nsa_attention(sl, cmp_k, cmp_vt, lay):
    q_t, k, v_t, misc_t = sl["qT"], sl["k"], sl["vT"], sl["fT"]
    B, _, T, _ = k.shape
    G, J = B_KV_GROUPS, B_PER_GROUP
    tq, tk = NSA_Q_TILE, KV_TILE
    nk = T // tk
    n_ch = T // CMP_STRIDE
    n_c = n_ch - CMP_BLOCK // CMP_STRIDE + 1
    n_s = T // SLC_BLOCK
    n_sel = min(SLC_TOPN, n_s)
    assert n_s <= LANES
    c0 = np.arange(n_ch) * CMP_STRIDE
    s0 = np.arange(LANES) * SLC_BLOCK
    ov = ((c0[None, :] < s0[:, None] + SLC_BLOCK) & (c0[None, :] + CMP_BLOCK > s0[:, None]))
    ov = ov & (np.arange(n_ch)[None, :] < n_c) & (np.arange(LANES)[:, None] < n_s)
    ov = jnp.asarray(ov.astype(np.float32), BF16)
    pos = np.arange(T).reshape(nk, tk, 1)
    ex = (pos // SLC_BLOCK == np.arange(LANES).reshape(1, 1, LANES))
    ex = jnp.asarray(ex.astype(np.float32) * -BIG, BF16)
    n_off = tk // tq
    kp = np.arange(tk).reshape(1, tk, 1)
    tl = np.arange(tq).reshape(1, 1, tq) + np.arange(n_off + 1).reshape(n_off + 1, 1, 1) * tq
    cz = np.where((kp <= tl) | (np.arange(n_off + 1).reshape(-1, 1, 1) == n_off), 0.0, -BIG)
    cz = jnp.asarray(cz.astype(np.float32), BF16)
    wlen = min(WINDOW + tq, T)
    n_wt = wlen // tq
    assert wlen % tq == 0 and T >= wlen
    kp = np.arange(wlen).reshape(1, wlen, 1)
    tl = np.arange(tq).reshape(1, 1, tq)
    early = kp <= tl + np.arange(n_wt).reshape(n_wt, 1, 1) * tq
    late = (kp > tl) & (kp <= tl + WINDOW)
    band = np.where(np.arange(n_wt).reshape(n_wt, 1, 1) == n_wt - 1, late, early)
    wz = jnp.asarray(np.where(band, 0.0, -BIG).astype(np.float32), BF16)
    kern = functools.partial(_nsa_kernel, tq=tq, tk=tk, T=T, n_s=n_s, n_sel=n_sel,
                             gate_row0=lay["gate_row0"])
    N = J * tq
    kslab = lambda off: pl.BlockSpec((1, 1, T, LANES), lambda b, g, i: (b, off + g, 0, 0))
    return pl.pallas_call(
        kern,
        grid=(B, G, T // tq),
        in_specs=[
            pl.BlockSpec((1, 2, LANES, tq), lambda b, g, i: (b, lay["qB"] // 2 + g, 0, i)),
            pl.BlockSpec((1, 1, 1, n_ch, LANES), lambda b, g, i: (0, b, g, 0, 0)),
            pl.BlockSpec((1, 1, 1, LANES, n_ch), lambda b, g, i: (1, b, g, 0, 0)),
            kslab(lay["ks"]),
            pl.BlockSpec((1, 1, nk, LANES, tk), lambda b, g, i: (b, lay["vs"] + g, 0, 0, 0)),
            kslab(lay["kw"]),
            pl.BlockSpec((1, 1, T // VW_TILE, LANES, VW_TILE),
                         lambda b, g, i: (b, lay["vw"] + g, 0, 0, 0)),
            pl.BlockSpec((1, 1, LANES, tq), lambda b, g, i: (b, lay["misc"], 0, i)),
            pl.BlockSpec((LANES, n_ch), lambda b, g, i: (0, 0)),
            pl.BlockSpec((nk, tk, LANES), lambda b, g, i: (0, 0, 0)),
            pl.BlockSpec((n_off + 1, tk, tq), lambda b, g, i: (0, 0, 0)),
            pl.BlockSpec((n_wt, wlen, tq), lambda b, g, i: (0, 0, 0)),
        ],
        out_specs=pl.BlockSpec((1, tq, J * HEAD_DIM), lambda b, g, i: (b, i, g)),
        out_shape=jax.ShapeDtypeStruct((B, T, B_HEADS * HEAD_DIM), BF16),
        scratch_shapes=[
            pltpu.VMEM((2 * LANES, N), BF16),
            pltpu.VMEM((1, N), F32),
            pltpu.VMEM((LANES, N), F32),
            pltpu.VMEM((tk, N), F32),
            pltpu.VMEM((tk, N), F32),
            pltpu.VMEM((tk, N), BF16),
            pltpu.VMEM((1, N), F32),
            pltpu.VMEM((LANES, N), F32),
            pltpu.VMEM((wlen, N), F32),
            pltpu.VMEM((wlen, N), BF16),
        ],
        compiler_params=_params(("parallel", "parallel", "arbitrary")),
        name="nsa_attention",
    )(q_t, cmp_k, cmp_vt, k, v_t, k, sl["vTw"], misc_t, ov, ex, cz, wz)


def _rms(x, g):
    ms = jnp.mean(x * x, axis=-1, keepdims=True)
    return (x * lax.rsqrt(ms + NORM_EPS)) * g


def _layer_tail_kernel(*refs, n_in, final_norm):
    h_ref = refs[0]
    o_refs = refs[1:1 + n_in]
    w_refs = refs[1 + n_in:1 + 2 * n_in]
    g_ref, wu_ref, wd_ref = refs[1 + 2 * n_in:4 + 2 * n_in]
    rest = refs[4 + 2 * n_in:]
    if final_norm:
        fg_ref, out_ref, xn_sc = rest
    else:
        out_ref, xn_sc = rest
    f = pl.program_id(1)

    @pl.when(f == 0)
    def _():
        mix = _dot(o_refs[0][...], w_refs[0][...])
        for i in range(1, n_in):
            mix = mix + _dot(o_refs[i][...], w_refs[i][...])
        x = h_ref[...] + mix
        xn_sc[...] = _rms(x, g_ref[...]).astype(BF16)
        out_ref[...] = x

    u = _dot(xn_sc[...], wu_ref[...])
    a = jnp.square(jnp.maximum(u, 0.0)).astype(BF16)
    out_ref[...] += _dot(a, wd_ref[...])

    if final_norm:
        @pl.when(f == pl.num_programs(1) - 1)
        def _():
            out_ref[...] = _rms(out_ref[...], fg_ref[...])


def _layer_tail(h2, outs, ws, g, w_up, w_down, final_g=None, tm=1024, tf=1024):
    N, D = h2.shape
    F = w_up.shape[1]
    tm = min(tm, N)
    n_in = len(outs)
    in_specs = [pl.BlockSpec((tm, D), lambda i, f: (i, 0))]
    in_specs += [pl.BlockSpec((tm, o.shape[1]), lambda i, f: (i, 0)) for o in outs]
    in_specs += [pl.BlockSpec(w.shape, lambda i, f: (0, 0)) for w in ws]
    in_specs += [
        pl.BlockSpec((1, D), lambda i, f: (0, 0)),
        pl.BlockSpec((D, tf), lambda i, f: (0, f)),
        pl.BlockSpec((tf, D), lambda i, f: (f, 0)),
    ]
    args = [h2, *outs, *ws, g.reshape(1, D), w_up, w_down]
    if final_g is not None:
        in_specs.append(pl.BlockSpec((1, D), lambda i, f: (0, 0)))
        args.append(final_g.reshape(1, D))
    kern = functools.partial(_layer_tail_kernel, n_in=n_in, final_norm=final_g is not None)
    return pl.pallas_call(
        kern,
        grid=(N // tm, F // tf),
        in_specs=in_specs,
        out_specs=pl.BlockSpec((tm, D), lambda i, f: (i, 0)),
        out_shape=jax.ShapeDtypeStruct((N, D), F32),
        scratch_shapes=[pltpu.VMEM((tm, D), BF16)],
        compiler_params=_params(("parallel", "arbitrary")),
        name="out_proj_mlp",
    )(*args)


def _even_layout():
    offs = {}
    o = 0
    for name, size in (("qa", 512), ("ka", 64), ("va", 64), ("qi", 256), ("ki", 64), ("wi", 4),
                       ("qb", 512), ("kvb", 768), ("gb", 24)):
        offs[name] = o
        o += size
    kvb = lambda which, g: offs["kvb"] + (which * B_KV_GROUPS + g) * HEAD_DIM
    cols, plan, lay = [], [], {}
    n = {k: 0 for k in _KINDS}

    def add(c, roped, kind, opt=None):
        cols.append(c)
        plan.append((roped, kind, n[kind], opt))
        n[kind] += 1
        return n[kind] - 1

    lay["qA"] = n["qT"]
    for p in range(4):
        add(_pair_cols(offs["qa"] + 2 * p * 64, offs["qa"] + (2 * p + 1) * 64), True, "qT", Q_SCALE)
    lay["qB"] = n["qT"]
    for p in range(4):
        add(_pair_cols(offs["qb"] + 2 * p * 64, offs["qb"] + (2 * p + 1) * 64), True, "qT", Q_SCALE)
    lay["qi"] = n["qT"]
    for p in range(2):
        add(_pair_cols(offs["qi"] + 2 * p * 64, offs["qi"] + (2 * p + 1) * 64), True, "qT")
    lay["kA"] = add(_pair_cols(offs["ka"], offs["ka"]), True, "k")
    lay["ki"] = add(_pair_cols(offs["ki"], offs["ki"]), True, "k")
    lay["ks"] = n["k"]
    for g in range(2):
        add(_pair_cols(kvb(2, g), kvb(2, g)), True, "k")
    lay["kw"] = n["k"]
    for g in range(2):
        add(_pair_cols(kvb(4, g), kvb(4, g)), True, "k")
    lay["kc"] = add(_pair_cols(kvb(0, 0), kvb(0, 1)), True, "f")
    lay["vc"] = add(_pair_cols(kvb(1, 0), kvb(1, 1)), False, "f")
    misc = np.full(LANES, -1)
    misc[0:IDX_HEADS] = offs["wi"] + np.arange(IDX_HEADS)
    lay["gate_row0"] = 8
    misc[8:8 + 24] = offs["gb"] + np.arange(24)
    lay["misc"] = add(misc, False, "fT")
    lay["vw"] = n["vTw"]
    for g in range(2):
        add(_head_cols(kvb(5, g)), False, "vTw")
    lay["vA"] = add(_head_cols(offs["va"]), False, "vT", "ones_row_64")
    lay["vs"] = n["vT"]
    for g in range(2):
        add(_head_cols(kvb(3, g)), False, "vT", "ones_row_64")
    return np.concatenate(cols), plan, lay


def _odd_layout():
    cols, plan = [], []
    for h in range(C_HEADS):
        cols.append(_pair_cols(h * 128, h * 128 + 64))
        plan.append((True, "qT", h, Q_SCALE))
    for h in range(C_HEADS):
        cols.append(_pair_cols(1024 + h * 128, 1024 + h * 128 + 64))
        plan.append((True, "k", h, None))
    for h in range(C_HEADS):
        cols.append(2048 + h * 128 + np.arange(LANES))
        plan.append((False, "vT", h, "ones_rows_below"))
    return np.concatenate(cols), plan


def _compress_weights(pe, w1, w2):
    d = _PAIR_D
    which = _PAIR_WHICH
    pe_l = pe[:, :, d]
    pe_l = pe_l.reshape(2, 2, 1, CMP_STRIDE * LANES)
    w1r = w1.reshape(2, CMP_BLOCK, HEAD_DIM, CMP_HIDDEN)[:, :, d, :]
    per_g = []
    for g in range(B_KV_GROUPS):
        keep = jnp.asarray(which == g)[None, None, :, None]
        per_g.append(jnp.where(keep, w1r, 0.0))
    w1g = jnp.stack(per_g, axis=1)
    w1g = w1g.reshape(2, B_KV_GROUPS, 2, CMP_STRIDE * LANES, CMP_HIDDEN).astype(BF16)
    w2k = w2[0][:, d]
    w2v = w2[1][:, np.arange(LANES) % HEAD_DIM]
    w2l = jnp.stack([w2k, w2v], axis=0).astype(BF16)
    return pe_l, w1g, w2l


def _even_mixer(h, norm_g, w_in, cmp_pe, cmp_w1, cmp_w2, w_out, cos_slab, sin_slab):
    B, T, D = h.shape
    cols, plan, lay = _even_layout()
    w = _gather_cols(w_in, cols).astype(BF16)
    sl = _project(h, norm_g, w, cos_slab, sin_slab, plan)

    bias = _dsa_select(sl, lay)
    o_a = _dsa_attention(sl, bias, lay)

    n_ch = T // CMP_STRIDE
    flat = sl["f"][:, lay["kc"]:lay["vc"] + 1].reshape(B, 2, n_ch, CMP_STRIDE * LANES)
    flat = jnp.swapaxes(flat, 0, 1)
    pe_l, w1g, w2l = _compress_weights(cmp_pe, cmp_w1, cmp_w2)
    cmp_k, cmp_vt = _compress(flat, pe_l, w1g, w2l)
    o_b = _nsa_attention(sl, cmp_k, cmp_vt, lay)

    na = A_HEADS * HEAD_DIM
    wo = w_out.astype(BF16)
    return [o_a.reshape(B * T, -1), o_b.reshape(B * T, -1)], [wo[:na], wo[na:]]


def _odd_mixer(h, norm_g, w_in, lam, subln_g, w_out, cos_slab, sin_slab, lambda_init):
    B, T, D = h.shape
    cols, plan = _odd_layout()
    w = _gather_cols(w_in, cols).astype(BF16)
    sl = _project(h, norm_g, w, cos_slab, sin_slab, plan)
    o = _diff_attention(sl, lam, subln_g, lambda_init)
    return [o.reshape(B * T, -1)], [w_out.astype(BF16)]


def kernel(x, mix_norm_g, mlp_norm_g, even_w_in, even_cmp_pe, even_cmp_w1, even_cmp_w2, even_w_out, odd_w_in, odd_lambda, odd_subln_g, odd_w_out, mlp_w_up, mlp_w_down, final_norm_g):
    B, T, D = x.shape
    depth = mix_norm_g.shape[0]
    assert depth >= 1
    cos_slab, sin_slab = _rope_slabs(T)
    h = x
    for layer in range(depth):
        if layer % 2 == 0:
            e = layer // 2
            outs, ws = _even_mixer(h, mix_norm_g[layer], even_w_in[e], even_cmp_pe[e],
                                   even_cmp_w1[e], even_cmp_w2[e], even_w_out[e], cos_slab, sin_slab)
        else:
            o = layer // 2
            lambda_init = 0.8 - 0.6 * math.exp(-0.3 * layer)
            outs, ws = _odd_mixer(h, mix_norm_g[layer], odd_w_in[o], odd_lambda[o], odd_subln_g[o],
                                  odd_w_out[o], cos_slab, sin_slab, lambda_init)
        h2 = _layer_tail(h.reshape(B * T, D), outs, ws, mlp_norm_g[layer],
                         mlp_w_up[layer].astype(BF16), mlp_w_down[layer].astype(BF16),
                         final_g=final_norm_g if layer == depth - 1 else None)
        h = h2.reshape(B, T, D)
    return h
```

```python
import functools
import math

import numpy as np
import jax
import jax.numpy as jnp
from jax import lax
from jax.experimental import pallas as pl
from jax.experimental.pallas import tpu as pltpu

HEAD_DIM = 64
HALF = HEAD_DIM // 2
LANES = 128
ROPE_THETA = 10000.0
NORM_EPS = 1e-6
SCALE = HEAD_DIM ** -0.5

A_HEADS = 8
IDX_HEADS = 4
DSA_TOPK = 256
B_HEADS = 8
B_KV_GROUPS = 2
B_PER_GROUP = B_HEADS // B_KV_GROUPS
CMP_BLOCK = 32
CMP_STRIDE = 16
CMP_HIDDEN = 256
SLC_BLOCK = 64
SLC_TOPN = 16
WINDOW = 512
C_HEADS = 8

KV_TILE = 512
SEL_Q_TILE = 128
DSA_Q_TILE = 256
NSA_Q_TILE = 256
VW_TILE = 128

LOG2E = math.log2(math.e)
Q_SCALE = SCALE * LOG2E
ONES_ROWS = 16
BIG = 2.0 ** 100

NEG_INF = float("-inf")
M_FLOOR = -1e30
INT_MIN = -(2 ** 31)

VMEM_LIMIT = 56 * 1024 * 1024

BF16 = jnp.bfloat16
F32 = jnp.float32


def _dot(a, b):
    return jnp.dot(a, b, preferred_element_type=F32)


def _dot_tn(a, b):
    return lax.dot_general(a, b, (((0,), (0,)), ((), ())), preferred_element_type=F32)


def _params(sem):
    return pltpu.CompilerParams(dimension_semantics=sem, vmem_limit_bytes=VMEM_LIMIT)


def _iota(shape, axis):
    return lax.broadcasted_iota(jnp.int32, shape, axis)


def _pair_cols(base_a, base_b):
    lane = np.arange(LANES)
    half = lane // 64
    which = (lane % 64) // HALF
    i = lane % HALF
    base = np.where(which == 0, base_a, base_b)
    return base + half * HALF + i


def _dup_cols(base):
    return base + np.arange(LANES) % HEAD_DIM


def _head_cols(base):
    lane = np.arange(LANES)
    return np.where(lane < HEAD_DIM, base + lane, -1)


_PAIR_D = _pair_cols(0, 0)
_PAIR_WHICH = (np.arange(LANES) % 64) // HALF


def _gather_cols(w, cols):
    cols = np.asarray(cols)
    safe = np.where(cols >= 0, cols, 0)
    g = jnp.take(w, jnp.asarray(safe, dtype=jnp.int32), axis=1)
    return jnp.where(jnp.asarray(cols >= 0)[None, :], g, 0.0)


def _rope_slabs(T):
    inv = 1.0 / (ROPE_THETA ** (jnp.arange(0, HEAD_DIM, 2, dtype=F32) / HEAD_DIM))
    ang = jnp.arange(T, dtype=F32)[:, None] * inv[None, :]
    cos, sin = jnp.cos(ang), jnp.sin(ang)
    cos_slab = jnp.tile(cos, (1, 4))
    sin_slab = jnp.concatenate([-sin, -sin, sin, sin], axis=1)
    return cos_slab, sin_slab


_KINDS = ("qT", "k", "vT", "vTw", "f", "fT")


def _proj_kernel(x_ref, g_ref, w_ref, cos_ref, sin_ref, *out_refs, plan, kinds, chunk):
    outs = dict(zip(kinds, out_refs))
    x = x_ref[0]
    ms = jnp.mean(x * x, axis=-1, keepdims=True)
    xn = ((x * lax.rsqrt(ms + NORM_EPS)) * g_ref[...]).astype(BF16)
    cos = cos_ref[...]
    sin = sin_ref[...]
    n = len(plan)
    for c0 in range(0, n, chunk):
        c1 = min(c0 + chunk, n)
        r = _dot(xn, w_ref[:, c0 * LANES:c1 * LANES])
        for s in range(c0, c1):
            roped, kind, idx, opt = plan[s]
            y = r[:, (s - c0) * LANES:(s - c0 + 1) * LANES]
            if roped:
                y = y * cos + pltpu.roll(y, 64, 1) * sin
            if kind == "qT":
                if opt is not None:
                    y = y * opt
                outs[kind][0, idx] = y.T.astype(BF16)
            elif kind == "k":
                outs[kind][0, idx] = y.astype(BF16)
            elif kind == "vT":
                if opt == "ones_row_64":
                    y = jnp.where(_iota(y.shape, 1) == HEAD_DIM, 1.0, y)
                    outs[kind][0, idx, 0] = y.T.astype(BF16)
                else:
                    outs[kind][0, idx, 0, 0:LANES, :] = y.T.astype(BF16)
                    outs[kind][0, idx, 0, LANES:, :] = jnp.ones((ONES_ROWS, y.shape[0]), BF16)
            elif kind == "vTw":
                y_t = jnp.where(_iota(y.shape, 1) == HEAD_DIM, 1.0, y).T.astype(BF16)
                for sub in range(y.shape[0] // VW_TILE):
                    outs[kind][0, idx, sub] = y_t[:, sub * VW_TILE:(sub + 1) * VW_TILE]
            elif kind == "f":
                outs[kind][0, idx] = y
            else:
                outs[kind][0, idx] = y.T


def _project(x, g, w, cos_slab, sin_slab, plan, chunk=4):
    B, T, D = x.shape
    tm = KV_TILE
    assert T % tm == 0
    count = {k: sum(1 for p in plan if p[1] == k) for k in _KINDS}
    kinds = tuple(k for k in _KINDS if count[k])
    out_specs, out_shape = [], []
    for k in kinds:
        n = count[k]
        if k in ("qT", "fT"):
            out_specs.append(pl.BlockSpec((1, n, LANES, tm), lambda b, i: (b, 0, 0, i)))
            out_shape.append(jax.ShapeDtypeStruct((B, n, LANES, T), BF16 if k == "qT" else F32))
        elif k in ("k", "f"):
            out_specs.append(pl.BlockSpec((1, n, tm, LANES), lambda b, i: (b, 0, i, 0)))
            out_shape.append(jax.ShapeDtypeStruct((B, n, T, LANES), BF16 if k == "k" else F32))
        elif k == "vTw":
            sub = tm // VW_TILE
            out_specs.append(pl.BlockSpec((1, n, sub, LANES, VW_TILE), lambda b, i: (b, 0, i, 0, 0)))
            out_shape.append(jax.ShapeDtypeStruct((B, n, T // VW_TILE, LANES, VW_TILE), BF16))
        else:
            wide = any(p[1] == "vT" and p[3] != "ones_row_64" for p in plan)
            rows = LANES + ONES_ROWS if wide else LANES
            out_specs.append(pl.BlockSpec((1, n, 1, rows, tm), lambda b, i: (b, 0, i, 0, 0)))
            out_shape.append(jax.ShapeDtypeStruct((B, n, T // tm, rows, tm), BF16))
    kern = functools.partial(_proj_kernel, plan=tuple(plan), kinds=kinds, chunk=chunk)
    outs = pl.pallas_call(
        kern,
        grid=(B, T // tm),
        in_specs=[
            pl.BlockSpec((1, tm, D), lambda b, i: (b, i, 0)),
            pl.BlockSpec((1, D), lambda b, i: (0, 0)),
            pl.BlockSpec((D, len(plan) * LANES), lambda b, i: (0, 0)),
            pl.BlockSpec((tm, LANES), lambda b, i: (i, 0)),
            pl.BlockSpec((tm, LANES), lambda b, i: (i, 0)),
        ],
        out_specs=out_specs,
        out_shape=out_shape,
        compiler_params=_params(("parallel", "parallel")),
        name="norm_proj_rope",
    )(x, g.reshape(1, D), w, cos_slab, sin_slab)
    return dict(zip(kinds, outs))


def _head_rows(slab_t, which):
    row = _iota(slab_t.shape, 0)
    keep = ((row % 64) // HALF) == which
    return jnp.where(keep, slab_t, jnp.zeros_like(slab_t))


def _tree(op, xs):
    while len(xs) > 1:
        xs = [op(xs[i], xs[i + 1]) if i + 1 < len(xs) else xs[i] for i in range(0, len(xs), 2)]
    return xs[0]


def _fold_rows_max(x, ways=4):
    rows, n = x.shape
    per = rows // ways
    parts = [jnp.max(x[i * per:(i + 1) * per].reshape(per // 8, 8, n), axis=0) for i in range(ways)]
    return _tree(jnp.maximum, parts)


ROW_BLOCK = 64


def _online_update(s_sc, p_sc, v_aug, m_ref, acc_ref):
    tk, n = s_sc.shape
    m_prev = m_ref[...]
    m_tile = jnp.max(_fold_rows_max(s_sc[...]), axis=0, keepdims=True)
    m_new = jnp.maximum(m_prev, m_tile)
    alpha = jnp.exp2(m_prev - m_new)
    for r in range(tk // ROW_BLOCK):
        rows = slice(r * ROW_BLOCK, (r + 1) * ROW_BLOCK)
        p_sc[rows, :] = jnp.exp2(s_sc[rows, :] - m_new).astype(BF16)
    acc_ref[...] = alpha * acc_ref[...] + _dot(v_aug, p_sc[...])
    m_ref[...] = m_new


def _flash_pipeline(n, put_scores, update, buf_a, buf_b, put_last=None):
    if put_last is None:
        put_scores(buf_a, 0)
    else:
        @pl.when(n > 0)
        def _():
            put_scores(buf_a, 0)

    def pair(p, carry):
        j = 2 * p
        put_scores(buf_b, j + 1)
        update(buf_a, j)

        @pl.when(j + 2 < n)
        def _():
            put_scores(buf_a, j + 2)
            update(buf_b, j + 1)

        return carry

    lax.fori_loop(0, n // 2, pair, 0)
    in_b = jnp.logical_and(n > 0, n % 2 == 0)
    in_a = n % 2 == 1

    def finish(cur, other):
        if put_last is not None:
            put_last(other, n)
        update(cur, n - 1)
        if put_last is not None:
            update(other, n)

    pl.when(in_b)(lambda: finish(buf_b, buf_a))
    pl.when(in_a)(lambda: finish(buf_a, buf_b))
    if put_last is not None:
        @pl.when(n == 0)
        def _():
            put_last(buf_a, 0)
            update(buf_a, 0)


def _softmax_keys(s):
    m = jnp.max(s, axis=0, keepdims=True)
    m = jnp.where(m > NEG_INF, m, 0.0)
    e = jnp.exp2(s - m)
    return e / jnp.maximum(jnp.sum(e, axis=0, keepdims=True), 1e-30)


def _init_flash(m_ref, acc_ref):
    m_ref[...] = jnp.full(m_ref.shape, M_FLOOR, F32)
    acc_ref[...] = jnp.zeros(acc_ref.shape, F32)


def _diff_attn_kernel(lam_ref, q_ref, k_ref, v_ref, g_ref, o_ref, m_sc, acc_sc, sa_sc, sb_sc,
                      p_sc, *, tq, lambda_init):
    qi = pl.program_id(2)
    q_t = q_ref[0, 0]
    qs = [_head_rows(q_t, c) for c in range(2)]
    for c in range(2):
        _init_flash(m_sc.at[c], acc_sc.at[c])

    def put_scores(buf, j, masked=False):
        k = k_ref[0, 0, pl.ds(pl.multiple_of(j * tq, tq), tq), :]
        for c in range(2):
            s = _dot(k, qs[c])
            if masked:
                s = jnp.where(_iota(s.shape, 0) <= _iota(s.shape, 1), s, NEG_INF)
            buf[c] = s

    def put_diagonal(buf, j):
        put_scores(buf, j, masked=True)

    def update(buf, j):
        v_aug = v_ref[0, 0, j]
        for c in range(2):
            _online_update(buf.at[c], p_sc.at[c], v_aug, m_sc.at[c], acc_sc.at[c])

    _flash_pipeline(qi, put_scores, update, sa_sc, sb_sc, put_last=put_diagonal)

    lam = lam_ref[...]
    s01 = jnp.sum(lam[0:1] * lam[1:2], axis=-1, keepdims=True)
    s23 = jnp.sum(lam[2:3] * lam[3:4], axis=-1, keepdims=True)
    lam_val = jnp.exp(s01) - jnp.exp(s23) + lambda_init
    o0 = acc_sc[0, 0:LANES] / jnp.maximum(acc_sc[0, LANES:LANES + 1], 1e-30)
    o1 = acc_sc[1, 0:LANES] / jnp.maximum(acc_sc[1, LANES:LANES + 1], 1e-30)
    o = o0 - lam_val * o1
    y = o * lax.rsqrt(jnp.mean(o * o, axis=0, keepdims=True) + NORM_EPS)
    y = (y * g_ref[...]) * (1.0 - lambda_init)
    o_ref[0] = y.T.astype(o_ref.dtype)


def _diff_attention(sl, lam, subln_g, lambda_init):
    q_t, k, v_t = sl["qT"], sl["k"], sl["vT"]
    B, H, T, _ = k.shape
    tq = KV_TILE
    nk = T // tq
    kern = functools.partial(_diff_attn_kernel, tq=tq, lambda_init=lambda_init)
    return pl.pallas_call(
        kern,
        grid=(B, H, T // tq),
        in_specs=[
            pl.BlockSpec((4, HEAD_DIM), lambda b, h, i: (0, 0)),
            pl.BlockSpec((1, 1, LANES, tq), lambda b, h, i: (b, h, 0, i)),
            pl.BlockSpec((1, 1, T, LANES), lambda b, h, i: (b, h, 0, 0)),
            pl.BlockSpec((1, 1, nk, LANES + ONES_ROWS, tq), lambda b, h, i: (b, h, 0, 0, 0)),
            pl.BlockSpec((LANES, 1), lambda b, h, i: (0, 0)),
        ],
        out_specs=pl.BlockSpec((1, tq, LANES), lambda b, h, i: (b, i, h)),
        out_shape=jax.ShapeDtypeStruct((B, T, H * LANES), BF16),
        scratch_shapes=[
            pltpu.VMEM((2, 1, tq), F32),
            pltpu.VMEM((2, LANES + ONES_ROWS, tq), F32),
            pltpu.VMEM((2, tq, tq), F32),
            pltpu.VMEM((2, tq, tq), F32),
            pltpu.VMEM((2, tq, tq), BF16),
        ],
        compiler_params=_params(("parallel", "parallel", "arbitrary")),
        name="diff_attention",
    )(lam, q_t, k, v_t, subln_g.reshape(LANES, 1))


def _dsa_select_kernel(qi_ref, ki_ref, misc_ref, tril_ref, bias_ref, key_sc, *, tq, ck, nk, topk):
    t0 = pl.program_id(1) * tq
    nvalid = (t0 + tq - 1) // ck + 1
    w = misc_ref[0, 0]
    qh = jnp.concatenate([_head_rows(qi_ref[0, h // 2], h % 2) for h in range(IDX_HEADS)], axis=1)
    t_q = t0 + _iota((ck, tq), 1)

    def causal(c):
        return (c * ck + _iota((ck, tq), 0)) <= t_q

    def fill(c, carry):
        kk = ki_ref[0, 0, pl.ds(pl.multiple_of(c * ck, ck), ck), :]
        r = _dot(kk, qh)
        score = jnp.zeros((ck, tq), F32)
        for h in range(IDX_HEADS):
            score = score + w[h:h + 1, :] * jnp.maximum(r[:, h * tq:(h + 1) * tq], 0.0)
        bits = pltpu.bitcast(score, jnp.int32)
        key = bits ^ ((bits >> 31) & jnp.int32(0x7FFFFFFF))
        key = jnp.where(score == 0.0, 0, key)
        key_sc[c] = jnp.where(causal(c), key, INT_MIN)
        return carry

    lax.fori_loop(0, nvalid, fill, 0)

    kf = float(topk)

    def select_threshold(n):
        def count(pred):
            acc = jnp.zeros((ck // 8, tq), F32)
            for c in range(n):
                ind = pred(key_sc[c], c)
                acc = acc + jnp.sum(ind.reshape(8, ck // 8, tq), axis=0)
            return jnp.sum(acc, axis=0, keepdims=True)

        zero = jnp.zeros((1, tq), jnp.int32)
        ans = jnp.where(count(lambda kc, c: jnp.where(kc >= zero, 1.0, 0.0)) >= kf, 0, INT_MIN)

        def bit_step(i, ans):
            cand = ans | (jnp.int32(1) << (30 - i))
            cnt = count(lambda kc, c: jnp.where(kc >= cand, 1.0, 0.0))
            return jnp.where(cnt >= kf, cand, ans)

        ans = lax.fori_loop(0, 31, bit_step, ans)
        cnt_gt = count(lambda kc, c: jnp.where(kc > ans, 1.0, 0.0))
        cnt_eq = count(lambda kc, c: jnp.where(kc == ans, jnp.where(causal(c), 1.0, 0.0), 0.0))
        return ans, cnt_gt, cnt_eq

    ans, cnt_gt, cnt_eq = lax.switch(
        nvalid - 1, [functools.partial(select_threshold, n) for n in range(1, nk + 1)])
    need = kf - cnt_gt
    has_tie = jnp.max(cnt_eq - need) > 0.0

    @pl.when(jnp.logical_not(has_tie))
    def _():
        def emit(c, carry):
            sel = jnp.where(key_sc[c] >= ans, jnp.where(causal(c), 0.0, NEG_INF), NEG_INF)
            bias_ref[0, 0, c] = sel.astype(BF16)
            return carry
        lax.fori_loop(0, nvalid, emit, 0)

    @pl.when(has_tie)
    def _():
        def emit(c, carry):
            kc = key_sc[c]
            eq = jnp.where(kc == ans, jnp.where(causal(c), 1.0, 0.0), 0.0)
            pre = _dot(tril_ref[...], eq.astype(BF16)) + carry
            take = jnp.where(kc > ans, 1.0, jnp.where(pre < need, eq, 0.0))
            sel = jnp.where(take > 0.5, jnp.where(causal(c), 0.0, NEG_INF), NEG_INF)
            bias_ref[0, 0, c] = sel.astype(BF16)
            return carry + jnp.sum(eq, axis=0, keepdims=True)
        lax.fori_loop(0, nvalid, emit, jnp.zeros((1, tq), F32))

    def blank(c, carry):
        bias_ref[0, 0, c] = jnp.full((ck, tq), NEG_INF, BF16)
        return carry

    lax.fori_loop(nvalid, nk, blank, 0)


def _dsa_select(sl, lay):
    q_t, k, misc_t = sl["qT"], sl["k"], sl["fT"]
    B, _, T, _ = k.shape
    tq, ck = SEL_Q_TILE, KV_TILE
    nk = T // ck
    topk = min(DSA_TOPK, T // 4)
    tril = jnp.asarray(np.tril(np.ones((ck, ck), np.float32), -1), BF16)
    kern = functools.partial(_dsa_select_kernel, tq=tq, ck=ck, nk=nk, topk=topk)
    return pl.pallas_call(
        kern,
        grid=(B, T // tq),
        in_specs=[
            pl.BlockSpec((1, 2, LANES, tq), lambda b, i: (b, lay["qi"] // 2, 0, i)),
            pl.BlockSpec((1, 1, T, LANES), lambda b, i: (b, lay["ki"], 0, 0)),
            pl.BlockSpec((1, 1, LANES, tq), lambda b, i: (b, lay["misc"], 0, i)),
            pl.BlockSpec((ck, ck), lambda b, i: (0, 0)),
        ],
        out_specs=pl.BlockSpec((1, 1, nk, ck, tq), lambda b, i: (b, i, 0, 0, 0)),
        out_shape=jax.ShapeDtypeStruct((B, T // tq, nk, ck, tq), BF16),
        scratch_shapes=[pltpu.VMEM((nk, ck, tq), jnp.int32)],
        compiler_params=_params(("parallel", "parallel")),
        name="dsa_select",
    )(q_t, k, misc_t, tril)


def _store_head_pairs(o_ref, o, n_heads, tq):
    for p in range(n_heads // 2):
        even = o[0:HEAD_DIM, (2 * p) * tq:(2 * p + 1) * tq]
        odd = o[0:HEAD_DIM, (2 * p + 1) * tq:(2 * p + 2) * tq]
        pair = jnp.concatenate([even, odd], axis=0)
        o_ref[0, :, p * LANES:(p + 1) * LANES] = pair.T.astype(o_ref.dtype)


def _normalized(acc):
    return acc[0:HEAD_DIM] / jnp.maximum(acc[HEAD_DIM:HEAD_DIM + 1], 1e-30)


def _dsa_attn_kernel(q_ref, k_ref, v_ref, bias_ref, o_ref, qst, m_sc, acc_sc, sa_sc, sb_sc,
                     p_sc, *, tq, tk):
    qi = pl.program_id(1)
    H = A_HEADS
    for h in range(H):
        qst[:, h * tq:(h + 1) * tq] = _head_rows(q_ref[0, h // 2], h % 2)
    _init_flash(m_sc, acc_sc)
    nkv = (qi * tq) // tk + 1

    def put_scores(buf, j):
        k = k_ref[0, 0, pl.ds(pl.multiple_of(j * tk, tk), tk), :]
        parts = [bias_ref[0, s, j].astype(F32) for s in range(bias_ref.shape[1])]
        b = parts[0] if len(parts) == 1 else jnp.concatenate(parts, axis=1)
        buf[...] = _dot(k, qst[...]) + jnp.concatenate([b] * H, axis=1)

    def update(buf, j):
        _online_update(buf, p_sc, v_ref[0, 0, j], m_sc, acc_sc)

    _flash_pipeline(nkv, put_scores, update, sa_sc, sb_sc)
    _store_head_pairs(o_ref, _normalized(acc_sc[...]), H, tq)


def _dsa_attention(sl, bias, lay):
    q_t, k, v_t = sl["qT"], sl["k"], sl["vT"]
    B, _, T, _ = k.shape
    tq, tk = DSA_Q_TILE, KV_TILE
    nk = T // tk
    H = A_HEADS
    N = H * tq
    sel_tq = bias.shape[4]
    nsub = tq // sel_tq
    kern = functools.partial(_dsa_attn_kernel, tq=tq, tk=tk)
    return pl.pallas_call(
        kern,
        grid=(B, T // tq),
        in_specs=[
            pl.BlockSpec((1, H // 2, LANES, tq), lambda b, i: (b, lay["qA"] // (H // 2), 0, i)),
            pl.BlockSpec((1, 1, T, LANES), lambda b, i: (b, lay["kA"], 0, 0)),
            pl.BlockSpec((1, 1, nk, LANES, tk), lambda b, i: (b, lay["vA"], 0, 0, 0)),
            pl.BlockSpec((1, nsub, nk, tk, sel_tq), lambda b, i: (b, i, 0, 0, 0)),
        ],
        out_specs=pl.BlockSpec((1, tq, H * HEAD_DIM), lambda b, i: (b, i, 0)),
        out_shape=jax.ShapeDtypeStruct((B, T, H * HEAD_DIM), BF16),
        scratch_shapes=[
            pltpu.VMEM((LANES, N), BF16),
            pltpu.VMEM((1, N), F32),
            pltpu.VMEM((LANES, N), F32),
            pltpu.VMEM((tk, N), F32),
            pltpu.VMEM((tk, N), F32),
            pltpu.VMEM((tk, N), BF16),
        ],
        compiler_params=_params(("parallel", "arbitrary")),
        name="dsa_attention",
    )(q_t, k, v_t, bias)


def _compress_kernel(x_ref, pe_ref, w1_ref, w2_ref, o_ref, ot_ref, *, n_ch):
    x = x_ref[0, 0]
    xt = (x + pe_ref[0, 0]).astype(BF16)
    xb = (x + pe_ref[0, 1]).astype(BF16)
    for g in range(B_KV_GROUPS):
        a = _dot(xt, w1_ref[0, g, 0])
        b = _dot(xb, w1_ref[0, g, 1])
        h = jax.nn.gelu(a + pltpu.roll(b, n_ch - 1, 0))
        r = _dot(h.astype(BF16), w2_ref[0])
        o_ref[0, 0, g] = r.astype(BF16)
        ot_ref[0, 0, g] = r.T.astype(BF16)


def _compress(flat, pe, w1, w2):
    _, B, n_ch, W = flat.shape
    G = B_KV_GROUPS
    kern = functools.partial(_compress_kernel, n_ch=n_ch)
    return pl.pallas_call(
        kern,
        grid=(2, B),
        in_specs=[
            pl.BlockSpec((1, 1, n_ch, W), lambda s, b: (s, b, 0, 0)),
            pl.BlockSpec((1, 2, 1, W), lambda s, b: (s, 0, 0, 0)),
            pl.BlockSpec((1, G, 2, W, CMP_HIDDEN), lambda s, b: (s, 0, 0, 0, 0)),
            pl.BlockSpec((1, CMP_HIDDEN, LANES), lambda s, b: (s, 0, 0)),
        ],
        out_specs=[
            pl.BlockSpec((1, 1, G, n_ch, LANES), lambda s, b: (s, b, 0, 0, 0)),
            pl.BlockSpec((1, 1, G, LANES, n_ch), lambda s, b: (s, b, 0, 0, 0)),
        ],
        out_shape=[
            jax.ShapeDtypeStruct((2, B, G, n_ch, LANES), BF16),
            jax.ShapeDtypeStruct((2, B, G, LANES, n_ch), BF16),
        ],
        compiler_params=_params(("parallel", "parallel")),
        name="nsa_compress",
    )(flat, pe, w1, w2)


def _nsa_kernel(q_ref, kc_ref, vc_ref, ks_ref, vs_ref, kw_ref, vw_ref, misc_ref, ov_ref, ex_ref,
                cz_ref, wz_ref, o_ref, qst, m_sc, acc_sc, sa_sc, sb_sc, p_sc, mw_sc, accw_sc,
                sw_sc, pw_sc, *, tq, tk, T, n_s, n_sel, gate_row0):
    g = pl.program_id(1)
    qi = pl.program_id(2)
    t0 = qi * tq
    J = B_PER_GROUP
    N = J * tq
    for j in range(J):
        qst[0:LANES, j * tq:(j + 1) * tq] = _head_rows(q_ref[0, j // 2], j % 2)
    q = qst[0:LANES, :]

    def q_time(shape):
        return t0 + (_iota(shape, 1) % tq)

    wlen = min(WINDOW + tq, T)
    wstart = pl.multiple_of(jnp.maximum(t0 - WINDOW, 0), tq)
    kw = kw_ref[0, 0, pl.ds(wstart, wlen), :]
    n_wt = wlen // tq
    wt = wstart // VW_TILE
    vw = jnp.concatenate([vw_ref[0, 0, wt + i] for i in range(wlen // VW_TILE)], axis=1)
    wz = wz_ref[jnp.minimum(qi, n_wt - 1)].astype(F32)
    sw_sc[...] = _dot(kw, q) + jnp.concatenate([wz] * J, axis=1)
    _init_flash(mw_sc, accw_sc)
    _online_update(sw_sc, pw_sc, vw, mw_sc, accw_sc)

    kc = kc_ref[0, 0, 0]
    n_ch = kc.shape[0]
    s_c = _dot(kc, q)
    cmp_end = _iota((n_ch, N), 0) * CMP_STRIDE + (CMP_BLOCK - 1)
    s_c = jnp.where(cmp_end <= q_time((n_ch, N)), s_c, NEG_INF)
    p_c = _softmax_keys(s_c)
    o_c = _dot(vc_ref[0, 0, 0], p_c.astype(BF16))

    psum = p_c[:, 0:tq]
    for j in range(1, J):
        psum = psum + p_c[:, j * tq:(j + 1) * tq]
    p_hi = psum.astype(BF16)
    p_lo = (psum - p_hi.astype(F32)).astype(BF16)
    imp = _dot(ov_ref[...], p_hi) + _dot(ov_ref[...], p_lo)
    rows = -(-n_s // 8) * 8
    imp = imp[0:rows]
    blk = _iota((rows, tq), 0)
    t_q = t0 + _iota((rows, tq), 1)
    cur = t_q // SLC_BLOCK
    forced = (blk == 0) | (blk == cur) | (blk == cur - 1)
    sc = jnp.where(forced, jnp.inf, imp)
    sc = jnp.where(blk * SLC_BLOCK <= t_q, sc, NEG_INF)
    groups = [sc[8 * r:8 * r + 8] for r in range(rows // 8)]
    ranks = [jnp.zeros((8, tq), F32) for _ in groups]
    blk8 = _iota((8, tq), 0)
    for m in range(n_s):
        cm = sc[m:m + 1, :]
        for r, grp in enumerate(groups):
            gt = jnp.where(cm > grp, 1.0, 0.0)
            ge = jnp.where(cm >= grp, 1.0, 0.0)
            if 8 * r + 7 <= m:
                first = gt
            elif 8 * r > m:
                first = ge
            else:
                first = jnp.where(blk8 + 8 * r > m, ge, gt)
            ranks[r] = ranks[r] + first
    rank = jnp.concatenate(ranks, axis=0)
    drop = jnp.where(rank < float(n_sel), 0.0, 1.0)
    if rows < LANES:
        drop = jnp.concatenate([drop, jnp.ones((LANES - rows, tq), F32)], axis=0)
    drop = drop.astype(BF16)
    qst[LANES:2 * LANES, :] = jnp.concatenate([drop] * J, axis=1)

    _init_flash(m_sc, acc_sc)
    jd = t0 // tk
    off = (t0 - jd * tk) // tq
    n_off = tk // tq

    def put_scores(buf, j):
        k = ks_ref[0, 0, pl.ds(pl.multiple_of(j * tk, tk), tk), :]
        lhs = jnp.concatenate([k, ex_ref[j]], axis=1)
        cz = cz_ref[jnp.where(j == jd, off, n_off)].astype(F32)
        buf[...] = _dot(lhs, qst[...]) + jnp.concatenate([cz] * J, axis=1)

    def update(buf, j):
        _online_update(buf, p_sc, vs_ref[0, 0, j], m_sc, acc_sc)

    _flash_pipeline(jd + 1, put_scores, update, sa_sc, sb_sc)
    o_s = _normalized(acc_sc[...])
    o_w = _normalized(accw_sc[...])

    gates = jax.nn.sigmoid(misc_ref[0, 0])

    def gate_row(c):
        parts = []
        for j in range(J):
            r0 = gate_row0 + j * 3 + c
            r1 = gate_row0 + (J + j) * 3 + c
            parts.append(jnp.where(g == 0, gates[r0:r0 + 1, :], gates[r1:r1 + 1, :]))
        return jnp.concatenate(parts, axis=1)

    o = gate_row(0) * o_c[0:HEAD_DIM] + (gate_row(1) * o_s + gate_row(2) * o_w)
    _store_head_pairs(o_ref, o, J, tq)


def _nsa_attention(sl, cmp_k, cmp_vt, lay):
    q_t, k, v_t, misc_t = sl["qT"], sl["k"], sl["vT"], sl["fT"]
    B, _, T, _ = k.shape
    G, J = B_KV_GROUPS, B_PER_GROUP
    tq, tk = NSA_Q_TILE, KV_TILE
    nk = T // tk
    n_ch = T // CMP_STRIDE
    n_c = n_ch - CMP_BLOCK // CMP_STRIDE + 1
    n_s = T // SLC_BLOCK
    n_sel = min(SLC_TOPN, n_s)
    assert n_s <= LANES
    c0 = np.arange(n_ch) * CMP_STRIDE
    s0 = np.arange(LANES) * SLC_BLOCK
    ov = ((c0[None, :] < s0[:, None] + SLC_BLOCK) & (c0[None, :] + CMP_BLOCK > s0[:, None]))
    ov = ov & (np.arange(n_ch)[None, :] < n_c) & (np.arange(LANES)[:, None] < n_s)
    ov = jnp.asarray(ov.astype(np.float32), BF16)
    pos = np.arange(T).reshape(nk, tk, 1)
    ex = (pos // SLC_BLOCK == np.arange(LANES).reshape(1, 1, LANES))
    ex = jnp.asarray(ex.astype(np.float32) * -BIG, BF16)
    n_off = tk // tq
    kp = np.arange(tk).reshape(1, tk, 1)
    tl = np.arange(tq).reshape(1, 1, tq) + np.arange(n_off + 1).reshape(n_off + 1, 1, 1) * tq
    cz = np.where((kp <= tl) | (np.arange(n_off + 1).reshape(-1, 1, 1) == n_off), 0.0, -BIG)
    cz = jnp.asarray(cz.astype(np.float32), BF16)
    wlen = min(WINDOW + tq, T)
    n_wt = wlen // tq
    assert wlen % tq == 0 and T >= wlen
    kp = np.arange(wlen).reshape(1, wlen, 1)
    tl = np.arange(tq).reshape(1, 1, tq)
    early = kp <= tl + np.arange(n_wt).reshape(n_wt, 1, 1) * tq
    late = (kp > tl) & (kp <= tl + WINDOW)
    band = np.where(np.arange(n_wt).reshape(n_wt, 1, 1) == n_wt - 1, late, early)
    wz = jnp.asarray(np.where(band, 0.0, -BIG).astype(np.float32), BF16)
    kern = functools.partial(_nsa_kernel, tq=tq, tk=tk, T=T, n_s=n_s, n_sel=n_sel,
                             gate_row0=lay["gate_row0"])
    N = J * tq
    kslab = lambda off: pl.BlockSpec((1, 1, T, LANES), lambda b, g, i: (b, off + g, 0, 0))
    return pl.pallas_call(
        kern,
        grid=(B, G, T // tq),
        in_specs=[
            pl.BlockSpec((1, 2, LANES, tq), lambda b, g, i: (b, lay["qB"] // 2 + g, 0, i)),
            pl.BlockSpec((1, 1, 1, n_ch, LANES), lambda b, g, i: (0, b, g, 0, 0)),
            pl.BlockSpec((1, 1, 1, LANES, n_ch), lambda b, g, i: (1, b, g, 0, 0)),
            kslab(lay["ks"]),
            pl.BlockSpec((1, 1, nk, LANES, tk), lambda b, g, i: (b, lay["vs"] + g, 0, 0, 0)),
            kslab(lay["kw"]),
            pl.BlockSpec((1, 1, T // VW_TILE, LANES, VW_TILE),
                         lambda b, g, i: (b, lay["vw"] + g, 0, 0, 0)),
            pl.BlockSpec((1, 1, LANES, tq), lambda b, g, i: (b, lay["misc"], 0, i)),
            pl.BlockSpec((LANES, n_ch), lambda b, g, i: (0, 0)),
            pl.BlockSpec((nk, tk, LANES), lambda b, g, i: (0, 0, 0)),
            pl.BlockSpec((n_off + 1, tk, tq), lambda b, g, i: (0, 0, 0)),
            pl.BlockSpec((n_wt, wlen, tq), lambda b, g, i: (0, 0, 0)),
        ],
        out_specs=pl.BlockSpec((1, tq, J * HEAD_DIM), lambda b, g, i: (b, i, g)),
        out_shape=jax.ShapeDtypeStruct((B, T, B_HEADS * HEAD_DIM), BF16),
        scratch_shapes=[
            pltpu.VMEM((2 * LANES, N), BF16),
            pltpu.VMEM((1, N), F32),
            pltpu.VMEM((LANES, N), F32),
            pltpu.VMEM((tk, N), F32),
            pltpu.VMEM((tk, N), F32),
            pltpu.VMEM((tk, N), BF16),
            pltpu.VMEM((1, N), F32),
            pltpu.VMEM((LANES, N), F32),
            pltpu.VMEM((wlen, N), F32),
            pltpu.VMEM((wlen, N), BF16),
        ],
        compiler_params=_params(("parallel", "parallel", "arbitrary")),
        name="nsa_attention",
    )(q_t, cmp_k, cmp_vt, k, v_t, k, sl["vTw"], misc_t, ov, ex, cz, wz)


def _rms(x, g):
    ms = jnp.mean(x * x, axis=-1, keepdims=True)
    return (x * lax.rsqrt(ms + NORM_EPS)) * g


def _layer_tail_kernel(*refs, n_in, final_norm):
    h_ref = refs[0]
    o_refs = refs[1:1 + n_in]
    w_refs = refs[1 + n_in:1 + 2 * n_in]
    g_ref, wu_ref, wd_ref = refs[1 + 2 * n_in:4 + 2 * n_in]
    rest = refs[4 + 2 * n_in:]
    if final_norm:
        fg_ref, out_ref, xn_sc = rest
    else:
        out_ref, xn_sc = rest
    f = pl.program_id(1)

    @pl.when(f == 0)
    def _():
        mix = _dot(o_refs[0][...], w_refs[0][...])
        for i in range(1, n_in):
            mix = mix + _dot(o_refs[i][...], w_refs[i][...])
        x = h_ref[...] + mix
        xn_sc[...] = _rms(x, g_ref[...]).astype(BF16)
        out_ref[...] = x

    u = _dot(xn_sc[...], wu_ref[...])
    a = jnp.square(jnp.maximum(u, 0.0)).astype(BF16)
    out_ref[...] += _dot(a, wd_ref[...])

    if final_norm:
        @pl.when(f == pl.num_programs(1) - 1)
        def _():
            out_ref[...] = _rms(out_ref[...], fg_ref[...])


def _layer_tail(h2, outs, ws, g, w_up, w_down, final_g=None, tm=1024, tf=1024):
    N, D = h2.shape
    F = w_up.shape[1]
    tm = min(tm, N)
    n_in = len(outs)
    in_specs = [pl.BlockSpec((tm, D), lambda i, f: (i, 0))]
    in_specs += [pl.BlockSpec((tm, o.shape[1]), lambda i, f: (i, 0)) for o in outs]
    in_specs += [pl.BlockSpec(w.shape, lambda i, f: (0, 0)) for w in ws]
    in_specs += [
        pl.BlockSpec((1, D), lambda i, f: (0, 0)),
        pl.BlockSpec((D, tf), lambda i, f: (0, f)),
        pl.BlockSpec((tf, D), lambda i, f: (f, 0)),
    ]
    args = [h2, *outs, *ws, g.reshape(1, D), w_up, w_down]
    if final_g is not None:
        in_specs.append(pl.BlockSpec((1, D), lambda i, f: (0, 0)))
        args.append(final_g.reshape(1, D))
    kern = functools.partial(_layer_tail_kernel, n_in=n_in, final_norm=final_g is not None)
    return pl.pallas_call(
        kern,
        grid=(N // tm, F // tf),
        in_specs=in_specs,
        out_specs=pl.BlockSpec((tm, D), lambda i, f: (i, 0)),
        out_shape=jax.ShapeDtypeStruct((N, D), F32),
        scratch_shapes=[pltpu.VMEM((tm, D), BF16)],
        compiler_params=_params(("parallel", "arbitrary")),
        name="out_proj_mlp",
    )(*args)


def _even_layout():
    offs = {}
    o = 0
    for name, size in (("qa", 512), ("ka", 64), ("va", 64), ("qi", 256), ("ki", 64), ("wi", 4),
                       ("qb", 512), ("kvb", 768), ("gb", 24)):
        offs[name] = o
        o += size
    kvb = lambda which, g: offs["kvb"] + (which * B_KV_GROUPS + g) * HEAD_DIM
    cols, plan, lay = [], [], {}
    n = {k: 0 for k in _KINDS}

    def add(c, roped, kind, opt=None):
        cols.append(c)
        plan.append((roped, kind, n[kind], opt))
        n[kind] += 1
        return n[kind] - 1

    lay["qA"] = n["qT"]
    for p in range(4):
        add(_pair_cols(offs["qa"] + 2 * p * 64, offs["qa"] + (2 * p + 1) * 64), True, "qT", Q_SCALE)
    lay["qB"] = n["qT"]
    for p in range(4):
        add(_pair_cols(offs["qb"] + 2 * p * 64, offs["qb"] + (2 * p + 1) * 64), True, "qT", Q_SCALE)
    lay["qi"] = n["qT"]
    for p in range(2):
        add(_pair_cols(offs["qi"] + 2 * p * 64, offs["qi"] + (2 * p + 1) * 64), True, "qT")
    lay["kA"] = add(_pair_cols(offs["ka"], offs["ka"]), True, "k")
    lay["ki"] = add(_pair_cols(offs["ki"], offs["ki"]), True, "k")
    lay["ks"] = n["k"]
    for g in range(2):
        add(_pair_cols(kvb(2, g), kvb(2, g)), True, "k")
    lay["kw"] = n["k"]
    for g in range(2):
        add(_pair_cols(kvb(4, g), kvb(4, g)), True, "k")
    lay["kc"] = add(_pair_cols(kvb(0, 0), kvb(0, 1)), True, "f")
    lay["vc"] = add(_pair_cols(kvb(1, 0), kvb(1, 1)), False, "f")
    misc = np.full(LANES, -1)
    misc[0:IDX_HEADS] = offs["wi"] + np.arange(IDX_HEADS)
    lay["gate_row0"] = 8
    misc[8:8 + 24] = offs["gb"] + np.arange(24)
    lay["misc"] = add(misc, False, "fT")
    lay["vw"] = n["vTw"]
    for g in range(2):
        add(_head_cols(kvb(5, g)), False, "vTw")
    lay["vA"] = add(_head_cols(offs["va"]), False, "vT", "ones_row_64")
    lay["vs"] = n["vT"]
    for g in range(2):
        add(_head_cols(kvb(3, g)), False, "vT", "ones_row_64")
    return np.concatenate(cols), plan, lay


def _odd_layout():
    cols, plan = [], []
    for h in range(C_HEADS):
        cols.append(_pair_cols(h * 128, h * 128 + 64))
        plan.append((True, "qT", h, Q_SCALE))
    for h in range(C_HEADS):
        cols.append(_pair_cols(1024 + h * 128, 1024 + h * 128 + 64))
        plan.append((True, "k", h, None))
    for h in range(C_HEADS):
        cols.append(2048 + h * 128 + np.arange(LANES))
        plan.append((False, "vT", h, "ones_rows_below"))
    return np.concatenate(cols), plan


def _compress_weights(pe, w1, w2):
    d = _PAIR_D
    which = _PAIR_WHICH
    pe_l = pe[:, :, d]
    pe_l = pe_l.reshape(2, 2, 1, CMP_STRIDE * LANES)
    w1r = w1.reshape(2, CMP_BLOCK, HEAD_DIM, CMP_HIDDEN)[:, :, d, :]
    per_g = []
    for g in range(B_KV_GROUPS):
        keep = jnp.asarray(which == g)[None, None, :, None]
        per_g.append(jnp.where(keep, w1r, 0.0))
    w1g = jnp.stack(per_g, axis=1)
    w1g = w1g.reshape(2, B_KV_GROUPS, 2, CMP_STRIDE * LANES, CMP_HIDDEN).astype(BF16)
    w2k = w2[0][:, d]
    w2v = w2[1][:, np.arange(LANES) % HEAD_DIM]
    w2l = jnp.stack([w2k, w2v], axis=0).astype(BF16)
    return pe_l, w1g, w2l


def _even_mixer(h, norm_g, w_in, cmp_pe, cmp_w1, cmp_w2, w_out, cos_slab, sin_slab):
    B, T, D = h.shape
    cols, plan, lay = _even_layout()
    w = _gather_cols(w_in, cols).astype(BF16)
    sl = _project(h, norm_g, w, cos_slab, sin_slab, plan)

    bias = _dsa_select(sl, lay)
    o_a = _dsa_attention(sl, bias, lay)

    n_ch = T // CMP_STRIDE
    flat = sl["f"][:, lay["kc"]:lay["vc"] + 1].reshape(B, 2, n_ch, CMP_STRIDE * LANES)
    flat = jnp.swapaxes(flat, 0, 1)
    pe_l, w1g, w2l = _compress_weights(cmp_pe, cmp_w1, cmp_w2)
    cmp_k, cmp_vt = _compress(flat, pe_l, w1g, w2l)
    o_b = _nsa_attention(sl, cmp_k, cmp_vt, lay)

    na = A_HEADS * HEAD_DIM
    wo = w_out.astype(BF16)
    return [o_a.reshape(B * T, -1), o_b.reshape(B * T, -1)], [wo[:na], wo[na:]]


def _odd_mixer(h, norm_g, w_in, lam, subln_g, w_out, cos_slab, sin_slab, lambda_init):
    B, T, D = h.shape
    cols, plan = _odd_layout()
    w = _gather_cols(w_in, cols).astype(BF16)
    sl = _project(h, norm_g, w, cos_slab, sin_slab, plan)
    o = _diff_attention(sl, lam, subln_g, lambda_init)
    return [o.reshape(B * T, -1)], [w_out.astype(BF16)]


def kernel(x, mix_norm_g, mlp_norm_g, even_w_in, even_cmp_pe, even_cmp_w1, even_cmp_w2, even_w_out, odd_w_in, odd_lambda, odd_subln_g, odd_w_out, mlp_w_up, mlp_w_down, final_norm_g):
    B, T, D = x.shape
    depth = mix_norm_g.shape[0]
    assert depth >= 1
    cos_slab, sin_slab = _rope_slabs(T)
    h = x
    for layer in range(depth):
        if layer % 2 == 0:
            e = layer // 2
            outs, ws = _even_mixer(h, mix_norm_g[layer], even_w_in[e], even_cmp_pe[e],
                                   even_cmp_w1[e], even_cmp_w2[e], even_w_out[e], cos_slab, sin_slab)
        else:
            o = layer // 2
            lambda_init = 0.8 - 0.6 * math.exp(-0.3 * layer)
            outs, ws = _odd_mixer(h, mix_norm_g[layer], odd_w_in[o], odd_lambda[o], odd_subln_g[o],
                                  odd_w_out[o], cos_slab, sin_slab, lambda_init)
        h2 = _layer_tail(h.reshape(B * T, D), outs, ws, mlp_norm_g[layer],
                         mlp_w_up[layer].astype(BF16), mlp_w_down[layer].astype(BF16),
                         final_g=final_norm_g if layer == depth - 1 else None)
        h = h2.reshape(B, T, D)
    return h
```

```python
import functools
import math

import numpy as np
import jax
import jax.numpy as jnp
from jax import lax
from jax.experimental import pallas as pl
from jax.experimental.pallas import tpu as pltpu

HEAD_DIM = 64
HALF = HEAD_DIM // 2
LANES = 128
ROPE_THETA = 10000.0
NORM_EPS = 1e-6
SCALE = HEAD_DIM ** -0.5

A_HEADS = 8
IDX_HEADS = 4
DSA_TOPK = 256
B_HEADS = 8
B_KV_GROUPS = 2
B_PER_GROUP = B_HEADS // B_KV_GROUPS
CMP_BLOCK = 32
CMP_STRIDE = 16
CMP_HIDDEN = 256
SLC_BLOCK = 64
SLC_TOPN = 16
WINDOW = 512
C_HEADS = 8

KV_TILE = 512
SEL_Q_TILE = 128
DSA_Q_TILE = 256
NSA_Q_TILE = 256
VW_TILE = 128

LOG2E = math.log2(math.e)
Q_SCALE = SCALE * LOG2E
ONES_ROWS = 16
BIG = 2.0 ** 100

NEG_INF = float("-inf")
M_FLOOR = -1e30
INT_MIN = -(2 ** 31)

VMEM_LIMIT = 56 * 1024 * 1024

BF16 = jnp.bfloat16
F32 = jnp.float32


def _dot(a, b):
    return jnp.dot(a, b, preferred_element_type=F32)


def _dot_tn(a, b):
    return lax.dot_general(a, b, (((0,), (0,)), ((), ())), preferred_element_type=F32)


def _params(sem):
    return pltpu.CompilerParams(dimension_semantics=sem, vmem_limit_bytes=VMEM_LIMIT)


def _iota(shape, axis):
    return lax.broadcasted_iota(jnp.int32, shape, axis)


def _pair_cols(base_a, base_b):
    lane = np.arange(LANES)
    half = lane // 64
    which = (lane % 64) // HALF
    i = lane % HALF
    base = np.where(which == 0, base_a, base_b)
    return base + half * HALF + i


def _dup_cols(base):
    return base + np.arange(LANES) % HEAD_DIM


def _head_cols(base):
    lane = np.arange(LANES)
    return np.where(lane < HEAD_DIM, base + lane, -1)


_PAIR_D = _pair_cols(0, 0)
_PAIR_WHICH = (np.arange(LANES) % 64) // HALF


def _gather_cols(w, cols):
    cols = np.asarray(cols)
    safe = np.where(cols >= 0, cols, 0)
    g = jnp.take(w, jnp.asarray(safe, dtype=jnp.int32), axis=1)
    return jnp.where(jnp.asarray(cols >= 0)[None, :], g, 0.0)


def _rope_slabs(T):
    inv = 1.0 / (ROPE_THETA ** (jnp.arange(0, HEAD_DIM, 2, dtype=F32) / HEAD_DIM))
    ang = jnp.arange(T, dtype=F32)[:, None] * inv[None, :]
    cos, sin = jnp.cos(ang), jnp.sin(ang)
    cos_slab = jnp.tile(cos, (1, 4))
    sin_slab = jnp.concatenate([-sin, -sin, sin, sin], axis=1)
    return cos_slab, sin_slab


_KINDS = ("qT", "k", "vT", "vTw", "f", "fT")


def _proj_kernel(x_ref, g_ref, w_ref, cos_ref, sin_ref, *out_refs, plan, kinds, chunk):
    outs = dict(zip(kinds, out_refs))
    x = x_ref[0]
    ms = jnp.mean(x * x, axis=-1, keepdims=True)
    xn = ((x * lax.rsqrt(ms + NORM_EPS)) * g_ref[...]).astype(BF16)
    cos = cos_ref[...]
    sin = sin_ref[...]
    n = len(plan)
    for c0 in range(0, n, chunk):
        c1 = min(c0 + chunk, n)
        r = _dot(xn, w_ref[:, c0 * LANES:c1 * LANES])
        for s in range(c0, c1):
            roped, kind, idx, opt = plan[s]
            y = r[:, (s - c0) * LANES:(s - c0 + 1) * LANES]
            if roped:
                y = y * cos + pltpu.roll(y, 64, 1) * sin
            if kind == "qT":
                if opt is not None:
                    y = y * opt
                outs[kind][0, idx] = y.T.astype(BF16)
            elif kind == "k":
                outs[kind][0, idx] = y.astype(BF16)
            elif kind == "vT":
                if opt == "ones_row_64":
                    y = jnp.where(_iota(y.shape, 1) == HEAD_DIM, 1.0, y)
                    outs[kind][0, idx, 0] = y.T.astype(BF16)
                else:
                    outs[kind][0, idx, 0, 0:LANES, :] = y.T.astype(BF16)
                    outs[kind][0, idx, 0, LANES:, :] = jnp.ones((ONES_ROWS, y.shape[0]), BF16)
            elif kind == "vTw":
                y_t = jnp.where(_iota(y.shape, 1) == HEAD_DIM, 1.0, y).T.astype(BF16)
                for sub in range(y.shape[0] // VW_TILE):
                    outs[kind][0, idx, sub] = y_t[:, sub * VW_TILE:(sub + 1) * VW_TILE]
            elif kind == "f":
                outs[kind][0, idx] = y
            else:
                outs[kind][0, idx] = y.T


def _project(x, g, w, cos_slab, sin_slab, plan, chunk=4):
    B, T, D = x.shape
    tm = KV_TILE
    assert T % tm == 0
    count = {k: sum(1 for p in plan if p[1] == k) for k in _KINDS}
    kinds = tuple(k for k in _KINDS if count[k])
    out_specs, out_shape = [], []
    for k in kinds:
        n = count[k]
        if k in ("qT", "fT"):
            out_specs.append(pl.BlockSpec((1, n, LANES, tm), lambda b, i: (b, 0, 0, i)))
            out_shape.append(jax.ShapeDtypeStruct((B, n, LANES, T), BF16 if k == "qT" else F32))
        elif k in ("k", "f"):
            out_specs.append(pl.BlockSpec((1, n, tm, LANES), lambda b, i: (b, 0, i, 0)))
            out_shape.append(jax.ShapeDtypeStruct((B, n, T, LANES), BF16 if k == "k" else F32))
        elif k == "vTw":
            sub = tm // VW_TILE
            out_specs.append(pl.BlockSpec((1, n, sub, LANES, VW_TILE), lambda b, i: (b, 0, i, 0, 0)))
            out_shape.append(jax.ShapeDtypeStruct((B, n, T // VW_TILE, LANES, VW_TILE), BF16))
        else:
            wide = any(p[1] == "vT" and p[3] != "ones_row_64" for p in plan)
            rows = LANES + ONES_ROWS if wide else LANES
            out_specs.append(pl.BlockSpec((1, n, 1, rows, tm), lambda b, i: (b, 0, i, 0, 0)))
            out_shape.append(jax.ShapeDtypeStruct((B, n, T // tm, rows, tm), BF16))
    kern = functools.partial(_proj_kernel, plan=tuple(plan), kinds=kinds, chunk=chunk)
    outs = pl.pallas_call(
        kern,
        grid=(B, T // tm),
        in_specs=[
            pl.BlockSpec((1, tm, D), lambda b, i: (b, i, 0)),
            pl.BlockSpec((1, D), lambda b, i: (0, 0)),
            pl.BlockSpec((D, len(plan) * LANES), lambda b, i: (0, 0)),
            pl.BlockSpec((tm, LANES), lambda b, i: (i, 0)),
            pl.BlockSpec((tm, LANES), lambda b, i: (i, 0)),
        ],
        out_specs=out_specs,
        out_shape=out_shape,
        compiler_params=_params(("parallel", "parallel")),
        name="norm_proj_rope",
    )(x, g.reshape(1, D), w, cos_slab, sin_slab)
    return dict(zip(kinds, outs))


def _head_rows(slab_t, which):
    row = _iota(slab_t.shape, 0)
    keep = ((row % 64) // HALF) == which
    return jnp.where(keep, slab_t, jnp.zeros_like(slab_t))


def _tree(op, xs):
    while len(xs) > 1:
        xs = [op(xs[i], xs[i + 1]) if i + 1 < len(xs) else xs[i] for i in range(0, len(xs), 2)]
    return xs[0]


def _fold_rows_max(x, ways=4):
    rows, n = x.shape
    per = rows // ways
    parts = [jnp.max(x[i * per:(i + 1) * per].reshape(per // 8, 8, n), axis=0) for i in range(ways)]
    return _tree(jnp.maximum, parts)


ROW_BLOCK = 64


def _online_update(s_sc, p_sc, v_aug, m_ref, acc_ref):
    tk, n = s_sc.shape
    m_prev = m_ref[...]
    m_tile = jnp.max(_fold_rows_max(s_sc[...]), axis=0, keepdims=True)
    m_new = jnp.maximum(m_prev, m_tile)
    alpha = jnp.exp2(m_prev - m_new)
    for r in range(tk // ROW_BLOCK):
        rows = slice(r * ROW_BLOCK, (r + 1) * ROW_BLOCK)
        p_sc[rows, :] = jnp.exp2(s_sc[rows, :] - m_new).astype(BF16)
    acc_ref[...] = alpha * acc_ref[...] + _dot(v_aug, p_sc[...])
    m_ref[...] = m_new


def _flash_pipeline(n, put_scores, update, buf_a, buf_b, put_last=None):
    if put_last is None:
        put_scores(buf_a, 0)
    else:
        @pl.when(n > 0)
        def _():
            put_scores(buf_a, 0)

    def pair(p, carry):
        j = 2 * p
        put_scores(buf_b, j + 1)
        update(buf_a, j)

        @pl.when(j + 2 < n)
        def _():
            put_scores(buf_a, j + 2)
            update(buf_b, j + 1)

        return carry

    lax.fori_loop(0, n // 2, pair, 0)
    in_b = jnp.logical_and(n > 0, n % 2 == 0)
    in_a = n % 2 == 1

    def finish(cur, other):
        if put_last is not None:
            put_last(other, n)
        update(cur, n - 1)
        if put_last is not None:
            update(other, n)

    pl.when(in_b)(lambda: finish(buf_b, buf_a))
    pl.when(in_a)(lambda: finish(buf_a, buf_b))
    if put_last is not None:
        @pl.when(n == 0)
        def _():
            put_last(buf_a, 0)
            update(buf_a, 0)


def _softmax_keys(s):
    m = jnp.max(s, axis=0, keepdims=True)
    m = jnp.where(m > NEG_INF, m, 0.0)
    e = jnp.exp2(s - m)
    return e / jnp.maximum(jnp.sum(e, axis=0, keepdims=True), 1e-30)


def _init_flash(m_ref, acc_ref):
    m_ref[...] = jnp.full(m_ref.shape, M_FLOOR, F32)
    acc_ref[...] = jnp.zeros(acc_ref.shape, F32)


def _diff_attn_kernel(lam_ref, q_ref, k_ref, v_ref, g_ref, o_ref, m_sc, acc_sc, sa_sc, sb_sc,
                      p_sc, *, tq, lambda_init):
    qi = pl.program_id(2)
    q_t = q_ref[0, 0]
    qs = [_head_rows(q_t, c) for c in range(2)]
    for c in range(2):
        _init_flash(m_sc.at[c], acc_sc.at[c])

    def put_scores(buf, j, masked=False):
        k = k_ref[0, 0, pl.ds(pl.multiple_of(j * tq, tq), tq), :]
        for c in range(2):
            s = _dot(k, qs[c])
            if masked:
                s = jnp.where(_iota(s.shape, 0) <= _iota(s.shape, 1), s, NEG_INF)
            buf[c] = s

    def put_diagonal(buf, j):
        put_scores(buf, j, masked=True)

    def update(buf, j):
        v_aug = v_ref[0, 0, j]
        for c in range(2):
            _online_update(buf.at[c], p_sc.at[c], v_aug, m_sc.at[c], acc_sc.at[c])

    _flash_pipeline(qi, put_scores, update, sa_sc, sb_sc, put_last=put_diagonal)

    lam = lam_ref[...]
    s01 = jnp.sum(lam[0:1] * lam[1:2], axis=-1, keepdims=True)
    s23 = jnp.sum(lam[2:3] * lam[3:4], axis=-1, keepdims=True)
    lam_val = jnp.exp(s01) - jnp.exp(s23) + lambda_init
    o0 = acc_sc[0, 0:LANES] / jnp.maximum(acc_sc[0, LANES:LANES + 1], 1e-30)
    o1 = acc_sc[1, 0:LANES] / jnp.maximum(acc_sc[1, LANES:LANES + 1], 1e-30)
    o = o0 - lam_val * o1
    y = o * lax.rsqrt(jnp.mean(o * o, axis=0, keepdims=True) + NORM_EPS)
    y = (y * g_ref[...]) * (1.0 - lambda_init)
    o_ref[0] = y.T.astype(o_ref.dtype)


def _diff_attention(sl, lam, subln_g, lambda_init):
    q_t, k, v_t = sl["qT"], sl["k"], sl["vT"]
    B, H, T, _ = k.shape
    tq = KV_TILE
    nk = T // tq
    kern = functools.partial(_diff_attn_kernel, tq=tq, lambda_init=lambda_init)
    return pl.pallas_call(
        kern,
        grid=(B, H, T // tq),
        in_specs=[
            pl.BlockSpec((4, HEAD_DIM), lambda b, h, i: (0, 0)),
            pl.BlockSpec((1, 1, LANES, tq), lambda b, h, i: (b, h, 0, i)),
            pl.BlockSpec((1, 1, T, LANES), lambda b, h, i: (b, h, 0, 0)),
            pl.BlockSpec((1, 1, nk, LANES + ONES_ROWS, tq), lambda b, h, i: (b, h, 0, 0, 0)),
            pl.BlockSpec((LANES, 1), lambda b, h, i: (0, 0)),
        ],
        out_specs=pl.BlockSpec((1, tq, LANES), lambda b, h, i: (b, i, h)),
        out_shape=jax.ShapeDtypeStruct((B, T, H * LANES), BF16),
        scratch_shapes=[
            pltpu.VMEM((2, 1, tq), F32),
            pltpu.VMEM((2, LANES + ONES_ROWS, tq), F32),
            pltpu.VMEM((2, tq, tq), F32),
            pltpu.VMEM((2, tq, tq), F32),
            pltpu.VMEM((2, tq, tq), BF16),
        ],
        compiler_params=_params(("parallel", "parallel", "arbitrary")),
        name="diff_attention",
    )(lam, q_t, k, v_t, subln_g.reshape(LANES, 1))


def _dsa_select_kernel(qi_ref, ki_ref, misc_ref, tril_ref, bias_ref, key_sc, *, tq, ck, nk, topk):
    t0 = pl.program_id(1) * tq
    nvalid = (t0 + tq - 1) // ck + 1
    w = misc_ref[0, 0]
    qh = jnp.concatenate([_head_rows(qi_ref[0, h // 2], h % 2) for h in range(IDX_HEADS)], axis=1)
    t_q = t0 + _iota((ck, tq), 1)

    def causal(c):
        return (c * ck + _iota((ck, tq), 0)) <= t_q

    def fill(c):
        kk = ki_ref[0, 0, c * ck:(c + 1) * ck, :]
        r = _dot(kk, qh)
        score = jnp.zeros((ck, tq), F32)
        for h in range(IDX_HEADS):
            score = score + w[h:h + 1, :] * jnp.maximum(r[:, h * tq:(h + 1) * tq], 0.0)
        bits = pltpu.bitcast(score, jnp.int32)
        key = bits ^ ((bits >> 31) & jnp.int32(0x7FFFFFFF))
        key = jnp.where(score == 0.0, 0, key)
        key_sc[c] = jnp.where(causal(c), key, INT_MIN)

    kf = float(topk)

    def select_threshold(n):
        for c in range(n):
            fill(c)

        def count(pred):
            acc = jnp.zeros((ck // 8, tq), F32)
            for c in range(n):
                ind = pred(key_sc[c], c)
                acc = acc + jnp.sum(ind.reshape(8, ck // 8, tq), axis=0)
            return jnp.sum(acc, axis=0, keepdims=True)

        zero = jnp.zeros((1, tq), jnp.int32)
        ans = jnp.where(count(lambda kc, c: jnp.where(kc >= zero, 1.0, 0.0)) >= kf, 0, INT_MIN)

        def bit_step(i, ans):
            cand = ans | (jnp.int32(1) << (30 - i))
            cnt = count(lambda kc, c: jnp.where(kc >= cand, 1.0, 0.0))
            return jnp.where(cnt >= kf, cand, ans)

        ans = lax.fori_loop(0, 31, bit_step, ans)
        cnt_gt = count(lambda kc, c: jnp.where(kc > ans, 1.0, 0.0))
        cnt_eq = count(lambda kc, c: jnp.where(kc == ans, jnp.where(causal(c), 1.0, 0.0), 0.0))
        return ans, cnt_gt, cnt_eq

    ans, cnt_gt, cnt_eq = lax.switch(
        nvalid - 1, [functools.partial(select_threshold, n) for n in range(1, nk + 1)])
    need = kf - cnt_gt
    has_tie = jnp.max(cnt_eq - need) > 0.0

    @pl.when(jnp.logical_not(has_tie))
    def _():
        def emit(c, carry):
            sel = jnp.where(key_sc[c] >= ans, jnp.where(causal(c), 0.0, NEG_INF), NEG_INF)
            bias_ref[0, 0, c] = sel.astype(BF16)
            return carry
        lax.fori_loop(0, nvalid, emit, 0)

    @pl.when(has_tie)
    def _():
        def emit(c, carry):
            kc = key_sc[c]
            eq = jnp.where(kc == ans, jnp.where(causal(c), 1.0, 0.0), 0.0)
            pre = _dot(tril_ref[...], eq.astype(BF16)) + carry
            take = jnp.where(kc > ans, 1.0, jnp.where(pre < need, eq, 0.0))
            sel = jnp.where(take > 0.5, jnp.where(causal(c), 0.0, NEG_INF), NEG_INF)
            bias_ref[0, 0, c] = sel.astype(BF16)
            return carry + jnp.sum(eq, axis=0, keepdims=True)
        lax.fori_loop(0, nvalid, emit, jnp.zeros((1, tq), F32))

    def blank(c, carry):
        bias_ref[0, 0, c] = jnp.full((ck, tq), NEG_INF, BF16)
        return carry

    lax.fori_loop(nvalid, nk, blank, 0)


def _dsa_select(sl, lay):
    q_t, k, misc_t = sl["qT"], sl["k"], sl["fT"]
    B, _, T, _ = k.shape
    tq, ck = SEL_Q_TILE, KV_TILE
    nk = T // ck
    topk = min(DSA_TOPK, T // 4)
    tril = jnp.asarray(np.tril(np.ones((ck, ck), np.float32), -1), BF16)
    kern = functools.partial(_dsa_select_kernel, tq=tq, ck=ck, nk=nk, topk=topk)
    return pl.pallas_call(
        kern,
        grid=(B, T // tq),
        in_specs=[
            pl.BlockSpec((1, 2, LANES, tq), lambda b, i: (b, lay["qi"] // 2, 0, i)),
            pl.BlockSpec((1, 1, T, LANES), lambda b, i: (b, lay["ki"], 0, 0)),
            pl.BlockSpec((1, 1, LANES, tq), lambda b, i: (b, lay["misc"], 0, i)),
            pl.BlockSpec((ck, ck), lambda b, i: (0, 0)),
        ],
        out_specs=pl.BlockSpec((1, 1, nk, ck, tq), lambda b, i: (b, i, 0, 0, 0)),
        out_shape=jax.ShapeDtypeStruct((B, T // tq, nk, ck, tq), BF16),
        scratch_shapes=[pltpu.VMEM((nk, ck, tq), jnp.int32)],
        compiler_params=_params(("parallel", "parallel")),
        name="dsa_select",
    )(q_t, k, misc_t, tril)


def _store_head_pairs(o_ref, o, n_heads, tq):
    for p in range(n_heads // 2):
        even = o[0:HEAD_DIM, (2 * p) * tq:(2 * p + 1) * tq]
        odd = o[0:HEAD_DIM, (2 * p + 1) * tq:(2 * p + 2) * tq]
        pair = jnp.concatenate([even, odd], axis=0)
        o_ref[0, :, p * LANES:(p + 1) * LANES] = pair.T.astype(o_ref.dtype)


def _normalized(acc):
    return acc[0:HEAD_DIM] / jnp.maximum(acc[HEAD_DIM:HEAD_DIM + 1], 1e-30)


def _dsa_attn_kernel(q_ref, k_ref, v_ref, bias_ref, o_ref, qst, m_sc, acc_sc, sa_sc, sb_sc,
                     p_sc, *, tq, tk):
    qi = pl.program_id(1)
    H = A_HEADS
    for h in range(H):
        qst[:, h * tq:(h + 1) * tq] = _head_rows(q_ref[0, h // 2], h % 2)
    _init_flash(m_sc, acc_sc)
    nkv = (qi * tq) // tk + 1

    def put_scores(buf, j):
        k = k_ref[0, 0, pl.ds(pl.multiple_of(j * tk, tk), tk), :]
        parts = [bias_ref[0, s, j].astype(F32) for s in range(bias_ref.shape[1])]
        b = parts[0] if len(parts) == 1 else jnp.concatenate(parts, axis=1)
        buf[...] = _dot(k, qst[...]) + jnp.concatenate([b] * H, axis=1)

    def update(buf, j):
        _online_update(buf, p_sc, v_ref[0, 0, j], m_sc, acc_sc)

    _flash_pipeline(nkv, put_scores, update, sa_sc, sb_sc)
    _store_head_pairs(o_ref, _normalized(acc_sc[...]), H, tq)


def _dsa_attention(sl, bias, lay):
    q_t, k, v_t = sl["qT"], sl["k"], sl["vT"]
    B, _, T, _ = k.shape
    tq, tk = DSA_Q_TILE, KV_TILE
    nk = T // tk
    H = A_HEADS
    N = H * tq
    sel_tq = bias.shape[4]
    nsub = tq // sel_tq
    kern = functools.partial(_dsa_attn_kernel, tq=tq, tk=tk)
    return pl.pallas_call(
        kern,
        grid=(B, T // tq),
        in_specs=[
            pl.BlockSpec((1, H // 2, LANES, tq), lambda b, i: (b, lay["qA"] // (H // 2), 0, i)),
            pl.BlockSpec((1, 1, T, LANES), lambda b, i: (b, lay["kA"], 0, 0)),
            pl.BlockSpec((1, 1, nk, LANES, tk), lambda b, i: (b, lay["vA"], 0, 0, 0)),
            pl.BlockSpec((1, nsub, nk, tk, sel_tq), lambda b, i: (b, i, 0, 0, 0)),
        ],
        out_specs=pl.BlockSpec((1, tq, H * HEAD_DIM), lambda b, i: (b, i, 0)),
        out_shape=jax.ShapeDtypeStruct((B, T, H * HEAD_DIM), BF16),
        scratch_shapes=[
            pltpu.VMEM((LANES, N), BF16),
            pltpu.VMEM((1, N), F32),
            pltpu.VMEM((LANES, N), F32),
            pltpu.VMEM((tk, N), F32),
            pltpu.VMEM((tk, N), F32),
            pltpu.VMEM((tk, N), BF16),
        ],
        compiler_params=_params(("parallel", "arbitrary")),
        name="dsa_attention",
    )(q_t, k, v_t, bias)


def _compress_kernel(x_ref, pe_ref, w1_ref, w2_ref, o_ref, ot_ref, *, n_ch):
    x = x_ref[0, 0]
    xt = (x + pe_ref[0, 0]).astype(BF16)
    xb = (x + pe_ref[0, 1]).astype(BF16)
    for g in range(B_KV_GROUPS):
        a = _dot(xt, w1_ref[0, g, 0])
        b = _dot(xb, w1_ref[0, g, 1])
        h = jax.nn.gelu(a + pltpu.roll(b, n_ch - 1, 0))
        r = _dot(h.astype(BF16), w2_ref[0])
        o_ref[0, 0, g] = r.astype(BF16)
        ot_ref[0, 0, g] = r.T.astype(BF16)


def _compress(flat, pe, w1, w2):
    _, B, n_ch, W = flat.shape
    G = B_KV_GROUPS
    kern = functools.partial(_compress_kernel, n_ch=n_ch)
    return pl.pallas_call(
        kern,
        grid=(2, B),
        in_specs=[
            pl.BlockSpec((1, 1, n_ch, W), lambda s, b: (s, b, 0, 0)),
            pl.BlockSpec((1, 2, 1, W), lambda s, b: (s, 0, 0, 0)),
            pl.BlockSpec((1, G, 2, W, CMP_HIDDEN), lambda s, b: (s, 0, 0, 0, 0)),
            pl.BlockSpec((1, CMP_HIDDEN, LANES), lambda s, b: (s, 0, 0)),
        ],
        out_specs=[
            pl.BlockSpec((1, 1, G, n_ch, LANES), lambda s, b: (s, b, 0, 0, 0)),
            pl.BlockSpec((1, 1, G, LANES, n_ch), lambda s, b: (s, b, 0, 0, 0)),
        ],
        out_shape=[
            jax.ShapeDtypeStruct((2, B, G, n_ch, LANES), BF16),
            jax.ShapeDtypeStruct((2, B, G, LANES, n_ch), BF16),
        ],
        compiler_params=_params(("parallel", "parallel")),
        name="nsa_compress",
    )(flat, pe, w1, w2)


def _nsa_kernel(q_ref, kc_ref, vc_ref, ks_ref, vs_ref, kw_ref, vw_ref, misc_ref, ov_ref, ex_ref,
                cz_ref, wz_ref, o_ref, qst, m_sc, acc_sc, sa_sc, sb_sc, p_sc, mw_sc, accw_sc,
                sw_sc, pw_sc, *, tq, tk, T, n_s, n_sel, gate_row0):
    g = pl.program_id(1)
    qi = pl.program_id(2)
    t0 = qi * tq
    J = B_PER_GROUP
    N = J * tq
    for j in range(J):
        qst[0:LANES, j * tq:(j + 1) * tq] = _head_rows(q_ref[0, j // 2], j % 2)
    q = qst[0:LANES, :]

    def q_time(shape):
        return t0 + (_iota(shape, 1) % tq)

    wlen = min(WINDOW + tq, T)
    wstart = pl.multiple_of(jnp.maximum(t0 - WINDOW, 0), tq)
    kw = kw_ref[0, 0, pl.ds(wstart, wlen), :]
    n_wt = wlen // tq
    wt = wstart // VW_TILE
    vw = jnp.concatenate([vw_ref[0, 0, wt + i] for i in range(wlen // VW_TILE)], axis=1)
    wz = wz_ref[jnp.minimum(qi, n_wt - 1)].astype(F32)
    sw_sc[...] = _dot(kw, q) + jnp.concatenate([wz] * J, axis=1)
    _init_flash(mw_sc, accw_sc)
    _online_update(sw_sc, pw_sc, vw, mw_sc, accw_sc)

    kc = kc_ref[0, 0, 0]
    n_ch = kc.shape[0]
    s_c = _dot(kc, q)
    cmp_end = _iota((n_ch, N), 0) * CMP_STRIDE + (CMP_BLOCK - 1)
    s_c = jnp.where(cmp_end <= q_time((n_ch, N)), s_c, NEG_INF)
    p_c = _softmax_keys(s_c)
    o_c = _dot(vc_ref[0, 0, 0], p_c.astype(BF16))

    psum = p_c[:, 0:tq]
    for j in range(1, J):
        psum = psum + p_c[:, j * tq:(j + 1) * tq]
    p_hi = psum.astype(BF16)
    p_lo = (psum - p_hi.astype(F32)).astype(BF16)
    imp = _dot(ov_ref[...], p_hi) + _dot(ov_ref[...], p_lo)
    rows = -(-n_s // 8) * 8
    imp = imp[0:rows]
    blk = _iota((rows, tq), 0)
    t_q = t0 + _iota((rows, tq), 1)
    cur = t_q // SLC_BLOCK
    forced = (blk == 0) | (blk == cur) | (blk == cur - 1)
    sc = jnp.where(forced, jnp.inf, imp)
    sc = jnp.where(blk * SLC_BLOCK <= t_q, sc, NEG_INF)
    groups = [sc[8 * r:8 * r + 8] for r in range(rows // 8)]
    ranks = [jnp.zeros((8, tq), F32) for _ in groups]
    blk8 = _iota((8, tq), 0)
    for m in range(n_s):
        cm = sc[m:m + 1, :]
        for r, grp in enumerate(groups):
            gt = jnp.where(cm > grp, 1.0, 0.0)
            ge = jnp.where(cm >= grp, 1.0, 0.0)
            if 8 * r + 7 <= m:
                first = gt
            elif 8 * r > m:
                first = ge
            else:
                first = jnp.where(blk8 + 8 * r > m, ge, gt)
            ranks[r] = ranks[r] + first
    rank = jnp.concatenate(ranks, axis=0)
    drop = jnp.where(rank < float(n_sel), 0.0, 1.0)
    if rows < LANES:
        drop = jnp.concatenate([drop, jnp.ones((LANES - rows, tq), F32)], axis=0)
    drop = drop.astype(BF16)
    qst[LANES:2 * LANES, :] = jnp.concatenate([drop] * J, axis=1)

    _init_flash(m_sc, acc_sc)
    jd = t0 // tk
    off = (t0 - jd * tk) // tq
    n_off = tk // tq

    def put_scores(buf, j):
        k = ks_ref[0, 0, pl.ds(pl.multiple_of(j * tk, tk), tk), :]
        lhs = jnp.concatenate([k, ex_ref[j]], axis=1)
        cz = cz_ref[jnp.where(j == jd, off, n_off)].astype(F32)
        buf[...] = _dot(lhs, qst[...]) + jnp.concatenate([cz] * J, axis=1)

    def update(buf, j):
        _online_update(buf, p_sc, vs_ref[0, 0, j], m_sc, acc_sc)

    _flash_pipeline(jd + 1, put_scores, update, sa_sc, sb_sc)
    o_s = _normalized(acc_sc[...])
    o_w = _normalized(accw_sc[...])

    gates = jax.nn.sigmoid(misc_ref[0, 0])

    def gate_row(c):
        parts = []
        for j in range(J):
            r0 = gate_row0 + j * 3 + c
            r1 = gate_row0 + (J + j) * 3 + c
            parts.append(jnp.where(g == 0, gates[r0:r0 + 1, :], gates[r1:r1 + 1, :]))
        return jnp.concatenate(parts, axis=1)

    o = gate_row(0) * o_c[0:HEAD_DIM] + (gate_row(1) * o_s + gate_row(2) * o_w)
    _store_head_pairs(o_ref, o, J, tq)


def _nsa_attention(sl, cmp_k, cmp_vt, lay):
    q_t, k, v_t, misc_t = sl["qT"], sl["k"], sl["vT"], sl["fT"]
    B, _, T, _ = k.shape
    G, J = B_KV_GROUPS, B_PER_GROUP
    tq, tk = NSA_Q_TILE, KV_TILE
    nk = T // tk
    n_ch = T // CMP_STRIDE
    n_c = n_ch - CMP_BLOCK // CMP_STRIDE + 1
    n_s = T // SLC_BLOCK
    n_sel = min(SLC_TOPN, n_s)
    assert n_s <= LANES
    c0 = np.arange(n_ch) * CMP_STRIDE
    s0 = np.arange(LANES) * SLC_BLOCK
    ov = ((c0[None, :] < s0[:, None] + SLC_BLOCK) & (c0[None, :] + CMP_BLOCK > s0[:, None]))
    ov = ov & (np.arange(n_ch)[None, :] < n_c) & (np.arange(LANES)[:, None] < n_s)
    ov = jnp.asarray(ov.astype(np.float32), BF16)
    pos = np.arange(T).reshape(nk, tk, 1)
    ex = (pos // SLC_BLOCK == np.arange(LANES).reshape(1, 1, LANES))
    ex = jnp.asarray(ex.astype(np.float32) * -BIG, BF16)
    n_off = tk // tq
    kp = np.arange(tk).reshape(1, tk, 1)
    tl = np.arange(tq).reshape(1, 1, tq) + np.arange(n_off + 1).reshape(n_off + 1, 1, 1) * tq
    cz = np.where((kp <= tl) | (np.arange(n_off + 1).reshape(-1, 1, 1) == n_off), 0.0, -BIG)
    cz = jnp.asarray(cz.astype(np.float32), BF16)
    wlen = min(WINDOW + tq, T)
    n_wt = wlen // tq
    assert wlen % tq == 0 and T >= wlen
    kp = np.arange(wlen).reshape(1, wlen, 1)
    tl = np.arange(tq).reshape(1, 1, tq)
    early = kp <= tl + np.arange(n_wt).reshape(n_wt, 1, 1) * tq
    late = (kp > tl) & (kp <= tl + WINDOW)
    band = np.where(np.arange(n_wt).reshape(n_wt, 1, 1) == n_wt - 1, late, early)
    wz = jnp.asarray(np.where(band, 0.0, -BIG).astype(np.float32), BF16)
    kern = functools.partial(_nsa_kernel, tq=tq, tk=tk, T=T, n_s=n_s, n_sel=n_sel,
                             gate_row0=lay["gate_row0"])
    N = J * tq
    kslab = lambda off: pl.BlockSpec((1, 1, T, LANES), lambda b, g, i: (b, off + g, 0, 0))
    return pl.pallas_call(
        kern,
        grid=(B, G, T // tq),
        in_specs=[
            pl.BlockSpec((1, 2, LANES, tq), lambda b, g, i: (b, lay["qB"] // 2 + g, 0, i)),
            pl.BlockSpec((1, 1, 1, n_ch, LANES), lambda b, g, i: (0, b, g, 0, 0)),
            pl.BlockSpec((1, 1, 1, LANES, n_ch), lambda b, g, i: (1, b, g, 0, 0)),
            kslab(lay["ks"]),
            pl.BlockSpec((1, 1, nk, LANES, tk), lambda b, g, i: (b, lay["vs"] + g, 0, 0, 0)),
            kslab(lay["kw"]),
            pl.BlockSpec((1, 1, T // VW_TILE, LANES, VW_TILE),
                         lambda b, g, i: (b, lay["vw"] + g, 0, 0, 0)),
            pl.BlockSpec((1, 1, LANES, tq), lambda b, g, i: (b, lay["misc"], 0, i)),
            pl.BlockSpec((LANES, n_ch), lambda b, g, i: (0, 0)),
            pl.BlockSpec((nk, tk, LANES), lambda b, g, i: (0, 0, 0)),
            pl.BlockSpec((n_off + 1, tk, tq), lambda b, g, i: (0, 0, 0)),
            pl.BlockSpec((n_wt, wlen, tq), lambda b, g, i: (0, 0, 0)),
        ],
        out_specs=pl.BlockSpec((1, tq, J * HEAD_DIM), lambda b, g, i: (b, i, g)),
        out_shape=jax.ShapeDtypeStruct((B, T, B_HEADS * HEAD_DIM), BF16),
        scratch_shapes=[
            pltpu.VMEM((2 * LANES, N), BF16),
            pltpu.VMEM((1, N), F32),
            pltpu.VMEM((LANES, N), F32),
            pltpu.VMEM((tk, N), F32),
            pltpu.VMEM((tk, N), F32),
            pltpu.VMEM((tk, N), BF16),
            pltpu.VMEM((1, N), F32),
            pltpu.VMEM((LANES, N), F32),
            pltpu.VMEM((wlen, N), F32),
            pltpu.VMEM((wlen, N), BF16),
        ],
        compiler_params=_params(("parallel", "parallel", "arbitrary")),
        name="nsa_attention",
    )(q_t, cmp_k, cmp_vt, k, v_t, k, sl["vTw"], misc_t, ov, ex, cz, wz)


def _rms(x, g):
    ms = jnp.mean(x * x, axis=-1, keepdims=True)
    return (x * lax.rsqrt(ms + NORM_EPS)) * g


def _layer_tail_kernel(*refs, n_in, final_norm):
    h_ref = refs[0]
    o_refs = refs[1:1 + n_in]
    w_refs = refs[1 + n_in:1 + 2 * n_in]
    g_ref, wu_ref, wd_ref = refs[1 + 2 * n_in:4 + 2 * n_in]
    rest = refs[4 + 2 * n_in:]
    if final_norm:
        fg_ref, out_ref, xn_sc = rest
    else:
        out_ref, xn_sc = rest
    f = pl.program_id(1)

    @pl.when(f == 0)
    def _():
        mix = _dot(o_refs[0][...], w_refs[0][...])
        for i in range(1, n_in):
            mix = mix + _dot(o_refs[i][...], w_refs[i][...])
        x = h_ref[...] + mix
        xn_sc[...] = _rms(x, g_ref[...]).astype(BF16)
        out_ref[...] = x

    u = _dot(xn_sc[...], wu_ref[...])
    a = jnp.square(jnp.maximum(u, 0.0)).astype(BF16)
    out_ref[...] += _dot(a, wd_ref[...])

    if final_norm:
        @pl.when(f == pl.num_programs(1) - 1)
        def _():
            out_ref[...] = _rms(out_ref[...], fg_ref[...])


def _layer_tail(h2, outs, ws, g, w_up, w_down, final_g=None, tm=1024, tf=1024):
    N, D = h2.shape
    F = w_up.shape[1]
    tm = min(tm, N)
    n_in = len(outs)
    in_specs = [pl.BlockSpec((tm, D), lambda i, f: (i, 0))]
    in_specs += [pl.BlockSpec((tm, o.shape[1]), lambda i, f: (i, 0)) for o in outs]
    in_specs += [pl.BlockSpec(w.shape, lambda i, f: (0, 0)) for w in ws]
    in_specs += [
        pl.BlockSpec((1, D), lambda i, f: (0, 0)),
        pl.BlockSpec((D, tf), lambda i, f: (0, f)),
        pl.BlockSpec((tf, D), lambda i, f: (f, 0)),
    ]
    args = [h2, *outs, *ws, g.reshape(1, D), w_up, w_down]
    if final_g is not None:
        in_specs.append(pl.BlockSpec((1, D), lambda i, f: (0, 0)))
        args.append(final_g.reshape(1, D))
    kern = functools.partial(_layer_tail_kernel, n_in=n_in, final_norm=final_g is not None)
    return pl.pallas_call(
        kern,
        grid=(N // tm, F // tf),
        in_specs=in_specs,
        out_specs=pl.BlockSpec((tm, D), lambda i, f: (i, 0)),
        out_shape=jax.ShapeDtypeStruct((N, D), F32),
        scratch_shapes=[pltpu.VMEM((tm, D), BF16)],
        compiler_params=_params(("parallel", "arbitrary")),
        name="out_proj_mlp",
    )(*args)


def _even_layout():
    offs = {}
    o = 0
    for name, size in (("qa", 512), ("ka", 64), ("va", 64), ("qi", 256), ("ki", 64), ("wi", 4),
                       ("qb", 512), ("kvb", 768), ("gb", 24)):
        offs[name] = o
        o += size
    kvb = lambda which, g: offs["kvb"] + (which * B_KV_GROUPS + g) * HEAD_DIM
    cols, plan, lay = [], [], {}
    n = {k: 0 for k in _KINDS}

    def add(c, roped, kind, opt=None):
        cols.append(c)
        plan.append((roped, kind, n[kind], opt))
        n[kind] += 1
        return n[kind] - 1

    lay["qA"] = n["qT"]
    for p in range(4):
        add(_pair_cols(offs["qa"] + 2 * p * 64, offs["qa"] + (2 * p + 1) * 64), True, "qT", Q_SCALE)
    lay["qB"] = n["qT"]
    for p in range(4):
        add(_pair_cols(offs["qb"] + 2 * p * 64, offs["qb"] + (2 * p + 1) * 64), True, "qT", Q_SCALE)
    lay["qi"] = n["qT"]
    for p in range(2):
        add(_pair_cols(offs["qi"] + 2 * p * 64, offs["qi"] + (2 * p + 1) * 64), True, "qT")
    lay["kA"] = add(_pair_cols(offs["ka"], offs["ka"]), True, "k")
    lay["ki"] = add(_pair_cols(offs["ki"], offs["ki"]), True, "k")
    lay["ks"] = n["k"]
    for g in range(2):
        add(_pair_cols(kvb(2, g), kvb(2, g)), True, "k")
    lay["kw"] = n["k"]
    for g in range(2):
        add(_pair_cols(kvb(4, g), kvb(4, g)), True, "k")
    lay["kc"] = add(_pair_cols(kvb(0, 0), kvb(0, 1)), True, "f")
    lay["vc"] = add(_pair_cols(kvb(1, 0), kvb(1, 1)), False, "f")
    misc = np.full(LANES, -1)
    misc[0:IDX_HEADS] = offs["wi"] + np.arange(IDX_HEADS)
    lay["gate_row0"] = 8
    misc[8:8 + 24] = offs["gb"] + np.arange(24)
    lay["misc"] = add(misc, False, "fT")
    lay["vw"] = n["vTw"]
    for g in range(2):
        add(_head_cols(kvb(5, g)), False, "vTw")
    lay["vA"] = add(_head_cols(offs["va"]), False, "vT", "ones_row_64")
    lay["vs"] = n["vT"]
    for g in range(2):
        add(_head_cols(kvb(3, g)), False, "vT", "ones_row_64")
    return np.concatenate(cols), plan, lay


def _odd_layout():
    cols, plan = [], []
    for h in range(C_HEADS):
        cols.append(_pair_cols(h * 128, h * 128 + 64))
        plan.append((True, "qT", h, Q_SCALE))
    for h in range(C_HEADS):
        cols.append(_pair_cols(1024 + h * 128, 1024 + h * 128 + 64))
        plan.append((True, "k", h, None))
    for h in range(C_HEADS):
        cols.append(2048 + h * 128 + np.arange(LANES))
        plan.append((False, "vT", h, "ones_rows_below"))
    return np.concatenate(cols), plan


def _compress_weights(pe, w1, w2):
    d = _PAIR_D
    which = _PAIR_WHICH
    pe_l = pe[:, :, d]
    pe_l = pe_l.reshape(2, 2, 1, CMP_STRIDE * LANES)
    w1r = w1.reshape(2, CMP_BLOCK, HEAD_DIM, CMP_HIDDEN)[:, :, d, :]
    per_g = []
    for g in range(B_KV_GROUPS):
        keep = jnp.asarray(which == g)[None, None, :, None]
        per_g.append(jnp.where(keep, w1r, 0.0))
    w1g = jnp.stack(per_g, axis=1)
    w1g = w1g.reshape(2, B_KV_GROUPS, 2, CMP_STRIDE * LANES, CMP_HIDDEN).astype(BF16)
    w2k = w2[0][:, d]
    w2v = w2[1][:, np.arange(LANES) % HEAD_DIM]
    w2l = jnp.stack([w2k, w2v], axis=0).astype(BF16)
    return pe_l, w1g, w2l


def _even_mixer(h, norm_g, w_in, cmp_pe, cmp_w1, cmp_w2, w_out, cos_slab, sin_slab):
    B, T, D = h.shape
    cols, plan, lay = _even_layout()
    w = _gather_cols(w_in, cols).astype(BF16)
    sl = _project(h, norm_g, w, cos_slab, sin_slab, plan)

    bias = _dsa_select(sl, lay)
    o_a = _dsa_attention(sl, bias, lay)

    n_ch = T // CMP_STRIDE
    flat = sl["f"][:, lay["kc"]:lay["vc"] + 1].reshape(B, 2, n_ch, CMP_STRIDE * LANES)
    flat = jnp.swapaxes(flat, 0, 1)
    pe_l, w1g, w2l = _compress_weights(cmp_pe, cmp_w1, cmp_w2)
    cmp_k, cmp_vt = _compress(flat, pe_l, w1g, w2l)
    o_b = _nsa_attention(sl, cmp_k, cmp_vt, lay)

    na = A_HEADS * HEAD_DIM
    wo = w_out.astype(BF16)
    return [o_a.reshape(B * T, -1), o_b.reshape(B * T, -1)], [wo[:na], wo[na:]]


def _odd_mixer(h, norm_g, w_in, lam, subln_g, w_out, cos_slab, sin_slab, lambda_init):
    B, T, D = h.shape
    cols, plan = _odd_layout()
    w = _gather_cols(w_in, cols).astype(BF16)
    sl = _project(h, norm_g, w, cos_slab, sin_slab, plan)
    o = _diff_attention(sl, lam, subln_g, lambda_init)
    return [o.reshape(B * T, -1)], [w_out.astype(BF16)]


def kernel(x, mix_norm_g, mlp_norm_g, even_w_in, even_cmp_pe, even_cmp_w1, even_cmp_w2, even_w_out, odd_w_in, odd_lambda, odd_subln_g, odd_w_out, mlp_w_up, mlp_w_down, final_norm_g):
    B, T, D = x.shape
    depth = mix_norm_g.shape[0]
    assert depth >= 1
    cos_slab, sin_slab = _rope_slabs(T)
    h = x
    for layer in range(depth):
        if layer % 2 == 0:
            e = layer // 2
            outs, ws = _even_mixer(h, mix_norm_g[layer], even_w_in[e], even_cmp_pe[e],
                                   even_cmp_w1[e], even_cmp_w2[e], even_w_out[e], cos_slab, sin_slab)
        else:
            o = layer // 2
            lambda_init = 0.8 - 0.6 * math.exp(-0.3 * layer)
            outs, ws = _odd_mixer(h, mix_norm_g[layer], odd_w_in[o], odd_lambda[o], odd_subln_g[o],
                                  odd_w_out[o], cos_slab, sin_slab, lambda_init)
        h2 = _layer_tail(h.reshape(B * T, D), outs, ws, mlp_norm_g[layer],
                         mlp_w_up[layer].astype(BF16), mlp_w_down[layer].astype(BF16),
                         final_g=final_norm_g if layer == depth - 1 else None)
        h = h2.reshape(B, T, D)
    return h
```

```python
import functools
import math

import numpy as np
import jax
import jax.numpy as jnp
from jax import lax
from jax.experimental import pallas as pl
from jax.experimental.pallas import tpu as pltpu

HEAD_DIM = 64
HALF = HEAD_DIM // 2
LANES = 128
ROPE_THETA = 10000.0
NORM_EPS = 1e-6
SCALE = HEAD_DIM ** -0.5

A_HEADS = 8
IDX_HEADS = 4
DSA_TOPK = 256
B_HEADS = 8
B_KV_GROUPS = 2
B_PER_GROUP = B_HEADS // B_KV_GROUPS
CMP_BLOCK = 32
CMP_STRIDE = 16
CMP_HIDDEN = 256
SLC_BLOCK = 64
SLC_TOPN = 16
WINDOW = 512
C_HEADS = 8

KV_TILE = 512
SEL_Q_TILE = 128
DSA_Q_TILE = 256
NSA_Q_TILE = 256
VW_TILE = 128

LOG2E = math.log2(math.e)
Q_SCALE = SCALE * LOG2E
ONES_ROWS = 16
BIG = 2.0 ** 100

NEG_INF = float("-inf")
M_FLOOR = -1e30
INT_MIN = -(2 ** 31)

VMEM_LIMIT = 56 * 1024 * 1024

BF16 = jnp.bfloat16
F32 = jnp.float32


def _dot(a, b):
    return jnp.dot(a, b, preferred_element_type=F32)


def _dot_tn(a, b):
    return lax.dot_general(a, b, (((0,), (0,)), ((), ())), preferred_element_type=F32)


def _params(sem):
    return pltpu.CompilerParams(dimension_semantics=sem, vmem_limit_bytes=VMEM_LIMIT)


def _iota(shape, axis):
    return lax.broadcasted_iota(jnp.int32, shape, axis)


def _pair_cols(base_a, base_b):
    lane = np.arange(LANES)
    half = lane // 64
    which = (lane % 64) // HALF
    i = lane % HALF
    base = np.where(which == 0, base_a, base_b)
    return base + half * HALF + i


def _dup_cols(base):
    return base + np.arange(LANES) % HEAD_DIM


def _head_cols(base):
    lane = np.arange(LANES)
    return np.where(lane < HEAD_DIM, base + lane, -1)


_PAIR_D = _pair_cols(0, 0)
_PAIR_WHICH = (np.arange(LANES) % 64) // HALF


def _gather_cols(w, cols):
    cols = np.asarray(cols)
    safe = np.where(cols >= 0, cols, 0)
    g = jnp.take(w, jnp.asarray(safe, dtype=jnp.int32), axis=1)
    return jnp.where(jnp.asarray(cols >= 0)[None, :], g, jnp.zeros_like(g))


def _rope_slabs(T):
    inv = 1.0 / (ROPE_THETA ** (jnp.arange(0, HEAD_DIM, 2, dtype=F32) / HEAD_DIM))
    ang = jnp.arange(T, dtype=F32)[:, None] * inv[None, :]
    cos, sin = jnp.cos(ang), jnp.sin(ang)
    cos_slab = jnp.tile(cos, (1, 4))
    sin_slab = jnp.concatenate([-sin, -sin, sin, sin], axis=1)
    return cos_slab, sin_slab


_KINDS = ("qT", "k", "vT", "vTw", "f", "fT")


def _proj_kernel(x_ref, g_ref, w_ref, cos_ref, sin_ref, *out_refs, plan, kinds, chunk):
    outs = dict(zip(kinds, out_refs))
    x = x_ref[0]
    ms = jnp.mean(x * x, axis=-1, keepdims=True)
    xn = ((x * lax.rsqrt(ms + NORM_EPS)) * g_ref[...]).astype(BF16)
    cos = cos_ref[...]
    sin = sin_ref[...]
    n = len(plan)
    for c0 in range(0, n, chunk):
        c1 = min(c0 + chunk, n)
        r = _dot(xn, w_ref[:, c0 * LANES:c1 * LANES])
        for s in range(c0, c1):
            roped, kind, idx, opt = plan[s]
            y = r[:, (s - c0) * LANES:(s - c0 + 1) * LANES]
            if roped:
                y = y * cos + pltpu.roll(y, 64, 1) * sin
            if kind == "qT":
                if opt is not None:
                    y = y * opt
                outs[kind][0, idx] = y.T.astype(BF16)
            elif kind == "k":
                outs[kind][0, idx] = y.astype(BF16)
            elif kind == "vT":
                if opt == "ones_row_64":
                    y = jnp.where(_iota(y.shape, 1) == HEAD_DIM, 1.0, y)
                    outs[kind][0, idx, 0] = y.T.astype(BF16)
                else:
                    outs[kind][0, idx, 0, 0:LANES, :] = y.T.astype(BF16)
                    outs[kind][0, idx, 0, LANES:, :] = jnp.ones((ONES_ROWS, y.shape[0]), BF16)
            elif kind == "vTw":
                y_t = jnp.where(_iota(y.shape, 1) == HEAD_DIM, 1.0, y).T.astype(BF16)
                for sub in range(y.shape[0] // VW_TILE):
                    outs[kind][0, idx, sub] = y_t[:, sub * VW_TILE:(sub + 1) * VW_TILE]
            elif kind == "f":
                outs[kind][0, idx] = y
            else:
                outs[kind][0, idx] = y.T


def _project(x, g, w, cos_slab, sin_slab, plan, chunk=4):
    B, T, D = x.shape
    tm = KV_TILE
    assert T % tm == 0
    count = {k: sum(1 for p in plan if p[1] == k) for k in _KINDS}
    kinds = tuple(k for k in _KINDS if count[k])
    out_specs, out_shape = [], []
    for k in kinds:
        n = count[k]
        if k in ("qT", "fT"):
            out_specs.append(pl.BlockSpec((1, n, LANES, tm), lambda b, i: (b, 0, 0, i)))
            out_shape.append(jax.ShapeDtypeStruct((B, n, LANES, T), BF16 if k == "qT" else F32))
        elif k in ("k", "f"):
            out_specs.append(pl.BlockSpec((1, n, tm, LANES), lambda b, i: (b, 0, i, 0)))
            out_shape.append(jax.ShapeDtypeStruct((B, n, T, LANES), BF16 if k == "k" else F32))
        elif k == "vTw":
            sub = tm // VW_TILE
            out_specs.append(pl.BlockSpec((1, n, sub, LANES, VW_TILE), lambda b, i: (b, 0, i, 0, 0)))
            out_shape.append(jax.ShapeDtypeStruct((B, n, T // VW_TILE, LANES, VW_TILE), BF16))
        else:
            wide = any(p[1] == "vT" and p[3] != "ones_row_64" for p in plan)
            rows = LANES + ONES_ROWS if wide else LANES
            out_specs.append(pl.BlockSpec((1, n, 1, rows, tm), lambda b, i: (b, 0, i, 0, 0)))
            out_shape.append(jax.ShapeDtypeStruct((B, n, T // tm, rows, tm), BF16))
    kern = functools.partial(_proj_kernel, plan=tuple(plan), kinds=kinds, chunk=chunk)
    outs = pl.pallas_call(
        kern,
        grid=(B, T // tm),
        in_specs=[
            pl.BlockSpec((1, tm, D), lambda b, i: (b, i, 0)),
            pl.BlockSpec((1, D), lambda b, i: (0, 0)),
            pl.BlockSpec((D, len(plan) * LANES), lambda b, i: (0, 0)),
            pl.BlockSpec((tm, LANES), lambda b, i: (i, 0)),
            pl.BlockSpec((tm, LANES), lambda b, i: (i, 0)),
        ],
        out_specs=out_specs,
        out_shape=out_shape,
        compiler_params=_params(("parallel", "parallel")),
        name="norm_proj_rope",
    )(x, g.reshape(1, D), w, cos_slab, sin_slab)
    return dict(zip(kinds, outs))


def _head_rows(slab_t, which):
    row = _iota(slab_t.shape, 0)
    keep = ((row % 64) // HALF) == which
    return jnp.where(keep, slab_t, jnp.zeros_like(slab_t))


def _tree(op, xs):
    while len(xs) > 1:
        xs = [op(xs[i], xs[i + 1]) if i + 1 < len(xs) else xs[i] for i in range(0, len(xs), 2)]
    return xs[0]


def _fold_rows_max(x, ways=4):
    rows, n = x.shape
    per = rows // ways
    parts = [jnp.max(x[i * per:(i + 1) * per].reshape(per // 8, 8, n), axis=0) for i in range(ways)]
    return _tree(jnp.maximum, parts)


ROW_BLOCK = 64


def _online_update(s_sc, p_sc, v_aug, m_ref, acc_ref):
    tk, n = s_sc.shape
    m_prev = m_ref[...]
    m_tile = jnp.max(_fold_rows_max(s_sc[...]), axis=0, keepdims=True)
    m_new = jnp.maximum(m_prev, m_tile)
    alpha = jnp.exp2(m_prev - m_new)
    for r in range(tk // ROW_BLOCK):
        rows = slice(r * ROW_BLOCK, (r + 1) * ROW_BLOCK)
        p_sc[rows, :] = jnp.exp2(s_sc[rows, :] - m_new).astype(BF16)
    acc_ref[...] = alpha * acc_ref[...] + _dot(v_aug, p_sc[...])
    m_ref[...] = m_new


def _flash_pipeline(n, put_scores, update, buf_a, buf_b, put_last=None):
    if put_last is None:
        put_scores(buf_a, 0)
    else:
        @pl.when(n > 0)
        def _():
            put_scores(buf_a, 0)

    def pair(p, carry):
        j = 2 * p
        put_scores(buf_b, j + 1)
        update(buf_a, j)

        @pl.when(j + 2 < n)
        def _():
            put_scores(buf_a, j + 2)
            update(buf_b, j + 1)

        return carry

    lax.fori_loop(0, n // 2, pair, 0)
    in_b = jnp.logical_and(n > 0, n % 2 == 0)
    in_a = n % 2 == 1

    def finish(cur, other):
        if put_last is not None:
            put_last(other, n)
        update(cur, n - 1)
        if put_last is not None:
            update(other, n)

    pl.when(in_b)(lambda: finish(buf_b, buf_a))
    pl.when(in_a)(lambda: finish(buf_a, buf_b))
    if put_last is not None:
        @pl.when(n == 0)
        def _():
            put_last(buf_a, 0)
            update(buf_a, 0)


def _softmax_keys(s):
    m = jnp.max(s, axis=0, keepdims=True)
    m = jnp.where(m > NEG_INF, m, 0.0)
    e = jnp.exp2(s - m)
    return e / jnp.maximum(jnp.sum(e, axis=0, keepdims=True), 1e-30)


def _init_flash(m_ref, acc_ref):
    m_ref[...] = jnp.full(m_ref.shape, M_FLOOR, F32)
    acc_ref[...] = jnp.zeros(acc_ref.shape, F32)


def _diff_attn_kernel(lam_ref, q_ref, k_ref, v_ref, g_ref, o_ref, m_sc, acc_sc, sa_sc, sb_sc,
                      p_sc, *, tq, lambda_init):
    qi = pl.program_id(2)
    q_t = q_ref[0, 0]
    qs = [_head_rows(q_t, c) for c in range(2)]
    for c in range(2):
        _init_flash(m_sc.at[c], acc_sc.at[c])

    def put_scores(buf, j, masked=False):
        k = k_ref[0, 0, pl.ds(pl.multiple_of(j * tq, tq), tq), :]
        for c in range(2):
            s = _dot(k, qs[c])
            if masked:
                s = jnp.where(_iota(s.shape, 0) <= _iota(s.shape, 1), s, NEG_INF)
            buf[c] = s

    def put_diagonal(buf, j):
        put_scores(buf, j, masked=True)

    def update(buf, j):
        v_aug = v_ref[0, 0, j]
        for c in range(2):
            _online_update(buf.at[c], p_sc.at[c], v_aug, m_sc.at[c], acc_sc.at[c])

    _flash_pipeline(qi, put_scores, update, sa_sc, sb_sc, put_last=put_diagonal)

    lam = lam_ref[...]
    s01 = jnp.sum(lam[0:1] * lam[1:2], axis=-1, keepdims=True)
    s23 = jnp.sum(lam[2:3] * lam[3:4], axis=-1, keepdims=True)
    lam_val = jnp.exp(s01) - jnp.exp(s23) + lambda_init
    o0 = acc_sc[0, 0:LANES] / jnp.maximum(acc_sc[0, LANES:LANES + 1], 1e-30)
    o1 = acc_sc[1, 0:LANES] / jnp.maximum(acc_sc[1, LANES:LANES + 1], 1e-30)
    o = o0 - lam_val * o1
    y = o * lax.rsqrt(jnp.mean(o * o, axis=0, keepdims=True) + NORM_EPS)
    y = (y * g_ref[...]) * (1.0 - lambda_init)
    o_ref[0] = y.T.astype(o_ref.dtype)


def _diff_attention(sl, lam, subln_g, lambda_init):
    q_t, k, v_t = sl["qT"], sl["k"], sl["vT"]
    B, H, T, _ = k.shape
    tq = KV_TILE
    nk = T // tq
    kern = functools.partial(_diff_attn_kernel, tq=tq, lambda_init=lambda_init)
    return pl.pallas_call(
        kern,
        grid=(B, H, T // tq),
        in_specs=[
            pl.BlockSpec((4, HEAD_DIM), lambda b, h, i: (0, 0)),
            pl.BlockSpec((1, 1, LANES, tq), lambda b, h, i: (b, h, 0, i)),
            pl.BlockSpec((1, 1, T, LANES), lambda b, h, i: (b, h, 0, 0)),
            pl.BlockSpec((1, 1, nk, LANES + ONES_ROWS, tq), lambda b, h, i: (b, h, 0, 0, 0)),
            pl.BlockSpec((LANES, 1), lambda b, h, i: (0, 0)),
        ],
        out_specs=pl.BlockSpec((1, tq, LANES), lambda b, h, i: (b, i, h)),
        out_shape=jax.ShapeDtypeStruct((B, T, H * LANES), BF16),
        scratch_shapes=[
            pltpu.VMEM((2, 1, tq), F32),
            pltpu.VMEM((2, LANES + ONES_ROWS, tq), F32),
            pltpu.VMEM((2, tq, tq), F32),
            pltpu.VMEM((2, tq, tq), F32),
            pltpu.VMEM((2, tq, tq), BF16),
        ],
        compiler_params=_params(("parallel", "parallel", "arbitrary")),
        name="diff_attention",
    )(lam, q_t, k, v_t, subln_g.reshape(LANES, 1))


def _dsa_select_kernel(qi_ref, ki_ref, misc_ref, tril_ref, bias_ref, key_sc, *, tq, ck, nk, topk):
    t0 = pl.program_id(1) * tq
    nvalid = (t0 + tq - 1) // ck + 1
    w = misc_ref[0, 0]
    qh = jnp.concatenate([_head_rows(qi_ref[0, h // 2], h % 2) for h in range(IDX_HEADS)], axis=1)
    t_q = t0 + _iota((ck, tq), 1)

    def causal(c):
        return (c * ck + _iota((ck, tq), 0)) <= t_q

    def fill(c):
        kk = ki_ref[0, 0, c * ck:(c + 1) * ck, :]
        r = _dot(kk, qh)
        score = jnp.zeros((ck, tq), F32)
        for h in range(IDX_HEADS):
            score = score + w[h:h + 1, :] * jnp.maximum(r[:, h * tq:(h + 1) * tq], 0.0)
        bits = pltpu.bitcast(score, jnp.int32)
        key = bits ^ ((bits >> 31) & jnp.int32(0x7FFFFFFF))
        key = jnp.where(score == 0.0, 0, key)
        key_sc[c] = jnp.where(causal(c), key, INT_MIN)

    kf = float(topk)

    def select_threshold(n):
        for c in range(n):
            fill(c)

        def count(pred):
            acc = jnp.zeros((ck // 8, tq), F32)
            for c in range(n):
                ind = pred(key_sc[c], c)
                acc = acc + jnp.sum(ind.reshape(8, ck // 8, tq), axis=0)
            return jnp.sum(acc, axis=0, keepdims=True)

        zero = jnp.zeros((1, tq), jnp.int32)
        ans = jnp.where(count(lambda kc, c: jnp.where(kc >= zero, 1.0, 0.0)) >= kf, 0, INT_MIN)

        def bit_step(i, ans):
            cand = ans | (jnp.int32(1) << (30 - i))
            cnt = count(lambda kc, c: jnp.where(kc >= cand, 1.0, 0.0))
            return jnp.where(cnt >= kf, cand, ans)

        ans = lax.fori_loop(0, 31, bit_step, ans)
        cnt_gt = count(lambda kc, c: jnp.where(kc > ans, 1.0, 0.0))
        cnt_eq = count(lambda kc, c: jnp.where(kc == ans, jnp.where(causal(c), 1.0, 0.0), 0.0))
        return ans, cnt_gt, cnt_eq

    ans, cnt_gt, cnt_eq = lax.switch(
        nvalid - 1, [functools.partial(select_threshold, n) for n in range(1, nk + 1)])
    need = kf - cnt_gt
    has_tie = jnp.max(cnt_eq - need) > 0.0

    @pl.when(jnp.logical_not(has_tie))
    def _():
        def emit(c, carry):
            sel = jnp.where(key_sc[c] >= ans, jnp.where(causal(c), 0.0, NEG_INF), NEG_INF)
            bias_ref[0, 0, c] = sel.astype(BF16)
            return carry
        lax.fori_loop(0, nvalid, emit, 0)

    @pl.when(has_tie)
    def _():
        def emit(c, carry):
            kc = key_sc[c]
            eq = jnp.where(kc == ans, jnp.where(causal(c), 1.0, 0.0), 0.0)
            pre = _dot(tril_ref[...], eq.astype(BF16)) + carry
            take = jnp.where(kc > ans, 1.0, jnp.where(pre < need, eq, 0.0))
            sel = jnp.where(take > 0.5, jnp.where(causal(c), 0.0, NEG_INF), NEG_INF)
            bias_ref[0, 0, c] = sel.astype(BF16)
            return carry + jnp.sum(eq, axis=0, keepdims=True)
        lax.fori_loop(0, nvalid, emit, jnp.zeros((1, tq), F32))

    def blank(c, carry):
        bias_ref[0, 0, c] = jnp.full((ck, tq), NEG_INF, BF16)
        return carry

    lax.fori_loop(nvalid, nk, blank, 0)


def _dsa_select(sl, lay):
    q_t, k, misc_t = sl["qT"], sl["k"], sl["fT"]
    B, _, T, _ = k.shape
    tq, ck = SEL_Q_TILE, KV_TILE
    nk = T // ck
    topk = min(DSA_TOPK, T // 4)
    tril = jnp.asarray(np.tril(np.ones((ck, ck), np.float32), -1), BF16)
    kern = functools.partial(_dsa_select_kernel, tq=tq, ck=ck, nk=nk, topk=topk)
    return pl.pallas_call(
        kern,
        grid=(B, T // tq),
        in_specs=[
            pl.BlockSpec((1, 2, LANES, tq), lambda b, i: (b, lay["qi"] // 2, 0, i)),
            pl.BlockSpec((1, 1, T, LANES), lambda b, i: (b, lay["ki"], 0, 0)),
            pl.BlockSpec((1, 1, LANES, tq), lambda b, i: (b, lay["misc"], 0, i)),
            pl.BlockSpec((ck, ck), lambda b, i: (0, 0)),
        ],
        out_specs=pl.BlockSpec((1, 1, nk, ck, tq), lambda b, i: (b, i, 0, 0, 0)),
        out_shape=jax.ShapeDtypeStruct((B, T // tq, nk, ck, tq), BF16),
        scratch_shapes=[pltpu.VMEM((nk, ck, tq), jnp.int32)],
        compiler_params=_params(("parallel", "parallel")),
        name="dsa_select",
    )(q_t, k, misc_t, tril)


def _store_head_pairs(o_ref, o, n_heads, tq):
    for p in range(n_heads // 2):
        even = o[0:HEAD_DIM, (2 * p) * tq:(2 * p + 1) * tq]
        odd = o[0:HEAD_DIM, (2 * p + 1) * tq:(2 * p + 2) * tq]
        pair = jnp.concatenate([even, odd], axis=0)
        o_ref[0, :, p * LANES:(p + 1) * LANES] = pair.T.astype(o_ref.dtype)


def _normalized(acc):
    return acc[0:HEAD_DIM] / jnp.maximum(acc[HEAD_DIM:HEAD_DIM + 1], 1e-30)


def _dsa_attn_kernel(q_ref, k_ref, v_ref, bias_ref, o_ref, qst, m_sc, acc_sc, sa_sc, sb_sc,
                     p_sc, *, tq, tk):
    qi = pl.program_id(1)
    H = A_HEADS
    for h in range(H):
        qst[:, h * tq:(h + 1) * tq] = _head_rows(q_ref[0, h // 2], h % 2)
    _init_flash(m_sc, acc_sc)
    nkv = (qi * tq) // tk + 1

    def put_scores(buf, j):
        k = k_ref[0, 0, pl.ds(pl.multiple_of(j * tk, tk), tk), :]
        parts = [bias_ref[0, s, j].astype(F32) for s in range(bias_ref.shape[1])]
        b = parts[0] if len(parts) == 1 else jnp.concatenate(parts, axis=1)
        buf[...] = _dot(k, qst[...]) + jnp.concatenate([b] * H, axis=1)

    def update(buf, j):
        _online_update(buf, p_sc, v_ref[0, 0, j], m_sc, acc_sc)

    _flash_pipeline(nkv, put_scores, update, sa_sc, sb_sc)
    _store_head_pairs(o_ref, _normalized(acc_sc[...]), H, tq)


def _dsa_attention(sl, bias, lay):
    q_t, k, v_t = sl["qT"], sl["k"], sl["vT"]
    B, _, T, _ = k.shape
    tq, tk = DSA_Q_TILE, KV_TILE
    nk = T // tk
    H = A_HEADS
    N = H * tq
    sel_tq = bias.shape[4]
    nsub = tq // sel_tq
    kern = functools.partial(_dsa_attn_kernel, tq=tq, tk=tk)
    return pl.pallas_call(
        kern,
        grid=(B, T // tq),
        in_specs=[
            pl.BlockSpec((1, H // 2, LANES, tq), lambda b, i: (b, lay["qA"] // (H // 2), 0, i)),
            pl.BlockSpec((1, 1, T, LANES), lambda b, i: (b, lay["kA"], 0, 0)),
            pl.BlockSpec((1, 1, nk, LANES, tk), lambda b, i: (b, lay["vA"], 0, 0, 0)),
            pl.BlockSpec((1, nsub, nk, tk, sel_tq), lambda b, i: (b, i, 0, 0, 0)),
        ],
        out_specs=pl.BlockSpec((1, tq, H * HEAD_DIM), lambda b, i: (b, i, 0)),
        out_shape=jax.ShapeDtypeStruct((B, T, H * HEAD_DIM), BF16),
        scratch_shapes=[
            pltpu.VMEM((LANES, N), BF16),
            pltpu.VMEM((1, N), F32),
            pltpu.VMEM((LANES, N), F32),
            pltpu.VMEM((tk, N), F32),
            pltpu.VMEM((tk, N), F32),
            pltpu.VMEM((tk, N), BF16),
        ],
        compiler_params=_params(("parallel", "arbitrary")),
        name="dsa_attention",
    )(q_t, k, v_t, bias)


def _compress_kernel(x_ref, pe_ref, w1_ref, w2_ref, o_ref, ot_ref, *, n_ch):
    x = x_ref[0, 0]
    xt = (x + pe_ref[0, 0]).astype(BF16)
    xb = (x + pe_ref[0, 1]).astype(BF16)
    for g in range(B_KV_GROUPS):
        a = _dot(xt, w1_ref[0, g, 0])
        b = _dot(xb, w1_ref[0, g, 1])
        h = jax.nn.gelu(a + pltpu.roll(b, n_ch - 1, 0))
        r = _dot(h.astype(BF16), w2_ref[0])
        o_ref[0, 0, g] = r.astype(BF16)
        ot_ref[0, 0, g] = r.T.astype(BF16)


def _compress(flat, pe, w1, w2):
    B, _, n_ch, W = flat.shape
    G = B_KV_GROUPS
    kern = functools.partial(_compress_kernel, n_ch=n_ch)
    return pl.pallas_call(
        kern,
        grid=(2, B),
        in_specs=[
            pl.BlockSpec((1, 1, n_ch, W), lambda s, b: (b, s, 0, 0)),
            pl.BlockSpec((1, 2, 1, W), lambda s, b: (s, 0, 0, 0)),
            pl.BlockSpec((1, G, 2, W, CMP_HIDDEN), lambda s, b: (s, 0, 0, 0, 0)),
            pl.BlockSpec((1, CMP_HIDDEN, LANES), lambda s, b: (s, 0, 0)),
        ],
        out_specs=[
            pl.BlockSpec((1, 1, G, n_ch, LANES), lambda s, b: (s, b, 0, 0, 0)),
            pl.BlockSpec((1, 1, G, LANES, n_ch), lambda s, b: (s, b, 0, 0, 0)),
        ],
        out_shape=[
            jax.ShapeDtypeStruct((2, B, G, n_ch, LANES), BF16),
            jax.ShapeDtypeStruct((2, B, G, LANES, n_ch), BF16),
        ],
        compiler_params=_params(("parallel", "parallel")),
        name="nsa_compress",
    )(flat, pe, w1, w2)


def _nsa_kernel(q_ref, kc_ref, vc_ref, ks_ref, vs_ref, kw_ref, vw_ref, misc_ref, ov_ref, ex_ref,
                cz_ref, wz_ref, o_ref, qst, m_sc, acc_sc, sa_sc, sb_sc, p_sc, mw_sc, accw_sc,
                sw_sc, pw_sc, *, tq, tk, T, n_s, n_sel, gate_row0):
    g = pl.program_id(1)
    qi = pl.program_id(2)
    t0 = qi * tq
    J = B_PER_GROUP
    N = J * tq
    for j in range(J):
        qst[0:LANES, j * tq:(j + 1) * tq] = _head_rows(q_ref[0, j // 2], j % 2)
    q = qst[0:LANES, :]

    def q_time(shape):
        return t0 + (_iota(shape, 1) % tq)

    wlen = min(WINDOW + tq, T)
    wstart = pl.multiple_of(jnp.maximum(t0 - WINDOW, 0), tq)
    kw = kw_ref[0, 0, pl.ds(wstart, wlen), :]
    n_wt = wlen // tq
    wt = wstart // VW_TILE
    vw = jnp.concatenate([vw_ref[0, 0, wt + i] for i in range(wlen // VW_TILE)], axis=1)
    wz = wz_ref[jnp.minimum(qi, n_wt - 1)].astype(F32)
    sw_sc[...] = _dot(kw, q) + jnp.concatenate([wz] * J, axis=1)
    _init_flash(mw_sc, accw_sc)
    _online_update(sw_sc, pw_sc, vw, mw_sc, accw_sc)

    kc = kc_ref[0, 0, 0]
    n_ch = kc.shape[0]
    s_c = _dot(kc, q)
    cmp_end = _iota((n_ch, N), 0) * CMP_STRIDE + (CMP_BLOCK - 1)
    s_c = jnp.where(cmp_end <= q_time((n_ch, N)), s_c, NEG_INF)
    p_c = _softmax_keys(s_c)
    o_c = _dot(vc_ref[0, 0, 0], p_c.astype(BF16))

    psum = p_c[:, 0:tq]
    for j in range(1, J):
        psum = psum + p_c[:, j * tq:(j + 1) * tq]
    p_hi = psum.astype(BF16)
    p_lo = (psum - p_hi.astype(F32)).astype(BF16)
    imp = _dot(ov_ref[...], p_hi) + _dot(ov_ref[...], p_lo)
    rows = -(-n_s // 8) * 8
    imp = imp[0:rows]
    blk = _iota((rows, tq), 0)
    t_q = t0 + _iota((rows, tq), 1)
    cur = t_q // SLC_BLOCK
    forced = (blk == 0) | (blk == cur) | (blk == cur - 1)
    sc = jnp.where(forced, jnp.inf, imp)
    sc = jnp.where(blk * SLC_BLOCK <= t_q, sc, NEG_INF)
    groups = [sc[8 * r:8 * r + 8] for r in range(rows // 8)]
    ranks = [jnp.zeros((8, tq), F32) for _ in groups]
    blk8 = _iota((8, tq), 0)
    for m in range(n_s):
        cm = sc[m:m + 1, :]
        for r, grp in enumerate(groups):
            gt = jnp.where(cm > grp, 1.0, 0.0)
            ge = jnp.where(cm >= grp, 1.0, 0.0)
            if 8 * r + 7 <= m:
                first = gt
            elif 8 * r > m:
                first = ge
            else:
                first = jnp.where(blk8 + 8 * r > m, ge, gt)
            ranks[r] = ranks[r] + first
    rank = jnp.concatenate(ranks, axis=0)
    drop = jnp.where(rank < float(n_sel), 0.0, 1.0)
    if rows < LANES:
        drop = jnp.concatenate([drop, jnp.ones((LANES - rows, tq), F32)], axis=0)
    drop = drop.astype(BF16)
    qst[LANES:2 * LANES, :] = jnp.concatenate([drop] * J, axis=1)

    _init_flash(m_sc, acc_sc)
    jd = t0 // tk
    off = (t0 - jd * tk) // tq
    n_off = tk // tq

    def put_scores(buf, j):
        k = ks_ref[0, 0, pl.ds(pl.multiple_of(j * tk, tk), tk), :]
        lhs = jnp.concatenate([k, ex_ref[j]], axis=1)
        cz = cz_ref[jnp.where(j == jd, off, n_off)].astype(F32)
        buf[...] = _dot(lhs, qst[...]) + jnp.concatenate([cz] * J, axis=1)

    def update(buf, j):
        _online_update(buf, p_sc, vs_ref[0, 0, j], m_sc, acc_sc)

    _flash_pipeline(jd + 1, put_scores, update, sa_sc, sb_sc)
    o_s = _normalized(acc_sc[...])
    o_w = _normalized(accw_sc[...])

    gates = jax.nn.sigmoid(misc_ref[0, 0])

    def gate_row(c):
        parts = []
        for j in range(J):
            r0 = gate_row0 + j * 3 + c
            r1 = gate_row0 + (J + j) * 3 + c
            parts.append(jnp.where(g == 0, gates[r0:r0 + 1, :], gates[r1:r1 + 1, :]))
        return jnp.concatenate(parts, axis=1)

    o = gate_row(0) * o_c[0:HEAD_DIM] + (gate_row(1) * o_s + gate_row(2) * o_w)
    _store_head_pairs(o_ref, o, J, tq)


def _nsa_attention(sl, cmp_k, cmp_vt, lay):
    q_t, k, v_t, misc_t = sl["qT"], sl["k"], sl["vT"], sl["fT"]
    B, _, T, _ = k.shape
    G, J = B_KV_GROUPS, B_PER_GROUP
    tq, tk = NSA_Q_TILE, KV_TILE
    nk = T // tk
    n_ch = T // CMP_STRIDE
    n_c = n_ch - CMP_BLOCK // CMP_STRIDE + 1
    n_s = T // SLC_BLOCK
    n_sel = min(SLC_TOPN, n_s)
    assert n_s <= LANES
    c0 = np.arange(n_ch) * CMP_STRIDE
    s0 = np.arange(LANES) * SLC_BLOCK
    ov = ((c0[None, :] < s0[:, None] + SLC_BLOCK) & (c0[None, :] + CMP_BLOCK > s0[:, None]))
    ov = ov & (np.arange(n_ch)[None, :] < n_c) & (np.arange(LANES)[:, None] < n_s)
    ov = jnp.asarray(ov.astype(np.float32), BF16)
    pos = np.arange(T).reshape(nk, tk, 1)
    ex = (pos // SLC_BLOCK == np.arange(LANES).reshape(1, 1, LANES))
    ex = jnp.asarray(ex.astype(np.float32) * -BIG, BF16)
    n_off = tk // tq
    kp = np.arange(tk).reshape(1, tk, 1)
    tl = np.arange(tq).reshape(1, 1, tq) + np.arange(n_off + 1).reshape(n_off + 1, 1, 1) * tq
    cz = np.where((kp <= tl) | (np.arange(n_off + 1).reshape(-1, 1, 1) == n_off), 0.0, -BIG)
    cz = jnp.asarray(cz.astype(np.float32), BF16)
    wlen = min(WINDOW + tq, T)
    n_wt = wlen // tq
    assert wlen % tq == 0 and T >= wlen
    kp = np.arange(wlen).reshape(1, wlen, 1)
    tl = np.arange(tq).reshape(1, 1, tq)
    early = kp <= tl + np.arange(n_wt).reshape(n_wt, 1, 1) * tq
    late = (kp > tl) & (kp <= tl + WINDOW)
    band = np.where(np.arange(n_wt).reshape(n_wt, 1, 1) == n_wt - 1, late, early)
    wz = jnp.asarray(np.where(band, 0.0, -BIG).astype(np.float32), BF16)
    kern = functools.partial(_nsa_kernel, tq=tq, tk=tk, T=T, n_s=n_s, n_sel=n_sel,
                             gate_row0=lay["gate_row0"])
    N = J * tq
    kslab = lambda off: pl.BlockSpec((1, 1, T, LANES), lambda b, g, i: (b, off + g, 0, 0))
    return pl.pallas_call(
        kern,
        grid=(B, G, T // tq),
        in_specs=[
            pl.BlockSpec((1, 2, LANES, tq), lambda b, g, i: (b, lay["qB"] // 2 + g, 0, i)),
            pl.BlockSpec((1, 1, 1, n_ch, LANES), lambda b, g, i: (0, b, g, 0, 0)),
            pl.BlockSpec((1, 1, 1, LANES, n_ch), lambda b, g, i: (1, b, g, 0, 0)),
            kslab(lay["ks"]),
            pl.BlockSpec((1, 1, nk, LANES, tk), lambda b, g, i: (b, lay["vs"] + g, 0, 0, 0)),
            kslab(lay["kw"]),
            pl.BlockSpec((1, 1, T // VW_TILE, LANES, VW_TILE),
                         lambda b, g, i: (b, lay["vw"] + g, 0, 0, 0)),
            pl.BlockSpec((1, 1, LANES, tq), lambda b, g, i: (b, lay["misc"], 0, i)),
            pl.BlockSpec((LANES, n_ch), lambda b, g, i: (0, 0)),
            pl.BlockSpec((nk, tk, LANES), lambda b, g, i: (0, 0, 0)),
            pl.BlockSpec((n_off + 1, tk, tq), lambda b, g, i: (0, 0, 0)),
            pl.BlockSpec((n_wt, wlen, tq), lambda b, g, i: (0, 0, 0)),
        ],
        out_specs=pl.BlockSpec((1, tq, J * HEAD_DIM), lambda b, g, i: (b, i, g)),
        out_shape=jax.ShapeDtypeStruct((B, T, B_HEADS * HEAD_DIM), BF16),
        scratch_shapes=[
            pltpu.VMEM((2 * LANES, N), BF16),
            pltpu.VMEM((1, N), F32),
            pltpu.VMEM((LANES, N), F32),
            pltpu.VMEM((tk, N), F32),
            pltpu.VMEM((tk, N), F32),
            pltpu.VMEM((tk, N), BF16),
            pltpu.VMEM((1, N), F32),
            pltpu.VMEM((LANES, N), F32),
            pltpu.VMEM((wlen, N), F32),
            pltpu.VMEM((wlen, N), BF16),
        ],
        compiler_params=_params(("parallel", "parallel", "arbitrary")),
        name="nsa_attention",
    )(q_t, cmp_k, cmp_vt, k, v_t, k, sl["vTw"], misc_t, ov, ex, cz, wz)


def _rms(x, g):
    ms = jnp.mean(x * x, axis=-1, keepdims=True)
    return (x * lax.rsqrt(ms + NORM_EPS)) * g


def _layer_tail_kernel(*refs, n_in, final_norm):
    h_ref = refs[0]
    o_refs = refs[1:1 + n_in]
    w_refs = refs[1 + n_in:1 + 2 * n_in]
    g_ref, wu_ref, wd_ref = refs[1 + 2 * n_in:4 + 2 * n_in]
    rest = refs[4 + 2 * n_in:]
    if final_norm:
        fg_ref, out_ref, xn_sc = rest
    else:
        out_ref, xn_sc = rest
    f = pl.program_id(1)

    @pl.when(f == 0)
    def _():
        mix = _dot(o_refs[0][...], w_refs[0][...])
        for i in range(1, n_in):
            mix = mix + _dot(o_refs[i][...], w_refs[i][...])
        x = h_ref[...] + mix
        xn_sc[...] = _rms(x, g_ref[...]).astype(BF16)
        out_ref[...] = x

    u = _dot(xn_sc[...], wu_ref[...])
    a = jnp.square(jnp.maximum(u, 0.0)).astype(BF16)
    out_ref[...] += _dot(a, wd_ref[...])

    if final_norm:
        @pl.when(f == pl.num_programs(1) - 1)
        def _():
            out_ref[...] = _rms(out_ref[...], fg_ref[...])


def _layer_tail(h2, outs, ws, g, w_up, w_down, final_g=None, tm=1024, tf=1024):
    N, D = h2.shape
    F = w_up.shape[1]
    tm = min(tm, N)
    n_in = len(outs)
    in_specs = [pl.BlockSpec((tm, D), lambda i, f: (i, 0))]
    in_specs += [pl.BlockSpec((tm, o.shape[1]), lambda i, f: (i, 0)) for o in outs]
    in_specs += [pl.BlockSpec(w.shape, lambda i, f: (0, 0)) for w in ws]
    in_specs += [
        pl.BlockSpec((1, D), lambda i, f: (0, 0)),
        pl.BlockSpec((D, tf), lambda i, f: (0, f)),
        pl.BlockSpec((tf, D), lambda i, f: (f, 0)),
    ]
    args = [h2, *outs, *ws, g.reshape(1, D), w_up, w_down]
    if final_g is not None:
        in_specs.append(pl.BlockSpec((1, D), lambda i, f: (0, 0)))
        args.append(final_g.reshape(1, D))
    kern = functools.partial(_layer_tail_kernel, n_in=n_in, final_norm=final_g is not None)
    return pl.pallas_call(
        kern,
        grid=(N // tm, F // tf),
        in_specs=in_specs,
        out_specs=pl.BlockSpec((tm, D), lambda i, f: (i, 0)),
        out_shape=jax.ShapeDtypeStruct((N, D), F32),
        scratch_shapes=[pltpu.VMEM((tm, D), BF16)],
        compiler_params=_params(("parallel", "arbitrary")),
        name="out_proj_mlp",
    )(*args)


def _even_layout():
    offs = {}
    o = 0
    for name, size in (("qa", 512), ("ka", 64), ("va", 64), ("qi", 256), ("ki", 64), ("wi", 4),
                       ("qb", 512), ("kvb", 768), ("gb", 24)):
        offs[name] = o
        o += size
    kvb = lambda which, g: offs["kvb"] + (which * B_KV_GROUPS + g) * HEAD_DIM
    cols, plan, lay = [], [], {}
    n = {k: 0 for k in _KINDS}

    def add(c, roped, kind, opt=None):
        cols.append(c)
        plan.append((roped, kind, n[kind], opt))
        n[kind] += 1
        return n[kind] - 1

    lay["qA"] = n["qT"]
    for p in range(4):
        add(_pair_cols(offs["qa"] + 2 * p * 64, offs["qa"] + (2 * p + 1) * 64), True, "qT", Q_SCALE)
    lay["qB"] = n["qT"]
    for p in range(4):
        add(_pair_cols(offs["qb"] + 2 * p * 64, offs["qb"] + (2 * p + 1) * 64), True, "qT", Q_SCALE)
    lay["qi"] = n["qT"]
    for p in range(2):
        add(_pair_cols(offs["qi"] + 2 * p * 64, offs["qi"] + (2 * p + 1) * 64), True, "qT")
    lay["kA"] = add(_pair_cols(offs["ka"], offs["ka"]), True, "k")
    lay["ki"] = add(_pair_cols(offs["ki"], offs["ki"]), True, "k")
    lay["ks"] = n["k"]
    for g in range(2):
        add(_pair_cols(kvb(2, g), kvb(2, g)), True, "k")
    lay["kw"] = n["k"]
    for g in range(2):
        add(_pair_cols(kvb(4, g), kvb(4, g)), True, "k")
    lay["kc"] = add(_pair_cols(kvb(0, 0), kvb(0, 1)), True, "f")
    lay["vc"] = add(_pair_cols(kvb(1, 0), kvb(1, 1)), False, "f")
    misc = np.full(LANES, -1)
    misc[0:IDX_HEADS] = offs["wi"] + np.arange(IDX_HEADS)
    lay["gate_row0"] = 8
    misc[8:8 + 24] = offs["gb"] + np.arange(24)
    lay["misc"] = add(misc, False, "fT")
    lay["vw"] = n["vTw"]
    for g in range(2):
        add(_head_cols(kvb(5, g)), False, "vTw")
    lay["vA"] = add(_head_cols(offs["va"]), False, "vT", "ones_row_64")
    lay["vs"] = n["vT"]
    for g in range(2):
        add(_head_cols(kvb(3, g)), False, "vT", "ones_row_64")
    return np.concatenate(cols), plan, lay


def _odd_layout():
    cols, plan = [], []
    for h in range(C_HEADS):
        cols.append(_pair_cols(h * 128, h * 128 + 64))
        plan.append((True, "qT", h, Q_SCALE))
    for h in range(C_HEADS):
        cols.append(_pair_cols(1024 + h * 128, 1024 + h * 128 + 64))
        plan.append((True, "k", h, None))
    for h in range(C_HEADS):
        cols.append(2048 + h * 128 + np.arange(LANES))
        plan.append((False, "vT", h, "ones_rows_below"))
    return np.concatenate(cols), plan


def _compress_weights(pe, w1, w2):
    d = _PAIR_D
    which = _PAIR_WHICH
    pe_l = pe[:, :, d]
    pe_l = pe_l.reshape(2, 2, 1, CMP_STRIDE * LANES)
    w1r = w1.astype(BF16).reshape(2, CMP_BLOCK, HEAD_DIM, CMP_HIDDEN)[:, :, d, :]
    per_g = []
    for g in range(B_KV_GROUPS):
        keep = jnp.asarray(which == g)[None, None, :, None]
        per_g.append(jnp.where(keep, w1r, jnp.zeros_like(w1r)))
    w1g = jnp.stack(per_g, axis=1)
    w1g = w1g.reshape(2, B_KV_GROUPS, 2, CMP_STRIDE * LANES, CMP_HIDDEN)
    w2b = w2.astype(BF16)
    w2k = w2b[0][:, d]
    w2v = w2b[1][:, np.arange(LANES) % HEAD_DIM]
    w2l = jnp.stack([w2k, w2v], axis=0)
    return pe_l, w1g, w2l


def _even_mixer(h, norm_g, w_in, cmp_pe, cmp_w1, cmp_w2, w_out, cos_slab, sin_slab):
    B, T, D = h.shape
    cols, plan, lay = _even_layout()
    w = _gather_cols(w_in.astype(BF16), cols)
    sl = _project(h, norm_g, w, cos_slab, sin_slab, plan)

    bias = _dsa_select(sl, lay)
    o_a = _dsa_attention(sl, bias, lay)

    n_ch = T // CMP_STRIDE
    assert (lay["kc"], lay["vc"]) == (0, 1) and sl["f"].shape[1] == 2
    flat = sl["f"].reshape(B, 2, n_ch, CMP_STRIDE * LANES)
    pe_l, w1g, w2l = _compress_weights(cmp_pe, cmp_w1, cmp_w2)
    cmp_k, cmp_vt = _compress(flat, pe_l, w1g, w2l)
    o_b = _nsa_attention(sl, cmp_k, cmp_vt, lay)

    na = A_HEADS * HEAD_DIM
    wo = w_out.astype(BF16)
    return [o_a.reshape(B * T, -1), o_b.reshape(B * T, -1)], [wo[:na], wo[na:]]


def _odd_mixer(h, norm_g, w_in, lam, subln_g, w_out, cos_slab, sin_slab, lambda_init):
    B, T, D = h.shape
    cols, plan = _odd_layout()
    w = _gather_cols(w_in.astype(BF16), cols)
    sl = _project(h, norm_g, w, cos_slab, sin_slab, plan)
    o = _diff_attention(sl, lam, subln_g, lambda_init)
    return [o.reshape(B * T, -1)], [w_out.astype(BF16)]


def kernel(x, mix_norm_g, mlp_norm_g, even_w_in, even_cmp_pe, even_cmp_w1, even_cmp_w2, even_w_out, odd_w_in, odd_lambda, odd_subln_g, odd_w_out, mlp_w_up, mlp_w_down, final_norm_g):
    B, T, D = x.shape
    depth = mix_norm_g.shape[0]
    assert depth >= 1
    cos_slab, sin_slab = _rope_slabs(T)
    h = x
    for layer in range(depth):
        if layer % 2 == 0:
            e = layer // 2
            outs, ws = _even_mixer(h, mix_norm_g[layer], even_w_in[e], even_cmp_pe[e],
                                   even_cmp_w1[e], even_cmp_w2[e], even_w_out[e], cos_slab, sin_slab)
        else:
            o = layer // 2
            lambda_init = 0.8 - 0.6 * math.exp(-0.3 * layer)
            outs, ws = _odd_mixer(h, mix_norm_g[layer], odd_w_in[o], odd_lambda[o], odd_subln_g[o],
                                  odd_w_out[o], cos_slab, sin_slab, lambda_init)
        h2 = _layer_tail(h.reshape(B * T, D), outs, ws, mlp_norm_g[layer],
                         mlp_w_up[layer].astype(BF16), mlp_w_down[layer].astype(BF16),
                         final_g=final_norm_g if layer == depth - 1 else None)
        h = h2.reshape(B, T, D)
    return h
```

```python
import functools
import math

import numpy as np
import jax
import jax.numpy as jnp
from jax import lax
from jax.experimental import pallas as pl
from jax.experimental.pallas import tpu as pltpu

HEAD_DIM = 64
HALF = HEAD_DIM // 2
LANES = 128
ROPE_THETA = 10000.0
NORM_EPS = 1e-6
SCALE = HEAD_DIM ** -0.5

A_HEADS = 8
IDX_HEADS = 4
DSA_TOPK = 256
B_HEADS = 8
B_KV_GROUPS = 2
B_PER_GROUP = B_HEADS // B_KV_GROUPS
CMP_BLOCK = 32
CMP_STRIDE = 16
CMP_HIDDEN = 256
SLC_BLOCK = 64
SLC_TOPN = 16
WINDOW = 512
C_HEADS = 8

KV_TILE = 512
SEL_Q_TILE = 128
DSA_Q_TILE = 256
NSA_Q_TILE = 256
VW_TILE = 128

LOG2E = math.log2(math.e)
Q_SCALE = SCALE * LOG2E
ONES_ROWS = 16
BIG = 2.0 ** 100

NEG_INF = float("-inf")
M_FLOOR = -1e30
INT_MIN = -(2 ** 31)

VMEM_LIMIT = 56 * 1024 * 1024

BF16 = jnp.bfloat16
F32 = jnp.float32


def _dot(a, b):
    return jnp.dot(a, b, preferred_element_type=F32)


def _dot_tn(a, b):
    return lax.dot_general(a, b, (((0,), (0,)), ((), ())), preferred_element_type=F32)


def _params(sem):
    return pltpu.CompilerParams(dimension_semantics=sem, vmem_limit_bytes=VMEM_LIMIT)


def _iota(shape, axis):
    return lax.broadcasted_iota(jnp.int32, shape, axis)


def _pair_cols(base_a, base_b):
    lane = np.arange(LANES)
    half = lane // 64
    which = (lane % 64) // HALF
    i = lane % HALF
    base = np.where(which == 0, base_a, base_b)
    return base + half * HALF + i


def _dup_cols(base):
    return base + np.arange(LANES) % HEAD_DIM


def _head_cols(base):
    lane = np.arange(LANES)
    return np.where(lane < HEAD_DIM, base + lane, -1)


_PAIR_D = _pair_cols(0, 0)
_PAIR_WHICH = (np.arange(LANES) % 64) // HALF


def _gather_cols(w, cols):
    cols = np.asarray(cols)
    safe = np.where(cols >= 0, cols, 0)
    g = jnp.take(w, jnp.asarray(safe, dtype=jnp.int32), axis=1)
    return jnp.where(jnp.asarray(cols >= 0)[None, :], g, jnp.zeros_like(g))


def _rope_slabs(T):
    inv = 1.0 / (ROPE_THETA ** (jnp.arange(0, HEAD_DIM, 2, dtype=F32) / HEAD_DIM))
    ang = jnp.arange(T, dtype=F32)[:, None] * inv[None, :]
    cos, sin = jnp.cos(ang), jnp.sin(ang)
    cos_slab = jnp.tile(cos, (1, 4))
    sin_slab = jnp.concatenate([-sin, -sin, sin, sin], axis=1)
    return cos_slab, sin_slab


_KINDS = ("qT", "k", "vT", "vTw", "f", "fT")


def _proj_kernel(x_ref, g_ref, w_ref, cos_ref, sin_ref, *out_refs, plan, kinds, chunk):
    outs = dict(zip(kinds, out_refs))
    x = x_ref[0]
    ms = jnp.mean(x * x, axis=-1, keepdims=True)
    xn = ((x * lax.rsqrt(ms + NORM_EPS)) * g_ref[...]).astype(BF16)
    cos = cos_ref[...]
    sin = sin_ref[...]
    n = len(plan)
    for c0 in range(0, n, chunk):
        c1 = min(c0 + chunk, n)
        r = _dot(xn, w_ref[:, c0 * LANES:c1 * LANES])
        for s in range(c0, c1):
            roped, kind, idx, opt = plan[s]
            y = r[:, (s - c0) * LANES:(s - c0 + 1) * LANES]
            if roped:
                y = y * cos + pltpu.roll(y, 64, 1) * sin
            if kind == "qT":
                if opt is not None:
                    y = y * opt
                outs[kind][0, idx] = y.T.astype(BF16)
            elif kind == "k":
                outs[kind][0, idx] = y.astype(BF16)
            elif kind == "vT":
                if opt == "ones_row_64":
                    y = jnp.where(_iota(y.shape, 1) == HEAD_DIM, 1.0, y)
                    outs[kind][0, idx, 0] = y.T.astype(BF16)
                else:
                    outs[kind][0, idx, 0, 0:LANES, :] = y.T.astype(BF16)
                    outs[kind][0, idx, 0, LANES:, :] = jnp.ones((ONES_ROWS, y.shape[0]), BF16)
            elif kind == "vTw":
                y_t = jnp.where(_iota(y.shape, 1) == HEAD_DIM, 1.0, y).T.astype(BF16)
                for sub in range(y.shape[0] // VW_TILE):
                    outs[kind][0, idx, sub] = y_t[:, sub * VW_TILE:(sub + 1) * VW_TILE]
            elif kind == "f":
                outs[kind][0, idx] = y
            else:
                outs[kind][0, idx] = y.T


def _project(x, g, w, cos_slab, sin_slab, plan, chunk=4):
    B, T, D = x.shape
    tm = KV_TILE
    assert T % tm == 0
    count = {k: sum(1 for p in plan if p[1] == k) for k in _KINDS}
    kinds = tuple(k for k in _KINDS if count[k])
    out_specs, out_shape = [], []
    for k in kinds:
        n = count[k]
        if k in ("qT", "fT"):
            out_specs.append(pl.BlockSpec((1, n, LANES, tm), lambda b, i: (b, 0, 0, i)))
            out_shape.append(jax.ShapeDtypeStruct((B, n, LANES, T), BF16 if k == "qT" else F32))
        elif k in ("k", "f"):
            out_specs.append(pl.BlockSpec((1, n, tm, LANES), lambda b, i: (b, 0, i, 0)))
            out_shape.append(jax.ShapeDtypeStruct((B, n, T, LANES), BF16 if k == "k" else F32))
        elif k == "vTw":
            sub = tm // VW_TILE
            out_specs.append(pl.BlockSpec((1, n, sub, LANES, VW_TILE), lambda b, i: (b, 0, i, 0, 0)))
            out_shape.append(jax.ShapeDtypeStruct((B, n, T // VW_TILE, LANES, VW_TILE), BF16))
        else:
            wide = any(p[1] == "vT" and p[3] != "ones_row_64" for p in plan)
            rows = LANES + ONES_ROWS if wide else LANES
            out_specs.append(pl.BlockSpec((1, n, 1, rows, tm), lambda b, i: (b, 0, i, 0, 0)))
            out_shape.append(jax.ShapeDtypeStruct((B, n, T // tm, rows, tm), BF16))
    kern = functools.partial(_proj_kernel, plan=tuple(plan), kinds=kinds, chunk=chunk)
    outs = pl.pallas_call(
        kern,
        grid=(B, T // tm),
        in_specs=[
            pl.BlockSpec((1, tm, D), lambda b, i: (b, i, 0)),
            pl.BlockSpec((1, D), lambda b, i: (0, 0)),
            pl.BlockSpec((D, len(plan) * LANES), lambda b, i: (0, 0)),
            pl.BlockSpec((tm, LANES), lambda b, i: (i, 0)),
            pl.BlockSpec((tm, LANES), lambda b, i: (i, 0)),
        ],
        out_specs=out_specs,
        out_shape=out_shape,
        compiler_params=_params(("parallel", "parallel")),
        name="norm_proj_rope",
    )(x, g.reshape(1, D), w, cos_slab, sin_slab)
    return dict(zip(kinds, outs))


def _head_rows(slab_t, which):
    row = _iota(slab_t.shape, 0)
    keep = ((row % 64) // HALF) == which
    return jnp.where(keep, slab_t, jnp.zeros_like(slab_t))


def _tree(op, xs):
    while len(xs) > 1:
        xs = [op(xs[i], xs[i + 1]) if i + 1 < len(xs) else xs[i] for i in range(0, len(xs), 2)]
    return xs[0]


def _fold_rows_max(x, ways=4):
    rows, n = x.shape
    per = rows // ways
    parts = [jnp.max(x[i * per:(i + 1) * per].reshape(per // 8, 8, n), axis=0) for i in range(ways)]
    return _tree(jnp.maximum, parts)


ROW_BLOCK = 64


def _online_update(s_sc, p_sc, v_aug, m_ref, acc_ref):
    tk, n = s_sc.shape
    m_prev = m_ref[...]
    m_tile = jnp.max(_fold_rows_max(s_sc[...]), axis=0, keepdims=True)
    m_new = jnp.maximum(m_prev, m_tile)
    alpha = jnp.exp2(m_prev - m_new)
    for r in range(tk // ROW_BLOCK):
        rows = slice(r * ROW_BLOCK, (r + 1) * ROW_BLOCK)
        p_sc[rows, :] = jnp.exp2(s_sc[rows, :] - m_new).astype(BF16)
    acc_ref[...] = alpha * acc_ref[...] + _dot(v_aug, p_sc[...])
    m_ref[...] = m_new


def _flash_pipeline(n, put_scores, update, buf_a, buf_b, put_last=None):
    if put_last is None:
        put_scores(buf_a, 0)
    else:
        @pl.when(n > 0)
        def _():
            put_scores(buf_a, 0)

    def pair(p, carry):
        j = 2 * p
        put_scores(buf_b, j + 1)
        update(buf_a, j)

        @pl.when(j + 2 < n)
        def _():
            put_scores(buf_a, j + 2)
            update(buf_b, j + 1)

        return carry

    lax.fori_loop(0, n // 2, pair, 0)
    in_b = jnp.logical_and(n > 0, n % 2 == 0)
    in_a = n % 2 == 1

    def finish(cur, other):
        if put_last is not None:
            put_last(other, n)
        update(cur, n - 1)
        if put_last is not None:
            update(other, n)

    pl.when(in_b)(lambda: finish(buf_b, buf_a))
    pl.when(in_a)(lambda: finish(buf_a, buf_b))
    if put_last is not None:
        @pl.when(n == 0)
        def _():
            put_last(buf_a, 0)
            update(buf_a, 0)


def _softmax_keys(s):
    m = jnp.max(s, axis=0, keepdims=True)
    m = jnp.where(m > NEG_INF, m, 0.0)
    e = jnp.exp2(s - m)
    return e / jnp.maximum(jnp.sum(e, axis=0, keepdims=True), 1e-30)


def _init_flash(m_ref, acc_ref):
    m_ref[...] = jnp.full(m_ref.shape, M_FLOOR, F32)
    acc_ref[...] = jnp.zeros(acc_ref.shape, F32)


def _diff_attn_kernel(lam_ref, q_ref, k_ref, v_ref, g_ref, o_ref, m_sc, acc_sc, sa_sc, sb_sc,
                      p_sc, *, tq, lambda_init):
    qi = pl.program_id(2)
    q_t = q_ref[0, 0]
    qs = [_head_rows(q_t, c) for c in range(2)]
    for c in range(2):
        _init_flash(m_sc.at[c], acc_sc.at[c])

    def put_scores(buf, j, masked=False):
        k = k_ref[0, 0, pl.ds(pl.multiple_of(j * tq, tq), tq), :]
        for c in range(2):
            s = _dot(k, qs[c])
            if masked:
                s = jnp.where(_iota(s.shape, 0) <= _iota(s.shape, 1), s, NEG_INF)
            buf[c] = s

    def put_diagonal(buf, j):
        put_scores(buf, j, masked=True)

    def update(buf, j):
        v_aug = v_ref[0, 0, j]
        for c in range(2):
            _online_update(buf.at[c], p_sc.at[c], v_aug, m_sc.at[c], acc_sc.at[c])

    _flash_pipeline(qi, put_scores, update, sa_sc, sb_sc, put_last=put_diagonal)

    lam = lam_ref[...]
    s01 = jnp.sum(lam[0:1] * lam[1:2], axis=-1, keepdims=True)
    s23 = jnp.sum(lam[2:3] * lam[3:4], axis=-1, keepdims=True)
    lam_val = jnp.exp(s01) - jnp.exp(s23) + lambda_init
    o0 = acc_sc[0, 0:LANES] / jnp.maximum(acc_sc[0, LANES:LANES + 1], 1e-30)
    o1 = acc_sc[1, 0:LANES] / jnp.maximum(acc_sc[1, LANES:LANES + 1], 1e-30)
    o = o0 - lam_val * o1
    y = o * lax.rsqrt(jnp.mean(o * o, axis=0, keepdims=True) + NORM_EPS)
    y = (y * g_ref[...]) * (1.0 - lambda_init)
    o_ref[0] = y.T.astype(o_ref.dtype)


def _diff_attention(sl, lam, subln_g, lambda_init):
    q_t, k, v_t = sl["qT"], sl["k"], sl["vT"]
    B, H, T, _ = k.shape
    tq = KV_TILE
    nk = T // tq
    kern = functools.partial(_diff_attn_kernel, tq=tq, lambda_init=lambda_init)
    return pl.pallas_call(
        kern,
        grid=(B, H, T // tq),
        in_specs=[
            pl.BlockSpec((4, HEAD_DIM), lambda b, h, i: (0, 0)),
            pl.BlockSpec((1, 1, LANES, tq), lambda b, h, i: (b, h, 0, i)),
            pl.BlockSpec((1, 1, T, LANES), lambda b, h, i: (b, h, 0, 0)),
            pl.BlockSpec((1, 1, nk, LANES + ONES_ROWS, tq), lambda b, h, i: (b, h, 0, 0, 0)),
            pl.BlockSpec((LANES, 1), lambda b, h, i: (0, 0)),
        ],
        out_specs=pl.BlockSpec((1, tq, LANES), lambda b, h, i: (b, i, h)),
        out_shape=jax.ShapeDtypeStruct((B, T, H * LANES), BF16),
        scratch_shapes=[
            pltpu.VMEM((2, 1, tq), F32),
            pltpu.VMEM((2, LANES + ONES_ROWS, tq), F32),
            pltpu.VMEM((2, tq, tq), F32),
            pltpu.VMEM((2, tq, tq), F32),
            pltpu.VMEM((2, tq, tq), BF16),
        ],
        compiler_params=_params(("parallel", "parallel", "arbitrary")),
        name="diff_attention",
    )(lam, q_t, k, v_t, subln_g.reshape(LANES, 1))


def _dsa_select_kernel(qi_ref, ki_ref, misc_ref, tril_ref, bias_ref, key_sc, *, tq, ck, nk, topk):
    t0 = pl.program_id(1) * tq
    nvalid = (t0 + tq - 1) // ck + 1
    w = misc_ref[0, 0]
    qh = jnp.concatenate([_head_rows(qi_ref[0, h // 2], h % 2) for h in range(IDX_HEADS)], axis=1)
    t_q = t0 + _iota((ck, tq), 1)

    def causal(c):
        return (c * ck + _iota((ck, tq), 0)) <= t_q

    def fill(c):
        kk = ki_ref[0, 0, c * ck:(c + 1) * ck, :]
        r = _dot(kk, qh)
        score = jnp.zeros((ck, tq), F32)
        for h in range(IDX_HEADS):
            score = score + w[h:h + 1, :] * jnp.maximum(r[:, h * tq:(h + 1) * tq], 0.0)
        bits = pltpu.bitcast(score, jnp.int32)
        key = bits ^ ((bits >> 31) & jnp.int32(0x7FFFFFFF))
        key = jnp.where(score == 0.0, 0, key)
        key_sc[c] = jnp.where(causal(c), key, INT_MIN)

    kf = float(topk)

    def select_threshold(n):
        for c in range(n):
            fill(c)

        def count(pred):
            acc = jnp.zeros((ck // 8, tq), F32)
            for c in range(n):
                ind = pred(key_sc[c], c)
                acc = acc + jnp.sum(ind.reshape(8, ck // 8, tq), axis=0)
            return jnp.sum(acc, axis=0, keepdims=True)

        zero = jnp.zeros((1, tq), jnp.int32)
        ans = jnp.where(count(lambda kc, c: jnp.where(kc >= zero, 1.0, 0.0)) >= kf, 0, INT_MIN)

        def bit_step(i, ans):
            cand = ans | (jnp.int32(1) << (30 - i))
            cnt = count(lambda kc, c: jnp.where(kc >= cand, 1.0, 0.0))
            return jnp.where(cnt >= kf, cand, ans)

        ans = lax.fori_loop(0, 31, bit_step, ans)
        cnt_gt = count(lambda kc, c: jnp.where(kc > ans, 1.0, 0.0))
        cnt_eq = count(lambda kc, c: jnp.where(kc == ans, jnp.where(causal(c), 1.0, 0.0), 0.0))
        return ans, cnt_gt, cnt_eq

    ans, cnt_gt, cnt_eq = lax.switch(
        nvalid - 1, [functools.partial(select_threshold, n) for n in range(1, nk + 1)])
    need = kf - cnt_gt
    has_tie = jnp.max(cnt_eq - need) > 0.0

    @pl.when(jnp.logical_not(has_tie))
    def _():
        def emit(c, carry):
            sel = jnp.where(key_sc[c] >= ans, jnp.where(causal(c), 0.0, NEG_INF), NEG_INF)
            bias_ref[0, 0, c] = sel.astype(BF16)
            return carry
        lax.fori_loop(0, nvalid, emit, 0)

    @pl.when(has_tie)
    def _():
        def emit(c, carry):
            kc = key_sc[c]
            eq = jnp.where(kc == ans, jnp.where(causal(c), 1.0, 0.0), 0.0)
            pre = _dot(tril_ref[...], eq.astype(BF16)) + carry
            take = jnp.where(kc > ans, 1.0, jnp.where(pre < need, eq, 0.0))
            sel = jnp.where(take > 0.5, jnp.where(causal(c), 0.0, NEG_INF), NEG_INF)
            bias_ref[0, 0, c] = sel.astype(BF16)
            return carry + jnp.sum(eq, axis=0, keepdims=True)
        lax.fori_loop(0, nvalid, emit, jnp.zeros((1, tq), F32))

    def blank(c, carry):
        bias_ref[0, 0, c] = jnp.full((ck, tq), NEG_INF, BF16)
        return carry

    lax.fori_loop(nvalid, nk, blank, 0)


def _dsa_select(sl, lay):
    q_t, k, misc_t = sl["qT"], sl["k"], sl["fT"]
    B, _, T, _ = k.shape
    tq, ck = SEL_Q_TILE, KV_TILE
    nk = T // ck
    topk = min(DSA_TOPK, T // 4)
    tril = jnp.asarray(np.tril(np.ones((ck, ck), np.float32), -1), BF16)
    kern = functools.partial(_dsa_select_kernel, tq=tq, ck=ck, nk=nk, topk=topk)
    return pl.pallas_call(
        kern,
        grid=(B, T // tq),
        in_specs=[
            pl.BlockSpec((1, 2, LANES, tq), lambda b, i: (b, lay["qi"] // 2, 0, i)),
            pl.BlockSpec((1, 1, T, LANES), lambda b, i: (b, lay["ki"], 0, 0)),
            pl.BlockSpec((1, 1, LANES, tq), lambda b, i: (b, lay["misc"], 0, i)),
            pl.BlockSpec((ck, ck), lambda b, i: (0, 0)),
        ],
        out_specs=pl.BlockSpec((1, 1, nk, ck, tq), lambda b, i: (b, i, 0, 0, 0)),
        out_shape=jax.ShapeDtypeStruct((B, T // tq, nk, ck, tq), BF16),
        scratch_shapes=[pltpu.VMEM((nk, ck, tq), jnp.int32)],
        compiler_params=_params(("parallel", "parallel")),
        name="dsa_select",
    )(q_t, k, misc_t, tril)


def _store_head_pairs(o_ref, o, n_heads, tq, col0=0):
    for p in range(n_heads // 2):
        even = o[0:HEAD_DIM, (2 * p) * tq:(2 * p + 1) * tq]
        odd = o[0:HEAD_DIM, (2 * p + 1) * tq:(2 * p + 2) * tq]
        pair = jnp.concatenate([even, odd], axis=0)
        o_ref[0, :, col0 + p * LANES:col0 + (p + 1) * LANES] = pair.T.astype(o_ref.dtype)


def _normalized(acc):
    return acc[0:HEAD_DIM] / jnp.maximum(acc[HEAD_DIM:HEAD_DIM + 1], 1e-30)


def _dsa_attn_kernel(q_ref, k_ref, v_ref, bias_ref, o_ref, qst, m_sc, acc_sc, sa_sc, sb_sc,
                     p_sc, *, tq, tk):
    qi = pl.program_id(1)
    H = A_HEADS
    for h in range(H):
        qst[:, h * tq:(h + 1) * tq] = _head_rows(q_ref[0, h // 2], h % 2)
    _init_flash(m_sc, acc_sc)
    nkv = (qi * tq) // tk + 1

    def put_scores(buf, j):
        k = k_ref[0, 0, pl.ds(pl.multiple_of(j * tk, tk), tk), :]
        parts = [bias_ref[0, s, j].astype(F32) for s in range(bias_ref.shape[1])]
        b = parts[0] if len(parts) == 1 else jnp.concatenate(parts, axis=1)
        buf[...] = _dot(k, qst[...]) + jnp.concatenate([b] * H, axis=1)

    def update(buf, j):
        _online_update(buf, p_sc, v_ref[0, 0, j], m_sc, acc_sc)

    _flash_pipeline(nkv, put_scores, update, sa_sc, sb_sc)
    _store_head_pairs(o_ref, _normalized(acc_sc[...]), H, tq)


def _dsa_attention(sl, bias, lay):
    q_t, k, v_t = sl["qT"], sl["k"], sl["vT"]
    B, _, T, _ = k.shape
    tq, tk = DSA_Q_TILE, KV_TILE
    nk = T // tk
    H = A_HEADS
    N = H * tq
    sel_tq = bias.shape[4]
    nsub = tq // sel_tq
    kern = functools.partial(_dsa_attn_kernel, tq=tq, tk=tk)
    return pl.pallas_call(
        kern,
        grid=(B, T // tq),
        in_specs=[
            pl.BlockSpec((1, H // 2, LANES, tq), lambda b, i: (b, lay["qA"] // (H // 2), 0, i)),
            pl.BlockSpec((1, 1, T, LANES), lambda b, i: (b, lay["kA"], 0, 0)),
            pl.BlockSpec((1, 1, nk, LANES, tk), lambda b, i: (b, lay["vA"], 0, 0, 0)),
            pl.BlockSpec((1, nsub, nk, tk, sel_tq), lambda b, i: (b, i, 0, 0, 0)),
        ],
        out_specs=pl.BlockSpec((1, tq, H * HEAD_DIM), lambda b, i: (b, i, 0)),
        out_shape=jax.ShapeDtypeStruct((B, T, H * HEAD_DIM), BF16),
        scratch_shapes=[
            pltpu.VMEM((LANES, N), BF16),
            pltpu.VMEM((1, N), F32),
            pltpu.VMEM((LANES, N), F32),
            pltpu.VMEM((tk, N), F32),
            pltpu.VMEM((tk, N), F32),
            pltpu.VMEM((tk, N), BF16),
        ],
        compiler_params=_params(("parallel", "arbitrary")),
        name="dsa_attention",
    )(q_t, k, v_t, bias)


def _compress_kernel(x_ref, pe_ref, w1_ref, w2_ref, o_ref, ot_ref, *, n_ch):
    x = x_ref[0, 0]
    xt = (x + pe_ref[0, 0]).astype(BF16)
    xb = (x + pe_ref[0, 1]).astype(BF16)
    for g in range(B_KV_GROUPS):
        a = _dot(xt, w1_ref[0, g, 0])
        b = _dot(xb, w1_ref[0, g, 1])
        h = jax.nn.gelu(a + pltpu.roll(b, n_ch - 1, 0))
        r = _dot(h.astype(BF16), w2_ref[0])
        o_ref[0, 0, g] = r.astype(BF16)
        ot_ref[0, 0, g] = r.T.astype(BF16)


def _compress(flat, pe, w1, w2):
    B, _, n_ch, W = flat.shape
    G = B_KV_GROUPS
    kern = functools.partial(_compress_kernel, n_ch=n_ch)
    return pl.pallas_call(
        kern,
        grid=(2, B),
        in_specs=[
            pl.BlockSpec((1, 1, n_ch, W), lambda s, b: (b, s, 0, 0)),
            pl.BlockSpec((1, 2, 1, W), lambda s, b: (s, 0, 0, 0)),
            pl.BlockSpec((1, G, 2, W, CMP_HIDDEN), lambda s, b: (s, 0, 0, 0, 0)),
            pl.BlockSpec((1, CMP_HIDDEN, LANES), lambda s, b: (s, 0, 0)),
        ],
        out_specs=[
            pl.BlockSpec((1, 1, G, n_ch, LANES), lambda s, b: (s, b, 0, 0, 0)),
            pl.BlockSpec((1, 1, G, LANES, n_ch), lambda s, b: (s, b, 0, 0, 0)),
        ],
        out_shape=[
            jax.ShapeDtypeStruct((2, B, G, n_ch, LANES), BF16),
            jax.ShapeDtypeStruct((2, B, G, LANES, n_ch), BF16),
        ],
        compiler_params=_params(("parallel", "parallel")),
        name="nsa_compress",
    )(flat, pe, w1, w2)


def _nsa_kernel(q_ref, kc_ref, vc_ref, ks_ref, vs_ref, kw_ref, vw_ref, misc_ref, ov_ref, ex_ref,
                cz_ref, wz_ref, o_ref, qst, m_sc, acc_sc, sa_sc, sb_sc, p_sc, mw_sc, accw_sc,
                sw_sc, pw_sc, oc_sc, *, tq, tk, T, n_s, n_sel, gate_row0):
    qi = pl.program_id(1)
    t0 = qi * tq
    G, J = B_KV_GROUPS, B_PER_GROUP
    N = J * tq
    for g in range(G):
        for j in range(J):
            qst[g, 0:LANES, j * tq:(j + 1) * tq] = _head_rows(q_ref[0, 2 * g + j // 2], j % 2)

    def q_time(shape):
        return t0 + (_iota(shape, 1) % tq)

    wlen = min(WINDOW + tq, T)
    wstart = pl.multiple_of(jnp.maximum(t0 - WINDOW, 0), tq)
    n_wt = wlen // tq
    wt = wstart // VW_TILE
    wz = wz_ref[jnp.minimum(qi, n_wt - 1)].astype(F32)
    wz = jnp.concatenate([wz] * J, axis=1)
    for g in range(G):
        kw = kw_ref[0, g, pl.ds(wstart, wlen), :]
        vw = jnp.concatenate([vw_ref[0, g, wt + i] for i in range(wlen // VW_TILE)], axis=1)
        sw_sc[g] = _dot(kw, qst[g, 0:LANES, :]) + wz
        _init_flash(mw_sc.at[g], accw_sc.at[g])
        _online_update(sw_sc.at[g], pw_sc.at[g], vw, mw_sc.at[g], accw_sc.at[g])

    rows = -(-n_s // 8) * 8
    blk = _iota((rows, tq), 0)
    t_q = t0 + _iota((rows, tq), 1)
    cur = t_q // SLC_BLOCK
    forced = (blk == 0) | (blk == cur) | (blk == cur - 1)
    admissible = blk * SLC_BLOCK <= t_q
    blk8 = _iota((8, tq), 0)
    for g in range(G):
        q = qst[g, 0:LANES, :]
        kc = kc_ref[0, 0, g]
        n_ch = kc.shape[0]
        s_c = _dot(kc, q)
        cmp_end = _iota((n_ch, N), 0) * CMP_STRIDE + (CMP_BLOCK - 1)
        s_c = jnp.where(cmp_end <= q_time((n_ch, N)), s_c, NEG_INF)
        p_c = _softmax_keys(s_c)
        oc_sc[g] = _dot(vc_ref[0, 0, g], p_c.astype(BF16))[0:HEAD_DIM]
        psum = p_c[:, 0:tq]
        for j in range(1, J):
            psum = psum + p_c[:, j * tq:(j + 1) * tq]
        p_hi = psum.astype(BF16)
        p_lo = (psum - p_hi.astype(F32)).astype(BF16)
        imp = _dot(ov_ref[...], p_hi) + _dot(ov_ref[...], p_lo)
        sc = jnp.where(forced, jnp.inf, imp[0:rows])
        sc = jnp.where(admissible, sc, NEG_INF)
        groups = [sc[8 * r:8 * r + 8] for r in range(rows // 8)]
        ranks = [jnp.zeros((8, tq), F32) for _ in groups]
        for m in range(n_s):
            cm = sc[m:m + 1, :]
            for r, grp in enumerate(groups):
                gt = jnp.where(cm > grp, 1.0, 0.0)
                ge = jnp.where(cm >= grp, 1.0, 0.0)
                if 8 * r + 7 <= m:
                    first = gt
                elif 8 * r > m:
                    first = ge
                else:
                    first = jnp.where(blk8 + 8 * r > m, ge, gt)
                ranks[r] = ranks[r] + first
        rank = jnp.concatenate(ranks, axis=0)
        drop = jnp.where(rank < float(n_sel), 0.0, 1.0)
        if rows < LANES:
            drop = jnp.concatenate([drop, jnp.ones((LANES - rows, tq), F32)], axis=0)
        drop = drop.astype(BF16)
        qst[g, LANES:2 * LANES, :] = jnp.concatenate([drop] * J, axis=1)

    jd = t0 // tk
    off = (t0 - jd * tk) // tq
    n_off = tk // tq
    for g in range(G):
        _init_flash(m_sc.at[g], acc_sc.at[g])

        def put_scores(buf, j, g=g):
            k = ks_ref[0, g, pl.ds(pl.multiple_of(j * tk, tk), tk), :]
            lhs = jnp.concatenate([k, ex_ref[j]], axis=1)
            cz = cz_ref[jnp.where(j == jd, off, n_off)].astype(F32)
            buf[...] = _dot(lhs, qst[g]) + jnp.concatenate([cz] * J, axis=1)

        def update(buf, j, g=g):
            _online_update(buf, p_sc.at[g], vs_ref[0, g, j], m_sc.at[g], acc_sc.at[g])

        _flash_pipeline(jd + 1, put_scores, update, sa_sc.at[g], sb_sc.at[g])

    gates = jax.nn.sigmoid(misc_ref[0, 0])
    for g in range(G):
        def gate_row(c, g=g):
            rws = [gate_row0 + (g * J + j) * 3 + c for j in range(J)]
            return jnp.concatenate([gates[r:r + 1, :] for r in rws], axis=1)

        o_s = _normalized(acc_sc[g])
        o_w = _normalized(accw_sc[g])
        o = gate_row(0) * oc_sc[g] + (gate_row(1) * o_s + gate_row(2) * o_w)
        _store_head_pairs(o_ref, o, J, tq, col0=g * J * HEAD_DIM)


def _nsa_attention(sl, cmp_k, cmp_vt, lay):
    q_t, k, v_t, misc_t = sl["qT"], sl["k"], sl["vT"], sl["fT"]
    B, _, T, _ = k.shape
    G, J = B_KV_GROUPS, B_PER_GROUP
    tq, tk = NSA_Q_TILE, KV_TILE
    nk = T // tk
    n_ch = T // CMP_STRIDE
    n_c = n_ch - CMP_BLOCK // CMP_STRIDE + 1
    n_s = T // SLC_BLOCK
    n_sel = min(SLC_TOPN, n_s)
    assert n_s <= LANES
    c0 = np.arange(n_ch) * CMP_STRIDE
    s0 = np.arange(LANES) * SLC_BLOCK
    ov = ((c0[None, :] < s0[:, None] + SLC_BLOCK) & (c0[None, :] + CMP_BLOCK > s0[:, None]))
    ov = ov & (np.arange(n_ch)[None, :] < n_c) & (np.arange(LANES)[:, None] < n_s)
    ov = jnp.asarray(ov.astype(np.float32), BF16)
    pos = np.arange(T).reshape(nk, tk, 1)
    ex = (pos // SLC_BLOCK == np.arange(LANES).reshape(1, 1, LANES))
    ex = jnp.asarray(ex.astype(np.float32) * -BIG, BF16)
    n_off = tk // tq
    kp = np.arange(tk).reshape(1, tk, 1)
    tl = np.arange(tq).reshape(1, 1, tq) + np.arange(n_off + 1).reshape(n_off + 1, 1, 1) * tq
    cz = np.where((kp <= tl) | (np.arange(n_off + 1).reshape(-1, 1, 1) == n_off), 0.0, -BIG)
    cz = jnp.asarray(cz.astype(np.float32), BF16)
    wlen = min(WINDOW + tq, T)
    n_wt = wlen // tq
    assert wlen % tq == 0 and T >= wlen
    kp = np.arange(wlen).reshape(1, wlen, 1)
    tl = np.arange(tq).reshape(1, 1, tq)
    early = kp <= tl + np.arange(n_wt).reshape(n_wt, 1, 1) * tq
    late = (kp > tl) & (kp <= tl + WINDOW)
    band = np.where(np.arange(n_wt).reshape(n_wt, 1, 1) == n_wt - 1, late, early)
    wz = jnp.asarray(np.where(band, 0.0, -BIG).astype(np.float32), BF16)
    kern = functools.partial(_nsa_kernel, tq=tq, tk=tk, T=T, n_s=n_s, n_sel=n_sel,
                             gate_row0=lay["gate_row0"])
    N = J * tq
    for name in ("qB", "ks", "vs", "kw", "vw"):
        assert lay[name] % (2 * G if name == "qB" else G) == 0
    kslab = lambda off: pl.BlockSpec((1, G, T, LANES), lambda b, i: (b, off // G, 0, 0))
    return pl.pallas_call(
        kern,
        grid=(B, T // tq),
        in_specs=[
            pl.BlockSpec((1, 2 * G, LANES, tq), lambda b, i: (b, lay["qB"] // (2 * G), 0, i)),
            pl.BlockSpec((1, 1, G, n_ch, LANES), lambda b, i: (0, b, 0, 0, 0)),
            pl.BlockSpec((1, 1, G, LANES, n_ch), lambda b, i: (1, b, 0, 0, 0)),
            kslab(lay["ks"]),
            pl.BlockSpec((1, G, nk, LANES, tk), lambda b, i: (b, lay["vs"] // G, 0, 0, 0)),
            kslab(lay["kw"]),
            pl.BlockSpec((1, G, T // VW_TILE, LANES, VW_TILE),
                         lambda b, i: (b, lay["vw"] // G, 0, 0, 0)),
            pl.BlockSpec((1, 1, LANES, tq), lambda b, i: (b, lay["misc"], 0, i)),
            pl.BlockSpec((LANES, n_ch), lambda b, i: (0, 0)),
            pl.BlockSpec((nk, tk, LANES), lambda b, i: (0, 0, 0)),
            pl.BlockSpec((n_off + 1, tk, tq), lambda b, i: (0, 0, 0)),
            pl.BlockSpec((n_wt, wlen, tq), lambda b, i: (0, 0, 0)),
        ],
        out_specs=pl.BlockSpec((1, tq, B_HEADS * HEAD_DIM), lambda b, i: (b, i, 0)),
        out_shape=jax.ShapeDtypeStruct((B, T, B_HEADS * HEAD_DIM), BF16),
        scratch_shapes=[
            pltpu.VMEM((G, 2 * LANES, N), BF16),
            pltpu.VMEM((G, 1, N), F32),
            pltpu.VMEM((G, LANES, N), F32),
            pltpu.VMEM((G, tk, N), F32),
            pltpu.VMEM((G, tk, N), F32),
            pltpu.VMEM((G, tk, N), BF16),
            pltpu.VMEM((G, 1, N), F32),
            pltpu.VMEM((G, LANES, N), F32),
            pltpu.VMEM((G, wlen, N), F32),
            pltpu.VMEM((G, wlen, N), BF16),
            pltpu.VMEM((G, HEAD_DIM, N), F32),
        ],
        compiler_params=_params(("parallel", "arbitrary")),
        name="nsa_attention",
    )(q_t, cmp_k, cmp_vt, k, v_t, k, sl["vTw"], misc_t, ov, ex, cz, wz)


def _rms(x, g):
    ms = jnp.mean(x * x, axis=-1, keepdims=True)
    return (x * lax.rsqrt(ms + NORM_EPS)) * g


def _layer_tail_kernel(*refs, n_in, final_norm):
    h_ref = refs[0]
    o_refs = refs[1:1 + n_in]
    w_refs = refs[1 + n_in:1 + 2 * n_in]
    g_ref, wu_ref, wd_ref = refs[1 + 2 * n_in:4 + 2 * n_in]
    rest = refs[4 + 2 * n_in:]
    if final_norm:
        fg_ref, out_ref, xn_sc = rest
    else:
        out_ref, xn_sc = rest
    f = pl.program_id(1)

    @pl.when(f == 0)
    def _():
        mix = _dot(o_refs[0][...], w_refs[0][...])
        for i in range(1, n_in):
            mix = mix + _dot(o_refs[i][...], w_refs[i][...])
        x = h_ref[...] + mix
        xn_sc[...] = _rms(x, g_ref[...]).astype(BF16)
        out_ref[...] = x

    u = _dot(xn_sc[...], wu_ref[...])
    a = jnp.square(jnp.maximum(u, 0.0)).astype(BF16)
    out_ref[...] += _dot(a, wd_ref[...])

    if final_norm:
        @pl.when(f == pl.num_programs(1) - 1)
        def _():
            out_ref[...] = _rms(out_ref[...], fg_ref[...])


def _layer_tail(h2, outs, ws, g, w_up, w_down, final_g=None, tm=1024, tf=1024):
    N, D = h2.shape
    F = w_up.shape[1]
    tm = min(tm, N)
    n_in = len(outs)
    in_specs = [pl.BlockSpec((tm, D), lambda i, f: (i, 0))]
    in_specs += [pl.BlockSpec((tm, o.shape[1]), lambda i, f: (i, 0)) for o in outs]
    in_specs += [pl.BlockSpec(w.shape, lambda i, f: (0, 0)) for w in ws]
    in_specs += [
        pl.BlockSpec((1, D), lambda i, f: (0, 0)),
        pl.BlockSpec((D, tf), lambda i, f: (0, f)),
        pl.BlockSpec((tf, D), lambda i, f: (f, 0)),
    ]
    args = [h2, *outs, *ws, g.reshape(1, D), w_up, w_down]
    if final_g is not None:
        in_specs.append(pl.BlockSpec((1, D), lambda i, f: (0, 0)))
        args.append(final_g.reshape(1, D))
    kern = functools.partial(_layer_tail_kernel, n_in=n_in, final_norm=final_g is not None)
    return pl.pallas_call(
        kern,
        grid=(N // tm, F // tf),
        in_specs=in_specs,
        out_specs=pl.BlockSpec((tm, D), lambda i, f: (i, 0)),
        out_shape=jax.ShapeDtypeStruct((N, D), F32),
        scratch_shapes=[pltpu.VMEM((tm, D), BF16)],
        compiler_params=_params(("parallel", "arbitrary")),
        name="out_proj_mlp",
    )(*args)


def _even_layout():
    offs = {}
    o = 0
    for name, size in (("qa", 512), ("ka", 64), ("va", 64), ("qi", 256), ("ki", 64), ("wi", 4),
                       ("qb", 512), ("kvb", 768), ("gb", 24)):
        offs[name] = o
        o += size
    kvb = lambda which, g: offs["kvb"] + (which * B_KV_GROUPS + g) * HEAD_DIM
    cols, plan, lay = [], [], {}
    n = {k: 0 for k in _KINDS}

    def add(c, roped, kind, opt=None):
        cols.append(c)
        plan.append((roped, kind, n[kind], opt))
        n[kind] += 1
        return n[kind] - 1

    lay["qA"] = n["qT"]
    for p in range(4):
        add(_pair_cols(offs["qa"] + 2 * p * 64, offs["qa"] + (2 * p + 1) * 64), True, "qT", Q_SCALE)
    lay["qB"] = n["qT"]
    for p in range(4):
        add(_pair_cols(offs["qb"] + 2 * p * 64, offs["qb"] + (2 * p + 1) * 64), True, "qT", Q_SCALE)
    lay["qi"] = n["qT"]
    for p in range(2):
        add(_pair_cols(offs["qi"] + 2 * p * 64, offs["qi"] + (2 * p + 1) * 64), True, "qT")
    lay["kA"] = add(_pair_cols(offs["ka"], offs["ka"]), True, "k")
    lay["ki"] = add(_pair_cols(offs["ki"], offs["ki"]), True, "k")
    lay["ks"] = n["k"]
    for g in range(2):
        add(_pair_cols(kvb(2, g), kvb(2, g)), True, "k")
    lay["kw"] = n["k"]
    for g in range(2):
        add(_pair_cols(kvb(4, g), kvb(4, g)), True, "k")
    lay["kc"] = add(_pair_cols(kvb(0, 0), kvb(0, 1)), True, "f")
    lay["vc"] = add(_pair_cols(kvb(1, 0), kvb(1, 1)), False, "f")
    misc = np.full(LANES, -1)
    misc[0:IDX_HEADS] = offs["wi"] + np.arange(IDX_HEADS)
    lay["gate_row0"] = 8
    misc[8:8 + 24] = offs["gb"] + np.arange(24)
    lay["misc"] = add(misc, False, "fT")
    lay["vw"] = n["vTw"]
    for g in range(2):
        add(_head_cols(kvb(5, g)), False, "vTw")
    lay["vs"] = n["vT"]
    for g in range(2):
        add(_head_cols(kvb(3, g)), False, "vT", "ones_row_64")
    lay["vA"] = add(_head_cols(offs["va"]), False, "vT", "ones_row_64")
    return np.concatenate(cols), plan, lay


def _odd_layout():
    cols, plan = [], []
    for h in range(C_HEADS):
        cols.append(_pair_cols(h * 128, h * 128 + 64))
        plan.append((True, "qT", h, Q_SCALE))
    for h in range(C_HEADS):
        cols.append(_pair_cols(1024 + h * 128, 1024 + h * 128 + 64))
        plan.append((True, "k", h, None))
    for h in range(C_HEADS):
        cols.append(2048 + h * 128 + np.arange(LANES))
        plan.append((False, "vT", h, "ones_rows_below"))
    return np.concatenate(cols), plan


def _compress_weights(pe, w1, w2):
    d = _PAIR_D
    which = _PAIR_WHICH
    pe_l = pe[:, :, d]
    pe_l = pe_l.reshape(2, 2, 1, CMP_STRIDE * LANES)
    w1r = w1.astype(BF16).reshape(2, CMP_BLOCK, HEAD_DIM, CMP_HIDDEN)[:, :, d, :]
    per_g = []
    for g in range(B_KV_GROUPS):
        keep = jnp.asarray(which == g)[None, None, :, None]
        per_g.append(jnp.where(keep, w1r, jnp.zeros_like(w1r)))
    w1g = jnp.stack(per_g, axis=1)
    w1g = w1g.reshape(2, B_KV_GROUPS, 2, CMP_STRIDE * LANES, CMP_HIDDEN)
    w2b = w2.astype(BF16)
    w2k = w2b[0][:, d]
    w2v = w2b[1][:, np.arange(LANES) % HEAD_DIM]
    w2l = jnp.stack([w2k, w2v], axis=0)
    return pe_l, w1g, w2l


def _even_mixer(h, norm_g, w_in, cmp_pe, cmp_w1, cmp_w2, w_out, cos_slab, sin_slab):
    B, T, D = h.shape
    cols, plan, lay = _even_layout()
    w = _gather_cols(w_in.astype(BF16), cols)
    sl = _project(h, norm_g, w, cos_slab, sin_slab, plan)

    bias = _dsa_select(sl, lay)
    o_a = _dsa_attention(sl, bias, lay)

    n_ch = T // CMP_STRIDE
    assert (lay["kc"], lay["vc"]) == (0, 1) and sl["f"].shape[1] == 2
    flat = sl["f"].reshape(B, 2, n_ch, CMP_STRIDE * LANES)
    pe_l, w1g, w2l = _compress_weights(cmp_pe, cmp_w1, cmp_w2)
    cmp_k, cmp_vt = _compress(flat, pe_l, w1g, w2l)
    o_b = _nsa_attention(sl, cmp_k, cmp_vt, lay)

    na = A_HEADS * HEAD_DIM
    wo = w_out.astype(BF16)
    return [o_a.reshape(B * T, -1), o_b.reshape(B * T, -1)], [wo[:na], wo[na:]]


def _odd_mixer(h, norm_g, w_in, lam, subln_g, w_out, cos_slab, sin_slab, lambda_init):
    B, T, D = h.shape
    cols, plan = _odd_layout()
    w = _gather_cols(w_in.astype(BF16), cols)
    sl = _project(h, norm_g, w, cos_slab, sin_slab, plan)
    o = _diff_attention(sl, lam, subln_g, lambda_init)
    return [o.reshape(B * T, -1)], [w_out.astype(BF16)]


def kernel(x, mix_norm_g, mlp_norm_g, even_w_in, even_cmp_pe, even_cmp_w1, even_cmp_w2, even_w_out, odd_w_in, odd_lambda, odd_subln_g, odd_w_out, mlp_w_up, mlp_w_down, final_norm_g):
    B, T, D = x.shape
    depth = mix_norm_g.shape[0]
    assert depth >= 1
    cos_slab, sin_slab = _rope_slabs(T)
    h = x
    for layer in range(depth):
        if layer % 2 == 0:
            e = layer // 2
            outs, ws = _even_mixer(h, mix_norm_g[layer], even_w_in[e], even_cmp_pe[e],
                                   even_cmp_w1[e], even_cmp_w2[e], even_w_out[e], cos_slab, sin_slab)
        else:
            o = layer // 2
            lambda_init = 0.8 - 0.6 * math.exp(-0.3 * layer)
            outs, ws = _odd_mixer(h, mix_norm_g[layer], odd_w_in[o], odd_lambda[o], odd_subln_g[o],
                                  odd_w_out[o], cos_slab, sin_slab, lambda_init)
        h2 = _layer_tail(h.reshape(B * T, D), outs, ws, mlp_norm_g[layer],
                         mlp_w_up[layer].astype(BF16), mlp_w_down[layer].astype(BF16),
                         final_g=final_norm_g if layer == depth - 1 else None)
        h = h2.reshape(B, T, D)
    return h
```

```python
import functools
import math

import numpy as np
import jax
import jax.numpy as jnp
from jax import lax
from jax.experimental import pallas as pl
from jax.experimental.pallas import tpu as pltpu

HEAD_DIM = 64
HALF = HEAD_DIM // 2
LANES = 128
ROPE_THETA = 10000.0
NORM_EPS = 1e-6
SCALE = HEAD_DIM ** -0.5

A_HEADS = 8
IDX_HEADS = 4
DSA_TOPK = 256
B_HEADS = 8
B_KV_GROUPS = 2
B_PER_GROUP = B_HEADS // B_KV_GROUPS
CMP_BLOCK = 32
CMP_STRIDE = 16
CMP_HIDDEN = 256
SLC_BLOCK = 64
SLC_TOPN = 16
WINDOW = 512
C_HEADS = 8

KV_TILE = 512
SEL_Q_TILE = 128
DSA_Q_TILE = 256
NSA_Q_TILE = 256
VW_TILE = 128

LOG2E = math.log2(math.e)
Q_SCALE = SCALE * LOG2E
ONES_ROWS = 16
BIG = 2.0 ** 100

NEG_INF = float("-inf")
M_FLOOR = -1e30
INT_MIN = -(2 ** 31)

VMEM_LIMIT = 56 * 1024 * 1024

BF16 = jnp.bfloat16
F32 = jnp.float32


def _dot(a, b):
    return jnp.dot(a, b, preferred_element_type=F32)


def _params(sem):
    return pltpu.CompilerParams(dimension_semantics=sem, vmem_limit_bytes=VMEM_LIMIT)


def _iota(shape, axis):
    return lax.broadcasted_iota(jnp.int32, shape, axis)


def _pair_cols(base_a, base_b):
    lane = np.arange(LANES)
    half = lane // 64
    which = (lane % 64) // HALF
    i = lane % HALF
    base = np.where(which == 0, base_a, base_b)
    return base + half * HALF + i


def _head_cols(base):
    lane = np.arange(LANES)
    return np.where(lane < HEAD_DIM, base + lane, -1)


_PAIR_D = _pair_cols(0, 0)
_PAIR_WHICH = (np.arange(LANES) % 64) // HALF


def _gather_cols(w, cols):
    cols = np.asarray(cols)
    safe = np.where(cols >= 0, cols, 0)
    g = jnp.take(w, jnp.asarray(safe, dtype=jnp.int32), axis=1)
    return jnp.where(jnp.asarray(cols >= 0)[None, :], g, jnp.zeros_like(g))


def _rope_slabs(T):
    inv = 1.0 / (ROPE_THETA ** (jnp.arange(0, HEAD_DIM, 2, dtype=F32) / HEAD_DIM))
    ang = jnp.arange(T, dtype=F32)[:, None] * inv[None, :]
    cos, sin = jnp.cos(ang), jnp.sin(ang)
    cos_slab = jnp.tile(cos, (1, 4))
    sin_slab = jnp.concatenate([-sin, -sin, sin, sin], axis=1)
    return cos_slab, sin_slab


_KINDS = ("qT", "k", "vT", "vTw", "f", "fT")


def _proj_kernel(x_ref, g_ref, w_ref, cos_ref, sin_ref, *out_refs, plan, kinds, chunk):
    outs = dict(zip(kinds, out_refs))
    x = x_ref[0]
    ms = jnp.mean(x * x, axis=-1, keepdims=True)
    xn = ((x * lax.rsqrt(ms + NORM_EPS)) * g_ref[...]).astype(BF16)
    cos = cos_ref[...]
    sin = sin_ref[...]
    n = len(plan)
    for c0 in range(0, n, chunk):
        c1 = min(c0 + chunk, n)
        r = _dot(xn, w_ref[:, c0 * LANES:c1 * LANES])
        for s in range(c0, c1):
            roped, kind, idx, opt = plan[s]
            y = r[:, (s - c0) * LANES:(s - c0 + 1) * LANES]
            if roped:
                y = y * cos + pltpu.roll(y, 64, 1) * sin
            if kind == "qT":
                if opt is not None:
                    y = y * opt
                outs[kind][0, idx] = y.T.astype(BF16)
            elif kind == "k":
                outs[kind][0, idx] = y.astype(BF16)
            elif kind == "vT":
                if opt == "ones_row_64":
                    y = jnp.where(_iota(y.shape, 1) == HEAD_DIM, 1.0, y)
                    outs[kind][0, idx, 0] = y.T.astype(BF16)
                else:
                    outs[kind][0, idx, 0, 0:LANES, :] = y.T.astype(BF16)
                    outs[kind][0, idx, 0, LANES:, :] = jnp.ones((ONES_ROWS, y.shape[0]), BF16)
            elif kind == "vTw":
                y_t = jnp.where(_iota(y.shape, 1) == HEAD_DIM, 1.0, y).T.astype(BF16)
                for sub in range(y.shape[0] // VW_TILE):
                    outs[kind][0, idx, sub] = y_t[:, sub * VW_TILE:(sub + 1) * VW_TILE]
            elif kind == "f":
                outs[kind][0, idx] = y
            else:
                outs[kind][0, idx] = y.T


def _project(x, g, w, cos_slab, sin_slab, plan, chunk=4):
    B, T, D = x.shape
    tm = KV_TILE
    assert T % tm == 0
    count = {k: sum(1 for p in plan if p[1] == k) for k in _KINDS}
    kinds = tuple(k for k in _KINDS if count[k])
    out_specs, out_shape = [], []
    for k in kinds:
        n = count[k]
        if k in ("qT", "fT"):
            out_specs.append(pl.BlockSpec((1, n, LANES, tm), lambda b, i: (b, 0, 0, i)))
            out_shape.append(jax.ShapeDtypeStruct((B, n, LANES, T), BF16 if k == "qT" else F32))
        elif k in ("k", "f"):
            out_specs.append(pl.BlockSpec((1, n, tm, LANES), lambda b, i: (b, 0, i, 0)))
            out_shape.append(jax.ShapeDtypeStruct((B, n, T, LANES), BF16 if k == "k" else F32))
        elif k == "vTw":
            sub = tm // VW_TILE
            out_specs.append(pl.BlockSpec((1, n, sub, LANES, VW_TILE), lambda b, i: (b, 0, i, 0, 0)))
            out_shape.append(jax.ShapeDtypeStruct((B, n, T // VW_TILE, LANES, VW_TILE), BF16))
        else:
            wide = any(p[1] == "vT" and p[3] != "ones_row_64" for p in plan)
            rows = LANES + ONES_ROWS if wide else LANES
            out_specs.append(pl.BlockSpec((1, n, 1, rows, tm), lambda b, i: (b, 0, i, 0, 0)))
            out_shape.append(jax.ShapeDtypeStruct((B, n, T // tm, rows, tm), BF16))
    kern = functools.partial(_proj_kernel, plan=tuple(plan), kinds=kinds, chunk=chunk)
    outs = pl.pallas_call(
        kern,
        grid=(B, T // tm),
        in_specs=[
            pl.BlockSpec((1, tm, D), lambda b, i: (b, i, 0)),
            pl.BlockSpec((1, D), lambda b, i: (0, 0)),
            pl.BlockSpec((D, len(plan) * LANES), lambda b, i: (0, 0)),
            pl.BlockSpec((tm, LANES), lambda b, i: (i, 0)),
            pl.BlockSpec((tm, LANES), lambda b, i: (i, 0)),
        ],
        out_specs=out_specs,
        out_shape=out_shape,
        compiler_params=_params(("parallel", "parallel")),
        name="norm_proj_rope",
    )(x, g.reshape(1, D), w, cos_slab, sin_slab)
    return dict(zip(kinds, outs))


def _head_rows(slab_t, which):
    row = _iota(slab_t.shape, 0)
    keep = ((row % 64) // HALF) == which
    return jnp.where(keep, slab_t, jnp.zeros_like(slab_t))


def _tree(op, xs):
    while len(xs) > 1:
        xs = [op(xs[i], xs[i + 1]) if i + 1 < len(xs) else xs[i] for i in range(0, len(xs), 2)]
    return xs[0]


def _fold_rows_max(x, ways=4):
    rows, n = x.shape
    per = rows // ways
    parts = [jnp.max(x[i * per:(i + 1) * per].reshape(per // 8, 8, n), axis=0) for i in range(ways)]
    return _tree(jnp.maximum, parts)


ROW_BLOCK = 64


def _online_update(s_sc, p_sc, v_aug, m_ref, acc_ref):
    tk, n = s_sc.shape
    m_prev = m_ref[...]
    m_tile = jnp.max(_fold_rows_max(s_sc[...]), axis=0, keepdims=True)
    m_new = jnp.maximum(m_prev, m_tile)
    alpha = jnp.exp2(m_prev - m_new)
    for r in range(tk // ROW_BLOCK):
        rows = slice(r * ROW_BLOCK, (r + 1) * ROW_BLOCK)
        p_sc[rows, :] = jnp.exp2(s_sc[rows, :] - m_new).astype(BF16)
    acc_ref[...] = alpha * acc_ref[...] + _dot(v_aug, p_sc[...])
    m_ref[...] = m_new


def _flash_pipeline(n, put_scores, update, buf_a, buf_b, put_last=None):
    if put_last is None:
        put_scores(buf_a, 0)
    else:
        @pl.when(n > 0)
        def _():
            put_scores(buf_a, 0)

    def pair(p, carry):
        j = 2 * p
        put_scores(buf_b, j + 1)
        update(buf_a, j)

        @pl.when(j + 2 < n)
        def _():
            put_scores(buf_a, j + 2)
            update(buf_b, j + 1)

        return carry

    lax.fori_loop(0, n // 2, pair, 0)
    in_b = jnp.logical_and(n > 0, n % 2 == 0)
    in_a = n % 2 == 1

    def finish(cur, other):
        if put_last is not None:
            put_last(other, n)
        update(cur, n - 1)
        if put_last is not None:
            update(other, n)

    pl.when(in_b)(lambda: finish(buf_b, buf_a))
    pl.when(in_a)(lambda: finish(buf_a, buf_b))
    if put_last is not None:
        @pl.when(n == 0)
        def _():
            put_last(buf_a, 0)
            update(buf_a, 0)


def _softmax_keys(s):
    m = jnp.max(s, axis=0, keepdims=True)
    m = jnp.where(m > NEG_INF, m, 0.0)
    e = jnp.exp2(s - m)
    return e / jnp.maximum(jnp.sum(e, axis=0, keepdims=True), 1e-30)


def _init_flash(m_ref, acc_ref):
    m_ref[...] = jnp.full(m_ref.shape, M_FLOOR, F32)
    acc_ref[...] = jnp.zeros(acc_ref.shape, F32)


def _diff_attn_kernel(lam_ref, q_ref, k_ref, v_ref, g_ref, o_ref, m_sc, acc_sc, sa_sc, sb_sc,
                      p_sc, *, tq, lambda_init):
    qi = pl.program_id(2)
    q_t = q_ref[0, 0]
    qs = [_head_rows(q_t, c) for c in range(2)]
    for c in range(2):
        _init_flash(m_sc.at[c], acc_sc.at[c])

    def put_scores(buf, j, masked=False):
        k = k_ref[0, 0, pl.ds(pl.multiple_of(j * tq, tq), tq), :]
        for c in range(2):
            s = _dot(k, qs[c])
            if masked:
                s = jnp.where(_iota(s.shape, 0) <= _iota(s.shape, 1), s, NEG_INF)
            buf[c] = s

    def put_diagonal(buf, j):
        put_scores(buf, j, masked=True)

    def update(buf, j):
        v_aug = v_ref[0, 0, j]
        for c in range(2):
            _online_update(buf.at[c], p_sc.at[c], v_aug, m_sc.at[c], acc_sc.at[c])

    _flash_pipeline(qi, put_scores, update, sa_sc, sb_sc, put_last=put_diagonal)

    lam = lam_ref[...]
    s01 = jnp.sum(lam[0:1] * lam[1:2], axis=-1, keepdims=True)
    s23 = jnp.sum(lam[2:3] * lam[3:4], axis=-1, keepdims=True)
    lam_val = jnp.exp(s01) - jnp.exp(s23) + lambda_init
    o0 = acc_sc[0, 0:LANES] / jnp.maximum(acc_sc[0, LANES:LANES + 1], 1e-30)
    o1 = acc_sc[1, 0:LANES] / jnp.maximum(acc_sc[1, LANES:LANES + 1], 1e-30)
    o = o0 - lam_val * o1
    y = o * lax.rsqrt(jnp.mean(o * o, axis=0, keepdims=True) + NORM_EPS)
    y = (y * g_ref[...]) * (1.0 - lambda_init)
    o_ref[0] = y.T.astype(o_ref.dtype)


def _diff_attention(sl, lam, subln_g, lambda_init):
    q_t, k, v_t = sl["qT"], sl["k"], sl["vT"]
    B, H, T, _ = k.shape
    tq = KV_TILE
    nk = T // tq
    kern = functools.partial(_diff_attn_kernel, tq=tq, lambda_init=lambda_init)
    return pl.pallas_call(
        kern,
        grid=(B, H, T // tq),
        in_specs=[
            pl.BlockSpec((4, HEAD_DIM), lambda b, h, i: (0, 0)),
            pl.BlockSpec((1, 1, LANES, tq), lambda b, h, i: (b, h, 0, i)),
            pl.BlockSpec((1, 1, T, LANES), lambda b, h, i: (b, h, 0, 0)),
            pl.BlockSpec((1, 1, nk, LANES + ONES_ROWS, tq), lambda b, h, i: (b, h, 0, 0, 0)),
            pl.BlockSpec((LANES, 1), lambda b, h, i: (0, 0)),
        ],
        out_specs=pl.BlockSpec((1, tq, LANES), lambda b, h, i: (b, i, h)),
        out_shape=jax.ShapeDtypeStruct((B, T, H * LANES), BF16),
        scratch_shapes=[
            pltpu.VMEM((2, 1, tq), F32),
            pltpu.VMEM((2, LANES + ONES_ROWS, tq), F32),
            pltpu.VMEM((2, tq, tq), F32),
            pltpu.VMEM((2, tq, tq), F32),
            pltpu.VMEM((2, tq, tq), BF16),
        ],
        compiler_params=_params(("parallel", "parallel", "arbitrary")),
        name="diff_attention",
    )(lam, q_t, k, v_t, subln_g.reshape(LANES, 1))


def _dsa_select_kernel(qi_ref, ki_ref, misc_ref, tril_ref, bias_ref, key_sc, *, tq, ck, nk, topk):
    t0 = pl.program_id(1) * tq
    nvalid = (t0 + tq - 1) // ck + 1
    w = misc_ref[0, 0]
    qh = jnp.concatenate([_head_rows(qi_ref[0, h // 2], h % 2) for h in range(IDX_HEADS)], axis=1)
    t_q = t0 + _iota((ck, tq), 1)

    def causal(c):
        return (c * ck + _iota((ck, tq), 0)) <= t_q

    def fill(c):
        kk = ki_ref[0, 0, c * ck:(c + 1) * ck, :]
        r = _dot(kk, qh)
        score = jnp.zeros((ck, tq), F32)
        for h in range(IDX_HEADS):
            score = score + w[h:h + 1, :] * jnp.maximum(r[:, h * tq:(h + 1) * tq], 0.0)
        bits = pltpu.bitcast(score, jnp.int32)
        key = bits ^ ((bits >> 31) & jnp.int32(0x7FFFFFFF))
        key = jnp.where(score == 0.0, 0, key)
        key_sc[c] = jnp.where(causal(c), key, INT_MIN)

    kf = float(topk)

    def select_threshold(n):
        for c in range(n):
            fill(c)

        def count(pred):
            acc = jnp.zeros((ck // 8, tq), F32)
            for c in range(n):
                ind = pred(key_sc[c], c)
                acc = acc + jnp.sum(ind.reshape(8, ck // 8, tq), axis=0)
            return jnp.sum(acc, axis=0, keepdims=True)

        zero = jnp.zeros((1, tq), jnp.int32)
        ans = jnp.where(count(lambda kc, c: jnp.where(kc >= zero, 1.0, 0.0)) >= kf, 0, INT_MIN)

        def bit_step(i, ans):
            cand = ans | (jnp.int32(1) << (30 - i))
            cnt = count(lambda kc, c: jnp.where(kc >= cand, 1.0, 0.0))
            return jnp.where(cnt >= kf, cand, ans)

        ans = lax.fori_loop(0, 31, bit_step, ans)
        cnt_gt = count(lambda kc, c: jnp.where(kc > ans, 1.0, 0.0))
        cnt_eq = count(lambda kc, c: jnp.where(kc == ans, jnp.where(causal(c), 1.0, 0.0), 0.0))
        return ans, cnt_gt, cnt_eq

    ans, cnt_gt, cnt_eq = lax.switch(
        nvalid - 1, [functools.partial(select_threshold, n) for n in range(1, nk + 1)])
    need = kf - cnt_gt
    has_tie = jnp.max(cnt_eq - need) > 0.0

    @pl.when(jnp.logical_not(has_tie))
    def _():
        def emit(c, carry):
            sel = jnp.where(key_sc[c] >= ans, jnp.where(causal(c), 0.0, NEG_INF), NEG_INF)
            bias_ref[0, 0, c] = sel.astype(BF16)
            return carry
        lax.fori_loop(0, nvalid, emit, 0)

    @pl.when(has_tie)
    def _():
        def emit(c, carry):
            kc = key_sc[c]
            eq = jnp.where(kc == ans, jnp.where(causal(c), 1.0, 0.0), 0.0)
            pre = _dot(tril_ref[...], eq.astype(BF16)) + carry
            take = jnp.where(kc > ans, 1.0, jnp.where(pre < need, eq, 0.0))
            sel = jnp.where(take > 0.5, jnp.where(causal(c), 0.0, NEG_INF), NEG_INF)
            bias_ref[0, 0, c] = sel.astype(BF16)
            return carry + jnp.sum(eq, axis=0, keepdims=True)
        lax.fori_loop(0, nvalid, emit, jnp.zeros((1, tq), F32))

    def blank(c, carry):
        bias_ref[0, 0, c] = jnp.full((ck, tq), NEG_INF, BF16)
        return carry

    lax.fori_loop(nvalid, nk, blank, 0)


def _dsa_select(sl, lay):
    q_t, k, misc_t = sl["qT"], sl["k"], sl["fT"]
    B, _, T, _ = k.shape
    tq, ck = SEL_Q_TILE, KV_TILE
    nk = T // ck
    topk = min(DSA_TOPK, T // 4)
    tril = jnp.asarray(np.tril(np.ones((ck, ck), np.float32), -1), BF16)
    kern = functools.partial(_dsa_select_kernel, tq=tq, ck=ck, nk=nk, topk=topk)
    return pl.pallas_call(
        kern,
        grid=(B, T // tq),
        in_specs=[
            pl.BlockSpec((1, 2, LANES, tq), lambda b, i: (b, lay["qi"] // 2, 0, i)),
            pl.BlockSpec((1, 1, T, LANES), lambda b, i: (b, lay["ki"], 0, 0)),
            pl.BlockSpec((1, 1, LANES, tq), lambda b, i: (b, lay["misc"], 0, i)),
            pl.BlockSpec((ck, ck), lambda b, i: (0, 0)),
        ],
        out_specs=pl.BlockSpec((1, 1, nk, ck, tq), lambda b, i: (b, i, 0, 0, 0)),
        out_shape=jax.ShapeDtypeStruct((B, T // tq, nk, ck, tq), BF16),
        scratch_shapes=[pltpu.VMEM((nk, ck, tq), jnp.int32)],
        compiler_params=_params(("parallel", "parallel")),
        name="dsa_select",
    )(q_t, k, misc_t, tril)


def _store_head_pairs(o_ref, o, n_heads, tq, col0=0):
    for p in range(n_heads // 2):
        even = o[0:HEAD_DIM, (2 * p) * tq:(2 * p + 1) * tq]
        odd = o[0:HEAD_DIM, (2 * p + 1) * tq:(2 * p + 2) * tq]
        pair = jnp.concatenate([even, odd], axis=0)
        o_ref[0, :, col0 + p * LANES:col0 + (p + 1) * LANES] = pair.T.astype(o_ref.dtype)


def _normalized(acc):
    return acc[0:HEAD_DIM] / jnp.maximum(acc[HEAD_DIM:HEAD_DIM + 1], 1e-30)


def _dsa_attn_kernel(q_ref, k_ref, v_ref, bias_ref, o_ref, qst, m_sc, acc_sc, sa_sc, sb_sc,
                     p_sc, *, tq, tk):
    qi = pl.program_id(1)
    H = A_HEADS
    for h in range(H):
        qst[:, h * tq:(h + 1) * tq] = _head_rows(q_ref[0, h // 2], h % 2)
    _init_flash(m_sc, acc_sc)
    nkv = (qi * tq) // tk + 1

    def put_scores(buf, j):
        k = k_ref[0, 0, pl.ds(pl.multiple_of(j * tk, tk), tk), :]
        parts = [bias_ref[0, s, j].astype(F32) for s in range(bias_ref.shape[1])]
        b = parts[0] if len(parts) == 1 else jnp.concatenate(parts, axis=1)
        buf[...] = _dot(k, qst[...]) + jnp.concatenate([b] * H, axis=1)

    def update(buf, j):
        _online_update(buf, p_sc, v_ref[0, 0, j], m_sc, acc_sc)

    _flash_pipeline(nkv, put_scores, update, sa_sc, sb_sc)
    _store_head_pairs(o_ref, _normalized(acc_sc[...]), H, tq)


def _dsa_attention(sl, bias, lay):
    q_t, k, v_t = sl["qT"], sl["k"], sl["vT"]
    B, _, T, _ = k.shape
    tq, tk = DSA_Q_TILE, KV_TILE
    nk = T // tk
    H = A_HEADS
    N = H * tq
    sel_tq = bias.shape[4]
    nsub = tq // sel_tq
    kern = functools.partial(_dsa_attn_kernel, tq=tq, tk=tk)
    return pl.pallas_call(
        kern,
        grid=(B, T // tq),
        in_specs=[
            pl.BlockSpec((1, H // 2, LANES, tq), lambda b, i: (b, lay["qA"] // (H // 2), 0, i)),
            pl.BlockSpec((1, 1, T, LANES), lambda b, i: (b, lay["kA"], 0, 0)),
            pl.BlockSpec((1, 1, nk, LANES, tk), lambda b, i: (b, lay["vA"], 0, 0, 0)),
            pl.BlockSpec((1, nsub, nk, tk, sel_tq), lambda b, i: (b, i, 0, 0, 0)),
        ],
        out_specs=pl.BlockSpec((1, tq, H * HEAD_DIM), lambda b, i: (b, i, 0)),
        out_shape=jax.ShapeDtypeStruct((B, T, H * HEAD_DIM), BF16),
        scratch_shapes=[
            pltpu.VMEM((LANES, N), BF16),
            pltpu.VMEM((1, N), F32),
            pltpu.VMEM((LANES, N), F32),
            pltpu.VMEM((tk, N), F32),
            pltpu.VMEM((tk, N), F32),
            pltpu.VMEM((tk, N), BF16),
        ],
        compiler_params=_params(("parallel", "arbitrary")),
        name="dsa_attention",
    )(q_t, k, v_t, bias)


def _compress_kernel(x_ref, pe_ref, w1_ref, w2_ref, o_ref, ot_ref, *, n_ch):
    x = x_ref[0, 0]
    xt = (x + pe_ref[0, 0]).astype(BF16)
    xb = (x + pe_ref[0, 1]).astype(BF16)
    for g in range(B_KV_GROUPS):
        a = _dot(xt, w1_ref[0, g, 0])
        b = _dot(xb, w1_ref[0, g, 1])
        h = jax.nn.gelu(a + pltpu.roll(b, n_ch - 1, 0))
        r = _dot(h.astype(BF16), w2_ref[0])
        o_ref[0, 0, g] = r.astype(BF16)
        ot_ref[0, 0, g] = r.T.astype(BF16)


def _compress(flat, pe, w1, w2):
    B, _, n_ch, W = flat.shape
    G = B_KV_GROUPS
    kern = functools.partial(_compress_kernel, n_ch=n_ch)
    return pl.pallas_call(
        kern,
        grid=(2, B),
        in_specs=[
            pl.BlockSpec((1, 1, n_ch, W), lambda s, b: (b, s, 0, 0)),
            pl.BlockSpec((1, 2, 1, W), lambda s, b: (s, 0, 0, 0)),
            pl.BlockSpec((1, G, 2, W, CMP_HIDDEN), lambda s, b: (s, 0, 0, 0, 0)),
            pl.BlockSpec((1, CMP_HIDDEN, LANES), lambda s, b: (s, 0, 0)),
        ],
        out_specs=[
            pl.BlockSpec((1, 1, G, n_ch, LANES), lambda s, b: (s, b, 0, 0, 0)),
            pl.BlockSpec((1, 1, G, LANES, n_ch), lambda s, b: (s, b, 0, 0, 0)),
        ],
        out_shape=[
            jax.ShapeDtypeStruct((2, B, G, n_ch, LANES), BF16),
            jax.ShapeDtypeStruct((2, B, G, LANES, n_ch), BF16),
        ],
        compiler_params=_params(("parallel", "parallel")),
        name="nsa_compress",
    )(flat, pe, w1, w2)


def _nsa_kernel(q_ref, kc_ref, vc_ref, ks_ref, vs_ref, kw_ref, vw_ref, misc_ref, ov_ref, ex_ref,
                cz_ref, wz_ref, o_ref, qst, m_sc, acc_sc, sa_sc, sb_sc, p_sc, mw_sc, accw_sc,
                sw_sc, pw_sc, oc_sc, *, tq, tk, T, n_s, n_sel, gate_row0):
    qi = pl.program_id(1)
    t0 = qi * tq
    G, J = B_KV_GROUPS, B_PER_GROUP
    N = J * tq
    for g in range(G):
        for j in range(J):
            qst[g, 0:LANES, j * tq:(j + 1) * tq] = _head_rows(q_ref[0, 2 * g + j // 2], j % 2)

    def q_time(shape):
        return t0 + (_iota(shape, 1) % tq)

    wlen = min(WINDOW + tq, T)
    wstart = pl.multiple_of(jnp.maximum(t0 - WINDOW, 0), tq)
    n_wt = wlen // tq
    wt = wstart // VW_TILE
    wz = wz_ref[jnp.minimum(qi, n_wt - 1)].astype(F32)
    wz = jnp.concatenate([wz] * J, axis=1)
    for g in range(G):
        kw = kw_ref[0, g, pl.ds(wstart, wlen), :]
        vw = jnp.concatenate([vw_ref[0, g, wt + i] for i in range(wlen // VW_TILE)], axis=1)
        sw_sc[g] = _dot(kw, qst[g, 0:LANES, :]) + wz
        _init_flash(mw_sc.at[g], accw_sc.at[g])
        _online_update(sw_sc.at[g], pw_sc.at[g], vw, mw_sc.at[g], accw_sc.at[g])

    rows = -(-n_s // 8) * 8
    blk = _iota((rows, tq), 0)
    t_q = t0 + _iota((rows, tq), 1)
    cur = t_q // SLC_BLOCK
    forced = (blk == 0) | (blk == cur) | (blk == cur - 1)
    admissible = blk * SLC_BLOCK <= t_q
    blk8 = _iota((8, tq), 0)
    for g in range(G):
        q = qst[g, 0:LANES, :]
        kc = kc_ref[0, 0, g]
        n_ch = kc.shape[0]
        s_c = _dot(kc, q)
        cmp_end = _iota((n_ch, N), 0) * CMP_STRIDE + (CMP_BLOCK - 1)
        s_c = jnp.where(cmp_end <= q_time((n_ch, N)), s_c, NEG_INF)
        p_c = _softmax_keys(s_c)
        oc_sc[g] = _dot(vc_ref[0, 0, g], p_c.astype(BF16))[0:HEAD_DIM]
        psum = p_c[:, 0:tq]
        for j in range(1, J):
            psum = psum + p_c[:, j * tq:(j + 1) * tq]
        p_hi = psum.astype(BF16)
        p_lo = (psum - p_hi.astype(F32)).astype(BF16)
        imp = _dot(ov_ref[...], p_hi) + _dot(ov_ref[...], p_lo)
        sc = jnp.where(forced, jnp.inf, imp[0:rows])
        sc = jnp.where(admissible, sc, NEG_INF)
        groups = [sc[8 * r:8 * r + 8] for r in range(rows // 8)]
        ranks = [jnp.zeros((8, tq), F32) for _ in groups]
        for m in range(n_s):
            cm = sc[m:m + 1, :]
            for r, grp in enumerate(groups):
                gt = jnp.where(cm > grp, 1.0, 0.0)
                ge = jnp.where(cm >= grp, 1.0, 0.0)
                if 8 * r + 7 <= m:
                    first = gt
                elif 8 * r > m:
                    first = ge
                else:
                    first = jnp.where(blk8 + 8 * r > m, ge, gt)
                ranks[r] = ranks[r] + first
        rank = jnp.concatenate(ranks, axis=0)
        drop = jnp.where(rank < float(n_sel), 0.0, 1.0)
        if rows < LANES:
            drop = jnp.concatenate([drop, jnp.ones((LANES - rows, tq), F32)], axis=0)
        drop = drop.astype(BF16)
        qst[g, LANES:2 * LANES, :] = jnp.concatenate([drop] * J, axis=1)

    jd = t0 // tk
    off = (t0 - jd * tk) // tq
    n_off = tk // tq
    for g in range(G):
        _init_flash(m_sc.at[g], acc_sc.at[g])

        def put_scores(buf, j, g=g):
            k = ks_ref[0, g, pl.ds(pl.multiple_of(j * tk, tk), tk), :]
            lhs = jnp.concatenate([k, ex_ref[j]], axis=1)
            cz = cz_ref[jnp.where(j == jd, off, n_off)].astype(F32)
            buf[...] = _dot(lhs, qst[g]) + jnp.concatenate([cz] * J, axis=1)

        def update(buf, j, g=g):
            _online_update(buf, p_sc.at[g], vs_ref[0, g, j], m_sc.at[g], acc_sc.at[g])

        _flash_pipeline(jd + 1, put_scores, update, sa_sc.at[g], sb_sc.at[g])

    gates = jax.nn.sigmoid(misc_ref[0, 0])
    for g in range(G):
        def gate_row(c, g=g):
            rws = [gate_row0 + (g * J + j) * 3 + c for j in range(J)]
            return jnp.concatenate([gates[r:r + 1, :] for r in rws], axis=1)

        o_s = _normalized(acc_sc[g])
        o_w = _normalized(accw_sc[g])
        o = gate_row(0) * oc_sc[g] + (gate_row(1) * o_s + gate_row(2) * o_w)
        _store_head_pairs(o_ref, o, J, tq, col0=g * J * HEAD_DIM)


def _nsa_attention(sl, cmp_k, cmp_vt, lay):
    q_t, k, v_t, misc_t = sl["qT"], sl["k"], sl["vT"], sl["fT"]
    B, _, T, _ = k.shape
    G, J = B_KV_GROUPS, B_PER_GROUP
    tq, tk = NSA_Q_TILE, KV_TILE
    nk = T // tk
    n_ch = T // CMP_STRIDE
    n_c = n_ch - CMP_BLOCK // CMP_STRIDE + 1
    n_s = T // SLC_BLOCK
    n_sel = min(SLC_TOPN, n_s)
    assert n_s <= LANES
    c0 = np.arange(n_ch) * CMP_STRIDE
    s0 = np.arange(LANES) * SLC_BLOCK
    ov = ((c0[None, :] < s0[:, None] + SLC_BLOCK) & (c0[None, :] + CMP_BLOCK > s0[:, None]))
    ov = ov & (np.arange(n_ch)[None, :] < n_c) & (np.arange(LANES)[:, None] < n_s)
    ov = jnp.asarray(ov.astype(np.float32), BF16)
    pos = np.arange(T).reshape(nk, tk, 1)
    ex = (pos // SLC_BLOCK == np.arange(LANES).reshape(1, 1, LANES))
    ex = jnp.asarray(ex.astype(np.float32) * -BIG, BF16)
    n_off = tk // tq
    kp = np.arange(tk).reshape(1, tk, 1)
    tl = np.arange(tq).reshape(1, 1, tq) + np.arange(n_off + 1).reshape(n_off + 1, 1, 1) * tq
    cz = np.where((kp <= tl) | (np.arange(n_off + 1).reshape(-1, 1, 1) == n_off), 0.0, -BIG)
    cz = jnp.asarray(cz.astype(np.float32), BF16)
    wlen = min(WINDOW + tq, T)
    n_wt = wlen // tq
    assert wlen % tq == 0 and T >= wlen
    kp = np.arange(wlen).reshape(1, wlen, 1)
    tl = np.arange(tq).reshape(1, 1, tq)
    early = kp <= tl + np.arange(n_wt).reshape(n_wt, 1, 1) * tq
    late = (kp > tl) & (kp <= tl + WINDOW)
    band = np.where(np.arange(n_wt).reshape(n_wt, 1, 1) == n_wt - 1, late, early)
    wz = jnp.asarray(np.where(band, 0.0, -BIG).astype(np.float32), BF16)
    kern = functools.partial(_nsa_kernel, tq=tq, tk=tk, T=T, n_s=n_s, n_sel=n_sel,
                             gate_row0=lay["gate_row0"])
    N = J * tq
    for name in ("qB", "ks", "vs", "kw", "vw"):
        assert lay[name] % (2 * G if name == "qB" else G) == 0
    kslab = lambda off: pl.BlockSpec((1, G, T, LANES), lambda b, i: (b, off // G, 0, 0))
    return pl.pallas_call(
        kern,
        grid=(B, T // tq),
        in_specs=[
            pl.BlockSpec((1, 2 * G, LANES, tq), lambda b, i: (b, lay["qB"] // (2 * G), 0, i)),
            pl.BlockSpec((1, 1, G, n_ch, LANES), lambda b, i: (0, b, 0, 0, 0)),
            pl.BlockSpec((1, 1, G, LANES, n_ch), lambda b, i: (1, b, 0, 0, 0)),
            kslab(lay["ks"]),
            pl.BlockSpec((1, G, nk, LANES, tk), lambda b, i: (b, lay["vs"] // G, 0, 0, 0)),
            kslab(lay["kw"]),
            pl.BlockSpec((1, G, T // VW_TILE, LANES, VW_TILE),
                         lambda b, i: (b, lay["vw"] // G, 0, 0, 0)),
            pl.BlockSpec((1, 1, LANES, tq), lambda b, i: (b, lay["misc"], 0, i)),
            pl.BlockSpec((LANES, n_ch), lambda b, i: (0, 0)),
            pl.BlockSpec((nk, tk, LANES), lambda b, i: (0, 0, 0)),
            pl.BlockSpec((n_off + 1, tk, tq), lambda b, i: (0, 0, 0)),
            pl.BlockSpec((n_wt, wlen, tq), lambda b, i: (0, 0, 0)),
        ],
        out_specs=pl.BlockSpec((1, tq, B_HEADS * HEAD_DIM), lambda b, i: (b, i, 0)),
        out_shape=jax.ShapeDtypeStruct((B, T, B_HEADS * HEAD_DIM), BF16),
        scratch_shapes=[
            pltpu.VMEM((G, 2 * LANES, N), BF16),
            pltpu.VMEM((G, 1, N), F32),
            pltpu.VMEM((G, LANES, N), F32),
            pltpu.VMEM((G, tk, N), F32),
            pltpu.VMEM((G, tk, N), F32),
            pltpu.VMEM((G, tk, N), BF16),
            pltpu.VMEM((G, 1, N), F32),
            pltpu.VMEM((G, LANES, N), F32),
            pltpu.VMEM((G, wlen, N), F32),
            pltpu.VMEM((G, wlen, N), BF16),
            pltpu.VMEM((G, HEAD_DIM, N), F32),
        ],
        compiler_params=_params(("parallel", "arbitrary")),
        name="nsa_attention",
    )(q_t, cmp_k, cmp_vt, k, v_t, k, sl["vTw"], misc_t, ov, ex, cz, wz)


def _rms(x, g):
    ms = jnp.mean(x * x, axis=-1, keepdims=True)
    return (x * lax.rsqrt(ms + NORM_EPS)) * g


def _layer_tail_kernel(*refs, n_in, final_norm):
    h_ref = refs[0]
    o_refs = refs[1:1 + n_in]
    w_refs = refs[1 + n_in:1 + 2 * n_in]
    g_ref, wu_ref, wd_ref = refs[1 + 2 * n_in:4 + 2 * n_in]
    rest = refs[4 + 2 * n_in:]
    if final_norm:
        fg_ref, out_ref, xn_sc = rest
    else:
        out_ref, xn_sc = rest
    f = pl.program_id(1)

    @pl.when(f == 0)
    def _():
        mix = _dot(o_refs[0][...], w_refs[0][...])
        for i in range(1, n_in):
            mix = mix + _dot(o_refs[i][...], w_refs[i][...])
        x = h_ref[...] + mix
        xn_sc[...] = _rms(x, g_ref[...]).astype(BF16)
        out_ref[...] = x

    u = _dot(xn_sc[...], wu_ref[...])
    a = jnp.square(jnp.maximum(u, 0.0)).astype(BF16)
    out_ref[...] += _dot(a, wd_ref[...])

    if final_norm:
        @pl.when(f == pl.num_programs(1) - 1)
        def _():
            out_ref[...] = _rms(out_ref[...], fg_ref[...])


def _layer_tail(h2, outs, ws, g, w_up, w_down, final_g=None, tm=1024, tf=1024):
    N, D = h2.shape
    F = w_up.shape[1]
    tm = min(tm, N)
    n_in = len(outs)
    in_specs = [pl.BlockSpec((tm, D), lambda i, f: (i, 0))]
    in_specs += [pl.BlockSpec((tm, o.shape[1]), lambda i, f: (i, 0)) for o in outs]
    in_specs += [pl.BlockSpec(w.shape, lambda i, f: (0, 0)) for w in ws]
    in_specs += [
        pl.BlockSpec((1, D), lambda i, f: (0, 0)),
        pl.BlockSpec((D, tf), lambda i, f: (0, f)),
        pl.BlockSpec((tf, D), lambda i, f: (f, 0)),
    ]
    args = [h2, *outs, *ws, g.reshape(1, D), w_up, w_down]
    if final_g is not None:
        in_specs.append(pl.BlockSpec((1, D), lambda i, f: (0, 0)))
        args.append(final_g.reshape(1, D))
    kern = functools.partial(_layer_tail_kernel, n_in=n_in, final_norm=final_g is not None)
    return pl.pallas_call(
        kern,
        grid=(N // tm, F // tf),
        in_specs=in_specs,
        out_specs=pl.BlockSpec((tm, D), lambda i, f: (i, 0)),
        out_shape=jax.ShapeDtypeStruct((N, D), F32),
        scratch_shapes=[pltpu.VMEM((tm, D), BF16)],
        compiler_params=_params(("parallel", "arbitrary")),
        name="out_proj_mlp",
    )(*args)


def _even_layout():
    offs = {}
    o = 0
    for name, size in (("qa", 512), ("ka", 64), ("va", 64), ("qi", 256), ("ki", 64), ("wi", 4),
                       ("qb", 512), ("kvb", 768), ("gb", 24)):
        offs[name] = o
        o += size
    kvb = lambda which, g: offs["kvb"] + (which * B_KV_GROUPS + g) * HEAD_DIM
    cols, plan, lay = [], [], {}
    n = {k: 0 for k in _KINDS}

    def add(c, roped, kind, opt=None):
        cols.append(c)
        plan.append((roped, kind, n[kind], opt))
        n[kind] += 1
        return n[kind] - 1

    lay["qA"] = n["qT"]
    for p in range(4):
        add(_pair_cols(offs["qa"] + 2 * p * 64, offs["qa"] + (2 * p + 1) * 64), True, "qT", Q_SCALE)
    lay["qB"] = n["qT"]
    for p in range(4):
        add(_pair_cols(offs["qb"] + 2 * p * 64, offs["qb"] + (2 * p + 1) * 64), True, "qT", Q_SCALE)
    lay["qi"] = n["qT"]
    for p in range(2):
        add(_pair_cols(offs["qi"] + 2 * p * 64, offs["qi"] + (2 * p + 1) * 64), True, "qT")
    lay["kA"] = add(_pair_cols(offs["ka"], offs["ka"]), True, "k")
    lay["ki"] = add(_pair_cols(offs["ki"], offs["ki"]), True, "k")
    lay["ks"] = n["k"]
    for g in range(2):
        add(_pair_cols(kvb(2, g), kvb(2, g)), True, "k")
    lay["kw"] = n["k"]
    for g in range(2):
        add(_pair_cols(kvb(4, g), kvb(4, g)), True, "k")
    lay["kc"] = add(_pair_cols(kvb(0, 0), kvb(0, 1)), True, "f")
    lay["vc"] = add(_pair_cols(kvb(1, 0), kvb(1, 1)), False, "f")
    misc = np.full(LANES, -1)
    misc[0:IDX_HEADS] = offs["wi"] + np.arange(IDX_HEADS)
    lay["gate_row0"] = 8
    misc[8:8 + 24] = offs["gb"] + np.arange(24)
    lay["misc"] = add(misc, False, "fT")
    lay["vw"] = n["vTw"]
    for g in range(2):
        add(_head_cols(kvb(5, g)), False, "vTw")
    lay["vs"] = n["vT"]
    for g in range(2):
        add(_head_cols(kvb(3, g)), False, "vT", "ones_row_64")
    lay["vA"] = add(_head_cols(offs["va"]), False, "vT", "ones_row_64")
    return np.concatenate(cols), plan, lay


def _odd_layout():
    cols, plan = [], []
    for h in range(C_HEADS):
        cols.append(_pair_cols(h * 128, h * 128 + 64))
        plan.append((True, "qT", h, Q_SCALE))
    for h in range(C_HEADS):
        cols.append(_pair_cols(1024 + h * 128, 1024 + h * 128 + 64))
        plan.append((True, "k", h, None))
    for h in range(C_HEADS):
        cols.append(2048 + h * 128 + np.arange(LANES))
        plan.append((False, "vT", h, "ones_rows_below"))
    return np.concatenate(cols), plan


def _compress_weights(pe, w1, w2):
    d = _PAIR_D
    which = _PAIR_WHICH
    pe_l = pe[:, :, d]
    pe_l = pe_l.reshape(2, 2, 1, CMP_STRIDE * LANES)
    w1r = w1.astype(BF16).reshape(2, CMP_BLOCK, HEAD_DIM, CMP_HIDDEN)[:, :, d, :]
    per_g = []
    for g in range(B_KV_GROUPS):
        keep = jnp.asarray(which == g)[None, None, :, None]
        per_g.append(jnp.where(keep, w1r, jnp.zeros_like(w1r)))
    w1g = jnp.stack(per_g, axis=1)
    w1g = w1g.reshape(2, B_KV_GROUPS, 2, CMP_STRIDE * LANES, CMP_HIDDEN)
    w2b = w2.astype(BF16)
    w2k = w2b[0][:, d]
    w2v = w2b[1][:, np.arange(LANES) % HEAD_DIM]
    w2l = jnp.stack([w2k, w2v], axis=0)
    return pe_l, w1g, w2l


def _even_mixer(h, norm_g, w_in, cmp_pe, cmp_w1, cmp_w2, w_out, cos_slab, sin_slab):
    B, T, D = h.shape
    cols, plan, lay = _even_layout()
    w = _gather_cols(w_in.astype(BF16), cols)
    sl = _project(h, norm_g, w, cos_slab, sin_slab, plan)

    bias = _dsa_select(sl, lay)
    o_a = _dsa_attention(sl, bias, lay)

    n_ch = T // CMP_STRIDE
    assert (lay["kc"], lay["vc"]) == (0, 1) and sl["f"].shape[1] == 2
    flat = sl["f"].reshape(B, 2, n_ch, CMP_STRIDE * LANES)
    pe_l, w1g, w2l = _compress_weights(cmp_pe, cmp_w1, cmp_w2)
    cmp_k, cmp_vt = _compress(flat, pe_l, w1g, w2l)
    o_b = _nsa_attention(sl, cmp_k, cmp_vt, lay)

    na = A_HEADS * HEAD_DIM
    wo = w_out.astype(BF16)
    return [o_a.reshape(B * T, -1), o_b.reshape(B * T, -1)], [wo[:na], wo[na:]]


def _odd_mixer(h, norm_g, w_in, lam, subln_g, w_out, cos_slab, sin_slab, lambda_init):
    B, T, D = h.shape
    cols, plan = _odd_layout()
    w = _gather_cols(w_in.astype(BF16), cols)
    sl = _project(h, norm_g, w, cos_slab, sin_slab, plan)
    o = _diff_attention(sl, lam, subln_g, lambda_init)
    return [o.reshape(B * T, -1)], [w_out.astype(BF16)]


def kernel(x, mix_norm_g, mlp_norm_g, even_w_in, even_cmp_pe, even_cmp_w1, even_cmp_w2, even_w_out, odd_w_in, odd_lambda, odd_subln_g, odd_w_out, mlp_w_up, mlp_w_down, final_norm_g):
    B, T, D = x.shape
    depth = mix_norm_g.shape[0]
    assert depth >= 1
    cos_slab, sin_slab = _rope_slabs(T)
    h = x
    for layer in range(depth):
        if layer % 2 == 0:
            e = layer // 2
            outs, ws = _even_mixer(h, mix_norm_g[layer], even_w_in[e], even_cmp_pe[e],
                                   even_cmp_w1[e], even_cmp_w2[e], even_w_out[e], cos_slab, sin_slab)
        else:
            o = layer // 2
            lambda_init = 0.8 - 0.6 * math.exp(-0.3 * layer)
            outs, ws = _odd_mixer(h, mix_norm_g[layer], odd_w_in[o], odd_lambda[o], odd_subln_g[o],
                                  odd_w_out[o], cos_slab, sin_slab, lambda_init)
        h2 = _layer_tail(h.reshape(B * T, D), outs, ws, mlp_norm_g[layer],
                         mlp_w_up[layer].astype(BF16), mlp_w_down[layer].astype(BF16),
                         final_g=final_norm_g if layer == depth - 1 else None)
        h = h2.reshape(B, T, D)
    return h
```

```python
import functools
import math

import numpy as np
import jax
import jax.numpy as jnp
from jax import lax
from jax.experimental import pallas as pl
from jax.experimental.pallas import tpu as pltpu

HEAD_DIM = 64
HALF = HEAD_DIM // 2
LANES = 128
ROPE_THETA = 10000.0
NORM_EPS = 1e-6
SCALE = HEAD_DIM ** -0.5

A_HEADS = 8
IDX_HEADS = 4
DSA_TOPK = 256
B_HEADS = 8
B_KV_GROUPS = 2
B_PER_GROUP = B_HEADS // B_KV_GROUPS
CMP_BLOCK = 32
CMP_STRIDE = 16
CMP_HIDDEN = 256
SLC_BLOCK = 64
SLC_TOPN = 16
WINDOW = 512
C_HEADS = 8

KV_TILE = 512
SEL_Q_TILE = 128
DSA_Q_TILE = 256
NSA_Q_TILE = 256
VW_TILE = 128

LOG2E = math.log2(math.e)
Q_SCALE = SCALE * LOG2E
ONES_ROWS = 16
BIG = 2.0 ** 100

NEG_INF = float("-inf")
M_FLOOR = -1e30
INT_MIN = -(2 ** 31)

VMEM_LIMIT = 56 * 1024 * 1024

BF16 = jnp.bfloat16
F32 = jnp.float32


def _dot(a, b):
    return jnp.dot(a, b, preferred_element_type=F32)


def _params(sem):
    return pltpu.CompilerParams(dimension_semantics=sem, vmem_limit_bytes=VMEM_LIMIT)


def _iota(shape, axis):
    return lax.broadcasted_iota(jnp.int32, shape, axis)


def _pair_cols(base_a, base_b):
    lane = np.arange(LANES)
    half = lane // 64
    which = (lane % 64) // HALF
    i = lane % HALF
    base = np.where(which == 0, base_a, base_b)
    return base + half * HALF + i


def _head_cols(base):
    lane = np.arange(LANES)
    return np.where(lane < HEAD_DIM, base + lane, -1)


_PAIR_D = _pair_cols(0, 0)
_PAIR_WHICH = (np.arange(LANES) % 64) // HALF


def _gather_cols(w, cols):
    cols = np.asarray(cols)
    safe = np.where(cols >= 0, cols, 0)
    g = jnp.take(w, jnp.asarray(safe, dtype=jnp.int32), axis=1)
    return jnp.where(jnp.asarray(cols >= 0)[None, :], g, jnp.zeros_like(g))


def _rope_slabs(T):
    inv = 1.0 / (ROPE_THETA ** (jnp.arange(0, HEAD_DIM, 2, dtype=F32) / HEAD_DIM))
    ang = jnp.arange(T, dtype=F32)[:, None] * inv[None, :]
    cos, sin = jnp.cos(ang), jnp.sin(ang)
    cos_slab = jnp.tile(cos, (1, 4))
    sin_slab = jnp.concatenate([-sin, -sin, sin, sin], axis=1)
    return cos_slab, sin_slab


_KINDS = ("qT", "k", "vT", "vTw", "f", "fT")


def _proj_kernel(x_ref, g_ref, w_ref, cos_ref, sin_ref, *out_refs, plan, kinds, chunk):
    outs = dict(zip(kinds, out_refs))
    x = x_ref[0]
    ms = jnp.mean(x * x, axis=-1, keepdims=True)
    xn = ((x * lax.rsqrt(ms + NORM_EPS)) * g_ref[...]).astype(BF16)
    cos = cos_ref[...]
    sin = sin_ref[...]
    n = len(plan)
    for c0 in range(0, n, chunk):
        c1 = min(c0 + chunk, n)
        r = _dot(xn, w_ref[:, c0 * LANES:c1 * LANES])
        for s in range(c0, c1):
            roped, kind, idx, opt = plan[s]
            y = r[:, (s - c0) * LANES:(s - c0 + 1) * LANES]
            if roped:
                y = y * cos + pltpu.roll(y, 64, 1) * sin
            if kind == "qT":
                if opt is not None:
                    y = y * opt
                outs[kind][0, idx] = y.T.astype(BF16)
            elif kind == "k":
                outs[kind][0, idx] = y.astype(BF16)
            elif kind == "vT":
                if opt == "ones_row_64":
                    y = jnp.where(_iota(y.shape, 1) == HEAD_DIM, 1.0, y)
                    outs[kind][0, idx, 0] = y.T.astype(BF16)
                else:
                    outs[kind][0, idx, 0, 0:LANES, :] = y.T.astype(BF16)
                    outs[kind][0, idx, 0, LANES:, :] = jnp.ones((ONES_ROWS, y.shape[0]), BF16)
            elif kind == "vTw":
                y_t = jnp.where(_iota(y.shape, 1) == HEAD_DIM, 1.0, y).T.astype(BF16)
                for sub in range(y.shape[0] // VW_TILE):
                    outs[kind][0, idx, sub] = y_t[:, sub * VW_TILE:(sub + 1) * VW_TILE]
            elif kind == "f":
                outs[kind][0, idx] = y
            else:
                outs[kind][0, idx] = y.T


def _project(x, g, w, cos_slab, sin_slab, plan, chunk=4):
    B, T, D = x.shape
    tm = KV_TILE
    assert T % tm == 0
    count = {k: sum(1 for p in plan if p[1] == k) for k in _KINDS}
    kinds = tuple(k for k in _KINDS if count[k])
    out_specs, out_shape = [], []
    for k in kinds:
        n = count[k]
        if k in ("qT", "fT"):
            out_specs.append(pl.BlockSpec((1, n, LANES, tm), lambda b, i: (b, 0, 0, i)))
            out_shape.append(jax.ShapeDtypeStruct((B, n, LANES, T), BF16 if k == "qT" else F32))
        elif k in ("k", "f"):
            out_specs.append(pl.BlockSpec((1, n, tm, LANES), lambda b, i: (b, 0, i, 0)))
            out_shape.append(jax.ShapeDtypeStruct((B, n, T, LANES), BF16 if k == "k" else F32))
        elif k == "vTw":
            sub = tm // VW_TILE
            out_specs.append(pl.BlockSpec((1, n, sub, LANES, VW_TILE), lambda b, i: (b, 0, i, 0, 0)))
            out_shape.append(jax.ShapeDtypeStruct((B, n, T // VW_TILE, LANES, VW_TILE), BF16))
        else:
            wide = any(p[1] == "vT" and p[3] != "ones_row_64" for p in plan)
            rows = LANES + ONES_ROWS if wide else LANES
            out_specs.append(pl.BlockSpec((1, n, 1, rows, tm), lambda b, i: (b, 0, i, 0, 0)))
            out_shape.append(jax.ShapeDtypeStruct((B, n, T // tm, rows, tm), BF16))
    kern = functools.partial(_proj_kernel, plan=tuple(plan), kinds=kinds, chunk=chunk)
    outs = pl.pallas_call(
        kern,
        grid=(B, T // tm),
        in_specs=[
            pl.BlockSpec((1, tm, D), lambda b, i: (b, i, 0)),
            pl.BlockSpec((1, D), lambda b, i: (0, 0)),
            pl.BlockSpec((D, len(plan) * LANES), lambda b, i: (0, 0)),
            pl.BlockSpec((tm, LANES), lambda b, i: (i, 0)),
            pl.BlockSpec((tm, LANES), lambda b, i: (i, 0)),
        ],
        out_specs=out_specs,
        out_shape=out_shape,
        compiler_params=_params(("parallel", "parallel")),
        name="norm_proj_rope",
    )(x, g.reshape(1, D), w, cos_slab, sin_slab)
    return dict(zip(kinds, outs))


def _head_rows(slab_t, which):
    row = _iota(slab_t.shape, 0)
    keep = ((row % 64) // HALF) == which
    return jnp.where(keep, slab_t, jnp.zeros_like(slab_t))


def _tree(op, xs):
    while len(xs) > 1:
        xs = [op(xs[i], xs[i + 1]) if i + 1 < len(xs) else xs[i] for i in range(0, len(xs), 2)]
    return xs[0]


def _fold_rows_max(x, ways=4):
    rows, n = x.shape
    per = rows // ways
    parts = [jnp.max(x[i * per:(i + 1) * per].reshape(per // 8, 8, n), axis=0) for i in range(ways)]
    return _tree(jnp.maximum, parts)


ROW_BLOCK = 64


def _online_update(s_sc, p_sc, v_aug, m_ref, acc_ref):
    tk, n = s_sc.shape
    m_prev = m_ref[...]
    m_tile = jnp.max(_fold_rows_max(s_sc[...]), axis=0, keepdims=True)
    m_new = jnp.maximum(m_prev, m_tile)
    alpha = jnp.exp2(m_prev - m_new)
    for r in range(tk // ROW_BLOCK):
        rows = slice(r * ROW_BLOCK, (r + 1) * ROW_BLOCK)
        p_sc[rows, :] = jnp.exp2(s_sc[rows, :] - m_new).astype(BF16)
    acc_ref[...] = alpha * acc_ref[...] + _dot(v_aug, p_sc[...])
    m_ref[...] = m_new


def _flash_pipeline(n, put_scores, update, buf_a, buf_b, put_last=None):
    if put_last is None:
        put_scores(buf_a, 0)
    else:
        @pl.when(n > 0)
        def _():
            put_scores(buf_a, 0)

    def pair(p, carry):
        j = 2 * p
        put_scores(buf_b, j + 1)
        update(buf_a, j)

        @pl.when(j + 2 < n)
        def _():
            put_scores(buf_a, j + 2)
            update(buf_b, j + 1)

        return carry

    lax.fori_loop(0, n // 2, pair, 0)
    in_b = jnp.logical_and(n > 0, n % 2 == 0)
    in_a = n % 2 == 1

    def finish(cur, other):
        if put_last is not None:
            put_last(other, n)
        update(cur, n - 1)
        if put_last is not None:
            update(other, n)

    pl.when(in_b)(lambda: finish(buf_b, buf_a))
    pl.when(in_a)(lambda: finish(buf_a, buf_b))
    if put_last is not None:
        @pl.when(n == 0)
        def _():
            put_last(buf_a, 0)
            update(buf_a, 0)


def _softmax_keys(s):
    m = jnp.max(s, axis=0, keepdims=True)
    m = jnp.where(m > NEG_INF, m, 0.0)
    e = jnp.exp2(s - m)
    return e / jnp.maximum(jnp.sum(e, axis=0, keepdims=True), 1e-30)


def _init_flash(m_ref, acc_ref):
    m_ref[...] = jnp.full(m_ref.shape, M_FLOOR, F32)
    acc_ref[...] = jnp.zeros(acc_ref.shape, F32)


def _diff_attn_kernel(lam_ref, q_ref, k_ref, v_ref, g_ref, o_ref, m_sc, acc_sc, sa_sc, sb_sc,
                      p_sc, *, tq, lambda_init):
    qi = pl.program_id(2)
    q_t = q_ref[0, 0]
    qs = [_head_rows(q_t, c) for c in range(2)]
    for c in range(2):
        _init_flash(m_sc.at[c], acc_sc.at[c])

    def put_scores(buf, j, masked=False):
        k = k_ref[0, 0, pl.ds(pl.multiple_of(j * tq, tq), tq), :]
        for c in range(2):
            s = _dot(k, qs[c])
            if masked:
                s = jnp.where(_iota(s.shape, 0) <= _iota(s.shape, 1), s, NEG_INF)
            buf[c] = s

    def put_diagonal(buf, j):
        put_scores(buf, j, masked=True)

    def update(buf, j):
        v_aug = v_ref[0, 0, j]
        for c in range(2):
            _online_update(buf.at[c], p_sc.at[c], v_aug, m_sc.at[c], acc_sc.at[c])

    _flash_pipeline(qi, put_scores, update, sa_sc, sb_sc, put_last=put_diagonal)

    lam = lam_ref[...]
    s01 = jnp.sum(lam[0:1] * lam[1:2], axis=-1, keepdims=True)
    s23 = jnp.sum(lam[2:3] * lam[3:4], axis=-1, keepdims=True)
    lam_val = jnp.exp(s01) - jnp.exp(s23) + lambda_init
    o0 = acc_sc[0, 0:LANES] / jnp.maximum(acc_sc[0, LANES:LANES + 1], 1e-30)
    o1 = acc_sc[1, 0:LANES] / jnp.maximum(acc_sc[1, LANES:LANES + 1], 1e-30)
    o = o0 - lam_val * o1
    y = o * lax.rsqrt(jnp.mean(o * o, axis=0, keepdims=True) + NORM_EPS)
    y = (y * g_ref[...]) * (1.0 - lambda_init)
    o_ref[0] = y.T.astype(o_ref.dtype)


def _diff_attention(sl, lam, subln_g, lambda_init):
    q_t, k, v_t = sl["qT"], sl["k"], sl["vT"]
    B, H, T, _ = k.shape
    tq = KV_TILE
    nk = T // tq
    kern = functools.partial(_diff_attn_kernel, tq=tq, lambda_init=lambda_init)
    return pl.pallas_call(
        kern,
        grid=(B, H, T // tq),
        in_specs=[
            pl.BlockSpec((4, HEAD_DIM), lambda b, h, i: (0, 0)),
            pl.BlockSpec((1, 1, LANES, tq), lambda b, h, i: (b, h, 0, i)),
            pl.BlockSpec((1, 1, T, LANES), lambda b, h, i: (b, h, 0, 0)),
            pl.BlockSpec((1, 1, nk, LANES + ONES_ROWS, tq), lambda b, h, i: (b, h, 0, 0, 0)),
            pl.BlockSpec((LANES, 1), lambda b, h, i: (0, 0)),
        ],
        out_specs=pl.BlockSpec((1, tq, LANES), lambda b, h, i: (b, i, h)),
        out_shape=jax.ShapeDtypeStruct((B, T, H * LANES), BF16),
        scratch_shapes=[
            pltpu.VMEM((2, 1, tq), F32),
            pltpu.VMEM((2, LANES + ONES_ROWS, tq), F32),
            pltpu.VMEM((2, tq, tq), F32),
            pltpu.VMEM((2, tq, tq), F32),
            pltpu.VMEM((2, tq, tq), BF16),
        ],
        compiler_params=_params(("parallel", "parallel", "arbitrary")),
        name="diff_attention",
    )(lam, q_t, k, v_t, subln_g.reshape(LANES, 1))


def _dsa_select_kernel(qi_ref, ki_ref, misc_ref, tril_ref, bias_ref, key_sc, *, tq, ck, nk, topk):
    t0 = pl.program_id(1) * tq
    nvalid = (t0 + tq - 1) // ck + 1
    w = misc_ref[0, 0]
    qh = jnp.concatenate([_head_rows(qi_ref[0, h // 2], h % 2) for h in range(IDX_HEADS)], axis=1)
    t_q = t0 + _iota((ck, tq), 1)

    def causal(c):
        return (c * ck + _iota((ck, tq), 0)) <= t_q

    def fill(c):
        kk = ki_ref[0, 0, c * ck:(c + 1) * ck, :]
        r = _dot(kk, qh)
        score = jnp.zeros((ck, tq), F32)
        for h in range(IDX_HEADS):
            score = score + w[h:h + 1, :] * jnp.maximum(r[:, h * tq:(h + 1) * tq], 0.0)
        bits = pltpu.bitcast(score, jnp.int32)
        key = bits ^ ((bits >> 31) & jnp.int32(0x7FFFFFFF))
        key = jnp.where(score == 0.0, 0, key)
        key_sc[c] = jnp.where(causal(c), key, INT_MIN)

    kf = float(topk)

    def select_threshold(n):
        for c in range(n):
            fill(c)

        def count(pred):
            acc = jnp.zeros((ck // 8, tq), F32)
            for c in range(n):
                ind = pred(key_sc[c], c)
                acc = acc + jnp.sum(ind.reshape(8, ck // 8, tq), axis=0)
            return jnp.sum(acc, axis=0, keepdims=True)

        zero = jnp.zeros((1, tq), jnp.int32)
        ans = jnp.where(count(lambda kc, c: jnp.where(kc >= zero, 1.0, 0.0)) >= kf, 0, INT_MIN)

        def bit_step(i, ans):
            cand = ans | (jnp.int32(1) << (30 - i))
            cnt = count(lambda kc, c: jnp.where(kc >= cand, 1.0, 0.0))
            return jnp.where(cnt >= kf, cand, ans)

        ans = lax.fori_loop(0, 31, bit_step, ans)
        cnt_gt = count(lambda kc, c: jnp.where(kc > ans, 1.0, 0.0))
        cnt_eq = count(lambda kc, c: jnp.where(kc == ans, jnp.where(causal(c), 1.0, 0.0), 0.0))
        return ans, cnt_gt, cnt_eq

    ans, cnt_gt, cnt_eq = lax.switch(
        nvalid - 1, [functools.partial(select_threshold, n) for n in range(1, nk + 1)])
    need = kf - cnt_gt
    has_tie = jnp.max(cnt_eq - need) > 0.0

    @pl.when(jnp.logical_not(has_tie))
    def _():
        def emit(c, carry):
            sel = jnp.where(key_sc[c] >= ans, jnp.where(causal(c), 0.0, NEG_INF), NEG_INF)
            bias_ref[0, 0, c] = sel.astype(BF16)
            return carry
        lax.fori_loop(0, nvalid, emit, 0)

    @pl.when(has_tie)
    def _():
        def emit(c, carry):
            kc = key_sc[c]
            eq = jnp.where(kc == ans, jnp.where(causal(c), 1.0, 0.0), 0.0)
            pre = _dot(tril_ref[...], eq.astype(BF16)) + carry
            take = jnp.where(kc > ans, 1.0, jnp.where(pre < need, eq, 0.0))
            sel = jnp.where(take > 0.5, jnp.where(causal(c), 0.0, NEG_INF), NEG_INF)
            bias_ref[0, 0, c] = sel.astype(BF16)
            return carry + jnp.sum(eq, axis=0, keepdims=True)
        lax.fori_loop(0, nvalid, emit, jnp.zeros((1, tq), F32))

    def blank(c, carry):
        bias_ref[0, 0, c] = jnp.full((ck, tq), NEG_INF, BF16)
        return carry

    lax.fori_loop(nvalid, nk, blank, 0)


def _dsa_select(sl, lay):
    q_t, k, misc_t = sl["qT"], sl["k"], sl["fT"]
    B, _, T, _ = k.shape
    tq, ck = SEL_Q_TILE, KV_TILE
    nk = T // ck
    topk = min(DSA_TOPK, T // 4)
    tril = jnp.asarray(np.tril(np.ones((ck, ck), np.float32), -1), BF16)
    kern = functools.partial(_dsa_select_kernel, tq=tq, ck=ck, nk=nk, topk=topk)
    return pl.pallas_call(
        kern,
        grid=(B, T // tq),
        in_specs=[
            pl.BlockSpec((1, 2, LANES, tq), lambda b, i: (b, lay["qi"] // 2, 0, i)),
            pl.BlockSpec((1, 1, T, LANES), lambda b, i: (b, lay["ki"], 0, 0)),
            pl.BlockSpec((1, 1, LANES, tq), lambda b, i: (b, lay["misc"], 0, i)),
            pl.BlockSpec((ck, ck), lambda b, i: (0, 0)),
        ],
        out_specs=pl.BlockSpec((1, 1, nk, ck, tq), lambda b, i: (b, i, 0, 0, 0)),
        out_shape=jax.ShapeDtypeStruct((B, T // tq, nk, ck, tq), BF16),
        scratch_shapes=[pltpu.VMEM((nk, ck, tq), jnp.int32)],
        compiler_params=_params(("parallel", "parallel")),
        name="dsa_select",
    )(q_t, k, misc_t, tril)


def _store_head_pairs(o_ref, o, n_heads, tq, col0=0):
    for p in range(n_heads // 2):
        even = o[0:HEAD_DIM, (2 * p) * tq:(2 * p + 1) * tq]
        odd = o[0:HEAD_DIM, (2 * p + 1) * tq:(2 * p + 2) * tq]
        pair = jnp.concatenate([even, odd], axis=0)
        o_ref[0, :, col0 + p * LANES:col0 + (p + 1) * LANES] = pair.T.astype(o_ref.dtype)


def _normalized(acc):
    return acc[0:HEAD_DIM] / jnp.maximum(acc[HEAD_DIM:HEAD_DIM + 1], 1e-30)


def _dsa_attn_kernel(q_ref, k_ref, v_ref, bias_ref, o_ref, qst, m_sc, acc_sc, sa_sc, sb_sc,
                     p_sc, *, tq, tk):
    qi = pl.program_id(1)
    H = A_HEADS
    for h in range(H):
        qst[:, h * tq:(h + 1) * tq] = _head_rows(q_ref[0, h // 2], h % 2)
    _init_flash(m_sc, acc_sc)
    nkv = (qi * tq) // tk + 1

    def put_scores(buf, j):
        k = k_ref[0, 0, pl.ds(pl.multiple_of(j * tk, tk), tk), :]
        parts = [bias_ref[0, s, j].astype(F32) for s in range(bias_ref.shape[1])]
        b = parts[0] if len(parts) == 1 else jnp.concatenate(parts, axis=1)
        buf[...] = _dot(k, qst[...]) + jnp.concatenate([b] * H, axis=1)

    def update(buf, j):
        _online_update(buf, p_sc, v_ref[0, 0, j], m_sc, acc_sc)

    _flash_pipeline(nkv, put_scores, update, sa_sc, sb_sc)
    _store_head_pairs(o_ref, _normalized(acc_sc[...]), H, tq)


def _dsa_attention(sl, bias, lay):
    q_t, k, v_t = sl["qT"], sl["k"], sl["vT"]
    B, _, T, _ = k.shape
    tq, tk = DSA_Q_TILE, KV_TILE
    nk = T // tk
    H = A_HEADS
    N = H * tq
    sel_tq = bias.shape[4]
    nsub = tq // sel_tq
    kern = functools.partial(_dsa_attn_kernel, tq=tq, tk=tk)
    return pl.pallas_call(
        kern,
        grid=(B, T // tq),
        in_specs=[
            pl.BlockSpec((1, H // 2, LANES, tq), lambda b, i: (b, lay["qA"] // (H // 2), 0, i)),
            pl.BlockSpec((1, 1, T, LANES), lambda b, i: (b, lay["kA"], 0, 0)),
            pl.BlockSpec((1, 1, nk, LANES, tk), lambda b, i: (b, lay["vA"], 0, 0, 0)),
            pl.BlockSpec((1, nsub, nk, tk, sel_tq), lambda b, i: (b, i, 0, 0, 0)),
        ],
        out_specs=pl.BlockSpec((1, tq, H * HEAD_DIM), lambda b, i: (b, i, 0)),
        out_shape=jax.ShapeDtypeStruct((B, T, H * HEAD_DIM), BF16),
        scratch_shapes=[
            pltpu.VMEM((LANES, N), BF16),
            pltpu.VMEM((1, N), F32),
            pltpu.VMEM((LANES, N), F32),
            pltpu.VMEM((tk, N), F32),
            pltpu.VMEM((tk, N), F32),
            pltpu.VMEM((tk, N), BF16),
        ],
        compiler_params=_params(("parallel", "arbitrary")),
        name="dsa_attention",
    )(q_t, k, v_t, bias)


def _compress_kernel(x_ref, pe_ref, w1_ref, w2_ref, o_ref, ot_ref, *, n_ch):
    x = x_ref[0, 0]
    xt = (x + pe_ref[0, 0]).astype(BF16)
    xb = (x + pe_ref[0, 1]).astype(BF16)
    for g in range(B_KV_GROUPS):
        a = _dot(xt, w1_ref[0, g, 0])
        b = _dot(xb, w1_ref[0, g, 1])
        h = jax.nn.gelu(a + pltpu.roll(b, n_ch - 1, 0))
        r = _dot(h.astype(BF16), w2_ref[0])
        o_ref[0, 0, g] = r.astype(BF16)
        ot_ref[0, 0, g] = r.T.astype(BF16)


def _compress(flat, pe, w1, w2):
    B, _, n_ch, W = flat.shape
    G = B_KV_GROUPS
    kern = functools.partial(_compress_kernel, n_ch=n_ch)
    return pl.pallas_call(
        kern,
        grid=(2, B),
        in_specs=[
            pl.BlockSpec((1, 1, n_ch, W), lambda s, b: (b, s, 0, 0)),
            pl.BlockSpec((1, 2, 1, W), lambda s, b: (s, 0, 0, 0)),
            pl.BlockSpec((1, G, 2, W, CMP_HIDDEN), lambda s, b: (s, 0, 0, 0, 0)),
            pl.BlockSpec((1, CMP_HIDDEN, LANES), lambda s, b: (s, 0, 0)),
        ],
        out_specs=[
            pl.BlockSpec((1, 1, G, n_ch, LANES), lambda s, b: (s, b, 0, 0, 0)),
            pl.BlockSpec((1, 1, G, LANES, n_ch), lambda s, b: (s, b, 0, 0, 0)),
        ],
        out_shape=[
            jax.ShapeDtypeStruct((2, B, G, n_ch, LANES), BF16),
            jax.ShapeDtypeStruct((2, B, G, LANES, n_ch), BF16),
        ],
        compiler_params=_params(("parallel", "parallel")),
        name="nsa_compress",
    )(flat, pe, w1, w2)


def _nsa_kernel(q_ref, kc_ref, vc_ref, ks_ref, vs_ref, kw_ref, vw_ref, misc_ref, ov_ref, ex_ref,
                cz_ref, wz_ref, o_ref, qst, m_sc, acc_sc, sa_sc, sb_sc, p_sc, mw_sc, accw_sc,
                sw_sc, pw_sc, oc_sc, *, tq, tk, T, n_s, n_sel, gate_row0):
    qi = pl.program_id(1)
    t0 = qi * tq
    G, J = B_KV_GROUPS, B_PER_GROUP
    N = J * tq
    for g in range(G):
        for j in range(J):
            qst[g, 0:LANES, j * tq:(j + 1) * tq] = _head_rows(q_ref[0, 2 * g + j // 2], j % 2)

    def q_time(shape):
        return t0 + (_iota(shape, 1) % tq)

    wlen = min(WINDOW + tq, T)
    wstart = pl.multiple_of(jnp.maximum(t0 - WINDOW, 0), tq)
    n_wt = wlen // tq
    wt = wstart // VW_TILE
    wz = wz_ref[jnp.minimum(qi, n_wt - 1)].astype(F32)
    wz = jnp.concatenate([wz] * J, axis=1)
    for g in range(G):
        kw = kw_ref[0, g, pl.ds(wstart, wlen), :]
        vw = jnp.concatenate([vw_ref[0, g, wt + i] for i in range(wlen // VW_TILE)], axis=1)
        sw_sc[g] = _dot(kw, qst[g, 0:LANES, :]) + wz
        _init_flash(mw_sc.at[g], accw_sc.at[g])
        _online_update(sw_sc.at[g], pw_sc.at[g], vw, mw_sc.at[g], accw_sc.at[g])

    rows = -(-n_s // 8) * 8
    blk = _iota((rows, tq), 0)
    t_q = t0 + _iota((rows, tq), 1)
    cur = t_q // SLC_BLOCK
    forced = (blk == 0) | (blk == cur) | (blk == cur - 1)
    admissible = blk * SLC_BLOCK <= t_q
    blk8 = _iota((8, tq), 0)
    for g in range(G):
        q = qst[g, 0:LANES, :]
        kc = kc_ref[0, 0, g]
        n_ch = kc.shape[0]
        s_c = _dot(kc, q)
        cmp_end = _iota((n_ch, N), 0) * CMP_STRIDE + (CMP_BLOCK - 1)
        s_c = jnp.where(cmp_end <= q_time((n_ch, N)), s_c, NEG_INF)
        p_c = _softmax_keys(s_c)
        oc_sc[g] = _dot(vc_ref[0, 0, g], p_c.astype(BF16))[0:HEAD_DIM]
        psum = p_c[:, 0:tq]
        for j in range(1, J):
            psum = psum + p_c[:, j * tq:(j + 1) * tq]
        p_hi = psum.astype(BF16)
        p_lo = (psum - p_hi.astype(F32)).astype(BF16)
        imp = _dot(ov_ref[...], p_hi) + _dot(ov_ref[...], p_lo)
        sc = jnp.where(forced, jnp.inf, imp[0:rows])
        sc = jnp.where(admissible, sc, NEG_INF)
        groups = [sc[8 * r:8 * r + 8] for r in range(rows // 8)]
        ranks = [jnp.zeros((8, tq), F32) for _ in groups]
        for m in range(n_s):
            cm = sc[m:m + 1, :]
            for r, grp in enumerate(groups):
                gt = jnp.where(cm > grp, 1.0, 0.0)
                ge = jnp.where(cm >= grp, 1.0, 0.0)
                if 8 * r + 7 <= m:
                    first = gt
                elif 8 * r > m:
                    first = ge
                else:
                    first = jnp.where(blk8 + 8 * r > m, ge, gt)
                ranks[r] = ranks[r] + first
        rank = jnp.concatenate(ranks, axis=0)
        drop = jnp.where(rank < float(n_sel), 0.0, 1.0)
        if rows < LANES:
            drop = jnp.concatenate([drop, jnp.ones((LANES - rows, tq), F32)], axis=0)
        drop = drop.astype(BF16)
        qst[g, LANES:2 * LANES, :] = jnp.concatenate([drop] * J, axis=1)

    jd = t0 // tk
    off = (t0 - jd * tk) // tq
    n_off = tk // tq
    for g in range(G):
        _init_flash(m_sc.at[g], acc_sc.at[g])

    def put_scores(buf, j):
        cz = cz_ref[jnp.where(j == jd, off, n_off)].astype(F32)
        cz = jnp.concatenate([cz] * J, axis=1)
        for g in range(G):
            k = ks_ref[0, g, pl.ds(pl.multiple_of(j * tk, tk), tk), :]
            lhs = jnp.concatenate([k, ex_ref[j]], axis=1)
            buf[g] = _dot(lhs, qst[g]) + cz

    def update(buf, j):
        for g in range(G):
            _online_update(buf.at[g], p_sc.at[g], vs_ref[0, g, j], m_sc.at[g], acc_sc.at[g])

    _flash_pipeline(jd + 1, put_scores, update, sa_sc, sb_sc)

    gates = jax.nn.sigmoid(misc_ref[0, 0])
    for g in range(G):
        def gate_row(c, g=g):
            rws = [gate_row0 + (g * J + j) * 3 + c for j in range(J)]
            return jnp.concatenate([gates[r:r + 1, :] for r in rws], axis=1)

        o_s = _normalized(acc_sc[g])
        o_w = _normalized(accw_sc[g])
        o = gate_row(0) * oc_sc[g] + (gate_row(1) * o_s + gate_row(2) * o_w)
        _store_head_pairs(o_ref, o, J, tq, col0=g * J * HEAD_DIM)


def _nsa_attention(sl, cmp_k, cmp_vt, lay):
    q_t, k, v_t, misc_t = sl["qT"], sl["k"], sl["vT"], sl["fT"]
    B, _, T, _ = k.shape
    G, J = B_KV_GROUPS, B_PER_GROUP
    tq, tk = NSA_Q_TILE, KV_TILE
    nk = T // tk
    n_ch = T // CMP_STRIDE
    n_c = n_ch - CMP_BLOCK // CMP_STRIDE + 1
    n_s = T // SLC_BLOCK
    n_sel = min(SLC_TOPN, n_s)
    assert n_s <= LANES
    c0 = np.arange(n_ch) * CMP_STRIDE
    s0 = np.arange(LANES) * SLC_BLOCK
    ov = ((c0[None, :] < s0[:, None] + SLC_BLOCK) & (c0[None, :] + CMP_BLOCK > s0[:, None]))
    ov = ov & (np.arange(n_ch)[None, :] < n_c) & (np.arange(LANES)[:, None] < n_s)
    ov = jnp.asarray(ov.astype(np.float32), BF16)
    pos = np.arange(T).reshape(nk, tk, 1)
    ex = (pos // SLC_BLOCK == np.arange(LANES).reshape(1, 1, LANES))
    ex = jnp.asarray(ex.astype(np.float32) * -BIG, BF16)
    n_off = tk // tq
    kp = np.arange(tk).reshape(1, tk, 1)
    tl = np.arange(tq).reshape(1, 1, tq) + np.arange(n_off + 1).reshape(n_off + 1, 1, 1) * tq
    cz = np.where((kp <= tl) | (np.arange(n_off + 1).reshape(-1, 1, 1) == n_off), 0.0, -BIG)
    cz = jnp.asarray(cz.astype(np.float32), BF16)
    wlen = min(WINDOW + tq, T)
    n_wt = wlen // tq
    assert wlen % tq == 0 and T >= wlen
    kp = np.arange(wlen).reshape(1, wlen, 1)
    tl = np.arange(tq).reshape(1, 1, tq)
    early = kp <= tl + np.arange(n_wt).reshape(n_wt, 1, 1) * tq
    late = (kp > tl) & (kp <= tl + WINDOW)
    band = np.where(np.arange(n_wt).reshape(n_wt, 1, 1) == n_wt - 1, late, early)
    wz = jnp.asarray(np.where(band, 0.0, -BIG).astype(np.float32), BF16)
    kern = functools.partial(_nsa_kernel, tq=tq, tk=tk, T=T, n_s=n_s, n_sel=n_sel,
                             gate_row0=lay["gate_row0"])
    N = J * tq
    for name in ("qB", "ks", "vs", "kw", "vw"):
        assert lay[name] % (2 * G if name == "qB" else G) == 0
    kslab = lambda off: pl.BlockSpec((1, G, T, LANES), lambda b, i: (b, off // G, 0, 0))
    return pl.pallas_call(
        kern,
        grid=(B, T // tq),
        in_specs=[
            pl.BlockSpec((1, 2 * G, LANES, tq), lambda b, i: (b, lay["qB"] // (2 * G), 0, i)),
            pl.BlockSpec((1, 1, G, n_ch, LANES), lambda b, i: (0, b, 0, 0, 0)),
            pl.BlockSpec((1, 1, G, LANES, n_ch), lambda b, i: (1, b, 0, 0, 0)),
            kslab(lay["ks"]),
            pl.BlockSpec((1, G, nk, LANES, tk), lambda b, i: (b, lay["vs"] // G, 0, 0, 0)),
            kslab(lay["kw"]),
            pl.BlockSpec((1, G, T // VW_TILE, LANES, VW_TILE),
                         lambda b, i: (b, lay["vw"] // G, 0, 0, 0)),
            pl.BlockSpec((1, 1, LANES, tq), lambda b, i: (b, lay["misc"], 0, i)),
            pl.BlockSpec((LANES, n_ch), lambda b, i: (0, 0)),
            pl.BlockSpec((nk, tk, LANES), lambda b, i: (0, 0, 0)),
            pl.BlockSpec((n_off + 1, tk, tq), lambda b, i: (0, 0, 0)),
            pl.BlockSpec((n_wt, wlen, tq), lambda b, i: (0, 0, 0)),
        ],
        out_specs=pl.BlockSpec((1, tq, B_HEADS * HEAD_DIM), lambda b, i: (b, i, 0)),
        out_shape=jax.ShapeDtypeStruct((B, T, B_HEADS * HEAD_DIM), BF16),
        scratch_shapes=[
            pltpu.VMEM((G, 2 * LANES, N), BF16),
            pltpu.VMEM((G, 1, N), F32),
            pltpu.VMEM((G, LANES, N), F32),
            pltpu.VMEM((G, tk, N), F32),
            pltpu.VMEM((G, tk, N), F32),
            pltpu.VMEM((G, tk, N), BF16),
            pltpu.VMEM((G, 1, N), F32),
            pltpu.VMEM((G, LANES, N), F32),
            pltpu.VMEM((G, wlen, N), F32),
            pltpu.VMEM((G, wlen, N), BF16),
            pltpu.VMEM((G, HEAD_DIM, N), F32),
        ],
        compiler_params=_params(("parallel", "arbitrary")),
        name="nsa_attention",
    )(q_t, cmp_k, cmp_vt, k, v_t, k, sl["vTw"], misc_t, ov, ex, cz, wz)


def _rms(x, g):
    ms = jnp.mean(x * x, axis=-1, keepdims=True)
    return (x * lax.rsqrt(ms + NORM_EPS)) * g


def _layer_tail_kernel(*refs, n_in, final_norm):
    h_ref = refs[0]
    o_refs = refs[1:1 + n_in]
    w_refs = refs[1 + n_in:1 + 2 * n_in]
    g_ref, wu_ref, wd_ref = refs[1 + 2 * n_in:4 + 2 * n_in]
    rest = refs[4 + 2 * n_in:]
    if final_norm:
        fg_ref, out_ref, xn_sc = rest
    else:
        out_ref, xn_sc = rest
    f = pl.program_id(1)

    @pl.when(f == 0)
    def _():
        mix = _dot(o_refs[0][...], w_refs[0][...])
        for i in range(1, n_in):
            mix = mix + _dot(o_refs[i][...], w_refs[i][...])
        x = h_ref[...] + mix
        xn_sc[...] = _rms(x, g_ref[...]).astype(BF16)
        out_ref[...] = x

    u = _dot(xn_sc[...], wu_ref[...])
    a = jnp.square(jnp.maximum(u, 0.0)).astype(BF16)
    out_ref[...] += _dot(a, wd_ref[...])

    if final_norm:
        @pl.when(f == pl.num_programs(1) - 1)
        def _():
            out_ref[...] = _rms(out_ref[...], fg_ref[...])


def _layer_tail(h2, outs, ws, g, w_up, w_down, final_g=None, tm=1024, tf=1024):
    N, D = h2.shape
    F = w_up.shape[1]
    tm = min(tm, N)
    n_in = len(outs)
    in_specs = [pl.BlockSpec((tm, D), lambda i, f: (i, 0))]
    in_specs += [pl.BlockSpec((tm, o.shape[1]), lambda i, f: (i, 0)) for o in outs]
    in_specs += [pl.BlockSpec(w.shape, lambda i, f: (0, 0)) for w in ws]
    in_specs += [
        pl.BlockSpec((1, D), lambda i, f: (0, 0)),
        pl.BlockSpec((D, tf), lambda i, f: (0, f)),
        pl.BlockSpec((tf, D), lambda i, f: (f, 0)),
    ]
    args = [h2, *outs, *ws, g.reshape(1, D), w_up, w_down]
    if final_g is not None:
        in_specs.append(pl.BlockSpec((1, D), lambda i, f: (0, 0)))
        args.append(final_g.reshape(1, D))
    kern = functools.partial(_layer_tail_kernel, n_in=n_in, final_norm=final_g is not None)
    return pl.pallas_call(
        kern,
        grid=(N // tm, F // tf),
        in_specs=in_specs,
        out_specs=pl.BlockSpec((tm, D), lambda i, f: (i, 0)),
        out_shape=jax.ShapeDtypeStruct((N, D), F32),
        scratch_shapes=[pltpu.VMEM((tm, D), BF16)],
        compiler_params=_params(("parallel", "arbitrary")),
        name="out_proj_mlp",
    )(*args)


def _even_layout():
    offs = {}
    o = 0
    for name, size in (("qa", 512), ("ka", 64), ("va", 64), ("qi", 256), ("ki", 64), ("wi", 4),
                       ("qb", 512), ("kvb", 768), ("gb", 24)):
        offs[name] = o
        o += size
    kvb = lambda which, g: offs["kvb"] + (which * B_KV_GROUPS + g) * HEAD_DIM
    cols, plan, lay = [], [], {}
    n = {k: 0 for k in _KINDS}

    def add(c, roped, kind, opt=None):
        cols.append(c)
        plan.append((roped, kind, n[kind], opt))
        n[kind] += 1
        return n[kind] - 1

    lay["qA"] = n["qT"]
    for p in range(4):
        add(_pair_cols(offs["qa"] + 2 * p * 64, offs["qa"] + (2 * p + 1) * 64), True, "qT", Q_SCALE)
    lay["qB"] = n["qT"]
    for p in range(4):
        add(_pair_cols(offs["qb"] + 2 * p * 64, offs["qb"] + (2 * p + 1) * 64), True, "qT", Q_SCALE)
    lay["qi"] = n["qT"]
    for p in range(2):
        add(_pair_cols(offs["qi"] + 2 * p * 64, offs["qi"] + (2 * p + 1) * 64), True, "qT")
    lay["kA"] = add(_pair_cols(offs["ka"], offs["ka"]), True, "k")
    lay["ki"] = add(_pair_cols(offs["ki"], offs["ki"]), True, "k")
    lay["ks"] = n["k"]
    for g in range(2):
        add(_pair_cols(kvb(2, g), kvb(2, g)), True, "k")
    lay["kw"] = n["k"]
    for g in range(2):
        add(_pair_cols(kvb(4, g), kvb(4, g)), True, "k")
    lay["kc"] = add(_pair_cols(kvb(0, 0), kvb(0, 1)), True, "f")
    lay["vc"] = add(_pair_cols(kvb(1, 0), kvb(1, 1)), False, "f")
    misc = np.full(LANES, -1)
    misc[0:IDX_HEADS] = offs["wi"] + np.arange(IDX_HEADS)
    lay["gate_row0"] = 8
    misc[8:8 + 24] = offs["gb"] + np.arange(24)
    lay["misc"] = add(misc, False, "fT")
    lay["vw"] = n["vTw"]
    for g in range(2):
        add(_head_cols(kvb(5, g)), False, "vTw")
    lay["vs"] = n["vT"]
    for g in range(2):
        add(_head_cols(kvb(3, g)), False, "vT", "ones_row_64")
    lay["vA"] = add(_head_cols(offs["va"]), False, "vT", "ones_row_64")
    return np.concatenate(cols), plan, lay


def _odd_layout():
    cols, plan = [], []
    for h in range(C_HEADS):
        cols.append(_pair_cols(h * 128, h * 128 + 64))
        plan.append((True, "qT", h, Q_SCALE))
    for h in range(C_HEADS):
        cols.append(_pair_cols(1024 + h * 128, 1024 + h * 128 + 64))
        plan.append((True, "k", h, None))
    for h in range(C_HEADS):
        cols.append(2048 + h * 128 + np.arange(LANES))
        plan.append((False, "vT", h, "ones_rows_below"))
    return np.concatenate(cols), plan


def _compress_weights(pe, w1, w2):
    d = _PAIR_D
    which = _PAIR_WHICH
    pe_l = pe[:, :, d]
    pe_l = pe_l.reshape(2, 2, 1, CMP_STRIDE * LANES)
    w1r = w1.astype(BF16).reshape(2, CMP_BLOCK, HEAD_DIM, CMP_HIDDEN)[:, :, d, :]
    per_g = []
    for g in range(B_KV_GROUPS):
        keep = jnp.asarray(which == g)[None, None, :, None]
        per_g.append(jnp.where(keep, w1r, jnp.zeros_like(w1r)))
    w1g = jnp.stack(per_g, axis=1)
    w1g = w1g.reshape(2, B_KV_GROUPS, 2, CMP_STRIDE * LANES, CMP_HIDDEN)
    w2b = w2.astype(BF16)
    w2k = w2b[0][:, d]
    w2v = w2b[1][:, np.arange(LANES) % HEAD_DIM]
    w2l = jnp.stack([w2k, w2v], axis=0)
    return pe_l, w1g, w2l


def _even_mixer(h, norm_g, w_in, cmp_pe, cmp_w1, cmp_w2, w_out, cos_slab, sin_slab):
    B, T, D = h.shape
    cols, plan, lay = _even_layout()
    w = _gather_cols(w_in.astype(BF16), cols)
    sl = _project(h, norm_g, w, cos_slab, sin_slab, plan)

    bias = _dsa_select(sl, lay)
    o_a = _dsa_attention(sl, bias, lay)

    n_ch = T // CMP_STRIDE
    assert (lay["kc"], lay["vc"]) == (0, 1) and sl["f"].shape[1] == 2
    flat = sl["f"].reshape(B, 2, n_ch, CMP_STRIDE * LANES)
    pe_l, w1g, w2l = _compress_weights(cmp_pe, cmp_w1, cmp_w2)
    cmp_k, cmp_vt = _compress(flat, pe_l, w1g, w2l)
    o_b = _nsa_attention(sl, cmp_k, cmp_vt, lay)

    na = A_HEADS * HEAD_DIM
    wo = w_out.astype(BF16)
    return [o_a.reshape(B * T, -1), o_b.reshape(B * T, -1)], [wo[:na], wo[na:]]


def _odd_mixer(h, norm_g, w_in, lam, subln_g, w_out, cos_slab, sin_slab, lambda_init):
    B, T, D = h.shape
    cols, plan = _odd_layout()
    w = _gather_cols(w_in.astype(BF16), cols)
    sl = _project(h, norm_g, w, cos_slab, sin_slab, plan)
    o = _diff_attention(sl, lam, subln_g, lambda_init)
    return [o.reshape(B * T, -1)], [w_out.astype(BF16)]


def kernel(x, mix_norm_g, mlp_norm_g, even_w_in, even_cmp_pe, even_cmp_w1, even_cmp_w2, even_w_out, odd_w_in, odd_lambda, odd_subln_g, odd_w_out, mlp_w_up, mlp_w_down, final_norm_g):
    B, T, D = x.shape
    depth = mix_norm_g.shape[0]
    assert depth >= 1
    cos_slab, sin_slab = _rope_slabs(T)
    h = x
    for layer in range(depth):
        if layer % 2 == 0:
            e = layer // 2
            outs, ws = _even_mixer(h, mix_norm_g[layer], even_w_in[e], even_cmp_pe[e],
                                   even_cmp_w1[e], even_cmp_w2[e], even_w_out[e], cos_slab, sin_slab)
        else:
            o = layer // 2
            lambda_init = 0.8 - 0.6 * math.exp(-0.3 * layer)
            outs, ws = _odd_mixer(h, mix_norm_g[layer], odd_w_in[o], odd_lambda[o], odd_subln_g[o],
                                  odd_w_out[o], cos_slab, sin_slab, lambda_init)
        h2 = _layer_tail(h.reshape(B * T, D), outs, ws, mlp_norm_g[layer],
                         mlp_w_up[layer].astype(BF16), mlp_w_down[layer].astype(BF16),
                         final_g=final_norm_g if layer == depth - 1 else None)
        h = h2.reshape(B, T, D)
    return h
```

```python
import functools
import math

import numpy as np
import jax
import jax.numpy as jnp
from jax import lax
from jax.experimental import pallas as pl
from jax.experimental.pallas import tpu as pltpu

HEAD_DIM = 64
HALF = HEAD_DIM // 2
LANES = 128
ROPE_THETA = 10000.0
NORM_EPS = 1e-6
SCALE = HEAD_DIM ** -0.5

A_HEADS = 8
IDX_HEADS = 4
DSA_TOPK = 256
B_HEADS = 8
B_KV_GROUPS = 2
B_PER_GROUP = B_HEADS // B_KV_GROUPS
CMP_BLOCK = 32
CMP_STRIDE = 16
CMP_HIDDEN = 256
SLC_BLOCK = 64
SLC_TOPN = 16
WINDOW = 512
C_HEADS = 8

KV_TILE = 512
SEL_Q_TILE = 128
DSA_Q_TILE = 256
NSA_Q_TILE = 256
VW_TILE = 128

LOG2E = math.log2(math.e)
Q_SCALE = SCALE * LOG2E
ONES_ROWS = 16
BIG = 2.0 ** 100

NEG_INF = float("-inf")
M_FLOOR = -1e30
INT_MIN = -(2 ** 31)

VMEM_LIMIT = 56 * 1024 * 1024

BF16 = jnp.bfloat16
F32 = jnp.float32


def _dot(a, b):
    return jnp.dot(a, b, preferred_element_type=F32)


def _params(sem):
    return pltpu.CompilerParams(dimension_semantics=sem, vmem_limit_bytes=VMEM_LIMIT)


def _iota(shape, axis):
    return lax.broadcasted_iota(jnp.int32, shape, axis)


def _pair_cols(base_a, base_b):
    lane = np.arange(LANES)
    half = lane // 64
    which = (lane % 64) // HALF
    i = lane % HALF
    base = np.where(which == 0, base_a, base_b)
    return base + half * HALF + i


def _head_cols(base):
    lane = np.arange(LANES)
    return np.where(lane < HEAD_DIM, base + lane, -1)


_PAIR_D = _pair_cols(0, 0)
_PAIR_WHICH = (np.arange(LANES) % 64) // HALF


def _gather_cols(w, cols):
    cols = np.asarray(cols)
    safe = np.where(cols >= 0, cols, 0)
    g = jnp.take(w, jnp.asarray(safe, dtype=jnp.int32), axis=1)
    return jnp.where(jnp.asarray(cols >= 0)[None, :], g, jnp.zeros_like(g))


def _rope_slabs(T):
    inv = 1.0 / (ROPE_THETA ** (jnp.arange(0, HEAD_DIM, 2, dtype=F32) / HEAD_DIM))
    ang = jnp.arange(T, dtype=F32)[:, None] * inv[None, :]
    cos, sin = jnp.cos(ang), jnp.sin(ang)
    cos_slab = jnp.tile(cos, (1, 4))
    sin_slab = jnp.concatenate([-sin, -sin, sin, sin], axis=1)
    return cos_slab, sin_slab


_KINDS = ("qT", "k", "vT", "vTw", "f", "fT")


def _proj_kernel(x_ref, g_ref, w_ref, cos_ref, sin_ref, *out_refs, plan, kinds, chunk):
    outs = dict(zip(kinds, out_refs))
    x = x_ref[0]
    ms = jnp.mean(x * x, axis=-1, keepdims=True)
    xn = ((x * lax.rsqrt(ms + NORM_EPS)) * g_ref[...]).astype(BF16)
    cos = cos_ref[...]
    sin = sin_ref[...]
    n = len(plan)
    for c0 in range(0, n, chunk):
        c1 = min(c0 + chunk, n)
        r = _dot(xn, w_ref[:, c0 * LANES:c1 * LANES])
        for s in range(c0, c1):
            roped, kind, idx, opt = plan[s]
            y = r[:, (s - c0) * LANES:(s - c0 + 1) * LANES]
            if roped:
                y = y * cos + pltpu.roll(y, 64, 1) * sin
            if kind == "qT":
                if opt is not None:
                    y = y * opt
                outs[kind][0, idx] = y.T.astype(BF16)
            elif kind == "k":
                outs[kind][0, idx] = y.astype(BF16)
            elif kind == "vT":
                if opt == "ones_row_64":
                    y = jnp.where(_iota(y.shape, 1) == HEAD_DIM, 1.0, y)
                    outs[kind][0, idx, 0] = y.T.astype(BF16)
                else:
                    outs[kind][0, idx, 0, 0:LANES, :] = y.T.astype(BF16)
                    outs[kind][0, idx, 0, LANES:, :] = jnp.ones((ONES_ROWS, y.shape[0]), BF16)
            elif kind == "vTw":
                y_t = jnp.where(_iota(y.shape, 1) == HEAD_DIM, 1.0, y).T.astype(BF16)
                for sub in range(y.shape[0] // VW_TILE):
                    outs[kind][0, idx, sub] = y_t[:, sub * VW_TILE:(sub + 1) * VW_TILE]
            elif kind == "f":
                outs[kind][0, idx] = y
            else:
                outs[kind][0, idx] = y.T


def _project(x, g, w, cos_slab, sin_slab, plan, chunk=4):
    B, T, D = x.shape
    tm = KV_TILE
    assert T % tm == 0
    count = {k: sum(1 for p in plan if p[1] == k) for k in _KINDS}
    kinds = tuple(k for k in _KINDS if count[k])
    out_specs, out_shape = [], []
    for k in kinds:
        n = count[k]
        if k in ("qT", "fT"):
            out_specs.append(pl.BlockSpec((1, n, LANES, tm), lambda b, i: (b, 0, 0, i)))
            out_shape.append(jax.ShapeDtypeStruct((B, n, LANES, T), BF16 if k == "qT" else F32))
        elif k in ("k", "f"):
            out_specs.append(pl.BlockSpec((1, n, tm, LANES), lambda b, i: (b, 0, i, 0)))
            out_shape.append(jax.ShapeDtypeStruct((B, n, T, LANES), BF16 if k == "k" else F32))
        elif k == "vTw":
            sub = tm // VW_TILE
            out_specs.append(pl.BlockSpec((1, n, sub, LANES, VW_TILE), lambda b, i: (b, 0, i, 0, 0)))
            out_shape.append(jax.ShapeDtypeStruct((B, n, T // VW_TILE, LANES, VW_TILE), BF16))
        else:
            wide = any(p[1] == "vT" and p[3] != "ones_row_64" for p in plan)
            rows = LANES + ONES_ROWS if wide else LANES
            out_specs.append(pl.BlockSpec((1, n, 1, rows, tm), lambda b, i: (b, 0, i, 0, 0)))
            out_shape.append(jax.ShapeDtypeStruct((B, n, T // tm, rows, tm), BF16))
    kern = functools.partial(_proj_kernel, plan=tuple(plan), kinds=kinds, chunk=chunk)
    outs = pl.pallas_call(
        kern,
        grid=(B, T // tm),
        in_specs=[
            pl.BlockSpec((1, tm, D), lambda b, i: (b, i, 0)),
            pl.BlockSpec((1, D), lambda b, i: (0, 0)),
            pl.BlockSpec((D, len(plan) * LANES), lambda b, i: (0, 0)),
            pl.BlockSpec((tm, LANES), lambda b, i: (i, 0)),
            pl.BlockSpec((tm, LANES), lambda b, i: (i, 0)),
        ],
        out_specs=out_specs,
        out_shape=out_shape,
        compiler_params=_params(("parallel", "parallel")),
        name="norm_proj_rope",
    )(x, g.reshape(1, D), w, cos_slab, sin_slab)
    return dict(zip(kinds, outs))


def _head_rows(slab_t, which):
    row = _iota(slab_t.shape, 0)
    keep = ((row % 64) // HALF) == which
    return jnp.where(keep, slab_t, jnp.zeros_like(slab_t))


def _tree(op, xs):
    while len(xs) > 1:
        xs = [op(xs[i], xs[i + 1]) if i + 1 < len(xs) else xs[i] for i in range(0, len(xs), 2)]
    return xs[0]


def _fold_rows_max(x, ways=4):
    rows, n = x.shape
    per = rows // ways
    parts = [jnp.max(x[i * per:(i + 1) * per].reshape(per // 8, 8, n), axis=0) for i in range(ways)]
    return _tree(jnp.maximum, parts)


ROW_BLOCK = 64


def _online_update(s_sc, p_sc, v_aug, m_ref, acc_ref):
    tk, n = s_sc.shape
    m_prev = m_ref[...]
    m_tile = jnp.max(_fold_rows_max(s_sc[...]), axis=0, keepdims=True)
    m_new = jnp.maximum(m_prev, m_tile)
    alpha = jnp.exp2(m_prev - m_new)
    for r in range(tk // ROW_BLOCK):
        rows = slice(r * ROW_BLOCK, (r + 1) * ROW_BLOCK)
        p_sc[rows, :] = jnp.exp2(s_sc[rows, :] - m_new).astype(BF16)
    acc_ref[...] = alpha * acc_ref[...] + _dot(v_aug, p_sc[...])
    m_ref[...] = m_new


def _flash_pipeline(n, put_scores, update, buf_a, buf_b, put_last=None):
    if put_last is None:
        put_scores(buf_a, 0)
    else:
        @pl.when(n > 0)
        def _():
            put_scores(buf_a, 0)

    def pair(p, carry):
        j = 2 * p
        put_scores(buf_b, j + 1)
        update(buf_a, j)

        @pl.when(j + 2 < n)
        def _():
            put_scores(buf_a, j + 2)
            update(buf_b, j + 1)

        return carry

    lax.fori_loop(0, n // 2, pair, 0)
    in_b = jnp.logical_and(n > 0, n % 2 == 0)
    in_a = n % 2 == 1

    def finish(cur, other):
        if put_last is not None:
            put_last(other, n)
        update(cur, n - 1)
        if put_last is not None:
            update(other, n)

    pl.when(in_b)(lambda: finish(buf_b, buf_a))
    pl.when(in_a)(lambda: finish(buf_a, buf_b))
    if put_last is not None:
        @pl.when(n == 0)
        def _():
            put_last(buf_a, 0)
            update(buf_a, 0)


def _softmax_keys(s):
    m = jnp.max(s, axis=0, keepdims=True)
    m = jnp.where(m > NEG_INF, m, 0.0)
    e = jnp.exp2(s - m)
    return e / jnp.maximum(jnp.sum(e, axis=0, keepdims=True), 1e-30)


def _init_flash(m_ref, acc_ref):
    m_ref[...] = jnp.full(m_ref.shape, M_FLOOR, F32)
    acc_ref[...] = jnp.zeros(acc_ref.shape, F32)


DIFF_HEADS_PER_STEP = 2


def _diff_attn_kernel(lam_ref, q_ref, k_ref, v_ref, g_ref, o_ref, m_sc, acc_sc, sa_sc, sb_sc,
                      p_sc, *, tq, lambda_init):
    qi = pl.program_id(2)
    HS = DIFF_HEADS_PER_STEP
    qs = [[_head_rows(q_ref[0, hh], c) for c in range(2)] for hh in range(HS)]
    for s in range(2 * HS):
        _init_flash(m_sc.at[s], acc_sc.at[s])

    def put_scores(buf, j, masked=False):
        for hh in range(HS):
            k = k_ref[0, hh, pl.ds(pl.multiple_of(j * tq, tq), tq), :]
            for c in range(2):
                s = _dot(k, qs[hh][c])
                if masked:
                    s = jnp.where(_iota(s.shape, 0) <= _iota(s.shape, 1), s, NEG_INF)
                buf[2 * hh + c] = s

    def put_diagonal(buf, j):
        put_scores(buf, j, masked=True)

    def update(buf, j):
        for hh in range(HS):
            v_aug = v_ref[0, hh, j]
            for c in range(2):
                s = 2 * hh + c
                _online_update(buf.at[s], p_sc.at[s], v_aug, m_sc.at[s], acc_sc.at[s])

    _flash_pipeline(qi, put_scores, update, sa_sc, sb_sc, put_last=put_diagonal)

    lam = lam_ref[...]
    s01 = jnp.sum(lam[0:1] * lam[1:2], axis=-1, keepdims=True)
    s23 = jnp.sum(lam[2:3] * lam[3:4], axis=-1, keepdims=True)
    lam_val = jnp.exp(s01) - jnp.exp(s23) + lambda_init
    for hh in range(HS):
        a0, a1 = acc_sc[2 * hh], acc_sc[2 * hh + 1]
        o0 = a0[0:LANES] / jnp.maximum(a0[LANES:LANES + 1], 1e-30)
        o1 = a1[0:LANES] / jnp.maximum(a1[LANES:LANES + 1], 1e-30)
        o = o0 - lam_val * o1
        y = o * lax.rsqrt(jnp.mean(o * o, axis=0, keepdims=True) + NORM_EPS)
        y = (y * g_ref[...]) * (1.0 - lambda_init)
        o_ref[0, :, hh * LANES:(hh + 1) * LANES] = y.T.astype(o_ref.dtype)


def _diff_attention(sl, lam, subln_g, lambda_init):
    q_t, k, v_t = sl["qT"], sl["k"], sl["vT"]
    B, H, T, _ = k.shape
    tq = KV_TILE
    nk = T // tq
    kern = functools.partial(_diff_attn_kernel, tq=tq, lambda_init=lambda_init)
    HS = DIFF_HEADS_PER_STEP
    ns = 2 * HS
    return pl.pallas_call(
        kern,
        grid=(B, H // HS, T // tq),
        in_specs=[
            pl.BlockSpec((4, HEAD_DIM), lambda b, h, i: (0, 0)),
            pl.BlockSpec((1, HS, LANES, tq), lambda b, h, i: (b, h, 0, i)),
            pl.BlockSpec((1, HS, T, LANES), lambda b, h, i: (b, h, 0, 0)),
            pl.BlockSpec((1, HS, nk, LANES + ONES_ROWS, tq), lambda b, h, i: (b, h, 0, 0, 0)),
            pl.BlockSpec((LANES, 1), lambda b, h, i: (0, 0)),
        ],
        out_specs=pl.BlockSpec((1, tq, HS * LANES), lambda b, h, i: (b, i, h)),
        out_shape=jax.ShapeDtypeStruct((B, T, H * LANES), BF16),
        scratch_shapes=[
            pltpu.VMEM((ns, 1, tq), F32),
            pltpu.VMEM((ns, LANES + ONES_ROWS, tq), F32),
            pltpu.VMEM((ns, tq, tq), F32),
            pltpu.VMEM((ns, tq, tq), F32),
            pltpu.VMEM((ns, tq, tq), BF16),
        ],
        compiler_params=_params(("parallel", "parallel", "arbitrary")),
        name="diff_attention",
    )(lam, q_t, k, v_t, subln_g.reshape(LANES, 1))


def _dsa_select_kernel(qi_ref, ki_ref, misc_ref, tril_ref, bias_ref, key_sc, *, tq, ck, nk, topk):
    t0 = pl.program_id(1) * tq
    nvalid = (t0 + tq - 1) // ck + 1
    w = misc_ref[0, 0]
    qh = jnp.concatenate([_head_rows(qi_ref[0, h // 2], h % 2) for h in range(IDX_HEADS)], axis=1)
    t_q = t0 + _iota((ck, tq), 1)

    def causal(c):
        return (c * ck + _iota((ck, tq), 0)) <= t_q

    def fill(c):
        kk = ki_ref[0, 0, c * ck:(c + 1) * ck, :]
        r = _dot(kk, qh)
        score = jnp.zeros((ck, tq), F32)
        for h in range(IDX_HEADS):
            score = score + w[h:h + 1, :] * jnp.maximum(r[:, h * tq:(h + 1) * tq], 0.0)
        bits = pltpu.bitcast(score, jnp.int32)
        key = bits ^ ((bits >> 31) & jnp.int32(0x7FFFFFFF))
        key = jnp.where(score == 0.0, 0, key)
        key_sc[c] = jnp.where(causal(c), key, INT_MIN)

    kf = float(topk)

    def select_threshold(n):
        for c in range(n):
            fill(c)

        def count(pred):
            acc = jnp.zeros((ck // 8, tq), F32)
            for c in range(n):
                ind = pred(key_sc[c], c)
                acc = acc + jnp.sum(ind.reshape(8, ck // 8, tq), axis=0)
            return jnp.sum(acc, axis=0, keepdims=True)

        zero = jnp.zeros((1, tq), jnp.int32)
        ans = jnp.where(count(lambda kc, c: jnp.where(kc >= zero, 1.0, 0.0)) >= kf, 0, INT_MIN)

        def bit_step(i, ans):
            cand = ans | (jnp.int32(1) << (30 - i))
            cnt = count(lambda kc, c: jnp.where(kc >= cand, 1.0, 0.0))
            return jnp.where(cnt >= kf, cand, ans)

        ans = lax.fori_loop(0, 31, bit_step, ans)
        cnt_gt = count(lambda kc, c: jnp.where(kc > ans, 1.0, 0.0))
        cnt_eq = count(lambda kc, c: jnp.where(kc == ans, jnp.where(causal(c), 1.0, 0.0), 0.0))
        return ans, cnt_gt, cnt_eq

    ans, cnt_gt, cnt_eq = lax.switch(
        nvalid - 1, [functools.partial(select_threshold, n) for n in range(1, nk + 1)])
    need = kf - cnt_gt
    has_tie = jnp.max(cnt_eq - need) > 0.0

    @pl.when(jnp.logical_not(has_tie))
    def _():
        def emit(c, carry):
            sel = jnp.where(key_sc[c] >= ans, jnp.where(causal(c), 0.0, NEG_INF), NEG_INF)
            bias_ref[0, 0, c] = sel.astype(BF16)
            return carry
        lax.fori_loop(0, nvalid, emit, 0)

    @pl.when(has_tie)
    def _():
        def emit(c, carry):
            kc = key_sc[c]
            eq = jnp.where(kc == ans, jnp.where(causal(c), 1.0, 0.0), 0.0)
            pre = _dot(tril_ref[...], eq.astype(BF16)) + carry
            take = jnp.where(kc > ans, 1.0, jnp.where(pre < need, eq, 0.0))
            sel = jnp.where(take > 0.5, jnp.where(causal(c), 0.0, NEG_INF), NEG_INF)
            bias_ref[0, 0, c] = sel.astype(BF16)
            return carry + jnp.sum(eq, axis=0, keepdims=True)
        lax.fori_loop(0, nvalid, emit, jnp.zeros((1, tq), F32))

    def blank(c, carry):
        bias_ref[0, 0, c] = jnp.full((ck, tq), NEG_INF, BF16)
        return carry

    lax.fori_loop(nvalid, nk, blank, 0)


def _dsa_select(sl, lay):
    q_t, k, misc_t = sl["qT"], sl["k"], sl["fT"]
    B, _, T, _ = k.shape
    tq, ck = SEL_Q_TILE, KV_TILE
    nk = T // ck
    topk = min(DSA_TOPK, T // 4)
    tril = jnp.asarray(np.tril(np.ones((ck, ck), np.float32), -1), BF16)
    kern = functools.partial(_dsa_select_kernel, tq=tq, ck=ck, nk=nk, topk=topk)
    return pl.pallas_call(
        kern,
        grid=(B, T // tq),
        in_specs=[
            pl.BlockSpec((1, 2, LANES, tq), lambda b, i: (b, lay["qi"] // 2, 0, i)),
            pl.BlockSpec((1, 1, T, LANES), lambda b, i: (b, lay["ki"], 0, 0)),
            pl.BlockSpec((1, 1, LANES, tq), lambda b, i: (b, lay["misc"], 0, i)),
            pl.BlockSpec((ck, ck), lambda b, i: (0, 0)),
        ],
        out_specs=pl.BlockSpec((1, 1, nk, ck, tq), lambda b, i: (b, i, 0, 0, 0)),
        out_shape=jax.ShapeDtypeStruct((B, T // tq, nk, ck, tq), BF16),
        scratch_shapes=[pltpu.VMEM((nk, ck, tq), jnp.int32)],
        compiler_params=_params(("parallel", "parallel")),
        name="dsa_select",
    )(q_t, k, misc_t, tril)


def _store_head_pairs(o_ref, o, n_heads, tq, col0=0):
    for p in range(n_heads // 2):
        even = o[0:HEAD_DIM, (2 * p) * tq:(2 * p + 1) * tq]
        odd = o[0:HEAD_DIM, (2 * p + 1) * tq:(2 * p + 2) * tq]
        pair = jnp.concatenate([even, odd], axis=0)
        o_ref[0, :, col0 + p * LANES:col0 + (p + 1) * LANES] = pair.T.astype(o_ref.dtype)


def _normalized(acc):
    return acc[0:HEAD_DIM] / jnp.maximum(acc[HEAD_DIM:HEAD_DIM + 1], 1e-30)


def _dsa_attn_kernel(q_ref, k_ref, v_ref, bias_ref, o_ref, qst, m_sc, acc_sc, sa_sc, sb_sc,
                     p_sc, *, tq, tk):
    qi = pl.program_id(1)
    H = A_HEADS
    for h in range(H):
        qst[:, h * tq:(h + 1) * tq] = _head_rows(q_ref[0, h // 2], h % 2)
    _init_flash(m_sc, acc_sc)
    nkv = (qi * tq) // tk + 1

    def put_scores(buf, j):
        k = k_ref[0, 0, pl.ds(pl.multiple_of(j * tk, tk), tk), :]
        parts = [bias_ref[0, s, j].astype(F32) for s in range(bias_ref.shape[1])]
        b = parts[0] if len(parts) == 1 else jnp.concatenate(parts, axis=1)
        buf[...] = _dot(k, qst[...]) + jnp.concatenate([b] * H, axis=1)

    def update(buf, j):
        _online_update(buf, p_sc, v_ref[0, 0, j], m_sc, acc_sc)

    _flash_pipeline(nkv, put_scores, update, sa_sc, sb_sc)
    _store_head_pairs(o_ref, _normalized(acc_sc[...]), H, tq)


def _dsa_attention(sl, bias, lay):
    q_t, k, v_t = sl["qT"], sl["k"], sl["vT"]
    B, _, T, _ = k.shape
    tq, tk = DSA_Q_TILE, KV_TILE
    nk = T // tk
    H = A_HEADS
    N = H * tq
    sel_tq = bias.shape[4]
    nsub = tq // sel_tq
    kern = functools.partial(_dsa_attn_kernel, tq=tq, tk=tk)
    return pl.pallas_call(
        kern,
        grid=(B, T // tq),
        in_specs=[
            pl.BlockSpec((1, H // 2, LANES, tq), lambda b, i: (b, lay["qA"] // (H // 2), 0, i)),
            pl.BlockSpec((1, 1, T, LANES), lambda b, i: (b, lay["kA"], 0, 0)),
            pl.BlockSpec((1, 1, nk, LANES, tk), lambda b, i: (b, lay["vA"], 0, 0, 0)),
            pl.BlockSpec((1, nsub, nk, tk, sel_tq), lambda b, i: (b, i, 0, 0, 0)),
        ],
        out_specs=pl.BlockSpec((1, tq, H * HEAD_DIM), lambda b, i: (b, i, 0)),
        out_shape=jax.ShapeDtypeStruct((B, T, H * HEAD_DIM), BF16),
        scratch_shapes=[
            pltpu.VMEM((LANES, N), BF16),
            pltpu.VMEM((1, N), F32),
            pltpu.VMEM((LANES, N), F32),
            pltpu.VMEM((tk, N), F32),
            pltpu.VMEM((tk, N), F32),
            pltpu.VMEM((tk, N), BF16),
        ],
        compiler_params=_params(("parallel", "arbitrary")),
        name="dsa_attention",
    )(q_t, k, v_t, bias)


def _compress_kernel(x_ref, pe_ref, w1_ref, w2_ref, o_ref, ot_ref, *, n_ch):
    x = x_ref[0, 0]
    xt = (x + pe_ref[0, 0]).astype(BF16)
    xb = (x + pe_ref[0, 1]).astype(BF16)
    for g in range(B_KV_GROUPS):
        a = _dot(xt, w1_ref[0, g, 0])
        b = _dot(xb, w1_ref[0, g, 1])
        h = jax.nn.gelu(a + pltpu.roll(b, n_ch - 1, 0))
        r = _dot(h.astype(BF16), w2_ref[0])
        o_ref[0, 0, g] = r.astype(BF16)
        ot_ref[0, 0, g] = r.T.astype(BF16)


def _compress(flat, pe, w1, w2):
    B, _, n_ch, W = flat.shape
    G = B_KV_GROUPS
    kern = functools.partial(_compress_kernel, n_ch=n_ch)
    return pl.pallas_call(
        kern,
        grid=(2, B),
        in_specs=[
            pl.BlockSpec((1, 1, n_ch, W), lambda s, b: (b, s, 0, 0)),
            pl.BlockSpec((1, 2, 1, W), lambda s, b: (s, 0, 0, 0)),
            pl.BlockSpec((1, G, 2, W, CMP_HIDDEN), lambda s, b: (s, 0, 0, 0, 0)),
            pl.BlockSpec((1, CMP_HIDDEN, LANES), lambda s, b: (s, 0, 0)),
        ],
        out_specs=[
            pl.BlockSpec((1, 1, G, n_ch, LANES), lambda s, b: (s, b, 0, 0, 0)),
            pl.BlockSpec((1, 1, G, LANES, n_ch), lambda s, b: (s, b, 0, 0, 0)),
        ],
        out_shape=[
            jax.ShapeDtypeStruct((2, B, G, n_ch, LANES), BF16),
            jax.ShapeDtypeStruct((2, B, G, LANES, n_ch), BF16),
        ],
        compiler_params=_params(("parallel", "parallel")),
        name="nsa_compress",
    )(flat, pe, w1, w2)


def _nsa_kernel(q_ref, kc_ref, vc_ref, ks_ref, vs_ref, kw_ref, vw_ref, misc_ref, ov_ref, ex_ref,
                cz_ref, wz_ref, o_ref, qst, m_sc, acc_sc, sa_sc, sb_sc, p_sc, mw_sc, accw_sc,
                sw_sc, pw_sc, oc_sc, *, tq, tk, T, n_s, n_sel, gate_row0):
    qi = pl.program_id(1)
    t0 = qi * tq
    G, J = B_KV_GROUPS, B_PER_GROUP
    N = J * tq
    for g in range(G):
        for j in range(J):
            qst[g, 0:LANES, j * tq:(j + 1) * tq] = _head_rows(q_ref[0, 2 * g + j // 2], j % 2)

    def q_time(shape):
        return t0 + (_iota(shape, 1) % tq)

    wlen = min(WINDOW + tq, T)
    wstart = pl.multiple_of(jnp.maximum(t0 - WINDOW, 0), tq)
    n_wt = wlen // tq
    wt = wstart // VW_TILE
    wz = wz_ref[jnp.minimum(qi, n_wt - 1)].astype(F32)
    wz = jnp.concatenate([wz] * J, axis=1)
    for g in range(G):
        kw = kw_ref[0, g, pl.ds(wstart, wlen), :]
        vw = jnp.concatenate([vw_ref[0, g, wt + i] for i in range(wlen // VW_TILE)], axis=1)
        sw_sc[g] = _dot(kw, qst[g, 0:LANES, :]) + wz
        _init_flash(mw_sc.at[g], accw_sc.at[g])
        _online_update(sw_sc.at[g], pw_sc.at[g], vw, mw_sc.at[g], accw_sc.at[g])

    rows = -(-n_s // 8) * 8
    blk = _iota((rows, tq), 0)
    t_q = t0 + _iota((rows, tq), 1)
    cur = t_q // SLC_BLOCK
    forced = (blk == 0) | (blk == cur) | (blk == cur - 1)
    admissible = blk * SLC_BLOCK <= t_q
    blk8 = _iota((8, tq), 0)
    for g in range(G):
        q = qst[g, 0:LANES, :]
        kc = kc_ref[0, 0, g]
        n_ch = kc.shape[0]
        s_c = _dot(kc, q)
        cmp_end = _iota((n_ch, N), 0) * CMP_STRIDE + (CMP_BLOCK - 1)
        s_c = jnp.where(cmp_end <= q_time((n_ch, N)), s_c, NEG_INF)
        p_c = _softmax_keys(s_c)
        oc_sc[g] = _dot(vc_ref[0, 0, g], p_c.astype(BF16))[0:HEAD_DIM]
        psum = p_c[:, 0:tq]
        for j in range(1, J):
            psum = psum + p_c[:, j * tq:(j + 1) * tq]
        p_hi = psum.astype(BF16)
        p_lo = (psum - p_hi.astype(F32)).astype(BF16)
        imp = _dot(ov_ref[...], p_hi) + _dot(ov_ref[...], p_lo)
        sc = jnp.where(forced, jnp.inf, imp[0:rows])
        sc = jnp.where(admissible, sc, NEG_INF)
        groups = [sc[8 * r:8 * r + 8] for r in range(rows // 8)]
        ranks = [jnp.zeros((8, tq), F32) for _ in groups]
        for m in range(n_s):
            cm = sc[m:m + 1, :]
            for r, grp in enumerate(groups):
                gt = jnp.where(cm > grp, 1.0, 0.0)
                ge = jnp.where(cm >= grp, 1.0, 0.0)
                if 8 * r + 7 <= m:
                    first = gt
                elif 8 * r > m:
                    first = ge
                else:
                    first = jnp.where(blk8 + 8 * r > m, ge, gt)
                ranks[r] = ranks[r] + first
        rank = jnp.concatenate(ranks, axis=0)
        drop = jnp.where(rank < float(n_sel), 0.0, 1.0)
        if rows < LANES:
            drop = jnp.concatenate([drop, jnp.ones((LANES - rows, tq), F32)], axis=0)
        drop = drop.astype(BF16)
        qst[g, LANES:2 * LANES, :] = jnp.concatenate([drop] * J, axis=1)

    jd = t0 // tk
    off = (t0 - jd * tk) // tq
    n_off = tk // tq
    for g in range(G):
        _init_flash(m_sc.at[g], acc_sc.at[g])

    def put_scores(buf, j):
        cz = cz_ref[jnp.where(j == jd, off, n_off)].astype(F32)
        cz = jnp.concatenate([cz] * J, axis=1)
        for g in range(G):
            k = ks_ref[0, g, pl.ds(pl.multiple_of(j * tk, tk), tk), :]
            lhs = jnp.concatenate([k, ex_ref[j]], axis=1)
            buf[g] = _dot(lhs, qst[g]) + cz

    def update(buf, j):
        for g in range(G):
            _online_update(buf.at[g], p_sc.at[g], vs_ref[0, g, j], m_sc.at[g], acc_sc.at[g])

    _flash_pipeline(jd + 1, put_scores, update, sa_sc, sb_sc)

    gates = jax.nn.sigmoid(misc_ref[0, 0])
    for g in range(G):
        def gate_row(c, g=g):
            rws = [gate_row0 + (g * J + j) * 3 + c for j in range(J)]
            return jnp.concatenate([gates[r:r + 1, :] for r in rws], axis=1)

        o_s = _normalized(acc_sc[g])
        o_w = _normalized(accw_sc[g])
        o = gate_row(0) * oc_sc[g] + (gate_row(1) * o_s + gate_row(2) * o_w)
        _store_head_pairs(o_ref, o, J, tq, col0=g * J * HEAD_DIM)


def _nsa_attention(sl, cmp_k, cmp_vt, lay):
    q_t, k, v_t, misc_t = sl["qT"], sl["k"], sl["vT"], sl["fT"]
    B, _, T, _ = k.shape
    G, J = B_KV_GROUPS, B_PER_GROUP
    tq, tk = NSA_Q_TILE, KV_TILE
    nk = T // tk
    n_ch = T // CMP_STRIDE
    n_c = n_ch - CMP_BLOCK // CMP_STRIDE + 1
    n_s = T // SLC_BLOCK
    n_sel = min(SLC_TOPN, n_s)
    assert n_s <= LANES
    c0 = np.arange(n_ch) * CMP_STRIDE
    s0 = np.arange(LANES) * SLC_BLOCK
    ov = ((c0[None, :] < s0[:, None] + SLC_BLOCK) & (c0[None, :] + CMP_BLOCK > s0[:, None]))
    ov = ov & (np.arange(n_ch)[None, :] < n_c) & (np.arange(LANES)[:, None] < n_s)
    ov = jnp.asarray(ov.astype(np.float32), BF16)
    pos = np.arange(T).reshape(nk, tk, 1)
    ex = (pos // SLC_BLOCK == np.arange(LANES).reshape(1, 1, LANES))
    ex = jnp.asarray(ex.astype(np.float32) * -BIG, BF16)
    n_off = tk // tq
    kp = np.arange(tk).reshape(1, tk, 1)
    tl = np.arange(tq).reshape(1, 1, tq) + np.arange(n_off + 1).reshape(n_off + 1, 1, 1) * tq
    cz = np.where((kp <= tl) | (np.arange(n_off + 1).reshape(-1, 1, 1) == n_off), 0.0, -BIG)
    cz = jnp.asarray(cz.astype(np.float32), BF16)
    wlen = min(WINDOW + tq, T)
    n_wt = wlen // tq
    assert wlen % tq == 0 and T >= wlen
    kp = np.arange(wlen).reshape(1, wlen, 1)
    tl = np.arange(tq).reshape(1, 1, tq)
    early = kp <= tl + np.arange(n_wt).reshape(n_wt, 1, 1) * tq
    late = (kp > tl) & (kp <= tl + WINDOW)
    band = np.where(np.arange(n_wt).reshape(n_wt, 1, 1) == n_wt - 1, late, early)
    wz = jnp.asarray(np.where(band, 0.0, -BIG).astype(np.float32), BF16)
    kern = functools.partial(_nsa_kernel, tq=tq, tk=tk, T=T, n_s=n_s, n_sel=n_sel,
                             gate_row0=lay["gate_row0"])
    N = J * tq
    for name in ("qB", "ks", "vs", "kw", "vw"):
        assert lay[name] % (2 * G if name == "qB" else G) == 0
    kslab = lambda off: pl.BlockSpec((1, G, T, LANES), lambda b, i: (b, off // G, 0, 0))
    return pl.pallas_call(
        kern,
        grid=(B, T // tq),
        in_specs=[
            pl.BlockSpec((1, 2 * G, LANES, tq), lambda b, i: (b, lay["qB"] // (2 * G), 0, i)),
            pl.BlockSpec((1, 1, G, n_ch, LANES), lambda b, i: (0, b, 0, 0, 0)),
            pl.BlockSpec((1, 1, G, LANES, n_ch), lambda b, i: (1, b, 0, 0, 0)),
            kslab(lay["ks"]),
            pl.BlockSpec((1, G, nk, LANES, tk), lambda b, i: (b, lay["vs"] // G, 0, 0, 0)),
            kslab(lay["kw"]),
            pl.BlockSpec((1, G, T // VW_TILE, LANES, VW_TILE),
                         lambda b, i: (b, lay["vw"] // G, 0, 0, 0)),
            pl.BlockSpec((1, 1, LANES, tq), lambda b, i: (b, lay["misc"], 0, i)),
            pl.BlockSpec((LANES, n_ch), lambda b, i: (0, 0)),
            pl.BlockSpec((nk, tk, LANES), lambda b, i: (0, 0, 0)),
            pl.BlockSpec((n_off + 1, tk, tq), lambda b, i: (0, 0, 0)),
            pl.BlockSpec((n_wt, wlen, tq), lambda b, i: (0, 0, 0)),
        ],
        out_specs=pl.BlockSpec((1, tq, B_HEADS * HEAD_DIM), lambda b, i: (b, i, 0)),
        out_shape=jax.ShapeDtypeStruct((B, T, B_HEADS * HEAD_DIM), BF16),
        scratch_shapes=[
            pltpu.VMEM((G, 2 * LANES, N), BF16),
            pltpu.VMEM((G, 1, N), F32),
            pltpu.VMEM((G, LANES, N), F32),
            pltpu.VMEM((G, tk, N), F32),
            pltpu.VMEM((G, tk, N), F32),
            pltpu.VMEM((G, tk, N), BF16),
            pltpu.VMEM((G, 1, N), F32),
            pltpu.VMEM((G, LANES, N), F32),
            pltpu.VMEM((G, wlen, N), F32),
            pltpu.VMEM((G, wlen, N), BF16),
            pltpu.VMEM((G, HEAD_DIM, N), F32),
        ],
        compiler_params=_params(("parallel", "arbitrary")),
        name="nsa_attention",
    )(q_t, cmp_k, cmp_vt, k, v_t, k, sl["vTw"], misc_t, ov, ex, cz, wz)


def _rms(x, g):
    ms = jnp.mean(x * x, axis=-1, keepdims=True)
    return (x * lax.rsqrt(ms + NORM_EPS)) * g


def _layer_tail_kernel(*refs, n_in, final_norm):
    h_ref = refs[0]
    o_refs = refs[1:1 + n_in]
    w_refs = refs[1 + n_in:1 + 2 * n_in]
    g_ref, wu_ref, wd_ref = refs[1 + 2 * n_in:4 + 2 * n_in]
    rest = refs[4 + 2 * n_in:]
    if final_norm:
        fg_ref, out_ref, xn_sc = rest
    else:
        out_ref, xn_sc = rest
    f = pl.program_id(1)

    @pl.when(f == 0)
    def _():
        mix = _dot(o_refs[0][...], w_refs[0][...])
        for i in range(1, n_in):
            mix = mix + _dot(o_refs[i][...], w_refs[i][...])
        x = h_ref[...] + mix
        xn_sc[...] = _rms(x, g_ref[...]).astype(BF16)
        out_ref[...] = x

    u = _dot(xn_sc[...], wu_ref[...])
    a = jnp.square(jnp.maximum(u, 0.0)).astype(BF16)
    out_ref[...] += _dot(a, wd_ref[...])

    if final_norm:
        @pl.when(f == pl.num_programs(1) - 1)
        def _():
            out_ref[...] = _rms(out_ref[...], fg_ref[...])


def _layer_tail(h2, outs, ws, g, w_up, w_down, final_g=None, tm=1024, tf=1024):
    N, D = h2.shape
    F = w_up.shape[1]
    tm = min(tm, N)
    n_in = len(outs)
    in_specs = [pl.BlockSpec((tm, D), lambda i, f: (i, 0))]
    in_specs += [pl.BlockSpec((tm, o.shape[1]), lambda i, f: (i, 0)) for o in outs]
    in_specs += [pl.BlockSpec(w.shape, lambda i, f: (0, 0)) for w in ws]
    in_specs += [
        pl.BlockSpec((1, D), lambda i, f: (0, 0)),
        pl.BlockSpec((D, tf), lambda i, f: (0, f)),
        pl.BlockSpec((tf, D), lambda i, f: (f, 0)),
    ]
    args = [h2, *outs, *ws, g.reshape(1, D), w_up, w_down]
    if final_g is not None:
        in_specs.append(pl.BlockSpec((1, D), lambda i, f: (0, 0)))
        args.append(final_g.reshape(1, D))
    kern = functools.partial(_layer_tail_kernel, n_in=n_in, final_norm=final_g is not None)
    return pl.pallas_call(
        kern,
        grid=(N // tm, F // tf),
        in_specs=in_specs,
        out_specs=pl.BlockSpec((tm, D), lambda i, f: (i, 0)),
        out_shape=jax.ShapeDtypeStruct((N, D), F32),
        scratch_shapes=[pltpu.VMEM((tm, D), BF16)],
        compiler_params=_params(("parallel", "arbitrary")),
        name="out_proj_mlp",
    )(*args)


def _even_layout():
    offs = {}
    o = 0
    for name, size in (("qa", 512), ("ka", 64), ("va", 64), ("qi", 256), ("ki", 64), ("wi", 4),
                       ("qb", 512), ("kvb", 768), ("gb", 24)):
        offs[name] = o
        o += size
    kvb = lambda which, g: offs["kvb"] + (which * B_KV_GROUPS + g) * HEAD_DIM
    cols, plan, lay = [], [], {}
    n = {k: 0 for k in _KINDS}

    def add(c, roped, kind, opt=None):
        cols.append(c)
        plan.append((roped, kind, n[kind], opt))
        n[kind] += 1
        return n[kind] - 1

    lay["qA"] = n["qT"]
    for p in range(4):
        add(_pair_cols(offs["qa"] + 2 * p * 64, offs["qa"] + (2 * p + 1) * 64), True, "qT", Q_SCALE)
    lay["qB"] = n["qT"]
    for p in range(4):
        add(_pair_cols(offs["qb"] + 2 * p * 64, offs["qb"] + (2 * p + 1) * 64), True, "qT", Q_SCALE)
    lay["qi"] = n["qT"]
    for p in range(2):
        add(_pair_cols(offs["qi"] + 2 * p * 64, offs["qi"] + (2 * p + 1) * 64), True, "qT")
    lay["kA"] = add(_pair_cols(offs["ka"], offs["ka"]), True, "k")
    lay["ki"] = add(_pair_cols(offs["ki"], offs["ki"]), True, "k")
    lay["ks"] = n["k"]
    for g in range(2):
        add(_pair_cols(kvb(2, g), kvb(2, g)), True, "k")
    lay["kw"] = n["k"]
    for g in range(2):
        add(_pair_cols(kvb(4, g), kvb(4, g)), True, "k")
    lay["kc"] = add(_pair_cols(kvb(0, 0), kvb(0, 1)), True, "f")
    lay["vc"] = add(_pair_cols(kvb(1, 0), kvb(1, 1)), False, "f")
    misc = np.full(LANES, -1)
    misc[0:IDX_HEADS] = offs["wi"] + np.arange(IDX_HEADS)
    lay["gate_row0"] = 8
    misc[8:8 + 24] = offs["gb"] + np.arange(24)
    lay["misc"] = add(misc, False, "fT")
    lay["vw"] = n["vTw"]
    for g in range(2):
        add(_head_cols(kvb(5, g)), False, "vTw")
    lay["vs"] = n["vT"]
    for g in range(2):
        add(_head_cols(kvb(3, g)), False, "vT", "ones_row_64")
    lay["vA"] = add(_head_cols(offs["va"]), False, "vT", "ones_row_64")
    return np.concatenate(cols), plan, lay


def _odd_layout():
    cols, plan = [], []
    for h in range(C_HEADS):
        cols.append(_pair_cols(h * 128, h * 128 + 64))
        plan.append((True, "qT", h, Q_SCALE))
    for h in range(C_HEADS):
        cols.append(_pair_cols(1024 + h * 128, 1024 + h * 128 + 64))
        plan.append((True, "k", h, None))
    for h in range(C_HEADS):
        cols.append(2048 + h * 128 + np.arange(LANES))
        plan.append((False, "vT", h, "ones_rows_below"))
    return np.concatenate(cols), plan


def _compress_weights(pe, w1, w2):
    d = _PAIR_D
    which = _PAIR_WHICH
    pe_l = pe[:, :, d]
    pe_l = pe_l.reshape(2, 2, 1, CMP_STRIDE * LANES)
    w1r = w1.astype(BF16).reshape(2, CMP_BLOCK, HEAD_DIM, CMP_HIDDEN)[:, :, d, :]
    per_g = []
    for g in range(B_KV_GROUPS):
        keep = jnp.asarray(which == g)[None, None, :, None]
        per_g.append(jnp.where(keep, w1r, jnp.zeros_like(w1r)))
    w1g = jnp.stack(per_g, axis=1)
    w1g = w1g.reshape(2, B_KV_GROUPS, 2, CMP_STRIDE * LANES, CMP_HIDDEN)
    w2b = w2.astype(BF16)
    w2k = w2b[0][:, d]
    w2v = w2b[1][:, np.arange(LANES) % HEAD_DIM]
    w2l = jnp.stack([w2k, w2v], axis=0)
    return pe_l, w1g, w2l


def _even_mixer(h, norm_g, w_in, cmp_pe, cmp_w1, cmp_w2, w_out, cos_slab, sin_slab):
    B, T, D = h.shape
    cols, plan, lay = _even_layout()
    w = _gather_cols(w_in.astype(BF16), cols)
    sl = _project(h, norm_g, w, cos_slab, sin_slab, plan)

    bias = _dsa_select(sl, lay)
    o_a = _dsa_attention(sl, bias, lay)

    n_ch = T // CMP_STRIDE
    assert (lay["kc"], lay["vc"]) == (0, 1) and sl["f"].shape[1] == 2
    flat = sl["f"].reshape(B, 2, n_ch, CMP_STRIDE * LANES)
    pe_l, w1g, w2l = _compress_weights(cmp_pe, cmp_w1, cmp_w2)
    cmp_k, cmp_vt = _compress(flat, pe_l, w1g, w2l)
    o_b = _nsa_attention(sl, cmp_k, cmp_vt, lay)

    na = A_HEADS * HEAD_DIM
    wo = w_out.astype(BF16)
    return [o_a.reshape(B * T, -1), o_b.reshape(B * T, -1)], [wo[:na], wo[na:]]


def _odd_mixer(h, norm_g, w_in, lam, subln_g, w_out, cos_slab, sin_slab, lambda_init):
    B, T, D = h.shape
    cols, plan = _odd_layout()
    w = _gather_cols(w_in.astype(BF16), cols)
    sl = _project(h, norm_g, w, cos_slab, sin_slab, plan)
    o = _diff_attention(sl, lam, subln_g, lambda_init)
    return [o.reshape(B * T, -1)], [w_out.astype(BF16)]


def kernel(x, mix_norm_g, mlp_norm_g, even_w_in, even_cmp_pe, even_cmp_w1, even_cmp_w2, even_w_out, odd_w_in, odd_lambda, odd_subln_g, odd_w_out, mlp_w_up, mlp_w_down, final_norm_g):
    B, T, D = x.shape
    depth = mix_norm_g.shape[0]
    assert depth >= 1
    cos_slab, sin_slab = _rope_slabs(T)
    h = x
    for layer in range(depth):
        if layer % 2 == 0:
            e = layer // 2
            outs, ws = _even_mixer(h, mix_norm_g[layer], even_w_in[e], even_cmp_pe[e],
                                   even_cmp_w1[e], even_cmp_w2[e], even_w_out[e], cos_slab, sin_slab)
        else:
            o = layer // 2
            lambda_init = 0.8 - 0.6 * math.exp(-0.3 * layer)
            outs, ws = _odd_mixer(h, mix_norm_g[layer], odd_w_in[o], odd_lambda[o], odd_subln_g[o],
                                  odd_w_out[o], cos_slab, sin_slab, lambda_init)
        h2 = _layer_tail(h.reshape(B * T, D), outs, ws, mlp_norm_g[layer],
                         mlp_w_up[layer].astype(BF16), mlp_w_down[layer].astype(BF16),
                         final_g=final_norm_g if layer == depth - 1 else None)
        h = h2.reshape(B, T, D)
    return h
```

```python
import functools
import math

import numpy as np
import jax
import jax.numpy as jnp
from jax import lax
from jax.experimental import pallas as pl
from jax.experimental.pallas import tpu as pltpu

HEAD_DIM = 64
HALF = HEAD_DIM // 2
LANES = 128
ROPE_THETA = 10000.0
NORM_EPS = 1e-6
SCALE = HEAD_DIM ** -0.5

A_HEADS = 8
IDX_HEADS = 4
DSA_TOPK = 256
B_HEADS = 8
B_KV_GROUPS = 2
B_PER_GROUP = B_HEADS // B_KV_GROUPS
CMP_BLOCK = 32
CMP_STRIDE = 16
CMP_HIDDEN = 256
SLC_BLOCK = 64
SLC_TOPN = 16
WINDOW = 512
C_HEADS = 8

KV_TILE = 512
SEL_Q_TILE = 128
DSA_Q_TILE = 256
NSA_Q_TILE = 256
VW_TILE = 128

LOG2E = math.log2(math.e)
Q_SCALE = SCALE * LOG2E
ONES_ROWS = 16
BIG = 2.0 ** 100

NEG_INF = float("-inf")
M_FLOOR = -1e30
INT_MIN = -(2 ** 31)

VMEM_LIMIT = 56 * 1024 * 1024

BF16 = jnp.bfloat16
F32 = jnp.float32


def _dot(a, b):
    return jnp.dot(a, b, preferred_element_type=F32)


def _params(sem):
    return pltpu.CompilerParams(dimension_semantics=sem, vmem_limit_bytes=VMEM_LIMIT)


def _iota(shape, axis):
    return lax.broadcasted_iota(jnp.int32, shape, axis)


def _pair_cols(base_a, base_b):
    lane = np.arange(LANES)
    half = lane // 64
    which = (lane % 64) // HALF
    i = lane % HALF
    base = np.where(which == 0, base_a, base_b)
    return base + half * HALF + i


def _head_cols(base):
    lane = np.arange(LANES)
    return np.where(lane < HEAD_DIM, base + lane, -1)


_PAIR_D = _pair_cols(0, 0)
_PAIR_WHICH = (np.arange(LANES) % 64) // HALF


def _gather_cols(w, cols):
    cols = np.asarray(cols)
    safe = np.where(cols >= 0, cols, 0)
    g = jnp.take(w, jnp.asarray(safe, dtype=jnp.int32), axis=1)
    return jnp.where(jnp.asarray(cols >= 0)[None, :], g, jnp.zeros_like(g))


def _rope_slabs(T):
    inv = 1.0 / (ROPE_THETA ** (jnp.arange(0, HEAD_DIM, 2, dtype=F32) / HEAD_DIM))
    ang = jnp.arange(T, dtype=F32)[:, None] * inv[None, :]
    cos, sin = jnp.cos(ang), jnp.sin(ang)
    cos_slab = jnp.tile(cos, (1, 4))
    sin_slab = jnp.concatenate([-sin, -sin, sin, sin], axis=1)
    return cos_slab, sin_slab


_KINDS = ("qT", "k", "vT", "vTw", "f", "fT")


def _proj_kernel(x_ref, g_ref, w_ref, cos_ref, sin_ref, *out_refs, plan, kinds, chunk):
    outs = dict(zip(kinds, out_refs))
    x = x_ref[0]
    ms = jnp.mean(x * x, axis=-1, keepdims=True)
    xn = ((x * lax.rsqrt(ms + NORM_EPS)) * g_ref[...]).astype(BF16)
    cos = cos_ref[...]
    sin = sin_ref[...]
    n = len(plan)
    for c0 in range(0, n, chunk):
        c1 = min(c0 + chunk, n)
        r = _dot(xn, w_ref[:, c0 * LANES:c1 * LANES])
        for s in range(c0, c1):
            roped, kind, idx, opt = plan[s]
            y = r[:, (s - c0) * LANES:(s - c0 + 1) * LANES]
            if roped:
                y = y * cos + pltpu.roll(y, 64, 1) * sin
            if kind == "qT":
                if opt is not None:
                    y = y * opt
                outs[kind][0, idx] = y.T.astype(BF16)
            elif kind == "k":
                outs[kind][0, idx] = y.astype(BF16)
            elif kind == "vT":
                if opt == "ones_row_64":
                    y = jnp.where(_iota(y.shape, 1) == HEAD_DIM, 1.0, y)
                    outs[kind][0, idx, 0] = y.T.astype(BF16)
                else:
                    outs[kind][0, idx, 0, 0:LANES, :] = y.T.astype(BF16)
                    outs[kind][0, idx, 0, LANES:, :] = jnp.ones((ONES_ROWS, y.shape[0]), BF16)
            elif kind == "vTw":
                y_t = jnp.where(_iota(y.shape, 1) == HEAD_DIM, 1.0, y).T.astype(BF16)
                for sub in range(y.shape[0] // VW_TILE):
                    outs[kind][0, idx, sub] = y_t[:, sub * VW_TILE:(sub + 1) * VW_TILE]
            elif kind == "f":
                outs[kind][0, idx] = y
            else:
                outs[kind][0, idx] = y.T


def _project(x, g, w, cos_slab, sin_slab, plan, chunk=4):
    B, T, D = x.shape
    tm = KV_TILE
    assert T % tm == 0
    count = {k: sum(1 for p in plan if p[1] == k) for k in _KINDS}
    kinds = tuple(k for k in _KINDS if count[k])
    out_specs, out_shape = [], []
    for k in kinds:
        n = count[k]
        if k in ("qT", "fT"):
            out_specs.append(pl.BlockSpec((1, n, LANES, tm), lambda b, i: (b, 0, 0, i)))
            out_shape.append(jax.ShapeDtypeStruct((B, n, LANES, T), BF16 if k == "qT" else F32))
        elif k in ("k", "f"):
            out_specs.append(pl.BlockSpec((1, n, tm, LANES), lambda b, i: (b, 0, i, 0)))
            out_shape.append(jax.ShapeDtypeStruct((B, n, T, LANES), BF16 if k == "k" else F32))
        elif k == "vTw":
            sub = tm // VW_TILE
            out_specs.append(pl.BlockSpec((1, n, sub, LANES, VW_TILE), lambda b, i: (b, 0, i, 0, 0)))
            out_shape.append(jax.ShapeDtypeStruct((B, n, T // VW_TILE, LANES, VW_TILE), BF16))
        else:
            wide = any(p[1] == "vT" and p[3] != "ones_row_64" for p in plan)
            rows = LANES + ONES_ROWS if wide else LANES
            out_specs.append(pl.BlockSpec((1, n, 1, rows, tm), lambda b, i: (b, 0, i, 0, 0)))
            out_shape.append(jax.ShapeDtypeStruct((B, n, T // tm, rows, tm), BF16))
    kern = functools.partial(_proj_kernel, plan=tuple(plan), kinds=kinds, chunk=chunk)
    outs = pl.pallas_call(
        kern,
        grid=(B, T // tm),
        in_specs=[
            pl.BlockSpec((1, tm, D), lambda b, i: (b, i, 0)),
            pl.BlockSpec((1, D), lambda b, i: (0, 0)),
            pl.BlockSpec((D, len(plan) * LANES), lambda b, i: (0, 0)),
            pl.BlockSpec((tm, LANES), lambda b, i: (i, 0)),
            pl.BlockSpec((tm, LANES), lambda b, i: (i, 0)),
        ],
        out_specs=out_specs,
        out_shape=out_shape,
        compiler_params=_params(("parallel", "parallel")),
        name="norm_proj_rope",
    )(x, g.reshape(1, D), w, cos_slab, sin_slab)
    return dict(zip(kinds, outs))


def _head_rows(slab_t, which):
    row = _iota(slab_t.shape, 0)
    keep = ((row % 64) // HALF) == which
    return jnp.where(keep, slab_t, jnp.zeros_like(slab_t))


def _tree(op, xs):
    while len(xs) > 1:
        xs = [op(xs[i], xs[i + 1]) if i + 1 < len(xs) else xs[i] for i in range(0, len(xs), 2)]
    return xs[0]


def _fold_rows_max(x, ways=4):
    rows, n = x.shape
    per = rows // ways
    parts = [jnp.max(x[i * per:(i + 1) * per].reshape(per // 8, 8, n), axis=0) for i in range(ways)]
    return _tree(jnp.maximum, parts)


ROW_BLOCK = 64


def _online_update(s_sc, p_sc, v_aug, m_ref, acc_ref):
    tk, n = s_sc.shape
    m_prev = m_ref[...]
    m_tile = jnp.max(_fold_rows_max(s_sc[...]), axis=0, keepdims=True)
    m_new = jnp.maximum(m_prev, m_tile)
    alpha = jnp.exp2(m_prev - m_new)
    for r in range(tk // ROW_BLOCK):
        rows = slice(r * ROW_BLOCK, (r + 1) * ROW_BLOCK)
        p_sc[rows, :] = jnp.exp2(s_sc[rows, :] - m_new).astype(BF16)
    acc_ref[...] = alpha * acc_ref[...] + _dot(v_aug, p_sc[...])
    m_ref[...] = m_new


def _flash_pipeline(n, put_scores, update, buf_a, buf_b, put_last=None):
    if put_last is None:
        put_scores(buf_a, 0)
    else:
        @pl.when(n > 0)
        def _():
            put_scores(buf_a, 0)

    def pair(p, carry):
        j = 2 * p
        put_scores(buf_b, j + 1)
        update(buf_a, j)

        @pl.when(j + 2 < n)
        def _():
            put_scores(buf_a, j + 2)
            update(buf_b, j + 1)

        return carry

    lax.fori_loop(0, n // 2, pair, 0)
    in_b = jnp.logical_and(n > 0, n % 2 == 0)
    in_a = n % 2 == 1

    def finish(cur, other):
        if put_last is not None:
            put_last(other, n)
        update(cur, n - 1)
        if put_last is not None:
            update(other, n)

    pl.when(in_b)(lambda: finish(buf_b, buf_a))
    pl.when(in_a)(lambda: finish(buf_a, buf_b))
    if put_last is not None:
        @pl.when(n == 0)
        def _():
            put_last(buf_a, 0)
            update(buf_a, 0)


def _softmax_keys(s):
    m = jnp.max(s, axis=0, keepdims=True)
    m = jnp.where(m > NEG_INF, m, 0.0)
    e = jnp.exp2(s - m)
    return e / jnp.maximum(jnp.sum(e, axis=0, keepdims=True), 1e-30)


def _init_flash(m_ref, acc_ref):
    m_ref[...] = jnp.full(m_ref.shape, M_FLOOR, F32)
    acc_ref[...] = jnp.zeros(acc_ref.shape, F32)


DIFF_HEADS_PER_STEP = 2


def _diff_attn_kernel(lam_ref, q_ref, k_ref, v_ref, g_ref, o_ref, m_sc, acc_sc, sa_sc, sb_sc,
                      p_sc, *, tq, lambda_init):
    qi = pl.program_id(2)
    HS = DIFF_HEADS_PER_STEP
    qs = [[_head_rows(q_ref[0, hh], c) for c in range(2)] for hh in range(HS)]
    for s in range(2 * HS):
        _init_flash(m_sc.at[s], acc_sc.at[s])

    def put_scores(buf, j, masked=False):
        for hh in range(HS):
            k = k_ref[0, hh, pl.ds(pl.multiple_of(j * tq, tq), tq), :]
            for c in range(2):
                s = _dot(k, qs[hh][c])
                if masked:
                    s = jnp.where(_iota(s.shape, 0) <= _iota(s.shape, 1), s, NEG_INF)
                buf[2 * hh + c] = s

    def put_diagonal(buf, j):
        put_scores(buf, j, masked=True)

    def update(buf, j):
        for hh in range(HS):
            v_aug = v_ref[0, hh, j]
            for c in range(2):
                s = 2 * hh + c
                _online_update(buf.at[s], p_sc.at[s], v_aug, m_sc.at[s], acc_sc.at[s])

    _flash_pipeline(qi, put_scores, update, sa_sc, sb_sc, put_last=put_diagonal)

    lam = lam_ref[...]
    s01 = jnp.sum(lam[0:1] * lam[1:2], axis=-1, keepdims=True)
    s23 = jnp.sum(lam[2:3] * lam[3:4], axis=-1, keepdims=True)
    lam_val = jnp.exp(s01) - jnp.exp(s23) + lambda_init
    for hh in range(HS):
        a0, a1 = acc_sc[2 * hh], acc_sc[2 * hh + 1]
        o0 = a0[0:LANES] / jnp.maximum(a0[LANES:LANES + 1], 1e-30)
        o1 = a1[0:LANES] / jnp.maximum(a1[LANES:LANES + 1], 1e-30)
        o = o0 - lam_val * o1
        y = o * lax.rsqrt(jnp.mean(o * o, axis=0, keepdims=True) + NORM_EPS)
        y = (y * g_ref[...]) * (1.0 - lambda_init)
        o_ref[0, :, hh * LANES:(hh + 1) * LANES] = y.T.astype(o_ref.dtype)


def _diff_attention(sl, lam, subln_g, lambda_init):
    q_t, k, v_t = sl["qT"], sl["k"], sl["vT"]
    B, H, T, _ = k.shape
    tq = KV_TILE
    nk = T // tq
    kern = functools.partial(_diff_attn_kernel, tq=tq, lambda_init=lambda_init)
    HS = DIFF_HEADS_PER_STEP
    ns = 2 * HS
    return pl.pallas_call(
        kern,
        grid=(B, H // HS, T // tq),
        in_specs=[
            pl.BlockSpec((4, HEAD_DIM), lambda b, h, i: (0, 0)),
            pl.BlockSpec((1, HS, LANES, tq), lambda b, h, i: (b, h, 0, i)),
            pl.BlockSpec((1, HS, T, LANES), lambda b, h, i: (b, h, 0, 0)),
            pl.BlockSpec((1, HS, nk, LANES + ONES_ROWS, tq), lambda b, h, i: (b, h, 0, 0, 0)),
            pl.BlockSpec((LANES, 1), lambda b, h, i: (0, 0)),
        ],
        out_specs=pl.BlockSpec((1, tq, HS * LANES), lambda b, h, i: (b, i, h)),
        out_shape=jax.ShapeDtypeStruct((B, T, H * LANES), BF16),
        scratch_shapes=[
            pltpu.VMEM((ns, 1, tq), F32),
            pltpu.VMEM((ns, LANES + ONES_ROWS, tq), F32),
            pltpu.VMEM((ns, tq, tq), F32),
            pltpu.VMEM((ns, tq, tq), F32),
            pltpu.VMEM((ns, tq, tq), BF16),
        ],
        compiler_params=_params(("parallel", "parallel", "arbitrary")),
        name="diff_attention",
    )(lam, q_t, k, v_t, subln_g.reshape(LANES, 1))


def _dsa_select_kernel(qi_ref, ki_ref, misc_ref, tril_ref, bias_ref, key_sc, *, tq, ck, nk, topk):
    t0 = pl.program_id(1) * tq
    nvalid = (t0 + tq - 1) // ck + 1
    w = misc_ref[0, 0]
    qh = jnp.concatenate([_head_rows(qi_ref[0, h // 2], h % 2) for h in range(IDX_HEADS)], axis=1)
    t_q = t0 + _iota((ck, tq), 1)

    def causal(c):
        return (c * ck + _iota((ck, tq), 0)) <= t_q

    def fill(c):
        kk = ki_ref[0, 0, c * ck:(c + 1) * ck, :]
        r = _dot(kk, qh)
        score = jnp.zeros((ck, tq), F32)
        for h in range(IDX_HEADS):
            score = score + w[h:h + 1, :] * jnp.maximum(r[:, h * tq:(h + 1) * tq], 0.0)
        bits = pltpu.bitcast(score, jnp.int32)
        key = bits ^ ((bits >> 31) & jnp.int32(0x7FFFFFFF))
        key = jnp.where(score == 0.0, 0, key)
        key_sc[c] = jnp.where(causal(c), key, INT_MIN)

    kf = float(topk)

    def select_threshold(n):
        for c in range(n):
            fill(c)

        def count(pred):
            acc = jnp.zeros((ck // 8, tq), F32)
            for c in range(n):
                ind = pred(key_sc[c], c)
                acc = acc + jnp.sum(ind.reshape(8, ck // 8, tq), axis=0)
            return jnp.sum(acc, axis=0, keepdims=True)

        zero = jnp.zeros((1, tq), jnp.int32)
        ans = jnp.where(count(lambda kc, c: jnp.where(kc >= zero, 1.0, 0.0)) >= kf, 0, INT_MIN)

        def bit_step(i, ans):
            cand = ans | (jnp.int32(1) << (30 - i))
            cnt = count(lambda kc, c: jnp.where(kc >= cand, 1.0, 0.0))
            return jnp.where(cnt >= kf, cand, ans)

        ans = lax.fori_loop(0, 31, bit_step, ans)
        cnt_gt = count(lambda kc, c: jnp.where(kc > ans, 1.0, 0.0))
        cnt_eq = count(lambda kc, c: jnp.where(kc == ans, jnp.where(causal(c), 1.0, 0.0), 0.0))
        return ans, cnt_gt, cnt_eq

    ans, cnt_gt, cnt_eq = lax.switch(
        nvalid - 1, [functools.partial(select_threshold, n) for n in range(1, nk + 1)])
    need = kf - cnt_gt
    has_tie = jnp.max(cnt_eq - need) > 0.0

    @pl.when(jnp.logical_not(has_tie))
    def _():
        def emit(c, carry):
            sel = jnp.where(key_sc[c] >= ans, jnp.where(causal(c), 0.0, NEG_INF), NEG_INF)
            bias_ref[0, 0, c] = sel.astype(BF16)
            return carry
        lax.fori_loop(0, nvalid, emit, 0)

    @pl.when(has_tie)
    def _():
        def emit(c, carry):
            kc = key_sc[c]
            eq = jnp.where(kc == ans, jnp.where(causal(c), 1.0, 0.0), 0.0)
            pre = _dot(tril_ref[...], eq.astype(BF16)) + carry
            take = jnp.where(kc > ans, 1.0, jnp.where(pre < need, eq, 0.0))
            sel = jnp.where(take > 0.5, jnp.where(causal(c), 0.0, NEG_INF), NEG_INF)
            bias_ref[0, 0, c] = sel.astype(BF16)
            return carry + jnp.sum(eq, axis=0, keepdims=True)
        lax.fori_loop(0, nvalid, emit, jnp.zeros((1, tq), F32))

    def blank(c, carry):
        bias_ref[0, 0, c] = jnp.full((ck, tq), NEG_INF, BF16)
        return carry

    lax.fori_loop(nvalid, nk, blank, 0)


def _dsa_select(sl, lay):
    q_t, k, misc_t = sl["qT"], sl["k"], sl["fT"]
    B, _, T, _ = k.shape
    tq, ck = SEL_Q_TILE, KV_TILE
    nk = T // ck
    topk = min(DSA_TOPK, T // 4)
    tril = jnp.asarray(np.tril(np.ones((ck, ck), np.float32), -1), BF16)
    kern = functools.partial(_dsa_select_kernel, tq=tq, ck=ck, nk=nk, topk=topk)
    return pl.pallas_call(
        kern,
        grid=(B, T // tq),
        in_specs=[
            pl.BlockSpec((1, 2, LANES, tq), lambda b, i: (b, lay["qi"] // 2, 0, i)),
            pl.BlockSpec((1, 1, T, LANES), lambda b, i: (b, lay["ki"], 0, 0)),
            pl.BlockSpec((1, 1, LANES, tq), lambda b, i: (b, lay["misc"], 0, i)),
            pl.BlockSpec((ck, ck), lambda b, i: (0, 0)),
        ],
        out_specs=pl.BlockSpec((1, 1, nk, ck, tq), lambda b, i: (b, i, 0, 0, 0)),
        out_shape=jax.ShapeDtypeStruct((B, T // tq, nk, ck, tq), BF16),
        scratch_shapes=[pltpu.VMEM((nk, ck, tq), jnp.int32)],
        compiler_params=_params(("parallel", "parallel")),
        name="dsa_select",
    )(q_t, k, misc_t, tril)


def _store_head_pairs(o_ref, o, n_heads, tq, col0=0, row0=0):
    for p in range(n_heads // 2):
        even = o[0:HEAD_DIM, (2 * p) * tq:(2 * p + 1) * tq]
        odd = o[0:HEAD_DIM, (2 * p + 1) * tq:(2 * p + 2) * tq]
        pair = jnp.concatenate([even, odd], axis=0)
        cols = slice(col0 + p * LANES, col0 + (p + 1) * LANES)
        o_ref[0, row0:row0 + tq, cols] = pair.T.astype(o_ref.dtype)


def _normalized(acc):
    return acc[0:HEAD_DIM] / jnp.maximum(acc[HEAD_DIM:HEAD_DIM + 1], 1e-30)


def _dsa_attn_kernel(q_ref, k_ref, v_ref, bias_ref, o_ref, qst, m_sc, acc_sc, sa_sc, sb_sc,
                     p_sc, *, tq, tk):
    qi = pl.program_id(1)
    H = A_HEADS
    S = sa_sc.shape[0]
    nsub = bias_ref.shape[1] // S
    for s in range(S):
        for h in range(H):
            qst[s, :, h * tq:(h + 1) * tq] = _head_rows(q_ref[0, h // 2, :, s * tq:(s + 1) * tq], h % 2)
        _init_flash(m_sc.at[s], acc_sc.at[s])
    nkv = (qi * S * tq) // tk + 1

    def put_scores(buf, j):
        k = k_ref[0, 0, pl.ds(pl.multiple_of(j * tk, tk), tk), :]
        for s in range(S):
            parts = [bias_ref[0, s * nsub + u, j].astype(F32) for u in range(nsub)]
            b = parts[0] if len(parts) == 1 else jnp.concatenate(parts, axis=1)
            buf[s] = _dot(k, qst[s]) + jnp.concatenate([b] * H, axis=1)

    def update(buf, j):
        for s in range(S):
            _online_update(buf.at[s], p_sc.at[s], v_ref[0, 0, j], m_sc.at[s], acc_sc.at[s])

    _flash_pipeline(nkv, put_scores, update, sa_sc, sb_sc)
    for s in range(S):
        _store_head_pairs(o_ref, _normalized(acc_sc[s]), H, tq, row0=s * tq)


def _dsa_attention(sl, bias, lay):
    q_t, k, v_t = sl["qT"], sl["k"], sl["vT"]
    B, _, T, _ = k.shape
    tq, tk = DSA_Q_TILE, KV_TILE
    nk = T // tk
    H = A_HEADS
    N = H * tq
    sel_tq = bias.shape[4]
    S = tk // tq
    nsub = S * tq // sel_tq
    kern = functools.partial(_dsa_attn_kernel, tq=tq, tk=tk)
    return pl.pallas_call(
        kern,
        grid=(B, T // (S * tq)),
        in_specs=[
            pl.BlockSpec((1, H // 2, LANES, S * tq), lambda b, i: (b, lay["qA"] // (H // 2), 0, i)),
            pl.BlockSpec((1, 1, T, LANES), lambda b, i: (b, lay["kA"], 0, 0)),
            pl.BlockSpec((1, 1, nk, LANES, tk), lambda b, i: (b, lay["vA"], 0, 0, 0)),
            pl.BlockSpec((1, nsub, nk, tk, sel_tq), lambda b, i: (b, i, 0, 0, 0)),
        ],
        out_specs=pl.BlockSpec((1, S * tq, H * HEAD_DIM), lambda b, i: (b, i, 0)),
        out_shape=jax.ShapeDtypeStruct((B, T, H * HEAD_DIM), BF16),
        scratch_shapes=[
            pltpu.VMEM((S, LANES, N), BF16),
            pltpu.VMEM((S, 1, N), F32),
            pltpu.VMEM((S, LANES, N), F32),
            pltpu.VMEM((S, tk, N), F32),
            pltpu.VMEM((S, tk, N), F32),
            pltpu.VMEM((S, tk, N), BF16),
        ],
        compiler_params=_params(("parallel", "arbitrary")),
        name="dsa_attention",
    )(q_t, k, v_t, bias)


def _compress_kernel(x_ref, pe_ref, w1_ref, w2_ref, o_ref, ot_ref, *, n_ch):
    x = x_ref[0, 0]
    xt = (x + pe_ref[0, 0]).astype(BF16)
    xb = (x + pe_ref[0, 1]).astype(BF16)
    for g in range(B_KV_GROUPS):
        a = _dot(xt, w1_ref[0, g, 0])
        b = _dot(xb, w1_ref[0, g, 1])
        h = jax.nn.gelu(a + pltpu.roll(b, n_ch - 1, 0))
        r = _dot(h.astype(BF16), w2_ref[0])
        o_ref[0, 0, g] = r.astype(BF16)
        ot_ref[0, 0, g] = r.T.astype(BF16)


def _compress(flat, pe, w1, w2):
    B, _, n_ch, W = flat.shape
    G = B_KV_GROUPS
    kern = functools.partial(_compress_kernel, n_ch=n_ch)
    return pl.pallas_call(
        kern,
        grid=(2, B),
        in_specs=[
            pl.BlockSpec((1, 1, n_ch, W), lambda s, b: (b, s, 0, 0)),
            pl.BlockSpec((1, 2, 1, W), lambda s, b: (s, 0, 0, 0)),
            pl.BlockSpec((1, G, 2, W, CMP_HIDDEN), lambda s, b: (s, 0, 0, 0, 0)),
            pl.BlockSpec((1, CMP_HIDDEN, LANES), lambda s, b: (s, 0, 0)),
        ],
        out_specs=[
            pl.BlockSpec((1, 1, G, n_ch, LANES), lambda s, b: (s, b, 0, 0, 0)),
            pl.BlockSpec((1, 1, G, LANES, n_ch), lambda s, b: (s, b, 0, 0, 0)),
        ],
        out_shape=[
            jax.ShapeDtypeStruct((2, B, G, n_ch, LANES), BF16),
            jax.ShapeDtypeStruct((2, B, G, LANES, n_ch), BF16),
        ],
        compiler_params=_params(("parallel", "parallel")),
        name="nsa_compress",
    )(flat, pe, w1, w2)


def _nsa_kernel(q_ref, kc_ref, vc_ref, ks_ref, vs_ref, kw_ref, vw_ref, misc_ref, ov_ref, ex_ref,
                cz_ref, wz_ref, o_ref, qst, m_sc, acc_sc, sa_sc, sb_sc, p_sc, mw_sc, accw_sc,
                sw_sc, pw_sc, oc_sc, *, tq, tk, T, n_s, n_sel, gate_row0):
    qi = pl.program_id(1)
    t0 = qi * tq
    G, J = B_KV_GROUPS, B_PER_GROUP
    N = J * tq
    for g in range(G):
        for j in range(J):
            qst[g, 0:LANES, j * tq:(j + 1) * tq] = _head_rows(q_ref[0, 2 * g + j // 2], j % 2)

    def q_time(shape):
        return t0 + (_iota(shape, 1) % tq)

    wlen = min(WINDOW + tq, T)
    wstart = pl.multiple_of(jnp.maximum(t0 - WINDOW, 0), tq)
    n_wt = wlen // tq
    wt = wstart // VW_TILE
    wz = wz_ref[jnp.minimum(qi, n_wt - 1)].astype(F32)
    wz = jnp.concatenate([wz] * J, axis=1)
    for g in range(G):
        kw = kw_ref[0, g, pl.ds(wstart, wlen), :]
        vw = jnp.concatenate([vw_ref[0, g, wt + i] for i in range(wlen // VW_TILE)], axis=1)
        sw_sc[g] = _dot(kw, qst[g, 0:LANES, :]) + wz
        _init_flash(mw_sc.at[g], accw_sc.at[g])
        _online_update(sw_sc.at[g], pw_sc.at[g], vw, mw_sc.at[g], accw_sc.at[g])

    rows = -(-n_s // 8) * 8
    blk = _iota((rows, tq), 0)
    t_q = t0 + _iota((rows, tq), 1)
    cur = t_q // SLC_BLOCK
    forced = (blk == 0) | (blk == cur) | (blk == cur - 1)
    admissible = blk * SLC_BLOCK <= t_q
    blk8 = _iota((8, tq), 0)
    for g in range(G):
        q = qst[g, 0:LANES, :]
        kc = kc_ref[0, 0, g]
        n_ch = kc.shape[0]
        s_c = _dot(kc, q)
        cmp_end = _iota((n_ch, N), 0) * CMP_STRIDE + (CMP_BLOCK - 1)
        s_c = jnp.where(cmp_end <= q_time((n_ch, N)), s_c, NEG_INF)
        p_c = _softmax_keys(s_c)
        oc_sc[g] = _dot(vc_ref[0, 0, g], p_c.astype(BF16))[0:HEAD_DIM]
        psum = p_c[:, 0:tq]
        for j in range(1, J):
            psum = psum + p_c[:, j * tq:(j + 1) * tq]
        p_hi = psum.astype(BF16)
        p_lo = (psum - p_hi.astype(F32)).astype(BF16)
        imp = _dot(ov_ref[...], p_hi) + _dot(ov_ref[...], p_lo)
        sc = jnp.where(forced, jnp.inf, imp[0:rows])
        sc = jnp.where(admissible, sc, NEG_INF)
        groups = [sc[8 * r:8 * r + 8] for r in range(rows // 8)]
        ranks = [jnp.zeros((8, tq), F32) for _ in groups]
        for m in range(n_s):
            cm = sc[m:m + 1, :]
            for r, grp in enumerate(groups):
                gt = jnp.where(cm > grp, 1.0, 0.0)
                ge = jnp.where(cm >= grp, 1.0, 0.0)
                if 8 * r + 7 <= m:
                    first = gt
                elif 8 * r > m:
                    first = ge
                else:
                    first = jnp.where(blk8 + 8 * r > m, ge, gt)
                ranks[r] = ranks[r] + first
        rank = jnp.concatenate(ranks, axis=0)
        drop = jnp.where(rank < float(n_sel), 0.0, 1.0)
        if rows < LANES:
            drop = jnp.concatenate([drop, jnp.ones((LANES - rows, tq), F32)], axis=0)
        drop = drop.astype(BF16)
        qst[g, LANES:2 * LANES, :] = jnp.concatenate([drop] * J, axis=1)

    jd = t0 // tk
    off = (t0 - jd * tk) // tq
    n_off = tk // tq
    for g in range(G):
        _init_flash(m_sc.at[g], acc_sc.at[g])

    def put_scores(buf, j):
        cz = cz_ref[jnp.where(j == jd, off, n_off)].astype(F32)
        cz = jnp.concatenate([cz] * J, axis=1)
        for g in range(G):
            k = ks_ref[0, g, pl.ds(pl.multiple_of(j * tk, tk), tk), :]
            lhs = jnp.concatenate([k, ex_ref[j]], axis=1)
            buf[g] = _dot(lhs, qst[g]) + cz

    def update(buf, j):
        for g in range(G):
            _online_update(buf.at[g], p_sc.at[g], vs_ref[0, g, j], m_sc.at[g], acc_sc.at[g])

    _flash_pipeline(jd + 1, put_scores, update, sa_sc, sb_sc)

    gates = jax.nn.sigmoid(misc_ref[0, 0])
    for g in range(G):
        def gate_row(c, g=g):
            rws = [gate_row0 + (g * J + j) * 3 + c for j in range(J)]
            return jnp.concatenate([gates[r:r + 1, :] for r in rws], axis=1)

        o_s = _normalized(acc_sc[g])
        o_w = _normalized(accw_sc[g])
        o = gate_row(0) * oc_sc[g] + (gate_row(1) * o_s + gate_row(2) * o_w)
        _store_head_pairs(o_ref, o, J, tq, col0=g * J * HEAD_DIM)


def _nsa_attention(sl, cmp_k, cmp_vt, lay):
    q_t, k, v_t, misc_t = sl["qT"], sl["k"], sl["vT"], sl["fT"]
    B, _, T, _ = k.shape
    G, J = B_KV_GROUPS, B_PER_GROUP
    tq, tk = NSA_Q_TILE, KV_TILE
    nk = T // tk
    n_ch = T // CMP_STRIDE
    n_c = n_ch - CMP_BLOCK // CMP_STRIDE + 1
    n_s = T // SLC_BLOCK
    n_sel = min(SLC_TOPN, n_s)
    assert n_s <= LANES
    c0 = np.arange(n_ch) * CMP_STRIDE
    s0 = np.arange(LANES) * SLC_BLOCK
    ov = ((c0[None, :] < s0[:, None] + SLC_BLOCK) & (c0[None, :] + CMP_BLOCK > s0[:, None]))
    ov = ov & (np.arange(n_ch)[None, :] < n_c) & (np.arange(LANES)[:, None] < n_s)
    ov = jnp.asarray(ov.astype(np.float32), BF16)
    pos = np.arange(T).reshape(nk, tk, 1)
    ex = (pos // SLC_BLOCK == np.arange(LANES).reshape(1, 1, LANES))
    ex = jnp.asarray(ex.astype(np.float32) * -BIG, BF16)
    n_off = tk // tq
    kp = np.arange(tk).reshape(1, tk, 1)
    tl = np.arange(tq).reshape(1, 1, tq) + np.arange(n_off + 1).reshape(n_off + 1, 1, 1) * tq
    cz = np.where((kp <= tl) | (np.arange(n_off + 1).reshape(-1, 1, 1) == n_off), 0.0, -BIG)
    cz = jnp.asarray(cz.astype(np.float32), BF16)
    wlen = min(WINDOW + tq, T)
    n_wt = wlen // tq
    assert wlen % tq == 0 and T >= wlen
    kp = np.arange(wlen).reshape(1, wlen, 1)
    tl = np.arange(tq).reshape(1, 1, tq)
    early = kp <= tl + np.arange(n_wt).reshape(n_wt, 1, 1) * tq
    late = (kp > tl) & (kp <= tl + WINDOW)
    band = np.where(np.arange(n_wt).reshape(n_wt, 1, 1) == n_wt - 1, late, early)
    wz = jnp.asarray(np.where(band, 0.0, -BIG).astype(np.float32), BF16)
    kern = functools.partial(_nsa_kernel, tq=tq, tk=tk, T=T, n_s=n_s, n_sel=n_sel,
                             gate_row0=lay["gate_row0"])
    N = J * tq
    for name in ("qB", "ks", "vs", "kw", "vw"):
        assert lay[name] % (2 * G if name == "qB" else G) == 0
    kslab = lambda off: pl.BlockSpec((1, G, T, LANES), lambda b, i: (b, off // G, 0, 0))
    return pl.pallas_call(
        kern,
        grid=(B, T // tq),
        in_specs=[
            pl.BlockSpec((1, 2 * G, LANES, tq), lambda b, i: (b, lay["qB"] // (2 * G), 0, i)),
            pl.BlockSpec((1, 1, G, n_ch, LANES), lambda b, i: (0, b, 0, 0, 0)),
            pl.BlockSpec((1, 1, G, LANES, n_ch), lambda b, i: (1, b, 0, 0, 0)),
            kslab(lay["ks"]),
            pl.BlockSpec((1, G, nk, LANES, tk), lambda b, i: (b, lay["vs"] // G, 0, 0, 0)),
            kslab(lay["kw"]),
            pl.BlockSpec((1, G, T // VW_TILE, LANES, VW_TILE),
                         lambda b, i: (b, lay["vw"] // G, 0, 0, 0)),
            pl.BlockSpec((1, 1, LANES, tq), lambda b, i: (b, lay["misc"], 0, i)),
            pl.BlockSpec((LANES, n_ch), lambda b, i: (0, 0)),
            pl.BlockSpec((nk, tk, LANES), lambda b, i: (0, 0, 0)),
            pl.BlockSpec((n_off + 1, tk, tq), lambda b, i: (0, 0, 0)),
            pl.BlockSpec((n_wt, wlen, tq), lambda b, i: (0, 0, 0)),
        ],
        out_specs=pl.BlockSpec((1, tq, B_HEADS * HEAD_DIM), lambda b, i: (b, i, 0)),
        out_shape=jax.ShapeDtypeStruct((B, T, B_HEADS * HEAD_DIM), BF16),
        scratch_shapes=[
            pltpu.VMEM((G, 2 * LANES, N), BF16),
            pltpu.VMEM((G, 1, N), F32),
            pltpu.VMEM((G, LANES, N), F32),
            pltpu.VMEM((G, tk, N), F32),
            pltpu.VMEM((G, tk, N), F32),
            pltpu.VMEM((G, tk, N), BF16),
            pltpu.VMEM((G, 1, N), F32),
            pltpu.VMEM((G, LANES, N), F32),
            pltpu.VMEM((G, wlen, N), F32),
            pltpu.VMEM((G, wlen, N), BF16),
            pltpu.VMEM((G, HEAD_DIM, N), F32),
        ],
        compiler_params=_params(("parallel", "arbitrary")),
        name="nsa_attention",
    )(q_t, cmp_k, cmp_vt, k, v_t, k, sl["vTw"], misc_t, ov, ex, cz, wz)


def _rms(x, g):
    ms = jnp.mean(x * x, axis=-1, keepdims=True)
    return (x * lax.rsqrt(ms + NORM_EPS)) * g


def _layer_tail_kernel(*refs, n_in, final_norm):
    h_ref = refs[0]
    o_refs = refs[1:1 + n_in]
    w_refs = refs[1 + n_in:1 + 2 * n_in]
    g_ref, wu_ref, wd_ref = refs[1 + 2 * n_in:4 + 2 * n_in]
    rest = refs[4 + 2 * n_in:]
    if final_norm:
        fg_ref, out_ref, xn_sc = rest
    else:
        out_ref, xn_sc = rest
    f = pl.program_id(1)

    @pl.when(f == 0)
    def _():
        mix = _dot(o_refs[0][...], w_refs[0][...])
        for i in range(1, n_in):
            mix = mix + _dot(o_refs[i][...], w_refs[i][...])
        x = h_ref[...] + mix
        xn_sc[...] = _rms(x, g_ref[...]).astype(BF16)
        out_ref[...] = x

    u = _dot(xn_sc[...], wu_ref[...])
    a = jnp.square(jnp.maximum(u, 0.0)).astype(BF16)
    out_ref[...] += _dot(a, wd_ref[...])

    if final_norm:
        @pl.when(f == pl.num_programs(1) - 1)
        def _():
            out_ref[...] = _rms(out_ref[...], fg_ref[...])


def _layer_tail(h2, outs, ws, g, w_up, w_down, final_g=None, tm=1024, tf=1024):
    N, D = h2.shape
    F = w_up.shape[1]
    tm = min(tm, N)
    n_in = len(outs)
    in_specs = [pl.BlockSpec((tm, D), lambda i, f: (i, 0))]
    in_specs += [pl.BlockSpec((tm, o.shape[1]), lambda i, f: (i, 0)) for o in outs]
    in_specs += [pl.BlockSpec(w.shape, lambda i, f: (0, 0)) for w in ws]
    in_specs += [
        pl.BlockSpec((1, D), lambda i, f: (0, 0)),
        pl.BlockSpec((D, tf), lambda i, f: (0, f)),
        pl.BlockSpec((tf, D), lambda i, f: (f, 0)),
    ]
    args = [h2, *outs, *ws, g.reshape(1, D), w_up, w_down]
    if final_g is not None:
        in_specs.append(pl.BlockSpec((1, D), lambda i, f: (0, 0)))
        args.append(final_g.reshape(1, D))
    kern = functools.partial(_layer_tail_kernel, n_in=n_in, final_norm=final_g is not None)
    return pl.pallas_call(
        kern,
        grid=(N // tm, F // tf),
        in_specs=in_specs,
        out_specs=pl.BlockSpec((tm, D), lambda i, f: (i, 0)),
        out_shape=jax.ShapeDtypeStruct((N, D), F32),
        scratch_shapes=[pltpu.VMEM((tm, D), BF16)],
        compiler_params=_params(("parallel", "arbitrary")),
        name="out_proj_mlp",
    )(*args)


def _even_layout():
    offs = {}
    o = 0
    for name, size in (("qa", 512), ("ka", 64), ("va", 64), ("qi", 256), ("ki", 64), ("wi", 4),
                       ("qb", 512), ("kvb", 768), ("gb", 24)):
        offs[name] = o
        o += size
    kvb = lambda which, g: offs["kvb"] + (which * B_KV_GROUPS + g) * HEAD_DIM
    cols, plan, lay = [], [], {}
    n = {k: 0 for k in _KINDS}

    def add(c, roped, kind, opt=None):
        cols.append(c)
        plan.append((roped, kind, n[kind], opt))
        n[kind] += 1
        return n[kind] - 1

    lay["qA"] = n["qT"]
    for p in range(4):
        add(_pair_cols(offs["qa"] + 2 * p * 64, offs["qa"] + (2 * p + 1) * 64), True, "qT", Q_SCALE)
    lay["qB"] = n["qT"]
    for p in range(4):
        add(_pair_cols(offs["qb"] + 2 * p * 64, offs["qb"] + (2 * p + 1) * 64), True, "qT", Q_SCALE)
    lay["qi"] = n["qT"]
    for p in range(2):
        add(_pair_cols(offs["qi"] + 2 * p * 64, offs["qi"] + (2 * p + 1) * 64), True, "qT")
    lay["kA"] = add(_pair_cols(offs["ka"], offs["ka"]), True, "k")
    lay["ki"] = add(_pair_cols(offs["ki"], offs["ki"]), True, "k")
    lay["ks"] = n["k"]
    for g in range(2):
        add(_pair_cols(kvb(2, g), kvb(2, g)), True, "k")
    lay["kw"] = n["k"]
    for g in range(2):
        add(_pair_cols(kvb(4, g), kvb(4, g)), True, "k")
    lay["kc"] = add(_pair_cols(kvb(0, 0), kvb(0, 1)), True, "f")
    lay["vc"] = add(_pair_cols(kvb(1, 0), kvb(1, 1)), False, "f")
    misc = np.full(LANES, -1)
    misc[0:IDX_HEADS] = offs["wi"] + np.arange(IDX_HEADS)
    lay["gate_row0"] = 8
    misc[8:8 + 24] = offs["gb"] + np.arange(24)
    lay["misc"] = add(misc, False, "fT")
    lay["vw"] = n["vTw"]
    for g in range(2):
        add(_head_cols(kvb(5, g)), False, "vTw")
    lay["vs"] = n["vT"]
    for g in range(2):
        add(_head_cols(kvb(3, g)), False, "vT", "ones_row_64")
    lay["vA"] = add(_head_cols(offs["va"]), False, "vT", "ones_row_64")
    return np.concatenate(cols), plan, lay


def _odd_layout():
    cols, plan = [], []
    for h in range(C_HEADS):
        cols.append(_pair_cols(h * 128, h * 128 + 64))
        plan.append((True, "qT", h, Q_SCALE))
    for h in range(C_HEADS):
        cols.append(_pair_cols(1024 + h * 128, 1024 + h * 128 + 64))
        plan.append((True, "k", h, None))
    for h in range(C_HEADS):
        cols.append(2048 + h * 128 + np.arange(LANES))
        plan.append((False, "vT", h, "ones_rows_below"))
    return np.concatenate(cols), plan


def _compress_weights(pe, w1, w2):
    d = _PAIR_D
    which = _PAIR_WHICH
    pe_l = pe[:, :, d]
    pe_l = pe_l.reshape(2, 2, 1, CMP_STRIDE * LANES)
    w1r = w1.astype(BF16).reshape(2, CMP_BLOCK, HEAD_DIM, CMP_HIDDEN)[:, :, d, :]
    per_g = []
    for g in range(B_KV_GROUPS):
        keep = jnp.asarray(which == g)[None, None, :, None]
        per_g.append(jnp.where(keep, w1r, jnp.zeros_like(w1r)))
    w1g = jnp.stack(per_g, axis=1)
    w1g = w1g.reshape(2, B_KV_GROUPS, 2, CMP_STRIDE * LANES, CMP_HIDDEN)
    w2b = w2.astype(BF16)
    w2k = w2b[0][:, d]
    w2v = w2b[1][:, np.arange(LANES) % HEAD_DIM]
    w2l = jnp.stack([w2k, w2v], axis=0)
    return pe_l, w1g, w2l


def _even_mixer(h, norm_g, w_in, cmp_pe, cmp_w1, cmp_w2, w_out, cos_slab, sin_slab):
    B, T, D = h.shape
    cols, plan, lay = _even_layout()
    w = _gather_cols(w_in.astype(BF16), cols)
    sl = _project(h, norm_g, w, cos_slab, sin_slab, plan)

    bias = _dsa_select(sl, lay)
    o_a = _dsa_attention(sl, bias, lay)

    n_ch = T // CMP_STRIDE
    assert (lay["kc"], lay["vc"]) == (0, 1) and sl["f"].shape[1] == 2
    flat = sl["f"].reshape(B, 2, n_ch, CMP_STRIDE * LANES)
    pe_l, w1g, w2l = _compress_weights(cmp_pe, cmp_w1, cmp_w2)
    cmp_k, cmp_vt = _compress(flat, pe_l, w1g, w2l)
    o_b = _nsa_attention(sl, cmp_k, cmp_vt, lay)

    na = A_HEADS * HEAD_DIM
    wo = w_out.astype(BF16)
    return [o_a.reshape(B * T, -1), o_b.reshape(B * T, -1)], [wo[:na], wo[na:]]


def _odd_mixer(h, norm_g, w_in, lam, subln_g, w_out, cos_slab, sin_slab, lambda_init):
    B, T, D = h.shape
    cols, plan = _odd_layout()
    w = _gather_cols(w_in.astype(BF16), cols)
    sl = _project(h, norm_g, w, cos_slab, sin_slab, plan)
    o = _diff_attention(sl, lam, subln_g, lambda_init)
    return [o.reshape(B * T, -1)], [w_out.astype(BF16)]


def kernel(x, mix_norm_g, mlp_norm_g, even_w_in, even_cmp_pe, even_cmp_w1, even_cmp_w2, even_w_out, odd_w_in, odd_lambda, odd_subln_g, odd_w_out, mlp_w_up, mlp_w_down, final_norm_g):
    B, T, D = x.shape
    depth = mix_norm_g.shape[0]
    assert depth >= 1
    cos_slab, sin_slab = _rope_slabs(T)
    h = x
    for layer in range(depth):
        if layer % 2 == 0:
            e = layer // 2
            outs, ws = _even_mixer(h, mix_norm_g[layer], even_w_in[e], even_cmp_pe[e],
                                   even_cmp_w1[e], even_cmp_w2[e], even_w_out[e], cos_slab, sin_slab)
        else:
            o = layer // 2
            lambda_init = 0.8 - 0.6 * math.exp(-0.3 * layer)
            outs, ws = _odd_mixer(h, mix_norm_g[layer], odd_w_in[o], odd_lambda[o], odd_subln_g[o],
                                  odd_w_out[o], cos_slab, sin_slab, lambda_init)
        h2 = _layer_tail(h.reshape(B * T, D), outs, ws, mlp_norm_g[layer],
                         mlp_w_up[layer].astype(BF16), mlp_w_down[layer].astype(BF16),
                         final_g=final_norm_g if layer == depth - 1 else None)
        h = h2.reshape(B, T, D)
    return h
```

```python
import functools
import math

import numpy as np
import jax
import jax.numpy as jnp
from jax import lax
from jax.experimental import pallas as pl
from jax.experimental.pallas import tpu as pltpu

HEAD_DIM = 64
HALF = HEAD_DIM // 2
LANES = 128
ROPE_THETA = 10000.0
NORM_EPS = 1e-6
SCALE = HEAD_DIM ** -0.5

A_HEADS = 8
IDX_HEADS = 4
DSA_TOPK = 256
B_HEADS = 8
B_KV_GROUPS = 2
B_PER_GROUP = B_HEADS // B_KV_GROUPS
CMP_BLOCK = 32
CMP_STRIDE = 16
CMP_HIDDEN = 256
SLC_BLOCK = 64
SLC_TOPN = 16
WINDOW = 512
C_HEADS = 8

KV_TILE = 512
SEL_Q_TILE = 128
SEL_TILES_PER_STEP = 2
DSA_Q_TILE = 256
NSA_Q_TILE = 256
VW_TILE = 128

LOG2E = math.log2(math.e)
Q_SCALE = SCALE * LOG2E
ONES_ROWS = 16
BIG = 2.0 ** 100

NEG_INF = float("-inf")
M_FLOOR = -1e30
INT_MIN = -(2 ** 31)

VMEM_LIMIT = 56 * 1024 * 1024

BF16 = jnp.bfloat16
F32 = jnp.float32


def _dot(a, b):
    return jnp.dot(a, b, preferred_element_type=F32)


def _params(sem):
    return pltpu.CompilerParams(dimension_semantics=sem, vmem_limit_bytes=VMEM_LIMIT)


def _iota(shape, axis):
    return lax.broadcasted_iota(jnp.int32, shape, axis)


def _pair_cols(base_a, base_b):
    lane = np.arange(LANES)
    half = lane // 64
    which = (lane % 64) // HALF
    i = lane % HALF
    base = np.where(which == 0, base_a, base_b)
    return base + half * HALF + i


def _head_cols(base):
    lane = np.arange(LANES)
    return np.where(lane < HEAD_DIM, base + lane, -1)


_PAIR_D = _pair_cols(0, 0)
_PAIR_WHICH = (np.arange(LANES) % 64) // HALF


def _gather_cols(w, cols):
    cols = np.asarray(cols)
    safe = np.where(cols >= 0, cols, 0)
    g = jnp.take(w, jnp.asarray(safe, dtype=jnp.int32), axis=1)
    return jnp.where(jnp.asarray(cols >= 0)[None, :], g, jnp.zeros_like(g))


def _rope_slabs(T):
    inv = 1.0 / (ROPE_THETA ** (jnp.arange(0, HEAD_DIM, 2, dtype=F32) / HEAD_DIM))
    ang = jnp.arange(T, dtype=F32)[:, None] * inv[None, :]
    cos, sin = jnp.cos(ang), jnp.sin(ang)
    cos_slab = jnp.tile(cos, (1, 4))
    sin_slab = jnp.concatenate([-sin, -sin, sin, sin], axis=1)
    return cos_slab, sin_slab


_KINDS = ("qT", "k", "vT", "vTw", "f", "fT")


def _proj_kernel(x_ref, g_ref, w_ref, cos_ref, sin_ref, *out_refs, plan, kinds, chunk):
    outs = dict(zip(kinds, out_refs))
    x = x_ref[0]
    ms = jnp.mean(x * x, axis=-1, keepdims=True)
    xn = ((x * lax.rsqrt(ms + NORM_EPS)) * g_ref[...]).astype(BF16)
    cos = cos_ref[...]
    sin = sin_ref[...]
    n = len(plan)
    for c0 in range(0, n, chunk):
        c1 = min(c0 + chunk, n)
        r = _dot(xn, w_ref[:, c0 * LANES:c1 * LANES])
        for s in range(c0, c1):
            roped, kind, idx, opt = plan[s]
            y = r[:, (s - c0) * LANES:(s - c0 + 1) * LANES]
            if roped:
                y = y * cos + pltpu.roll(y, 64, 1) * sin
            if kind == "qT":
                if opt is not None:
                    y = y * opt
                outs[kind][0, idx] = y.T.astype(BF16)
            elif kind == "k":
                outs[kind][0, idx] = y.astype(BF16)
            elif kind == "vT":
                if opt == "ones_row_64":
                    y = jnp.where(_iota(y.shape, 1) == HEAD_DIM, 1.0, y)
                    outs[kind][0, idx, 0] = y.T.astype(BF16)
                else:
                    outs[kind][0, idx, 0, 0:LANES, :] = y.T.astype(BF16)
                    outs[kind][0, idx, 0, LANES:, :] = jnp.ones((ONES_ROWS, y.shape[0]), BF16)
            elif kind == "vTw":
                y_t = jnp.where(_iota(y.shape, 1) == HEAD_DIM, 1.0, y).T.astype(BF16)
                for sub in range(y.shape[0] // VW_TILE):
                    outs[kind][0, idx, sub] = y_t[:, sub * VW_TILE:(sub + 1) * VW_TILE]
            elif kind == "f":
                outs[kind][0, idx] = y
            else:
                outs[kind][0, idx] = y.T


def _project(x, g, w, cos_slab, sin_slab, plan, chunk=4):
    B, T, D = x.shape
    tm = KV_TILE
    assert T % tm == 0
    count = {k: sum(1 for p in plan if p[1] == k) for k in _KINDS}
    kinds = tuple(k for k in _KINDS if count[k])
    out_specs, out_shape = [], []
    for k in kinds:
        n = count[k]
        if k in ("qT", "fT"):
            out_specs.append(pl.BlockSpec((1, n, LANES, tm), lambda b, i: (b, 0, 0, i)))
            out_shape.append(jax.ShapeDtypeStruct((B, n, LANES, T), BF16 if k == "qT" else F32))
        elif k in ("k", "f"):
            out_specs.append(pl.BlockSpec((1, n, tm, LANES), lambda b, i: (b, 0, i, 0)))
            out_shape.append(jax.ShapeDtypeStruct((B, n, T, LANES), BF16 if k == "k" else F32))
        elif k == "vTw":
            sub = tm // VW_TILE
            out_specs.append(pl.BlockSpec((1, n, sub, LANES, VW_TILE), lambda b, i: (b, 0, i, 0, 0)))
            out_shape.append(jax.ShapeDtypeStruct((B, n, T // VW_TILE, LANES, VW_TILE), BF16))
        else:
            wide = any(p[1] == "vT" and p[3] != "ones_row_64" for p in plan)
            rows = LANES + ONES_ROWS if wide else LANES
            out_specs.append(pl.BlockSpec((1, n, 1, rows, tm), lambda b, i: (b, 0, i, 0, 0)))
            out_shape.append(jax.ShapeDtypeStruct((B, n, T // tm, rows, tm), BF16))
    kern = functools.partial(_proj_kernel, plan=tuple(plan), kinds=kinds, chunk=chunk)
    outs = pl.pallas_call(
        kern,
        grid=(B, T // tm),
        in_specs=[
            pl.BlockSpec((1, tm, D), lambda b, i: (b, i, 0)),
            pl.BlockSpec((1, D), lambda b, i: (0, 0)),
            pl.BlockSpec((D, len(plan) * LANES), lambda b, i: (0, 0)),
            pl.BlockSpec((tm, LANES), lambda b, i: (i, 0)),
            pl.BlockSpec((tm, LANES), lambda b, i: (i, 0)),
        ],
        out_specs=out_specs,
        out_shape=out_shape,
        compiler_params=_params(("parallel", "parallel")),
        name="norm_proj_rope",
    )(x, g.reshape(1, D), w, cos_slab, sin_slab)
    return dict(zip(kinds, outs))


def _head_rows(slab_t, which):
    row = _iota(slab_t.shape, 0)
    keep = ((row % 64) // HALF) == which
    return jnp.where(keep, slab_t, jnp.zeros_like(slab_t))


def _tree(op, xs):
    while len(xs) > 1:
        xs = [op(xs[i], xs[i + 1]) if i + 1 < len(xs) else xs[i] for i in range(0, len(xs), 2)]
    return xs[0]


def _fold_rows_max(x, ways=4):
    rows, n = x.shape
    per = rows // ways
    parts = [jnp.max(x[i * per:(i + 1) * per].reshape(per // 8, 8, n), axis=0) for i in range(ways)]
    return _tree(jnp.maximum, parts)


ROW_BLOCK = 64


def _online_update(s_sc, p_sc, v_aug, m_ref, acc_ref):
    tk, n = s_sc.shape
    m_prev = m_ref[...]
    m_tile = jnp.max(_fold_rows_max(s_sc[...]), axis=0, keepdims=True)
    m_new = jnp.maximum(m_prev, m_tile)
    alpha = jnp.exp2(m_prev - m_new)
    for r in range(tk // ROW_BLOCK):
        rows = slice(r * ROW_BLOCK, (r + 1) * ROW_BLOCK)
        p_sc[rows, :] = jnp.exp2(s_sc[rows, :] - m_new).astype(BF16)
    acc_ref[...] = alpha * acc_ref[...] + _dot(v_aug, p_sc[...])
    m_ref[...] = m_new


def _flash_pipeline(n, put_scores, update, buf_a, buf_b, put_last=None):
    if put_last is None:
        put_scores(buf_a, 0)
    else:
        @pl.when(n > 0)
        def _():
            put_scores(buf_a, 0)

    def pair(p, carry):
        j = 2 * p
        put_scores(buf_b, j + 1)
        update(buf_a, j)

        @pl.when(j + 2 < n)
        def _():
            put_scores(buf_a, j + 2)
            update(buf_b, j + 1)

        return carry

    lax.fori_loop(0, n // 2, pair, 0)
    in_b = jnp.logical_and(n > 0, n % 2 == 0)
    in_a = n % 2 == 1

    def finish(cur, other):
        if put_last is not None:
            put_last(other, n)
        update(cur, n - 1)
        if put_last is not None:
            update(other, n)

    pl.when(in_b)(lambda: finish(buf_b, buf_a))
    pl.when(in_a)(lambda: finish(buf_a, buf_b))
    if put_last is not None:
        @pl.when(n == 0)
        def _():
            put_last(buf_a, 0)
            update(buf_a, 0)


def _softmax_keys(s):
    m = jnp.max(s, axis=0, keepdims=True)
    m = jnp.where(m > NEG_INF, m, 0.0)
    e = jnp.exp2(s - m)
    return e / jnp.maximum(jnp.sum(e, axis=0, keepdims=True), 1e-30)


def _init_flash(m_ref, acc_ref):
    m_ref[...] = jnp.full(m_ref.shape, M_FLOOR, F32)
    acc_ref[...] = jnp.zeros(acc_ref.shape, F32)


DIFF_HEADS_PER_STEP = 2


def _diff_attn_kernel(lam_ref, q_ref, k_ref, v_ref, g_ref, o_ref, m_sc, acc_sc, sa_sc, sb_sc,
                      p_sc, *, tq, lambda_init):
    qi = pl.program_id(2)
    HS = DIFF_HEADS_PER_STEP
    qs = [[_head_rows(q_ref[0, hh], c) for c in range(2)] for hh in range(HS)]
    for s in range(2 * HS):
        _init_flash(m_sc.at[s], acc_sc.at[s])

    def put_scores(buf, j, masked=False):
        for hh in range(HS):
            k = k_ref[0, hh, pl.ds(pl.multiple_of(j * tq, tq), tq), :]
            for c in range(2):
                s = _dot(k, qs[hh][c])
                if masked:
                    s = jnp.where(_iota(s.shape, 0) <= _iota(s.shape, 1), s, NEG_INF)
                buf[2 * hh + c] = s

    def put_diagonal(buf, j):
        put_scores(buf, j, masked=True)

    def update(buf, j):
        for hh in range(HS):
            v_aug = v_ref[0, hh, j]
            for c in range(2):
                s = 2 * hh + c
                _online_update(buf.at[s], p_sc.at[s], v_aug, m_sc.at[s], acc_sc.at[s])

    _flash_pipeline(qi, put_scores, update, sa_sc, sb_sc, put_last=put_diagonal)

    lam = lam_ref[...]
    s01 = jnp.sum(lam[0:1] * lam[1:2], axis=-1, keepdims=True)
    s23 = jnp.sum(lam[2:3] * lam[3:4], axis=-1, keepdims=True)
    lam_val = jnp.exp(s01) - jnp.exp(s23) + lambda_init
    for hh in range(HS):
        a0, a1 = acc_sc[2 * hh], acc_sc[2 * hh + 1]
        o0 = a0[0:LANES] / jnp.maximum(a0[LANES:LANES + 1], 1e-30)
        o1 = a1[0:LANES] / jnp.maximum(a1[LANES:LANES + 1], 1e-30)
        o = o0 - lam_val * o1
        y = o * lax.rsqrt(jnp.mean(o * o, axis=0, keepdims=True) + NORM_EPS)
        y = (y * g_ref[...]) * (1.0 - lambda_init)
        o_ref[0, :, hh * LANES:(hh + 1) * LANES] = y.T.astype(o_ref.dtype)


def _diff_attention(sl, lam, subln_g, lambda_init):
    q_t, k, v_t = sl["qT"], sl["k"], sl["vT"]
    B, H, T, _ = k.shape
    tq = KV_TILE
    nk = T // tq
    kern = functools.partial(_diff_attn_kernel, tq=tq, lambda_init=lambda_init)
    HS = DIFF_HEADS_PER_STEP
    ns = 2 * HS
    return pl.pallas_call(
        kern,
        grid=(B, H // HS, T // tq),
        in_specs=[
            pl.BlockSpec((4, HEAD_DIM), lambda b, h, i: (0, 0)),
            pl.BlockSpec((1, HS, LANES, tq), lambda b, h, i: (b, h, 0, i)),
            pl.BlockSpec((1, HS, T, LANES), lambda b, h, i: (b, h, 0, 0)),
            pl.BlockSpec((1, HS, nk, LANES + ONES_ROWS, tq), lambda b, h, i: (b, h, 0, 0, 0)),
            pl.BlockSpec((LANES, 1), lambda b, h, i: (0, 0)),
        ],
        out_specs=pl.BlockSpec((1, tq, HS * LANES), lambda b, h, i: (b, i, h)),
        out_shape=jax.ShapeDtypeStruct((B, T, H * LANES), BF16),
        scratch_shapes=[
            pltpu.VMEM((ns, 1, tq), F32),
            pltpu.VMEM((ns, LANES + ONES_ROWS, tq), F32),
            pltpu.VMEM((ns, tq, tq), F32),
            pltpu.VMEM((ns, tq, tq), F32),
            pltpu.VMEM((ns, tq, tq), BF16),
        ],
        compiler_params=_params(("parallel", "parallel", "arbitrary")),
        name="diff_attention",
    )(lam, q_t, k, v_t, subln_g.reshape(LANES, 1))


def _dsa_select_kernel(qi_ref, ki_ref, misc_ref, tril_ref, bias_ref, key_sc, *, tq, ck, nk, topk):
    S = key_sc.shape[0]
    t0 = pl.program_id(1) * (S * tq)
    nvalid = (t0 + S * tq - 1) // ck + 1
    lanes = [slice(s * tq, (s + 1) * tq) for s in range(S)]
    w = [misc_ref[0, 0, :, lanes[s]] for s in range(S)]
    qh = [jnp.concatenate([_head_rows(qi_ref[0, h // 2, :, lanes[s]], h % 2)
                           for h in range(IDX_HEADS)], axis=1) for s in range(S)]
    t_q = [t0 + s * tq + _iota((ck, tq), 1) for s in range(S)]

    def causal(c, s):
        return (c * ck + _iota((ck, tq), 0)) <= t_q[s]

    def fill(c, s):
        kk = ki_ref[0, 0, c * ck:(c + 1) * ck, :]
        r = _dot(kk, qh[s])
        score = jnp.zeros((ck, tq), F32)
        for h in range(IDX_HEADS):
            score = score + w[s][h:h + 1, :] * jnp.maximum(r[:, h * tq:(h + 1) * tq], 0.0)
        bits = pltpu.bitcast(score, jnp.int32)
        key = bits ^ ((bits >> 31) & jnp.int32(0x7FFFFFFF))
        key = jnp.where(score == 0.0, 0, key)
        key_sc[s, c] = jnp.where(causal(c, s), key, INT_MIN)

    kf = float(topk)

    def select_threshold(n):
        for c in range(n):
            for s in range(S):
                fill(c, s)

        def count(pred, s):
            acc = jnp.zeros((ck // 8, tq), F32)
            for c in range(n):
                ind = pred(key_sc[s, c], c)
                acc = acc + jnp.sum(ind.reshape(8, ck // 8, tq), axis=0)
            return jnp.sum(acc, axis=0, keepdims=True)

        zero = jnp.zeros((1, tq), jnp.int32)
        ans = tuple(jnp.where(count(lambda kc, c: jnp.where(kc >= zero, 1.0, 0.0), s) >= kf, 0, INT_MIN)
                    for s in range(S))

        def bit_step(i, ans):
            out = []
            for s in range(S):
                cand = ans[s] | (jnp.int32(1) << (30 - i))
                cnt = count(lambda kc, c: jnp.where(kc >= cand, 1.0, 0.0), s)
                out.append(jnp.where(cnt >= kf, cand, ans[s]))
            return tuple(out)

        ans = lax.fori_loop(0, 31, bit_step, ans)
        cnt_gt = tuple(count(lambda kc, c: jnp.where(kc > ans[s], 1.0, 0.0), s) for s in range(S))
        cnt_eq = tuple(count(lambda kc, c: jnp.where(kc == ans[s],
                                                     jnp.where(causal(c, s), 1.0, 0.0), 0.0), s)
                       for s in range(S))
        return ans, cnt_gt, cnt_eq

    ans, cnt_gt, cnt_eq = lax.switch(
        nvalid - 1, [functools.partial(select_threshold, n) for n in range(1, nk + 1)])
    need = [kf - cnt_gt[s] for s in range(S)]
    has_tie = _tree(jnp.maximum, [jnp.max(cnt_eq[s] - need[s]) for s in range(S)]) > 0.0

    @pl.when(jnp.logical_not(has_tie))
    def _():
        def emit(c, carry):
            for s in range(S):
                sel = jnp.where(key_sc[s, c] >= ans[s], jnp.where(causal(c, s), 0.0, NEG_INF), NEG_INF)
                bias_ref[0, s, c] = sel.astype(BF16)
            return carry
        lax.fori_loop(0, nvalid, emit, 0)

    @pl.when(has_tie)
    def _():
        def emit(c, carry):
            out = []
            for s in range(S):
                kc = key_sc[s, c]
                eq = jnp.where(kc == ans[s], jnp.where(causal(c, s), 1.0, 0.0), 0.0)
                pre = _dot(tril_ref[...], eq.astype(BF16)) + carry[s]
                take = jnp.where(kc > ans[s], 1.0, jnp.where(pre < need[s], eq, 0.0))
                sel = jnp.where(take > 0.5, jnp.where(causal(c, s), 0.0, NEG_INF), NEG_INF)
                bias_ref[0, s, c] = sel.astype(BF16)
                out.append(carry[s] + jnp.sum(eq, axis=0, keepdims=True))
            return tuple(out)
        lax.fori_loop(0, nvalid, emit, tuple(jnp.zeros((1, tq), F32) for _ in range(S)))

    def blank(c, carry):
        for s in range(S):
            bias_ref[0, s, c] = jnp.full((ck, tq), NEG_INF, BF16)
        return carry

    lax.fori_loop(nvalid, nk, blank, 0)


def _dsa_select(sl, lay):
    q_t, k, misc_t = sl["qT"], sl["k"], sl["fT"]
    B, _, T, _ = k.shape
    tq, ck = SEL_Q_TILE, KV_TILE
    nk = T // ck
    topk = min(DSA_TOPK, T // 4)
    tril = jnp.asarray(np.tril(np.ones((ck, ck), np.float32), -1), BF16)
    kern = functools.partial(_dsa_select_kernel, tq=tq, ck=ck, nk=nk, topk=topk)
    S = SEL_TILES_PER_STEP
    assert ck % (S * tq) == 0 and T % (S * tq) == 0
    return pl.pallas_call(
        kern,
        grid=(B, T // (S * tq)),
        in_specs=[
            pl.BlockSpec((1, 2, LANES, S * tq), lambda b, i: (b, lay["qi"] // 2, 0, i)),
            pl.BlockSpec((1, 1, T, LANES), lambda b, i: (b, lay["ki"], 0, 0)),
            pl.BlockSpec((1, 1, LANES, S * tq), lambda b, i: (b, lay["misc"], 0, i)),
            pl.BlockSpec((ck, ck), lambda b, i: (0, 0)),
        ],
        out_specs=pl.BlockSpec((1, S, nk, ck, tq), lambda b, i: (b, i, 0, 0, 0)),
        out_shape=jax.ShapeDtypeStruct((B, T // tq, nk, ck, tq), BF16),
        scratch_shapes=[pltpu.VMEM((S, nk, ck, tq), jnp.int32)],
        compiler_params=_params(("parallel", "parallel")),
        name="dsa_select",
    )(q_t, k, misc_t, tril)


def _store_head_pairs(o_ref, o, n_heads, tq, col0=0, row0=0):
    for p in range(n_heads // 2):
        even = o[0:HEAD_DIM, (2 * p) * tq:(2 * p + 1) * tq]
        odd = o[0:HEAD_DIM, (2 * p + 1) * tq:(2 * p + 2) * tq]
        pair = jnp.concatenate([even, odd], axis=0)
        cols = slice(col0 + p * LANES, col0 + (p + 1) * LANES)
        o_ref[0, row0:row0 + tq, cols] = pair.T.astype(o_ref.dtype)


def _normalized(acc):
    return acc[0:HEAD_DIM] / jnp.maximum(acc[HEAD_DIM:HEAD_DIM + 1], 1e-30)


def _dsa_attn_kernel(q_ref, k_ref, v_ref, bias_ref, o_ref, qst, m_sc, acc_sc, sa_sc, sb_sc,
                     p_sc, *, tq, tk):
    qi = pl.program_id(1)
    H = A_HEADS
    S = sa_sc.shape[0]
    nsub = bias_ref.shape[1] // S
    for s in range(S):
        for h in range(H):
            qst[s, :, h * tq:(h + 1) * tq] = _head_rows(q_ref[0, h // 2, :, s * tq:(s + 1) * tq], h % 2)
        _init_flash(m_sc.at[s], acc_sc.at[s])
    nkv = (qi * S * tq) // tk + 1

    def put_scores(buf, j):
        k = k_ref[0, 0, pl.ds(pl.multiple_of(j * tk, tk), tk), :]
        for s in range(S):
            parts = [bias_ref[0, s * nsub + u, j].astype(F32) for u in range(nsub)]
            b = parts[0] if len(parts) == 1 else jnp.concatenate(parts, axis=1)
            buf[s] = _dot(k, qst[s]) + jnp.concatenate([b] * H, axis=1)

    def update(buf, j):
        for s in range(S):
            _online_update(buf.at[s], p_sc.at[s], v_ref[0, 0, j], m_sc.at[s], acc_sc.at[s])

    _flash_pipeline(nkv, put_scores, update, sa_sc, sb_sc)
    for s in range(S):
        _store_head_pairs(o_ref, _normalized(acc_sc[s]), H, tq, row0=s * tq)


def _dsa_attention(sl, bias, lay):
    q_t, k, v_t = sl["qT"], sl["k"], sl["vT"]
    B, _, T, _ = k.shape
    tq, tk = DSA_Q_TILE, KV_TILE
    nk = T // tk
    H = A_HEADS
    N = H * tq
    sel_tq = bias.shape[4]
    S = tk // tq
    nsub = S * tq // sel_tq
    kern = functools.partial(_dsa_attn_kernel, tq=tq, tk=tk)
    return pl.pallas_call(
        kern,
        grid=(B, T // (S * tq)),
        in_specs=[
            pl.BlockSpec((1, H // 2, LANES, S * tq), lambda b, i: (b, lay["qA"] // (H // 2), 0, i)),
            pl.BlockSpec((1, 1, T, LANES), lambda b, i: (b, lay["kA"], 0, 0)),
            pl.BlockSpec((1, 1, nk, LANES, tk), lambda b, i: (b, lay["vA"], 0, 0, 0)),
            pl.BlockSpec((1, nsub, nk, tk, sel_tq), lambda b, i: (b, i, 0, 0, 0)),
        ],
        out_specs=pl.BlockSpec((1, S * tq, H * HEAD_DIM), lambda b, i: (b, i, 0)),
        out_shape=jax.ShapeDtypeStruct((B, T, H * HEAD_DIM), BF16),
        scratch_shapes=[
            pltpu.VMEM((S, LANES, N), BF16),
            pltpu.VMEM((S, 1, N), F32),
            pltpu.VMEM((S, LANES, N), F32),
            pltpu.VMEM((S, tk, N), F32),
            pltpu.VMEM((S, tk, N), F32),
            pltpu.VMEM((S, tk, N), BF16),
        ],
        compiler_params=_params(("parallel", "arbitrary")),
        name="dsa_attention",
    )(q_t, k, v_t, bias)


def _compress_kernel(x_ref, pe_ref, w1_ref, w2_ref, o_ref, ot_ref, *, n_ch):
    x = x_ref[0, 0]
    xt = (x + pe_ref[0, 0]).astype(BF16)
    xb = (x + pe_ref[0, 1]).astype(BF16)
    for g in range(B_KV_GROUPS):
        a = _dot(xt, w1_ref[0, g, 0])
        b = _dot(xb, w1_ref[0, g, 1])
        h = jax.nn.gelu(a + pltpu.roll(b, n_ch - 1, 0))
        r = _dot(h.astype(BF16), w2_ref[0])
        o_ref[0, 0, g] = r.astype(BF16)
        ot_ref[0, 0, g] = r.T.astype(BF16)


def _compress(flat, pe, w1, w2):
    B, _, n_ch, W = flat.shape
    G = B_KV_GROUPS
    kern = functools.partial(_compress_kernel, n_ch=n_ch)
    return pl.pallas_call(
        kern,
        grid=(2, B),
        in_specs=[
            pl.BlockSpec((1, 1, n_ch, W), lambda s, b: (b, s, 0, 0)),
            pl.BlockSpec((1, 2, 1, W), lambda s, b: (s, 0, 0, 0)),
            pl.BlockSpec((1, G, 2, W, CMP_HIDDEN), lambda s, b: (s, 0, 0, 0, 0)),
            pl.BlockSpec((1, CMP_HIDDEN, LANES), lambda s, b: (s, 0, 0)),
        ],
        out_specs=[
            pl.BlockSpec((1, 1, G, n_ch, LANES), lambda s, b: (s, b, 0, 0, 0)),
            pl.BlockSpec((1, 1, G, LANES, n_ch), lambda s, b: (s, b, 0, 0, 0)),
        ],
        out_shape=[
            jax.ShapeDtypeStruct((2, B, G, n_ch, LANES), BF16),
            jax.ShapeDtypeStruct((2, B, G, LANES, n_ch), BF16),
        ],
        compiler_params=_params(("parallel", "parallel")),
        name="nsa_compress",
    )(flat, pe, w1, w2)


def _nsa_kernel(q_ref, kc_ref, vc_ref, ks_ref, vs_ref, kw_ref, vw_ref, misc_ref, ov_ref, ex_ref,
                cz_ref, wz_ref, o_ref, qst, m_sc, acc_sc, sa_sc, sb_sc, p_sc, mw_sc, accw_sc,
                sw_sc, pw_sc, oc_sc, *, tq, tk, T, n_s, n_sel, gate_row0):
    qi = pl.program_id(1)
    t0 = qi * tq
    G, J = B_KV_GROUPS, B_PER_GROUP
    N = J * tq
    for g in range(G):
        for j in range(J):
            qst[g, 0:LANES, j * tq:(j + 1) * tq] = _head_rows(q_ref[0, 2 * g + j // 2], j % 2)

    def q_time(shape):
        return t0 + (_iota(shape, 1) % tq)

    wlen = min(WINDOW + tq, T)
    wstart = pl.multiple_of(jnp.maximum(t0 - WINDOW, 0), tq)
    n_wt = wlen // tq
    wt = wstart // VW_TILE
    wz = wz_ref[jnp.minimum(qi, n_wt - 1)].astype(F32)
    wz = jnp.concatenate([wz] * J, axis=1)
    for g in range(G):
        kw = kw_ref[0, g, pl.ds(wstart, wlen), :]
        vw = jnp.concatenate([vw_ref[0, g, wt + i] for i in range(wlen // VW_TILE)], axis=1)
        sw_sc[g] = _dot(kw, qst[g, 0:LANES, :]) + wz
        _init_flash(mw_sc.at[g], accw_sc.at[g])
        _online_update(sw_sc.at[g], pw_sc.at[g], vw, mw_sc.at[g], accw_sc.at[g])

    rows = -(-n_s // 8) * 8
    blk = _iota((rows, tq), 0)
    t_q = t0 + _iota((rows, tq), 1)
    cur = t_q // SLC_BLOCK
    forced = (blk == 0) | (blk == cur) | (blk == cur - 1)
    admissible = blk * SLC_BLOCK <= t_q
    blk8 = _iota((8, tq), 0)
    for g in range(G):
        q = qst[g, 0:LANES, :]
        kc = kc_ref[0, 0, g]
        n_ch = kc.shape[0]
        s_c = _dot(kc, q)
        cmp_end = _iota((n_ch, N), 0) * CMP_STRIDE + (CMP_BLOCK - 1)
        s_c = jnp.where(cmp_end <= q_time((n_ch, N)), s_c, NEG_INF)
        p_c = _softmax_keys(s_c)
        oc_sc[g] = _dot(vc_ref[0, 0, g], p_c.astype(BF16))[0:HEAD_DIM]
        psum = p_c[:, 0:tq]
        for j in range(1, J):
            psum = psum + p_c[:, j * tq:(j + 1) * tq]
        p_hi = psum.astype(BF16)
        p_lo = (psum - p_hi.astype(F32)).astype(BF16)
        imp = _dot(ov_ref[...], p_hi) + _dot(ov_ref[...], p_lo)
        sc = jnp.where(forced, jnp.inf, imp[0:rows])
        sc = jnp.where(admissible, sc, NEG_INF)
        groups = [sc[8 * r:8 * r + 8] for r in range(rows // 8)]
        ranks = [jnp.zeros((8, tq), F32) for _ in groups]
        for m in range(n_s):
            cm = sc[m:m + 1, :]
            for r, grp in enumerate(groups):
                gt = jnp.where(cm > grp, 1.0, 0.0)
                ge = jnp.where(cm >= grp, 1.0, 0.0)
                if 8 * r + 7 <= m:
                    first = gt
                elif 8 * r > m:
                    first = ge
                else:
                    first = jnp.where(blk8 + 8 * r > m, ge, gt)
                ranks[r] = ranks[r] + first
        rank = jnp.concatenate(ranks, axis=0)
        drop = jnp.where(rank < float(n_sel), 0.0, 1.0)
        if rows < LANES:
            drop = jnp.concatenate([drop, jnp.ones((LANES - rows, tq), F32)], axis=0)
        drop = drop.astype(BF16)
        qst[g, LANES:2 * LANES, :] = jnp.concatenate([drop] * J, axis=1)

    jd = t0 // tk
    off = (t0 - jd * tk) // tq
    n_off = tk // tq
    for g in range(G):
        _init_flash(m_sc.at[g], acc_sc.at[g])

    def put_scores(buf, j):
        cz = cz_ref[jnp.where(j == jd, off, n_off)].astype(F32)
        cz = jnp.concatenate([cz] * J, axis=1)
        for g in range(G):
            k = ks_ref[0, g, pl.ds(pl.multiple_of(j * tk, tk), tk), :]
            lhs = jnp.concatenate([k, ex_ref[j]], axis=1)
            buf[g] = _dot(lhs, qst[g]) + cz

    def update(buf, j):
        for g in range(G):
            _online_update(buf.at[g], p_sc.at[g], vs_ref[0, g, j], m_sc.at[g], acc_sc.at[g])

    _flash_pipeline(jd + 1, put_scores, update, sa_sc, sb_sc)

    gates = jax.nn.sigmoid(misc_ref[0, 0])
    for g in range(G):
        def gate_row(c, g=g):
            rws = [gate_row0 + (g * J + j) * 3 + c for j in range(J)]
            return jnp.concatenate([gates[r:r + 1, :] for r in rws], axis=1)

        o_s = _normalized(acc_sc[g])
        o_w = _normalized(accw_sc[g])
        o = gate_row(0) * oc_sc[g] + (gate_row(1) * o_s + gate_row(2) * o_w)
        _store_head_pairs(o_ref, o, J, tq, col0=g * J * HEAD_DIM)


def _nsa_attention(sl, cmp_k, cmp_vt, lay):
    q_t, k, v_t, misc_t = sl["qT"], sl["k"], sl["vT"], sl["fT"]
    B, _, T, _ = k.shape
    G, J = B_KV_GROUPS, B_PER_GROUP
    tq, tk = NSA_Q_TILE, KV_TILE
    nk = T // tk
    n_ch = T // CMP_STRIDE
    n_c = n_ch - CMP_BLOCK // CMP_STRIDE + 1
    n_s = T // SLC_BLOCK
    n_sel = min(SLC_TOPN, n_s)
    assert n_s <= LANES
    c0 = np.arange(n_ch) * CMP_STRIDE
    s0 = np.arange(LANES) * SLC_BLOCK
    ov = ((c0[None, :] < s0[:, None] + SLC_BLOCK) & (c0[None, :] + CMP_BLOCK > s0[:, None]))
    ov = ov & (np.arange(n_ch)[None, :] < n_c) & (np.arange(LANES)[:, None] < n_s)
    ov = jnp.asarray(ov.astype(np.float32), BF16)
    pos = np.arange(T).reshape(nk, tk, 1)
    ex = (pos // SLC_BLOCK == np.arange(LANES).reshape(1, 1, LANES))
    ex = jnp.asarray(ex.astype(np.float32) * -BIG, BF16)
    n_off = tk // tq
    kp = np.arange(tk).reshape(1, tk, 1)
    tl = np.arange(tq).reshape(1, 1, tq) + np.arange(n_off + 1).reshape(n_off + 1, 1, 1) * tq
    cz = np.where((kp <= tl) | (np.arange(n_off + 1).reshape(-1, 1, 1) == n_off), 0.0, -BIG)
    cz = jnp.asarray(cz.astype(np.float32), BF16)
    wlen = min(WINDOW + tq, T)
    n_wt = wlen // tq
    assert wlen % tq == 0 and T >= wlen
    kp = np.arange(wlen).reshape(1, wlen, 1)
    tl = np.arange(tq).reshape(1, 1, tq)
    early = kp <= tl + np.arange(n_wt).reshape(n_wt, 1, 1) * tq
    late = (kp > tl) & (kp <= tl + WINDOW)
    band = np.where(np.arange(n_wt).reshape(n_wt, 1, 1) == n_wt - 1, late, early)
    wz = jnp.asarray(np.where(band, 0.0, -BIG).astype(np.float32), BF16)
    kern = functools.partial(_nsa_kernel, tq=tq, tk=tk, T=T, n_s=n_s, n_sel=n_sel,
                             gate_row0=lay["gate_row0"])
    N = J * tq
    for name in ("qB", "ks", "vs", "kw", "vw"):
        assert lay[name] % (2 * G if name == "qB" else G) == 0
    kslab = lambda off: pl.BlockSpec((1, G, T, LANES), lambda b, i: (b, off // G, 0, 0))
    return pl.pallas_call(
        kern,
        grid=(B, T // tq),
        in_specs=[
            pl.BlockSpec((1, 2 * G, LANES, tq), lambda b, i: (b, lay["qB"] // (2 * G), 0, i)),
            pl.BlockSpec((1, 1, G, n_ch, LANES), lambda b, i: (0, b, 0, 0, 0)),
            pl.BlockSpec((1, 1, G, LANES, n_ch), lambda b, i: (1, b, 0, 0, 0)),
            kslab(lay["ks"]),
            pl.BlockSpec((1, G, nk, LANES, tk), lambda b, i: (b, lay["vs"] // G, 0, 0, 0)),
            kslab(lay["kw"]),
            pl.BlockSpec((1, G, T // VW_TILE, LANES, VW_TILE),
                         lambda b, i: (b, lay["vw"] // G, 0, 0, 0)),
            pl.BlockSpec((1, 1, LANES, tq), lambda b, i: (b, lay["misc"], 0, i)),
            pl.BlockSpec((LANES, n_ch), lambda b, i: (0, 0)),
            pl.BlockSpec((nk, tk, LANES), lambda b, i: (0, 0, 0)),
            pl.BlockSpec((n_off + 1, tk, tq), lambda b, i: (0, 0, 0)),
            pl.BlockSpec((n_wt, wlen, tq), lambda b, i: (0, 0, 0)),
        ],
        out_specs=pl.BlockSpec((1, tq, B_HEADS * HEAD_DIM), lambda b, i: (b, i, 0)),
        out_shape=jax.ShapeDtypeStruct((B, T, B_HEADS * HEAD_DIM), BF16),
        scratch_shapes=[
            pltpu.VMEM((G, 2 * LANES, N), BF16),
            pltpu.VMEM((G, 1, N), F32),
            pltpu.VMEM((G, LANES, N), F32),
            pltpu.VMEM((G, tk, N), F32),
            pltpu.VMEM((G, tk, N), F32),
            pltpu.VMEM((G, tk, N), BF16),
            pltpu.VMEM((G, 1, N), F32),
            pltpu.VMEM((G, LANES, N), F32),
            pltpu.VMEM((G, wlen, N), F32),
            pltpu.VMEM((G, wlen, N), BF16),
            pltpu.VMEM((G, HEAD_DIM, N), F32),
        ],
        compiler_params=_params(("parallel", "arbitrary")),
        name="nsa_attention",
    )(q_t, cmp_k, cmp_vt, k, v_t, k, sl["vTw"], misc_t, ov, ex, cz, wz)


def _rms(x, g):
    ms = jnp.mean(x * x, axis=-1, keepdims=True)
    return (x * lax.rsqrt(ms + NORM_EPS)) * g


def _layer_tail_kernel(*refs, n_in, final_norm):
    h_ref = refs[0]
    o_refs = refs[1:1 + n_in]
    w_refs = refs[1 + n_in:1 + 2 * n_in]
    g_ref, wu_ref, wd_ref = refs[1 + 2 * n_in:4 + 2 * n_in]
    rest = refs[4 + 2 * n_in:]
    if final_norm:
        fg_ref, out_ref, xn_sc = rest
    else:
        out_ref, xn_sc = rest
    f = pl.program_id(1)

    @pl.when(f == 0)
    def _():
        mix = _dot(o_refs[0][...], w_refs[0][...])
        for i in range(1, n_in):
            mix = mix + _dot(o_refs[i][...], w_refs[i][...])
        x = h_ref[...] + mix
        xn_sc[...] = _rms(x, g_ref[...]).astype(BF16)
        out_ref[...] = x

    u = _dot(xn_sc[...], wu_ref[...])
    a = jnp.square(jnp.maximum(u, 0.0)).astype(BF16)
    out_ref[...] += _dot(a, wd_ref[...])

    if final_norm:
        @pl.when(f == pl.num_programs(1) - 1)
        def _():
            out_ref[...] = _rms(out_ref[...], fg_ref[...])


def _layer_tail(h2, outs, ws, g, w_up, w_down, final_g=None, tm=1024, tf=1024):
    N, D = h2.shape
    F = w_up.shape[1]
    tm = min(tm, N)
    n_in = len(outs)
    in_specs = [pl.BlockSpec((tm, D), lambda i, f: (i, 0))]
    in_specs += [pl.BlockSpec((tm, o.shape[1]), lambda i, f: (i, 0)) for o in outs]
    in_specs += [pl.BlockSpec(w.shape, lambda i, f: (0, 0)) for w in ws]
    in_specs += [
        pl.BlockSpec((1, D), lambda i, f: (0, 0)),
        pl.BlockSpec((D, tf), lambda i, f: (0, f)),
        pl.BlockSpec((tf, D), lambda i, f: (f, 0)),
    ]
    args = [h2, *outs, *ws, g.reshape(1, D), w_up, w_down]
    if final_g is not None:
        in_specs.append(pl.BlockSpec((1, D), lambda i, f: (0, 0)))
        args.append(final_g.reshape(1, D))
    kern = functools.partial(_layer_tail_kernel, n_in=n_in, final_norm=final_g is not None)
    return pl.pallas_call(
        kern,
        grid=(N // tm, F // tf),
        in_specs=in_specs,
        out_specs=pl.BlockSpec((tm, D), lambda i, f: (i, 0)),
        out_shape=jax.ShapeDtypeStruct((N, D), F32),
        scratch_shapes=[pltpu.VMEM((tm, D), BF16)],
        compiler_params=_params(("parallel", "arbitrary")),
        name="out_proj_mlp",
    )(*args)


def _even_layout():
    offs = {}
    o = 0
    for name, size in (("qa", 512), ("ka", 64), ("va", 64), ("qi", 256), ("ki", 64), ("wi", 4),
                       ("qb", 512), ("kvb", 768), ("gb", 24)):
        offs[name] = o
        o += size
    kvb = lambda which, g: offs["kvb"] + (which * B_KV_GROUPS + g) * HEAD_DIM
    cols, plan, lay = [], [], {}
    n = {k: 0 for k in _KINDS}

    def add(c, roped, kind, opt=None):
        cols.append(c)
        plan.append((roped, kind, n[kind], opt))
        n[kind] += 1
        return n[kind] - 1

    lay["qA"] = n["qT"]
    for p in range(4):
        add(_pair_cols(offs["qa"] + 2 * p * 64, offs["qa"] + (2 * p + 1) * 64), True, "qT", Q_SCALE)
    lay["qB"] = n["qT"]
    for p in range(4):
        add(_pair_cols(offs["qb"] + 2 * p * 64, offs["qb"] + (2 * p + 1) * 64), True, "qT", Q_SCALE)
    lay["qi"] = n["qT"]
    for p in range(2):
        add(_pair_cols(offs["qi"] + 2 * p * 64, offs["qi"] + (2 * p + 1) * 64), True, "qT")
    lay["kA"] = add(_pair_cols(offs["ka"], offs["ka"]), True, "k")
    lay["ki"] = add(_pair_cols(offs["ki"], offs["ki"]), True, "k")
    lay["ks"] = n["k"]
    for g in range(2):
        add(_pair_cols(kvb(2, g), kvb(2, g)), True, "k")
    lay["kw"] = n["k"]
    for g in range(2):
        add(_pair_cols(kvb(4, g), kvb(4, g)), True, "k")
    lay["kc"] = add(_pair_cols(kvb(0, 0), kvb(0, 1)), True, "f")
    lay["vc"] = add(_pair_cols(kvb(1, 0), kvb(1, 1)), False, "f")
    misc = np.full(LANES, -1)
    misc[0:IDX_HEADS] = offs["wi"] + np.arange(IDX_HEADS)
    lay["gate_row0"] = 8
    misc[8:8 + 24] = offs["gb"] + np.arange(24)
    lay["misc"] = add(misc, False, "fT")
    lay["vw"] = n["vTw"]
    for g in range(2):
        add(_head_cols(kvb(5, g)), False, "vTw")
    lay["vs"] = n["vT"]
    for g in range(2):
        add(_head_cols(kvb(3, g)), False, "vT", "ones_row_64")
    lay["vA"] = add(_head_cols(offs["va"]), False, "vT", "ones_row_64")
    return np.concatenate(cols), plan, lay


def _odd_layout():
    cols, plan = [], []
    for h in range(C_HEADS):
        cols.append(_pair_cols(h * 128, h * 128 + 64))
        plan.append((True, "qT", h, Q_SCALE))
    for h in range(C_HEADS):
        cols.append(_pair_cols(1024 + h * 128, 1024 + h * 128 + 64))
        plan.append((True, "k", h, None))
    for h in range(C_HEADS):
        cols.append(2048 + h * 128 + np.arange(LANES))
        plan.append((False, "vT", h, "ones_rows_below"))
    return np.concatenate(cols), plan


def _compress_weights(pe, w1, w2):
    d = _PAIR_D
    which = _PAIR_WHICH
    pe_l = pe[:, :, d]
    pe_l = pe_l.reshape(2, 2, 1, CMP_STRIDE * LANES)
    w1r = w1.astype(BF16).reshape(2, CMP_BLOCK, HEAD_DIM, CMP_HIDDEN)[:, :, d, :]
    per_g = []
    for g in range(B_KV_GROUPS):
        keep = jnp.asarray(which == g)[None, None, :, None]
        per_g.append(jnp.where(keep, w1r, jnp.zeros_like(w1r)))
    w1g = jnp.stack(per_g, axis=1)
    w1g = w1g.reshape(2, B_KV_GROUPS, 2, CMP_STRIDE * LANES, CMP_HIDDEN)
    w2b = w2.astype(BF16)
    w2k = w2b[0][:, d]
    w2v = w2b[1][:, np.arange(LANES) % HEAD_DIM]
    w2l = jnp.stack([w2k, w2v], axis=0)
    return pe_l, w1g, w2l


def _even_mixer(h, norm_g, w_in, cmp_pe, cmp_w1, cmp_w2, w_out, cos_slab, sin_slab):
    B, T, D = h.shape
    cols, plan, lay = _even_layout()
    w = _gather_cols(w_in.astype(BF16), cols)
    sl = _project(h, norm_g, w, cos_slab, sin_slab, plan)

    bias = _dsa_select(sl, lay)
    o_a = _dsa_attention(sl, bias, lay)

    n_ch = T // CMP_STRIDE
    assert (lay["kc"], lay["vc"]) == (0, 1) and sl["f"].shape[1] == 2
    flat = sl["f"].reshape(B, 2, n_ch, CMP_STRIDE * LANES)
    pe_l, w1g, w2l = _compress_weights(cmp_pe, cmp_w1, cmp_w2)
    cmp_k, cmp_vt = _compress(flat, pe_l, w1g, w2l)
    o_b = _nsa_attention(sl, cmp_k, cmp_vt, lay)

    na = A_HEADS * HEAD_DIM
    wo = w_out.astype(BF16)
    return [o_a.reshape(B * T, -1), o_b.reshape(B * T, -1)], [wo[:na], wo[na:]]


def _odd_mixer(h, norm_g, w_in, lam, subln_g, w_out, cos_slab, sin_slab, lambda_init):
    B, T, D = h.shape
    cols, plan = _odd_layout()
    w = _gather_cols(w_in.astype(BF16), cols)
    sl = _project(h, norm_g, w, cos_slab, sin_slab, plan)
    o = _diff_attention(sl, lam, subln_g, lambda_init)
    return [o.reshape(B * T, -1)], [w_out.astype(BF16)]


def kernel(x, mix_norm_g, mlp_norm_g, even_w_in, even_cmp_pe, even_cmp_w1, even_cmp_w2, even_w_out, odd_w_in, odd_lambda, odd_subln_g, odd_w_out, mlp_w_up, mlp_w_down, final_norm_g):
    B, T, D = x.shape
    depth = mix_norm_g.shape[0]
    assert depth >= 1
    cos_slab, sin_slab = _rope_slabs(T)
    h = x
    for layer in range(depth):
        if layer % 2 == 0:
            e = layer // 2
            outs, ws = _even_mixer(h, mix_norm_g[layer], even_w_in[e], even_cmp_pe[e],
                                   even_cmp_w1[e], even_cmp_w2[e], even_w_out[e], cos_slab, sin_slab)
        else:
            o = layer // 2
            lambda_init = 0.8 - 0.6 * math.exp(-0.3 * layer)
            outs, ws = _odd_mixer(h, mix_norm_g[layer], odd_w_in[o], odd_lambda[o], odd_subln_g[o],
                                  odd_w_out[o], cos_slab, sin_slab, lambda_init)
        h2 = _layer_tail(h.reshape(B * T, D), outs, ws, mlp_norm_g[layer],
                         mlp_w_up[layer].astype(BF16), mlp_w_down[layer].astype(BF16),
                         final_g=final_norm_g if layer == depth - 1 else None)
        h = h2.reshape(B, T, D)
    return h
```

```python
import functools
import math

import numpy as np
import jax
import jax.numpy as jnp
from jax import lax
from jax.experimental import pallas as pl
from jax.experimental.pallas import tpu as pltpu

HEAD_DIM = 64
HALF = HEAD_DIM // 2
LANES = 128
ROPE_THETA = 10000.0
NORM_EPS = 1e-6
SCALE = HEAD_DIM ** -0.5

A_HEADS = 8
IDX_HEADS = 4
DSA_TOPK = 256
B_HEADS = 8
B_KV_GROUPS = 2
B_PER_GROUP = B_HEADS // B_KV_GROUPS
CMP_BLOCK = 32
CMP_STRIDE = 16
CMP_HIDDEN = 256
SLC_BLOCK = 64
SLC_TOPN = 16
WINDOW = 512
C_HEADS = 8

KV_TILE = 512
SEL_Q_TILE = 128
SEL_TILES_PER_STEP = 4
DSA_Q_TILE = 256
NSA_Q_TILE = 256
VW_TILE = 128

LOG2E = math.log2(math.e)
Q_SCALE = SCALE * LOG2E
ONES_ROWS = 16
BIG = 2.0 ** 100

NEG_INF = float("-inf")
M_FLOOR = -1e30
INT_MIN = -(2 ** 31)

VMEM_LIMIT = 56 * 1024 * 1024

BF16 = jnp.bfloat16
F32 = jnp.float32


def _dot(a, b):
    return jnp.dot(a, b, preferred_element_type=F32)


def _params(sem):
    return pltpu.CompilerParams(dimension_semantics=sem, vmem_limit_bytes=VMEM_LIMIT)


def _iota(shape, axis):
    return lax.broadcasted_iota(jnp.int32, shape, axis)


def _pair_cols(base_a, base_b):
    lane = np.arange(LANES)
    half = lane // 64
    which = (lane % 64) // HALF
    i = lane % HALF
    base = np.where(which == 0, base_a, base_b)
    return base + half * HALF + i


def _head_cols(base):
    lane = np.arange(LANES)
    return np.where(lane < HEAD_DIM, base + lane, -1)


_PAIR_D = _pair_cols(0, 0)
_PAIR_WHICH = (np.arange(LANES) % 64) // HALF


def _gather_cols(w, cols):
    cols = np.asarray(cols)
    safe = np.where(cols >= 0, cols, 0)
    g = jnp.take(w, jnp.asarray(safe, dtype=jnp.int32), axis=1)
    return jnp.where(jnp.asarray(cols >= 0)[None, :], g, jnp.zeros_like(g))


def _rope_slabs(T):
    inv = 1.0 / (ROPE_THETA ** (jnp.arange(0, HEAD_DIM, 2, dtype=F32) / HEAD_DIM))
    ang = jnp.arange(T, dtype=F32)[:, None] * inv[None, :]
    cos, sin = jnp.cos(ang), jnp.sin(ang)
    cos_slab = jnp.tile(cos, (1, 4))
    sin_slab = jnp.concatenate([-sin, -sin, sin, sin], axis=1)
    return cos_slab, sin_slab


_KINDS = ("qT", "k", "vT", "vTw", "f", "fT")


def _proj_kernel(x_ref, g_ref, w_ref, cos_ref, sin_ref, *out_refs, plan, kinds, chunk):
    outs = dict(zip(kinds, out_refs))
    x = x_ref[0]
    ms = jnp.mean(x * x, axis=-1, keepdims=True)
    xn = ((x * lax.rsqrt(ms + NORM_EPS)) * g_ref[...]).astype(BF16)
    cos = cos_ref[...]
    sin = sin_ref[...]
    n = len(plan)
    for c0 in range(0, n, chunk):
        c1 = min(c0 + chunk, n)
        r = _dot(xn, w_ref[:, c0 * LANES:c1 * LANES])
        for s in range(c0, c1):
            roped, kind, idx, opt = plan[s]
            y = r[:, (s - c0) * LANES:(s - c0 + 1) * LANES]
            if roped:
                y = y * cos + pltpu.roll(y, 64, 1) * sin
            if kind == "qT":
                if opt is not None:
                    y = y * opt
                outs[kind][0, idx] = y.T.astype(BF16)
            elif kind == "k":
                outs[kind][0, idx] = y.astype(BF16)
            elif kind == "vT":
                if opt == "ones_row_64":
                    y = jnp.where(_iota(y.shape, 1) == HEAD_DIM, 1.0, y)
                    outs[kind][0, idx, 0] = y.T.astype(BF16)
                else:
                    outs[kind][0, idx, 0, 0:LANES, :] = y.T.astype(BF16)
                    outs[kind][0, idx, 0, LANES:, :] = jnp.ones((ONES_ROWS, y.shape[0]), BF16)
            elif kind == "vTw":
                y_t = jnp.where(_iota(y.shape, 1) == HEAD_DIM, 1.0, y).T.astype(BF16)
                for sub in range(y.shape[0] // VW_TILE):
                    outs[kind][0, idx, sub] = y_t[:, sub * VW_TILE:(sub + 1) * VW_TILE]
            elif kind == "f":
                outs[kind][0, idx] = y
            else:
                outs[kind][0, idx] = y.T


def _project(x, g, w, cos_slab, sin_slab, plan, chunk=4):
    B, T, D = x.shape
    tm = KV_TILE
    assert T % tm == 0
    count = {k: sum(1 for p in plan if p[1] == k) for k in _KINDS}
    kinds = tuple(k for k in _KINDS if count[k])
    out_specs, out_shape = [], []
    for k in kinds:
        n = count[k]
        if k in ("qT", "fT"):
            out_specs.append(pl.BlockSpec((1, n, LANES, tm), lambda b, i: (b, 0, 0, i)))
            out_shape.append(jax.ShapeDtypeStruct((B, n, LANES, T), BF16 if k == "qT" else F32))
        elif k in ("k", "f"):
            out_specs.append(pl.BlockSpec((1, n, tm, LANES), lambda b, i: (b, 0, i, 0)))
            out_shape.append(jax.ShapeDtypeStruct((B, n, T, LANES), BF16 if k == "k" else F32))
        elif k == "vTw":
            sub = tm // VW_TILE
            out_specs.append(pl.BlockSpec((1, n, sub, LANES, VW_TILE), lambda b, i: (b, 0, i, 0, 0)))
            out_shape.append(jax.ShapeDtypeStruct((B, n, T // VW_TILE, LANES, VW_TILE), BF16))
        else:
            wide = any(p[1] == "vT" and p[3] != "ones_row_64" for p in plan)
            rows = LANES + ONES_ROWS if wide else LANES
            out_specs.append(pl.BlockSpec((1, n, 1, rows, tm), lambda b, i: (b, 0, i, 0, 0)))
            out_shape.append(jax.ShapeDtypeStruct((B, n, T // tm, rows, tm), BF16))
    kern = functools.partial(_proj_kernel, plan=tuple(plan), kinds=kinds, chunk=chunk)
    outs = pl.pallas_call(
        kern,
        grid=(B, T // tm),
        in_specs=[
            pl.BlockSpec((1, tm, D), lambda b, i: (b, i, 0)),
            pl.BlockSpec((1, D), lambda b, i: (0, 0)),
            pl.BlockSpec((D, len(plan) * LANES), lambda b, i: (0, 0)),
            pl.BlockSpec((tm, LANES), lambda b, i: (i, 0)),
            pl.BlockSpec((tm, LANES), lambda b, i: (i, 0)),
        ],
        out_specs=out_specs,
        out_shape=out_shape,
        compiler_params=_params(("parallel", "parallel")),
        name="norm_proj_rope",
    )(x, g.reshape(1, D), w, cos_slab, sin_slab)
    return dict(zip(kinds, outs))


def _head_rows(slab_t, which):
    row = _iota(slab_t.shape, 0)
    keep = ((row % 64) // HALF) == which
    return jnp.where(keep, slab_t, jnp.zeros_like(slab_t))


def _tree(op, xs):
    while len(xs) > 1:
        xs = [op(xs[i], xs[i + 1]) if i + 1 < len(xs) else xs[i] for i in range(0, len(xs), 2)]
    return xs[0]


def _fold_rows_max(x, ways=4):
    rows, n = x.shape
    per = rows // ways
    parts = [jnp.max(x[i * per:(i + 1) * per].reshape(per // 8, 8, n), axis=0) for i in range(ways)]
    return _tree(jnp.maximum, parts)


ROW_BLOCK = 64


def _online_update(s_sc, p_sc, v_aug, m_ref, acc_ref):
    tk, n = s_sc.shape
    m_prev = m_ref[...]
    m_tile = jnp.max(_fold_rows_max(s_sc[...]), axis=0, keepdims=True)
    m_new = jnp.maximum(m_prev, m_tile)
    alpha = jnp.exp2(m_prev - m_new)
    for r in range(tk // ROW_BLOCK):
        rows = slice(r * ROW_BLOCK, (r + 1) * ROW_BLOCK)
        p_sc[rows, :] = jnp.exp2(s_sc[rows, :] - m_new).astype(BF16)
    acc_ref[...] = alpha * acc_ref[...] + _dot(v_aug, p_sc[...])
    m_ref[...] = m_new


def _flash_pipeline(n, put_scores, update, buf_a, buf_b, put_last=None):
    if put_last is None:
        put_scores(buf_a, 0)
    else:
        @pl.when(n > 0)
        def _():
            put_scores(buf_a, 0)

    def pair(p, carry):
        j = 2 * p
        put_scores(buf_b, j + 1)
        update(buf_a, j)

        @pl.when(j + 2 < n)
        def _():
            put_scores(buf_a, j + 2)
            update(buf_b, j + 1)

        return carry

    lax.fori_loop(0, n // 2, pair, 0)
    in_b = jnp.logical_and(n > 0, n % 2 == 0)
    in_a = n % 2 == 1

    def finish(cur, other):
        if put_last is not None:
            put_last(other, n)
        update(cur, n - 1)
        if put_last is not None:
            update(other, n)

    pl.when(in_b)(lambda: finish(buf_b, buf_a))
    pl.when(in_a)(lambda: finish(buf_a, buf_b))
    if put_last is not None:
        @pl.when(n == 0)
        def _():
            put_last(buf_a, 0)
            update(buf_a, 0)


def _softmax_keys(s):
    m = jnp.max(s, axis=0, keepdims=True)
    m = jnp.where(m > NEG_INF, m, 0.0)
    e = jnp.exp2(s - m)
    return e / jnp.maximum(jnp.sum(e, axis=0, keepdims=True), 1e-30)


def _init_flash(m_ref, acc_ref):
    m_ref[...] = jnp.full(m_ref.shape, M_FLOOR, F32)
    acc_ref[...] = jnp.zeros(acc_ref.shape, F32)


DIFF_HEADS_PER_STEP = 2


def _diff_attn_kernel(lam_ref, q_ref, k_ref, v_ref, g_ref, o_ref, m_sc, acc_sc, sa_sc, sb_sc,
                      p_sc, *, tq, lambda_init):
    qi = pl.program_id(2)
    HS = DIFF_HEADS_PER_STEP
    qs = [[_head_rows(q_ref[0, hh], c) for c in range(2)] for hh in range(HS)]
    for s in range(2 * HS):
        _init_flash(m_sc.at[s], acc_sc.at[s])

    def put_scores(buf, j, masked=False):
        for hh in range(HS):
            k = k_ref[0, hh, pl.ds(pl.multiple_of(j * tq, tq), tq), :]
            for c in range(2):
                s = _dot(k, qs[hh][c])
                if masked:
                    s = jnp.where(_iota(s.shape, 0) <= _iota(s.shape, 1), s, NEG_INF)
                buf[2 * hh + c] = s

    def put_diagonal(buf, j):
        put_scores(buf, j, masked=True)

    def update(buf, j):
        for hh in range(HS):
            v_aug = v_ref[0, hh, j]
            for c in range(2):
                s = 2 * hh + c
                _online_update(buf.at[s], p_sc.at[s], v_aug, m_sc.at[s], acc_sc.at[s])

    _flash_pipeline(qi, put_scores, update, sa_sc, sb_sc, put_last=put_diagonal)

    lam = lam_ref[...]
    s01 = jnp.sum(lam[0:1] * lam[1:2], axis=-1, keepdims=True)
    s23 = jnp.sum(lam[2:3] * lam[3:4], axis=-1, keepdims=True)
    lam_val = jnp.exp(s01) - jnp.exp(s23) + lambda_init
    for hh in range(HS):
        a0, a1 = acc_sc[2 * hh], acc_sc[2 * hh + 1]
        o0 = a0[0:LANES] / jnp.maximum(a0[LANES:LANES + 1], 1e-30)
        o1 = a1[0:LANES] / jnp.maximum(a1[LANES:LANES + 1], 1e-30)
        o = o0 - lam_val * o1
        y = o * lax.rsqrt(jnp.mean(o * o, axis=0, keepdims=True) + NORM_EPS)
        y = (y * g_ref[...]) * (1.0 - lambda_init)
        o_ref[0, :, hh * LANES:(hh + 1) * LANES] = y.T.astype(o_ref.dtype)


def _diff_attention(sl, lam, subln_g, lambda_init):
    q_t, k, v_t = sl["qT"], sl["k"], sl["vT"]
    B, H, T, _ = k.shape
    tq = KV_TILE
    nk = T // tq
    kern = functools.partial(_diff_attn_kernel, tq=tq, lambda_init=lambda_init)
    HS = DIFF_HEADS_PER_STEP
    ns = 2 * HS
    return pl.pallas_call(
        kern,
        grid=(B, H // HS, T // tq),
        in_specs=[
            pl.BlockSpec((4, HEAD_DIM), lambda b, h, i: (0, 0)),
            pl.BlockSpec((1, HS, LANES, tq), lambda b, h, i: (b, h, 0, i)),
            pl.BlockSpec((1, HS, T, LANES), lambda b, h, i: (b, h, 0, 0)),
            pl.BlockSpec((1, HS, nk, LANES + ONES_ROWS, tq), lambda b, h, i: (b, h, 0, 0, 0)),
            pl.BlockSpec((LANES, 1), lambda b, h, i: (0, 0)),
        ],
        out_specs=pl.BlockSpec((1, tq, HS * LANES), lambda b, h, i: (b, i, h)),
        out_shape=jax.ShapeDtypeStruct((B, T, H * LANES), BF16),
        scratch_shapes=[
            pltpu.VMEM((ns, 1, tq), F32),
            pltpu.VMEM((ns, LANES + ONES_ROWS, tq), F32),
            pltpu.VMEM((ns, tq, tq), F32),
            pltpu.VMEM((ns, tq, tq), F32),
            pltpu.VMEM((ns, tq, tq), BF16),
        ],
        compiler_params=_params(("parallel", "parallel", "arbitrary")),
        name="diff_attention",
    )(lam, q_t, k, v_t, subln_g.reshape(LANES, 1))


def _dsa_select_kernel(qi_ref, ki_ref, misc_ref, tril_ref, bias_ref, key_sc, *, tq, ck, nk, topk):
    S = key_sc.shape[0]
    t0 = pl.program_id(1) * (S * tq)
    nvalid = (t0 + S * tq - 1) // ck + 1
    lanes = [slice(s * tq, (s + 1) * tq) for s in range(S)]
    w = [misc_ref[0, 0, :, lanes[s]] for s in range(S)]
    qh = [jnp.concatenate([_head_rows(qi_ref[0, h // 2, :, lanes[s]], h % 2)
                           for h in range(IDX_HEADS)], axis=1) for s in range(S)]
    t_q = [t0 + s * tq + _iota((ck, tq), 1) for s in range(S)]

    def causal(c, s):
        return (c * ck + _iota((ck, tq), 0)) <= t_q[s]

    def fill(c, s):
        kk = ki_ref[0, 0, c * ck:(c + 1) * ck, :]
        r = _dot(kk, qh[s])
        score = jnp.zeros((ck, tq), F32)
        for h in range(IDX_HEADS):
            score = score + w[s][h:h + 1, :] * jnp.maximum(r[:, h * tq:(h + 1) * tq], 0.0)
        bits = pltpu.bitcast(score, jnp.int32)
        key = bits ^ ((bits >> 31) & jnp.int32(0x7FFFFFFF))
        key = jnp.where(score == 0.0, 0, key)
        key_sc[s, c] = jnp.where(causal(c, s), key, INT_MIN)

    kf = float(topk)

    def select_threshold(n):
        for c in range(n):
            for s in range(S):
                fill(c, s)

        def count(pred, s):
            acc = jnp.zeros((ck // 8, tq), F32)
            for c in range(n):
                ind = pred(key_sc[s, c], c)
                acc = acc + jnp.sum(ind.reshape(8, ck // 8, tq), axis=0)
            return jnp.sum(acc, axis=0, keepdims=True)

        zero = jnp.zeros((1, tq), jnp.int32)
        ans = tuple(jnp.where(count(lambda kc, c: jnp.where(kc >= zero, 1.0, 0.0), s) >= kf, 0, INT_MIN)
                    for s in range(S))

        def bit_step(i, ans):
            out = []
            for s in range(S):
                cand = ans[s] | (jnp.int32(1) << (30 - i))
                cnt = count(lambda kc, c: jnp.where(kc >= cand, 1.0, 0.0), s)
                out.append(jnp.where(cnt >= kf, cand, ans[s]))
            return tuple(out)

        ans = lax.fori_loop(0, 31, bit_step, ans)
        cnt_gt = tuple(count(lambda kc, c: jnp.where(kc > ans[s], 1.0, 0.0), s) for s in range(S))
        cnt_eq = tuple(count(lambda kc, c: jnp.where(kc == ans[s],
                                                     jnp.where(causal(c, s), 1.0, 0.0), 0.0), s)
                       for s in range(S))
        return ans, cnt_gt, cnt_eq

    ans, cnt_gt, cnt_eq = lax.switch(
        nvalid - 1, [functools.partial(select_threshold, n) for n in range(1, nk + 1)])
    need = [kf - cnt_gt[s] for s in range(S)]
    has_tie = _tree(jnp.maximum, [jnp.max(cnt_eq[s] - need[s]) for s in range(S)]) > 0.0

    @pl.when(jnp.logical_not(has_tie))
    def _():
        def emit(c, carry):
            for s in range(S):
                sel = jnp.where(key_sc[s, c] >= ans[s], jnp.where(causal(c, s), 0.0, NEG_INF), NEG_INF)
                bias_ref[0, s, c] = sel.astype(BF16)
            return carry
        lax.fori_loop(0, nvalid, emit, 0)

    @pl.when(has_tie)
    def _():
        def emit(c, carry):
            out = []
            for s in range(S):
                kc = key_sc[s, c]
                eq = jnp.where(kc == ans[s], jnp.where(causal(c, s), 1.0, 0.0), 0.0)
                pre = _dot(tril_ref[...], eq.astype(BF16)) + carry[s]
                take = jnp.where(kc > ans[s], 1.0, jnp.where(pre < need[s], eq, 0.0))
                sel = jnp.where(take > 0.5, jnp.where(causal(c, s), 0.0, NEG_INF), NEG_INF)
                bias_ref[0, s, c] = sel.astype(BF16)
                out.append(carry[s] + jnp.sum(eq, axis=0, keepdims=True))
            return tuple(out)
        lax.fori_loop(0, nvalid, emit, tuple(jnp.zeros((1, tq), F32) for _ in range(S)))

    def blank(c, carry):
        for s in range(S):
            bias_ref[0, s, c] = jnp.full((ck, tq), NEG_INF, BF16)
        return carry

    lax.fori_loop(nvalid, nk, blank, 0)


def _dsa_select(sl, lay):
    q_t, k, misc_t = sl["qT"], sl["k"], sl["fT"]
    B, _, T, _ = k.shape
    tq, ck = SEL_Q_TILE, KV_TILE
    nk = T // ck
    topk = min(DSA_TOPK, T // 4)
    tril = jnp.asarray(np.tril(np.ones((ck, ck), np.float32), -1), BF16)
    kern = functools.partial(_dsa_select_kernel, tq=tq, ck=ck, nk=nk, topk=topk)
    S = SEL_TILES_PER_STEP
    assert ck % (S * tq) == 0 and T % (S * tq) == 0
    return pl.pallas_call(
        kern,
        grid=(B, T // (S * tq)),
        in_specs=[
            pl.BlockSpec((1, 2, LANES, S * tq), lambda b, i: (b, lay["qi"] // 2, 0, i)),
            pl.BlockSpec((1, 1, T, LANES), lambda b, i: (b, lay["ki"], 0, 0)),
            pl.BlockSpec((1, 1, LANES, S * tq), lambda b, i: (b, lay["misc"], 0, i)),
            pl.BlockSpec((ck, ck), lambda b, i: (0, 0)),
        ],
        out_specs=pl.BlockSpec((1, S, nk, ck, tq), lambda b, i: (b, i, 0, 0, 0)),
        out_shape=jax.ShapeDtypeStruct((B, T // tq, nk, ck, tq), BF16),
        scratch_shapes=[pltpu.VMEM((S, nk, ck, tq), jnp.int32)],
        compiler_params=_params(("parallel", "parallel")),
        name="dsa_select",
    )(q_t, k, misc_t, tril)


def _store_head_pairs(o_ref, o, n_heads, tq, col0=0, row0=0):
    for p in range(n_heads // 2):
        even = o[0:HEAD_DIM, (2 * p) * tq:(2 * p + 1) * tq]
        odd = o[0:HEAD_DIM, (2 * p + 1) * tq:(2 * p + 2) * tq]
        pair = jnp.concatenate([even, odd], axis=0)
        cols = slice(col0 + p * LANES, col0 + (p + 1) * LANES)
        o_ref[0, row0:row0 + tq, cols] = pair.T.astype(o_ref.dtype)


def _normalized(acc):
    return acc[0:HEAD_DIM] / jnp.maximum(acc[HEAD_DIM:HEAD_DIM + 1], 1e-30)


def _dsa_attn_kernel(q_ref, k_ref, v_ref, bias_ref, o_ref, qst, m_sc, acc_sc, sa_sc, sb_sc,
                     p_sc, *, tq, tk):
    qi = pl.program_id(1)
    H = A_HEADS
    S = sa_sc.shape[0]
    nsub = bias_ref.shape[1] // S
    for s in range(S):
        for h in range(H):
            qst[s, :, h * tq:(h + 1) * tq] = _head_rows(q_ref[0, h // 2, :, s * tq:(s + 1) * tq], h % 2)
        _init_flash(m_sc.at[s], acc_sc.at[s])
    nkv = (qi * S * tq) // tk + 1

    def put_scores(buf, j):
        k = k_ref[0, 0, pl.ds(pl.multiple_of(j * tk, tk), tk), :]
        for s in range(S):
            parts = [bias_ref[0, s * nsub + u, j].astype(F32) for u in range(nsub)]
            b = parts[0] if len(parts) == 1 else jnp.concatenate(parts, axis=1)
            buf[s] = _dot(k, qst[s]) + jnp.concatenate([b] * H, axis=1)

    def update(buf, j):
        for s in range(S):
            _online_update(buf.at[s], p_sc.at[s], v_ref[0, 0, j], m_sc.at[s], acc_sc.at[s])

    _flash_pipeline(nkv, put_scores, update, sa_sc, sb_sc)
    for s in range(S):
        _store_head_pairs(o_ref, _normalized(acc_sc[s]), H, tq, row0=s * tq)


def _dsa_attention(sl, bias, lay):
    q_t, k, v_t = sl["qT"], sl["k"], sl["vT"]
    B, _, T, _ = k.shape
    tq, tk = DSA_Q_TILE, KV_TILE
    nk = T // tk
    H = A_HEADS
    N = H * tq
    sel_tq = bias.shape[4]
    S = tk // tq
    nsub = S * tq // sel_tq
    kern = functools.partial(_dsa_attn_kernel, tq=tq, tk=tk)
    return pl.pallas_call(
        kern,
        grid=(B, T // (S * tq)),
        in_specs=[
            pl.BlockSpec((1, H // 2, LANES, S * tq), lambda b, i: (b, lay["qA"] // (H // 2), 0, i)),
            pl.BlockSpec((1, 1, T, LANES), lambda b, i: (b, lay["kA"], 0, 0)),
            pl.BlockSpec((1, 1, nk, LANES, tk), lambda b, i: (b, lay["vA"], 0, 0, 0)),
            pl.BlockSpec((1, nsub, nk, tk, sel_tq), lambda b, i: (b, i, 0, 0, 0)),
        ],
        out_specs=pl.BlockSpec((1, S * tq, H * HEAD_DIM), lambda b, i: (b, i, 0)),
        out_shape=jax.ShapeDtypeStruct((B, T, H * HEAD_DIM), BF16),
        scratch_shapes=[
            pltpu.VMEM((S, LANES, N), BF16),
            pltpu.VMEM((S, 1, N), F32),
            pltpu.VMEM((S, LANES, N), F32),
            pltpu.VMEM((S, tk, N), F32),
            pltpu.VMEM((S, tk, N), F32),
            pltpu.VMEM((S, tk, N), BF16),
        ],
        compiler_params=_params(("parallel", "arbitrary")),
        name="dsa_attention",
    )(q_t, k, v_t, bias)


def _compress_kernel(x_ref, pe_ref, w1_ref, w2_ref, o_ref, ot_ref, *, n_ch):
    x = x_ref[0, 0]
    xt = (x + pe_ref[0, 0]).astype(BF16)
    xb = (x + pe_ref[0, 1]).astype(BF16)
    for g in range(B_KV_GROUPS):
        a = _dot(xt, w1_ref[0, g, 0])
        b = _dot(xb, w1_ref[0, g, 1])
        h = jax.nn.gelu(a + pltpu.roll(b, n_ch - 1, 0))
        r = _dot(h.astype(BF16), w2_ref[0])
        o_ref[0, 0, g] = r.astype(BF16)
        ot_ref[0, 0, g] = r.T.astype(BF16)


def _compress(flat, pe, w1, w2):
    B, _, n_ch, W = flat.shape
    G = B_KV_GROUPS
    kern = functools.partial(_compress_kernel, n_ch=n_ch)
    return pl.pallas_call(
        kern,
        grid=(2, B),
        in_specs=[
            pl.BlockSpec((1, 1, n_ch, W), lambda s, b: (b, s, 0, 0)),
            pl.BlockSpec((1, 2, 1, W), lambda s, b: (s, 0, 0, 0)),
            pl.BlockSpec((1, G, 2, W, CMP_HIDDEN), lambda s, b: (s, 0, 0, 0, 0)),
            pl.BlockSpec((1, CMP_HIDDEN, LANES), lambda s, b: (s, 0, 0)),
        ],
        out_specs=[
            pl.BlockSpec((1, 1, G, n_ch, LANES), lambda s, b: (s, b, 0, 0, 0)),
            pl.BlockSpec((1, 1, G, LANES, n_ch), lambda s, b: (s, b, 0, 0, 0)),
        ],
        out_shape=[
            jax.ShapeDtypeStruct((2, B, G, n_ch, LANES), BF16),
            jax.ShapeDtypeStruct((2, B, G, LANES, n_ch), BF16),
        ],
        compiler_params=_params(("parallel", "parallel")),
        name="nsa_compress",
    )(flat, pe, w1, w2)


def _nsa_kernel(q_ref, kc_ref, vc_ref, ks_ref, vs_ref, kw_ref, vw_ref, misc_ref, ov_ref, ex_ref,
                cz_ref, wz_ref, o_ref, qst, m_sc, acc_sc, sa_sc, sb_sc, p_sc, mw_sc, accw_sc,
                sw_sc, pw_sc, oc_sc, *, tq, tk, T, n_s, n_sel, gate_row0):
    qi = pl.program_id(1)
    t0 = qi * tq
    G, J = B_KV_GROUPS, B_PER_GROUP
    N = J * tq
    for g in range(G):
        for j in range(J):
            qst[g, 0:LANES, j * tq:(j + 1) * tq] = _head_rows(q_ref[0, 2 * g + j // 2], j % 2)

    def q_time(shape):
        return t0 + (_iota(shape, 1) % tq)

    wlen = min(WINDOW + tq, T)
    wstart = pl.multiple_of(jnp.maximum(t0 - WINDOW, 0), tq)
    n_wt = wlen // tq
    wt = wstart // VW_TILE
    wz = wz_ref[jnp.minimum(qi, n_wt - 1)].astype(F32)
    wz = jnp.concatenate([wz] * J, axis=1)
    for g in range(G):
        kw = kw_ref[0, g, pl.ds(wstart, wlen), :]
        vw = jnp.concatenate([vw_ref[0, g, wt + i] for i in range(wlen // VW_TILE)], axis=1)
        sw_sc[g] = _dot(kw, qst[g, 0:LANES, :]) + wz
        _init_flash(mw_sc.at[g], accw_sc.at[g])
        _online_update(sw_sc.at[g], pw_sc.at[g], vw, mw_sc.at[g], accw_sc.at[g])

    rows = -(-n_s // 8) * 8
    blk = _iota((rows, tq), 0)
    t_q = t0 + _iota((rows, tq), 1)
    cur = t_q // SLC_BLOCK
    forced = (blk == 0) | (blk == cur) | (blk == cur - 1)
    admissible = blk * SLC_BLOCK <= t_q
    blk8 = _iota((8, tq), 0)
    for g in range(G):
        q = qst[g, 0:LANES, :]
        kc = kc_ref[0, 0, g]
        n_ch = kc.shape[0]
        s_c = _dot(kc, q)
        cmp_end = _iota((n_ch, N), 0) * CMP_STRIDE + (CMP_BLOCK - 1)
        s_c = jnp.where(cmp_end <= q_time((n_ch, N)), s_c, NEG_INF)
        p_c = _softmax_keys(s_c)
        oc_sc[g] = _dot(vc_ref[0, 0, g], p_c.astype(BF16))[0:HEAD_DIM]
        psum = p_c[:, 0:tq]
        for j in range(1, J):
            psum = psum + p_c[:, j * tq:(j + 1) * tq]
        p_hi = psum.astype(BF16)
        p_lo = (psum - p_hi.astype(F32)).astype(BF16)
        imp = _dot(ov_ref[...], p_hi) + _dot(ov_ref[...], p_lo)
        sc = jnp.where(forced, jnp.inf, imp[0:rows])
        sc = jnp.where(admissible, sc, NEG_INF)
        groups = [sc[8 * r:8 * r + 8] for r in range(rows // 8)]
        ranks = [jnp.zeros((8, tq), F32) for _ in groups]
        for m in range(n_s):
            cm = sc[m:m + 1, :]
            for r, grp in enumerate(groups):
                gt = jnp.where(cm > grp, 1.0, 0.0)
                ge = jnp.where(cm >= grp, 1.0, 0.0)
                if 8 * r + 7 <= m:
                    first = gt
                elif 8 * r > m:
                    first = ge
                else:
                    first = jnp.where(blk8 + 8 * r > m, ge, gt)
                ranks[r] = ranks[r] + first
        rank = jnp.concatenate(ranks, axis=0)
        drop = jnp.where(rank < float(n_sel), 0.0, 1.0)
        if rows < LANES:
            drop = jnp.concatenate([drop, jnp.ones((LANES - rows, tq), F32)], axis=0)
        drop = drop.astype(BF16)
        qst[g, LANES:2 * LANES, :] = jnp.concatenate([drop] * J, axis=1)

    jd = t0 // tk
    off = (t0 - jd * tk) // tq
    n_off = tk // tq
    for g in range(G):
        _init_flash(m_sc.at[g], acc_sc.at[g])

    def put_scores(buf, j):
        cz = cz_ref[jnp.where(j == jd, off, n_off)].astype(F32)
        cz = jnp.concatenate([cz] * J, axis=1)
        for g in range(G):
            k = ks_ref[0, g, pl.ds(pl.multiple_of(j * tk, tk), tk), :]
            lhs = jnp.concatenate([k, ex_ref[j]], axis=1)
            buf[g] = _dot(lhs, qst[g]) + cz

    def update(buf, j):
        for g in range(G):
            _online_update(buf.at[g], p_sc.at[g], vs_ref[0, g, j], m_sc.at[g], acc_sc.at[g])

    _flash_pipeline(jd + 1, put_scores, update, sa_sc, sb_sc)

    gates = jax.nn.sigmoid(misc_ref[0, 0])
    for g in range(G):
        def gate_row(c, g=g):
            rws = [gate_row0 + (g * J + j) * 3 + c for j in range(J)]
            return jnp.concatenate([gates[r:r + 1, :] for r in rws], axis=1)

        o_s = _normalized(acc_sc[g])
        o_w = _normalized(accw_sc[g])
        o = gate_row(0) * oc_sc[g] + (gate_row(1) * o_s + gate_row(2) * o_w)
        _store_head_pairs(o_ref, o, J, tq, col0=g * J * HEAD_DIM)


def _nsa_attention(sl, cmp_k, cmp_vt, lay):
    q_t, k, v_t, misc_t = sl["qT"], sl["k"], sl["vT"], sl["fT"]
    B, _, T, _ = k.shape
    G, J = B_KV_GROUPS, B_PER_GROUP
    tq, tk = NSA_Q_TILE, KV_TILE
    nk = T // tk
    n_ch = T // CMP_STRIDE
    n_c = n_ch - CMP_BLOCK // CMP_STRIDE + 1
    n_s = T // SLC_BLOCK
    n_sel = min(SLC_TOPN, n_s)
    assert n_s <= LANES
    c0 = np.arange(n_ch) * CMP_STRIDE
    s0 = np.arange(LANES) * SLC_BLOCK
    ov = ((c0[None, :] < s0[:, None] + SLC_BLOCK) & (c0[None, :] + CMP_BLOCK > s0[:, None]))
    ov = ov & (np.arange(n_ch)[None, :] < n_c) & (np.arange(LANES)[:, None] < n_s)
    ov = jnp.asarray(ov.astype(np.float32), BF16)
    pos = np.arange(T).reshape(nk, tk, 1)
    ex = (pos // SLC_BLOCK == np.arange(LANES).reshape(1, 1, LANES))
    ex = jnp.asarray(ex.astype(np.float32) * -BIG, BF16)
    n_off = tk // tq
    kp = np.arange(tk).reshape(1, tk, 1)
    tl = np.arange(tq).reshape(1, 1, tq) + np.arange(n_off + 1).reshape(n_off + 1, 1, 1) * tq
    cz = np.where((kp <= tl) | (np.arange(n_off + 1).reshape(-1, 1, 1) == n_off), 0.0, -BIG)
    cz = jnp.asarray(cz.astype(np.float32), BF16)
    wlen = min(WINDOW + tq, T)
    n_wt = wlen // tq
    assert wlen % tq == 0 and T >= wlen
    kp = np.arange(wlen).reshape(1, wlen, 1)
    tl = np.arange(tq).reshape(1, 1, tq)
    early = kp <= tl + np.arange(n_wt).reshape(n_wt, 1, 1) * tq
    late = (kp > tl) & (kp <= tl + WINDOW)
    band = np.where(np.arange(n_wt).reshape(n_wt, 1, 1) == n_wt - 1, late, early)
    wz = jnp.asarray(np.where(band, 0.0, -BIG).astype(np.float32), BF16)
    kern = functools.partial(_nsa_kernel, tq=tq, tk=tk, T=T, n_s=n_s, n_sel=n_sel,
                             gate_row0=lay["gate_row0"])
    N = J * tq
    for name in ("qB", "ks", "vs", "kw", "vw"):
        assert lay[name] % (2 * G if name == "qB" else G) == 0
    kslab = lambda off: pl.BlockSpec((1, G, T, LANES), lambda b, i: (b, off // G, 0, 0))
    return pl.pallas_call(
        kern,
        grid=(B, T // tq),
        in_specs=[
            pl.BlockSpec((1, 2 * G, LANES, tq), lambda b, i: (b, lay["qB"] // (2 * G), 0, i)),
            pl.BlockSpec((1, 1, G, n_ch, LANES), lambda b, i: (0, b, 0, 0, 0)),
            pl.BlockSpec((1, 1, G, LANES, n_ch), lambda b, i: (1, b, 0, 0, 0)),
            kslab(lay["ks"]),
            pl.BlockSpec((1, G, nk, LANES, tk), lambda b, i: (b, lay["vs"] // G, 0, 0, 0)),
            kslab(lay["kw"]),
            pl.BlockSpec((1, G, T // VW_TILE, LANES, VW_TILE),
                         lambda b, i: (b, lay["vw"] // G, 0, 0, 0)),
            pl.BlockSpec((1, 1, LANES, tq), lambda b, i: (b, lay["misc"], 0, i)),
            pl.BlockSpec((LANES, n_ch), lambda b, i: (0, 0)),
            pl.BlockSpec((nk, tk, LANES), lambda b, i: (0, 0, 0)),
            pl.BlockSpec((n_off + 1, tk, tq), lambda b, i: (0, 0, 0)),
            pl.BlockSpec((n_wt, wlen, tq), lambda b, i: (0, 0, 0)),
        ],
        out_specs=pl.BlockSpec((1, tq, B_HEADS * HEAD_DIM), lambda b, i: (b, i, 0)),
        out_shape=jax.ShapeDtypeStruct((B, T, B_HEADS * HEAD_DIM), BF16),
        scratch_shapes=[
            pltpu.VMEM((G, 2 * LANES, N), BF16),
            pltpu.VMEM((G, 1, N), F32),
            pltpu.VMEM((G, LANES, N), F32),
            pltpu.VMEM((G, tk, N), F32),
            pltpu.VMEM((G, tk, N), F32),
            pltpu.VMEM((G, tk, N), BF16),
            pltpu.VMEM((G, 1, N), F32),
            pltpu.VMEM((G, LANES, N), F32),
            pltpu.VMEM((G, wlen, N), F32),
            pltpu.VMEM((G, wlen, N), BF16),
            pltpu.VMEM((G, HEAD_DIM, N), F32),
        ],
        compiler_params=_params(("parallel", "arbitrary")),
        name="nsa_attention",
    )(q_t, cmp_k, cmp_vt, k, v_t, k, sl["vTw"], misc_t, ov, ex, cz, wz)


def _rms(x, g):
    ms = jnp.mean(x * x, axis=-1, keepdims=True)
    return (x * lax.rsqrt(ms + NORM_EPS)) * g


def _layer_tail_kernel(*refs, n_in, final_norm):
    h_ref = refs[0]
    o_refs = refs[1:1 + n_in]
    w_refs = refs[1 + n_in:1 + 2 * n_in]
    g_ref, wu_ref, wd_ref = refs[1 + 2 * n_in:4 + 2 * n_in]
    rest = refs[4 + 2 * n_in:]
    if final_norm:
        fg_ref, out_ref, xn_sc = rest
    else:
        out_ref, xn_sc = rest
    f = pl.program_id(1)

    @pl.when(f == 0)
    def _():
        mix = _dot(o_refs[0][...], w_refs[0][...])
        for i in range(1, n_in):
            mix = mix + _dot(o_refs[i][...], w_refs[i][...])
        x = h_ref[...] + mix
        xn_sc[...] = _rms(x, g_ref[...]).astype(BF16)
        out_ref[...] = x

    u = _dot(xn_sc[...], wu_ref[...])
    a = jnp.square(jnp.maximum(u, 0.0)).astype(BF16)
    out_ref[...] += _dot(a, wd_ref[...])

    if final_norm:
        @pl.when(f == pl.num_programs(1) - 1)
        def _():
            out_ref[...] = _rms(out_ref[...], fg_ref[...])


def _layer_tail(h2, outs, ws, g, w_up, w_down, final_g=None, tm=1024, tf=1024):
    N, D = h2.shape
    F = w_up.shape[1]
    tm = min(tm, N)
    n_in = len(outs)
    in_specs = [pl.BlockSpec((tm, D), lambda i, f: (i, 0))]
    in_specs += [pl.BlockSpec((tm, o.shape[1]), lambda i, f: (i, 0)) for o in outs]
    in_specs += [pl.BlockSpec(w.shape, lambda i, f: (0, 0)) for w in ws]
    in_specs += [
        pl.BlockSpec((1, D), lambda i, f: (0, 0)),
        pl.BlockSpec((D, tf), lambda i, f: (0, f)),
        pl.BlockSpec((tf, D), lambda i, f: (f, 0)),
    ]
    args = [h2, *outs, *ws, g.reshape(1, D), w_up, w_down]
    if final_g is not None:
        in_specs.append(pl.BlockSpec((1, D), lambda i, f: (0, 0)))
        args.append(final_g.reshape(1, D))
    kern = functools.partial(_layer_tail_kernel, n_in=n_in, final_norm=final_g is not None)
    return pl.pallas_call(
        kern,
        grid=(N // tm, F // tf),
        in_specs=in_specs,
        out_specs=pl.BlockSpec((tm, D), lambda i, f: (i, 0)),
        out_shape=jax.ShapeDtypeStruct((N, D), F32),
        scratch_shapes=[pltpu.VMEM((tm, D), BF16)],
        compiler_params=_params(("parallel", "arbitrary")),
        name="out_proj_mlp",
    )(*args)


def _even_layout():
    offs = {}
    o = 0
    for name, size in (("qa", 512), ("ka", 64), ("va", 64), ("qi", 256), ("ki", 64), ("wi", 4),
                       ("qb", 512), ("kvb", 768), ("gb", 24)):
        offs[name] = o
        o += size
    kvb = lambda which, g: offs["kvb"] + (which * B_KV_GROUPS + g) * HEAD_DIM
    cols, plan, lay = [], [], {}
    n = {k: 0 for k in _KINDS}

    def add(c, roped, kind, opt=None):
        cols.append(c)
        plan.append((roped, kind, n[kind], opt))
        n[kind] += 1
        return n[kind] - 1

    lay["qA"] = n["qT"]
    for p in range(4):
        add(_pair_cols(offs["qa"] + 2 * p * 64, offs["qa"] + (2 * p + 1) * 64), True, "qT", Q_SCALE)
    lay["qB"] = n["qT"]
    for p in range(4):
        add(_pair_cols(offs["qb"] + 2 * p * 64, offs["qb"] + (2 * p + 1) * 64), True, "qT", Q_SCALE)
    lay["qi"] = n["qT"]
    for p in range(2):
        add(_pair_cols(offs["qi"] + 2 * p * 64, offs["qi"] + (2 * p + 1) * 64), True, "qT")
    lay["kA"] = add(_pair_cols(offs["ka"], offs["ka"]), True, "k")
    lay["ki"] = add(_pair_cols(offs["ki"], offs["ki"]), True, "k")
    lay["ks"] = n["k"]
    for g in range(2):
        add(_pair_cols(kvb(2, g), kvb(2, g)), True, "k")
    lay["kw"] = n["k"]
    for g in range(2):
        add(_pair_cols(kvb(4, g), kvb(4, g)), True, "k")
    lay["kc"] = add(_pair_cols(kvb(0, 0), kvb(0, 1)), True, "f")
    lay["vc"] = add(_pair_cols(kvb(1, 0), kvb(1, 1)), False, "f")
    misc = np.full(LANES, -1)
    misc[0:IDX_HEADS] = offs["wi"] + np.arange(IDX_HEADS)
    lay["gate_row0"] = 8
    misc[8:8 + 24] = offs["gb"] + np.arange(24)
    lay["misc"] = add(misc, False, "fT")
    lay["vw"] = n["vTw"]
    for g in range(2):
        add(_head_cols(kvb(5, g)), False, "vTw")
    lay["vs"] = n["vT"]
    for g in range(2):
        add(_head_cols(kvb(3, g)), False, "vT", "ones_row_64")
    lay["vA"] = add(_head_cols(offs["va"]), False, "vT", "ones_row_64")
    return np.concatenate(cols), plan, lay


def _odd_layout():
    cols, plan = [], []
    for h in range(C_HEADS):
        cols.append(_pair_cols(h * 128, h * 128 + 64))
        plan.append((True, "qT", h, Q_SCALE))
    for h in range(C_HEADS):
        cols.append(_pair_cols(1024 + h * 128, 1024 + h * 128 + 64))
        plan.append((True, "k", h, None))
    for h in range(C_HEADS):
        cols.append(2048 + h * 128 + np.arange(LANES))
        plan.append((False, "vT", h, "ones_rows_below"))
    return np.concatenate(cols), plan


def _compress_weights(pe, w1, w2):
    d = _PAIR_D
    which = _PAIR_WHICH
    pe_l = pe[:, :, d]
    pe_l = pe_l.reshape(2, 2, 1, CMP_STRIDE * LANES)
    w1r = w1.astype(BF16).reshape(2, CMP_BLOCK, HEAD_DIM, CMP_HIDDEN)[:, :, d, :]
    per_g = []
    for g in range(B_KV_GROUPS):
        keep = jnp.asarray(which == g)[None, None, :, None]
        per_g.append(jnp.where(keep, w1r, jnp.zeros_like(w1r)))
    w1g = jnp.stack(per_g, axis=1)
    w1g = w1g.reshape(2, B_KV_GROUPS, 2, CMP_STRIDE * LANES, CMP_HIDDEN)
    w2b = w2.astype(BF16)
    w2k = w2b[0][:, d]
    w2v = w2b[1][:, np.arange(LANES) % HEAD_DIM]
    w2l = jnp.stack([w2k, w2v], axis=0)
    return pe_l, w1g, w2l


def _even_mixer(h, norm_g, w_in, cmp_pe, cmp_w1, cmp_w2, w_out, cos_slab, sin_slab):
    B, T, D = h.shape
    cols, plan, lay = _even_layout()
    w = _gather_cols(w_in.astype(BF16), cols)
    sl = _project(h, norm_g, w, cos_slab, sin_slab, plan)

    bias = _dsa_select(sl, lay)
    o_a = _dsa_attention(sl, bias, lay)

    n_ch = T // CMP_STRIDE
    assert (lay["kc"], lay["vc"]) == (0, 1) and sl["f"].shape[1] == 2
    flat = sl["f"].reshape(B, 2, n_ch, CMP_STRIDE * LANES)
    pe_l, w1g, w2l = _compress_weights(cmp_pe, cmp_w1, cmp_w2)
    cmp_k, cmp_vt = _compress(flat, pe_l, w1g, w2l)
    o_b = _nsa_attention(sl, cmp_k, cmp_vt, lay)

    na = A_HEADS * HEAD_DIM
    wo = w_out.astype(BF16)
    return [o_a.reshape(B * T, -1), o_b.reshape(B * T, -1)], [wo[:na], wo[na:]]


def _odd_mixer(h, norm_g, w_in, lam, subln_g, w_out, cos_slab, sin_slab, lambda_init):
    B, T, D = h.shape
    cols, plan = _odd_layout()
    w = _gather_cols(w_in.astype(BF16), cols)
    sl = _project(h, norm_g, w, cos_slab, sin_slab, plan)
    o = _diff_attention(sl, lam, subln_g, lambda_init)
    return [o.reshape(B * T, -1)], [w_out.astype(BF16)]


def kernel(x, mix_norm_g, mlp_norm_g, even_w_in, even_cmp_pe, even_cmp_w1, even_cmp_w2, even_w_out, odd_w_in, odd_lambda, odd_subln_g, odd_w_out, mlp_w_up, mlp_w_down, final_norm_g):
    B, T, D = x.shape
    depth = mix_norm_g.shape[0]
    assert depth >= 1
    cos_slab, sin_slab = _rope_slabs(T)
    h = x
    for layer in range(depth):
        if layer % 2 == 0:
            e = layer // 2
            outs, ws = _even_mixer(h, mix_norm_g[layer], even_w_in[e], even_cmp_pe[e],
                                   even_cmp_w1[e], even_cmp_w2[e], even_w_out[e], cos_slab, sin_slab)
        else:
            o = layer // 2
            lambda_init = 0.8 - 0.6 * math.exp(-0.3 * layer)
            outs, ws = _odd_mixer(h, mix_norm_g[layer], odd_w_in[o], odd_lambda[o], odd_subln_g[o],
                                  odd_w_out[o], cos_slab, sin_slab, lambda_init)
        h2 = _layer_tail(h.reshape(B * T, D), outs, ws, mlp_norm_g[layer],
                         mlp_w_up[layer].astype(BF16), mlp_w_down[layer].astype(BF16),
                         final_g=final_norm_g if layer == depth - 1 else None)
        h = h2.reshape(B, T, D)
    return h
```

```python
import functools
import math

import numpy as np
import jax
import jax.numpy as jnp
from jax import lax
from jax.experimental import pallas as pl
from jax.experimental.pallas import tpu as pltpu

HEAD_DIM = 64
HALF = HEAD_DIM // 2
LANES = 128
ROPE_THETA = 10000.0
NORM_EPS = 1e-6
SCALE = HEAD_DIM ** -0.5

A_HEADS = 8
IDX_HEADS = 4
DSA_TOPK = 256
B_HEADS = 8
B_KV_GROUPS = 2
B_PER_GROUP = B_HEADS // B_KV_GROUPS
CMP_BLOCK = 32
CMP_STRIDE = 16
CMP_HIDDEN = 256
SLC_BLOCK = 64
SLC_TOPN = 16
WINDOW = 512
C_HEADS = 8

KV_TILE = 512
SEL_Q_TILE = 128
SEL_TILES_PER_STEP = 2
DSA_Q_TILE = 256
NSA_Q_TILE = 256
VW_TILE = 128

LOG2E = math.log2(math.e)
Q_SCALE = SCALE * LOG2E
ONES_ROWS = 16
BIG = 2.0 ** 100

NEG_INF = float("-inf")
M_FLOOR = -1e30
INT_MIN = -(2 ** 31)

VMEM_LIMIT = 56 * 1024 * 1024

BF16 = jnp.bfloat16
F32 = jnp.float32


def _dot(a, b):
    return jnp.dot(a, b, preferred_element_type=F32)


def _params(sem):
    return pltpu.CompilerParams(dimension_semantics=sem, vmem_limit_bytes=VMEM_LIMIT)


def _iota(shape, axis):
    return lax.broadcasted_iota(jnp.int32, shape, axis)


def _pair_cols(base_a, base_b):
    lane = np.arange(LANES)
    half = lane // 64
    which = (lane % 64) // HALF
    i = lane % HALF
    base = np.where(which == 0, base_a, base_b)
    return base + half * HALF + i


def _head_cols(base):
    lane = np.arange(LANES)
    return np.where(lane < HEAD_DIM, base + lane, -1)


_PAIR_D = _pair_cols(0, 0)
_PAIR_WHICH = (np.arange(LANES) % 64) // HALF


def _gather_cols(w, cols):
    cols = np.asarray(cols)
    safe = np.where(cols >= 0, cols, 0)
    g = jnp.take(w, jnp.asarray(safe, dtype=jnp.int32), axis=1)
    return jnp.where(jnp.asarray(cols >= 0)[None, :], g, jnp.zeros_like(g))


def _rope_slabs(T):
    inv = 1.0 / (ROPE_THETA ** (jnp.arange(0, HEAD_DIM, 2, dtype=F32) / HEAD_DIM))
    ang = jnp.arange(T, dtype=F32)[:, None] * inv[None, :]
    cos, sin = jnp.cos(ang), jnp.sin(ang)
    cos_slab = jnp.tile(cos, (1, 4))
    sin_slab = jnp.concatenate([-sin, -sin, sin, sin], axis=1)
    return cos_slab, sin_slab


_KINDS = ("qT", "k", "vT", "vTw", "f", "fT")


def _proj_kernel(x_ref, g_ref, w_ref, cos_ref, sin_ref, *out_refs, plan, kinds, chunk):
    outs = dict(zip(kinds, out_refs))
    x = x_ref[0]
    ms = jnp.mean(x * x, axis=-1, keepdims=True)
    xn = ((x * lax.rsqrt(ms + NORM_EPS)) * g_ref[...]).astype(BF16)
    cos = cos_ref[...]
    sin = sin_ref[...]
    n = len(plan)
    for c0 in range(0, n, chunk):
        c1 = min(c0 + chunk, n)
        r = _dot(xn, w_ref[:, c0 * LANES:c1 * LANES])
        for s in range(c0, c1):
            roped, kind, idx, opt = plan[s]
            y = r[:, (s - c0) * LANES:(s - c0 + 1) * LANES]
            if roped:
                y = y * cos + pltpu.roll(y, 64, 1) * sin
            if kind == "qT":
                if opt is not None:
                    y = y * opt
                outs[kind][0, idx] = y.T.astype(BF16)
            elif kind == "k":
                outs[kind][0, idx] = y.astype(BF16)
            elif kind == "vT":
                if opt == "ones_row_64":
                    y = jnp.where(_iota(y.shape, 1) == HEAD_DIM, 1.0, y)
                    outs[kind][0, idx, 0] = y.T.astype(BF16)
                else:
                    outs[kind][0, idx, 0, 0:LANES, :] = y.T.astype(BF16)
                    outs[kind][0, idx, 0, LANES:, :] = jnp.ones((ONES_ROWS, y.shape[0]), BF16)
            elif kind == "vTw":
                y_t = jnp.where(_iota(y.shape, 1) == HEAD_DIM, 1.0, y).T.astype(BF16)
                for sub in range(y.shape[0] // VW_TILE):
                    outs[kind][0, idx, sub] = y_t[:, sub * VW_TILE:(sub + 1) * VW_TILE]
            elif kind == "f":
                outs[kind][0, idx] = y
            else:
                outs[kind][0, idx] = y.T


def _project(x, g, w, cos_slab, sin_slab, plan, chunk=4):
    B, T, D = x.shape
    tm = KV_TILE
    assert T % tm == 0
    count = {k: sum(1 for p in plan if p[1] == k) for k in _KINDS}
    kinds = tuple(k for k in _KINDS if count[k])
    out_specs, out_shape = [], []
    for k in kinds:
        n = count[k]
        if k in ("qT", "fT"):
            out_specs.append(pl.BlockSpec((1, n, LANES, tm), lambda b, i: (b, 0, 0, i)))
            out_shape.append(jax.ShapeDtypeStruct((B, n, LANES, T), BF16 if k == "qT" else F32))
        elif k in ("k", "f"):
            out_specs.append(pl.BlockSpec((1, n, tm, LANES), lambda b, i: (b, 0, i, 0)))
            out_shape.append(jax.ShapeDtypeStruct((B, n, T, LANES), BF16 if k == "k" else F32))
        elif k == "vTw":
            sub = tm // VW_TILE
            out_specs.append(pl.BlockSpec((1, n, sub, LANES, VW_TILE), lambda b, i: (b, 0, i, 0, 0)))
            out_shape.append(jax.ShapeDtypeStruct((B, n, T // VW_TILE, LANES, VW_TILE), BF16))
        else:
            wide = any(p[1] == "vT" and p[3] != "ones_row_64" for p in plan)
            rows = LANES + ONES_ROWS if wide else LANES
            out_specs.append(pl.BlockSpec((1, n, 1, rows, tm), lambda b, i: (b, 0, i, 0, 0)))
            out_shape.append(jax.ShapeDtypeStruct((B, n, T // tm, rows, tm), BF16))
    kern = functools.partial(_proj_kernel, plan=tuple(plan), kinds=kinds, chunk=chunk)
    outs = pl.pallas_call(
        kern,
        grid=(B, T // tm),
        in_specs=[
            pl.BlockSpec((1, tm, D), lambda b, i: (b, i, 0)),
            pl.BlockSpec((1, D), lambda b, i: (0, 0)),
            pl.BlockSpec((D, len(plan) * LANES), lambda b, i: (0, 0)),
            pl.BlockSpec((tm, LANES), lambda b, i: (i, 0)),
            pl.BlockSpec((tm, LANES), lambda b, i: (i, 0)),
        ],
        out_specs=out_specs,
        out_shape=out_shape,
        compiler_params=_params(("parallel", "parallel")),
        name="norm_proj_rope",
    )(x, g.reshape(1, D), w, cos_slab, sin_slab)
    return dict(zip(kinds, outs))


def _head_rows(slab_t, which):
    row = _iota(slab_t.shape, 0)
    keep = ((row % 64) // HALF) == which
    return jnp.where(keep, slab_t, jnp.zeros_like(slab_t))


def _tree(op, xs):
    while len(xs) > 1:
        xs = [op(xs[i], xs[i + 1]) if i + 1 < len(xs) else xs[i] for i in range(0, len(xs), 2)]
    return xs[0]


def _fold_rows_max(x, ways=4):
    rows, n = x.shape
    per = rows // ways
    parts = [jnp.max(x[i * per:(i + 1) * per].reshape(per // 8, 8, n), axis=0) for i in range(ways)]
    return _tree(jnp.maximum, parts)


ROW_BLOCK = 64


def _online_update(s_sc, p_sc, v_aug, m_ref, acc_ref):
    tk, n = s_sc.shape
    m_prev = m_ref[...]
    m_tile = jnp.max(_fold_rows_max(s_sc[...]), axis=0, keepdims=True)
    m_new = jnp.maximum(m_prev, m_tile)
    alpha = jnp.exp2(m_prev - m_new)
    for r in range(tk // ROW_BLOCK):
        rows = slice(r * ROW_BLOCK, (r + 1) * ROW_BLOCK)
        p_sc[rows, :] = jnp.exp2(s_sc[rows, :] - m_new).astype(BF16)
    acc_ref[...] = alpha * acc_ref[...] + _dot(v_aug, p_sc[...])
    m_ref[...] = m_new


def _flash_pipeline(n, put_scores, update, buf_a, buf_b, put_last=None, update_last=None):
    if put_last is None:
        put_scores(buf_a, 0)
    else:
        @pl.when(n > 0)
        def _():
            put_scores(buf_a, 0)

    def pair(p, carry):
        j = 2 * p
        put_scores(buf_b, j + 1)
        update(buf_a, j)

        @pl.when(j + 2 < n)
        def _():
            put_scores(buf_a, j + 2)
            update(buf_b, j + 1)

        return carry

    lax.fori_loop(0, n // 2, pair, 0)
    in_b = jnp.logical_and(n > 0, n % 2 == 0)
    in_a = n % 2 == 1

    update_last = update_last or update

    def finish(cur, other):
        if put_last is not None:
            put_last(other, n)
        update(cur, n - 1)
        if put_last is not None:
            update_last(other, n)

    pl.when(in_b)(lambda: finish(buf_b, buf_a))
    pl.when(in_a)(lambda: finish(buf_a, buf_b))
    if put_last is not None:
        @pl.when(n == 0)
        def _():
            put_last(buf_a, 0)
            update_last(buf_a, 0)


def _softmax_keys(s):
    m = jnp.max(s, axis=0, keepdims=True)
    m = jnp.where(m > NEG_INF, m, 0.0)
    e = jnp.exp2(s - m)
    return e / jnp.maximum(jnp.sum(e, axis=0, keepdims=True), 1e-30)


def _init_flash(m_ref, acc_ref):
    m_ref[...] = jnp.full(m_ref.shape, M_FLOOR, F32)
    acc_ref[...] = jnp.zeros(acc_ref.shape, F32)


DIFF_HEADS_PER_STEP = 2


def _diff_attn_kernel(lam_ref, q_ref, k_ref, v_ref, g_ref, o_ref, m_sc, acc_sc, sa_sc, sb_sc,
                      p_sc, *, tq, lambda_init):
    qi = pl.program_id(2)
    HS = DIFF_HEADS_PER_STEP
    qs = [[_head_rows(q_ref[0, hh], c) for c in range(2)] for hh in range(HS)]
    for s in range(2 * HS):
        _init_flash(m_sc.at[s], acc_sc.at[s])

    def put_scores(buf, j):
        for hh in range(HS):
            k = k_ref[0, hh, pl.ds(pl.multiple_of(j * tq, tq), tq), :]
            for c in range(2):
                buf[2 * hh + c] = _dot(k, qs[hh][c])

    half = tq // 2
    pieces = ((slice(0, half), half), (slice(half, tq), tq))

    def put_diagonal(buf, j):
        for hh in range(HS):
            start = pl.multiple_of(j * tq, tq)
            for c in range(2):
                for lanes, rows in pieces:
                    k = k_ref[0, hh, pl.ds(start, rows), :]
                    s = _dot(k, qs[hh][c][:, lanes])
                    keep = _iota(s.shape, 0) <= _iota(s.shape, 1) + lanes.start
                    buf[2 * hh + c, 0:rows, lanes] = jnp.where(keep, s, NEG_INF)

    def update(buf, j):
        for hh in range(HS):
            v_aug = v_ref[0, hh, j]
            for c in range(2):
                s = 2 * hh + c
                _online_update(buf.at[s], p_sc.at[s], v_aug, m_sc.at[s], acc_sc.at[s])

    def update_diagonal(buf, j):
        for hh in range(HS):
            v_aug = v_ref[0, hh, j]
            for c in range(2):
                s = 2 * hh + c
                for lanes, rows in pieces:
                    _online_update(buf.at[s, 0:rows, lanes], p_sc.at[s, 0:rows, lanes],
                                   v_aug[:, 0:rows], m_sc.at[s, :, lanes], acc_sc.at[s, :, lanes])

    _flash_pipeline(qi, put_scores, update, sa_sc, sb_sc, put_last=put_diagonal,
                    update_last=update_diagonal)

    lam = lam_ref[...]
    s01 = jnp.sum(lam[0:1] * lam[1:2], axis=-1, keepdims=True)
    s23 = jnp.sum(lam[2:3] * lam[3:4], axis=-1, keepdims=True)
    lam_val = jnp.exp(s01) - jnp.exp(s23) + lambda_init
    for hh in range(HS):
        a0, a1 = acc_sc[2 * hh], acc_sc[2 * hh + 1]
        o0 = a0[0:LANES] / jnp.maximum(a0[LANES:LANES + 1], 1e-30)
        o1 = a1[0:LANES] / jnp.maximum(a1[LANES:LANES + 1], 1e-30)
        o = o0 - lam_val * o1
        y = o * lax.rsqrt(jnp.mean(o * o, axis=0, keepdims=True) + NORM_EPS)
        y = (y * g_ref[...]) * (1.0 - lambda_init)
        o_ref[0, :, hh * LANES:(hh + 1) * LANES] = y.T.astype(o_ref.dtype)


def _diff_attention(sl, lam, subln_g, lambda_init):
    q_t, k, v_t = sl["qT"], sl["k"], sl["vT"]
    B, H, T, _ = k.shape
    tq = KV_TILE
    nk = T // tq
    kern = functools.partial(_diff_attn_kernel, tq=tq, lambda_init=lambda_init)
    HS = DIFF_HEADS_PER_STEP
    ns = 2 * HS
    return pl.pallas_call(
        kern,
        grid=(B, H // HS, T // tq),
        in_specs=[
            pl.BlockSpec((4, HEAD_DIM), lambda b, h, i: (0, 0)),
            pl.BlockSpec((1, HS, LANES, tq), lambda b, h, i: (b, h, 0, i)),
            pl.BlockSpec((1, HS, T, LANES), lambda b, h, i: (b, h, 0, 0)),
            pl.BlockSpec((1, HS, nk, LANES + ONES_ROWS, tq), lambda b, h, i: (b, h, 0, 0, 0)),
            pl.BlockSpec((LANES, 1), lambda b, h, i: (0, 0)),
        ],
        out_specs=pl.BlockSpec((1, tq, HS * LANES), lambda b, h, i: (b, i, h)),
        out_shape=jax.ShapeDtypeStruct((B, T, H * LANES), BF16),
        scratch_shapes=[
            pltpu.VMEM((ns, 1, tq), F32),
            pltpu.VMEM((ns, LANES + ONES_ROWS, tq), F32),
            pltpu.VMEM((ns, tq, tq), F32),
            pltpu.VMEM((ns, tq, tq), F32),
            pltpu.VMEM((ns, tq, tq), BF16),
        ],
        compiler_params=_params(("parallel", "parallel", "arbitrary")),
        name="diff_attention",
    )(lam, q_t, k, v_t, subln_g.reshape(LANES, 1))


def _dsa_select_kernel(qi_ref, ki_ref, misc_ref, tril_ref, bias_ref, key_sc, *, tq, ck, nk, topk):
    S = key_sc.shape[0]
    t0 = pl.program_id(1) * (S * tq)
    nvalid = (t0 + S * tq - 1) // ck + 1
    lanes = [slice(s * tq, (s + 1) * tq) for s in range(S)]
    w = [misc_ref[0, 0, :, lanes[s]] for s in range(S)]
    qh = [jnp.concatenate([_head_rows(qi_ref[0, h // 2, :, lanes[s]], h % 2)
                           for h in range(IDX_HEADS)], axis=1) for s in range(S)]
    t_q = [t0 + s * tq + _iota((ck, tq), 1) for s in range(S)]

    def causal(c, s):
        return (c * ck + _iota((ck, tq), 0)) <= t_q[s]

    def fill(c, s):
        kk = ki_ref[0, 0, c * ck:(c + 1) * ck, :]
        r = _dot(kk, qh[s])
        score = jnp.zeros((ck, tq), F32)
        for h in range(IDX_HEADS):
            score = score + w[s][h:h + 1, :] * jnp.maximum(r[:, h * tq:(h + 1) * tq], 0.0)
        bits = pltpu.bitcast(score, jnp.int32)
        key = bits ^ ((bits >> 31) & jnp.int32(0x7FFFFFFF))
        key = jnp.where(score == 0.0, 0, key)
        key_sc[s, c] = jnp.where(causal(c, s), key, INT_MIN)

    kf = float(topk)

    def select_threshold(n):
        for c in range(n):
            for s in range(S):
                fill(c, s)

        def count(pred, s):
            acc = jnp.zeros((ck // 8, tq), F32)
            for c in range(n):
                ind = pred(key_sc[s, c], c)
                acc = acc + jnp.sum(ind.reshape(8, ck // 8, tq), axis=0)
            return jnp.sum(acc, axis=0, keepdims=True)

        zero = jnp.zeros((1, tq), jnp.int32)
        ans = tuple(jnp.where(count(lambda kc, c: jnp.where(kc >= zero, 1.0, 0.0), s) >= kf, 0, INT_MIN)
                    for s in range(S))

        def bit_step(i, ans):
            out = []
            for s in range(S):
                cand = ans[s] | (jnp.int32(1) << (30 - i))
                cnt = count(lambda kc, c: jnp.where(kc >= cand, 1.0, 0.0), s)
                out.append(jnp.where(cnt >= kf, cand, ans[s]))
            return tuple(out)

        ans = lax.fori_loop(0, 31, bit_step, ans)
        cnt_gt = tuple(count(lambda kc, c: jnp.where(kc > ans[s], 1.0, 0.0), s) for s in range(S))
        cnt_eq = tuple(count(lambda kc, c: jnp.where(kc == ans[s],
                                                     jnp.where(causal(c, s), 1.0, 0.0), 0.0), s)
                       for s in range(S))
        return ans, cnt_gt, cnt_eq

    ans, cnt_gt, cnt_eq = lax.switch(
        nvalid - 1, [functools.partial(select_threshold, n) for n in range(1, nk + 1)])
    need = [kf - cnt_gt[s] for s in range(S)]
    has_tie = _tree(jnp.maximum, [jnp.max(cnt_eq[s] - need[s]) for s in range(S)]) > 0.0

    @pl.when(jnp.logical_not(has_tie))
    def _():
        def emit(c, carry):
            for s in range(S):
                sel = jnp.where(key_sc[s, c] >= ans[s], jnp.where(causal(c, s), 0.0, NEG_INF), NEG_INF)
                bias_ref[0, s, c] = sel.astype(BF16)
            return carry
        lax.fori_loop(0, nvalid, emit, 0)

    @pl.when(has_tie)
    def _():
        def emit(c, carry):
            out = []
            for s in range(S):
                kc = key_sc[s, c]
                eq = jnp.where(kc == ans[s], jnp.where(causal(c, s), 1.0, 0.0), 0.0)
                pre = _dot(tril_ref[...], eq.astype(BF16)) + carry[s]
                take = jnp.where(kc > ans[s], 1.0, jnp.where(pre < need[s], eq, 0.0))
                sel = jnp.where(take > 0.5, jnp.where(causal(c, s), 0.0, NEG_INF), NEG_INF)
                bias_ref[0, s, c] = sel.astype(BF16)
                out.append(carry[s] + jnp.sum(eq, axis=0, keepdims=True))
            return tuple(out)
        lax.fori_loop(0, nvalid, emit, tuple(jnp.zeros((1, tq), F32) for _ in range(S)))

    def blank(c, carry):
        for s in range(S):
            bias_ref[0, s, c] = jnp.full((ck, tq), NEG_INF, BF16)
        return carry

    lax.fori_loop(nvalid, nk, blank, 0)


def _dsa_select(sl, lay):
    q_t, k, misc_t = sl["qT"], sl["k"], sl["fT"]
    B, _, T, _ = k.shape
    tq, ck = SEL_Q_TILE, KV_TILE
    nk = T // ck
    topk = min(DSA_TOPK, T // 4)
    tril = jnp.asarray(np.tril(np.ones((ck, ck), np.float32), -1), BF16)
    kern = functools.partial(_dsa_select_kernel, tq=tq, ck=ck, nk=nk, topk=topk)
    S = SEL_TILES_PER_STEP
    assert ck % (S * tq) == 0 and T % (S * tq) == 0
    return pl.pallas_call(
        kern,
        grid=(B, T // (S * tq)),
        in_specs=[
            pl.BlockSpec((1, 2, LANES, S * tq), lambda b, i: (b, lay["qi"] // 2, 0, i)),
            pl.BlockSpec((1, 1, T, LANES), lambda b, i: (b, lay["ki"], 0, 0)),
            pl.BlockSpec((1, 1, LANES, S * tq), lambda b, i: (b, lay["misc"], 0, i)),
            pl.BlockSpec((ck, ck), lambda b, i: (0, 0)),
        ],
        out_specs=pl.BlockSpec((1, S, nk, ck, tq), lambda b, i: (b, i, 0, 0, 0)),
        out_shape=jax.ShapeDtypeStruct((B, T // tq, nk, ck, tq), BF16),
        scratch_shapes=[pltpu.VMEM((S, nk, ck, tq), jnp.int32)],
        compiler_params=_params(("parallel", "parallel")),
        name="dsa_select",
    )(q_t, k, misc_t, tril)


def _store_head_pairs(o_ref, o, n_heads, tq, col0=0, row0=0):
    for p in range(n_heads // 2):
        even = o[0:HEAD_DIM, (2 * p) * tq:(2 * p + 1) * tq]
        odd = o[0:HEAD_DIM, (2 * p + 1) * tq:(2 * p + 2) * tq]
        pair = jnp.concatenate([even, odd], axis=0)
        cols = slice(col0 + p * LANES, col0 + (p + 1) * LANES)
        o_ref[0, row0:row0 + tq, cols] = pair.T.astype(o_ref.dtype)


def _normalized(acc):
    return acc[0:HEAD_DIM] / jnp.maximum(acc[HEAD_DIM:HEAD_DIM + 1], 1e-30)


def _dsa_attn_kernel(q_ref, k_ref, v_ref, bias_ref, o_ref, qst, m_sc, acc_sc, sa_sc, sb_sc,
                     p_sc, *, tq, tk):
    qi = pl.program_id(1)
    H = A_HEADS
    S = sa_sc.shape[0]
    nsub = bias_ref.shape[1] // S
    for s in range(S):
        for h in range(H):
            qst[s, :, h * tq:(h + 1) * tq] = _head_rows(q_ref[0, h // 2, :, s * tq:(s + 1) * tq], h % 2)
        _init_flash(m_sc.at[s], acc_sc.at[s])
    nkv = (qi * S * tq) // tk + 1

    def put_scores(buf, j):
        k = k_ref[0, 0, pl.ds(pl.multiple_of(j * tk, tk), tk), :]
        for s in range(S):
            parts = [bias_ref[0, s * nsub + u, j].astype(F32) for u in range(nsub)]
            b = parts[0] if len(parts) == 1 else jnp.concatenate(parts, axis=1)
            buf[s] = _dot(k, qst[s]) + jnp.concatenate([b] * H, axis=1)

    def update(buf, j):
        for s in range(S):
            _online_update(buf.at[s], p_sc.at[s], v_ref[0, 0, j], m_sc.at[s], acc_sc.at[s])

    _flash_pipeline(nkv, put_scores, update, sa_sc, sb_sc)
    for s in range(S):
        _store_head_pairs(o_ref, _normalized(acc_sc[s]), H, tq, row0=s * tq)


def _dsa_attention(sl, bias, lay):
    q_t, k, v_t = sl["qT"], sl["k"], sl["vT"]
    B, _, T, _ = k.shape
    tq, tk = DSA_Q_TILE, KV_TILE
    nk = T // tk
    H = A_HEADS
    N = H * tq
    sel_tq = bias.shape[4]
    S = tk // tq
    nsub = S * tq // sel_tq
    kern = functools.partial(_dsa_attn_kernel, tq=tq, tk=tk)
    return pl.pallas_call(
        kern,
        grid=(B, T // (S * tq)),
        in_specs=[
            pl.BlockSpec((1, H // 2, LANES, S * tq), lambda b, i: (b, lay["qA"] // (H // 2), 0, i)),
            pl.BlockSpec((1, 1, T, LANES), lambda b, i: (b, lay["kA"], 0, 0)),
            pl.BlockSpec((1, 1, nk, LANES, tk), lambda b, i: (b, lay["vA"], 0, 0, 0)),
            pl.BlockSpec((1, nsub, nk, tk, sel_tq), lambda b, i: (b, i, 0, 0, 0)),
        ],
        out_specs=pl.BlockSpec((1, S * tq, H * HEAD_DIM), lambda b, i: (b, i, 0)),
        out_shape=jax.ShapeDtypeStruct((B, T, H * HEAD_DIM), BF16),
        scratch_shapes=[
            pltpu.VMEM((S, LANES, N), BF16),
            pltpu.VMEM((S, 1, N), F32),
            pltpu.VMEM((S, LANES, N), F32),
            pltpu.VMEM((S, tk, N), F32),
            pltpu.VMEM((S, tk, N), F32),
            pltpu.VMEM((S, tk, N), BF16),
        ],
        compiler_params=_params(("parallel", "arbitrary")),
        name="dsa_attention",
    )(q_t, k, v_t, bias)


def _compress_kernel(x_ref, pe_ref, w1_ref, w2_ref, o_ref, ot_ref, *, n_ch):
    x = x_ref[0, 0]
    xt = (x + pe_ref[0, 0]).astype(BF16)
    xb = (x + pe_ref[0, 1]).astype(BF16)
    for g in range(B_KV_GROUPS):
        a = _dot(xt, w1_ref[0, g, 0])
        b = _dot(xb, w1_ref[0, g, 1])
        h = jax.nn.gelu(a + pltpu.roll(b, n_ch - 1, 0))
        r = _dot(h.astype(BF16), w2_ref[0])
        o_ref[0, 0, g] = r.astype(BF16)
        ot_ref[0, 0, g] = r.T.astype(BF16)


def _compress(flat, pe, w1, w2):
    B, _, n_ch, W = flat.shape
    G = B_KV_GROUPS
    kern = functools.partial(_compress_kernel, n_ch=n_ch)
    return pl.pallas_call(
        kern,
        grid=(2, B),
        in_specs=[
            pl.BlockSpec((1, 1, n_ch, W), lambda s, b: (b, s, 0, 0)),
            pl.BlockSpec((1, 2, 1, W), lambda s, b: (s, 0, 0, 0)),
            pl.BlockSpec((1, G, 2, W, CMP_HIDDEN), lambda s, b: (s, 0, 0, 0, 0)),
            pl.BlockSpec((1, CMP_HIDDEN, LANES), lambda s, b: (s, 0, 0)),
        ],
        out_specs=[
            pl.BlockSpec((1, 1, G, n_ch, LANES), lambda s, b: (s, b, 0, 0, 0)),
            pl.BlockSpec((1, 1, G, LANES, n_ch), lambda s, b: (s, b, 0, 0, 0)),
        ],
        out_shape=[
            jax.ShapeDtypeStruct((2, B, G, n_ch, LANES), BF16),
            jax.ShapeDtypeStruct((2, B, G, LANES, n_ch), BF16),
        ],
        compiler_params=_params(("parallel", "parallel")),
        name="nsa_compress",
    )(flat, pe, w1, w2)


def _nsa_kernel(q_ref, kc_ref, vc_ref, ks_ref, vs_ref, kw_ref, vw_ref, misc_ref, ov_ref, ex_ref,
                cz_ref, wz_ref, o_ref, qst, m_sc, acc_sc, sa_sc, sb_sc, p_sc, mw_sc, accw_sc,
                sw_sc, pw_sc, oc_sc, *, tq, tk, T, n_s, n_sel, gate_row0):
    qi = pl.program_id(1)
    t0 = qi * tq
    G, J = B_KV_GROUPS, B_PER_GROUP
    N = J * tq
    for g in range(G):
        for j in range(J):
            qst[g, 0:LANES, j * tq:(j + 1) * tq] = _head_rows(q_ref[0, 2 * g + j // 2], j % 2)

    def q_time(shape):
        return t0 + (_iota(shape, 1) % tq)

    wlen = min(WINDOW + tq, T)
    wstart = pl.multiple_of(jnp.maximum(t0 - WINDOW, 0), tq)
    n_wt = wlen // tq
    wt = wstart // VW_TILE
    wz = wz_ref[jnp.minimum(qi, n_wt - 1)].astype(F32)
    wz = jnp.concatenate([wz] * J, axis=1)
    for g in range(G):
        kw = kw_ref[0, g, pl.ds(wstart, wlen), :]
        vw = jnp.concatenate([vw_ref[0, g, wt + i] for i in range(wlen // VW_TILE)], axis=1)
        sw_sc[g] = _dot(kw, qst[g, 0:LANES, :]) + wz
        _init_flash(mw_sc.at[g], accw_sc.at[g])
        _online_update(sw_sc.at[g], pw_sc.at[g], vw, mw_sc.at[g], accw_sc.at[g])

    rows = -(-n_s // 8) * 8
    blk = _iota((rows, tq), 0)
    t_q = t0 + _iota((rows, tq), 1)
    cur = t_q // SLC_BLOCK
    forced = (blk == 0) | (blk == cur) | (blk == cur - 1)
    admissible = blk * SLC_BLOCK <= t_q
    blk8 = _iota((8, tq), 0)
    for g in range(G):
        q = qst[g, 0:LANES, :]
        kc = kc_ref[0, 0, g]
        n_ch = kc.shape[0]
        s_c = _dot(kc, q)
        cmp_end = _iota((n_ch, N), 0) * CMP_STRIDE + (CMP_BLOCK - 1)
        s_c = jnp.where(cmp_end <= q_time((n_ch, N)), s_c, NEG_INF)
        p_c = _softmax_keys(s_c)
        oc_sc[g] = _dot(vc_ref[0, 0, g], p_c.astype(BF16))[0:HEAD_DIM]
        psum = p_c[:, 0:tq]
        for j in range(1, J):
            psum = psum + p_c[:, j * tq:(j + 1) * tq]
        p_hi = psum.astype(BF16)
        p_lo = (psum - p_hi.astype(F32)).astype(BF16)
        imp = _dot(ov_ref[...], p_hi) + _dot(ov_ref[...], p_lo)
        sc = jnp.where(forced, jnp.inf, imp[0:rows])
        sc = jnp.where(admissible, sc, NEG_INF)
        groups = [sc[8 * r:8 * r + 8] for r in range(rows // 8)]
        ranks = [jnp.zeros((8, tq), F32) for _ in groups]
        for m in range(n_s):
            cm = sc[m:m + 1, :]
            for r, grp in enumerate(groups):
                gt = jnp.where(cm > grp, 1.0, 0.0)
                ge = jnp.where(cm >= grp, 1.0, 0.0)
                if 8 * r + 7 <= m:
                    first = gt
                elif 8 * r > m:
                    first = ge
                else:
                    first = jnp.where(blk8 + 8 * r > m, ge, gt)
                ranks[r] = ranks[r] + first
        rank = jnp.concatenate(ranks, axis=0)
        drop = jnp.where(rank < float(n_sel), 0.0, 1.0)
        if rows < LANES:
            drop = jnp.concatenate([drop, jnp.ones((LANES - rows, tq), F32)], axis=0)
        drop = drop.astype(BF16)
        qst[g, LANES:2 * LANES, :] = jnp.concatenate([drop] * J, axis=1)

    jd = t0 // tk
    off = (t0 - jd * tk) // tq
    n_off = tk // tq
    for g in range(G):
        _init_flash(m_sc.at[g], acc_sc.at[g])

    def put_scores(buf, j):
        cz = cz_ref[jnp.where(j == jd, off, n_off)].astype(F32)
        cz = jnp.concatenate([cz] * J, axis=1)
        for g in range(G):
            k = ks_ref[0, g, pl.ds(pl.multiple_of(j * tk, tk), tk), :]
            lhs = jnp.concatenate([k, ex_ref[j]], axis=1)
            buf[g] = _dot(lhs, qst[g]) + cz

    def update(buf, j):
        for g in range(G):
            _online_update(buf.at[g], p_sc.at[g], vs_ref[0, g, j], m_sc.at[g], acc_sc.at[g])

    _flash_pipeline(jd + 1, put_scores, update, sa_sc, sb_sc)

    gates = jax.nn.sigmoid(misc_ref[0, 0])
    for g in range(G):
        def gate_row(c, g=g):
            rws = [gate_row0 + (g * J + j) * 3 + c for j in range(J)]
            return jnp.concatenate([gates[r:r + 1, :] for r in rws], axis=1)

        o_s = _normalized(acc_sc[g])
        o_w = _normalized(accw_sc[g])
        o = gate_row(0) * oc_sc[g] + (gate_row(1) * o_s + gate_row(2) * o_w)
        _store_head_pairs(o_ref, o, J, tq, col0=g * J * HEAD_DIM)


def _nsa_attention(sl, cmp_k, cmp_vt, lay):
    q_t, k, v_t, misc_t = sl["qT"], sl["k"], sl["vT"], sl["fT"]
    B, _, T, _ = k.shape
    G, J = B_KV_GROUPS, B_PER_GROUP
    tq, tk = NSA_Q_TILE, KV_TILE
    nk = T // tk
    n_ch = T // CMP_STRIDE
    n_c = n_ch - CMP_BLOCK // CMP_STRIDE + 1
    n_s = T // SLC_BLOCK
    n_sel = min(SLC_TOPN, n_s)
    assert n_s <= LANES
    c0 = np.arange(n_ch) * CMP_STRIDE
    s0 = np.arange(LANES) * SLC_BLOCK
    ov = ((c0[None, :] < s0[:, None] + SLC_BLOCK) & (c0[None, :] + CMP_BLOCK > s0[:, None]))
    ov = ov & (np.arange(n_ch)[None, :] < n_c) & (np.arange(LANES)[:, None] < n_s)
    ov = jnp.asarray(ov.astype(np.float32), BF16)
    pos = np.arange(T).reshape(nk, tk, 1)
    ex = (pos // SLC_BLOCK == np.arange(LANES).reshape(1, 1, LANES))
    ex = jnp.asarray(ex.astype(np.float32) * -BIG, BF16)
    n_off = tk // tq
    kp = np.arange(tk).reshape(1, tk, 1)
    tl = np.arange(tq).reshape(1, 1, tq) + np.arange(n_off + 1).reshape(n_off + 1, 1, 1) * tq
    cz = np.where((kp <= tl) | (np.arange(n_off + 1).reshape(-1, 1, 1) == n_off), 0.0, -BIG)
    cz = jnp.asarray(cz.astype(np.float32), BF16)
    wlen = min(WINDOW + tq, T)
    n_wt = wlen // tq
    assert wlen % tq == 0 and T >= wlen
    kp = np.arange(wlen).reshape(1, wlen, 1)
    tl = np.arange(tq).reshape(1, 1, tq)
    early = kp <= tl + np.arange(n_wt).reshape(n_wt, 1, 1) * tq
    late = (kp > tl) & (kp <= tl + WINDOW)
    band = np.where(np.arange(n_wt).reshape(n_wt, 1, 1) == n_wt - 1, late, early)
    wz = jnp.asarray(np.where(band, 0.0, -BIG).astype(np.float32), BF16)
    kern = functools.partial(_nsa_kernel, tq=tq, tk=tk, T=T, n_s=n_s, n_sel=n_sel,
                             gate_row0=lay["gate_row0"])
    N = J * tq
    for name in ("qB", "ks", "vs", "kw", "vw"):
        assert lay[name] % (2 * G if name == "qB" else G) == 0
    kslab = lambda off: pl.BlockSpec((1, G, T, LANES), lambda b, i: (b, off // G, 0, 0))
    return pl.pallas_call(
        kern,
        grid=(B, T // tq),
        in_specs=[
            pl.BlockSpec((1, 2 * G, LANES, tq), lambda b, i: (b, lay["qB"] // (2 * G), 0, i)),
            pl.BlockSpec((1, 1, G, n_ch, LANES), lambda b, i: (0, b, 0, 0, 0)),
            pl.BlockSpec((1, 1, G, LANES, n_ch), lambda b, i: (1, b, 0, 0, 0)),
            kslab(lay["ks"]),
            pl.BlockSpec((1, G, nk, LANES, tk), lambda b, i: (b, lay["vs"] // G, 0, 0, 0)),
            kslab(lay["kw"]),
            pl.BlockSpec((1, G, T // VW_TILE, LANES, VW_TILE),
                         lambda b, i: (b, lay["vw"] // G, 0, 0, 0)),
            pl.BlockSpec((1, 1, LANES, tq), lambda b, i: (b, lay["misc"], 0, i)),
            pl.BlockSpec((LANES, n_ch), lambda b, i: (0, 0)),
            pl.BlockSpec((nk, tk, LANES), lambda b, i: (0, 0, 0)),
            pl.BlockSpec((n_off + 1, tk, tq), lambda b, i: (0, 0, 0)),
            pl.BlockSpec((n_wt, wlen, tq), lambda b, i: (0, 0, 0)),
        ],
        out_specs=pl.BlockSpec((1, tq, B_HEADS * HEAD_DIM), lambda b, i: (b, i, 0)),
        out_shape=jax.ShapeDtypeStruct((B, T, B_HEADS * HEAD_DIM), BF16),
        scratch_shapes=[
            pltpu.VMEM((G, 2 * LANES, N), BF16),
            pltpu.VMEM((G, 1, N), F32),
            pltpu.VMEM((G, LANES, N), F32),
            pltpu.VMEM((G, tk, N), F32),
            pltpu.VMEM((G, tk, N), F32),
            pltpu.VMEM((G, tk, N), BF16),
            pltpu.VMEM((G, 1, N), F32),
            pltpu.VMEM((G, LANES, N), F32),
            pltpu.VMEM((G, wlen, N), F32),
            pltpu.VMEM((G, wlen, N), BF16),
            pltpu.VMEM((G, HEAD_DIM, N), F32),
        ],
        compiler_params=_params(("parallel", "arbitrary")),
        name="nsa_attention",
    )(q_t, cmp_k, cmp_vt, k, v_t, k, sl["vTw"], misc_t, ov, ex, cz, wz)


def _rms(x, g):
    ms = jnp.mean(x * x, axis=-1, keepdims=True)
    return (x * lax.rsqrt(ms + NORM_EPS)) * g


def _layer_tail_kernel(*refs, n_in, final_norm):
    h_ref = refs[0]
    o_refs = refs[1:1 + n_in]
    w_refs = refs[1 + n_in:1 + 2 * n_in]
    g_ref, wu_ref, wd_ref = refs[1 + 2 * n_in:4 + 2 * n_in]
    rest = refs[4 + 2 * n_in:]
    if final_norm:
        fg_ref, out_ref, xn_sc = rest
    else:
        out_ref, xn_sc = rest
    f = pl.program_id(1)

    @pl.when(f == 0)
    def _():
        mix = _dot(o_refs[0][...], w_refs[0][...])
        for i in range(1, n_in):
            mix = mix + _dot(o_refs[i][...], w_refs[i][...])
        x = h_ref[...] + mix
        xn_sc[...] = _rms(x, g_ref[...]).astype(BF16)
        out_ref[...] = x

    u = _dot(xn_sc[...], wu_ref[...])
    a = jnp.square(jnp.maximum(u, 0.0)).astype(BF16)
    out_ref[...] += _dot(a, wd_ref[...])

    if final_norm:
        @pl.when(f == pl.num_programs(1) - 1)
        def _():
            out_ref[...] = _rms(out_ref[...], fg_ref[...])


def _layer_tail(h2, outs, ws, g, w_up, w_down, final_g=None, tm=1024, tf=1024):
    N, D = h2.shape
    F = w_up.shape[1]
    tm = min(tm, N)
    n_in = len(outs)
    in_specs = [pl.BlockSpec((tm, D), lambda i, f: (i, 0))]
    in_specs += [pl.BlockSpec((tm, o.shape[1]), lambda i, f: (i, 0)) for o in outs]
    in_specs += [pl.BlockSpec(w.shape, lambda i, f: (0, 0)) for w in ws]
    in_specs += [
        pl.BlockSpec((1, D), lambda i, f: (0, 0)),
        pl.BlockSpec((D, tf), lambda i, f: (0, f)),
        pl.BlockSpec((tf, D), lambda i, f: (f, 0)),
    ]
    args = [h2, *outs, *ws, g.reshape(1, D), w_up, w_down]
    if final_g is not None:
        in_specs.append(pl.BlockSpec((1, D), lambda i, f: (0, 0)))
        args.append(final_g.reshape(1, D))
    kern = functools.partial(_layer_tail_kernel, n_in=n_in, final_norm=final_g is not None)
    return pl.pallas_call(
        kern,
        grid=(N // tm, F // tf),
        in_specs=in_specs,
        out_specs=pl.BlockSpec((tm, D), lambda i, f: (i, 0)),
        out_shape=jax.ShapeDtypeStruct((N, D), F32),
        scratch_shapes=[pltpu.VMEM((tm, D), BF16)],
        compiler_params=_params(("parallel", "arbitrary")),
        name="out_proj_mlp",
    )(*args)


def _even_layout():
    offs = {}
    o = 0
    for name, size in (("qa", 512), ("ka", 64), ("va", 64), ("qi", 256), ("ki", 64), ("wi", 4),
                       ("qb", 512), ("kvb", 768), ("gb", 24)):
        offs[name] = o
        o += size
    kvb = lambda which, g: offs["kvb"] + (which * B_KV_GROUPS + g) * HEAD_DIM
    cols, plan, lay = [], [], {}
    n = {k: 0 for k in _KINDS}

    def add(c, roped, kind, opt=None):
        cols.append(c)
        plan.append((roped, kind, n[kind], opt))
        n[kind] += 1
        return n[kind] - 1

    lay["qA"] = n["qT"]
    for p in range(4):
        add(_pair_cols(offs["qa"] + 2 * p * 64, offs["qa"] + (2 * p + 1) * 64), True, "qT", Q_SCALE)
    lay["qB"] = n["qT"]
    for p in range(4):
        add(_pair_cols(offs["qb"] + 2 * p * 64, offs["qb"] + (2 * p + 1) * 64), True, "qT", Q_SCALE)
    lay["qi"] = n["qT"]
    for p in range(2):
        add(_pair_cols(offs["qi"] + 2 * p * 64, offs["qi"] + (2 * p + 1) * 64), True, "qT")
    lay["kA"] = add(_pair_cols(offs["ka"], offs["ka"]), True, "k")
    lay["ki"] = add(_pair_cols(offs["ki"], offs["ki"]), True, "k")
    lay["ks"] = n["k"]
    for g in range(2):
        add(_pair_cols(kvb(2, g), kvb(2, g)), True, "k")
    lay["kw"] = n["k"]
    for g in range(2):
        add(_pair_cols(kvb(4, g), kvb(4, g)), True, "k")
    lay["kc"] = add(_pair_cols(kvb(0, 0), kvb(0, 1)), True, "f")
    lay["vc"] = add(_pair_cols(kvb(1, 0), kvb(1, 1)), False, "f")
    misc = np.full(LANES, -1)
    misc[0:IDX_HEADS] = offs["wi"] + np.arange(IDX_HEADS)
    lay["gate_row0"] = 8
    misc[8:8 + 24] = offs["gb"] + np.arange(24)
    lay["misc"] = add(misc, False, "fT")
    lay["vw"] = n["vTw"]
    for g in range(2):
        add(_head_cols(kvb(5, g)), False, "vTw")
    lay["vs"] = n["vT"]
    for g in range(2):
        add(_head_cols(kvb(3, g)), False, "vT", "ones_row_64")
    lay["vA"] = add(_head_cols(offs["va"]), False, "vT", "ones_row_64")
    return np.concatenate(cols), plan, lay


def _odd_layout():
    cols, plan = [], []
    for h in range(C_HEADS):
        cols.append(_pair_cols(h * 128, h * 128 + 64))
        plan.append((True, "qT", h, Q_SCALE))
    for h in range(C_HEADS):
        cols.append(_pair_cols(1024 + h * 128, 1024 + h * 128 + 64))
        plan.append((True, "k", h, None))
    for h in range(C_HEADS):
        cols.append(2048 + h * 128 + np.arange(LANES))
        plan.append((False, "vT", h, "ones_rows_below"))
    return np.concatenate(cols), plan


def _compress_weights(pe, w1, w2):
    d = _PAIR_D
    which = _PAIR_WHICH
    pe_l = pe[:, :, d]
    pe_l = pe_l.reshape(2, 2, 1, CMP_STRIDE * LANES)
    w1r = w1.astype(BF16).reshape(2, CMP_BLOCK, HEAD_DIM, CMP_HIDDEN)[:, :, d, :]
    per_g = []
    for g in range(B_KV_GROUPS):
        keep = jnp.asarray(which == g)[None, None, :, None]
        per_g.append(jnp.where(keep, w1r, jnp.zeros_like(w1r)))
    w1g = jnp.stack(per_g, axis=1)
    w1g = w1g.reshape(2, B_KV_GROUPS, 2, CMP_STRIDE * LANES, CMP_HIDDEN)
    w2b = w2.astype(BF16)
    w2k = w2b[0][:, d]
    w2v = w2b[1][:, np.arange(LANES) % HEAD_DIM]
    w2l = jnp.stack([w2k, w2v], axis=0)
    return pe_l, w1g, w2l


def _even_mixer(h, norm_g, w_in, cmp_pe, cmp_w1, cmp_w2, w_out, cos_slab, sin_slab):
    B, T, D = h.shape
    cols, plan, lay = _even_layout()
    w = _gather_cols(w_in.astype(BF16), cols)
    sl = _project(h, norm_g, w, cos_slab, sin_slab, plan)

    bias = _dsa_select(sl, lay)
    o_a = _dsa_attention(sl, bias, lay)

    n_ch = T // CMP_STRIDE
    assert (lay["kc"], lay["vc"]) == (0, 1) and sl["f"].shape[1] == 2
    flat = sl["f"].reshape(B, 2, n_ch, CMP_STRIDE * LANES)
    pe_l, w1g, w2l = _compress_weights(cmp_pe, cmp_w1, cmp_w2)
    cmp_k, cmp_vt = _compress(flat, pe_l, w1g, w2l)
    o_b = _nsa_attention(sl, cmp_k, cmp_vt, lay)

    na = A_HEADS * HEAD_DIM
    wo = w_out.astype(BF16)
    return [o_a.reshape(B * T, -1), o_b.reshape(B * T, -1)], [wo[:na], wo[na:]]


def _odd_mixer(h, norm_g, w_in, lam, subln_g, w_out, cos_slab, sin_slab, lambda_init):
    B, T, D = h.shape
    cols, plan = _odd_layout()
    w = _gather_cols(w_in.astype(BF16), cols)
    sl = _project(h, norm_g, w, cos_slab, sin_slab, plan)
    o = _diff_attention(sl, lam, subln_g, lambda_init)
    return [o.reshape(B * T, -1)], [w_out.astype(BF16)]


def kernel(x, mix_norm_g, mlp_norm_g, even_w_in, even_cmp_pe, even_cmp_w1, even_cmp_w2, even_w_out, odd_w_in, odd_lambda, odd_subln_g, odd_w_out, mlp_w_up, mlp_w_down, final_norm_g):
    B, T, D = x.shape
    depth = mix_norm_g.shape[0]
    assert depth >= 1
    cos_slab, sin_slab = _rope_slabs(T)
    h = x
    for layer in range(depth):
        if layer % 2 == 0:
            e = layer // 2
            outs, ws = _even_mixer(h, mix_norm_g[layer], even_w_in[e], even_cmp_pe[e],
                                   even_cmp_w1[e], even_cmp_w2[e], even_w_out[e], cos_slab, sin_slab)
        else:
            o = layer // 2
            lambda_init = 0.8 - 0.6 * math.exp(-0.3 * layer)
            outs, ws = _odd_mixer(h, mix_norm_g[layer], odd_w_in[o], odd_lambda[o], odd_subln_g[o],
                                  odd_w_out[o], cos_slab, sin_slab, lambda_init)
        h2 = _layer_tail(h.reshape(B * T, D), outs, ws, mlp_norm_g[layer],
                         mlp_w_up[layer].astype(BF16), mlp_w_down[layer].astype(BF16),
                         final_g=final_norm_g if layer == depth - 1 else None)
        h = h2.reshape(B, T, D)
    return h
```

```python
import functools
import math

import numpy as np
import jax
import jax.numpy as jnp
from jax import lax
from jax.experimental import pallas as pl
from jax.experimental.pallas import tpu as pltpu

HEAD_DIM = 64
HALF = HEAD_DIM // 2
LANES = 128
ROPE_THETA = 10000.0
NORM_EPS = 1e-6
SCALE = HEAD_DIM ** -0.5

A_HEADS = 8
IDX_HEADS = 4
DSA_TOPK = 256
B_HEADS = 8
B_KV_GROUPS = 2
B_PER_GROUP = B_HEADS // B_KV_GROUPS
CMP_BLOCK = 32
CMP_STRIDE = 16
CMP_HIDDEN = 256
SLC_BLOCK = 64
SLC_TOPN = 16
WINDOW = 512
C_HEADS = 8

KV_TILE = 512
SEL_Q_TILE = 128
SEL_TILES_PER_STEP = 2
DSA_Q_TILE = 256
NSA_Q_TILE = 256
VW_TILE = 128

LOG2E = math.log2(math.e)
Q_SCALE = SCALE * LOG2E
ONES_ROWS = 16
BIG = 2.0 ** 100

NEG_INF = float("-inf")
M_FLOOR = -1e30
INT_MIN = -(2 ** 31)

VMEM_LIMIT = 56 * 1024 * 1024

BF16 = jnp.bfloat16
F32 = jnp.float32


def _dot(a, b):
    return jnp.dot(a, b, preferred_element_type=F32)


def _params(sem):
    return pltpu.CompilerParams(dimension_semantics=sem, vmem_limit_bytes=VMEM_LIMIT)


def _iota(shape, axis):
    return lax.broadcasted_iota(jnp.int32, shape, axis)


def _pair_cols(base_a, base_b):
    lane = np.arange(LANES)
    half = lane // 64
    which = (lane % 64) // HALF
    i = lane % HALF
    base = np.where(which == 0, base_a, base_b)
    return base + half * HALF + i


def _head_cols(base):
    lane = np.arange(LANES)
    return np.where(lane < HEAD_DIM, base + lane, -1)


_PAIR_D = _pair_cols(0, 0)
_PAIR_WHICH = (np.arange(LANES) % 64) // HALF


def _gather_cols(w, cols):
    cols = np.asarray(cols)
    safe = np.where(cols >= 0, cols, 0)
    g = jnp.take(w, jnp.asarray(safe, dtype=jnp.int32), axis=1)
    return jnp.where(jnp.asarray(cols >= 0)[None, :], g, jnp.zeros_like(g))


def _rope_slabs(T):
    inv = 1.0 / (ROPE_THETA ** (jnp.arange(0, HEAD_DIM, 2, dtype=F32) / HEAD_DIM))
    ang = jnp.arange(T, dtype=F32)[:, None] * inv[None, :]
    cos, sin = jnp.cos(ang), jnp.sin(ang)
    cos_slab = jnp.tile(cos, (1, 4))
    sin_slab = jnp.concatenate([-sin, -sin, sin, sin], axis=1)
    return cos_slab, sin_slab


_KINDS = ("qT", "k", "vT", "vTw", "f", "fT")


def _proj_kernel(x_ref, g_ref, w_ref, cos_ref, sin_ref, *out_refs, plan, kinds, chunk):
    outs = dict(zip(kinds, out_refs))
    x = x_ref[0]
    ms = jnp.mean(x * x, axis=-1, keepdims=True)
    xn = ((x * lax.rsqrt(ms + NORM_EPS)) * g_ref[...]).astype(BF16)
    cos = cos_ref[...]
    sin = sin_ref[...]
    n = len(plan)
    for c0 in range(0, n, chunk):
        c1 = min(c0 + chunk, n)
        r = _dot(xn, w_ref[:, c0 * LANES:c1 * LANES])
        for s in range(c0, c1):
            roped, kind, idx, opt = plan[s]
            y = r[:, (s - c0) * LANES:(s - c0 + 1) * LANES]
            if roped:
                y = y * cos + pltpu.roll(y, 64, 1) * sin
            if kind == "qT":
                if opt is not None:
                    y = y * opt
                outs[kind][0, idx] = y.T.astype(BF16)
            elif kind == "k":
                outs[kind][0, idx] = y.astype(BF16)
            elif kind == "vT":
                if opt == "ones_row_64":
                    y = jnp.where(_iota(y.shape, 1) == HEAD_DIM, 1.0, y)
                    outs[kind][0, idx, 0] = y.T.astype(BF16)
                else:
                    outs[kind][0, idx, 0, 0:LANES, :] = y.T.astype(BF16)
                    outs[kind][0, idx, 0, LANES:, :] = jnp.ones((ONES_ROWS, y.shape[0]), BF16)
            elif kind == "vTw":
                y_t = jnp.where(_iota(y.shape, 1) == HEAD_DIM, 1.0, y).T.astype(BF16)
                for sub in range(y.shape[0] // VW_TILE):
                    outs[kind][0, idx, sub] = y_t[:, sub * VW_TILE:(sub + 1) * VW_TILE]
            elif kind == "f":
                outs[kind][0, idx] = y
            else:
                outs[kind][0, idx] = y.T


def _project(x, g, w, cos_slab, sin_slab, plan, chunk=4):
    B, T, D = x.shape
    tm = KV_TILE
    assert T % tm == 0
    count = {k: sum(1 for p in plan if p[1] == k) for k in _KINDS}
    kinds = tuple(k for k in _KINDS if count[k])
    out_specs, out_shape = [], []
    for k in kinds:
        n = count[k]
        if k in ("qT", "fT"):
            out_specs.append(pl.BlockSpec((1, n, LANES, tm), lambda b, i: (b, 0, 0, i)))
            out_shape.append(jax.ShapeDtypeStruct((B, n, LANES, T), BF16 if k == "qT" else F32))
        elif k in ("k", "f"):
            out_specs.append(pl.BlockSpec((1, n, tm, LANES), lambda b, i: (b, 0, i, 0)))
            out_shape.append(jax.ShapeDtypeStruct((B, n, T, LANES), BF16 if k == "k" else F32))
        elif k == "vTw":
            sub = tm // VW_TILE
            out_specs.append(pl.BlockSpec((1, n, sub, LANES, VW_TILE), lambda b, i: (b, 0, i, 0, 0)))
            out_shape.append(jax.ShapeDtypeStruct((B, n, T // VW_TILE, LANES, VW_TILE), BF16))
        else:
            wide = any(p[1] == "vT" and p[3] != "ones_row_64" for p in plan)
            rows = LANES + ONES_ROWS if wide else LANES
            out_specs.append(pl.BlockSpec((1, n, 1, rows, tm), lambda b, i: (b, 0, i, 0, 0)))
            out_shape.append(jax.ShapeDtypeStruct((B, n, T // tm, rows, tm), BF16))
    kern = functools.partial(_proj_kernel, plan=tuple(plan), kinds=kinds, chunk=chunk)
    outs = pl.pallas_call(
        kern,
        grid=(B, T // tm),
        in_specs=[
            pl.BlockSpec((1, tm, D), lambda b, i: (b, i, 0)),
            pl.BlockSpec((1, D), lambda b, i: (0, 0)),
            pl.BlockSpec((D, len(plan) * LANES), lambda b, i: (0, 0)),
            pl.BlockSpec((tm, LANES), lambda b, i: (i, 0)),
            pl.BlockSpec((tm, LANES), lambda b, i: (i, 0)),
        ],
        out_specs=out_specs,
        out_shape=out_shape,
        compiler_params=_params(("parallel", "parallel")),
        name="norm_proj_rope",
    )(x, g.reshape(1, D), w, cos_slab, sin_slab)
    return dict(zip(kinds, outs))


def _head_rows(slab_t, which):
    row = _iota(slab_t.shape, 0)
    keep = ((row % 64) // HALF) == which
    return jnp.where(keep, slab_t, jnp.zeros_like(slab_t))


def _tree(op, xs):
    while len(xs) > 1:
        xs = [op(xs[i], xs[i + 1]) if i + 1 < len(xs) else xs[i] for i in range(0, len(xs), 2)]
    return xs[0]


def _fold_rows_max(x, ways=4):
    rows, n = x.shape
    per = rows // ways
    parts = [jnp.max(x[i * per:(i + 1) * per].reshape(per // 8, 8, n), axis=0) for i in range(ways)]
    return _tree(jnp.maximum, parts)


ROW_BLOCK = 64


def _online_update(s_sc, p_sc, v_aug, m_ref, acc_ref):
    tk, n = s_sc.shape
    m_prev = m_ref[...]
    m_tile = jnp.max(_fold_rows_max(s_sc[...]), axis=0, keepdims=True)
    m_new = jnp.maximum(m_prev, m_tile)
    alpha = jnp.exp2(m_prev - m_new)
    for r in range(tk // ROW_BLOCK):
        rows = slice(r * ROW_BLOCK, (r + 1) * ROW_BLOCK)
        p_sc[rows, :] = jnp.exp2(s_sc[rows, :] - m_new).astype(BF16)
    acc_ref[...] = alpha * acc_ref[...] + _dot(v_aug, p_sc[...])
    m_ref[...] = m_new


def _flash_pipeline(n, put_scores, update, buf_a, buf_b, put_last=None, update_last=None):
    if put_last is None:
        put_scores(buf_a, 0)
    else:
        @pl.when(n > 0)
        def _():
            put_scores(buf_a, 0)

    def pair(p, carry):
        j = 2 * p
        put_scores(buf_b, j + 1)
        update(buf_a, j)

        @pl.when(j + 2 < n)
        def _():
            put_scores(buf_a, j + 2)
            update(buf_b, j + 1)

        return carry

    lax.fori_loop(0, n // 2, pair, 0)
    in_b = jnp.logical_and(n > 0, n % 2 == 0)
    in_a = n % 2 == 1

    update_last = update_last or update

    def finish(cur, other):
        if put_last is not None:
            put_last(other, n)
        update(cur, n - 1)
        if put_last is not None:
            update_last(other, n)

    pl.when(in_b)(lambda: finish(buf_b, buf_a))
    pl.when(in_a)(lambda: finish(buf_a, buf_b))
    if put_last is not None:
        @pl.when(n == 0)
        def _():
            put_last(buf_a, 0)
            update_last(buf_a, 0)


def _softmax_keys(s):
    m = jnp.max(s, axis=0, keepdims=True)
    m = jnp.where(m > NEG_INF, m, 0.0)
    e = jnp.exp2(s - m)
    return e / jnp.maximum(jnp.sum(e, axis=0, keepdims=True), 1e-30)


def _init_flash(m_ref, acc_ref):
    m_ref[...] = jnp.full(m_ref.shape, M_FLOOR, F32)
    acc_ref[...] = jnp.zeros(acc_ref.shape, F32)


DIFF_HEADS_PER_STEP = 2


def _diff_attn_kernel(lam_ref, q_ref, k_ref, v_ref, g_ref, o_ref, m_sc, acc_sc, sa_sc, sb_sc,
                      p_sc, *, tq, lambda_init):
    qi = pl.program_id(2)
    HS = DIFF_HEADS_PER_STEP
    qs = [[_head_rows(q_ref[0, hh], c) for c in range(2)] for hh in range(HS)]
    for s in range(2 * HS):
        _init_flash(m_sc.at[s], acc_sc.at[s])

    def put_scores(buf, j):
        for hh in range(HS):
            k = k_ref[0, hh, pl.ds(pl.multiple_of(j * tq, tq), tq), :]
            for c in range(2):
                buf[2 * hh + c] = _dot(k, qs[hh][c])

    half = tq // 2
    pieces = ((slice(0, half), half), (slice(half, tq), tq))

    def put_diagonal(buf, j):
        for hh in range(HS):
            start = pl.multiple_of(j * tq, tq)
            for c in range(2):
                for lanes, rows in pieces:
                    k = k_ref[0, hh, pl.ds(start, rows), :]
                    s = _dot(k, qs[hh][c][:, lanes])
                    keep = _iota(s.shape, 0) <= _iota(s.shape, 1) + lanes.start
                    buf[2 * hh + c, 0:rows, lanes] = jnp.where(keep, s, NEG_INF)

    def update(buf, j):
        for hh in range(HS):
            v_aug = v_ref[0, hh, j]
            for c in range(2):
                s = 2 * hh + c
                _online_update(buf.at[s], p_sc.at[s], v_aug, m_sc.at[s], acc_sc.at[s])

    def update_diagonal(buf, j):
        for hh in range(HS):
            v_aug = v_ref[0, hh, j]
            for c in range(2):
                s = 2 * hh + c
                for lanes, rows in pieces:
                    _online_update(buf.at[s, 0:rows, lanes], p_sc.at[s, 0:rows, lanes],
                                   v_aug[:, 0:rows], m_sc.at[s, :, lanes], acc_sc.at[s, :, lanes])

    _flash_pipeline(qi, put_scores, update, sa_sc, sb_sc, put_last=put_diagonal,
                    update_last=update_diagonal)

    lam = lam_ref[...]
    s01 = jnp.sum(lam[0:1] * lam[1:2], axis=-1, keepdims=True)
    s23 = jnp.sum(lam[2:3] * lam[3:4], axis=-1, keepdims=True)
    lam_val = jnp.exp(s01) - jnp.exp(s23) + lambda_init
    for hh in range(HS):
        a0, a1 = acc_sc[2 * hh], acc_sc[2 * hh + 1]
        o0 = a0[0:LANES] / jnp.maximum(a0[LANES:LANES + 1], 1e-30)
        o1 = a1[0:LANES] / jnp.maximum(a1[LANES:LANES + 1], 1e-30)
        o = o0 - lam_val * o1
        y = o * lax.rsqrt(jnp.mean(o * o, axis=0, keepdims=True) + NORM_EPS)
        y = (y * g_ref[...]) * (1.0 - lambda_init)
        o_ref[0, :, hh * LANES:(hh + 1) * LANES] = y.T.astype(o_ref.dtype)


def _diff_attention(sl, lam, subln_g, lambda_init):
    q_t, k, v_t = sl["qT"], sl["k"], sl["vT"]
    B, H, T, _ = k.shape
    tq = KV_TILE
    nk = T // tq
    kern = functools.partial(_diff_attn_kernel, tq=tq, lambda_init=lambda_init)
    HS = DIFF_HEADS_PER_STEP
    ns = 2 * HS
    return pl.pallas_call(
        kern,
        grid=(B, H // HS, T // tq),
        in_specs=[
            pl.BlockSpec((4, HEAD_DIM), lambda b, h, i: (0, 0)),
            pl.BlockSpec((1, HS, LANES, tq), lambda b, h, i: (b, h, 0, i)),
            pl.BlockSpec((1, HS, T, LANES), lambda b, h, i: (b, h, 0, 0)),
            pl.BlockSpec((1, HS, nk, LANES + ONES_ROWS, tq), lambda b, h, i: (b, h, 0, 0, 0)),
            pl.BlockSpec((LANES, 1), lambda b, h, i: (0, 0)),
        ],
        out_specs=pl.BlockSpec((1, tq, HS * LANES), lambda b, h, i: (b, i, h)),
        out_shape=jax.ShapeDtypeStruct((B, T, H * LANES), BF16),
        scratch_shapes=[
            pltpu.VMEM((ns, 1, tq), F32),
            pltpu.VMEM((ns, LANES + ONES_ROWS, tq), F32),
            pltpu.VMEM((ns, tq, tq), F32),
            pltpu.VMEM((ns, tq, tq), F32),
            pltpu.VMEM((ns, tq, tq), BF16),
        ],
        compiler_params=_params(("parallel", "parallel", "arbitrary")),
        name="diff_attention",
    )(lam, q_t, k, v_t, subln_g.reshape(LANES, 1))


def _dsa_select_kernel(qi_ref, ki_ref, misc_ref, tril_ref, bias_ref, key_sc, *, tq, ck, nk, topk):
    S = key_sc.shape[0]
    t0 = pl.program_id(1) * (S * tq)
    nvalid = (t0 + S * tq - 1) // ck + 1
    lanes = [slice(s * tq, (s + 1) * tq) for s in range(S)]
    w = [misc_ref[0, 0, :, lanes[s]] for s in range(S)]
    qh = [jnp.concatenate([_head_rows(qi_ref[0, h // 2, :, lanes[s]], h % 2)
                           for h in range(IDX_HEADS)], axis=1) for s in range(S)]
    t_q = [t0 + s * tq + _iota((ck, tq), 1) for s in range(S)]

    def causal(c, s):
        return (c * ck + _iota((ck, tq), 0)) <= t_q[s]

    def fill(c, s):
        kk = ki_ref[0, 0, c * ck:(c + 1) * ck, :]
        r = _dot(kk, qh[s])
        score = jnp.zeros((ck, tq), F32)
        for h in range(IDX_HEADS):
            score = score + w[s][h:h + 1, :] * jnp.maximum(r[:, h * tq:(h + 1) * tq], 0.0)
        bits = pltpu.bitcast(score, jnp.int32)
        key = bits ^ ((bits >> 31) & jnp.int32(0x7FFFFFFF))
        key = jnp.where(score == 0.0, 0, key)
        key_sc[s, c] = jnp.where(causal(c, s), key, INT_MIN)

    kf = float(topk)

    def select_threshold(n):
        for c in range(n):
            for s in range(S):
                fill(c, s)

        def count(pred, s):
            acc = jnp.zeros((ck // 8, tq), F32)
            for c in range(n):
                ind = pred(key_sc[s, c], c)
                acc = acc + jnp.sum(ind.reshape(8, ck // 8, tq), axis=0)
            return jnp.sum(acc, axis=0, keepdims=True)

        zero = jnp.zeros((1, tq), jnp.int32)
        ans = tuple(jnp.where(count(lambda kc, c: jnp.where(kc >= zero, 1.0, 0.0), s) >= kf, 0, INT_MIN)
                    for s in range(S))

        def bit_step(i, ans):
            out = []
            for s in range(S):
                cand = ans[s] | (jnp.int32(1) << (30 - i))
                cnt = count(lambda kc, c: jnp.where(kc >= cand, 1.0, 0.0), s)
                out.append(jnp.where(cnt >= kf, cand, ans[s]))
            return tuple(out)

        ans = lax.fori_loop(0, 31, bit_step, ans)
        cnt_gt = tuple(count(lambda kc, c: jnp.where(kc > ans[s], 1.0, 0.0), s) for s in range(S))
        cnt_eq = tuple(count(lambda kc, c: jnp.where(kc == ans[s],
                                                     jnp.where(causal(c, s), 1.0, 0.0), 0.0), s)
                       for s in range(S))
        return ans, cnt_gt, cnt_eq

    ans, cnt_gt, cnt_eq = lax.switch(
        nvalid - 1, [functools.partial(select_threshold, n) for n in range(1, nk + 1)])
    need = [kf - cnt_gt[s] for s in range(S)]
    has_tie = _tree(jnp.maximum, [jnp.max(cnt_eq[s] - need[s]) for s in range(S)]) > 0.0

    @pl.when(jnp.logical_not(has_tie))
    def _():
        def emit(c, carry):
            for s in range(S):
                sel = jnp.where(key_sc[s, c] >= ans[s], jnp.where(causal(c, s), 0.0, NEG_INF), NEG_INF)
                bias_ref[0, s, c] = sel.astype(BF16)
            return carry
        lax.fori_loop(0, nvalid, emit, 0)

    @pl.when(has_tie)
    def _():
        def emit(c, carry):
            out = []
            for s in range(S):
                kc = key_sc[s, c]
                eq = jnp.where(kc == ans[s], jnp.where(causal(c, s), 1.0, 0.0), 0.0)
                pre = _dot(tril_ref[...], eq.astype(BF16)) + carry[s]
                take = jnp.where(kc > ans[s], 1.0, jnp.where(pre < need[s], eq, 0.0))
                sel = jnp.where(take > 0.5, jnp.where(causal(c, s), 0.0, NEG_INF), NEG_INF)
                bias_ref[0, s, c] = sel.astype(BF16)
                out.append(carry[s] + jnp.sum(eq, axis=0, keepdims=True))
            return tuple(out)
        lax.fori_loop(0, nvalid, emit, tuple(jnp.zeros((1, tq), F32) for _ in range(S)))

    def blank(c, carry):
        for s in range(S):
            bias_ref[0, s, c] = jnp.full((ck, tq), NEG_INF, BF16)
        return carry

    lax.fori_loop(nvalid, nk, blank, 0)


def _dsa_select(sl, lay):
    q_t, k, misc_t = sl["qT"], sl["k"], sl["fT"]
    B, _, T, _ = k.shape
    tq, ck = SEL_Q_TILE, KV_TILE
    nk = T // ck
    topk = min(DSA_TOPK, T // 4)
    tril = jnp.asarray(np.tril(np.ones((ck, ck), np.float32), -1), BF16)
    kern = functools.partial(_dsa_select_kernel, tq=tq, ck=ck, nk=nk, topk=topk)
    S = SEL_TILES_PER_STEP
    assert ck % (S * tq) == 0 and T % (S * tq) == 0
    return pl.pallas_call(
        kern,
        grid=(B, T // (S * tq)),
        in_specs=[
            pl.BlockSpec((1, 2, LANES, S * tq), lambda b, i: (b, lay["qi"] // 2, 0, i)),
            pl.BlockSpec((1, 1, T, LANES), lambda b, i: (b, lay["ki"], 0, 0)),
            pl.BlockSpec((1, 1, LANES, S * tq), lambda b, i: (b, lay["misc"], 0, i)),
            pl.BlockSpec((ck, ck), lambda b, i: (0, 0)),
        ],
        out_specs=pl.BlockSpec((1, S, nk, ck, tq), lambda b, i: (b, i, 0, 0, 0)),
        out_shape=jax.ShapeDtypeStruct((B, T // tq, nk, ck, tq), BF16),
        scratch_shapes=[pltpu.VMEM((S, nk, ck, tq), jnp.int32)],
        compiler_params=_params(("parallel", "parallel")),
        name="dsa_select",
    )(q_t, k, misc_t, tril)


def _store_head_pairs(o_ref, o, n_heads, tq, col0=0, row0=0):
    for p in range(n_heads // 2):
        even = o[0:HEAD_DIM, (2 * p) * tq:(2 * p + 1) * tq]
        odd = o[0:HEAD_DIM, (2 * p + 1) * tq:(2 * p + 2) * tq]
        pair = jnp.concatenate([even, odd], axis=0)
        cols = slice(col0 + p * LANES, col0 + (p + 1) * LANES)
        o_ref[0, row0:row0 + tq, cols] = pair.T.astype(o_ref.dtype)


def _normalized(acc):
    return acc[0:HEAD_DIM] / jnp.maximum(acc[HEAD_DIM:HEAD_DIM + 1], 1e-30)


def _dsa_attn_kernel(q_ref, k_ref, v_ref, bias_ref, o_ref, qst, m_sc, acc_sc, sa_sc, sb_sc,
                     p_sc, *, tq, tk):
    qi = pl.program_id(1)
    H = A_HEADS
    S = sa_sc.shape[0]
    nsub = bias_ref.shape[1] // S
    for s in range(S):
        for h in range(H):
            qst[s, :, h * tq:(h + 1) * tq] = _head_rows(q_ref[0, h // 2, :, s * tq:(s + 1) * tq], h % 2)
        _init_flash(m_sc.at[s], acc_sc.at[s])
    nkv = (qi * S * tq) // tk + 1

    def put_scores(buf, j):
        k = k_ref[0, 0, pl.ds(pl.multiple_of(j * tk, tk), tk), :]
        for s in range(S):
            parts = [bias_ref[0, s * nsub + u, j].astype(F32) for u in range(nsub)]
            b = parts[0] if len(parts) == 1 else jnp.concatenate(parts, axis=1)
            buf[s] = _dot(k, qst[s]) + jnp.concatenate([b] * H, axis=1)

    def update(buf, j):
        for s in range(S):
            _online_update(buf.at[s], p_sc.at[s], v_ref[0, 0, j], m_sc.at[s], acc_sc.at[s])

    def put_last(buf, j):
        for s in range(S):
            rows = (s + 1) * tq
            k = k_ref[0, 0, pl.ds(pl.multiple_of(j * tk, tk), rows), :]
            parts = [bias_ref[0, s * nsub + u, j, 0:rows, :].astype(F32) for u in range(nsub)]
            b = parts[0] if len(parts) == 1 else jnp.concatenate(parts, axis=1)
            buf[s, 0:rows, :] = _dot(k, qst[s]) + jnp.concatenate([b] * H, axis=1)

    def update_last(buf, j):
        for s in range(S):
            rows = (s + 1) * tq
            _online_update(buf.at[s, 0:rows, :], p_sc.at[s, 0:rows, :], v_ref[0, 0, j, :, 0:rows],
                           m_sc.at[s], acc_sc.at[s])

    _flash_pipeline(nkv - 1, put_scores, update, sa_sc, sb_sc, put_last=put_last,
                    update_last=update_last)
    for s in range(S):
        _store_head_pairs(o_ref, _normalized(acc_sc[s]), H, tq, row0=s * tq)


def _dsa_attention(sl, bias, lay):
    q_t, k, v_t = sl["qT"], sl["k"], sl["vT"]
    B, _, T, _ = k.shape
    tq, tk = DSA_Q_TILE, KV_TILE
    nk = T // tk
    H = A_HEADS
    N = H * tq
    sel_tq = bias.shape[4]
    S = tk // tq
    nsub = S * tq // sel_tq
    kern = functools.partial(_dsa_attn_kernel, tq=tq, tk=tk)
    return pl.pallas_call(
        kern,
        grid=(B, T // (S * tq)),
        in_specs=[
            pl.BlockSpec((1, H // 2, LANES, S * tq), lambda b, i: (b, lay["qA"] // (H // 2), 0, i)),
            pl.BlockSpec((1, 1, T, LANES), lambda b, i: (b, lay["kA"], 0, 0)),
            pl.BlockSpec((1, 1, nk, LANES, tk), lambda b, i: (b, lay["vA"], 0, 0, 0)),
            pl.BlockSpec((1, nsub, nk, tk, sel_tq), lambda b, i: (b, i, 0, 0, 0)),
        ],
        out_specs=pl.BlockSpec((1, S * tq, H * HEAD_DIM), lambda b, i: (b, i, 0)),
        out_shape=jax.ShapeDtypeStruct((B, T, H * HEAD_DIM), BF16),
        scratch_shapes=[
            pltpu.VMEM((S, LANES, N), BF16),
            pltpu.VMEM((S, 1, N), F32),
            pltpu.VMEM((S, LANES, N), F32),
            pltpu.VMEM((S, tk, N), F32),
            pltpu.VMEM((S, tk, N), F32),
            pltpu.VMEM((S, tk, N), BF16),
        ],
        compiler_params=_params(("parallel", "arbitrary")),
        name="dsa_attention",
    )(q_t, k, v_t, bias)


def _compress_kernel(x_ref, pe_ref, w1_ref, w2_ref, o_ref, ot_ref, *, n_ch):
    x = x_ref[0, 0]
    xt = (x + pe_ref[0, 0]).astype(BF16)
    xb = (x + pe_ref[0, 1]).astype(BF16)
    for g in range(B_KV_GROUPS):
        a = _dot(xt, w1_ref[0, g, 0])
        b = _dot(xb, w1_ref[0, g, 1])
        h = jax.nn.gelu(a + pltpu.roll(b, n_ch - 1, 0))
        r = _dot(h.astype(BF16), w2_ref[0])
        o_ref[0, 0, g] = r.astype(BF16)
        ot_ref[0, 0, g] = r.T.astype(BF16)


def _compress(flat, pe, w1, w2):
    B, _, n_ch, W = flat.shape
    G = B_KV_GROUPS
    kern = functools.partial(_compress_kernel, n_ch=n_ch)
    return pl.pallas_call(
        kern,
        grid=(2, B),
        in_specs=[
            pl.BlockSpec((1, 1, n_ch, W), lambda s, b: (b, s, 0, 0)),
            pl.BlockSpec((1, 2, 1, W), lambda s, b: (s, 0, 0, 0)),
            pl.BlockSpec((1, G, 2, W, CMP_HIDDEN), lambda s, b: (s, 0, 0, 0, 0)),
            pl.BlockSpec((1, CMP_HIDDEN, LANES), lambda s, b: (s, 0, 0)),
        ],
        out_specs=[
            pl.BlockSpec((1, 1, G, n_ch, LANES), lambda s, b: (s, b, 0, 0, 0)),
            pl.BlockSpec((1, 1, G, LANES, n_ch), lambda s, b: (s, b, 0, 0, 0)),
        ],
        out_shape=[
            jax.ShapeDtypeStruct((2, B, G, n_ch, LANES), BF16),
            jax.ShapeDtypeStruct((2, B, G, LANES, n_ch), BF16),
        ],
        compiler_params=_params(("parallel", "parallel")),
        name="nsa_compress",
    )(flat, pe, w1, w2)


def _nsa_kernel(q_ref, kc_ref, vc_ref, ks_ref, vs_ref, kw_ref, vw_ref, misc_ref, ov_ref, ex_ref,
                cz_ref, wz_ref, o_ref, qst, m_sc, acc_sc, sa_sc, sb_sc, p_sc, mw_sc, accw_sc,
                sw_sc, pw_sc, oc_sc, *, tq, tk, T, n_s, n_sel, gate_row0):
    qi = pl.program_id(1)
    t0 = qi * tq
    G, J = B_KV_GROUPS, B_PER_GROUP
    N = J * tq
    for g in range(G):
        for j in range(J):
            qst[g, 0:LANES, j * tq:(j + 1) * tq] = _head_rows(q_ref[0, 2 * g + j // 2], j % 2)

    def q_time(shape):
        return t0 + (_iota(shape, 1) % tq)

    wlen = min(WINDOW + tq, T)
    wstart = pl.multiple_of(jnp.maximum(t0 - WINDOW, 0), tq)
    n_wt = wlen // tq
    wt = wstart // VW_TILE
    wz = wz_ref[jnp.minimum(qi, n_wt - 1)].astype(F32)
    wz = jnp.concatenate([wz] * J, axis=1)
    for g in range(G):
        kw = kw_ref[0, g, pl.ds(wstart, wlen), :]
        vw = jnp.concatenate([vw_ref[0, g, wt + i] for i in range(wlen // VW_TILE)], axis=1)
        sw_sc[g] = _dot(kw, qst[g, 0:LANES, :]) + wz
        _init_flash(mw_sc.at[g], accw_sc.at[g])
        _online_update(sw_sc.at[g], pw_sc.at[g], vw, mw_sc.at[g], accw_sc.at[g])

    rows = -(-n_s // 8) * 8
    blk = _iota((rows, tq), 0)
    t_q = t0 + _iota((rows, tq), 1)
    cur = t_q // SLC_BLOCK
    forced = (blk == 0) | (blk == cur) | (blk == cur - 1)
    admissible = blk * SLC_BLOCK <= t_q
    blk8 = _iota((8, tq), 0)
    for g in range(G):
        q = qst[g, 0:LANES, :]
        kc = kc_ref[0, 0, g]
        n_ch = kc.shape[0]
        s_c = _dot(kc, q)
        cmp_end = _iota((n_ch, N), 0) * CMP_STRIDE + (CMP_BLOCK - 1)
        s_c = jnp.where(cmp_end <= q_time((n_ch, N)), s_c, NEG_INF)
        p_c = _softmax_keys(s_c)
        oc_sc[g] = _dot(vc_ref[0, 0, g], p_c.astype(BF16))[0:HEAD_DIM]
        psum = p_c[:, 0:tq]
        for j in range(1, J):
            psum = psum + p_c[:, j * tq:(j + 1) * tq]
        p_hi = psum.astype(BF16)
        p_lo = (psum - p_hi.astype(F32)).astype(BF16)
        imp = _dot(ov_ref[...], p_hi) + _dot(ov_ref[...], p_lo)
        sc = jnp.where(forced, jnp.inf, imp[0:rows])
        sc = jnp.where(admissible, sc, NEG_INF)
        groups = [sc[8 * r:8 * r + 8] for r in range(rows // 8)]
        ranks = [jnp.zeros((8, tq), F32) for _ in groups]
        for m in range(n_s):
            cm = sc[m:m + 1, :]
            for r, grp in enumerate(groups):
                gt = jnp.where(cm > grp, 1.0, 0.0)
                ge = jnp.where(cm >= grp, 1.0, 0.0)
                if 8 * r + 7 <= m:
                    first = gt
                elif 8 * r > m:
                    first = ge
                else:
                    first = jnp.where(blk8 + 8 * r > m, ge, gt)
                ranks[r] = ranks[r] + first
        rank = jnp.concatenate(ranks, axis=0)
        drop = jnp.where(rank < float(n_sel), 0.0, 1.0)
        if rows < LANES:
            drop = jnp.concatenate([drop, jnp.ones((LANES - rows, tq), F32)], axis=0)
        drop = drop.astype(BF16)
        qst[g, LANES:2 * LANES, :] = jnp.concatenate([drop] * J, axis=1)

    jd = t0 // tk
    off = (t0 - jd * tk) // tq
    n_off = tk // tq
    for g in range(G):
        _init_flash(m_sc.at[g], acc_sc.at[g])

    def put_scores(buf, j):
        cz = cz_ref[jnp.where(j == jd, off, n_off)].astype(F32)
        cz = jnp.concatenate([cz] * J, axis=1)
        for g in range(G):
            k = ks_ref[0, g, pl.ds(pl.multiple_of(j * tk, tk), tk), :]
            lhs = jnp.concatenate([k, ex_ref[j]], axis=1)
            buf[g] = _dot(lhs, qst[g]) + cz

    def update(buf, j):
        for g in range(G):
            _online_update(buf.at[g], p_sc.at[g], vs_ref[0, g, j], m_sc.at[g], acc_sc.at[g])

    _flash_pipeline(jd + 1, put_scores, update, sa_sc, sb_sc)

    gates = jax.nn.sigmoid(misc_ref[0, 0])
    for g in range(G):
        def gate_row(c, g=g):
            rws = [gate_row0 + (g * J + j) * 3 + c for j in range(J)]
            return jnp.concatenate([gates[r:r + 1, :] for r in rws], axis=1)

        o_s = _normalized(acc_sc[g])
        o_w = _normalized(accw_sc[g])
        o = gate_row(0) * oc_sc[g] + (gate_row(1) * o_s + gate_row(2) * o_w)
        _store_head_pairs(o_ref, o, J, tq, col0=g * J * HEAD_DIM)


def _nsa_attention(sl, cmp_k, cmp_vt, lay):
    q_t, k, v_t, misc_t = sl["qT"], sl["k"], sl["vT"], sl["fT"]
    B, _, T, _ = k.shape
    G, J = B_KV_GROUPS, B_PER_GROUP
    tq, tk = NSA_Q_TILE, KV_TILE
    nk = T // tk
    n_ch = T // CMP_STRIDE
    n_c = n_ch - CMP_BLOCK // CMP_STRIDE + 1
    n_s = T // SLC_BLOCK
    n_sel = min(SLC_TOPN, n_s)
    assert n_s <= LANES
    c0 = np.arange(n_ch) * CMP_STRIDE
    s0 = np.arange(LANES) * SLC_BLOCK
    ov = ((c0[None, :] < s0[:, None] + SLC_BLOCK) & (c0[None, :] + CMP_BLOCK > s0[:, None]))
    ov = ov & (np.arange(n_ch)[None, :] < n_c) & (np.arange(LANES)[:, None] < n_s)
    ov = jnp.asarray(ov.astype(np.float32), BF16)
    pos = np.arange(T).reshape(nk, tk, 1)
    ex = (pos // SLC_BLOCK == np.arange(LANES).reshape(1, 1, LANES))
    ex = jnp.asarray(ex.astype(np.float32) * -BIG, BF16)
    n_off = tk // tq
    kp = np.arange(tk).reshape(1, tk, 1)
    tl = np.arange(tq).reshape(1, 1, tq) + np.arange(n_off + 1).reshape(n_off + 1, 1, 1) * tq
    cz = np.where((kp <= tl) | (np.arange(n_off + 1).reshape(-1, 1, 1) == n_off), 0.0, -BIG)
    cz = jnp.asarray(cz.astype(np.float32), BF16)
    wlen = min(WINDOW + tq, T)
    n_wt = wlen // tq
    assert wlen % tq == 0 and T >= wlen
    kp = np.arange(wlen).reshape(1, wlen, 1)
    tl = np.arange(tq).reshape(1, 1, tq)
    early = kp <= tl + np.arange(n_wt).reshape(n_wt, 1, 1) * tq
    late = (kp > tl) & (kp <= tl + WINDOW)
    band = np.where(np.arange(n_wt).reshape(n_wt, 1, 1) == n_wt - 1, late, early)
    wz = jnp.asarray(np.where(band, 0.0, -BIG).astype(np.float32), BF16)
    kern = functools.partial(_nsa_kernel, tq=tq, tk=tk, T=T, n_s=n_s, n_sel=n_sel,
                             gate_row0=lay["gate_row0"])
    N = J * tq
    for name in ("qB", "ks", "vs", "kw", "vw"):
        assert lay[name] % (2 * G if name == "qB" else G) == 0
    kslab = lambda off: pl.BlockSpec((1, G, T, LANES), lambda b, i: (b, off // G, 0, 0))
    return pl.pallas_call(
        kern,
        grid=(B, T // tq),
        in_specs=[
            pl.BlockSpec((1, 2 * G, LANES, tq), lambda b, i: (b, lay["qB"] // (2 * G), 0, i)),
            pl.BlockSpec((1, 1, G, n_ch, LANES), lambda b, i: (0, b, 0, 0, 0)),
            pl.BlockSpec((1, 1, G, LANES, n_ch), lambda b, i: (1, b, 0, 0, 0)),
            kslab(lay["ks"]),
            pl.BlockSpec((1, G, nk, LANES, tk), lambda b, i: (b, lay["vs"] // G, 0, 0, 0)),
            kslab(lay["kw"]),
            pl.BlockSpec((1, G, T // VW_TILE, LANES, VW_TILE),
                         lambda b, i: (b, lay["vw"] // G, 0, 0, 0)),
            pl.BlockSpec((1, 1, LANES, tq), lambda b, i: (b, lay["misc"], 0, i)),
            pl.BlockSpec((LANES, n_ch), lambda b, i: (0, 0)),
            pl.BlockSpec((nk, tk, LANES), lambda b, i: (0, 0, 0)),
            pl.BlockSpec((n_off + 1, tk, tq), lambda b, i: (0, 0, 0)),
            pl.BlockSpec((n_wt, wlen, tq), lambda b, i: (0, 0, 0)),
        ],
        out_specs=pl.BlockSpec((1, tq, B_HEADS * HEAD_DIM), lambda b, i: (b, i, 0)),
        out_shape=jax.ShapeDtypeStruct((B, T, B_HEADS * HEAD_DIM), BF16),
        scratch_shapes=[
            pltpu.VMEM((G, 2 * LANES, N), BF16),
            pltpu.VMEM((G, 1, N), F32),
            pltpu.VMEM((G, LANES, N), F32),
            pltpu.VMEM((G, tk, N), F32),
            pltpu.VMEM((G, tk, N), F32),
            pltpu.VMEM((G, tk, N), BF16),
            pltpu.VMEM((G, 1, N), F32),
            pltpu.VMEM((G, LANES, N), F32),
            pltpu.VMEM((G, wlen, N), F32),
            pltpu.VMEM((G, wlen, N), BF16),
            pltpu.VMEM((G, HEAD_DIM, N), F32),
        ],
        compiler_params=_params(("parallel", "arbitrary")),
        name="nsa_attention",
    )(q_t, cmp_k, cmp_vt, k, v_t, k, sl["vTw"], misc_t, ov, ex, cz, wz)


def _rms(x, g):
    ms = jnp.mean(x * x, axis=-1, keepdims=True)
    return (x * lax.rsqrt(ms + NORM_EPS)) * g


def _layer_tail_kernel(*refs, n_in, final_norm):
    h_ref = refs[0]
    o_refs = refs[1:1 + n_in]
    w_refs = refs[1 + n_in:1 + 2 * n_in]
    g_ref, wu_ref, wd_ref = refs[1 + 2 * n_in:4 + 2 * n_in]
    rest = refs[4 + 2 * n_in:]
    if final_norm:
        fg_ref, out_ref, xn_sc = rest
    else:
        out_ref, xn_sc = rest
    f = pl.program_id(1)

    @pl.when(f == 0)
    def _():
        mix = _dot(o_refs[0][...], w_refs[0][...])
        for i in range(1, n_in):
            mix = mix + _dot(o_refs[i][...], w_refs[i][...])
        x = h_ref[...] + mix
        xn_sc[...] = _rms(x, g_ref[...]).astype(BF16)
        out_ref[...] = x

    u = _dot(xn_sc[...], wu_ref[...])
    a = jnp.square(jnp.maximum(u, 0.0)).astype(BF16)
    out_ref[...] += _dot(a, wd_ref[...])

    if final_norm:
        @pl.when(f == pl.num_programs(1) - 1)
        def _():
            out_ref[...] = _rms(out_ref[...], fg_ref[...])


def _layer_tail(h2, outs, ws, g, w_up, w_down, final_g=None, tm=1024, tf=1024):
    N, D = h2.shape
    F = w_up.shape[1]
    tm = min(tm, N)
    n_in = len(outs)
    in_specs = [pl.BlockSpec((tm, D), lambda i, f: (i, 0))]
    in_specs += [pl.BlockSpec((tm, o.shape[1]), lambda i, f: (i, 0)) for o in outs]
    in_specs += [pl.BlockSpec(w.shape, lambda i, f: (0, 0)) for w in ws]
    in_specs += [
        pl.BlockSpec((1, D), lambda i, f: (0, 0)),
        pl.BlockSpec((D, tf), lambda i, f: (0, f)),
        pl.BlockSpec((tf, D), lambda i, f: (f, 0)),
    ]
    args = [h2, *outs, *ws, g.reshape(1, D), w_up, w_down]
    if final_g is not None:
        in_specs.append(pl.BlockSpec((1, D), lambda i, f: (0, 0)))
        args.append(final_g.reshape(1, D))
    kern = functools.partial(_layer_tail_kernel, n_in=n_in, final_norm=final_g is not None)
    return pl.pallas_call(
        kern,
        grid=(N // tm, F // tf),
        in_specs=in_specs,
        out_specs=pl.BlockSpec((tm, D), lambda i, f: (i, 0)),
        out_shape=jax.ShapeDtypeStruct((N, D), F32),
        scratch_shapes=[pltpu.VMEM((tm, D), BF16)],
        compiler_params=_params(("parallel", "arbitrary")),
        name="out_proj_mlp",
    )(*args)


def _even_layout():
    offs = {}
    o = 0
    for name, size in (("qa", 512), ("ka", 64), ("va", 64), ("qi", 256), ("ki", 64), ("wi", 4),
                       ("qb", 512), ("kvb", 768), ("gb", 24)):
        offs[name] = o
        o += size
    kvb = lambda which, g: offs["kvb"] + (which * B_KV_GROUPS + g) * HEAD_DIM
    cols, plan, lay = [], [], {}
    n = {k: 0 for k in _KINDS}

    def add(c, roped, kind, opt=None):
        cols.append(c)
        plan.append((roped, kind, n[kind], opt))
        n[kind] += 1
        return n[kind] - 1

    lay["qA"] = n["qT"]
    for p in range(4):
        add(_pair_cols(offs["qa"] + 2 * p * 64, offs["qa"] + (2 * p + 1) * 64), True, "qT", Q_SCALE)
    lay["qB"] = n["qT"]
    for p in range(4):
        add(_pair_cols(offs["qb"] + 2 * p * 64, offs["qb"] + (2 * p + 1) * 64), True, "qT", Q_SCALE)
    lay["qi"] = n["qT"]
    for p in range(2):
        add(_pair_cols(offs["qi"] + 2 * p * 64, offs["qi"] + (2 * p + 1) * 64), True, "qT")
    lay["kA"] = add(_pair_cols(offs["ka"], offs["ka"]), True, "k")
    lay["ki"] = add(_pair_cols(offs["ki"], offs["ki"]), True, "k")
    lay["ks"] = n["k"]
    for g in range(2):
        add(_pair_cols(kvb(2, g), kvb(2, g)), True, "k")
    lay["kw"] = n["k"]
    for g in range(2):
        add(_pair_cols(kvb(4, g), kvb(4, g)), True, "k")
    lay["kc"] = add(_pair_cols(kvb(0, 0), kvb(0, 1)), True, "f")
    lay["vc"] = add(_pair_cols(kvb(1, 0), kvb(1, 1)), False, "f")
    misc = np.full(LANES, -1)
    misc[0:IDX_HEADS] = offs["wi"] + np.arange(IDX_HEADS)
    lay["gate_row0"] = 8
    misc[8:8 + 24] = offs["gb"] + np.arange(24)
    lay["misc"] = add(misc, False, "fT")
    lay["vw"] = n["vTw"]
    for g in range(2):
        add(_head_cols(kvb(5, g)), False, "vTw")
    lay["vs"] = n["vT"]
    for g in range(2):
        add(_head_cols(kvb(3, g)), False, "vT", "ones_row_64")
    lay["vA"] = add(_head_cols(offs["va"]), False, "vT", "ones_row_64")
    return np.concatenate(cols), plan, lay


def _odd_layout():
    cols, plan = [], []
    for h in range(C_HEADS):
        cols.append(_pair_cols(h * 128, h * 128 + 64))
        plan.append((True, "qT", h, Q_SCALE))
    for h in range(C_HEADS):
        cols.append(_pair_cols(1024 + h * 128, 1024 + h * 128 + 64))
        plan.append((True, "k", h, None))
    for h in range(C_HEADS):
        cols.append(2048 + h * 128 + np.arange(LANES))
        plan.append((False, "vT", h, "ones_rows_below"))
    return np.concatenate(cols), plan


def _compress_weights(pe, w1, w2):
    d = _PAIR_D
    which = _PAIR_WHICH
    pe_l = pe[:, :, d]
    pe_l = pe_l.reshape(2, 2, 1, CMP_STRIDE * LANES)
    w1r = w1.astype(BF16).reshape(2, CMP_BLOCK, HEAD_DIM, CMP_HIDDEN)[:, :, d, :]
    per_g = []
    for g in range(B_KV_GROUPS):
        keep = jnp.asarray(which == g)[None, None, :, None]
        per_g.append(jnp.where(keep, w1r, jnp.zeros_like(w1r)))
    w1g = jnp.stack(per_g, axis=1)
    w1g = w1g.reshape(2, B_KV_GROUPS, 2, CMP_STRIDE * LANES, CMP_HIDDEN)
    w2b = w2.astype(BF16)
    w2k = w2b[0][:, d]
    w2v = w2b[1][:, np.arange(LANES) % HEAD_DIM]
    w2l = jnp.stack([w2k, w2v], axis=0)
    return pe_l, w1g, w2l


def _even_mixer(h, norm_g, w_in, cmp_pe, cmp_w1, cmp_w2, w_out, cos_slab, sin_slab):
    B, T, D = h.shape
    cols, plan, lay = _even_layout()
    w = _gather_cols(w_in.astype(BF16), cols)
    sl = _project(h, norm_g, w, cos_slab, sin_slab, plan)

    bias = _dsa_select(sl, lay)
    o_a = _dsa_attention(sl, bias, lay)

    n_ch = T // CMP_STRIDE
    assert (lay["kc"], lay["vc"]) == (0, 1) and sl["f"].shape[1] == 2
    flat = sl["f"].reshape(B, 2, n_ch, CMP_STRIDE * LANES)
    pe_l, w1g, w2l = _compress_weights(cmp_pe, cmp_w1, cmp_w2)
    cmp_k, cmp_vt = _compress(flat, pe_l, w1g, w2l)
    o_b = _nsa_attention(sl, cmp_k, cmp_vt, lay)

    na = A_HEADS * HEAD_DIM
    wo = w_out.astype(BF16)
    return [o_a.reshape(B * T, -1), o_b.reshape(B * T, -1)], [wo[:na], wo[na:]]


def _odd_mixer(h, norm_g, w_in, lam, subln_g, w_out, cos_slab, sin_slab, lambda_init):
    B, T, D = h.shape
    cols, plan = _odd_layout()
    w = _gather_cols(w_in.astype(BF16), cols)
    sl = _project(h, norm_g, w, cos_slab, sin_slab, plan)
    o = _diff_attention(sl, lam, subln_g, lambda_init)
    return [o.reshape(B * T, -1)], [w_out.astype(BF16)]


def kernel(x, mix_norm_g, mlp_norm_g, even_w_in, even_cmp_pe, even_cmp_w1, even_cmp_w2, even_w_out, odd_w_in, odd_lambda, odd_subln_g, odd_w_out, mlp_w_up, mlp_w_down, final_norm_g):
    B, T, D = x.shape
    depth = mix_norm_g.shape[0]
    assert depth >= 1
    cos_slab, sin_slab = _rope_slabs(T)
    h = x
    for layer in range(depth):
        if layer % 2 == 0:
            e = layer // 2
            outs, ws = _even_mixer(h, mix_norm_g[layer], even_w_in[e], even_cmp_pe[e],
                                   even_cmp_w1[e], even_cmp_w2[e], even_w_out[e], cos_slab, sin_slab)
        else:
            o = layer // 2
            lambda_init = 0.8 - 0.6 * math.exp(-0.3 * layer)
            outs, ws = _odd_mixer(h, mix_norm_g[layer], odd_w_in[o], odd_lambda[o], odd_subln_g[o],
                                  odd_w_out[o], cos_slab, sin_slab, lambda_init)
        h2 = _layer_tail(h.reshape(B * T, D), outs, ws, mlp_norm_g[layer],
                         mlp_w_up[layer].astype(BF16), mlp_w_down[layer].astype(BF16),
                         final_g=final_norm_g if layer == depth - 1 else None)
        h = h2.reshape(B, T, D)
    return h
```
